```python
import math
import jax, jax.numpy as jnp
from jax import lax
import numpy as np

D_MODEL = 2048
BATCH = 8
SEQ = 2048
DEPTH = 1

HEAD_DIM = 128
MEM_LEN = 256
ATT_GROUPS = ((128, 1), (512, 4), (2048, 16))
ATT_HEADS_PER_GROUP = 4
N_ATT_GROUPS = len(ATT_GROUPS)
ATT_WIDTH = N_ATT_GROUPS * ATT_HEADS_PER_GROUP * HEAD_DIM
ATT_OUT_WIDTH = ATT_HEADS_PER_GROUP * HEAD_DIM
ATT_BLOCK = 128
HG_HEADS = 8
HG_KEY = 128
HG_VAL = 128
HG_WIDTH = HG_HEADS * HG_KEY
HG_CHUNK = 64
N_BRANCH = 2
IN_WIDTH = 3 * ATT_WIDTH + 4 * HG_WIDTH + N_BRANCH * D_MODEL
CROSS_HEADS = 4
CROSS_WIDTH = CROSS_HEADS * HEAD_DIM
D_FF = int(math.ceil(8 * D_MODEL / 3 / 256) * 256)
RMS_EPS = 1e-6

kernel_name = "hybrid_dilated_attn_hgrn2_gated_block"


def rms_norm(x, w):
    xf = x.astype(jnp.float32)
    y = xf * lax.rsqrt(jnp.mean(xf * xf, axis=-1, keepdims=True) + RMS_EPS)
    return (y * w.astype(jnp.float32)).astype(x.dtype)


def banded_window_attention(q, k, v, window):
    assert window <= ATT_BLOCK
    lead = q.shape[:-2]
    L, dh = q.shape[-2], q.shape[-1]
    nb = -(-L // ATT_BLOCK)
    Lp = nb * ATT_BLOCK
    if Lp != L:
        pad = [(0, 0)] * len(lead) + [(0, Lp - L), (0, 0)]
        q, k, v = jnp.pad(q, pad), jnp.pad(k, pad), jnp.pad(v, pad)
    qb = q.reshape(*lead, nb, ATT_BLOCK, dh)
    kb = k.reshape(*lead, nb, ATT_BLOCK, dh)
    vb = v.reshape(*lead, nb, ATT_BLOCK, dh)
    kk = jnp.concatenate([jnp.concatenate([jnp.zeros_like(kb[..., :1, :, :]), kb[..., :-1, :, :]], axis=-3), kb], axis=-2)
    vv = jnp.concatenate([jnp.concatenate([jnp.zeros_like(vb[..., :1, :, :]), vb[..., :-1, :, :]], axis=-3), vb], axis=-2)
    blk = jnp.arange(nb)[:, None, None] * ATT_BLOCK
    qpos = blk + jnp.arange(ATT_BLOCK)[None, :, None]
    kpos = blk - ATT_BLOCK + jnp.arange(2 * ATT_BLOCK)[None, None, :]
    dist = qpos - kpos
    mask = (dist >= 0) & (dist <= window) & (kpos >= 0)
    s = jnp.einsum('...nqd,...nkd->...nqk', qb, kk).astype(jnp.float32) * (dh ** -0.5)
    s = jnp.where(mask, s, -jnp.inf)
    m = jnp.max(s, axis=-1, keepdims=True)
    p = jnp.exp(s - m)
    l = jnp.sum(p, axis=-1, keepdims=True)
    o = jnp.einsum('...nqk,...nkd->...nqd', (p / l).astype(v.dtype), vv)
    lse = (m + jnp.log(l))[..., 0]
    o = o.reshape(*lead, Lp, dh)[..., :L, :]
    lse = lse.reshape(*lead, Lp)[..., :L]
    return o, lse


def dilated_window_attention(q, k, v, window, dilation):
    B, H, S, dh = q.shape
    L = S // dilation

    def to_residue(t):
        return t.reshape(B, H, L, dilation, dh).transpose(0, 1, 3, 2, 4)

    o, lse = banded_window_attention(to_residue(q), to_residue(k), to_residue(v), window // dilation)
    o = o.transpose(0, 1, 3, 2, 4).reshape(B, H, S, dh)
    lse = lse.transpose(0, 1, 3, 2).reshape(B, H, S)
    return o, lse


def hgrn2_chunked(q, log_f, k, v):
    B, H, S, K = q.shape
    V = v.shape[-1]
    C = HG_CHUNK
    N = S // C
    q, log_f, k = (t.reshape(B, H, N, C, K) for t in (q, log_f, k))
    v = v.reshape(B, H, N, C, V)
    b = jnp.cumsum(log_f, axis=-2)
    b_last = b[..., -1:, :]
    chunk_kv = jnp.einsum('bhnck,bhncv->bhnkv', k * jnp.exp(b_last - b), v)
    decay = jnp.exp(b_last[..., 0, :])

    def step(state, inp):
        dec, kv = inp
        return dec[..., None] * state + kv, state

    _, states = lax.scan(step, jnp.zeros((B, H, K, V), jnp.float32),
                         (jnp.moveaxis(decay, 2, 0), jnp.moveaxis(chunk_kv, 2, 0)))
    states = jnp.moveaxis(states, 0, 2)
    inter = jnp.einsum('bhnck,bhnkv->bhncv', q * jnp.exp(b), states)
    b_ref = b[..., C // 2:C // 2 + 1, :]
    a = jnp.einsum('bhnck,bhnsk->bhncs', q * jnp.exp(b - b_ref), k * jnp.exp(b_ref - b))
    causal = jnp.arange(C)[:, None] >= jnp.arange(C)[None, :]
    a = jnp.where(causal, a, 0.0)
    intra = jnp.einsum('bhncs,bhnsv->bhncv', a, v)
    return (inter + intra).reshape(B, H, S, V)


def _fwd_setup_inputs(seed: int = 0) -> dict:
    key = jax.random.key(seed)
    ks = jax.random.split(key, 20)
    nrm = lambda k, shape, fan_in: jax.random.normal(k, shape, jnp.float32) * (fan_in ** -0.5)
    gain = lambda k, shape: 1.0 + 0.02 * jax.random.normal(k, shape, jnp.float32)
    return {
        "x": jax.random.normal(ks[0], (BATCH, SEQ, D_MODEL), jnp.float32),
        "mem": jax.random.normal(ks[1], (BATCH, MEM_LEN, D_MODEL), jnp.float32),
        "ln_mix_w": gain(ks[2], (DEPTH, D_MODEL)),
        "w_in": nrm(ks[3], (DEPTH, D_MODEL, IN_WIDTH), D_MODEL),
        "hg_norm_w": gain(ks[4], (DEPTH, HG_VAL)),
        "hg_lower_bounds": 0.1 * jax.random.normal(ks[5], (DEPTH + 1, HG_WIDTH), jnp.float32),
        "w_branch_a": nrm(ks[6], (DEPTH, ATT_OUT_WIDTH, D_MODEL), ATT_OUT_WIDTH),
        "w_branch_b": nrm(ks[7], (DEPTH, HG_WIDTH, D_MODEL), HG_WIDTH),
        "w_out": nrm(ks[8], (DEPTH, D_MODEL, D_MODEL), D_MODEL),
        "ln_cross_w": gain(ks[9], (DEPTH, D_MODEL)),
        "ln_mem_w": gain(ks[10], (DEPTH, D_MODEL)),
        "wq_cross": nrm(ks[11], (DEPTH, D_MODEL, CROSS_WIDTH), D_MODEL),
        "wkv_cross": nrm(ks[12], (DEPTH, D_MODEL, 2 * CROSS_WIDTH), D_MODEL),
        "wo_cross": nrm(ks[13], (DEPTH, CROSS_WIDTH, D_MODEL), CROSS_WIDTH),
        "ln_ffn_w": gain(ks[14], (DEPTH, D_MODEL)),
        "w1": nrm(ks[15], (DEPTH, D_MODEL, D_FF), D_MODEL),
        "w3": nrm(ks[16], (DEPTH, D_MODEL, D_FF), D_MODEL),
        "w2": nrm(ks[17], (DEPTH, D_FF, D_MODEL), D_FF),
        "ln_final_w": gain(ks[18], (D_MODEL,)),
    }


def _fwd_reference(x, mem, ln_mix_w, w_in, hg_norm_w, hg_lower_bounds, w_branch_a, w_branch_b, w_out,
              ln_cross_w, ln_mem_w, wq_cross, wkv_cross, wo_cross, ln_ffn_w, w1, w3, w2, ln_final_w):
    B, S, D = x.shape
    M = mem.shape[1]
    lower_bounds = jnp.cumsum(jax.nn.softmax(hg_lower_bounds.astype(jnp.float32), axis=0), axis=0)
    for l in range(DEPTH):
        h = rms_norm(x, ln_mix_w[l])
        proj = h @ w_in[l]
        q_a, k_a, v_a, q_h, f_h, i_h, g_h, gates = jnp.split(
            proj, np.cumsum([ATT_WIDTH] * 3 + [HG_WIDTH] * 4).tolist(), axis=-1)

        def att_heads(t):
            return t.reshape(B, S, N_ATT_GROUPS, ATT_HEADS_PER_GROUP, HEAD_DIM).transpose(0, 2, 3, 1, 4)
        qa, ka, va = att_heads(q_a), att_heads(k_a), att_heads(v_a)
        outs, lses = [], []
        for g, (window, dilation) in enumerate(ATT_GROUPS):
            o_g, lse_g = dilated_window_attention(qa[:, g], ka[:, g], va[:, g], window, dilation)
            outs.append(o_g)
            lses.append(lse_g)
        outs = jnp.stack(outs, axis=1)
        alpha = jax.nn.softmax(jnp.stack(lses, axis=1), axis=1)
        o_att = jnp.sum(alpha[..., None].astype(outs.dtype) * outs, axis=1)
        o_att = o_att.transpose(0, 2, 1, 3).reshape(B, S, ATT_OUT_WIDTH)

        def hg_heads(t):
            return t.reshape(B, S, HG_HEADS, HG_KEY).transpose(0, 2, 1, 3).astype(jnp.float32)
        lb = lower_bounds[l].reshape(HG_HEADS, HG_KEY)[None, :, None, :]
        f = lb + (1.0 - lb) * jax.nn.sigmoid(hg_heads(f_h))
        o_hg = hgrn2_chunked(jax.nn.silu(hg_heads(q_h)), jnp.log(f), 1.0 - f, hg_heads(i_h))
        o_hg = o_hg * lax.rsqrt(jnp.mean(o_hg * o_hg, axis=-1, keepdims=True) + RMS_EPS) * hg_norm_w[l].astype(jnp.float32)
        o_hg = (o_hg * jax.nn.silu(hg_heads(g_h))).transpose(0, 2, 1, 3).reshape(B, S, HG_WIDTH).astype(x.dtype)

        gate_a, gate_b = jnp.split(jax.nn.sigmoid(gates), N_BRANCH, axis=-1)
        merged = gate_a * (o_att @ w_branch_a[l]) + gate_b * (o_hg @ w_branch_b[l])
        x = x + merged @ w_out[l]

        hc = rms_norm(x, ln_cross_w[l])
        mn = rms_norm(mem, ln_mem_w[l])
        qc = (hc @ wq_cross[l]).reshape(B, S, CROSS_HEADS, HEAD_DIM)
        kvc = (mn @ wkv_cross[l]).reshape(B, M, 2, CROSS_HEADS, HEAD_DIM)
        sc = jnp.einsum('bshd,bmhd->bhsm', qc, kvc[:, :, 0]).astype(jnp.float32) * (HEAD_DIM ** -0.5)
        pc = jax.nn.softmax(sc, axis=-1).astype(x.dtype)
        oc = jnp.einsum('bhsm,bmhd->bshd', pc, kvc[:, :, 1]).reshape(B, S, CROSS_WIDTH)
        x = x + oc @ wo_cross[l]

        hf = rms_norm(x, ln_ffn_w[l])
        x = x + (jax.nn.silu(hf @ w1[l]) * (hf @ w3[l])) @ w2[l]
    return rms_norm(x, ln_final_w)


import jax as _jax
import jax.numpy as _jnp

TWIN_FORMAT = 'train_step'
FWD_PARAMS = ['x', 'mem', 'ln_mix_w', 'w_in', 'hg_norm_w', 'hg_lower_bounds', 'w_branch_a', 'w_branch_b', 'w_out', 'ln_cross_w', 'ln_mem_w', 'wq_cross', 'wkv_cross', 'wo_cross', 'ln_ffn_w', 'w1', 'w3', 'w2', 'ln_final_w']
TWIN_WEIGHTS = ['ln_mix_w', 'w_in', 'hg_norm_w', 'hg_lower_bounds', 'w_branch_a', 'w_branch_b', 'w_out', 'ln_cross_w', 'ln_mem_w', 'wq_cross', 'wkv_cross', 'wo_cross', 'ln_ffn_w', 'w1', 'w3', 'w2', 'ln_final_w']
TWIN_DIFF_INPUT = 'x'
TWIN_INPUTS = ['x', 'mem', 'ln_mix_w', 'w_in', 'hg_norm_w', 'hg_lower_bounds', 'w_branch_a', 'w_branch_b', 'w_out', 'ln_cross_w', 'ln_mem_w', 'wq_cross', 'wkv_cross', 'wo_cross', 'ln_ffn_w', 'w1', 'w3', 'w2', 'ln_final_w', 'loss_target', 'm_ln_mix_w', 'm_w_in', 'm_hg_norm_w', 'm_hg_lower_bounds', 'm_w_branch_a', 'm_w_branch_b', 'm_w_out', 'm_ln_cross_w', 'm_ln_mem_w', 'm_wq_cross', 'm_wkv_cross', 'm_wo_cross', 'm_ln_ffn_w', 'm_w1', 'm_w3', 'm_w2', 'm_ln_final_w', 'v_ln_mix_w', 'v_w_in', 'v_hg_norm_w', 'v_hg_lower_bounds', 'v_w_branch_a', 'v_w_branch_b', 'v_w_out', 'v_ln_cross_w', 'v_ln_mem_w', 'v_wq_cross', 'v_wkv_cross', 'v_wo_cross', 'v_ln_ffn_w', 'v_w1', 'v_w3', 'v_w2', 'v_ln_final_w']
TWIN_OUTPUTS = ['loss', 'grad_x', 'grad_ln_mix_w', 'grad_w_in', 'grad_hg_norm_w', 'grad_hg_lower_bounds', 'grad_w_branch_a', 'grad_w_branch_b', 'grad_w_out', 'grad_ln_cross_w', 'grad_ln_mem_w', 'grad_wq_cross', 'grad_wkv_cross', 'grad_wo_cross', 'grad_ln_ffn_w', 'grad_w1', 'grad_w3', 'grad_w2', 'grad_ln_final_w', 'delta_ln_mix_w', 'delta_w_in', 'delta_hg_norm_w', 'delta_hg_lower_bounds', 'delta_w_branch_a', 'delta_w_branch_b', 'delta_w_out', 'delta_ln_cross_w', 'delta_ln_mem_w', 'delta_wq_cross', 'delta_wkv_cross', 'delta_wo_cross', 'delta_ln_ffn_w', 'delta_w1', 'delta_w3', 'delta_w2', 'delta_ln_final_w', 'new_m_ln_mix_w', 'new_m_w_in', 'new_m_hg_norm_w', 'new_m_hg_lower_bounds', 'new_m_w_branch_a', 'new_m_w_branch_b', 'new_m_w_out', 'new_m_ln_cross_w', 'new_m_ln_mem_w', 'new_m_wq_cross', 'new_m_wkv_cross', 'new_m_wo_cross', 'new_m_ln_ffn_w', 'new_m_w1', 'new_m_w3', 'new_m_w2', 'new_m_ln_final_w', 'new_v_ln_mix_w', 'new_v_w_in', 'new_v_hg_norm_w', 'new_v_hg_lower_bounds', 'new_v_w_branch_a', 'new_v_w_branch_b', 'new_v_w_out', 'new_v_ln_cross_w', 'new_v_ln_mem_w', 'new_v_wq_cross', 'new_v_wkv_cross', 'new_v_wo_cross', 'new_v_ln_ffn_w', 'new_v_w1', 'new_v_w3', 'new_v_w2', 'new_v_ln_final_w']
TWIN_LEAF_KINDS = {'loss': 'loss', 'grad_x': 'grad_x', 'grad_ln_mix_w': 'grad_w', 'grad_w_in': 'grad_w', 'grad_hg_norm_w': 'grad_w', 'grad_hg_lower_bounds': 'grad_w', 'grad_w_branch_a': 'grad_w', 'grad_w_branch_b': 'grad_w', 'grad_w_out': 'grad_w', 'grad_ln_cross_w': 'grad_w', 'grad_ln_mem_w': 'grad_w', 'grad_wq_cross': 'grad_w', 'grad_wkv_cross': 'grad_w', 'grad_wo_cross': 'grad_w', 'grad_ln_ffn_w': 'grad_w', 'grad_w1': 'grad_w', 'grad_w3': 'grad_w', 'grad_w2': 'grad_w', 'grad_ln_final_w': 'grad_w', 'delta_ln_mix_w': 'delta_w', 'delta_w_in': 'delta_w', 'delta_hg_norm_w': 'delta_w', 'delta_hg_lower_bounds': 'delta_w', 'delta_w_branch_a': 'delta_w', 'delta_w_branch_b': 'delta_w', 'delta_w_out': 'delta_w', 'delta_ln_cross_w': 'delta_w', 'delta_ln_mem_w': 'delta_w', 'delta_wq_cross': 'delta_w', 'delta_wkv_cross': 'delta_w', 'delta_wo_cross': 'delta_w', 'delta_ln_ffn_w': 'delta_w', 'delta_w1': 'delta_w', 'delta_w3': 'delta_w', 'delta_w2': 'delta_w', 'delta_ln_final_w': 'delta_w', 'new_m_ln_mix_w': 'new_m', 'new_m_w_in': 'new_m', 'new_m_hg_norm_w': 'new_m', 'new_m_hg_lower_bounds': 'new_m', 'new_m_w_branch_a': 'new_m', 'new_m_w_branch_b': 'new_m', 'new_m_w_out': 'new_m', 'new_m_ln_cross_w': 'new_m', 'new_m_ln_mem_w': 'new_m', 'new_m_wq_cross': 'new_m', 'new_m_wkv_cross': 'new_m', 'new_m_wo_cross': 'new_m', 'new_m_ln_ffn_w': 'new_m', 'new_m_w1': 'new_m', 'new_m_w3': 'new_m', 'new_m_w2': 'new_m', 'new_m_ln_final_w': 'new_m', 'new_v_ln_mix_w': 'new_v', 'new_v_w_in': 'new_v', 'new_v_hg_norm_w': 'new_v', 'new_v_hg_lower_bounds': 'new_v', 'new_v_w_branch_a': 'new_v', 'new_v_w_branch_b': 'new_v', 'new_v_w_out': 'new_v', 'new_v_ln_cross_w': 'new_v', 'new_v_ln_mem_w': 'new_v', 'new_v_wq_cross': 'new_v', 'new_v_wkv_cross': 'new_v', 'new_v_wo_cross': 'new_v', 'new_v_ln_ffn_w': 'new_v', 'new_v_w1': 'new_v', 'new_v_w3': 'new_v', 'new_v_w2': 'new_v', 'new_v_ln_final_w': 'new_v'}


def _forward(args):
    return _fwd_reference(*[args[k] for k in FWD_PARAMS])


def _output_shape():
    out = _jax.eval_shape(lambda: _forward(_fwd_setup_inputs(0)))
    return out.shape, out.dtype

N_MICROBATCH = 1
ADAM_LR = 0.001
ADAM_B1 = 0.9
ADAM_B2 = 0.999
ADAM_EPS = 1e-08
ADAM_WD = 0.01
ADAM_STEP = 10
PER_EXAMPLE_BATCH_AXIS = {'x': 0, 'mem': 0, 'loss_target': 0}
SHARED_INPUTS = []
_WEIGHT_DTYPES = {'ln_mix_w': _jnp.float32, 'w_in': _jnp.float32, 'hg_norm_w': _jnp.float32, 'hg_lower_bounds': _jnp.float32, 'w_branch_a': _jnp.float32, 'w_branch_b': _jnp.float32, 'w_out': _jnp.float32, 'ln_cross_w': _jnp.float32, 'ln_mem_w': _jnp.float32, 'wq_cross': _jnp.float32, 'wkv_cross': _jnp.float32, 'wo_cross': _jnp.float32, 'ln_ffn_w': _jnp.float32, 'w1': _jnp.float32, 'w3': _jnp.float32, 'w2': _jnp.float32, 'ln_final_w': _jnp.float32}
MOMENT_SCALE = {'ln_mix_w': 3.355408e-02, 'w_in': 1.362111e-02, 'hg_norm_w': 9.366059e-02, 'hg_lower_bounds': 2.833352e-03, 'w_branch_a': 6.629352e-03, 'w_branch_b': 2.157073e-02, 'w_out': 2.244979e-02, 'ln_cross_w': 6.836962e-03, 'ln_mem_w': 9.957538e-03, 'wq_cross': 1.368496e-02, 'wkv_cross': 1.392973e-02, 'wo_cross': 6.897854e-03, 'ln_ffn_w': 4.368308e-02, 'w1': 1.915532e-02, 'w3': 1.856738e-02, 'w2': 3.079145e-02, 'ln_final_w': 8.003426e+00}


def _to_microbatches(a, axis):
    t = _jnp.moveaxis(a, axis, 0)
    t = t.reshape((N_MICROBATCH, t.shape[0] // N_MICROBATCH) + t.shape[1:])
    return _jnp.moveaxis(t, 1, axis + 1)


def setup_inputs(seed: int = 0) -> dict:
    inp = _fwd_setup_inputs(seed)
    key = _jax.random.fold_in(_jax.random.key(seed), 7919)
    shape, _ = _output_shape()
    out = dict(inp)
    out["loss_target"] = _jax.random.normal(_jax.random.fold_in(key, 0), shape, _jnp.float32)
    for i, name in enumerate(TWIN_WEIGHTS):
        w = inp[name].astype(_jnp.float32)
        if MOMENT_SCALE is None:
            s = _jnp.sqrt(_jnp.mean(_jnp.square(w)) + 1e-30)
        else:
            s = MOMENT_SCALE[name]
        km, kv = _jax.random.split(_jax.random.fold_in(key, i + 1))
        out[name] = w
        out["m_" + name] = s * _jax.random.normal(km, w.shape, _jnp.float32)
        out["v_" + name] = (s * s) * _jax.random.uniform(kv, w.shape, _jnp.float32, 0.5, 1.5)
    if N_MICROBATCH > 1:
        for name, axis in PER_EXAMPLE_BATCH_AXIS.items():
            out[name] = _to_microbatches(out[name], axis)
    return {'x': out['x'], 'mem': out['mem'], 'ln_mix_w': out['ln_mix_w'], 'w_in': out['w_in'], 'hg_norm_w': out['hg_norm_w'], 'hg_lower_bounds': out['hg_lower_bounds'], 'w_branch_a': out['w_branch_a'], 'w_branch_b': out['w_branch_b'], 'w_out': out['w_out'], 'ln_cross_w': out['ln_cross_w'], 'ln_mem_w': out['ln_mem_w'], 'wq_cross': out['wq_cross'], 'wkv_cross': out['wkv_cross'], 'wo_cross': out['wo_cross'], 'ln_ffn_w': out['ln_ffn_w'], 'w1': out['w1'], 'w3': out['w3'], 'w2': out['w2'], 'ln_final_w': out['ln_final_w'], 'loss_target': out['loss_target'], 'm_ln_mix_w': out['m_ln_mix_w'], 'm_w_in': out['m_w_in'], 'm_hg_norm_w': out['m_hg_norm_w'], 'm_hg_lower_bounds': out['m_hg_lower_bounds'], 'm_w_branch_a': out['m_w_branch_a'], 'm_w_branch_b': out['m_w_branch_b'], 'm_w_out': out['m_w_out'], 'm_ln_cross_w': out['m_ln_cross_w'], 'm_ln_mem_w': out['m_ln_mem_w'], 'm_wq_cross': out['m_wq_cross'], 'm_wkv_cross': out['m_wkv_cross'], 'm_wo_cross': out['m_wo_cross'], 'm_ln_ffn_w': out['m_ln_ffn_w'], 'm_w1': out['m_w1'], 'm_w3': out['m_w3'], 'm_w2': out['m_w2'], 'm_ln_final_w': out['m_ln_final_w'], 'v_ln_mix_w': out['v_ln_mix_w'], 'v_w_in': out['v_w_in'], 'v_hg_norm_w': out['v_hg_norm_w'], 'v_hg_lower_bounds': out['v_hg_lower_bounds'], 'v_w_branch_a': out['v_w_branch_a'], 'v_w_branch_b': out['v_w_branch_b'], 'v_w_out': out['v_w_out'], 'v_ln_cross_w': out['v_ln_cross_w'], 'v_ln_mem_w': out['v_ln_mem_w'], 'v_wq_cross': out['v_wq_cross'], 'v_wkv_cross': out['v_wkv_cross'], 'v_wo_cross': out['v_wo_cross'], 'v_ln_ffn_w': out['v_ln_ffn_w'], 'v_w1': out['v_w1'], 'v_w3': out['v_w3'], 'v_w2': out['v_w2'], 'v_ln_final_w': out['v_ln_final_w']}


def _loss(weights, diff, rest, loss_target):
    with _jax.named_scope("forward"):
        args = {**rest, TWIN_DIFF_INPUT: diff, **{k: w.astype(_WEIGHT_DTYPES[k]) for k, w in weights.items()}}
        y = _forward(args)
    with _jax.named_scope("loss_head"):
        err = _jnp.square(y.astype(_jnp.float32) - loss_target)
        return 0.5 * _jnp.sum(_jnp.mean(err, axis=-1)) if err.ndim else 0.5 * err


def _adamw(w, g, m, v):
    m = ADAM_B1 * m + (1.0 - ADAM_B1) * g
    v = ADAM_B2 * v + (1.0 - ADAM_B2) * _jnp.square(g)
    m_hat = m / (1.0 - ADAM_B1 ** ADAM_STEP)
    v_hat = v / (1.0 - ADAM_B2 ** ADAM_STEP)
    delta = -ADAM_LR * (m_hat / (_jnp.sqrt(v_hat) + ADAM_EPS) + ADAM_WD * w)
    return delta, m, v


def reference(x, mem, ln_mix_w, w_in, hg_norm_w, hg_lower_bounds, w_branch_a, w_branch_b, w_out, ln_cross_w, ln_mem_w, wq_cross, wkv_cross, wo_cross, ln_ffn_w, w1, w3, w2, ln_final_w, loss_target, m_ln_mix_w, m_w_in, m_hg_norm_w, m_hg_lower_bounds, m_w_branch_a, m_w_branch_b, m_w_out, m_ln_cross_w, m_ln_mem_w, m_wq_cross, m_wkv_cross, m_wo_cross, m_ln_ffn_w, m_w1, m_w3, m_w2, m_ln_final_w, v_ln_mix_w, v_w_in, v_hg_norm_w, v_hg_lower_bounds, v_w_branch_a, v_w_branch_b, v_w_out, v_ln_cross_w, v_ln_mem_w, v_wq_cross, v_wkv_cross, v_wo_cross, v_ln_ffn_w, v_w1, v_w3, v_w2, v_ln_final_w):
    given = dict(x=x, mem=mem, ln_mix_w=ln_mix_w, w_in=w_in, hg_norm_w=hg_norm_w, hg_lower_bounds=hg_lower_bounds, w_branch_a=w_branch_a, w_branch_b=w_branch_b, w_out=w_out, ln_cross_w=ln_cross_w, ln_mem_w=ln_mem_w, wq_cross=wq_cross, wkv_cross=wkv_cross, wo_cross=wo_cross, ln_ffn_w=ln_ffn_w, w1=w1, w3=w3, w2=w2, ln_final_w=ln_final_w, loss_target=loss_target, m_ln_mix_w=m_ln_mix_w, m_w_in=m_w_in, m_hg_norm_w=m_hg_norm_w, m_hg_lower_bounds=m_hg_lower_bounds, m_w_branch_a=m_w_branch_a, m_w_branch_b=m_w_branch_b, m_w_out=m_w_out, m_ln_cross_w=m_ln_cross_w, m_ln_mem_w=m_ln_mem_w, m_wq_cross=m_wq_cross, m_wkv_cross=m_wkv_cross, m_wo_cross=m_wo_cross, m_ln_ffn_w=m_ln_ffn_w, m_w1=m_w1, m_w3=m_w3, m_w2=m_w2, m_ln_final_w=m_ln_final_w, v_ln_mix_w=v_ln_mix_w, v_w_in=v_w_in, v_hg_norm_w=v_hg_norm_w, v_hg_lower_bounds=v_hg_lower_bounds, v_w_branch_a=v_w_branch_a, v_w_branch_b=v_w_branch_b, v_w_out=v_w_out, v_ln_cross_w=v_ln_cross_w, v_ln_mem_w=v_ln_mem_w, v_wq_cross=v_wq_cross, v_wkv_cross=v_wkv_cross, v_wo_cross=v_wo_cross, v_ln_ffn_w=v_ln_ffn_w, v_w1=v_w1, v_w3=v_w3, v_w2=v_w2, v_ln_final_w=v_ln_final_w)
    weights = {n: given[n] for n in TWIN_WEIGHTS}
    shared = {n: given[n] for n in SHARED_INPUTS}
    per_example = {n: given[n] for n in ['x', 'mem']}
    grad_fn = _jax.value_and_grad(_loss, argnums=(0, 1))

    def one_microbatch(ex, loss_target):
        ex = dict(ex)
        diff = ex.pop(TWIN_DIFF_INPUT)
        return grad_fn(weights, diff, {**shared, **ex}, loss_target)

    if N_MICROBATCH == 1:
        loss, (grad_w, grad_x) = one_microbatch(per_example, given["loss_target"])
    else:
        def body(carry, xs):
            loss_sum, grad_sum = carry
            l_k, (gw_k, gx_k) = one_microbatch(xs[0], xs[1])
            with _jax.named_scope("update"):
                return (loss_sum + l_k, _jax.tree.map(_jnp.add, grad_sum, gw_k)), gx_k

        init = (_jnp.zeros((), _jnp.float32), _jax.tree.map(_jnp.zeros_like, weights))
        (loss, grad_w), grad_x = _jax.lax.scan(body, init, (per_example, given["loss_target"]))
    with _jax.named_scope("update"):
        delta_w, new_m, new_v = {}, {}, {}
        for n in TWIN_WEIGHTS:
            delta_w[n], new_m[n], new_v[n] = _adamw(weights[n], grad_w[n], given["m_" + n], given["v_" + n])
    return (loss, grad_x, *[grad_w[n] for n in TWIN_WEIGHTS], *[delta_w[n] for n in TWIN_WEIGHTS],
            *[new_m[n] for n in TWIN_WEIGHTS], *[new_v[n] for n in TWIN_WEIGHTS])
```

```python
import functools
import math

import jax
import jax.numpy as jnp
from jax import lax
from jax.experimental import pallas as pl
from jax.experimental.pallas import tpu as pltpu

F32 = jnp.float32
BF = jnp.bfloat16
SDS = jax.ShapeDtypeStruct

D_MODEL = 2048
SEQ = 2048
HEAD_DIM = 128
MEM_LEN = 256
ATT_GROUPS = ((128, 1), (512, 4), (2048, 16))
ATT_HEADS = 4
ATT_WIDTH = 1536
ATT_OUT = 512
HG_HEADS = 8
HG_WIDTH = 1024
HG_CHUNK = 64
IN_WIDTH = 12800
CROSS_HEADS = 4
CROSS_WIDTH = 512
D_FF = 5632
RMS_EPS = 1e-6
ADAM_LR = 0.001
ADAM_B1 = 0.9
ADAM_B2 = 0.999
ADAM_EPS = 1e-08
ADAM_WD = 0.01
ADAM_STEP = 10
N_DEV = 8

LANE = 128
QA0, KA0, VA0 = 0, 12, 24
QH0, FH0, IH0, GH0 = 36, 44, 52, 60
GA0, GB0 = 68, 84

VMEM_BIG = 56 * 1024 * 1024
MESH = pl.DeviceIdType.MESH


def _cp(vmem=None, **kw):
    if vmem is not None:
        kw["vmem_limit_bytes"] = vmem
    return pltpu.CompilerParams(**kw)


def _pick(n, cands):
    for c in cands:
        if n % c == 0:
            return c
    raise ValueError(f"no tile for {n}")


def _mm(a, b, form, out_dtype, name, res=None, tiles=None):
    if form == "nn":
        (M, K), (K2, N) = a.shape, b.shape
    elif form == "nt":
        (M, K), (N, K2) = a.shape, b.shape
    else:
        (K, M), (K2, N) = a.shape, b.shape
    assert K == K2, (a.shape, b.shape, form)
    if tiles is None:
        tm = _pick(M, (1024, 512, 256))
        tn = _pick(N, (1280, 1024, 1408, 512, 256))
        tk = _pick(K, (512, 640, 256))
    else:
        tm, tn, tk = tiles
    nk = K // tk
    if form == "nn":
        a_spec = pl.BlockSpec((tm, tk), lambda i, j, k: (i, k))
        b_spec = pl.BlockSpec((tk, tn), lambda i, j, k: (k, j))
        dims = ((1,), (0,))
    elif form == "nt":
        a_spec = pl.BlockSpec((tm, tk), lambda i, j, k: (i, k))
        b_spec = pl.BlockSpec((tn, tk), lambda i, j, k: (j, k))
        dims = ((1,), (1,))
    else:
        a_spec = pl.BlockSpec((tk, tm), lambda i, j, k: (k, i))
        b_spec = pl.BlockSpec((tk, tn), lambda i, j, k: (k, j))
        dims = ((0,), (0,))
    o_spec = pl.BlockSpec((tm, tn), lambda i, j, k: (i, j))
    in_specs = [a_spec, b_spec]
    args = [a, b]
    if res is not None:
        in_specs.append(o_spec)
        args.append(res)

    def body(*refs):
        if res is not None:
            a_ref, b_ref, r_ref, o_ref, acc = refs
        else:
            a_ref, b_ref, o_ref, acc = refs
        k = pl.program_id(2)

        @pl.when(k == 0)
        def _():
            acc[...] = jnp.zeros_like(acc)

        acc[...] += lax.dot_general(a_ref[...].astype(BF), b_ref[...].astype(BF), (dims, ((), ())),
                                    preferred_element_type=F32)

        @pl.when(k == nk - 1)
        def _():
            r = acc[...]
            if res is not None:
                r = r + r_ref[...].astype(F32)
            o_ref[...] = r.astype(o_ref.dtype)

    return pl.pallas_call(
        body, grid=(M // tm, N // tn, nk), in_specs=in_specs, out_specs=o_spec,
        out_shape=SDS((M, N), out_dtype), scratch_shapes=[pltpu.VMEM((tm, tn), F32)],
        compiler_params=_cp(VMEM_BIG, dimension_semantics=("parallel", "parallel", "arbitrary")),
        name=name)(*args)


def _rms(x, w):
    return x * lax.rsqrt(jnp.mean(x * x, axis=-1, keepdims=True) + RMS_EPS) * w


def _rms_fwd(x, w, name):
    R, D = x.shape
    tr = 256

    def body(x_ref, w_ref, o_ref):
        o_ref[...] = _rms(x_ref[...], w_ref[...]).astype(o_ref.dtype)

    return pl.pallas_call(
        body, grid=(R // tr,),
        in_specs=[pl.BlockSpec((tr, D), lambda i: (i, 0)), pl.BlockSpec((1, D), lambda i: (0, 0))],
        out_specs=pl.BlockSpec((tr, D), lambda i: (i, 0)), out_shape=SDS((R, D), BF), name=name)(x, w)


def _rms_bwd(x, w, dh, dres, name):
    R, D = x.shape
    tr = 256
    want_dx = dres is not None

    def body(*refs):
        if want_dx:
            x_ref, w_ref, dh_ref, dr_ref, dx_ref, dw_ref = refs
        else:
            x_ref, w_ref, dh_ref, dw_ref = refs

        @pl.when(pl.program_id(0) == 0)
        def _():
            dw_ref[...] = jnp.zeros_like(dw_ref)

        _, vjp = jax.vjp(_rms, x_ref[...], w_ref[...])
        dx, dw = vjp(dh_ref[...].astype(F32))
        dw_ref[...] += dw
        if want_dx:
            dx_ref[...] = dx + dr_ref[...]

    row = pl.BlockSpec((tr, D), lambda i: (i, 0))
    vec = pl.BlockSpec((1, D), lambda i: (0, 0))
    if want_dx:
        return pl.pallas_call(body, grid=(R // tr,), in_specs=[row, vec, row, row], out_specs=[row, vec],
                              out_shape=[SDS((R, D), F32), SDS((1, D), F32)], name=name)(x, w, dh, dres)
    return pl.pallas_call(body, grid=(R // tr,), in_specs=[row, vec, row], out_specs=vec,
                          out_shape=SDS((1, D), F32), name=name)(x, w, dh)


def _final_loss(x3, w, tgt):
    R, D = x3.shape
    tr = 256

    def loss_fn(xv, wv, tv):
        err = _rms(xv, wv) - tv
        return 0.5 * jnp.sum(jnp.mean(err * err, axis=-1))

    def body(x_ref, w_ref, t_ref, loss_ref, dx_ref, dw_ref):
        @pl.when(pl.program_id(0) == 0)
        def _():
            dw_ref[...] = jnp.zeros_like(dw_ref)
            loss_ref[...] = jnp.zeros_like(loss_ref)

        tv = t_ref[...]
        val, vjp = jax.vjp(lambda a, b: loss_fn(a, b, tv), x_ref[...], w_ref[...])
        dx, dw = vjp(jnp.ones((), F32))
        dx_ref[...] = dx
        dw_ref[...] += dw
        loss_ref[...] += jnp.full(loss_ref.shape, val, F32)

    row = pl.BlockSpec((tr, D), lambda i: (i, 0))
    vec = pl.BlockSpec((1, D), lambda i: (0, 0))
    one = pl.BlockSpec((1, LANE), lambda i: (0, 0))
    return pl.pallas_call(body, grid=(R // tr,), in_specs=[row, vec, row], out_specs=[one, row, vec],
                          out_shape=[SDS((1, LANE), F32), SDS((R, D), F32), SDS((1, D), F32)],
                          name="final_loss")(x3, w, tgt)


NEG = -1e30


def _att_blocks(S, d):
    L = S // d
    nb = L // LANE
    return nb, (2 if nb > 1 else 1)


def _att_rows(S, d, idx):
    nb, nkb = _att_blocks(S, d)
    r = idx // nb
    n = idx % nb
    kb = jnp.maximum(n - 1, 0)
    if d == 1:
        qrows = pl.ds(pl.multiple_of(n * LANE, LANE), LANE)
        krows = pl.ds(pl.multiple_of(kb * LANE, LANE), LANE * nkb)
    else:
        qrows = pl.ds(r + n * (LANE * d), LANE, stride=d)
        krows = pl.ds(r + kb * (LANE * d), LANE * nkb, stride=d)
    qpos = n * LANE + lax.broadcasted_iota(jnp.int32, (LANE, LANE * nkb), 0)
    kpos = kb * LANE + lax.broadcasted_iota(jnp.int32, (LANE, LANE * nkb), 1)
    return qrows, krows, qpos - kpos


def _slab(col):
    return pl.BlockSpec((SEQ, LANE), lambda h, col=col: (0, col + h))


def _attn_fwd(proj):
    S = proj.shape[0]
    scale = HEAD_DIM ** -0.5

    def body(q0, q1, q2, k0, k1, k2, v0, v1, v2, o_ref, lse_ref, og, lg):
        qs, ks, vs = (q0, q1, q2), (k0, k1, k2), (v0, v1, v2)
        for g, (window, d) in enumerate(ATT_GROUPS):
            def blk(idx, carry, g=g, window=window, d=d):
                qrows, krows, dist = _att_rows(S, d, idx)
                q = qs[g][qrows, :].astype(BF)
                k = ks[g][krows, :].astype(BF)
                v = vs[g][krows, :].astype(BF)
                s = lax.dot_general(q, k, (((1,), (1,)), ((), ())), preferred_element_type=F32) * scale
                s = jnp.where((dist >= 0) & (dist <= window // d), s, NEG)
                m = jnp.max(s, axis=-1, keepdims=True)
                p = jnp.exp(s - m)
                l = jnp.sum(p, axis=-1, keepdims=True)
                o = jnp.dot((p / l).astype(BF), v, preferred_element_type=F32)
                og[g, qrows, :] = o
                lg[g, qrows, :] = jnp.broadcast_to(m + jnp.log(l), (LANE, LANE))
                return carry
            lax.fori_loop(0, S // LANE, blk, 0)

        def merge(t, carry):
            rows = pl.ds(pl.multiple_of(t * 256, 256), 256)
            l0, l1, l2 = lg[0, rows, :], lg[1, rows, :], lg[2, rows, :]
            m = jnp.maximum(jnp.maximum(l0, l1), l2)
            w0, w1, w2 = jnp.exp(l0 - m), jnp.exp(l1 - m), jnp.exp(l2 - m)
            den = w0 + w1 + w2
            o_ref[rows, :] = (w0 * og[0, rows, :] + w1 * og[1, rows, :] + w2 * og[2, rows, :]) / den
            lse_ref[rows, :] = m + jnp.log(den)
            return carry
        lax.fori_loop(0, S // 256, merge, 0)

    ins = [_slab(QA0 + 4 * g) for g in range(3)] + [_slab(KA0 + 4 * g) for g in range(3)] + [_slab(VA0 + 4 * g) for g in range(3)]
    out = pl.BlockSpec((SEQ, LANE), lambda h: (0, h))
    return pl.pallas_call(
        body, grid=(ATT_HEADS,), in_specs=ins, out_specs=[out, out],
        out_shape=[SDS((S, ATT_OUT), F32), SDS((S, ATT_OUT), F32)],
        scratch_shapes=[pltpu.VMEM((3, S, LANE), F32), pltpu.VMEM((3, S, LANE), F32)],
        compiler_params=_cp(VMEM_BIG), name="attn_fwd")(*([proj] * 9))


def _attn_bwd(proj, o_att, lse, do_att):
    S = proj.shape[0]
    scale = HEAD_DIM ** -0.5

    def body(q0, q1, q2, k0, k1, k2, v0, v1, v2, o_ref, lse_ref, do_ref,
             dq0, dq1, dq2, dk0, dk1, dk2, dv0, dv1, dv2, dqa, dka, dva, delta):
        qs, ks, vs = (q0, q1, q2), (k0, k1, k2), (v0, v1, v2)
        dqs, dks, dvs = (dq0, dq1, dq2), (dk0, dk1, dk2), (dv0, dv1, dv2)
        delta[...] = jnp.broadcast_to(jnp.sum(do_ref[...] * o_ref[...], axis=-1, keepdims=True), delta.shape)
        for g, (window, d) in enumerate(ATT_GROUPS):
            dka[...] = jnp.zeros_like(dka)
            dva[...] = jnp.zeros_like(dva)

            def blk(idx, carry, g=g, window=window, d=d):
                qrows, krows, dist = _att_rows(S, d, idx)
                q = qs[g][qrows, :].astype(BF)
                k = ks[g][krows, :].astype(BF)
                v = vs[g][krows, :].astype(BF)
                do = do_ref[qrows, :]
                s = lax.dot_general(q, k, (((1,), (1,)), ((), ())), preferred_element_type=F32) * scale
                p = jnp.exp(s - lse_ref[qrows, :][:, 0:1])
                p = jnp.where((dist >= 0) & (dist <= window // d), p, 0.0)
                dob = do.astype(BF)
                dva[krows, :] += lax.dot_general(p.astype(BF), dob, (((0,), (0,)), ((), ())), preferred_element_type=F32)
                dp = lax.dot_general(dob, v, (((1,), (1,)), ((), ())), preferred_element_type=F32)
                ds = (p * (dp - delta[qrows, :][:, 0:1]) * scale).astype(BF)
                dqa[qrows, :] = jnp.dot(ds, k, preferred_element_type=F32)
                dka[krows, :] += lax.dot_general(ds, q, (((0,), (0,)), ((), ())), preferred_element_type=F32)
                return carry
            lax.fori_loop(0, S // LANE, blk, 0)
            dqs[g][...] = dqa[...].astype(dqs[g].dtype)
            dks[g][...] = dka[...].astype(dks[g].dtype)
            dvs[g][...] = dva[...].astype(dvs[g].dtype)

    cols = [QA0 + 4 * g for g in range(3)] + [KA0 + 4 * g for g in range(3)] + [VA0 + 4 * g for g in range(3)]
    ins = [_slab(c) for c in cols]
    one = pl.BlockSpec((SEQ, LANE), lambda h: (0, h))
    return pl.pallas_call(
        body, grid=(ATT_HEADS,), in_specs=ins + [one, one, one], out_specs=[one] * 9,
        out_shape=[SDS((S, ATT_OUT), BF)] * 9,
        scratch_shapes=[pltpu.VMEM((S, LANE), F32)] * 4,
        compiler_params=_cp(VMEM_BIG), name="attn_bwd")(*([proj] * 9), o_att, lse, do_att)


def _bdot(a, b, dims):
    return lax.dot_general(a.astype(BF), b.astype(BF), (dims, ((), ())), preferred_element_type=F32)


def _hg_chunk(qh, fh, ih, gh, lbraw, nw, st):
    C = HG_CHUNK
    mx = jnp.maximum(lbraw[0:1], lbraw[1:2])
    e0 = jnp.exp(lbraw[0:1] - mx)
    e1 = jnp.exp(lbraw[1:2] - mx)
    lb = e0 / (e0 + e1)
    f = lb + (1.0 - lb) * jax.nn.sigmoid(fh)
    logf = jnp.log(f)
    k = 1.0 - f
    q = qh * jax.nn.sigmoid(qh)
    causal = lax.broadcasted_iota(jnp.int32, (C, C), 0) >= lax.broadcasted_iota(jnp.int32, (C, C), 1)
    b = lax.dot_general(causal.astype(F32), logf, (((1,), (0,)), ((), ())), precision=lax.Precision.HIGHEST,
                        preferred_element_type=F32)
    b_last = b[C - 1:C]
    st_out = st * jnp.exp(b_last) + _bdot(ih, k * jnp.exp(b_last - b), ((0,), (0,)))
    inter = _bdot(q * jnp.exp(b), st, ((1,), (1,)))
    b_ref = b[C // 2:C // 2 + 1]
    a = _bdot(q * jnp.exp(b - b_ref), k * jnp.exp(b_ref - b), ((1,), (1,)))
    a = jnp.where(causal, a, 0.0)
    o = inter + _bdot(a, ih, ((1,), (0,)))
    o = o * lax.rsqrt(jnp.mean(o * o, axis=-1, keepdims=True) + RMS_EPS) * nw
    return o * (gh * jax.nn.sigmoid(gh)), st_out


def _hg_fwd(proj, lb, nw):
    S = proj.shape[0]
    C = HG_CHUNK
    NC = S // C

    def body(q_ref, f_ref, i_ref, g_ref, lb_ref, nw_ref, o_ref, st_ref, st_scr):
        @pl.when(pl.program_id(1) == 0)
        def _():
            st_scr[...] = jnp.zeros_like(st_scr)

        st = st_scr[...]
        st_ref[0, 0] = st
        o, st_out = _hg_chunk(q_ref[...], f_ref[...], i_ref[...], g_ref[...], lb_ref[...], nw_ref[...], st)
        o_ref[...] = o.astype(o_ref.dtype)
        st_scr[...] = st_out

    spec = lambda base: pl.BlockSpec((C, LANE), lambda h, n, base=base: (n, base + h))
    return pl.pallas_call(
        body, grid=(HG_HEADS, NC),
        in_specs=[spec(QH0), spec(FH0), spec(IH0), spec(GH0),
                  pl.BlockSpec((2, LANE), lambda h, n: (0, h)), pl.BlockSpec((1, LANE), lambda h, n: (0, 0))],
        out_specs=[pl.BlockSpec((C, LANE), lambda h, n: (n, h)), pl.BlockSpec((1, 1, LANE, LANE), lambda h, n: (h, n, 0, 0))],
        out_shape=[SDS((S, HG_WIDTH), BF), SDS((HG_HEADS, NC, LANE, LANE), F32)],
        scratch_shapes=[pltpu.VMEM((LANE, LANE), F32)], name="hg_fwd")(proj, proj, proj, proj, lb, nw)


def _hg_bwd(proj, lb, nw, states, do_hg):
    S = proj.shape[0]
    C = HG_CHUNK
    NC = S // C

    def body(q_ref, f_ref, i_ref, g_ref, lb_ref, nw_ref, st_ref, do_ref,
             dq_ref, df_ref, di_ref, dg_ref, dlb_ref, dnw_ref, dst_scr):
        n = pl.program_id(1)

        @pl.when(n == 0)
        def _():
            dst_scr[...] = jnp.zeros_like(dst_scr)
            dlb_ref[...] = jnp.zeros_like(dlb_ref)

        @pl.when((n == 0) & (pl.program_id(0) == 0))
        def _():
            dnw_ref[...] = jnp.zeros_like(dnw_ref)

        _, vjp = jax.vjp(_hg_chunk, q_ref[...], f_ref[...], i_ref[...], g_ref[...], lb_ref[...], nw_ref[...], st_ref[0, 0])
        dq, df, di, dg, dlb, dnw, dst = vjp((do_ref[...].astype(F32), dst_scr[...]))
        dq_ref[...] = dq.astype(dq_ref.dtype)
        df_ref[...] = df.astype(df_ref.dtype)
        di_ref[...] = di.astype(di_ref.dtype)
        dg_ref[...] = dg.astype(dg_ref.dtype)
        dlb_ref[...] += dlb
        dnw_ref[...] += dnw
        dst_scr[...] = dst

    spec = lambda base: pl.BlockSpec((C, LANE), lambda h, n, base=base: (NC - 1 - n, base + h))
    ospec = pl.BlockSpec((C, LANE), lambda h, n: (NC - 1 - n, h))
    lbs = pl.BlockSpec((2, LANE), lambda h, n: (0, h))
    nws = pl.BlockSpec((1, LANE), lambda h, n: (0, 0))
    outs = pl.pallas_call(
        body, grid=(HG_HEADS, NC),
        in_specs=[spec(QH0), spec(FH0), spec(IH0), spec(GH0), lbs, nws,
                  pl.BlockSpec((1, 1, LANE, LANE), lambda h, n: (h, NC - 1 - n, 0, 0)),
                  ospec],
        out_specs=[ospec] * 4 + [lbs, nws],
        out_shape=[SDS((S, HG_WIDTH), BF)] * 4 + [SDS((2, HG_WIDTH), F32), SDS((1, LANE), F32)],
        scratch_shapes=[pltpu.VMEM((LANE, LANE), F32)], name="hg_bwd")(proj, proj, proj, proj, lb, nw, states, do_hg)
    return outs


def _merge(ga, gb, ba, bb):
    return jax.nn.sigmoid(ga) * ba + jax.nn.sigmoid(gb) * bb


_MT = 512


def _gate_specs():
    ga = pl.BlockSpec((_MT, _MT), lambda i, j: (i, GA0 * LANE // _MT + j))
    gb = pl.BlockSpec((_MT, _MT), lambda i, j: (i, GB0 * LANE // _MT + j))
    t = pl.BlockSpec((_MT, _MT), lambda i, j: (i, j))
    return ga, gb, t


def _merge_fwd(proj, ba, bb):
    S, D = ba.shape
    ga, gb, t = _gate_specs()

    def body(ga_ref, gb_ref, ba_ref, bb_ref, o_ref):
        o_ref[...] = _merge(ga_ref[...], gb_ref[...], ba_ref[...], bb_ref[...]).astype(o_ref.dtype)

    return pl.pallas_call(body, grid=(S // _MT, D // _MT), in_specs=[ga, gb, t, t], out_specs=t,
                          out_shape=SDS((S, D), BF), name="merge_fwd")(proj, proj, ba, bb)


def _merge_bwd(proj, ba, bb, dm):
    S, D = ba.shape
    ga, gb, t = _gate_specs()

    def body(ga_ref, gb_ref, ba_ref, bb_ref, dm_ref, dga_ref, dgb_ref, dba_ref, dbb_ref):
        _, vjp = jax.vjp(_merge, ga_ref[...], gb_ref[...], ba_ref[...], bb_ref[...])
        dga, dgb, dba, dbb = vjp(dm_ref[...])
        dga_ref[...] = dga.astype(BF)
        dgb_ref[...] = dgb.astype(BF)
        dba_ref[...] = dba.astype(BF)
        dbb_ref[...] = dbb.astype(BF)

    return pl.pallas_call(body, grid=(S // _MT, D // _MT), in_specs=[ga, gb, t, t, t], out_specs=[t] * 4,
                          out_shape=[SDS((S, D), BF)] * 4, name="merge_bwd")(proj, proj, ba, bb, dm)


def _swiglu(a, b):
    return a * jax.nn.sigmoid(a) * b


def _swiglu_fwd(a, b):
    S, F = a.shape
    tf = _pick(F, (1408, 512))
    t = pl.BlockSpec((_MT, tf), lambda i, j: (i, j))

    def body(a_ref, b_ref, o_ref):
        o_ref[...] = _swiglu(a_ref[...], b_ref[...]).astype(o_ref.dtype)

    return pl.pallas_call(body, grid=(S // _MT, F // tf), in_specs=[t, t], out_specs=t,
                          out_shape=SDS((S, F), BF), name="swiglu_fwd")(a, b)


def _swiglu_bwd(a, b, dg):
    S, F = a.shape
    tf = _pick(F, (1408, 512))
    t = pl.BlockSpec((_MT, tf), lambda i, j: (i, j))

    def body(a_ref, b_ref, dg_ref, da_ref, db_ref):
        _, vjp = jax.vjp(_swiglu, a_ref[...], b_ref[...])
        da, db = vjp(dg_ref[...])
        da_ref[...] = da.astype(BF)
        db_ref[...] = db.astype(BF)

    return pl.pallas_call(body, grid=(S // _MT, F // tf), in_specs=[t, t, t], out_specs=[t, t],
                          out_shape=[SDS((S, F), BF)] * 2, name="swiglu_bwd")(a, b, dg)


def _cross(q, k, v):
    s = _bdot(q, k, ((1,), (1,))) * (HEAD_DIM ** -0.5)
    e = jnp.exp(s - jnp.max(s, axis=-1, keepdims=True))
    p = e / jnp.sum(e, axis=-1, keepdims=True)
    return _bdot(p, v, ((1,), (0,)))


def _cross_specs():
    q = pl.BlockSpec((_MT, LANE), lambda h, i: (i, h))
    k = pl.BlockSpec((MEM_LEN, LANE), lambda h, i: (0, h))
    v = pl.BlockSpec((MEM_LEN, LANE), lambda h, i: (0, CROSS_HEADS + h))
    return q, k, v


def _cross_fwd(qc, kvc):
    S = qc.shape[0]
    q, k, v = _cross_specs()

    def body(q_ref, k_ref, v_ref, o_ref):
        o_ref[...] = _cross(q_ref[...], k_ref[...], v_ref[...]).astype(o_ref.dtype)

    return pl.pallas_call(body, grid=(CROSS_HEADS, S // _MT), in_specs=[q, k, v], out_specs=q,
                          out_shape=SDS((S, CROSS_WIDTH), BF), name="cross_fwd")(qc, kvc, kvc)


def _cross_bwd(qc, kvc, doc):
    S = qc.shape[0]
    q, k, v = _cross_specs()

    def body(q_ref, k_ref, v_ref, do_ref, dq_ref, dk_ref, dv_ref):
        @pl.when(pl.program_id(1) == 0)
        def _():
            dk_ref[...] = jnp.zeros_like(dk_ref)
            dv_ref[...] = jnp.zeros_like(dv_ref)

        _, vjp = jax.vjp(_cross, q_ref[...], k_ref[...], v_ref[...])
        dq, dk, dv = vjp(do_ref[...])
        dq_ref[...] = dq.astype(dq_ref.dtype)
        dk_ref[...] += dk
        dv_ref[...] += dv

    return pl.pallas_call(body, grid=(CROSS_HEADS, S // _MT), in_specs=[q, k, v, q], out_specs=[q, k, k],
                          out_shape=[SDS((S, CROSS_WIDTH), BF), SDS((MEM_LEN, CROSS_WIDTH), F32), SDS((MEM_LEN, CROSS_WIDTH), F32)],
                          name="cross_bwd")(qc, kvc, kvc, doc)


def _local_step(x, mem, tgt, p):
    h = _rms_fwd(x, p["ln_mix"], "rms_mix")
    proj = _mm(h, p["w_in"], "nn", F32, "mm_proj")
    o_att, lse = _attn_fwd(proj)
    o_hg, states = _hg_fwd(proj, p["hg_lb"], p["hg_norm"])
    ba = _mm(o_att, p["w_ba"], "nn", F32, "mm_ba")
    bb = _mm(o_hg, p["w_bb"], "nn", F32, "mm_bb")
    merged = _merge_fwd(proj, ba, bb)
    x1 = _mm(merged, p["w_out"], "nn", F32, "mm_out", res=x)
    hc = _rms_fwd(x1, p["ln_cross"], "rms_cross")
    qc = _mm(hc, p["wq"], "nn", F32, "mm_q")
    mn = _rms_fwd(mem, p["ln_mem"], "rms_mem")
    kvc = _mm(mn, p["wkv"], "nn", F32, "mm_kv")
    oc = _cross_fwd(qc, kvc)
    x2 = _mm(oc, p["wo"], "nn", F32, "mm_o", res=x1)
    hf = _rms_fwd(x2, p["ln_ffn"], "rms_ffn")
    a = _mm(hf, p["w1"], "nn", F32, "mm_w1")
    b = _mm(hf, p["w3"], "nn", F32, "mm_w3")
    g = _swiglu_fwd(a, b)
    x3 = _mm(g, p["w2"], "nn", F32, "mm_w2", res=x2)
    loss, dx3, d_ln_final = _final_loss(x3, p["ln_final"], tgt)

    gw, gs = {}, {"ln_final": d_ln_final}
    dg = _mm(dx3, p["w2"], "nt", F32, "mm_dg")
    gw["w2"] = _mm(g, dx3, "tn", BF, "mm_dw2")
    da, db = _swiglu_bwd(a, b, dg)
    dhf = _mm(da, p["w1"], "nt", F32, "mm_dhf1")
    dhf = _mm(db, p["w3"], "nt", F32, "mm_dhf3", res=dhf)
    gw["w1"] = _mm(hf, da, "tn", BF, "mm_dw1")
    gw["w3"] = _mm(hf, db, "tn", BF, "mm_dw3")
    dx2, gs["ln_ffn"] = _rms_bwd(x2, p["ln_ffn"], dhf, dx3, "rms_ffn_bwd")
    doc = _mm(dx2, p["wo"], "nt", F32, "mm_doc")
    gw["wo"] = _mm(oc, dx2, "tn", BF, "mm_dwo")
    dqc, dk, dv = _cross_bwd(qc, kvc, doc)
    dkvc = jnp.concatenate([dk, dv], axis=1)
    dhc = _mm(dqc, p["wq"], "nt", F32, "mm_dhc")
    gw["wq"] = _mm(hc, dqc, "tn", BF, "mm_dwq")
    dmn = _mm(dkvc, p["wkv"], "nt", F32, "mm_dmn")
    gw["wkv"] = _mm(mn, dkvc, "tn", BF, "mm_dwkv")
    gs["ln_mem"] = _rms_bwd(mem, p["ln_mem"], dmn, None, "rms_mem_bwd")
    dx1, gs["ln_cross"] = _rms_bwd(x1, p["ln_cross"], dhc, dx2, "rms_cross_bwd")
    dmerged = _mm(dx1, p["w_out"], "nt", F32, "mm_dmerged")
    gw["w_out"] = _mm(merged, dx1, "tn", BF, "mm_dwout")
    dga, dgb, dba, dbb = _merge_bwd(proj, ba, bb, dmerged)
    do_att = _mm(dba, p["w_ba"], "nt", F32, "mm_doatt")
    gw["w_ba"] = _mm(o_att, dba, "tn", BF, "mm_dwba")
    do_hg = _mm(dbb, p["w_bb"], "nt", F32, "mm_dohg")
    gw["w_bb"] = _mm(o_hg, dbb, "tn", BF, "mm_dwbb")
    datt = _attn_bwd(proj, o_att, lse, do_att)
    dqh, dfh, dih, dgh, gs["hg_lb"], gs["hg_norm"] = _hg_bwd(proj, p["hg_lb"], p["hg_norm"], states, do_hg)
    dproj = jnp.concatenate([*datt, dqh, dfh, dih, dgh, dga, dgb], axis=1)
    dh = _mm(dproj, p["w_in"], "nt", F32, "mm_dh")
    gw["w_in"] = _mm(h, dproj, "tn", BF, "mm_dwin")
    grad_x, gs["ln_mix"] = _rms_bwd(x, p["ln_mix"], dh, dx1, "rms_mix_bwd")
    return loss, grad_x, gw, gs


_HALF = LANE // 2
_MATS = (
    ("w_in", "w_in", D_MODEL, IN_WIDTH, "colx", IN_WIDTH // N_DEV),
    ("w_ba", "w_branch_a", ATT_OUT, D_MODEL, "col", D_MODEL // N_DEV),
    ("w_bb", "w_branch_b", HG_WIDTH, D_MODEL, "col", D_MODEL // N_DEV),
    ("w_out", "w_out", D_MODEL, D_MODEL, "row", D_MODEL // N_DEV),
    ("wq", "wq_cross", D_MODEL, CROSS_WIDTH, "row", D_MODEL // N_DEV),
    ("wkv", "wkv_cross", D_MODEL, 2 * CROSS_WIDTH, "row", D_MODEL // N_DEV),
    ("wo", "wo_cross", CROSS_WIDTH, D_MODEL, "col", D_MODEL // N_DEV),
    ("w1", "w1", D_MODEL, D_FF, "colx", D_FF // N_DEV),
    ("w3", "w3", D_MODEL, D_FF, "colx", D_FF // N_DEV),
    ("w2", "w2", D_FF, D_MODEL, "row", D_FF // N_DEV),
)
_NM = len(_MATS)
_ANY = pl.BlockSpec(memory_space=pl.ANY)


def _unit_shape(K, N, kind, sh):
    if kind == "row":
        return (sh, N)
    if kind == "col":
        return (K, sh)
    return (K, sh + _HALF)


def _ag_win(ref, kind, sh, chip, par):
    if kind == "row":
        return ref.at[pl.ds(pl.multiple_of((2 * chip + par) * sh, 16), sh), :]
    if kind == "col":
        return ref.at[:, pl.ds(pl.multiple_of((2 * chip + par) * sh, LANE), sh)]
    if par == 0:
        return ref.at[:, pl.ds(pl.multiple_of(2 * sh * chip, LANE), sh + _HALF)]
    return ref.at[:, pl.ds(pl.multiple_of(2 * sh * chip + sh + _HALF, LANE), sh - _HALF)]


def _ag_src(ref, kind, sh, par):
    if kind == "colx" and par == 1:
        return ref.at[:, pl.ds(0, sh - _HALF)]
    return ref


def _rs_win(ref, kind, sh, chip, par):
    if kind == "row":
        return ref.at[pl.ds(pl.multiple_of((2 * chip + par) * sh, 16), sh), :]
    if kind == "col":
        return ref.at[:, pl.ds(pl.multiple_of((2 * chip + par) * sh, LANE), sh)]
    return ref.at[:, pl.ds(pl.multiple_of(2 * sh * chip + par * (sh - _HALF), LANE), sh + _HALF)]


def _my_place():
    x, y, c = lax.axis_index("x"), lax.axis_index("y"), lax.axis_index("c")
    chips = [(1 - x, y), (x, 1 - y), (1 - x, 1 - y)]
    return x, y, c, chips


def _straddle_exchange(shards):
    n = len(shards)

    def body(*refs):
        ins, outs, ssem, rsem, lsem = refs[:n], refs[n:2 * n], refs[2 * n], refs[2 * n + 1], refs[2 * n + 2]
        x, y, c, _ = _my_place()

        @pl.when(c == 1)
        def _():
            cps = []
            for i in range(n):
                head = ins[i].at[:, pl.ds(0, LANE)]
                loc = pltpu.make_async_copy(head, outs[i], lsem.at[i])
                rem = pltpu.make_async_remote_copy(src_ref=head, dst_ref=outs[i], send_sem=ssem.at[i], recv_sem=rsem.at[i],
                                                   device_id=(x, y, 0), device_id_type=MESH)
                loc.start()
                rem.start()
                cps.append((loc, rem))
            for loc, rem in cps:
                loc.wait()
                rem.wait_send()

        @pl.when(c == 0)
        def _():
            for i in range(n):
                head = ins[i].at[:, pl.ds(0, LANE)]
                pltpu.make_async_remote_copy(src_ref=head, dst_ref=outs[i], send_sem=ssem.at[i], recv_sem=rsem.at[i],
                                             device_id=(x, y, 1), device_id_type=MESH).wait_recv()

    return pl.pallas_call(
        body, in_specs=[_ANY] * n, out_specs=[_ANY] * n,
        out_shape=[SDS((s.shape[0], LANE), F32) for s in shards],
        scratch_shapes=[pltpu.SemaphoreType.DMA((n,)), pltpu.SemaphoreType.DMA((n,)), pltpu.SemaphoreType.DMA((n,))],
        name="straddle_exchange")(*shards)


def _cast_unit(w, name):
    K, N = w.shape
    tr = _pick(K, (256, 352))

    def body(w_ref, o_ref):
        o_ref[...] = w_ref[...].astype(BF)

    blk = pl.BlockSpec((tr, N), lambda i: (i, 0))
    return pl.pallas_call(body, grid=(K // tr,), in_specs=[blk], out_specs=blk, out_shape=SDS((K, N), BF), name=name)(w)


def _cast_unit_x(w, head, name):
    K, sh = w.shape
    tr = 256

    def body(w_ref, h_ref, o_ref):
        c = lax.axis_index("c")

        @pl.when(c == 0)
        def _():
            o_ref[:, 0:sh] = w_ref[...].astype(BF)
            o_ref[:, sh:sh + _HALF] = h_ref[:, 0:_HALF].astype(BF)

        @pl.when(c == 1)
        def _():
            o_ref[:, 0:sh - _HALF] = w_ref[:, _HALF:sh].astype(BF)
            o_ref[:, sh - _HALF:sh + _HALF] = jnp.zeros((tr, LANE), BF)

    return pl.pallas_call(
        body, grid=(K // tr,),
        in_specs=[pl.BlockSpec((tr, sh), lambda i: (i, 0)), pl.BlockSpec((tr, LANE), lambda i: (i, 0))],
        out_specs=pl.BlockSpec((tr, sh + _HALF), lambda i: (i, 0)), out_shape=SDS((K, sh + _HALF), BF), name=name)(w, head)


def _all_gather(units):
    def body(*refs):
        us, fulls = refs[:_NM], refs[_NM:2 * _NM]
        ssem, rsem, lsem = refs[2 * _NM:]
        x, y, c, chips = _my_place()

        def gather(par):
            def win(i, chip_xy, p):
                _, _, _, _, kind, sh = _MATS[i]
                return _ag_win(fulls[i], kind, sh, 2 * chip_xy[0] + chip_xy[1], p)

            def rcopy(i, k, src, dst, to):
                return pltpu.make_async_remote_copy(src_ref=src, dst_ref=dst, send_sem=ssem.at[i, k], recv_sem=rsem.at[i, k],
                                                    device_id=to, device_id_type=MESH)

            sends, locs = [], []
            for i in range(_NM):
                _, _, _, _, kind, sh = _MATS[i]
                src = _ag_src(us[i], kind, sh, par)
                mine = win(i, (x, y), par)
                loc = pltpu.make_async_copy(src, mine, lsem.at[i])
                loc.start()
                locs.append(loc)
                first = [rcopy(i, 0, src, mine, (x, y, 1 - c))]
                first += [rcopy(i, 1 + j, src, mine, (*chips[j], c)) for j in range(3)]
                for cp in first:
                    cp.start()
                sends += first
            for i in range(_NM):
                for j in range(3):
                    blk = win(i, chips[j], par)
                    rcopy(i, 1 + j, blk, blk, (x, y, c)).wait_recv()
                    fwd = rcopy(i, 4 + j, blk, blk, (x, y, 1 - c))
                    fwd.start()
                    sends.append(fwd)
            for i in range(_NM):
                blk = win(i, (x, y), 1 - par)
                rcopy(i, 0, blk, blk, (x, y, c)).wait_recv()
                for j in range(3):
                    blk = win(i, chips[j], 1 - par)
                    rcopy(i, 4 + j, blk, blk, (x, y, c)).wait_recv()
            for cp in sends:
                cp.wait_send()
            for loc in locs:
                loc.wait()

        for par in (0, 1):
            pl.when(c == par)(functools.partial(gather, par))

    return pl.pallas_call(
        body, in_specs=[_ANY] * _NM, out_specs=[_ANY] * _NM,
        out_shape=[SDS((K, N), BF) for _, _, K, N, _, _ in _MATS],
        scratch_shapes=[pltpu.SemaphoreType.DMA((_NM, 7)), pltpu.SemaphoreType.DMA((_NM, 7)), pltpu.SemaphoreType.DMA((_NM,))],
        name="all_gather")(*units)


def _rs_sibling(grads):
    def body(*refs):
        gs, bufs = refs[:_NM], refs[_NM:2 * _NM]
        ssem, rsem = refs[2 * _NM:]
        x, y, c, _ = _my_place()
        cps = []
        for i in range(_NM):
            _, _, _, _, kind, sh = _MATS[i]
            for k in range(4):
                cp = pltpu.make_async_remote_copy(
                    src_ref=_rs_win(gs[i], kind, sh, k, 1 - c), dst_ref=bufs[i].at[k], send_sem=ssem.at[i, k], recv_sem=rsem.at[i, k],
                    device_id=(x, y, 1 - c), device_id_type=MESH)
                cp.start()
                cps.append(cp)
        for cp in cps:
            cp.wait()

    return pl.pallas_call(
        body, in_specs=[_ANY] * _NM, out_specs=[_ANY] * _NM,
        out_shape=[SDS((4, *_unit_shape(K, N, kind, sh)), BF) for _, _, K, N, kind, sh in _MATS],
        scratch_shapes=[pltpu.SemaphoreType.DMA((_NM, 4)), pltpu.SemaphoreType.DMA((_NM, 4))],
        name="rs_sibling")(*grads)


def _chip_sum(g, sib, kind, sh, place, name):
    K, N = g.shape
    U = _unit_shape(K, N, kind, sh)

    def body(pref, g_ref, s_ref, o_ref):
        o_ref[...] = (g_ref[...].astype(F32) + s_ref[...].astype(F32)).astype(BF)

    if kind == "row":
        grid = (4,)
        g_spec = pl.BlockSpec((sh, N), lambda k, pref: (2 * k + pref[0], 0))
        u_spec = pl.BlockSpec((None, sh, N), lambda k, pref: (k, 0, 0))
    else:
        wid = U[1] // LANE
        per_chip = 2 * sh // LANE
        off = (sh - _HALF) // LANE if kind == "colx" else sh // LANE
        grid = (4, wid)
        g_spec = pl.BlockSpec((K, LANE), lambda k, j, pref: (0, per_chip * k + off * pref[0] + j))
        u_spec = pl.BlockSpec((None, K, LANE), lambda k, j, pref: (k, 0, j))
    return pl.pallas_call(
        body, grid_spec=pltpu.PrefetchScalarGridSpec(num_scalar_prefetch=1, grid=grid, in_specs=[g_spec, u_spec], out_specs=u_spec),
        out_shape=SDS((4, *U), BF), name=name)(place, g, sib)


def _rs_chips(parts):
    def body(*refs):
        ts, bufs = refs[:_NM], refs[_NM:2 * _NM]
        ssem, rsem = refs[2 * _NM:]
        x, y, c, chips = _my_place()
        cps = []
        for i in range(_NM):
            for j in range(3):
                cp = pltpu.make_async_remote_copy(
                    src_ref=ts[i].at[2 * chips[j][0] + chips[j][1]], dst_ref=bufs[i].at[j], send_sem=ssem.at[i, j], recv_sem=rsem.at[i, j],
                    device_id=(*chips[j], c), device_id_type=MESH)
                cp.start()
                cps.append(cp)
        for cp in cps:
            cp.wait()

    return pl.pallas_call(
        body, in_specs=[_ANY] * _NM, out_specs=[_ANY] * _NM,
        out_shape=[SDS((3, *_unit_shape(K, N, kind, sh)), BF) for _, _, K, N, kind, sh in _MATS],
        scratch_shapes=[pltpu.SemaphoreType.DMA((_NM, 3)), pltpu.SemaphoreType.DMA((_NM, 3))],
        name="rs_chips")(*parts)


def _adamw(w, g, m, v):
    m = ADAM_B1 * m + (1.0 - ADAM_B1) * g
    v = ADAM_B2 * v + (1.0 - ADAM_B2) * jnp.square(g)
    m_hat = m / (1.0 - ADAM_B1 ** ADAM_STEP)
    v_hat = v / (1.0 - ADAM_B2 ** ADAM_STEP)
    delta = -ADAM_LR * (m_hat / (jnp.sqrt(v_hat) + ADAM_EPS) + ADAM_WD * w)
    return delta, m, v


def _adam_mat(part, recv, w, m, v, kind, sh, place, name):
    R, C = w.shape
    U = part.shape[1:]
    tr = _pick(R, (256, 352))

    def body(pref, t_ref, r_ref, w_ref, m_ref, v_ref, g_out, d_out, m_out, v_out):
        gp = t_ref[...].astype(F32) + r_ref[0].astype(F32) + r_ref[1].astype(F32) + r_ref[2].astype(F32)

        def finish(g):
            d, mn, vn = _adamw(w_ref[...], g, m_ref[...], v_ref[...])
            g_out[...] = g
            d_out[...] = d
            m_out[...] = mn
            v_out[...] = vn

        if kind == "colx":
            c = pref[0]
            pl.when(c == 0)(lambda: finish(gp[:, 0:sh]))
            pl.when(c == 1)(lambda: finish(gp[:, _HALF:sh + _HALF]))
        else:
            finish(gp)

    t_spec = pl.BlockSpec((None, tr, U[1]), lambda i, pref: (pref[1], i, 0))
    r_spec = pl.BlockSpec((3, tr, U[1]), lambda i, pref: (0, i, 0))
    s_spec = pl.BlockSpec((tr, C), lambda i, pref: (i, 0))
    return pl.pallas_call(
        body, grid_spec=pltpu.PrefetchScalarGridSpec(num_scalar_prefetch=1, grid=(R // tr,), in_specs=[t_spec, r_spec, s_spec, s_spec, s_spec],
                                                     out_specs=[s_spec] * 4),
        out_shape=[SDS((R, C), F32)] * 4, name=name)(place, part, recv, w, m, v)


_SMALL = (("ln_mix", "ln_mix_w", D_MODEL), ("ln_cross", "ln_cross_w", D_MODEL), ("ln_mem", "ln_mem_w", D_MODEL),
          ("ln_ffn", "ln_ffn_w", D_MODEL), ("ln_final", "ln_final_w", D_MODEL), ("hg_lb", "hg_lower_bounds", 2 * HG_WIDTH),
          ("hg_norm", "hg_norm_w", HEAD_DIM))
_PACK = sum(n for _, _, n in _SMALL) + LANE


def _small_sync(grow, wrow, mrow, vrow):
    def body(g_ref, w_ref, m_ref, v_ref, o_ref, gath, ssem, rsem):
        x, y, c, _ = _my_place()
        me = 4 * x + 2 * y + c
        gath[pl.ds(me, 1), :] = g_ref[...]
        cps = []
        for dlt in range(1, N_DEV):
            peer = (me + dlt) % N_DEV
            cp = pltpu.make_async_remote_copy(
                src_ref=g_ref, dst_ref=gath.at[pl.ds(me, 1), :], send_sem=ssem.at[peer], recv_sem=rsem.at[me],
                device_id=(peer // 4, (peer // 2) % 2, peer % 2), device_id_type=MESH)
            cp.start()
            cps.append(cp)
        for dlt in range(1, N_DEV):
            src = (me + dlt) % N_DEV
            pltpu.make_async_remote_copy(
                src_ref=g_ref, dst_ref=gath.at[pl.ds(src, 1), :], send_sem=ssem.at[src], recv_sem=rsem.at[src],
                device_id=(x, y, c), device_id_type=MESH).wait_recv()
        for cp in cps:
            cp.wait_send()
        g = gath[0:1, :]
        for j in range(1, N_DEV):
            g = g + gath[j:j + 1, :]
        d, mn, vn = _adamw(w_ref[...], g, m_ref[...], v_ref[...])
        o_ref[0:1, :] = g
        o_ref[1:2, :] = d
        o_ref[2:3, :] = mn
        o_ref[3:4, :] = vn

    vm = pl.BlockSpec(memory_space=pltpu.VMEM)
    return pl.pallas_call(
        body, in_specs=[vm] * 4, out_specs=vm, out_shape=SDS((4, _PACK), F32),
        scratch_shapes=[pltpu.VMEM((N_DEV, _PACK), F32), pltpu.SemaphoreType.DMA((N_DEV,)), pltpu.SemaphoreType.DMA((N_DEV,))],
        name="small_sync")(grow, wrow, mrow, vrow)


def _pack_small(d, loss_row=None):
    parts = [d[k].reshape(1, n).astype(F32) for k, _, n in _SMALL]
    parts.append(loss_row if loss_row is not None else jnp.ones((1, LANE), F32))
    return jnp.concatenate(parts, axis=1)


def kernel(x, mem, ln_mix_w, w_in, hg_norm_w, hg_lower_bounds, w_branch_a, w_branch_b, w_out, ln_cross_w, ln_mem_w, wq_cross, wkv_cross, wo_cross, ln_ffn_w, w1, w3, w2, ln_final_w, loss_target, m_ln_mix_w, m_w_in, m_hg_norm_w, m_hg_lower_bounds, m_w_branch_a, m_w_branch_b, m_w_out, m_ln_cross_w, m_ln_mem_w, m_wq_cross, m_wkv_cross, m_wo_cross, m_ln_ffn_w, m_w1, m_w3, m_w2, m_ln_final_w, v_ln_mix_w, v_w_in, v_hg_norm_w, v_hg_lower_bounds, v_w_branch_a, v_w_branch_b, v_w_out, v_ln_cross_w, v_ln_mem_w, v_wq_cross, v_wkv_cross, v_wo_cross, v_ln_ffn_w, v_w1, v_w3, v_w2, v_ln_final_w):
    given = dict(locals())
    place = jnp.stack([lax.axis_index("c"), 2 * lax.axis_index("x") + lax.axis_index("y")]).astype(jnp.int32)

    shard = {key: given[ref][0] for key, ref, _, _, _, _ in _MATS}
    xkeys = [key for key, _, _, _, kind, _ in _MATS if kind == "colx"]
    heads = dict(zip(xkeys, _straddle_exchange([shard[k] for k in xkeys])))
    units = []
    for key, _, _, _, kind, _ in _MATS:
        if kind == "colx":
            units.append(_cast_unit_x(shard[key], heads[key], "cast_" + key))
        else:
            units.append(_cast_unit(shard[key], "cast_" + key))
    full = _all_gather(units)

    p = {key: full[i] for i, (key, *_) in enumerate(_MATS)}
    small_w = {"ln_mix": ln_mix_w, "ln_cross": ln_cross_w, "ln_mem": ln_mem_w, "ln_ffn": ln_ffn_w,
               "ln_final": ln_final_w.reshape(1, D_MODEL), "hg_lb": hg_lower_bounds, "hg_norm": hg_norm_w}
    p.update(small_w)
    loss_row, grad_x, gw, gs = _local_step(x[0], mem[0], loss_target[0], p)

    sib = _rs_sibling([gw[key] for key, *_ in _MATS])
    parts = [_chip_sum(gw[key], sib[i], kind, sh, place, "chip_sum_" + key) for i, (key, _, _, _, kind, sh) in enumerate(_MATS)]
    recv = _rs_chips(parts)
    out = {}
    for i, (key, ref, _, _, kind, sh) in enumerate(_MATS):
        res = _adam_mat(parts[i], recv[i], shard[key], given["m_" + ref][0], given["v_" + ref][0], kind, sh, place, "adam_" + key)
        out[ref] = [r[None] for r in res]

    sm = {k: given["m_" + ref] for k, ref, _ in _SMALL}
    sv = {k: given["v_" + ref] for k, ref, _ in _SMALL}
    synced = _small_sync(_pack_small(gs, loss_row), _pack_small(small_w), _pack_small(sm), _pack_small(sv))
    off = 0
    for k, ref, n in _SMALL:
        shape = given[ref].shape
        out[ref] = [synced[r, off:off + n].reshape(shape) for r in range(4)]
        off += n
    loss = synced[0, off]

    order = ["ln_mix_w", "w_in", "hg_norm_w", "hg_lower_bounds", "w_branch_a", "w_branch_b", "w_out", "ln_cross_w", "ln_mem_w",
             "wq_cross", "wkv_cross", "wo_cross", "ln_ffn_w", "w1", "w3", "w2", "ln_final_w"]
    return (loss, grad_x[None], *[out[n][0] for n in order], *[out[n][1] for n in order],
            *[out[n][2] for n in order], *[out[n][3] for n in order])
```

```python
import functools
import math

import jax
import jax.numpy as jnp
from jax import lax
from jax.experimental import pallas as pl
from jax.experimental.pallas import tpu as pltpu
from jax.experimental.pallas import tpu_sc as plsc

F32 = jnp.float32
BF = jnp.bfloat16
SDS = jax.ShapeDtypeStruct

D_MODEL = 2048
SEQ = 2048
HEAD_DIM = 128
MEM_LEN = 256
ATT_GROUPS = ((128, 1), (512, 4), (2048, 16))
ATT_HEADS = 4
ATT_WIDTH = 1536
ATT_OUT = 512
HG_HEADS = 8
HG_WIDTH = 1024
HG_CHUNK = 64
IN_WIDTH = 12800
CROSS_HEADS = 4
CROSS_WIDTH = 512
D_FF = 5632
RMS_EPS = 1e-6
ADAM_LR = 0.001
ADAM_B1 = 0.9
ADAM_B2 = 0.999
ADAM_EPS = 1e-08
ADAM_WD = 0.01
ADAM_STEP = 10
N_DEV = 8

LANE = 128
QA0, KA0, VA0 = 0, 12, 24
QH0, FH0, IH0, GH0 = 36, 44, 52, 60
GA0, GB0 = 68, 84

VMEM_BIG = 56 * 1024 * 1024
MESH = pl.DeviceIdType.MESH


def _cp(vmem=None, **kw):
    if vmem is not None:
        kw["vmem_limit_bytes"] = vmem
    return pltpu.CompilerParams(**kw)


def _pick(n, cands):
    for c in cands:
        if n % c == 0:
            return c
    raise ValueError(f"no tile for {n}")


def _mm(a, b, form, out_dtype, name, res=None, tiles=None):
    if form == "nn":
        (M, K), (K2, N) = a.shape, b.shape
    elif form == "nt":
        (M, K), (N, K2) = a.shape, b.shape
    else:
        (K, M), (K2, N) = a.shape, b.shape
    assert K == K2, (a.shape, b.shape, form)
    if tiles is None:
        tm = _pick(M, (1024, 512, 256))
        tn = _pick(N, (1280, 1024, 1408, 512, 256))
        tk = _pick(K, (512, 640, 256))
    else:
        tm, tn, tk = tiles
    nk = K // tk
    if form == "nn":
        a_spec = pl.BlockSpec((tm, tk), lambda i, j, k: (i, k))
        b_spec = pl.BlockSpec((tk, tn), lambda i, j, k: (k, j))
        dims = ((1,), (0,))
    elif form == "nt":
        a_spec = pl.BlockSpec((tm, tk), lambda i, j, k: (i, k))
        b_spec = pl.BlockSpec((tn, tk), lambda i, j, k: (j, k))
        dims = ((1,), (1,))
    else:
        a_spec = pl.BlockSpec((tk, tm), lambda i, j, k: (k, i))
        b_spec = pl.BlockSpec((tk, tn), lambda i, j, k: (k, j))
        dims = ((0,), (0,))
    o_spec = pl.BlockSpec((tm, tn), lambda i, j, k: (i, j))
    in_specs = [a_spec, b_spec]
    args = [a, b]
    if res is not None:
        in_specs.append(o_spec)
        args.append(res)

    def body(*refs):
        if res is not None:
            a_ref, b_ref, r_ref, o_ref, acc = refs
        else:
            a_ref, b_ref, o_ref, acc = refs
        k = pl.program_id(2)

        @pl.when(k == 0)
        def _():
            acc[...] = jnp.zeros_like(acc)

        acc[...] += lax.dot_general(a_ref[...].astype(BF), b_ref[...].astype(BF), (dims, ((), ())),
                                    preferred_element_type=F32)

        @pl.when(k == nk - 1)
        def _():
            r = acc[...]
            if res is not None:
                r = r + r_ref[...].astype(F32)
            o_ref[...] = r.astype(o_ref.dtype)

    return pl.pallas_call(
        body, grid=(M // tm, N // tn, nk), in_specs=in_specs, out_specs=o_spec,
        out_shape=SDS((M, N), out_dtype), scratch_shapes=[pltpu.VMEM((tm, tn), F32)],
        compiler_params=_cp(VMEM_BIG, dimension_semantics=("parallel", "parallel", "arbitrary")),
        name=name)(*args)


def _rms(x, w):
    return x * lax.rsqrt(jnp.mean(x * x, axis=-1, keepdims=True) + RMS_EPS) * w


def _rms_fwd(x, w, name):
    R, D = x.shape
    tr = 256

    def body(x_ref, w_ref, o_ref):
        o_ref[...] = _rms(x_ref[...], w_ref[...]).astype(o_ref.dtype)

    return pl.pallas_call(
        body, grid=(R // tr,),
        in_specs=[pl.BlockSpec((tr, D), lambda i: (i, 0)), pl.BlockSpec((1, D), lambda i: (0, 0))],
        out_specs=pl.BlockSpec((tr, D), lambda i: (i, 0)), out_shape=SDS((R, D), BF), name=name)(x, w)


def _rms_bwd(x, w, dh, dres, name):
    R, D = x.shape
    tr = 256
    want_dx = dres is not None

    def body(*refs):
        if want_dx:
            x_ref, w_ref, dh_ref, dr_ref, dx_ref, dw_ref = refs
        else:
            x_ref, w_ref, dh_ref, dw_ref = refs

        @pl.when(pl.program_id(0) == 0)
        def _():
            dw_ref[...] = jnp.zeros_like(dw_ref)

        _, vjp = jax.vjp(_rms, x_ref[...], w_ref[...])
        dx, dw = vjp(dh_ref[...].astype(F32))
        dw_ref[...] += dw
        if want_dx:
            dx_ref[...] = dx + dr_ref[...]

    row = pl.BlockSpec((tr, D), lambda i: (i, 0))
    vec = pl.BlockSpec((1, D), lambda i: (0, 0))
    if want_dx:
        return pl.pallas_call(body, grid=(R // tr,), in_specs=[row, vec, row, row], out_specs=[row, vec],
                              out_shape=[SDS((R, D), F32), SDS((1, D), F32)], name=name)(x, w, dh, dres)
    return pl.pallas_call(body, grid=(R // tr,), in_specs=[row, vec, row], out_specs=vec,
                          out_shape=SDS((1, D), F32), name=name)(x, w, dh)


def _final_loss(x3, w, tgt):
    R, D = x3.shape
    tr = 256

    def loss_fn(xv, wv, tv):
        err = _rms(xv, wv) - tv
        return 0.5 * jnp.sum(jnp.mean(err * err, axis=-1))

    def body(x_ref, w_ref, t_ref, loss_ref, dx_ref, dw_ref):
        @pl.when(pl.program_id(0) == 0)
        def _():
            dw_ref[...] = jnp.zeros_like(dw_ref)
            loss_ref[...] = jnp.zeros_like(loss_ref)

        tv = t_ref[...]
        val, vjp = jax.vjp(lambda a, b: loss_fn(a, b, tv), x_ref[...], w_ref[...])
        dx, dw = vjp(jnp.ones((), F32))
        dx_ref[...] = dx
        dw_ref[...] += dw
        loss_ref[...] += jnp.full(loss_ref.shape, val, F32)

    row = pl.BlockSpec((tr, D), lambda i: (i, 0))
    vec = pl.BlockSpec((1, D), lambda i: (0, 0))
    one = pl.BlockSpec((1, LANE), lambda i: (0, 0))
    return pl.pallas_call(body, grid=(R // tr,), in_specs=[row, vec, row], out_specs=[one, row, vec],
                          out_shape=[SDS((1, LANE), F32), SDS((R, D), F32), SDS((1, D), F32)],
                          name="final_loss")(x3, w, tgt)


NEG = -1e30


def _att_blocks(S, d):
    L = S // d
    nb = L // LANE
    return nb, (2 if nb > 1 else 1)


def _att_rows(S, d, idx):
    nb, nkb = _att_blocks(S, d)
    r = idx // nb
    n = idx % nb
    kb = jnp.maximum(n - 1, 0)
    if d == 1:
        qrows = pl.ds(pl.multiple_of(n * LANE, LANE), LANE)
        krows = pl.ds(pl.multiple_of(kb * LANE, LANE), LANE * nkb)
    else:
        qrows = pl.ds(r + n * (LANE * d), LANE, stride=d)
        krows = pl.ds(r + kb * (LANE * d), LANE * nkb, stride=d)
    qpos = n * LANE + lax.broadcasted_iota(jnp.int32, (LANE, LANE * nkb), 0)
    kpos = kb * LANE + lax.broadcasted_iota(jnp.int32, (LANE, LANE * nkb), 1)
    return qrows, krows, qpos - kpos


def _slab(col):
    return pl.BlockSpec((SEQ, LANE), lambda h, col=col: (0, col + h))


def _attn_fwd(proj):
    S = proj.shape[0]
    scale = HEAD_DIM ** -0.5

    def body(q0, q1, q2, k0, k1, k2, v0, v1, v2, o_ref, lse_ref, og, lg):
        qs, ks, vs = (q0, q1, q2), (k0, k1, k2), (v0, v1, v2)
        for g, (window, d) in enumerate(ATT_GROUPS):
            def blk(idx, carry, g=g, window=window, d=d):
                qrows, krows, dist = _att_rows(S, d, idx)
                q = qs[g][qrows, :].astype(BF)
                k = ks[g][krows, :].astype(BF)
                v = vs[g][krows, :].astype(BF)
                s = lax.dot_general(q, k, (((1,), (1,)), ((), ())), preferred_element_type=F32) * scale
                s = jnp.where((dist >= 0) & (dist <= window // d), s, NEG)
                m = jnp.max(s, axis=-1, keepdims=True)
                p = jnp.exp(s - m)
                l = jnp.sum(p, axis=-1, keepdims=True)
                o = jnp.dot((p / l).astype(BF), v, preferred_element_type=F32)
                og[g, qrows, :] = o
                lg[g, qrows, :] = jnp.broadcast_to(m + jnp.log(l), (LANE, LANE))
                return carry
            lax.fori_loop(0, S // LANE, blk, 0)

        def merge(t, carry):
            rows = pl.ds(pl.multiple_of(t * 256, 256), 256)
            l0, l1, l2 = lg[0, rows, :], lg[1, rows, :], lg[2, rows, :]
            m = jnp.maximum(jnp.maximum(l0, l1), l2)
            w0, w1, w2 = jnp.exp(l0 - m), jnp.exp(l1 - m), jnp.exp(l2 - m)
            den = w0 + w1 + w2
            o_ref[rows, :] = (w0 * og[0, rows, :] + w1 * og[1, rows, :] + w2 * og[2, rows, :]) / den
            lse_ref[rows, :] = m + jnp.log(den)
            return carry
        lax.fori_loop(0, S // 256, merge, 0)

    ins = [_slab(QA0 + 4 * g) for g in range(3)] + [_slab(KA0 + 4 * g) for g in range(3)] + [_slab(VA0 + 4 * g) for g in range(3)]
    out = pl.BlockSpec((SEQ, LANE), lambda h: (0, h))
    return pl.pallas_call(
        body, grid=(ATT_HEADS,), in_specs=ins, out_specs=[out, out],
        out_shape=[SDS((S, ATT_OUT), F32), SDS((S, ATT_OUT), F32)],
        scratch_shapes=[pltpu.VMEM((3, S, LANE), F32), pltpu.VMEM((3, S, LANE), F32)],
        compiler_params=_cp(VMEM_BIG), name="attn_fwd")(*([proj] * 9))


def _attn_bwd(proj, o_att, lse, do_att):
    S = proj.shape[0]
    scale = HEAD_DIM ** -0.5

    def body(q0, q1, q2, k0, k1, k2, v0, v1, v2, o_ref, lse_ref, do_ref,
             dq0, dq1, dq2, dk0, dk1, dk2, dv0, dv1, dv2, dqa, dka, dva, delta):
        qs, ks, vs = (q0, q1, q2), (k0, k1, k2), (v0, v1, v2)
        dqs, dks, dvs = (dq0, dq1, dq2), (dk0, dk1, dk2), (dv0, dv1, dv2)
        delta[...] = jnp.broadcast_to(jnp.sum(do_ref[...] * o_ref[...], axis=-1, keepdims=True), delta.shape)
        for g, (window, d) in enumerate(ATT_GROUPS):
            dka[...] = jnp.zeros_like(dka)
            dva[...] = jnp.zeros_like(dva)

            def blk(idx, carry, g=g, window=window, d=d):
                qrows, krows, dist = _att_rows(S, d, idx)
                q = qs[g][qrows, :].astype(BF)
                k = ks[g][krows, :].astype(BF)
                v = vs[g][krows, :].astype(BF)
                do = do_ref[qrows, :]
                s = lax.dot_general(q, k, (((1,), (1,)), ((), ())), preferred_element_type=F32) * scale
                p = jnp.exp(s - lse_ref[qrows, :][:, 0:1])
                p = jnp.where((dist >= 0) & (dist <= window // d), p, 0.0)
                dob = do.astype(BF)
                dva[krows, :] += lax.dot_general(p.astype(BF), dob, (((0,), (0,)), ((), ())), preferred_element_type=F32)
                dp = lax.dot_general(dob, v, (((1,), (1,)), ((), ())), preferred_element_type=F32)
                ds = (p * (dp - delta[qrows, :][:, 0:1]) * scale).astype(BF)
                dqa[qrows, :] = jnp.dot(ds, k, preferred_element_type=F32)
                dka[krows, :] += lax.dot_general(ds, q, (((0,), (0,)), ((), ())), preferred_element_type=F32)
                return carry
            lax.fori_loop(0, S // LANE, blk, 0)
            dqs[g][...] = dqa[...].astype(dqs[g].dtype)
            dks[g][...] = dka[...].astype(dks[g].dtype)
            dvs[g][...] = dva[...].astype(dvs[g].dtype)

    cols = [QA0 + 4 * g for g in range(3)] + [KA0 + 4 * g for g in range(3)] + [VA0 + 4 * g for g in range(3)]
    ins = [_slab(c) for c in cols]
    one = pl.BlockSpec((SEQ, LANE), lambda h: (0, h))
    return pl.pallas_call(
        body, grid=(ATT_HEADS,), in_specs=ins + [one, one, one], out_specs=[one] * 9,
        out_shape=[SDS((S, ATT_OUT), BF)] * 9,
        scratch_shapes=[pltpu.VMEM((S, LANE), F32)] * 4,
        compiler_params=_cp(VMEM_BIG), name="attn_bwd")(*([proj] * 9), o_att, lse, do_att)


def _bdot(a, b, dims):
    return lax.dot_general(a.astype(BF), b.astype(BF), (dims, ((), ())), preferred_element_type=F32)


def _hg_chunk(qh, fh, ih, gh, lbraw, nw, st):
    C = HG_CHUNK
    mx = jnp.maximum(lbraw[0:1], lbraw[1:2])
    e0 = jnp.exp(lbraw[0:1] - mx)
    e1 = jnp.exp(lbraw[1:2] - mx)
    lb = e0 / (e0 + e1)
    f = lb + (1.0 - lb) * jax.nn.sigmoid(fh)
    logf = jnp.log(f)
    k = 1.0 - f
    q = qh * jax.nn.sigmoid(qh)
    causal = lax.broadcasted_iota(jnp.int32, (C, C), 0) >= lax.broadcasted_iota(jnp.int32, (C, C), 1)
    b = lax.dot_general(causal.astype(F32), logf, (((1,), (0,)), ((), ())), precision=lax.Precision.HIGHEST,
                        preferred_element_type=F32)
    b_last = b[C - 1:C]
    st_out = st * jnp.exp(b_last) + _bdot(ih, k * jnp.exp(b_last - b), ((0,), (0,)))
    inter = _bdot(q * jnp.exp(b), st, ((1,), (1,)))
    b_ref = b[C // 2:C // 2 + 1]
    a = _bdot(q * jnp.exp(b - b_ref), k * jnp.exp(b_ref - b), ((1,), (1,)))
    a = jnp.where(causal, a, 0.0)
    o = inter + _bdot(a, ih, ((1,), (0,)))
    o = o * lax.rsqrt(jnp.mean(o * o, axis=-1, keepdims=True) + RMS_EPS) * nw
    return o * (gh * jax.nn.sigmoid(gh)), st_out


def _hg_fwd(proj, lb, nw):
    S = proj.shape[0]
    C = HG_CHUNK
    NC = S // C

    def body(q_ref, f_ref, i_ref, g_ref, lb_ref, nw_ref, o_ref, st_ref, st_scr):
        @pl.when(pl.program_id(1) == 0)
        def _():
            st_scr[...] = jnp.zeros_like(st_scr)

        st = st_scr[...]
        st_ref[0, 0] = st
        o, st_out = _hg_chunk(q_ref[...], f_ref[...], i_ref[...], g_ref[...], lb_ref[...], nw_ref[...], st)
        o_ref[...] = o.astype(o_ref.dtype)
        st_scr[...] = st_out

    spec = lambda base: pl.BlockSpec((C, LANE), lambda h, n, base=base: (n, base + h))
    return pl.pallas_call(
        body, grid=(HG_HEADS, NC),
        in_specs=[spec(QH0), spec(FH0), spec(IH0), spec(GH0),
                  pl.BlockSpec((2, LANE), lambda h, n: (0, h)), pl.BlockSpec((1, LANE), lambda h, n: (0, 0))],
        out_specs=[pl.BlockSpec((C, LANE), lambda h, n: (n, h)), pl.BlockSpec((1, 1, LANE, LANE), lambda h, n: (h, n, 0, 0))],
        out_shape=[SDS((S, HG_WIDTH), BF), SDS((HG_HEADS, NC, LANE, LANE), F32)],
        scratch_shapes=[pltpu.VMEM((LANE, LANE), F32)], name="hg_fwd")(proj, proj, proj, proj, lb, nw)


def _hg_bwd(proj, lb, nw, states, do_hg):
    S = proj.shape[0]
    C = HG_CHUNK
    NC = S // C

    def body(q_ref, f_ref, i_ref, g_ref, lb_ref, nw_ref, st_ref, do_ref,
             dq_ref, df_ref, di_ref, dg_ref, dlb_ref, dnw_ref, dst_scr):
        n = pl.program_id(1)

        @pl.when(n == 0)
        def _():
            dst_scr[...] = jnp.zeros_like(dst_scr)
            dlb_ref[...] = jnp.zeros_like(dlb_ref)

        @pl.when((n == 0) & (pl.program_id(0) == 0))
        def _():
            dnw_ref[...] = jnp.zeros_like(dnw_ref)

        _, vjp = jax.vjp(_hg_chunk, q_ref[...], f_ref[...], i_ref[...], g_ref[...], lb_ref[...], nw_ref[...], st_ref[0, 0])
        dq, df, di, dg, dlb, dnw, dst = vjp((do_ref[...].astype(F32), dst_scr[...]))
        dq_ref[...] = dq.astype(dq_ref.dtype)
        df_ref[...] = df.astype(df_ref.dtype)
        di_ref[...] = di.astype(di_ref.dtype)
        dg_ref[...] = dg.astype(dg_ref.dtype)
        dlb_ref[...] += dlb
        dnw_ref[...] += dnw
        dst_scr[...] = dst

    spec = lambda base: pl.BlockSpec((C, LANE), lambda h, n, base=base: (NC - 1 - n, base + h))
    ospec = pl.BlockSpec((C, LANE), lambda h, n: (NC - 1 - n, h))
    lbs = pl.BlockSpec((2, LANE), lambda h, n: (0, h))
    nws = pl.BlockSpec((1, LANE), lambda h, n: (0, 0))
    outs = pl.pallas_call(
        body, grid=(HG_HEADS, NC),
        in_specs=[spec(QH0), spec(FH0), spec(IH0), spec(GH0), lbs, nws,
                  pl.BlockSpec((1, 1, LANE, LANE), lambda h, n: (h, NC - 1 - n, 0, 0)),
                  ospec],
        out_specs=[ospec] * 4 + [lbs, nws],
        out_shape=[SDS((S, HG_WIDTH), BF)] * 4 + [SDS((2, HG_WIDTH), F32), SDS((1, LANE), F32)],
        scratch_shapes=[pltpu.VMEM((LANE, LANE), F32)], name="hg_bwd")(proj, proj, proj, proj, lb, nw, states, do_hg)
    return outs


def _merge(ga, gb, ba, bb):
    return jax.nn.sigmoid(ga) * ba + jax.nn.sigmoid(gb) * bb


_MT = 512


def _gate_specs():
    ga = pl.BlockSpec((_MT, _MT), lambda i, j: (i, GA0 * LANE // _MT + j))
    gb = pl.BlockSpec((_MT, _MT), lambda i, j: (i, GB0 * LANE // _MT + j))
    t = pl.BlockSpec((_MT, _MT), lambda i, j: (i, j))
    return ga, gb, t


def _merge_fwd(proj, ba, bb):
    S, D = ba.shape
    ga, gb, t = _gate_specs()

    def body(ga_ref, gb_ref, ba_ref, bb_ref, o_ref):
        o_ref[...] = _merge(ga_ref[...], gb_ref[...], ba_ref[...], bb_ref[...]).astype(o_ref.dtype)

    return pl.pallas_call(body, grid=(S // _MT, D // _MT), in_specs=[ga, gb, t, t], out_specs=t,
                          out_shape=SDS((S, D), BF), name="merge_fwd")(proj, proj, ba, bb)


def _merge_bwd(proj, ba, bb, dm):
    S, D = ba.shape
    ga, gb, t = _gate_specs()

    def body(ga_ref, gb_ref, ba_ref, bb_ref, dm_ref, dga_ref, dgb_ref, dba_ref, dbb_ref):
        _, vjp = jax.vjp(_merge, ga_ref[...], gb_ref[...], ba_ref[...], bb_ref[...])
        dga, dgb, dba, dbb = vjp(dm_ref[...])
        dga_ref[...] = dga.astype(BF)
        dgb_ref[...] = dgb.astype(BF)
        dba_ref[...] = dba.astype(BF)
        dbb_ref[...] = dbb.astype(BF)

    return pl.pallas_call(body, grid=(S // _MT, D // _MT), in_specs=[ga, gb, t, t, t], out_specs=[t] * 4,
                          out_shape=[SDS((S, D), BF)] * 4, name="merge_bwd")(proj, proj, ba, bb, dm)


def _swiglu(a, b):
    return a * jax.nn.sigmoid(a) * b


def _swiglu_fwd(a, b):
    S, F = a.shape
    tf = _pick(F, (1408, 512))
    t = pl.BlockSpec((_MT, tf), lambda i, j: (i, j))

    def body(a_ref, b_ref, o_ref):
        o_ref[...] = _swiglu(a_ref[...], b_ref[...]).astype(o_ref.dtype)

    return pl.pallas_call(body, grid=(S // _MT, F // tf), in_specs=[t, t], out_specs=t,
                          out_shape=SDS((S, F), BF), name="swiglu_fwd")(a, b)


def _swiglu_bwd(a, b, dg):
    S, F = a.shape
    tf = _pick(F, (1408, 512))
    t = pl.BlockSpec((_MT, tf), lambda i, j: (i, j))

    def body(a_ref, b_ref, dg_ref, da_ref, db_ref):
        _, vjp = jax.vjp(_swiglu, a_ref[...], b_ref[...])
        da, db = vjp(dg_ref[...])
        da_ref[...] = da.astype(BF)
        db_ref[...] = db.astype(BF)

    return pl.pallas_call(body, grid=(S // _MT, F // tf), in_specs=[t, t, t], out_specs=[t, t],
                          out_shape=[SDS((S, F), BF)] * 2, name="swiglu_bwd")(a, b, dg)


def _cross(q, k, v):
    s = _bdot(q, k, ((1,), (1,))) * (HEAD_DIM ** -0.5)
    e = jnp.exp(s - jnp.max(s, axis=-1, keepdims=True))
    p = e / jnp.sum(e, axis=-1, keepdims=True)
    return _bdot(p, v, ((1,), (0,)))


def _cross_specs():
    q = pl.BlockSpec((_MT, LANE), lambda h, i: (i, h))
    k = pl.BlockSpec((MEM_LEN, LANE), lambda h, i: (0, h))
    v = pl.BlockSpec((MEM_LEN, LANE), lambda h, i: (0, CROSS_HEADS + h))
    return q, k, v


def _cross_fwd(qc, kvc):
    S = qc.shape[0]
    q, k, v = _cross_specs()

    def body(q_ref, k_ref, v_ref, o_ref):
        o_ref[...] = _cross(q_ref[...], k_ref[...], v_ref[...]).astype(o_ref.dtype)

    return pl.pallas_call(body, grid=(CROSS_HEADS, S // _MT), in_specs=[q, k, v], out_specs=q,
                          out_shape=SDS((S, CROSS_WIDTH), BF), name="cross_fwd")(qc, kvc, kvc)


def _cross_bwd(qc, kvc, doc):
    S = qc.shape[0]
    q, k, v = _cross_specs()

    def body(q_ref, k_ref, v_ref, do_ref, dq_ref, dk_ref, dv_ref):
        @pl.when(pl.program_id(1) == 0)
        def _():
            dk_ref[...] = jnp.zeros_like(dk_ref)
            dv_ref[...] = jnp.zeros_like(dv_ref)

        _, vjp = jax.vjp(_cross, q_ref[...], k_ref[...], v_ref[...])
        dq, dk, dv = vjp(do_ref[...])
        dq_ref[...] = dq.astype(dq_ref.dtype)
        dk_ref[...] += dk
        dv_ref[...] += dv

    return pl.pallas_call(body, grid=(CROSS_HEADS, S // _MT), in_specs=[q, k, v, q], out_specs=[q, k, k],
                          out_shape=[SDS((S, CROSS_WIDTH), BF), SDS((MEM_LEN, CROSS_WIDTH), F32), SDS((MEM_LEN, CROSS_WIDTH), F32)],
                          name="cross_bwd")(qc, kvc, kvc, doc)


def _local_step(x, mem, tgt, p, ready):
    h = _rms_fwd(x, p["ln_mix"], "rms_mix")
    proj = _mm(h, p["w_in"], "nn", F32, "mm_proj")
    o_att, lse = _attn_fwd(proj)
    o_hg, states = _hg_fwd(proj, p["hg_lb"], p["hg_norm"])
    ba = _mm(o_att, p["w_ba"], "nn", F32, "mm_ba")
    bb = _mm(o_hg, p["w_bb"], "nn", F32, "mm_bb")
    merged = _merge_fwd(proj, ba, bb)
    x1 = _mm(merged, p["w_out"], "nn", F32, "mm_out", res=x)
    hc = _rms_fwd(x1, p["ln_cross"], "rms_cross")
    qc = _mm(hc, p["wq"], "nn", F32, "mm_q")
    mn = _rms_fwd(mem, p["ln_mem"], "rms_mem")
    kvc = _mm(mn, p["wkv"], "nn", F32, "mm_kv")
    oc = _cross_fwd(qc, kvc)
    x2 = _mm(oc, p["wo"], "nn", F32, "mm_o", res=x1)
    hf = _rms_fwd(x2, p["ln_ffn"], "rms_ffn")
    a = _mm(hf, p["w1"], "nn", F32, "mm_w1")
    b = _mm(hf, p["w3"], "nn", F32, "mm_w3")
    g = _swiglu_fwd(a, b)
    x3 = _mm(g, p["w2"], "nn", F32, "mm_w2", res=x2)
    loss, dx3, d_ln_final = _final_loss(x3, p["ln_final"], tgt)

    gw, gs = {}, {"ln_final": d_ln_final}
    dg = _mm(dx3, p["w2"], "nt", F32, "mm_dg")
    gw["w2"] = _mm(g, dx3, "tn", BF, "mm_dw2")
    da, db = _swiglu_bwd(a, b, dg)
    dhf = _mm(da, p["w1"], "nt", F32, "mm_dhf1")
    dhf = _mm(db, p["w3"], "nt", F32, "mm_dhf3", res=dhf)
    gw["w1"] = _mm(hf, da, "tn", BF, "mm_dw1")
    gw["w3"] = _mm(hf, db, "tn", BF, "mm_dw3")
    ready("ffn", gw)
    dx2, gs["ln_ffn"] = _rms_bwd(x2, p["ln_ffn"], dhf, dx3, "rms_ffn_bwd")
    doc = _mm(dx2, p["wo"], "nt", F32, "mm_doc")
    gw["wo"] = _mm(oc, dx2, "tn", BF, "mm_dwo")
    dqc, dk, dv = _cross_bwd(qc, kvc, doc)
    dkvc = jnp.concatenate([dk, dv], axis=1)
    dhc = _mm(dqc, p["wq"], "nt", F32, "mm_dhc")
    gw["wq"] = _mm(hc, dqc, "tn", BF, "mm_dwq")
    dmn = _mm(dkvc, p["wkv"], "nt", F32, "mm_dmn")
    gw["wkv"] = _mm(mn, dkvc, "tn", BF, "mm_dwkv")
    ready("cross", gw)
    gs["ln_mem"] = _rms_bwd(mem, p["ln_mem"], dmn, None, "rms_mem_bwd")
    dx1, gs["ln_cross"] = _rms_bwd(x1, p["ln_cross"], dhc, dx2, "rms_cross_bwd")
    dmerged = _mm(dx1, p["w_out"], "nt", F32, "mm_dmerged")
    gw["w_out"] = _mm(merged, dx1, "tn", BF, "mm_dwout")
    dga, dgb, dba, dbb = _merge_bwd(proj, ba, bb, dmerged)
    do_att = _mm(dba, p["w_ba"], "nt", F32, "mm_doatt")
    gw["w_ba"] = _mm(o_att, dba, "tn", BF, "mm_dwba")
    do_hg = _mm(dbb, p["w_bb"], "nt", F32, "mm_dohg")
    gw["w_bb"] = _mm(o_hg, dbb, "tn", BF, "mm_dwbb")
    ready("mix", gw)
    datt = _attn_bwd(proj, o_att, lse, do_att)
    dqh, dfh, dih, dgh, gs["hg_lb"], gs["hg_norm"] = _hg_bwd(proj, p["hg_lb"], p["hg_norm"], states, do_hg)
    dproj = jnp.concatenate([*datt, dqh, dfh, dih, dgh, dga, dgb], axis=1)
    dh = _mm(dproj, p["w_in"], "nt", F32, "mm_dh")
    gw["w_in"] = _mm(h, dproj, "tn", BF, "mm_dwin")
    ready("in", gw)
    grad_x, gs["ln_mix"] = _rms_bwd(x, p["ln_mix"], dh, dx1, "rms_mix_bwd")
    return loss, grad_x, gs


_HALF = LANE // 2
_MATS = (
    ("w_in", "w_in", D_MODEL, IN_WIDTH, "colx", IN_WIDTH // N_DEV),
    ("w_ba", "w_branch_a", ATT_OUT, D_MODEL, "col", D_MODEL // N_DEV),
    ("w_bb", "w_branch_b", HG_WIDTH, D_MODEL, "col", D_MODEL // N_DEV),
    ("w_out", "w_out", D_MODEL, D_MODEL, "row", D_MODEL // N_DEV),
    ("wq", "wq_cross", D_MODEL, CROSS_WIDTH, "row", D_MODEL // N_DEV),
    ("wkv", "wkv_cross", D_MODEL, 2 * CROSS_WIDTH, "row", D_MODEL // N_DEV),
    ("wo", "wo_cross", CROSS_WIDTH, D_MODEL, "col", D_MODEL // N_DEV),
    ("w1", "w1", D_MODEL, D_FF, "colx", D_FF // N_DEV),
    ("w3", "w3", D_MODEL, D_FF, "colx", D_FF // N_DEV),
    ("w2", "w2", D_FF, D_MODEL, "row", D_FF // N_DEV),
)
_NM = len(_MATS)
_ANY = pl.BlockSpec(memory_space=pl.ANY)


def _unit_shape(K, N, kind, sh):
    if kind == "row":
        return (sh, N)
    if kind == "col":
        return (K, sh)
    return (K, sh + _HALF)


def _ag_win(ref, kind, sh, chip, par):
    if kind == "row":
        return ref.at[pl.ds(pl.multiple_of((2 * chip + par) * sh, 16), sh), :]
    if kind == "col":
        return ref.at[:, pl.ds(pl.multiple_of((2 * chip + par) * sh, LANE), sh)]
    if par == 0:
        return ref.at[:, pl.ds(pl.multiple_of(2 * sh * chip, LANE), sh + _HALF)]
    return ref.at[:, pl.ds(pl.multiple_of(2 * sh * chip + sh + _HALF, LANE), sh - _HALF)]


def _ag_src(ref, kind, sh, par):
    if kind == "colx" and par == 1:
        return ref.at[:, pl.ds(0, sh - _HALF)]
    return ref


def _rs_win(ref, kind, sh, chip, par):
    if kind == "row":
        return ref.at[pl.ds(pl.multiple_of((2 * chip + par) * sh, 16), sh), :]
    if kind == "col":
        return ref.at[:, pl.ds(pl.multiple_of((2 * chip + par) * sh, LANE), sh)]
    return ref.at[:, pl.ds(pl.multiple_of(2 * sh * chip + par * (sh - _HALF), LANE), sh + _HALF)]


def _my_place():
    x, y, c = lax.axis_index("x"), lax.axis_index("y"), lax.axis_index("c")
    chips = [(1 - x, y), (x, 1 - y), (1 - x, 1 - y)]
    return x, y, c, chips


def _straddle_exchange(shards):
    n = len(shards)

    def body(*refs):
        ins, outs, ssem, rsem, lsem = refs[:n], refs[n:2 * n], refs[2 * n], refs[2 * n + 1], refs[2 * n + 2]
        x, y, c, _ = _my_place()

        @pl.when(c == 1)
        def _():
            cps = []
            for i in range(n):
                head = ins[i].at[:, pl.ds(0, LANE)]
                loc = pltpu.make_async_copy(head, outs[i], lsem.at[i])
                rem = pltpu.make_async_remote_copy(src_ref=head, dst_ref=outs[i], send_sem=ssem.at[i], recv_sem=rsem.at[i],
                                                   device_id=(x, y, 0), device_id_type=MESH)
                loc.start()
                rem.start()
                cps.append((loc, rem))
            for loc, rem in cps:
                loc.wait()
                rem.wait_send()

        @pl.when(c == 0)
        def _():
            for i in range(n):
                head = ins[i].at[:, pl.ds(0, LANE)]
                pltpu.make_async_remote_copy(src_ref=head, dst_ref=outs[i], send_sem=ssem.at[i], recv_sem=rsem.at[i],
                                             device_id=(x, y, 1), device_id_type=MESH).wait_recv()

    return pl.pallas_call(
        body, in_specs=[_ANY] * n, out_specs=[_ANY] * n,
        out_shape=[SDS((s.shape[0], LANE), F32) for s in shards],
        scratch_shapes=[pltpu.SemaphoreType.DMA((n,)), pltpu.SemaphoreType.DMA((n,)), pltpu.SemaphoreType.DMA((n,))],
        name="straddle_exchange")(*shards)


def _cast_unit(w, name):
    K, N = w.shape
    tr = _pick(K, (256, 352))

    def body(w_ref, o_ref):
        o_ref[...] = w_ref[...].astype(BF)

    blk = pl.BlockSpec((tr, N), lambda i: (i, 0))
    return pl.pallas_call(body, grid=(K // tr,), in_specs=[blk], out_specs=blk, out_shape=SDS((K, N), BF), name=name)(w)


def _cast_unit_x(w, head, name):
    K, sh = w.shape
    tr = 256

    def body(w_ref, h_ref, o_ref):
        c = lax.axis_index("c")

        @pl.when(c == 0)
        def _():
            o_ref[:, 0:sh] = w_ref[...].astype(BF)
            o_ref[:, sh:sh + _HALF] = h_ref[:, 0:_HALF].astype(BF)

        @pl.when(c == 1)
        def _():
            o_ref[:, 0:sh - _HALF] = w_ref[:, _HALF:sh].astype(BF)
            o_ref[:, sh - _HALF:sh + _HALF] = jnp.zeros((tr, LANE), BF)

    return pl.pallas_call(
        body, grid=(K // tr,),
        in_specs=[pl.BlockSpec((tr, sh), lambda i: (i, 0)), pl.BlockSpec((tr, LANE), lambda i: (i, 0))],
        out_specs=pl.BlockSpec((tr, sh + _HALF), lambda i: (i, 0)), out_shape=SDS((K, sh + _HALF), BF), name=name)(w, head)


def _handshake(peers):
    barrier = pltpu.get_barrier_semaphore()
    for peer in peers:
        pl.semaphore_signal(barrier, inc=1, device_id=peer, device_id_type=MESH)
    pl.semaphore_wait(barrier, len(peers))


def _sequencer(body, out_type, sems, cid, name):
    return pl.kernel(body, out_type=out_type, mesh=plsc.ScalarSubcoreMesh(axis_name="seq", num_cores=1),
                     scratch_types=sems, compiler_params=pltpu.CompilerParams(collective_id=cid), name=name)


def _all_gather(units, idx, cid, name, after=None):
    n = len(idx)

    def body(*refs):
        us, fulls = refs[:n], refs[-n - 3:-3]
        ssem, rsem, lsem = refs[-3:]
        x, y, c, chips = _my_place()
        _handshake([(x, y, 1 - c)] + [(*ch, c) for ch in chips])

        def gather(par):
            def win(i, chip_xy, p):
                _, _, _, _, kind, sh = _MATS[idx[i]]
                return _ag_win(fulls[i], kind, sh, 2 * chip_xy[0] + chip_xy[1], p)

            def rcopy(i, k, src, dst, to):
                return pltpu.make_async_remote_copy(src_ref=src, dst_ref=dst, send_sem=ssem.at[i, k], recv_sem=rsem.at[i, k],
                                                    device_id=to, device_id_type=MESH)

            sends, locs = [], []
            for i in range(n):
                _, _, _, _, kind, sh = _MATS[idx[i]]
                src = _ag_src(us[i], kind, sh, par)
                mine = win(i, (x, y), par)
                loc = pltpu.make_async_copy(src, mine, lsem.at[i])
                loc.start()
                locs.append(loc)
                first = [rcopy(i, 0, src, mine, (x, y, 1 - c))]
                first += [rcopy(i, 1 + j, src, mine, (*chips[j], c)) for j in range(3)]
                for cp in first:
                    cp.start()
                sends += first
            for i in range(n):
                for j in range(3):
                    blk = win(i, chips[j], par)
                    rcopy(i, 1 + j, blk, blk, (x, y, c)).wait_recv()
                    fwd = rcopy(i, 4 + j, blk, blk, (x, y, 1 - c))
                    fwd.start()
                    sends.append(fwd)
            for i in range(n):
                blk = win(i, (x, y), 1 - par)
                rcopy(i, 0, blk, blk, (x, y, c)).wait_recv()
                for j in range(3):
                    blk = win(i, chips[j], 1 - par)
                    rcopy(i, 4 + j, blk, blk, (x, y, c)).wait_recv()
            for cp in sends:
                cp.wait_send()
            for loc in locs:
                loc.wait()

        for par in (0, 1):
            pl.when(c == par)(functools.partial(gather, par))

    out_type = [SDS((_MATS[i][2], _MATS[i][3]), BF) for i in idx]
    sems = [pltpu.SemaphoreType.DMA((n, 7)), pltpu.SemaphoreType.DMA((n, 7)), pltpu.SemaphoreType.DMA((n,))]
    args = list(units) + ([after] if after is not None else [])
    return _sequencer(body, out_type, sems, cid, name)(*args)


def _rs_sibling(grads, idx, cid, name):
    n = len(idx)

    def body(*refs):
        gs, bufs = refs[:n], refs[n:2 * n]
        ssem, rsem = refs[2 * n:]
        x, y, c, _ = _my_place()
        _handshake([(x, y, 1 - c)])
        cps = []
        for i in range(n):
            _, _, _, _, kind, sh = _MATS[idx[i]]
            for k in range(4):
                cp = pltpu.make_async_remote_copy(
                    src_ref=_rs_win(gs[i], kind, sh, k, 1 - c), dst_ref=bufs[i].at[k], send_sem=ssem.at[i, k], recv_sem=rsem.at[i, k],
                    device_id=(x, y, 1 - c), device_id_type=MESH)
                cp.start()
                cps.append(cp)
        for cp in cps:
            cp.wait()

    out_type = [SDS((4, *_unit_shape(*_MATS[i][2:])), BF) for i in idx]
    sems = [pltpu.SemaphoreType.DMA((n, 4)), pltpu.SemaphoreType.DMA((n, 4))]
    return _sequencer(body, out_type, sems, cid, name)(*grads)


def _chip_sum(g, sib, kind, sh, place, name):
    K, N = g.shape
    U = _unit_shape(K, N, kind, sh)

    def body(pref, g_ref, s_ref, o_ref):
        o_ref[...] = (g_ref[...].astype(F32) + s_ref[...].astype(F32)).astype(BF)

    if kind == "row":
        grid = (4,)
        g_spec = pl.BlockSpec((sh, N), lambda k, pref: (2 * k + pref[0], 0))
        u_spec = pl.BlockSpec((None, sh, N), lambda k, pref: (k, 0, 0))
    else:
        wid = U[1] // LANE
        per_chip = 2 * sh // LANE
        off = (sh - _HALF) // LANE if kind == "colx" else sh // LANE
        grid = (4, wid)
        g_spec = pl.BlockSpec((K, LANE), lambda k, j, pref: (0, per_chip * k + off * pref[0] + j))
        u_spec = pl.BlockSpec((None, K, LANE), lambda k, j, pref: (k, 0, j))
    return pl.pallas_call(
        body, grid_spec=pltpu.PrefetchScalarGridSpec(num_scalar_prefetch=1, grid=grid, in_specs=[g_spec, u_spec], out_specs=u_spec),
        out_shape=SDS((4, *U), BF), name=name)(place, g, sib)


def _rs_chips(parts, idx, cid, name):
    n = len(idx)

    def body(*refs):
        ts, bufs = refs[:n], refs[n:2 * n]
        ssem, rsem = refs[2 * n:]
        x, y, c, chips = _my_place()
        _handshake([(*ch, c) for ch in chips])
        cps = []
        for i in range(n):
            for j in range(3):
                cp = pltpu.make_async_remote_copy(
                    src_ref=ts[i].at[2 * chips[j][0] + chips[j][1]], dst_ref=bufs[i].at[j], send_sem=ssem.at[i, j], recv_sem=rsem.at[i, j],
                    device_id=(*chips[j], c), device_id_type=MESH)
                cp.start()
                cps.append(cp)
        for cp in cps:
            cp.wait()

    out_type = [SDS((3, *_unit_shape(*_MATS[i][2:])), BF) for i in idx]
    sems = [pltpu.SemaphoreType.DMA((n, 3)), pltpu.SemaphoreType.DMA((n, 3))]
    return _sequencer(body, out_type, sems, cid, name)(*parts)


def _adamw(w, g, m, v):
    m = ADAM_B1 * m + (1.0 - ADAM_B1) * g
    v = ADAM_B2 * v + (1.0 - ADAM_B2) * jnp.square(g)
    m_hat = m / (1.0 - ADAM_B1 ** ADAM_STEP)
    v_hat = v / (1.0 - ADAM_B2 ** ADAM_STEP)
    delta = -ADAM_LR * (m_hat / (jnp.sqrt(v_hat) + ADAM_EPS) + ADAM_WD * w)
    return delta, m, v


def _adam_mat(part, recv, w, m, v, kind, sh, place, name):
    R, C = w.shape
    U = part.shape[1:]
    tr = _pick(R, (256, 352))

    def body(pref, t_ref, r_ref, w_ref, m_ref, v_ref, g_out, d_out, m_out, v_out):
        gp = t_ref[...].astype(F32) + r_ref[0].astype(F32) + r_ref[1].astype(F32) + r_ref[2].astype(F32)

        def finish(g):
            d, mn, vn = _adamw(w_ref[...], g, m_ref[...], v_ref[...])
            g_out[...] = g
            d_out[...] = d
            m_out[...] = mn
            v_out[...] = vn

        if kind == "colx":
            c = pref[0]
            pl.when(c == 0)(lambda: finish(gp[:, 0:sh]))
            pl.when(c == 1)(lambda: finish(gp[:, _HALF:sh + _HALF]))
        else:
            finish(gp)

    t_spec = pl.BlockSpec((None, tr, U[1]), lambda i, pref: (pref[1], i, 0))
    r_spec = pl.BlockSpec((3, tr, U[1]), lambda i, pref: (0, i, 0))
    s_spec = pl.BlockSpec((tr, C), lambda i, pref: (i, 0))
    return pl.pallas_call(
        body, grid_spec=pltpu.PrefetchScalarGridSpec(num_scalar_prefetch=1, grid=(R // tr,), in_specs=[t_spec, r_spec, s_spec, s_spec, s_spec],
                                                     out_specs=[s_spec] * 4),
        out_shape=[SDS((R, C), F32)] * 4, name=name)(place, part, recv, w, m, v)


_SMALL = (("ln_mix", "ln_mix_w", D_MODEL), ("ln_cross", "ln_cross_w", D_MODEL), ("ln_mem", "ln_mem_w", D_MODEL),
          ("ln_ffn", "ln_ffn_w", D_MODEL), ("ln_final", "ln_final_w", D_MODEL), ("hg_lb", "hg_lower_bounds", 2 * HG_WIDTH),
          ("hg_norm", "hg_norm_w", HEAD_DIM))
_PACK = sum(n for _, _, n in _SMALL) + LANE


def _small_sync(grow, wrow, mrow, vrow):
    def body(g_ref, w_ref, m_ref, v_ref, o_ref, gath, ssem, rsem):
        x, y, c, _ = _my_place()
        me = 4 * x + 2 * y + c
        gath[pl.ds(me, 1), :] = g_ref[...]
        cps = []
        for dlt in range(1, N_DEV):
            peer = (me + dlt) % N_DEV
            cp = pltpu.make_async_remote_copy(
                src_ref=g_ref, dst_ref=gath.at[pl.ds(me, 1), :], send_sem=ssem.at[peer], recv_sem=rsem.at[me],
                device_id=(peer // 4, (peer // 2) % 2, peer % 2), device_id_type=MESH)
            cp.start()
            cps.append(cp)
        for dlt in range(1, N_DEV):
            src = (me + dlt) % N_DEV
            pltpu.make_async_remote_copy(
                src_ref=g_ref, dst_ref=gath.at[pl.ds(src, 1), :], send_sem=ssem.at[src], recv_sem=rsem.at[src],
                device_id=(x, y, c), device_id_type=MESH).wait_recv()
        for cp in cps:
            cp.wait_send()
        g = gath[0:1, :]
        for j in range(1, N_DEV):
            g = g + gath[j:j + 1, :]
        d, mn, vn = _adamw(w_ref[...], g, m_ref[...], v_ref[...])
        o_ref[0:1, :] = g
        o_ref[1:2, :] = d
        o_ref[2:3, :] = mn
        o_ref[3:4, :] = vn

    vm = pl.BlockSpec(memory_space=pltpu.VMEM)
    return pl.pallas_call(
        body, in_specs=[vm] * 4, out_specs=vm, out_shape=SDS((4, _PACK), F32),
        scratch_shapes=[pltpu.VMEM((N_DEV, _PACK), F32), pltpu.SemaphoreType.DMA((N_DEV,)), pltpu.SemaphoreType.DMA((N_DEV,))],
        name="small_sync")(grow, wrow, mrow, vrow)


def _pack_small(d, loss_row=None):
    parts = [d[k].reshape(1, n).astype(F32) for k, _, n in _SMALL]
    parts.append(loss_row if loss_row is not None else jnp.ones((1, LANE), F32))
    return jnp.concatenate(parts, axis=1)


def kernel(x, mem, ln_mix_w, w_in, hg_norm_w, hg_lower_bounds, w_branch_a, w_branch_b, w_out, ln_cross_w, ln_mem_w, wq_cross, wkv_cross, wo_cross, ln_ffn_w, w1, w3, w2, ln_final_w, loss_target, m_ln_mix_w, m_w_in, m_hg_norm_w, m_hg_lower_bounds, m_w_branch_a, m_w_branch_b, m_w_out, m_ln_cross_w, m_ln_mem_w, m_wq_cross, m_wkv_cross, m_wo_cross, m_ln_ffn_w, m_w1, m_w3, m_w2, m_ln_final_w, v_ln_mix_w, v_w_in, v_hg_norm_w, v_hg_lower_bounds, v_w_branch_a, v_w_branch_b, v_w_out, v_ln_cross_w, v_ln_mem_w, v_wq_cross, v_wkv_cross, v_wo_cross, v_ln_ffn_w, v_w1, v_w3, v_w2, v_ln_final_w):
    given = dict(locals())
    place = jnp.stack([lax.axis_index("c"), 2 * lax.axis_index("x") + lax.axis_index("y")]).astype(jnp.int32)

    shard = {key: given[ref][0] for key, ref, _, _, _, _ in _MATS}
    xkeys = [key for key, _, _, _, kind, _ in _MATS if kind == "colx"]
    heads = dict(zip(xkeys, _straddle_exchange([shard[k] for k in xkeys])))
    index = {key: i for i, (key, *_) in enumerate(_MATS)}

    def unit(key):
        kind = _MATS[index[key]][4]
        if kind == "colx":
            return _cast_unit_x(shard[key], heads[key], "cast_" + key)
        return _cast_unit(shard[key], "cast_" + key)

    first, rest = ["w_in"], ["w_ba", "w_bb", "w_out", "wq", "wkv", "wo", "w1", "w3", "w2"]
    full_first = _all_gather([unit(k) for k in first], [index[k] for k in first], 1, "all_gather_in")
    full_rest = _all_gather([unit(k) for k in rest], [index[k] for k in rest], 2, "all_gather_rest", after=full_first[0])
    p = dict(zip(first + rest, list(full_first) + list(full_rest)))
    small_w = {"ln_mix": ln_mix_w, "ln_cross": ln_cross_w, "ln_mem": ln_mem_w, "ln_ffn": ln_ffn_w,
               "ln_final": ln_final_w.reshape(1, D_MODEL), "hg_lb": hg_lower_bounds, "hg_norm": hg_norm_w}
    p.update(small_w)

    groups = {"ffn": ["w2", "w1", "w3"], "cross": ["wo", "wq", "wkv"], "mix": ["w_out", "w_ba", "w_bb"], "in": ["w_in"]}
    cids = {"ffn": (3, 4), "cross": (5, 6), "mix": (7, 8), "in": (9, 10)}
    stage_order = list(groups)
    sib, parts, recv, out = {}, {}, {}, {}

    def chip_stage(name, gw):
        keys = groups[name]
        parts[name] = [_chip_sum(gw[k], s, _MATS[index[k]][4], _MATS[index[k]][5], place, "chip_sum_" + k)
                       for k, s in zip(keys, sib[name])]
        recv[name] = _rs_chips(parts[name], [index[k] for k in keys], cids[name][1], "rs_chips_" + name)

    def adam_stage(name):
        for k, t, r in zip(groups[name], parts[name], recv[name]):
            _, ref, _, _, kind, sh = _MATS[index[k]]
            res = _adam_mat(t, r, shard[k], given["m_" + ref][0], given["v_" + ref][0], kind, sh, place, "adam_" + k)
            out[ref] = [a[None] for a in res]

    def ready(name, gw):
        keys = groups[name]
        sib[name] = _rs_sibling([gw[k] for k in keys], [index[k] for k in keys], cids[name][0], "rs_sibling_" + name)
        at = stage_order.index(name)
        if at >= 1:
            chip_stage(stage_order[at - 1], gw)
        if at >= 2:
            adam_stage(stage_order[at - 2])
        if name == stage_order[-1]:
            chip_stage(name, gw)
            adam_stage(stage_order[-2])

    loss_row, grad_x, gs = _local_step(x[0], mem[0], loss_target[0], p, ready)

    sm = {k: given["m_" + ref] for k, ref, _ in _SMALL}
    sv = {k: given["v_" + ref] for k, ref, _ in _SMALL}
    synced = _small_sync(_pack_small(gs, loss_row), _pack_small(small_w), _pack_small(sm), _pack_small(sv))
    adam_stage("in")
    off = 0
    for k, ref, n in _SMALL:
        shape = given[ref].shape
        out[ref] = [synced[r, off:off + n].reshape(shape) for r in range(4)]
        off += n
    loss = synced[0, off]

    order = ["ln_mix_w", "w_in", "hg_norm_w", "hg_lower_bounds", "w_branch_a", "w_branch_b", "w_out", "ln_cross_w", "ln_mem_w",
             "wq_cross", "wkv_cross", "wo_cross", "ln_ffn_w", "w1", "w3", "w2", "ln_final_w"]
    return (loss, grad_x[None], *[out[n][0] for n in order], *[out[n][1] for n in order],
            *[out[n][2] for n in order], *[out[n][3] for n in order])
```

```python
import functools
import math

import jax
import jax.numpy as jnp
from jax import lax
from jax.experimental import pallas as pl
from jax.experimental.pallas import tpu as pltpu
from jax.experimental.pallas import tpu_sc as plsc

F32 = jnp.float32
BF = jnp.bfloat16
SDS = jax.ShapeDtypeStruct

D_MODEL = 2048
SEQ = 2048
HEAD_DIM = 128
MEM_LEN = 256
ATT_GROUPS = ((128, 1), (512, 4), (2048, 16))
ATT_HEADS = 4
ATT_WIDTH = 1536
ATT_OUT = 512
HG_HEADS = 8
HG_WIDTH = 1024
HG_CHUNK = 64
IN_WIDTH = 12800
CROSS_HEADS = 4
CROSS_WIDTH = 512
D_FF = 5632
RMS_EPS = 1e-6
ADAM_LR = 0.001
ADAM_B1 = 0.9
ADAM_B2 = 0.999
ADAM_EPS = 1e-08
ADAM_WD = 0.01
ADAM_STEP = 10
N_DEV = 8

LANE = 128
QA0, KA0, VA0 = 0, 12, 24
QH0, FH0, IH0, GH0 = 36, 44, 52, 60
GA0, GB0 = 68, 84

VMEM_BIG = 56 * 1024 * 1024
MESH = pl.DeviceIdType.MESH


def _cp(vmem=None, **kw):
    if vmem is not None:
        kw["vmem_limit_bytes"] = vmem
    return pltpu.CompilerParams(**kw)


def _pick(n, cands):
    for c in cands:
        if n % c == 0:
            return c
    raise ValueError(f"no tile for {n}")


def _mm(a, b, form, out_dtype, name, res=None, tiles=None):
    if form == "nn":
        (M, K), (K2, N) = a.shape, b.shape
    elif form == "nt":
        (M, K), (N, K2) = a.shape, b.shape
    else:
        (K, M), (K2, N) = a.shape, b.shape
    assert K == K2, (a.shape, b.shape, form)
    if tiles is None:
        tm = _pick(M, (1024, 1408, 512, 256))
        tn = _pick(N, (1280, 1024, 1408, 512, 256))
        tk = K if K <= 2048 else _pick(K, (2560, 1408))
    else:
        tm, tn, tk = tiles
    nk = K // tk
    if form == "nn":
        a_spec = pl.BlockSpec((tm, tk), lambda i, j, k: (i, k))
        b_spec = pl.BlockSpec((tk, tn), lambda i, j, k: (k, j))
        dims = ((1,), (0,))
    elif form == "nt":
        a_spec = pl.BlockSpec((tm, tk), lambda i, j, k: (i, k))
        b_spec = pl.BlockSpec((tn, tk), lambda i, j, k: (j, k))
        dims = ((1,), (1,))
    else:
        a_spec = pl.BlockSpec((tk, tm), lambda i, j, k: (k, i))
        b_spec = pl.BlockSpec((tk, tn), lambda i, j, k: (k, j))
        dims = ((0,), (0,))
    o_spec = pl.BlockSpec((tm, tn), lambda i, j, k: (i, j))
    in_specs = [a_spec, b_spec]
    args = [a, b]
    if res is not None:
        in_specs.append(o_spec)
        args.append(res)

    def body(*refs):
        a_ref, b_ref = refs[0], refs[1]
        r_ref = refs[2] if res is not None else None
        o_ref = refs[3] if res is not None else refs[2]
        acc = refs[-1] if nk > 1 else None
        k = pl.program_id(2)
        d = lax.dot_general(a_ref[...].astype(BF), b_ref[...].astype(BF), (dims, ((), ())), preferred_element_type=F32)

        def finish(r):
            if res is not None:
                r = r + r_ref[...].astype(F32)
            o_ref[...] = r.astype(o_ref.dtype)

        if nk == 1:
            finish(d)
        else:
            @pl.when(k == 0)
            def _():
                acc[...] = d

            @pl.when((k > 0) & (k < nk - 1))
            def _():
                acc[...] += d

            @pl.when(k == nk - 1)
            def _():
                finish(acc[...] + d)

    return pl.pallas_call(
        body, grid=(M // tm, N // tn, nk), in_specs=in_specs, out_specs=o_spec,
        out_shape=SDS((M, N), out_dtype), scratch_shapes=[pltpu.VMEM((tm, tn), F32)] if nk > 1 else [],
        compiler_params=_cp(VMEM_BIG, dimension_semantics=("parallel", "parallel", "arbitrary")),
        name=name)(*args)


def _rms(x, w):
    return x * lax.rsqrt(jnp.mean(x * x, axis=-1, keepdims=True) + RMS_EPS) * w


def _rms_fwd(x, w, name):
    R, D = x.shape
    tr = 256

    def body(x_ref, w_ref, o_ref):
        o_ref[...] = _rms(x_ref[...], w_ref[...]).astype(o_ref.dtype)

    return pl.pallas_call(
        body, grid=(R // tr,),
        in_specs=[pl.BlockSpec((tr, D), lambda i: (i, 0)), pl.BlockSpec((1, D), lambda i: (0, 0))],
        out_specs=pl.BlockSpec((tr, D), lambda i: (i, 0)), out_shape=SDS((R, D), BF), name=name)(x, w)


def _rms_bwd(x, w, dh, dres, name):
    R, D = x.shape
    tr = 256
    want_dx = dres is not None

    def body(*refs):
        if want_dx:
            x_ref, w_ref, dh_ref, dr_ref, dx_ref, dxb_ref, dw_ref = refs
        else:
            x_ref, w_ref, dh_ref, dw_ref = refs

        @pl.when(pl.program_id(0) == 0)
        def _():
            dw_ref[...] = jnp.zeros_like(dw_ref)

        _, vjp = jax.vjp(_rms, x_ref[...], w_ref[...])
        dx, dw = vjp(dh_ref[...].astype(F32))
        dw_ref[...] += dw
        if want_dx:
            dx = dx + dr_ref[...]
            dx_ref[...] = dx
            dxb_ref[...] = dx.astype(BF)

    row = pl.BlockSpec((tr, D), lambda i: (i, 0))
    vec = pl.BlockSpec((1, D), lambda i: (0, 0))
    if want_dx:
        return pl.pallas_call(body, grid=(R // tr,), in_specs=[row, vec, row, row], out_specs=[row, row, vec],
                              out_shape=[SDS((R, D), F32), SDS((R, D), BF), SDS((1, D), F32)], name=name)(x, w, dh, dres)
    return pl.pallas_call(body, grid=(R // tr,), in_specs=[row, vec, row], out_specs=vec,
                          out_shape=SDS((1, D), F32), name=name)(x, w, dh)


def _final_loss(x3, w, tgt):
    R, D = x3.shape
    tr = 256

    def loss_fn(xv, wv, tv):
        err = _rms(xv, wv) - tv
        return 0.5 * jnp.sum(jnp.mean(err * err, axis=-1))

    def body(x_ref, w_ref, t_ref, loss_ref, dx_ref, dxb_ref, dw_ref):
        @pl.when(pl.program_id(0) == 0)
        def _():
            dw_ref[...] = jnp.zeros_like(dw_ref)
            loss_ref[...] = jnp.zeros_like(loss_ref)

        tv = t_ref[...]
        val, vjp = jax.vjp(lambda a, b: loss_fn(a, b, tv), x_ref[...], w_ref[...])
        dx, dw = vjp(jnp.ones((), F32))
        dx_ref[...] = dx
        dxb_ref[...] = dx.astype(BF)
        dw_ref[...] += dw
        loss_ref[...] += jnp.full(loss_ref.shape, val, F32)

    row = pl.BlockSpec((tr, D), lambda i: (i, 0))
    vec = pl.BlockSpec((1, D), lambda i: (0, 0))
    one = pl.BlockSpec((1, LANE), lambda i: (0, 0))
    return pl.pallas_call(body, grid=(R // tr,), in_specs=[row, vec, row], out_specs=[one, row, row, vec],
                          out_shape=[SDS((1, LANE), F32), SDS((R, D), F32), SDS((R, D), BF), SDS((1, D), F32)],
                          name="final_loss")(x3, w, tgt)


NEG = -1e30


def _att_blocks(S, d):
    L = S // d
    nb = L // LANE
    return nb, (2 if nb > 1 else 1)


def _att_rows(S, d, idx):
    nb, nkb = _att_blocks(S, d)
    r = idx // nb
    n = idx % nb
    kb = jnp.maximum(n - 1, 0)
    if d == 1:
        qrows = pl.ds(pl.multiple_of(n * LANE, LANE), LANE)
        krows = pl.ds(pl.multiple_of(kb * LANE, LANE), LANE * nkb)
    else:
        qrows = pl.ds(r + n * (LANE * d), LANE, stride=d)
        krows = pl.ds(r + kb * (LANE * d), LANE * nkb, stride=d)
    qpos = n * LANE + lax.broadcasted_iota(jnp.int32, (LANE, LANE * nkb), 0)
    kpos = kb * LANE + lax.broadcasted_iota(jnp.int32, (LANE, LANE * nkb), 1)
    return qrows, krows, qpos - kpos


def _slab(col):
    return pl.BlockSpec((SEQ, LANE), lambda h, col=col: (0, col + h))


def _attn_fwd(proj):
    S = proj.shape[0]
    scale = HEAD_DIM ** -0.5

    def body(q0, q1, q2, k0, k1, k2, v0, v1, v2, o_ref, lse_ref, og, lg):
        qs, ks, vs = (q0, q1, q2), (k0, k1, k2), (v0, v1, v2)
        for g, (window, d) in enumerate(ATT_GROUPS):
            def blk(idx, carry, g=g, window=window, d=d):
                qrows, krows, dist = _att_rows(S, d, idx)
                q = qs[g][qrows, :].astype(BF)
                k = ks[g][krows, :].astype(BF)
                v = vs[g][krows, :].astype(BF)
                s = lax.dot_general(q, k, (((1,), (1,)), ((), ())), preferred_element_type=F32) * scale
                s = jnp.where((dist >= 0) & (dist <= window // d), s, NEG)
                m = jnp.max(s, axis=-1, keepdims=True)
                p = jnp.exp(s - m)
                l = jnp.sum(p, axis=-1, keepdims=True)
                o = jnp.dot((p / l).astype(BF), v, preferred_element_type=F32)
                og[g, qrows, :] = o
                lg[g, qrows, :] = jnp.broadcast_to(m + jnp.log(l), (LANE, LANE))
                return carry
            lax.fori_loop(0, S // LANE, blk, 0)

        def merge(t, carry):
            rows = pl.ds(pl.multiple_of(t * 256, 256), 256)
            l0, l1, l2 = lg[0, rows, :], lg[1, rows, :], lg[2, rows, :]
            m = jnp.maximum(jnp.maximum(l0, l1), l2)
            w0, w1, w2 = jnp.exp(l0 - m), jnp.exp(l1 - m), jnp.exp(l2 - m)
            den = w0 + w1 + w2
            o_ref[rows, :] = (w0 * og[0, rows, :] + w1 * og[1, rows, :] + w2 * og[2, rows, :]) / den
            lse_ref[rows, :] = m + jnp.log(den)
            return carry
        lax.fori_loop(0, S // 256, merge, 0)

    ins = [_slab(QA0 + 4 * g) for g in range(3)] + [_slab(KA0 + 4 * g) for g in range(3)] + [_slab(VA0 + 4 * g) for g in range(3)]
    out = pl.BlockSpec((SEQ, LANE), lambda h: (0, h))
    return pl.pallas_call(
        body, grid=(ATT_HEADS,), in_specs=ins, out_specs=[out, out],
        out_shape=[SDS((S, ATT_OUT), F32), SDS((S, ATT_OUT), F32)],
        scratch_shapes=[pltpu.VMEM((3, S, LANE), F32), pltpu.VMEM((3, S, LANE), F32)],
        compiler_params=_cp(VMEM_BIG), name="attn_fwd")(*([proj] * 9))


def _attn_bwd(proj, o_att, lse, do_att):
    S = proj.shape[0]
    scale = HEAD_DIM ** -0.5

    def body(q0, q1, q2, k0, k1, k2, v0, v1, v2, o_ref, lse_ref, do_ref,
             dq0, dq1, dq2, dk0, dk1, dk2, dv0, dv1, dv2, dqa, dka, dva, delta):
        qs, ks, vs = (q0, q1, q2), (k0, k1, k2), (v0, v1, v2)
        dqs, dks, dvs = (dq0, dq1, dq2), (dk0, dk1, dk2), (dv0, dv1, dv2)
        delta[...] = jnp.broadcast_to(jnp.sum(do_ref[...] * o_ref[...], axis=-1, keepdims=True), delta.shape)
        for g, (window, d) in enumerate(ATT_GROUPS):
            dka[...] = jnp.zeros_like(dka)
            dva[...] = jnp.zeros_like(dva)

            def blk(idx, carry, g=g, window=window, d=d):
                qrows, krows, dist = _att_rows(S, d, idx)
                q = qs[g][qrows, :].astype(BF)
                k = ks[g][krows, :].astype(BF)
                v = vs[g][krows, :].astype(BF)
                do = do_ref[qrows, :]
                s = lax.dot_general(q, k, (((1,), (1,)), ((), ())), preferred_element_type=F32) * scale
                p = jnp.exp(s - lse_ref[qrows, :][:, 0:1])
                p = jnp.where((dist >= 0) & (dist <= window // d), p, 0.0)
                dob = do.astype(BF)
                dva[krows, :] += lax.dot_general(p.astype(BF), dob, (((0,), (0,)), ((), ())), preferred_element_type=F32)
                dp = lax.dot_general(dob, v, (((1,), (1,)), ((), ())), preferred_element_type=F32)
                ds = (p * (dp - delta[qrows, :][:, 0:1]) * scale).astype(BF)
                dqa[qrows, :] = jnp.dot(ds, k, preferred_element_type=F32)
                dka[krows, :] += lax.dot_general(ds, q, (((0,), (0,)), ((), ())), preferred_element_type=F32)
                return carry
            lax.fori_loop(0, S // LANE, blk, 0)
            dqs[g][...] = dqa[...].astype(dqs[g].dtype)
            dks[g][...] = dka[...].astype(dks[g].dtype)
            dvs[g][...] = dva[...].astype(dvs[g].dtype)

    cols = [QA0 + 4 * g for g in range(3)] + [KA0 + 4 * g for g in range(3)] + [VA0 + 4 * g for g in range(3)]
    ins = [_slab(c) for c in cols]
    one = pl.BlockSpec((SEQ, LANE), lambda h: (0, h))
    return pl.pallas_call(
        body, grid=(ATT_HEADS,), in_specs=ins + [one, one, one], out_specs=[one] * 9,
        out_shape=[SDS((S, ATT_OUT), BF)] * 9,
        scratch_shapes=[pltpu.VMEM((S, LANE), F32)] * 4,
        compiler_params=_cp(VMEM_BIG), name="attn_bwd")(*([proj] * 9), o_att, lse, do_att)


def _bdot(a, b, dims):
    return lax.dot_general(a.astype(BF), b.astype(BF), (dims, ((), ())), preferred_element_type=F32)


def _hg_chunk(qh, fh, ih, gh, lbraw, nw, st):
    C = HG_CHUNK
    mx = jnp.maximum(lbraw[0:1], lbraw[1:2])
    e0 = jnp.exp(lbraw[0:1] - mx)
    e1 = jnp.exp(lbraw[1:2] - mx)
    lb = e0 / (e0 + e1)
    f = lb + (1.0 - lb) * jax.nn.sigmoid(fh)
    logf = jnp.log(f)
    k = 1.0 - f
    q = qh * jax.nn.sigmoid(qh)
    causal = lax.broadcasted_iota(jnp.int32, (C, C), 0) >= lax.broadcasted_iota(jnp.int32, (C, C), 1)
    b = lax.dot_general(causal.astype(F32), logf, (((1,), (0,)), ((), ())), precision=lax.Precision.HIGHEST,
                        preferred_element_type=F32)
    b_last = b[C - 1:C]
    st_out = st * jnp.exp(b_last) + _bdot(ih, k * jnp.exp(b_last - b), ((0,), (0,)))
    inter = _bdot(q * jnp.exp(b), st, ((1,), (1,)))
    b_ref = b[C // 2:C // 2 + 1]
    a = _bdot(q * jnp.exp(b - b_ref), k * jnp.exp(b_ref - b), ((1,), (1,)))
    a = jnp.where(causal, a, 0.0)
    o = inter + _bdot(a, ih, ((1,), (0,)))
    o = o * lax.rsqrt(jnp.mean(o * o, axis=-1, keepdims=True) + RMS_EPS) * nw
    return o * (gh * jax.nn.sigmoid(gh)), st_out


def _head_cols(h):
    return slice(h * LANE, (h + 1) * LANE)


def _hg_in_specs(chunk_of):
    half = 4 * LANE
    specs = []
    for base in (QH0, FH0, IH0, GH0):
        for part in range(2):
            specs.append(pl.BlockSpec((HG_CHUNK, half), lambda n, col=base // 4 + part: (chunk_of(n), col)))
    return specs


def _hg_fwd(proj, lb, nw):
    S = proj.shape[0]
    C = HG_CHUNK
    NC = S // C

    def body(*refs):
        sec = refs[:8]
        lb_ref, nw_ref, o_ref, st_ref, st_scr = refs[8:]

        @pl.when(pl.program_id(0) == 0)
        def _():
            st_scr[...] = jnp.zeros_like(st_scr)

        for h in range(HG_HEADS):
            half, hs = h // 4, _head_cols(h % 4)
            st = st_scr[h]
            st_ref[h, 0] = st
            o, st_out = _hg_chunk(sec[half][:, hs], sec[2 + half][:, hs], sec[4 + half][:, hs], sec[6 + half][:, hs],
                                  lb_ref[:, _head_cols(h)], nw_ref[...], st)
            o_ref[:, _head_cols(h)] = o.astype(o_ref.dtype)
            st_scr[h] = st_out

    return pl.pallas_call(
        body, grid=(NC,),
        in_specs=_hg_in_specs(lambda n: n) + [pl.BlockSpec((2, HG_WIDTH), lambda n: (0, 0)), pl.BlockSpec((1, LANE), lambda n: (0, 0))],
        out_specs=[pl.BlockSpec((C, HG_WIDTH), lambda n: (n, 0)), pl.BlockSpec((HG_HEADS, 1, LANE, LANE), lambda n: (0, n, 0, 0))],
        out_shape=[SDS((S, HG_WIDTH), BF), SDS((HG_HEADS, NC, LANE, LANE), F32)],
        scratch_shapes=[pltpu.VMEM((HG_HEADS, LANE, LANE), F32)], name="hg_fwd")(*([proj] * 8), lb, nw)


def _hg_bwd(proj, lb, nw, states, do_hg):
    S = proj.shape[0]
    C = HG_CHUNK
    NC = S // C

    def body(*refs):
        sec = refs[:8]
        lb_ref, nw_ref, st_ref, do_ref, dq_ref, df_ref, di_ref, dg_ref, dlb_ref, dnw_ref, dst_scr = refs[8:]

        @pl.when(pl.program_id(0) == 0)
        def _():
            dst_scr[...] = jnp.zeros_like(dst_scr)
            dlb_ref[...] = jnp.zeros_like(dlb_ref)
            dnw_ref[...] = jnp.zeros_like(dnw_ref)

        dnw_sum = jnp.zeros((1, LANE), F32)
        for h in range(HG_HEADS):
            half, hs, hc = h // 4, _head_cols(h % 4), _head_cols(h)
            _, vjp = jax.vjp(_hg_chunk, sec[half][:, hs], sec[2 + half][:, hs], sec[4 + half][:, hs], sec[6 + half][:, hs],
                             lb_ref[:, hc], nw_ref[...], st_ref[h, 0])
            dq, df, di, dg, dlb, dnw, dst = vjp((do_ref[:, hc].astype(F32), dst_scr[h]))
            dq_ref[:, hc] = dq.astype(dq_ref.dtype)
            df_ref[:, hc] = df.astype(df_ref.dtype)
            di_ref[:, hc] = di.astype(di_ref.dtype)
            dg_ref[:, hc] = dg.astype(dg_ref.dtype)
            dlb_ref[:, hc] += dlb
            dnw_sum = dnw_sum + dnw
            dst_scr[h] = dst
        dnw_ref[...] += dnw_sum

    rev = lambda n: NC - 1 - n
    ospec = pl.BlockSpec((C, HG_WIDTH), lambda n: (rev(n), 0))
    lbs = pl.BlockSpec((2, HG_WIDTH), lambda n: (0, 0))
    nws = pl.BlockSpec((1, LANE), lambda n: (0, 0))
    outs = pl.pallas_call(
        body, grid=(NC,),
        in_specs=_hg_in_specs(rev) + [lbs, nws, pl.BlockSpec((HG_HEADS, 1, LANE, LANE), lambda n: (0, rev(n), 0, 0)), ospec],
        out_specs=[ospec] * 4 + [lbs, nws],
        out_shape=[SDS((S, HG_WIDTH), BF)] * 4 + [SDS((2, HG_WIDTH), F32), SDS((1, LANE), F32)],
        scratch_shapes=[pltpu.VMEM((HG_HEADS, LANE, LANE), F32)], name="hg_bwd")(*([proj] * 8), lb, nw, states, do_hg)
    return outs


def _merge(ga, gb, ba, bb):
    return jax.nn.sigmoid(ga) * ba + jax.nn.sigmoid(gb) * bb


_MT = 512


def _gate_specs():
    ga = pl.BlockSpec((_MT, _MT), lambda i, j: (i, GA0 * LANE // _MT + j))
    gb = pl.BlockSpec((_MT, _MT), lambda i, j: (i, GB0 * LANE // _MT + j))
    t = pl.BlockSpec((_MT, _MT), lambda i, j: (i, j))
    return ga, gb, t


def _merge_fwd(proj, ba, bb):
    S, D = ba.shape
    ga, gb, t = _gate_specs()

    def body(ga_ref, gb_ref, ba_ref, bb_ref, o_ref):
        o_ref[...] = _merge(ga_ref[...], gb_ref[...], ba_ref[...], bb_ref[...]).astype(o_ref.dtype)

    return pl.pallas_call(body, grid=(S // _MT, D // _MT), in_specs=[ga, gb, t, t], out_specs=t,
                          out_shape=SDS((S, D), BF), name="merge_fwd")(proj, proj, ba, bb)


def _merge_bwd(proj, ba, bb, dm):
    S, D = ba.shape
    ga, gb, t = _gate_specs()

    def body(ga_ref, gb_ref, ba_ref, bb_ref, dm_ref, dga_ref, dgb_ref, dba_ref, dbb_ref):
        _, vjp = jax.vjp(_merge, ga_ref[...], gb_ref[...], ba_ref[...], bb_ref[...])
        dga, dgb, dba, dbb = vjp(dm_ref[...])
        dga_ref[...] = dga.astype(BF)
        dgb_ref[...] = dgb.astype(BF)
        dba_ref[...] = dba.astype(BF)
        dbb_ref[...] = dbb.astype(BF)

    return pl.pallas_call(body, grid=(S // _MT, D // _MT), in_specs=[ga, gb, t, t, t], out_specs=[t] * 4,
                          out_shape=[SDS((S, D), BF)] * 4, name="merge_bwd")(proj, proj, ba, bb, dm)


def _swiglu(a, b):
    return a * jax.nn.sigmoid(a) * b


def _swiglu_fwd(a, b):
    S, F = a.shape
    tf = _pick(F, (1408, 512))
    t = pl.BlockSpec((_MT, tf), lambda i, j: (i, j))

    def body(a_ref, b_ref, o_ref):
        o_ref[...] = _swiglu(a_ref[...], b_ref[...]).astype(o_ref.dtype)

    return pl.pallas_call(body, grid=(S // _MT, F // tf), in_specs=[t, t], out_specs=t,
                          out_shape=SDS((S, F), BF), name="swiglu_fwd")(a, b)


def _swiglu_bwd(a, b, dg):
    S, F = a.shape
    tf = _pick(F, (1408, 512))
    t = pl.BlockSpec((_MT, tf), lambda i, j: (i, j))

    def body(a_ref, b_ref, dg_ref, da_ref, db_ref):
        _, vjp = jax.vjp(_swiglu, a_ref[...], b_ref[...])
        da, db = vjp(dg_ref[...].astype(F32))
        da_ref[...] = da.astype(BF)
        db_ref[...] = db.astype(BF)

    return pl.pallas_call(body, grid=(S // _MT, F // tf), in_specs=[t, t, t], out_specs=[t, t],
                          out_shape=[SDS((S, F), BF)] * 2, name="swiglu_bwd")(a, b, dg)


def _cross(q, k, v):
    s = _bdot(q, k, ((1,), (1,))) * (HEAD_DIM ** -0.5)
    e = jnp.exp(s - jnp.max(s, axis=-1, keepdims=True))
    p = e / jnp.sum(e, axis=-1, keepdims=True)
    return _bdot(p, v, ((1,), (0,)))


def _cross_specs():
    q = pl.BlockSpec((_MT, LANE), lambda h, i: (i, h))
    k = pl.BlockSpec((MEM_LEN, LANE), lambda h, i: (0, h))
    v = pl.BlockSpec((MEM_LEN, LANE), lambda h, i: (0, CROSS_HEADS + h))
    return q, k, v


def _cross_fwd(qc, kvc):
    S = qc.shape[0]
    q, k, v = _cross_specs()

    def body(q_ref, k_ref, v_ref, o_ref):
        o_ref[...] = _cross(q_ref[...], k_ref[...], v_ref[...]).astype(o_ref.dtype)

    return pl.pallas_call(body, grid=(CROSS_HEADS, S // _MT), in_specs=[q, k, v], out_specs=q,
                          out_shape=SDS((S, CROSS_WIDTH), BF), name="cross_fwd")(qc, kvc, kvc)


def _cross_bwd(qc, kvc, doc):
    S = qc.shape[0]
    q, k, v = _cross_specs()

    def body(q_ref, k_ref, v_ref, do_ref, dq_ref, dk_ref, dv_ref):
        @pl.when(pl.program_id(1) == 0)
        def _():
            dk_ref[...] = jnp.zeros_like(dk_ref)
            dv_ref[...] = jnp.zeros_like(dv_ref)

        _, vjp = jax.vjp(_cross, q_ref[...], k_ref[...], v_ref[...])
        dq, dk, dv = vjp(do_ref[...])
        dq_ref[...] = dq.astype(dq_ref.dtype)
        dk_ref[...] += dk
        dv_ref[...] += dv

    return pl.pallas_call(body, grid=(CROSS_HEADS, S // _MT), in_specs=[q, k, v, q], out_specs=[q, k, k],
                          out_shape=[SDS((S, CROSS_WIDTH), BF), SDS((MEM_LEN, CROSS_WIDTH), F32), SDS((MEM_LEN, CROSS_WIDTH), F32)],
                          name="cross_bwd")(qc, kvc, kvc, doc)


def _local_step(x, mem, tgt, p, ready):
    h = _rms_fwd(x, p["ln_mix"], "rms_mix")
    proj = _mm(h, p["w_in"], "nn", F32, "mm_proj")
    o_att, lse = _attn_fwd(proj)
    o_hg, states = _hg_fwd(proj, p["hg_lb"], p["hg_norm"])
    ba = _mm(o_att, p["w_ba"], "nn", F32, "mm_ba")
    bb = _mm(o_hg, p["w_bb"], "nn", F32, "mm_bb")
    merged = _merge_fwd(proj, ba, bb)
    x1 = _mm(merged, p["w_out"], "nn", F32, "mm_out", res=x)
    hc = _rms_fwd(x1, p["ln_cross"], "rms_cross")
    qc = _mm(hc, p["wq"], "nn", F32, "mm_q")
    mn = _rms_fwd(mem, p["ln_mem"], "rms_mem")
    kvc = _mm(mn, p["wkv"], "nn", F32, "mm_kv")
    oc = _cross_fwd(qc, kvc)
    x2 = _mm(oc, p["wo"], "nn", F32, "mm_o", res=x1)
    hf = _rms_fwd(x2, p["ln_ffn"], "rms_ffn")
    a = _mm(hf, p["w1"], "nn", F32, "mm_w1")
    b = _mm(hf, p["w3"], "nn", F32, "mm_w3")
    g = _swiglu_fwd(a, b)
    x3 = _mm(g, p["w2"], "nn", F32, "mm_w2", res=x2)
    loss, dx3, dx3b, d_ln_final = _final_loss(x3, p["ln_final"], tgt)

    gw, gs = {}, {"ln_final": d_ln_final}
    dg = _mm(dx3b, p["w2"], "nt", BF, "mm_dg")
    gw["w2"] = _mm(g, dx3b, "tn", BF, "mm_dw2")
    da, db = _swiglu_bwd(a, b, dg)
    dhf = _mm(da, p["w1"], "nt", F32, "mm_dhf1")
    dhf = _mm(db, p["w3"], "nt", F32, "mm_dhf3", res=dhf)
    gw["w1"] = _mm(hf, da, "tn", BF, "mm_dw1")
    gw["w3"] = _mm(hf, db, "tn", BF, "mm_dw3")
    ready("ffn", gw)
    dx2, dx2b, gs["ln_ffn"] = _rms_bwd(x2, p["ln_ffn"], dhf, dx3, "rms_ffn_bwd")
    doc = _mm(dx2b, p["wo"], "nt", F32, "mm_doc")
    gw["wo"] = _mm(oc, dx2b, "tn", BF, "mm_dwo")
    dqc, dk, dv = _cross_bwd(qc, kvc, doc)
    dkvc = jnp.concatenate([dk, dv], axis=1)
    dhc = _mm(dqc, p["wq"], "nt", F32, "mm_dhc")
    gw["wq"] = _mm(hc, dqc, "tn", BF, "mm_dwq")
    dmn = _mm(dkvc, p["wkv"], "nt", F32, "mm_dmn")
    gw["wkv"] = _mm(mn, dkvc, "tn", BF, "mm_dwkv")
    ready("cross", gw)
    gs["ln_mem"] = _rms_bwd(mem, p["ln_mem"], dmn, None, "rms_mem_bwd")
    dx1, dx1b, gs["ln_cross"] = _rms_bwd(x1, p["ln_cross"], dhc, dx2, "rms_cross_bwd")
    dmerged = _mm(dx1b, p["w_out"], "nt", F32, "mm_dmerged")
    gw["w_out"] = _mm(merged, dx1b, "tn", BF, "mm_dwout")
    dga, dgb, dba, dbb = _merge_bwd(proj, ba, bb, dmerged)
    do_att = _mm(dba, p["w_ba"], "nt", F32, "mm_doatt")
    gw["w_ba"] = _mm(o_att, dba, "tn", BF, "mm_dwba")
    do_hg = _mm(dbb, p["w_bb"], "nt", F32, "mm_dohg")
    gw["w_bb"] = _mm(o_hg, dbb, "tn", BF, "mm_dwbb")
    ready("mix", gw)
    datt = _attn_bwd(proj, o_att, lse, do_att)
    dqh, dfh, dih, dgh, gs["hg_lb"], gs["hg_norm"] = _hg_bwd(proj, p["hg_lb"], p["hg_norm"], states, do_hg)
    dproj = jnp.concatenate([*datt, dqh, dfh, dih, dgh, dga, dgb], axis=1)
    dh = _mm(dproj, p["w_in"], "nt", F32, "mm_dh")
    gw["w_in"] = _mm(h, dproj, "tn", BF, "mm_dwin")
    ready("in", gw)
    grad_x, _, gs["ln_mix"] = _rms_bwd(x, p["ln_mix"], dh, dx1, "rms_mix_bwd")
    return loss, grad_x, gs


_HALF = LANE // 2
_MATS = (
    ("w_in", "w_in", D_MODEL, IN_WIDTH, "colx", IN_WIDTH // N_DEV),
    ("w_ba", "w_branch_a", ATT_OUT, D_MODEL, "col", D_MODEL // N_DEV),
    ("w_bb", "w_branch_b", HG_WIDTH, D_MODEL, "col", D_MODEL // N_DEV),
    ("w_out", "w_out", D_MODEL, D_MODEL, "row", D_MODEL // N_DEV),
    ("wq", "wq_cross", D_MODEL, CROSS_WIDTH, "row", D_MODEL // N_DEV),
    ("wkv", "wkv_cross", D_MODEL, 2 * CROSS_WIDTH, "row", D_MODEL // N_DEV),
    ("wo", "wo_cross", CROSS_WIDTH, D_MODEL, "col", D_MODEL // N_DEV),
    ("w1", "w1", D_MODEL, D_FF, "colx", D_FF // N_DEV),
    ("w3", "w3", D_MODEL, D_FF, "colx", D_FF // N_DEV),
    ("w2", "w2", D_FF, D_MODEL, "row", D_FF // N_DEV),
)
_NM = len(_MATS)
_ANY = pl.BlockSpec(memory_space=pl.ANY)


def _unit_shape(K, N, kind, sh):
    if kind == "row":
        return (sh, N)
    if kind == "col":
        return (K, sh)
    return (K, sh + _HALF)


def _ag_win(ref, kind, sh, chip, par):
    if kind == "row":
        return ref.at[pl.ds(pl.multiple_of((2 * chip + par) * sh, 16), sh), :]
    if kind == "col":
        return ref.at[:, pl.ds(pl.multiple_of((2 * chip + par) * sh, LANE), sh)]
    if par == 0:
        return ref.at[:, pl.ds(pl.multiple_of(2 * sh * chip, LANE), sh + _HALF)]
    return ref.at[:, pl.ds(pl.multiple_of(2 * sh * chip + sh + _HALF, LANE), sh - _HALF)]


def _ag_src(ref, kind, sh, par):
    if kind == "colx" and par == 1:
        return ref.at[:, pl.ds(0, sh - _HALF)]
    return ref


def _rs_win(ref, kind, sh, chip, par):
    if kind == "row":
        return ref.at[pl.ds(pl.multiple_of((2 * chip + par) * sh, 16), sh), :]
    if kind == "col":
        return ref.at[:, pl.ds(pl.multiple_of((2 * chip + par) * sh, LANE), sh)]
    return ref.at[:, pl.ds(pl.multiple_of(2 * sh * chip + par * (sh - _HALF), LANE), sh + _HALF)]


def _my_place():
    x, y, c = lax.axis_index("x"), lax.axis_index("y"), lax.axis_index("c")
    chips = [(1 - x, y), (x, 1 - y), (1 - x, 1 - y)]
    return x, y, c, chips


def _straddle_exchange(shards):
    n = len(shards)

    def body(*refs):
        ins, outs, ssem, rsem, lsem = refs[:n], refs[n:2 * n], refs[2 * n], refs[2 * n + 1], refs[2 * n + 2]
        x, y, c, _ = _my_place()

        @pl.when(c == 1)
        def _():
            cps = []
            for i in range(n):
                head = ins[i].at[:, pl.ds(0, LANE)]
                loc = pltpu.make_async_copy(head, outs[i], lsem.at[i])
                rem = pltpu.make_async_remote_copy(src_ref=head, dst_ref=outs[i], send_sem=ssem.at[i], recv_sem=rsem.at[i],
                                                   device_id=(x, y, 0), device_id_type=MESH)
                loc.start()
                rem.start()
                cps.append((loc, rem))
            for loc, rem in cps:
                loc.wait()
                rem.wait_send()

        @pl.when(c == 0)
        def _():
            for i in range(n):
                head = ins[i].at[:, pl.ds(0, LANE)]
                pltpu.make_async_remote_copy(src_ref=head, dst_ref=outs[i], send_sem=ssem.at[i], recv_sem=rsem.at[i],
                                             device_id=(x, y, 1), device_id_type=MESH).wait_recv()

    return pl.pallas_call(
        body, in_specs=[_ANY] * n, out_specs=[_ANY] * n,
        out_shape=[SDS((s.shape[0], LANE), F32) for s in shards],
        scratch_shapes=[pltpu.SemaphoreType.DMA((n,)), pltpu.SemaphoreType.DMA((n,)), pltpu.SemaphoreType.DMA((n,))],
        name="straddle_exchange")(*shards)


def _cast_unit(w, name):
    K, N = w.shape
    tr = _pick(K, (256, 352))

    def body(w_ref, o_ref):
        o_ref[...] = w_ref[...].astype(BF)

    blk = pl.BlockSpec((tr, N), lambda i: (i, 0))
    return pl.pallas_call(body, grid=(K // tr,), in_specs=[blk], out_specs=blk, out_shape=SDS((K, N), BF), name=name)(w)


def _cast_unit_x(w, head, name):
    K, sh = w.shape
    tr = 256

    def body(w_ref, h_ref, o_ref):
        c = lax.axis_index("c")

        @pl.when(c == 0)
        def _():
            o_ref[:, 0:sh] = w_ref[...].astype(BF)
            o_ref[:, sh:sh + _HALF] = h_ref[:, 0:_HALF].astype(BF)

        @pl.when(c == 1)
        def _():
            o_ref[:, 0:sh - _HALF] = w_ref[:, _HALF:sh].astype(BF)
            o_ref[:, sh - _HALF:sh + _HALF] = jnp.zeros((tr, LANE), BF)

    return pl.pallas_call(
        body, grid=(K // tr,),
        in_specs=[pl.BlockSpec((tr, sh), lambda i: (i, 0)), pl.BlockSpec((tr, LANE), lambda i: (i, 0))],
        out_specs=pl.BlockSpec((tr, sh + _HALF), lambda i: (i, 0)), out_shape=SDS((K, sh + _HALF), BF), name=name)(w, head)


def _handshake(peers):
    barrier = pltpu.get_barrier_semaphore()
    for peer in peers:
        pl.semaphore_signal(barrier, inc=1, device_id=peer, device_id_type=MESH)
    pl.semaphore_wait(barrier, len(peers))


def _sequencer(body, out_type, sems, cid, name):
    return pl.kernel(body, out_type=out_type, mesh=plsc.ScalarSubcoreMesh(axis_name="seq", num_cores=1),
                     scratch_types=sems, compiler_params=pltpu.CompilerParams(collective_id=cid), name=name)


def _all_gather(units, idx, cid, name, after=None):
    n = len(idx)

    def body(*refs):
        us, fulls = refs[:n], refs[-n - 3:-3]
        ssem, rsem, lsem = refs[-3:]
        x, y, c, chips = _my_place()
        _handshake([(x, y, 1 - c)] + [(*ch, c) for ch in chips])

        def gather(par):
            def win(i, chip_xy, p):
                _, _, _, _, kind, sh = _MATS[idx[i]]
                return _ag_win(fulls[i], kind, sh, 2 * chip_xy[0] + chip_xy[1], p)

            def rcopy(i, k, src, dst, to):
                return pltpu.make_async_remote_copy(src_ref=src, dst_ref=dst, send_sem=ssem.at[i, k], recv_sem=rsem.at[i, k],
                                                    device_id=to, device_id_type=MESH)

            sends, locs = [], []
            for i in range(n):
                _, _, _, _, kind, sh = _MATS[idx[i]]
                src = _ag_src(us[i], kind, sh, par)
                mine = win(i, (x, y), par)
                loc = pltpu.make_async_copy(src, mine, lsem.at[i])
                loc.start()
                locs.append(loc)
                first = [rcopy(i, 0, src, mine, (x, y, 1 - c))]
                first += [rcopy(i, 1 + j, src, mine, (*chips[j], c)) for j in range(3)]
                for cp in first:
                    cp.start()
                sends += first
            for i in range(n):
                for j in range(3):
                    blk = win(i, chips[j], par)
                    rcopy(i, 1 + j, blk, blk, (x, y, c)).wait_recv()
                    fwd = rcopy(i, 4 + j, blk, blk, (x, y, 1 - c))
                    fwd.start()
                    sends.append(fwd)
            for i in range(n):
                blk = win(i, (x, y), 1 - par)
                rcopy(i, 0, blk, blk, (x, y, c)).wait_recv()
                for j in range(3):
                    blk = win(i, chips[j], 1 - par)
                    rcopy(i, 4 + j, blk, blk, (x, y, c)).wait_recv()
            for cp in sends:
                cp.wait_send()
            for loc in locs:
                loc.wait()

        for par in (0, 1):
            pl.when(c == par)(functools.partial(gather, par))

    out_type = [SDS((_MATS[i][2], _MATS[i][3]), BF) for i in idx]
    sems = [pltpu.SemaphoreType.DMA((n, 7)), pltpu.SemaphoreType.DMA((n, 7)), pltpu.SemaphoreType.DMA((n,))]
    args = list(units) + ([after] if after is not None else [])
    return _sequencer(body, out_type, sems, cid, name)(*args)


def _rs_sibling(grads, idx, cid, name):
    n = len(idx)

    def body(*refs):
        gs, bufs = refs[:n], refs[n:2 * n]
        ssem, rsem = refs[2 * n:]
        x, y, c, _ = _my_place()
        _handshake([(x, y, 1 - c)])
        cps = []
        for i in range(n):
            _, _, _, _, kind, sh = _MATS[idx[i]]
            for k in range(4):
                cp = pltpu.make_async_remote_copy(
                    src_ref=_rs_win(gs[i], kind, sh, k, 1 - c), dst_ref=bufs[i].at[k], send_sem=ssem.at[i, k], recv_sem=rsem.at[i, k],
                    device_id=(x, y, 1 - c), device_id_type=MESH)
                cp.start()
                cps.append(cp)
        for cp in cps:
            cp.wait()

    out_type = [SDS((4, *_unit_shape(*_MATS[i][2:])), BF) for i in idx]
    sems = [pltpu.SemaphoreType.DMA((n, 4)), pltpu.SemaphoreType.DMA((n, 4))]
    return _sequencer(body, out_type, sems, cid, name)(*grads)


def _chip_sum(g, sib, kind, sh, place, name):
    K, N = g.shape
    U = _unit_shape(K, N, kind, sh)

    def body(pref, g_ref, s_ref, o_ref):
        o_ref[...] = (g_ref[...].astype(F32) + s_ref[...].astype(F32)).astype(BF)

    if kind == "row":
        grid = (4,)
        g_spec = pl.BlockSpec((sh, N), lambda k, pref: (2 * k + pref[0], 0))
        u_spec = pl.BlockSpec((None, sh, N), lambda k, pref: (k, 0, 0))
    else:
        wid = U[1] // LANE
        per_chip = 2 * sh // LANE
        off = (sh - _HALF) // LANE if kind == "colx" else sh // LANE
        grid = (4, wid)
        g_spec = pl.BlockSpec((K, LANE), lambda k, j, pref: (0, per_chip * k + off * pref[0] + j))
        u_spec = pl.BlockSpec((None, K, LANE), lambda k, j, pref: (k, 0, j))
    return pl.pallas_call(
        body, grid_spec=pltpu.PrefetchScalarGridSpec(num_scalar_prefetch=1, grid=grid, in_specs=[g_spec, u_spec], out_specs=u_spec),
        out_shape=SDS((4, *U), BF), name=name)(place, g, sib)


def _rs_chips(parts, idx, cid, name):
    n = len(idx)

    def body(*refs):
        ts, bufs = refs[:n], refs[n:2 * n]
        ssem, rsem = refs[2 * n:]
        x, y, c, chips = _my_place()
        _handshake([(*ch, c) for ch in chips])
        cps = []
        for i in range(n):
            for j in range(3):
                cp = pltpu.make_async_remote_copy(
                    src_ref=ts[i].at[2 * chips[j][0] + chips[j][1]], dst_ref=bufs[i].at[j], send_sem=ssem.at[i, j], recv_sem=rsem.at[i, j],
                    device_id=(*chips[j], c), device_id_type=MESH)
                cp.start()
                cps.append(cp)
        for cp in cps:
            cp.wait()

    out_type = [SDS((3, *_unit_shape(*_MATS[i][2:])), BF) for i in idx]
    sems = [pltpu.SemaphoreType.DMA((n, 3)), pltpu.SemaphoreType.DMA((n, 3))]
    return _sequencer(body, out_type, sems, cid, name)(*parts)


def _adamw(w, g, m, v):
    m = ADAM_B1 * m + (1.0 - ADAM_B1) * g
    v = ADAM_B2 * v + (1.0 - ADAM_B2) * jnp.square(g)
    m_hat = m / (1.0 - ADAM_B1 ** ADAM_STEP)
    v_hat = v / (1.0 - ADAM_B2 ** ADAM_STEP)
    delta = -ADAM_LR * (m_hat / (jnp.sqrt(v_hat) + ADAM_EPS) + ADAM_WD * w)
    return delta, m, v


def _adam_mat(part, recv, w, m, v, kind, sh, place, name):
    R, C = w.shape
    U = part.shape[1:]
    tr = _pick(R, (256, 352))

    def body(pref, t_ref, r_ref, w_ref, m_ref, v_ref, g_out, d_out, m_out, v_out):
        gp = t_ref[...].astype(F32) + r_ref[0].astype(F32) + r_ref[1].astype(F32) + r_ref[2].astype(F32)

        def finish(g):
            d, mn, vn = _adamw(w_ref[...], g, m_ref[...], v_ref[...])
            g_out[...] = g
            d_out[...] = d
            m_out[...] = mn
            v_out[...] = vn

        if kind == "colx":
            c = pref[0]
            pl.when(c == 0)(lambda: finish(gp[:, 0:sh]))
            pl.when(c == 1)(lambda: finish(gp[:, _HALF:sh + _HALF]))
        else:
            finish(gp)

    t_spec = pl.BlockSpec((None, tr, U[1]), lambda i, pref: (pref[1], i, 0))
    r_spec = pl.BlockSpec((3, tr, U[1]), lambda i, pref: (0, i, 0))
    s_spec = pl.BlockSpec((tr, C), lambda i, pref: (i, 0))
    return pl.pallas_call(
        body, grid_spec=pltpu.PrefetchScalarGridSpec(num_scalar_prefetch=1, grid=(R // tr,), in_specs=[t_spec, r_spec, s_spec, s_spec, s_spec],
                                                     out_specs=[s_spec] * 4),
        out_shape=[SDS((R, C), F32)] * 4, name=name)(place, part, recv, w, m, v)


_SMALL = (("ln_mix", "ln_mix_w", D_MODEL), ("ln_cross", "ln_cross_w", D_MODEL), ("ln_mem", "ln_mem_w", D_MODEL),
          ("ln_ffn", "ln_ffn_w", D_MODEL), ("ln_final", "ln_final_w", D_MODEL), ("hg_lb", "hg_lower_bounds", 2 * HG_WIDTH),
          ("hg_norm", "hg_norm_w", HEAD_DIM))
_PACK = sum(n for _, _, n in _SMALL) + LANE


def _small_sync(grow, wrow, mrow, vrow):
    def body(g_ref, w_ref, m_ref, v_ref, o_ref, gath, ssem, rsem):
        x, y, c, _ = _my_place()
        me = 4 * x + 2 * y + c
        gath[pl.ds(me, 1), :] = g_ref[...]
        cps = []
        for dlt in range(1, N_DEV):
            peer = (me + dlt) % N_DEV
            cp = pltpu.make_async_remote_copy(
                src_ref=g_ref, dst_ref=gath.at[pl.ds(me, 1), :], send_sem=ssem.at[peer], recv_sem=rsem.at[me],
                device_id=(peer // 4, (peer // 2) % 2, peer % 2), device_id_type=MESH)
            cp.start()
            cps.append(cp)
        for dlt in range(1, N_DEV):
            src = (me + dlt) % N_DEV
            pltpu.make_async_remote_copy(
                src_ref=g_ref, dst_ref=gath.at[pl.ds(src, 1), :], send_sem=ssem.at[src], recv_sem=rsem.at[src],
                device_id=(x, y, c), device_id_type=MESH).wait_recv()
        for cp in cps:
            cp.wait_send()
        g = gath[0:1, :]
        for j in range(1, N_DEV):
            g = g + gath[j:j + 1, :]
        d, mn, vn = _adamw(w_ref[...], g, m_ref[...], v_ref[...])
        o_ref[0:1, :] = g
        o_ref[1:2, :] = d
        o_ref[2:3, :] = mn
        o_ref[3:4, :] = vn

    vm = pl.BlockSpec(memory_space=pltpu.VMEM)
    return pl.pallas_call(
        body, in_specs=[vm] * 4, out_specs=vm, out_shape=SDS((4, _PACK), F32),
        scratch_shapes=[pltpu.VMEM((N_DEV, _PACK), F32), pltpu.SemaphoreType.DMA((N_DEV,)), pltpu.SemaphoreType.DMA((N_DEV,))],
        name="small_sync")(grow, wrow, mrow, vrow)


def _pack_small(d, loss_row=None):
    parts = [d[k].reshape(1, n).astype(F32) for k, _, n in _SMALL]
    parts.append(loss_row if loss_row is not None else jnp.ones((1, LANE), F32))
    return jnp.concatenate(parts, axis=1)


def kernel(x, mem, ln_mix_w, w_in, hg_norm_w, hg_lower_bounds, w_branch_a, w_branch_b, w_out, ln_cross_w, ln_mem_w, wq_cross, wkv_cross, wo_cross, ln_ffn_w, w1, w3, w2, ln_final_w, loss_target, m_ln_mix_w, m_w_in, m_hg_norm_w, m_hg_lower_bounds, m_w_branch_a, m_w_branch_b, m_w_out, m_ln_cross_w, m_ln_mem_w, m_wq_cross, m_wkv_cross, m_wo_cross, m_ln_ffn_w, m_w1, m_w3, m_w2, m_ln_final_w, v_ln_mix_w, v_w_in, v_hg_norm_w, v_hg_lower_bounds, v_w_branch_a, v_w_branch_b, v_w_out, v_ln_cross_w, v_ln_mem_w, v_wq_cross, v_wkv_cross, v_wo_cross, v_ln_ffn_w, v_w1, v_w3, v_w2, v_ln_final_w):
    given = dict(locals())
    place = jnp.stack([lax.axis_index("c"), 2 * lax.axis_index("x") + lax.axis_index("y")]).astype(jnp.int32)

    shard = {key: given[ref][0] for key, ref, _, _, _, _ in _MATS}
    xkeys = [key for key, _, _, _, kind, _ in _MATS if kind == "colx"]
    heads = dict(zip(xkeys, _straddle_exchange([shard[k] for k in xkeys])))
    index = {key: i for i, (key, *_) in enumerate(_MATS)}

    def unit(key):
        kind = _MATS[index[key]][4]
        if kind == "colx":
            return _cast_unit_x(shard[key], heads[key], "cast_" + key)
        return _cast_unit(shard[key], "cast_" + key)

    first, rest = ["w_in"], ["w_ba", "w_bb", "w_out", "wq", "wkv", "wo", "w1", "w3", "w2"]
    full_first = _all_gather([unit(k) for k in first], [index[k] for k in first], 1, "all_gather_in")
    full_rest = _all_gather([unit(k) for k in rest], [index[k] for k in rest], 2, "all_gather_rest", after=full_first[0])
    p = dict(zip(first + rest, list(full_first) + list(full_rest)))
    small_w = {"ln_mix": ln_mix_w, "ln_cross": ln_cross_w, "ln_mem": ln_mem_w, "ln_ffn": ln_ffn_w,
               "ln_final": ln_final_w.reshape(1, D_MODEL), "hg_lb": hg_lower_bounds, "hg_norm": hg_norm_w}
    p.update(small_w)

    groups = {"ffn": ["w2", "w1", "w3"], "cross": ["wo", "wq", "wkv"], "mix": ["w_out", "w_ba", "w_bb"], "in": ["w_in"]}
    cids = {"ffn": (3, 4), "cross": (5, 6), "mix": (7, 8), "in": (9, 10)}
    stage_order = list(groups)
    sib, parts, recv, out = {}, {}, {}, {}

    def chip_stage(name, gw):
        keys = groups[name]
        parts[name] = [_chip_sum(gw[k], s, _MATS[index[k]][4], _MATS[index[k]][5], place, "chip_sum_" + k)
                       for k, s in zip(keys, sib[name])]
        recv[name] = _rs_chips(parts[name], [index[k] for k in keys], cids[name][1], "rs_chips_" + name)

    def adam_stage(name):
        for k, t, r in zip(groups[name], parts[name], recv[name]):
            _, ref, _, _, kind, sh = _MATS[index[k]]
            res = _adam_mat(t, r, shard[k], given["m_" + ref][0], given["v_" + ref][0], kind, sh, place, "adam_" + k)
            out[ref] = [a[None] for a in res]

    def ready(name, gw):
        keys = groups[name]
        sib[name] = _rs_sibling([gw[k] for k in keys], [index[k] for k in keys], cids[name][0], "rs_sibling_" + name)
        at = stage_order.index(name)
        if at >= 1:
            chip_stage(stage_order[at - 1], gw)
        if name == stage_order[-1]:
            chip_stage(name, gw)

    loss_row, grad_x, gs = _local_step(x[0], mem[0], loss_target[0], p, ready)

    sm = {k: given["m_" + ref] for k, ref, _ in _SMALL}
    sv = {k: given["v_" + ref] for k, ref, _ in _SMALL}
    synced = _small_sync(_pack_small(gs, loss_row), _pack_small(small_w), _pack_small(sm), _pack_small(sv))
    for name in stage_order:
        adam_stage(name)
    off = 0
    for k, ref, n in _SMALL:
        shape = given[ref].shape
        out[ref] = [synced[r, off:off + n].reshape(shape) for r in range(4)]
        off += n
    loss = synced[0, off]

    order = ["ln_mix_w", "w_in", "hg_norm_w", "hg_lower_bounds", "w_branch_a", "w_branch_b", "w_out", "ln_cross_w", "ln_mem_w",
             "wq_cross", "wkv_cross", "wo_cross", "ln_ffn_w", "w1", "w3", "w2", "ln_final_w"]
    return (loss, grad_x[None], *[out[n][0] for n in order], *[out[n][1] for n in order],
            *[out[n][2] for n in order], *[out[n][3] for n in order])
```

```python
import functools
import math

import jax
import jax.numpy as jnp
from jax import lax
from jax.experimental import pallas as pl
from jax.experimental.pallas import tpu as pltpu
from jax.experimental.pallas import tpu_sc as plsc

F32 = jnp.float32
BF = jnp.bfloat16
SDS = jax.ShapeDtypeStruct

D_MODEL = 2048
SEQ = 2048
HEAD_DIM = 128
MEM_LEN = 256
ATT_GROUPS = ((128, 1), (512, 4), (2048, 16))
ATT_HEADS = 4
ATT_WIDTH = 1536
ATT_OUT = 512
HG_HEADS = 8
HG_WIDTH = 1024
HG_CHUNK = 64
IN_WIDTH = 12800
CROSS_HEADS = 4
CROSS_WIDTH = 512
D_FF = 5632
RMS_EPS = 1e-6
ADAM_LR = 0.001
ADAM_B1 = 0.9
ADAM_B2 = 0.999
ADAM_EPS = 1e-08
ADAM_WD = 0.01
ADAM_STEP = 10
N_DEV = 8

LANE = 128
QA0, KA0, VA0 = 0, 12, 24
QH0, FH0, IH0, GH0 = 36, 44, 52, 60
GA0, GB0 = 68, 84

VMEM_BIG = 56 * 1024 * 1024
MESH = pl.DeviceIdType.MESH


def _cp(vmem=None, **kw):
    if vmem is not None:
        kw["vmem_limit_bytes"] = vmem
    return pltpu.CompilerParams(**kw)


def _pick(n, cands):
    for c in cands:
        if n % c == 0:
            return c
    raise ValueError(f"no tile for {n}")


_ANY = pl.BlockSpec(memory_space=pl.ANY)
_ORDER = [None]


def _pcall(body, *, name, in_specs, out_specs, out_shape, grid=(), scratch_shapes=(), compiler_params=None, prefetch=False):
    def run(*args):
        dep = _ORDER[0]
        specs, operands, kernel_fn = list(in_specs), list(args), body
        if dep is not None:
            at = len(operands)
            specs.append(_ANY)
            operands.append(dep)

            def kernel_fn(*refs):
                return body(*refs[:at], *refs[at + 1:])
        if prefetch:
            call = pl.pallas_call(
                kernel_fn, out_shape=out_shape, name=name, compiler_params=compiler_params,
                grid_spec=pltpu.PrefetchScalarGridSpec(num_scalar_prefetch=1, grid=grid, in_specs=specs, out_specs=out_specs,
                                                       scratch_shapes=scratch_shapes))
        else:
            call = pl.pallas_call(kernel_fn, grid=grid, in_specs=specs, out_specs=out_specs, out_shape=out_shape,
                                  scratch_shapes=scratch_shapes, compiler_params=compiler_params, name=name)
        out = call(*operands)
        _ORDER[0] = out[0] if isinstance(out, (list, tuple)) else out
        return out
    return run


def _mm(a, b, form, out_dtype, name, res=None, tiles=None):
    if form == "nn":
        (M, K), (K2, N) = a.shape, b.shape
    elif form == "nt":
        (M, K), (N, K2) = a.shape, b.shape
    else:
        (K, M), (K2, N) = a.shape, b.shape
    assert K == K2, (a.shape, b.shape, form)
    if tiles is None:
        tm = _pick(M, (1024, 1280, 1408, 512, 256))
        tn = _pick(N, (1280, 1024, 1408, 512, 256))
        tk = K if K <= 2048 else _pick(K, (2560, 1408))
    else:
        tm, tn, tk = tiles
    nk = K // tk
    if form == "nn":
        a_spec = pl.BlockSpec((tm, tk), lambda i, j, k: (i, k))
        b_spec = pl.BlockSpec((tk, tn), lambda i, j, k: (k, j))
        dims = ((1,), (0,))
    elif form == "nt":
        a_spec = pl.BlockSpec((tm, tk), lambda i, j, k: (i, k))
        b_spec = pl.BlockSpec((tn, tk), lambda i, j, k: (j, k))
        dims = ((1,), (1,))
    else:
        a_spec = pl.BlockSpec((tk, tm), lambda i, j, k: (k, i))
        b_spec = pl.BlockSpec((tk, tn), lambda i, j, k: (k, j))
        dims = ((0,), (0,))
    o_spec = pl.BlockSpec((tm, tn), lambda i, j, k: (i, j))
    in_specs = [a_spec, b_spec]
    args = [a, b]
    if res is not None:
        in_specs.append(o_spec)
        args.append(res)

    def body(*refs):
        a_ref, b_ref = refs[0], refs[1]
        r_ref = refs[2] if res is not None else None
        o_ref = refs[3] if res is not None else refs[2]
        acc = refs[-1] if nk > 1 else None
        k = pl.program_id(2)
        d = lax.dot_general(a_ref[...].astype(BF), b_ref[...].astype(BF), (dims, ((), ())), preferred_element_type=F32)

        def finish(r):
            if res is not None:
                r = r + r_ref[...].astype(F32)
            o_ref[...] = r.astype(o_ref.dtype)

        if nk == 1:
            finish(d)
        else:
            @pl.when(k == 0)
            def _():
                acc[...] = d

            @pl.when((k > 0) & (k < nk - 1))
            def _():
                acc[...] += d

            @pl.when(k == nk - 1)
            def _():
                finish(acc[...] + d)

    return _pcall(
        body, grid=(M // tm, N // tn, nk), in_specs=in_specs, out_specs=o_spec,
        out_shape=SDS((M, N), out_dtype), scratch_shapes=[pltpu.VMEM((tm, tn), F32)] if nk > 1 else [],
        compiler_params=_cp(VMEM_BIG, dimension_semantics=("parallel", "parallel", "arbitrary")),
        name=name)(*args)


def _rms(x, w):
    return x * lax.rsqrt(jnp.mean(x * x, axis=-1, keepdims=True) + RMS_EPS) * w


def _rms_fwd(x, w, name):
    R, D = x.shape
    tr = 256

    def body(x_ref, w_ref, o_ref):
        o_ref[...] = _rms(x_ref[...], w_ref[...]).astype(o_ref.dtype)

    return _pcall(
        body, grid=(R // tr,),
        in_specs=[pl.BlockSpec((tr, D), lambda i: (i, 0)), pl.BlockSpec((1, D), lambda i: (0, 0))],
        out_specs=pl.BlockSpec((tr, D), lambda i: (i, 0)), out_shape=SDS((R, D), BF), name=name)(x, w)


def _rms_bwd(x, w, dh, dres, name):
    R, D = x.shape
    tr = 256
    want_dx = dres is not None

    def body(*refs):
        if want_dx:
            x_ref, w_ref, dh_ref, dr_ref, dx_ref, dxb_ref, dw_ref = refs
        else:
            x_ref, w_ref, dh_ref, dw_ref = refs

        @pl.when(pl.program_id(0) == 0)
        def _():
            dw_ref[...] = jnp.zeros_like(dw_ref)

        _, vjp = jax.vjp(_rms, x_ref[...], w_ref[...])
        dx, dw = vjp(dh_ref[...].astype(F32))
        dw_ref[...] += dw
        if want_dx:
            dx = dx + dr_ref[...]
            dx_ref[...] = dx
            dxb_ref[...] = dx.astype(BF)

    row = pl.BlockSpec((tr, D), lambda i: (i, 0))
    vec = pl.BlockSpec((1, D), lambda i: (0, 0))
    if want_dx:
        return _pcall(body, grid=(R // tr,), in_specs=[row, vec, row, row], out_specs=[row, row, vec],
                              out_shape=[SDS((R, D), F32), SDS((R, D), BF), SDS((1, D), F32)], name=name)(x, w, dh, dres)
    return _pcall(body, grid=(R // tr,), in_specs=[row, vec, row], out_specs=vec,
                          out_shape=SDS((1, D), F32), name=name)(x, w, dh)


def _final_loss(x3, w, tgt):
    R, D = x3.shape
    tr = 256

    def loss_fn(xv, wv, tv):
        err = _rms(xv, wv) - tv
        return 0.5 * jnp.sum(jnp.mean(err * err, axis=-1))

    def body(x_ref, w_ref, t_ref, loss_ref, dx_ref, dxb_ref, dw_ref):
        @pl.when(pl.program_id(0) == 0)
        def _():
            dw_ref[...] = jnp.zeros_like(dw_ref)
            loss_ref[...] = jnp.zeros_like(loss_ref)

        tv = t_ref[...]
        val, vjp = jax.vjp(lambda a, b: loss_fn(a, b, tv), x_ref[...], w_ref[...])
        dx, dw = vjp(jnp.ones((), F32))
        dx_ref[...] = dx
        dxb_ref[...] = dx.astype(BF)
        dw_ref[...] += dw
        loss_ref[...] += jnp.full(loss_ref.shape, val, F32)

    row = pl.BlockSpec((tr, D), lambda i: (i, 0))
    vec = pl.BlockSpec((1, D), lambda i: (0, 0))
    one = pl.BlockSpec((1, LANE), lambda i: (0, 0))
    return _pcall(body, grid=(R // tr,), in_specs=[row, vec, row], out_specs=[one, row, row, vec],
                          out_shape=[SDS((1, LANE), F32), SDS((R, D), F32), SDS((R, D), BF), SDS((1, D), F32)],
                          name="final_loss")(x3, w, tgt)


NEG = -1e30


def _att_blocks(S, d):
    L = S // d
    nb = L // LANE
    return nb, (2 if nb > 1 else 1)


def _att_rows(S, d, idx):
    nb, nkb = _att_blocks(S, d)
    r = idx // nb
    n = idx % nb
    kb = jnp.maximum(n - 1, 0)
    if d == 1:
        qrows = pl.ds(pl.multiple_of(n * LANE, LANE), LANE)
        krows = pl.ds(pl.multiple_of(kb * LANE, LANE), LANE * nkb)
    else:
        qrows = pl.ds(r + n * (LANE * d), LANE, stride=d)
        krows = pl.ds(r + kb * (LANE * d), LANE * nkb, stride=d)
    qpos = n * LANE + lax.broadcasted_iota(jnp.int32, (LANE, LANE * nkb), 0)
    kpos = kb * LANE + lax.broadcasted_iota(jnp.int32, (LANE, LANE * nkb), 1)
    return qrows, krows, qpos - kpos


def _slab(col):
    return pl.BlockSpec((SEQ, LANE), lambda h, col=col: (0, col + h))


def _attn_fwd(proj):
    S = proj.shape[0]
    scale = HEAD_DIM ** -0.5

    def body(q0, q1, q2, k0, k1, k2, v0, v1, v2, o_ref, lse_ref, og, lg):
        qs, ks, vs = (q0, q1, q2), (k0, k1, k2), (v0, v1, v2)
        for g, (window, d) in enumerate(ATT_GROUPS):
            def blk(idx, carry, g=g, window=window, d=d):
                qrows, krows, dist = _att_rows(S, d, idx)
                q = qs[g][qrows, :].astype(BF)
                k = ks[g][krows, :].astype(BF)
                v = vs[g][krows, :].astype(BF)
                s = lax.dot_general(q, k, (((1,), (1,)), ((), ())), preferred_element_type=F32) * scale
                s = jnp.where((dist >= 0) & (dist <= window // d), s, NEG)
                m = jnp.max(s, axis=-1, keepdims=True)
                p = jnp.exp(s - m)
                l = jnp.sum(p, axis=-1, keepdims=True)
                o = jnp.dot((p / l).astype(BF), v, preferred_element_type=F32)
                og[g, qrows, :] = o
                lg[g, qrows, :] = jnp.broadcast_to(m + jnp.log(l), (LANE, LANE))
                return carry
            lax.fori_loop(0, S // LANE, blk, 0)

        def merge(t, carry):
            rows = pl.ds(pl.multiple_of(t * 256, 256), 256)
            l0, l1, l2 = lg[0, rows, :], lg[1, rows, :], lg[2, rows, :]
            m = jnp.maximum(jnp.maximum(l0, l1), l2)
            w0, w1, w2 = jnp.exp(l0 - m), jnp.exp(l1 - m), jnp.exp(l2 - m)
            den = w0 + w1 + w2
            o_ref[rows, :] = (w0 * og[0, rows, :] + w1 * og[1, rows, :] + w2 * og[2, rows, :]) / den
            lse_ref[rows, :] = m + jnp.log(den)
            return carry
        lax.fori_loop(0, S // 256, merge, 0)

    ins = [_slab(QA0 + 4 * g) for g in range(3)] + [_slab(KA0 + 4 * g) for g in range(3)] + [_slab(VA0 + 4 * g) for g in range(3)]
    out = pl.BlockSpec((SEQ, LANE), lambda h: (0, h))
    return _pcall(
        body, grid=(ATT_HEADS,), in_specs=ins, out_specs=[out, out],
        out_shape=[SDS((S, ATT_OUT), F32), SDS((S, ATT_OUT), F32)],
        scratch_shapes=[pltpu.VMEM((3, S, LANE), F32), pltpu.VMEM((3, S, LANE), F32)],
        compiler_params=_cp(VMEM_BIG), name="attn_fwd")(*([proj] * 9))


def _attn_bwd(proj, o_att, lse, do_att):
    S = proj.shape[0]
    scale = HEAD_DIM ** -0.5

    def body(q0, q1, q2, k0, k1, k2, v0, v1, v2, o_ref, lse_ref, do_ref,
             dq0, dq1, dq2, dk0, dk1, dk2, dv0, dv1, dv2, dqa, dka, dva, delta):
        qs, ks, vs = (q0, q1, q2), (k0, k1, k2), (v0, v1, v2)
        dqs, dks, dvs = (dq0, dq1, dq2), (dk0, dk1, dk2), (dv0, dv1, dv2)
        delta[...] = jnp.broadcast_to(jnp.sum(do_ref[...] * o_ref[...], axis=-1, keepdims=True), delta.shape)
        for g, (window, d) in enumerate(ATT_GROUPS):
            dka[...] = jnp.zeros_like(dka)
            dva[...] = jnp.zeros_like(dva)

            def blk(idx, carry, g=g, window=window, d=d):
                qrows, krows, dist = _att_rows(S, d, idx)
                q = qs[g][qrows, :].astype(BF)
                k = ks[g][krows, :].astype(BF)
                v = vs[g][krows, :].astype(BF)
                do = do_ref[qrows, :]
                s = lax.dot_general(q, k, (((1,), (1,)), ((), ())), preferred_element_type=F32) * scale
                p = jnp.exp(s - lse_ref[qrows, :][:, 0:1])
                p = jnp.where((dist >= 0) & (dist <= window // d), p, 0.0)
                dob = do.astype(BF)
                dva[krows, :] += lax.dot_general(p.astype(BF), dob, (((0,), (0,)), ((), ())), preferred_element_type=F32)
                dp = lax.dot_general(dob, v, (((1,), (1,)), ((), ())), preferred_element_type=F32)
                ds = (p * (dp - delta[qrows, :][:, 0:1]) * scale).astype(BF)
                dqa[qrows, :] = jnp.dot(ds, k, preferred_element_type=F32)
                dka[krows, :] += lax.dot_general(ds, q, (((0,), (0,)), ((), ())), preferred_element_type=F32)
                return carry
            lax.fori_loop(0, S // LANE, blk, 0)
            dqs[g][...] = dqa[...].astype(dqs[g].dtype)
            dks[g][...] = dka[...].astype(dks[g].dtype)
            dvs[g][...] = dva[...].astype(dvs[g].dtype)

    cols = [QA0 + 4 * g for g in range(3)] + [KA0 + 4 * g for g in range(3)] + [VA0 + 4 * g for g in range(3)]
    ins = [_slab(c) for c in cols]
    one = pl.BlockSpec((SEQ, LANE), lambda h: (0, h))
    return _pcall(
        body, grid=(ATT_HEADS,), in_specs=ins + [one, one, one], out_specs=[one] * 9,
        out_shape=[SDS((S, ATT_OUT), BF)] * 9,
        scratch_shapes=[pltpu.VMEM((S, LANE), F32)] * 4,
        compiler_params=_cp(VMEM_BIG), name="attn_bwd")(*([proj] * 9), o_att, lse, do_att)


def _bdot(a, b, dims):
    return lax.dot_general(a.astype(BF), b.astype(BF), (dims, ((), ())), preferred_element_type=F32)


def _hg_chunk(qh, fh, ih, gh, lbraw, nw, st):
    C = HG_CHUNK
    mx = jnp.maximum(lbraw[0:1], lbraw[1:2])
    e0 = jnp.exp(lbraw[0:1] - mx)
    e1 = jnp.exp(lbraw[1:2] - mx)
    lb = e0 / (e0 + e1)
    f = lb + (1.0 - lb) * jax.nn.sigmoid(fh)
    logf = jnp.log(f)
    k = 1.0 - f
    q = qh * jax.nn.sigmoid(qh)
    causal = lax.broadcasted_iota(jnp.int32, (C, C), 0) >= lax.broadcasted_iota(jnp.int32, (C, C), 1)
    b = lax.dot_general(causal.astype(F32), logf, (((1,), (0,)), ((), ())), precision=lax.Precision.HIGHEST,
                        preferred_element_type=F32)
    b_last = b[C - 1:C]
    st_out = st * jnp.exp(b_last) + _bdot(ih, k * jnp.exp(b_last - b), ((0,), (0,)))
    inter = _bdot(q * jnp.exp(b), st, ((1,), (1,)))
    b_ref = b[C // 2:C // 2 + 1]
    a = _bdot(q * jnp.exp(b - b_ref), k * jnp.exp(b_ref - b), ((1,), (1,)))
    a = jnp.where(causal, a, 0.0)
    o = inter + _bdot(a, ih, ((1,), (0,)))
    o = o * lax.rsqrt(jnp.mean(o * o, axis=-1, keepdims=True) + RMS_EPS) * nw
    return o * (gh * jax.nn.sigmoid(gh)), st_out


def _head_cols(h):
    return slice(h * LANE, (h + 1) * LANE)


def _hg_in_specs(chunk_of):
    half = 4 * LANE
    specs = []
    for base in (QH0, FH0, IH0, GH0):
        for part in range(2):
            specs.append(pl.BlockSpec((HG_CHUNK, half), lambda n, col=base // 4 + part: (chunk_of(n), col)))
    return specs


def _hg_fwd(proj, lb, nw):
    S = proj.shape[0]
    C = HG_CHUNK
    NC = S // C

    def body(*refs):
        sec = refs[:8]
        lb_ref, nw_ref, o_ref, st_ref, st_scr = refs[8:]

        @pl.when(pl.program_id(0) == 0)
        def _():
            st_scr[...] = jnp.zeros_like(st_scr)

        for h in range(HG_HEADS):
            half, hs = h // 4, _head_cols(h % 4)
            st = st_scr[h]
            st_ref[h, 0] = st
            o, st_out = _hg_chunk(sec[half][:, hs], sec[2 + half][:, hs], sec[4 + half][:, hs], sec[6 + half][:, hs],
                                  lb_ref[:, _head_cols(h)], nw_ref[...], st)
            o_ref[:, _head_cols(h)] = o.astype(o_ref.dtype)
            st_scr[h] = st_out

    return _pcall(
        body, grid=(NC,),
        in_specs=_hg_in_specs(lambda n: n) + [pl.BlockSpec((2, HG_WIDTH), lambda n: (0, 0)), pl.BlockSpec((1, LANE), lambda n: (0, 0))],
        out_specs=[pl.BlockSpec((C, HG_WIDTH), lambda n: (n, 0)), pl.BlockSpec((HG_HEADS, 1, LANE, LANE), lambda n: (0, n, 0, 0))],
        out_shape=[SDS((S, HG_WIDTH), BF), SDS((HG_HEADS, NC, LANE, LANE), F32)],
        scratch_shapes=[pltpu.VMEM((HG_HEADS, LANE, LANE), F32)], name="hg_fwd")(*([proj] * 8), lb, nw)


def _hg_bwd(proj, lb, nw, states, do_hg):
    S = proj.shape[0]
    C = HG_CHUNK
    NC = S // C

    def body(*refs):
        sec = refs[:8]
        lb_ref, nw_ref, st_ref, do_ref, dq_ref, df_ref, di_ref, dg_ref, dlb_ref, dnw_ref, dst_scr = refs[8:]

        @pl.when(pl.program_id(0) == 0)
        def _():
            dst_scr[...] = jnp.zeros_like(dst_scr)
            dlb_ref[...] = jnp.zeros_like(dlb_ref)
            dnw_ref[...] = jnp.zeros_like(dnw_ref)

        dnw_sum = jnp.zeros((1, LANE), F32)
        for h in range(HG_HEADS):
            half, hs, hc = h // 4, _head_cols(h % 4), _head_cols(h)
            _, vjp = jax.vjp(_hg_chunk, sec[half][:, hs], sec[2 + half][:, hs], sec[4 + half][:, hs], sec[6 + half][:, hs],
                             lb_ref[:, hc], nw_ref[...], st_ref[h, 0])
            dq, df, di, dg, dlb, dnw, dst = vjp((do_ref[:, hc].astype(F32), dst_scr[h]))
            dq_ref[:, hc] = dq.astype(dq_ref.dtype)
            df_ref[:, hc] = df.astype(df_ref.dtype)
            di_ref[:, hc] = di.astype(di_ref.dtype)
            dg_ref[:, hc] = dg.astype(dg_ref.dtype)
            dlb_ref[:, hc] += dlb
            dnw_sum = dnw_sum + dnw
            dst_scr[h] = dst
        dnw_ref[...] += dnw_sum

    rev = lambda n: NC - 1 - n
    ospec = pl.BlockSpec((C, HG_WIDTH), lambda n: (rev(n), 0))
    lbs = pl.BlockSpec((2, HG_WIDTH), lambda n: (0, 0))
    nws = pl.BlockSpec((1, LANE), lambda n: (0, 0))
    outs = _pcall(
        body, grid=(NC,),
        in_specs=_hg_in_specs(rev) + [lbs, nws, pl.BlockSpec((HG_HEADS, 1, LANE, LANE), lambda n: (0, rev(n), 0, 0)), ospec],
        out_specs=[ospec] * 4 + [lbs, nws],
        out_shape=[SDS((S, HG_WIDTH), BF)] * 4 + [SDS((2, HG_WIDTH), F32), SDS((1, LANE), F32)],
        scratch_shapes=[pltpu.VMEM((HG_HEADS, LANE, LANE), F32)], name="hg_bwd")(*([proj] * 8), lb, nw, states, do_hg)
    return outs


def _merge(ga, gb, ba, bb):
    return jax.nn.sigmoid(ga) * ba + jax.nn.sigmoid(gb) * bb


_MT = 512


def _gate_specs():
    ga = pl.BlockSpec((_MT, _MT), lambda i, j: (i, GA0 * LANE // _MT + j))
    gb = pl.BlockSpec((_MT, _MT), lambda i, j: (i, GB0 * LANE // _MT + j))
    t = pl.BlockSpec((_MT, _MT), lambda i, j: (i, j))
    return ga, gb, t


def _merge_fwd(proj, ba, bb):
    S, D = ba.shape
    ga, gb, t = _gate_specs()

    def body(ga_ref, gb_ref, ba_ref, bb_ref, o_ref):
        o_ref[...] = _merge(ga_ref[...], gb_ref[...], ba_ref[...], bb_ref[...]).astype(o_ref.dtype)

    return _pcall(body, grid=(S // _MT, D // _MT), in_specs=[ga, gb, t, t], out_specs=t,
                          out_shape=SDS((S, D), BF), name="merge_fwd")(proj, proj, ba, bb)


def _merge_bwd(proj, ba, bb, dm):
    S, D = ba.shape
    ga, gb, t = _gate_specs()

    def body(ga_ref, gb_ref, ba_ref, bb_ref, dm_ref, dga_ref, dgb_ref, dba_ref, dbb_ref):
        _, vjp = jax.vjp(_merge, ga_ref[...], gb_ref[...], ba_ref[...], bb_ref[...])
        dga, dgb, dba, dbb = vjp(dm_ref[...])
        dga_ref[...] = dga.astype(BF)
        dgb_ref[...] = dgb.astype(BF)
        dba_ref[...] = dba.astype(BF)
        dbb_ref[...] = dbb.astype(BF)

    return _pcall(body, grid=(S // _MT, D // _MT), in_specs=[ga, gb, t, t, t], out_specs=[t] * 4,
                          out_shape=[SDS((S, D), BF)] * 4, name="merge_bwd")(proj, proj, ba, bb, dm)


def _swiglu(a, b):
    return a * jax.nn.sigmoid(a) * b


def _swiglu_fwd(a, b):
    S, F = a.shape
    tf = _pick(F, (1408, 512))
    t = pl.BlockSpec((_MT, tf), lambda i, j: (i, j))

    def body(a_ref, b_ref, o_ref):
        o_ref[...] = _swiglu(a_ref[...], b_ref[...]).astype(o_ref.dtype)

    return _pcall(body, grid=(S // _MT, F // tf), in_specs=[t, t], out_specs=t,
                          out_shape=SDS((S, F), BF), name="swiglu_fwd")(a, b)


def _swiglu_bwd(a, b, dg):
    S, F = a.shape
    tf = _pick(F, (1408, 512))
    t = pl.BlockSpec((_MT, tf), lambda i, j: (i, j))

    def body(a_ref, b_ref, dg_ref, da_ref, db_ref):
        _, vjp = jax.vjp(_swiglu, a_ref[...], b_ref[...])
        da, db = vjp(dg_ref[...].astype(F32))
        da_ref[...] = da.astype(BF)
        db_ref[...] = db.astype(BF)

    return _pcall(body, grid=(S // _MT, F // tf), in_specs=[t, t, t], out_specs=[t, t],
                          out_shape=[SDS((S, F), BF)] * 2, name="swiglu_bwd")(a, b, dg)


def _cross(q, k, v):
    s = _bdot(q, k, ((1,), (1,))) * (HEAD_DIM ** -0.5)
    e = jnp.exp(s - jnp.max(s, axis=-1, keepdims=True))
    p = e / jnp.sum(e, axis=-1, keepdims=True)
    return _bdot(p, v, ((1,), (0,)))


def _cross_specs():
    q = pl.BlockSpec((_MT, LANE), lambda h, i: (i, h))
    k = pl.BlockSpec((MEM_LEN, LANE), lambda h, i: (0, h))
    v = pl.BlockSpec((MEM_LEN, LANE), lambda h, i: (0, CROSS_HEADS + h))
    return q, k, v


def _cross_fwd(qc, kvc):
    S = qc.shape[0]
    q, k, v = _cross_specs()

    def body(q_ref, k_ref, v_ref, o_ref):
        o_ref[...] = _cross(q_ref[...], k_ref[...], v_ref[...]).astype(o_ref.dtype)

    return _pcall(body, grid=(CROSS_HEADS, S // _MT), in_specs=[q, k, v], out_specs=q,
                          out_shape=SDS((S, CROSS_WIDTH), BF), name="cross_fwd")(qc, kvc, kvc)


def _cross_bwd(qc, kvc, doc):
    S = qc.shape[0]
    q, k, v = _cross_specs()

    def body(q_ref, k_ref, v_ref, do_ref, dq_ref, dk_ref, dv_ref):
        @pl.when(pl.program_id(1) == 0)
        def _():
            dk_ref[...] = jnp.zeros_like(dk_ref)
            dv_ref[...] = jnp.zeros_like(dv_ref)

        _, vjp = jax.vjp(_cross, q_ref[...], k_ref[...], v_ref[...])
        dq, dk, dv = vjp(do_ref[...])
        dq_ref[...] = dq.astype(dq_ref.dtype)
        dk_ref[...] += dk
        dv_ref[...] += dv

    return _pcall(body, grid=(CROSS_HEADS, S // _MT), in_specs=[q, k, v, q], out_specs=[q, k, k],
                          out_shape=[SDS((S, CROSS_WIDTH), BF), SDS((MEM_LEN, CROSS_WIDTH), F32), SDS((MEM_LEN, CROSS_WIDTH), F32)],
                          name="cross_bwd")(qc, kvc, kvc, doc)


def _local_step(x, mem, tgt, p, ready):
    h = _rms_fwd(x, p["ln_mix"], "rms_mix")
    proj = _mm(h, p["w_in"], "nt", F32, "mm_proj")
    o_att, lse = _attn_fwd(proj)
    o_hg, states = _hg_fwd(proj, p["hg_lb"], p["hg_norm"])
    ba = _mm(o_att, p["w_ba"], "nn", F32, "mm_ba")
    bb = _mm(o_hg, p["w_bb"], "nn", F32, "mm_bb")
    merged = _merge_fwd(proj, ba, bb)
    x1 = _mm(merged, p["w_out"], "nn", F32, "mm_out", res=x)
    hc = _rms_fwd(x1, p["ln_cross"], "rms_cross")
    qc = _mm(hc, p["wq"], "nn", F32, "mm_q")
    mn = _rms_fwd(mem, p["ln_mem"], "rms_mem")
    kvc = _mm(mn, p["wkv"], "nn", F32, "mm_kv")
    oc = _cross_fwd(qc, kvc)
    x2 = _mm(oc, p["wo"], "nn", F32, "mm_o", res=x1)
    hf = _rms_fwd(x2, p["ln_ffn"], "rms_ffn")
    a = _mm(hf, p["w1"], "nt", F32, "mm_w1")
    b = _mm(hf, p["w3"], "nt", F32, "mm_w3")
    g = _swiglu_fwd(a, b)
    x3 = _mm(g, p["w2"], "nn", F32, "mm_w2", res=x2)
    loss, dx3, dx3b, d_ln_final = _final_loss(x3, p["ln_final"], tgt)

    gw, gs = {}, {"ln_final": d_ln_final}
    dg = _mm(dx3b, p["w2"], "nt", BF, "mm_dg")
    gw["w2"] = _mm(g, dx3b, "tn", BF, "mm_dw2")
    da, db = _swiglu_bwd(a, b, dg)
    dhf = _mm(da, p["w1"], "nn", F32, "mm_dhf1")
    dhf = _mm(db, p["w3"], "nn", F32, "mm_dhf3", res=dhf)
    gw["w1"] = _mm(da, hf, "tn", BF, "mm_dw1")
    gw["w3"] = _mm(db, hf, "tn", BF, "mm_dw3")
    ready("ffn", gw)
    dx2, dx2b, gs["ln_ffn"] = _rms_bwd(x2, p["ln_ffn"], dhf, dx3, "rms_ffn_bwd")
    doc = _mm(dx2b, p["wo"], "nt", F32, "mm_doc")
    gw["wo"] = _mm(oc, dx2b, "tn", BF, "mm_dwo")
    dqc, dk, dv = _cross_bwd(qc, kvc, doc)
    dkvc = jnp.concatenate([dk, dv], axis=1)
    dhc = _mm(dqc, p["wq"], "nt", F32, "mm_dhc")
    gw["wq"] = _mm(hc, dqc, "tn", BF, "mm_dwq")
    dmn = _mm(dkvc, p["wkv"], "nt", F32, "mm_dmn")
    gw["wkv"] = _mm(mn, dkvc, "tn", BF, "mm_dwkv")
    ready("cross", gw)
    gs["ln_mem"] = _rms_bwd(mem, p["ln_mem"], dmn, None, "rms_mem_bwd")
    dx1, dx1b, gs["ln_cross"] = _rms_bwd(x1, p["ln_cross"], dhc, dx2, "rms_cross_bwd")
    dmerged = _mm(dx1b, p["w_out"], "nt", F32, "mm_dmerged")
    gw["w_out"] = _mm(merged, dx1b, "tn", BF, "mm_dwout")
    dga, dgb, dba, dbb = _merge_bwd(proj, ba, bb, dmerged)
    do_att = _mm(dba, p["w_ba"], "nt", F32, "mm_doatt")
    gw["w_ba"] = _mm(o_att, dba, "tn", BF, "mm_dwba")
    do_hg = _mm(dbb, p["w_bb"], "nt", F32, "mm_dohg")
    gw["w_bb"] = _mm(o_hg, dbb, "tn", BF, "mm_dwbb")
    ready("mix", gw)
    datt = _attn_bwd(proj, o_att, lse, do_att)
    dqh, dfh, dih, dgh, gs["hg_lb"], gs["hg_norm"] = _hg_bwd(proj, p["hg_lb"], p["hg_norm"], states, do_hg)
    dproj = jnp.concatenate([*datt, dqh, dfh, dih, dgh, dga, dgb], axis=1)
    gw["w_in"] = _mm(dproj, h, "tn", BF, "mm_dwin")
    ready("in", gw)
    dh = _mm(dproj, p["w_in"], "nn", F32, "mm_dh")
    grad_x, _, gs["ln_mix"] = _rms_bwd(x, p["ln_mix"], dh, dx1, "rms_mix_bwd")
    return loss, grad_x, gs


_MATS = (
    ("w_in", "w_in", IN_WIDTH, D_MODEL, "row", IN_WIDTH // N_DEV, True),
    ("w_ba", "w_branch_a", ATT_OUT, D_MODEL, "col", D_MODEL // N_DEV, False),
    ("w_bb", "w_branch_b", HG_WIDTH, D_MODEL, "col", D_MODEL // N_DEV, False),
    ("w_out", "w_out", D_MODEL, D_MODEL, "row", D_MODEL // N_DEV, False),
    ("wq", "wq_cross", D_MODEL, CROSS_WIDTH, "row", D_MODEL // N_DEV, False),
    ("wkv", "wkv_cross", D_MODEL, 2 * CROSS_WIDTH, "row", D_MODEL // N_DEV, False),
    ("wo", "wo_cross", CROSS_WIDTH, D_MODEL, "col", D_MODEL // N_DEV, False),
    ("w1", "w1", D_FF, D_MODEL, "row", D_FF // N_DEV, True),
    ("w3", "w3", D_FF, D_MODEL, "row", D_FF // N_DEV, True),
    ("w2", "w2", D_FF, D_MODEL, "row", D_FF // N_DEV, False),
)
_INDEX = {m[0]: i for i, m in enumerate(_MATS)}


def _unit_shape(i):
    _, _, K, N, kind, sh, _ = _MATS[i]
    return (sh, N) if kind == "row" else (K, sh)


def _win(ref, i, chip, par):
    _, _, _, _, kind, sh, _ = _MATS[i]
    if kind == "row":
        return ref.at[pl.ds(pl.multiple_of((2 * chip + par) * sh, 16), sh), :]
    return ref.at[:, pl.ds(pl.multiple_of((2 * chip + par) * sh, LANE), sh)]


def _my_place():
    x, y, c = lax.axis_index("x"), lax.axis_index("y"), lax.axis_index("c")
    chips = [(1 - x, y), (x, 1 - y), (1 - x, 1 - y)]
    return x, y, c, chips


def _cast_unit(w, name):
    K, N = w.shape
    tr = _pick(K, (256, 400, 352))

    def body(w_ref, o_ref):
        o_ref[...] = w_ref[...].astype(BF)

    blk = pl.BlockSpec((tr, N), lambda i: (i, 0))
    return _pcall(body, grid=(K // tr,), in_specs=[blk], out_specs=blk, out_shape=SDS((K, N), BF), name=name)(w)


def _handshake(peers):
    barrier = pltpu.get_barrier_semaphore()
    for peer in peers:
        pl.semaphore_signal(barrier, inc=1, device_id=peer, device_id_type=MESH)
    pl.semaphore_wait(barrier, len(peers))


def _sequencer(body, out_type, sems, cid, name):
    return pl.kernel(body, out_type=out_type, mesh=plsc.ScalarSubcoreMesh(axis_name="seq", num_cores=1),
                     scratch_types=sems, compiler_params=pltpu.CompilerParams(collective_id=cid), name=name)


def _all_gather(units, idx, cid, name):
    n = len(idx)

    def body(*refs):
        us, fulls = refs[:n], refs[n:2 * n]
        ssem, rsem, lsem = refs[2 * n:]
        x, y, c, chips = _my_place()
        _handshake([(x, y, 1 - c)] + [(*ch, c) for ch in chips])

        def win(i, chip_xy, par):
            return _win(fulls[i], idx[i], 2 * chip_xy[0] + chip_xy[1], par)

        def rcopy(i, k, src, dst, to):
            return pltpu.make_async_remote_copy(src_ref=src, dst_ref=dst, send_sem=ssem.at[i, k], recv_sem=rsem.at[i, k],
                                                device_id=to, device_id_type=MESH)

        sends, locs = [], []
        for i in range(n):
            mine = win(i, (x, y), c)
            loc = pltpu.make_async_copy(us[i], mine, lsem.at[i])
            loc.start()
            locs.append(loc)
            first = [rcopy(i, 0, us[i], mine, (x, y, 1 - c))]
            first += [rcopy(i, 1 + j, us[i], mine, (*chips[j], c)) for j in range(3)]
            for cp in first:
                cp.start()
            sends += first
        for i in range(n):
            for j in range(3):
                blk = win(i, chips[j], c)
                rcopy(i, 1 + j, blk, blk, (x, y, c)).wait_recv()
                fwd = rcopy(i, 4 + j, blk, blk, (x, y, 1 - c))
                fwd.start()
                sends.append(fwd)
        for i in range(n):
            blk = win(i, (x, y), 1 - c)
            rcopy(i, 0, blk, blk, (x, y, c)).wait_recv()
            for j in range(3):
                blk = win(i, chips[j], 1 - c)
                rcopy(i, 4 + j, blk, blk, (x, y, c)).wait_recv()
        for cp in sends:
            cp.wait_send()
        for loc in locs:
            loc.wait()

    out_type = [SDS((_MATS[i][2], _MATS[i][3]), BF) for i in idx]
    sems = [pltpu.SemaphoreType.DMA((n, 7)), pltpu.SemaphoreType.DMA((n, 7)), pltpu.SemaphoreType.DMA((n,))]
    return _sequencer(body, out_type, sems, cid, name)(*units)


def _rs_sibling(grads, idx, cid, name):
    n = len(idx)

    def body(*refs):
        gs, bufs = refs[:n], refs[n:2 * n]
        ssem, rsem = refs[2 * n:]
        x, y, c, _ = _my_place()
        _handshake([(x, y, 1 - c)])
        cps = []
        for i in range(n):
            for k in range(4):
                cp = pltpu.make_async_remote_copy(
                    src_ref=_win(gs[i], idx[i], k, 1 - c), dst_ref=bufs[i].at[k], send_sem=ssem.at[i, k], recv_sem=rsem.at[i, k],
                    device_id=(x, y, 1 - c), device_id_type=MESH)
                cp.start()
                cps.append(cp)
        for cp in cps:
            cp.wait()

    out_type = [SDS((4, *_unit_shape(i)), BF) for i in idx]
    sems = [pltpu.SemaphoreType.DMA((n, 4)), pltpu.SemaphoreType.DMA((n, 4))]
    return _sequencer(body, out_type, sems, cid, name)(*grads)


def _chip_sum(g, sib, i, place, name):
    _, _, K, N, kind, sh, _ = _MATS[i]

    def body(pref, g_ref, s_ref, o_ref):
        o_ref[...] = (g_ref[...].astype(F32) + s_ref[...].astype(F32)).astype(BF)

    if kind == "row":
        tr = _pick(sh, (800, 352, 256))
        per = sh // tr
        grid = (4, per)
        g_spec = pl.BlockSpec((tr, N), lambda k, r, pref: ((2 * k + pref[0]) * per + r, 0))
        u_spec = pl.BlockSpec((None, tr, N), lambda k, r, pref: (k, r, 0))
    else:
        grid = (4,)
        g_spec = pl.BlockSpec((K, sh), lambda k, pref: (0, 2 * k + pref[0]))
        u_spec = pl.BlockSpec((None, K, sh), lambda k, pref: (k, 0, 0))
    return _pcall(body, grid=grid, in_specs=[g_spec, u_spec], out_specs=u_spec, out_shape=SDS((4, *_unit_shape(i)), BF),
                  prefetch=True, name=name)(place, g, sib)


def _rs_chips(parts, idx, cid, name):
    n = len(idx)

    def body(*refs):
        ts, bufs = refs[:n], refs[n:2 * n]
        ssem, rsem = refs[2 * n:]
        x, y, c, chips = _my_place()
        _handshake([(*ch, c) for ch in chips])
        cps = []
        for i in range(n):
            for j in range(3):
                cp = pltpu.make_async_remote_copy(
                    src_ref=ts[i].at[2 * chips[j][0] + chips[j][1]], dst_ref=bufs[i].at[j], send_sem=ssem.at[i, j], recv_sem=rsem.at[i, j],
                    device_id=(*chips[j], c), device_id_type=MESH)
                cp.start()
                cps.append(cp)
        for cp in cps:
            cp.wait()

    out_type = [SDS((3, *_unit_shape(i)), BF) for i in idx]
    sems = [pltpu.SemaphoreType.DMA((n, 3)), pltpu.SemaphoreType.DMA((n, 3))]
    return _sequencer(body, out_type, sems, cid, name)(*parts)


def _adamw(w, g, m, v):
    m = ADAM_B1 * m + (1.0 - ADAM_B1) * g
    v = ADAM_B2 * v + (1.0 - ADAM_B2) * jnp.square(g)
    m_hat = m / (1.0 - ADAM_B1 ** ADAM_STEP)
    v_hat = v / (1.0 - ADAM_B2 ** ADAM_STEP)
    delta = -ADAM_LR * (m_hat / (jnp.sqrt(v_hat) + ADAM_EPS) + ADAM_WD * w)
    return delta, m, v


def _adam_mat(part, recv, w, m, v, place, name):
    R, C = w.shape
    tr = _pick(R, (256, 160, 176))

    def body(pref, t_ref, r_ref, w_ref, m_ref, v_ref, g_out, d_out, m_out, v_out):
        g = t_ref[...].astype(F32) + r_ref[0].astype(F32) + r_ref[1].astype(F32) + r_ref[2].astype(F32)
        d, mn, vn = _adamw(w_ref[...], g, m_ref[...], v_ref[...])
        g_out[...] = g
        d_out[...] = d
        m_out[...] = mn
        v_out[...] = vn

    t_spec = pl.BlockSpec((None, tr, C), lambda i, pref: (pref[1], i, 0))
    r_spec = pl.BlockSpec((3, tr, C), lambda i, pref: (0, i, 0))
    s_spec = pl.BlockSpec((tr, C), lambda i, pref: (i, 0))
    return _pcall(body, grid=(R // tr,), in_specs=[t_spec, r_spec, s_spec, s_spec, s_spec], out_specs=[s_spec] * 4,
                  out_shape=[SDS((R, C), F32)] * 4, prefetch=True, name=name)(place, part, recv, w, m, v)


_SMALL = (("ln_mix", "ln_mix_w", D_MODEL), ("ln_cross", "ln_cross_w", D_MODEL), ("ln_mem", "ln_mem_w", D_MODEL),
          ("ln_ffn", "ln_ffn_w", D_MODEL), ("ln_final", "ln_final_w", D_MODEL), ("hg_lb", "hg_lower_bounds", 2 * HG_WIDTH),
          ("hg_norm", "hg_norm_w", HEAD_DIM))
_PACK = sum(n for _, _, n in _SMALL) + LANE


def _small_sync(grow, wrow, mrow, vrow):
    def body(g_ref, w_ref, m_ref, v_ref, o_ref, gath, ssem, rsem):
        x, y, c, _ = _my_place()
        me = 4 * x + 2 * y + c
        gath[pl.ds(me, 1), :] = g_ref[...]
        cps = []
        for dlt in range(1, N_DEV):
            peer = (me + dlt) % N_DEV
            cp = pltpu.make_async_remote_copy(
                src_ref=g_ref, dst_ref=gath.at[pl.ds(me, 1), :], send_sem=ssem.at[peer], recv_sem=rsem.at[me],
                device_id=(peer // 4, (peer // 2) % 2, peer % 2), device_id_type=MESH)
            cp.start()
            cps.append(cp)
        for dlt in range(1, N_DEV):
            src = (me + dlt) % N_DEV
            pltpu.make_async_remote_copy(
                src_ref=g_ref, dst_ref=gath.at[pl.ds(src, 1), :], send_sem=ssem.at[src], recv_sem=rsem.at[src],
                device_id=(x, y, c), device_id_type=MESH).wait_recv()
        for cp in cps:
            cp.wait_send()
        g = gath[0:1, :]
        for j in range(1, N_DEV):
            g = g + gath[j:j + 1, :]
        d, mn, vn = _adamw(w_ref[...], g, m_ref[...], v_ref[...])
        o_ref[0:1, :] = g
        o_ref[1:2, :] = d
        o_ref[2:3, :] = mn
        o_ref[3:4, :] = vn

    vm = pl.BlockSpec(memory_space=pltpu.VMEM)
    return _pcall(
        body, in_specs=[vm] * 4, out_specs=vm, out_shape=SDS((4, _PACK), F32),
        scratch_shapes=[pltpu.VMEM((N_DEV, _PACK), F32), pltpu.SemaphoreType.DMA((N_DEV,)), pltpu.SemaphoreType.DMA((N_DEV,))],
        name="small_sync")(grow, wrow, mrow, vrow)


def _pack_small(d, loss_row=None):
    parts = [d[k].reshape(1, n).astype(F32) for k, _, n in _SMALL]
    parts.append(loss_row if loss_row is not None else jnp.ones((1, LANE), F32))
    return jnp.concatenate(parts, axis=1)


def kernel(x, mem, ln_mix_w, w_in, hg_norm_w, hg_lower_bounds, w_branch_a, w_branch_b, w_out, ln_cross_w, ln_mem_w, wq_cross, wkv_cross, wo_cross, ln_ffn_w, w1, w3, w2, ln_final_w, loss_target, m_ln_mix_w, m_w_in, m_hg_norm_w, m_hg_lower_bounds, m_w_branch_a, m_w_branch_b, m_w_out, m_ln_cross_w, m_ln_mem_w, m_wq_cross, m_wkv_cross, m_wo_cross, m_ln_ffn_w, m_w1, m_w3, m_w2, m_ln_final_w, v_ln_mix_w, v_w_in, v_hg_norm_w, v_hg_lower_bounds, v_w_branch_a, v_w_branch_b, v_w_out, v_ln_cross_w, v_ln_mem_w, v_wq_cross, v_wkv_cross, v_wo_cross, v_ln_ffn_w, v_w1, v_w3, v_w2, v_ln_final_w):
    given = dict(locals())
    place = jnp.stack([lax.axis_index("c"), 2 * lax.axis_index("x") + lax.axis_index("y")]).astype(jnp.int32)

    _ORDER[0] = None
    index = _INDEX

    def local_view(a, i):
        return a[0].T if _MATS[i][6] else a[0]

    shard = {key: local_view(given[ref], i) for i, (key, ref, *_) in enumerate(_MATS)}

    p = {}
    waves = (("in", ["w_in"], 1), ("mid", ["w_ba", "w_bb", "w_out", "wq", "wkv", "wo"], 2), ("ffn", ["w1", "w3", "w2"], 11))
    for wave, keys, cid in waves:
        units = [_cast_unit(shard[k], "cast_" + k) for k in keys]
        p.update(zip(keys, _all_gather(units, [index[k] for k in keys], cid, "all_gather_" + wave)))
    small_w = {"ln_mix": ln_mix_w, "ln_cross": ln_cross_w, "ln_mem": ln_mem_w, "ln_ffn": ln_ffn_w,
               "ln_final": ln_final_w.reshape(1, D_MODEL), "hg_lb": hg_lower_bounds, "hg_norm": hg_norm_w}
    p.update(small_w)

    groups = {"ffn": ["w2", "w1", "w3"], "cross": ["wo", "wq", "wkv"], "mix": ["w_out", "w_ba", "w_bb"], "in": ["w_in"]}
    cids = {"ffn": (3, 4), "cross": (5, 6), "mix": (7, 8), "in": (9, 10)}
    stage_order = list(groups)
    sib, parts, recv, out = {}, {}, {}, {}

    grads = {}

    def chip_stage(name):
        keys = groups[name]
        parts[name] = [_chip_sum(grads[k], s, index[k], place, "chip_sum_" + k) for k, s in zip(keys, sib[name])]
        recv[name] = _rs_chips(parts[name], [index[k] for k in keys], cids[name][1], "rs_chips_" + name)

    def adam_stage(name):
        for k, t, r in zip(groups[name], parts[name], recv[name]):
            i = index[k]
            ref = _MATS[i][1]
            res = _adam_mat(t, r, shard[k], local_view(given["m_" + ref], i), local_view(given["v_" + ref], i), place, "adam_" + k)
            out[ref] = [(a.T if _MATS[i][6] else a)[None] for a in res]

    def ready(name, gw):
        keys = groups[name]
        grads.update({k: gw[k] for k in keys})
        sib[name] = _rs_sibling([gw[k] for k in keys], [index[k] for k in keys], cids[name][0], "rs_sibling_" + name)
        at = stage_order.index(name)
        if at >= 1:
            chip_stage(stage_order[at - 1])

    loss_row, grad_x, gs = _local_step(x[0], mem[0], loss_target[0], p, ready)

    sm = {k: given["m_" + ref] for k, ref, _ in _SMALL}
    sv = {k: given["v_" + ref] for k, ref, _ in _SMALL}
    synced = _small_sync(_pack_small(gs, loss_row), _pack_small(small_w), _pack_small(sm), _pack_small(sv))
    chip_stage(stage_order[-1])
    for name in stage_order:
        adam_stage(name)
    off = 0
    for k, ref, n in _SMALL:
        shape = given[ref].shape
        out[ref] = [synced[r, off:off + n].reshape(shape) for r in range(4)]
        off += n
    loss = synced[0, off]

    order = ["ln_mix_w", "w_in", "hg_norm_w", "hg_lower_bounds", "w_branch_a", "w_branch_b", "w_out", "ln_cross_w", "ln_mem_w",
             "wq_cross", "wkv_cross", "wo_cross", "ln_ffn_w", "w1", "w3", "w2", "ln_final_w"]
    return (loss, grad_x[None], *[out[n][0] for n in order], *[out[n][1] for n in order],
            *[out[n][2] for n in order], *[out[n][3] for n in order])
```

```python
import functools
import math

import jax
import jax.numpy as jnp
from jax import lax
from jax.experimental import pallas as pl
from jax.experimental.pallas import tpu as pltpu
from jax.experimental.pallas import tpu_sc as plsc

F32 = jnp.float32
BF = jnp.bfloat16
SDS = jax.ShapeDtypeStruct

D_MODEL = 2048
SEQ = 2048
HEAD_DIM = 128
MEM_LEN = 256
ATT_GROUPS = ((128, 1), (512, 4), (2048, 16))
ATT_HEADS = 4
ATT_WIDTH = 1536
ATT_OUT = 512
HG_HEADS = 8
HG_WIDTH = 1024
HG_CHUNK = 64
IN_WIDTH = 12800
CROSS_HEADS = 4
CROSS_WIDTH = 512
D_FF = 5632
RMS_EPS = 1e-6
ADAM_LR = 0.001
ADAM_B1 = 0.9
ADAM_B2 = 0.999
ADAM_EPS = 1e-08
ADAM_WD = 0.01
ADAM_STEP = 10
N_DEV = 8

LANE = 128
QA0, KA0, VA0 = 0, 12, 24
QH0, FH0, IH0, GH0 = 36, 44, 52, 60
GA0, GB0 = 68, 84

VMEM_BIG = 56 * 1024 * 1024
MESH = pl.DeviceIdType.MESH


def _cp(vmem=None, **kw):
    if vmem is not None:
        kw["vmem_limit_bytes"] = vmem
    return pltpu.CompilerParams(**kw)


def _pick(n, cands):
    for c in cands:
        if n % c == 0:
            return c
    raise ValueError(f"no tile for {n}")


_ANY = pl.BlockSpec(memory_space=pl.ANY)
_ORDER = [None]


def _pcall(body, *, name, in_specs, out_specs, out_shape, grid=(), scratch_shapes=(), compiler_params=None, prefetch=False):
    def run(*args):
        dep = _ORDER[0]
        specs, operands, kernel_fn = list(in_specs), list(args), body
        if dep is not None:
            at = len(operands)
            specs.append(_ANY)
            operands.append(dep)

            def kernel_fn(*refs):
                return body(*refs[:at], *refs[at + 1:])
        if prefetch:
            call = pl.pallas_call(
                kernel_fn, out_shape=out_shape, name=name, compiler_params=compiler_params,
                grid_spec=pltpu.PrefetchScalarGridSpec(num_scalar_prefetch=1, grid=grid, in_specs=specs, out_specs=out_specs,
                                                       scratch_shapes=scratch_shapes))
        else:
            call = pl.pallas_call(kernel_fn, grid=grid, in_specs=specs, out_specs=out_specs, out_shape=out_shape,
                                  scratch_shapes=scratch_shapes, compiler_params=compiler_params, name=name)
        out = call(*operands)
        _ORDER[0] = out[0] if isinstance(out, (list, tuple)) else out
        return out
    return run


def _mm(a, b, form, out_dtype, name, res=None, tiles=None):
    if form == "nn":
        (M, K), (K2, N) = a.shape, b.shape
    elif form == "nt":
        (M, K), (N, K2) = a.shape, b.shape
    else:
        (K, M), (K2, N) = a.shape, b.shape
    assert K == K2, (a.shape, b.shape, form)
    if tiles is None:
        tm = _pick(M, (1024, 1280, 1408, 512, 256))
        tn = _pick(N, (1280, 1024, 1408, 512, 256))
        tk = K if K <= 2048 else _pick(K, (2560, 1408))
    else:
        tm, tn, tk = tiles
    nk = K // tk
    if form == "nn":
        a_spec = pl.BlockSpec((tm, tk), lambda i, j, k: (i, k))
        b_spec = pl.BlockSpec((tk, tn), lambda i, j, k: (k, j))
        dims = ((1,), (0,))
    elif form == "nt":
        a_spec = pl.BlockSpec((tm, tk), lambda i, j, k: (i, k))
        b_spec = pl.BlockSpec((tn, tk), lambda i, j, k: (j, k))
        dims = ((1,), (1,))
    else:
        a_spec = pl.BlockSpec((tk, tm), lambda i, j, k: (k, i))
        b_spec = pl.BlockSpec((tk, tn), lambda i, j, k: (k, j))
        dims = ((0,), (0,))
    o_spec = pl.BlockSpec((tm, tn), lambda i, j, k: (i, j))
    in_specs = [a_spec, b_spec]
    args = [a, b]
    if res is not None:
        in_specs.append(o_spec)
        args.append(res)

    def body(*refs):
        a_ref, b_ref = refs[0], refs[1]
        r_ref = refs[2] if res is not None else None
        o_ref = refs[3] if res is not None else refs[2]
        acc = refs[-1] if nk > 1 else None
        k = pl.program_id(2)
        d = lax.dot_general(a_ref[...].astype(BF), b_ref[...].astype(BF), (dims, ((), ())), preferred_element_type=F32)

        def finish(r):
            if res is not None:
                r = r + r_ref[...].astype(F32)
            o_ref[...] = r.astype(o_ref.dtype)

        if nk == 1:
            finish(d)
        else:
            @pl.when(k == 0)
            def _():
                acc[...] = d

            @pl.when((k > 0) & (k < nk - 1))
            def _():
                acc[...] += d

            @pl.when(k == nk - 1)
            def _():
                finish(acc[...] + d)

    return _pcall(
        body, grid=(M // tm, N // tn, nk), in_specs=in_specs, out_specs=o_spec,
        out_shape=SDS((M, N), out_dtype), scratch_shapes=[pltpu.VMEM((tm, tn), F32)] if nk > 1 else [],
        compiler_params=_cp(VMEM_BIG, dimension_semantics=("parallel", "parallel", "arbitrary")),
        name=name)(*args)


def _rms(x, w):
    return x * lax.rsqrt(jnp.mean(x * x, axis=-1, keepdims=True) + RMS_EPS) * w


def _rms_fwd(x, w, name):
    R, D = x.shape
    tr = 256

    def body(x_ref, w_ref, o_ref):
        o_ref[...] = _rms(x_ref[...], w_ref[...]).astype(o_ref.dtype)

    return _pcall(
        body, grid=(R // tr,),
        in_specs=[pl.BlockSpec((tr, D), lambda i: (i, 0)), pl.BlockSpec((1, D), lambda i: (0, 0))],
        out_specs=pl.BlockSpec((tr, D), lambda i: (i, 0)), out_shape=SDS((R, D), BF), name=name)(x, w)


def _rms_bwd(x, w, dh, dres, name):
    R, D = x.shape
    tr = 256
    want_dx = dres is not None

    def body(*refs):
        if want_dx:
            x_ref, w_ref, dh_ref, dr_ref, dx_ref, dxb_ref, dw_ref = refs
        else:
            x_ref, w_ref, dh_ref, dw_ref = refs

        @pl.when(pl.program_id(0) == 0)
        def _():
            dw_ref[...] = jnp.zeros_like(dw_ref)

        _, vjp = jax.vjp(_rms, x_ref[...], w_ref[...])
        dx, dw = vjp(dh_ref[...].astype(F32))
        dw_ref[...] += dw
        if want_dx:
            dx = dx + dr_ref[...]
            dx_ref[...] = dx
            dxb_ref[...] = dx.astype(BF)

    row = pl.BlockSpec((tr, D), lambda i: (i, 0))
    vec = pl.BlockSpec((1, D), lambda i: (0, 0))
    if want_dx:
        return _pcall(body, grid=(R // tr,), in_specs=[row, vec, row, row], out_specs=[row, row, vec],
                              out_shape=[SDS((R, D), F32), SDS((R, D), BF), SDS((1, D), F32)], name=name)(x, w, dh, dres)
    return _pcall(body, grid=(R // tr,), in_specs=[row, vec, row], out_specs=vec,
                          out_shape=SDS((1, D), F32), name=name)(x, w, dh)


def _final_loss(x3, w, tgt):
    R, D = x3.shape
    tr = 256

    def loss_fn(xv, wv, tv):
        err = _rms(xv, wv) - tv
        return 0.5 * jnp.sum(jnp.mean(err * err, axis=-1))

    def body(x_ref, w_ref, t_ref, loss_ref, dx_ref, dxb_ref, dw_ref):
        @pl.when(pl.program_id(0) == 0)
        def _():
            dw_ref[...] = jnp.zeros_like(dw_ref)
            loss_ref[...] = jnp.zeros_like(loss_ref)

        tv = t_ref[...]
        val, vjp = jax.vjp(lambda a, b: loss_fn(a, b, tv), x_ref[...], w_ref[...])
        dx, dw = vjp(jnp.ones((), F32))
        dx_ref[...] = dx
        dxb_ref[...] = dx.astype(BF)
        dw_ref[...] += dw
        loss_ref[...] += jnp.full(loss_ref.shape, val, F32)

    row = pl.BlockSpec((tr, D), lambda i: (i, 0))
    vec = pl.BlockSpec((1, D), lambda i: (0, 0))
    one = pl.BlockSpec((1, LANE), lambda i: (0, 0))
    return _pcall(body, grid=(R // tr,), in_specs=[row, vec, row], out_specs=[one, row, row, vec],
                          out_shape=[SDS((1, LANE), F32), SDS((R, D), F32), SDS((R, D), BF), SDS((1, D), F32)],
                          name="final_loss")(x3, w, tgt)


NEG = -1e30


def _att_blocks(S, d):
    L = S // d
    nb = L // LANE
    return nb, (2 if nb > 1 else 1)


def _att_rows(S, d, idx):
    nb, nkb = _att_blocks(S, d)
    r = idx // nb
    n = idx % nb
    kb = jnp.maximum(n - 1, 0)
    if d == 1:
        qrows = pl.ds(pl.multiple_of(n * LANE, LANE), LANE)
        krows = pl.ds(pl.multiple_of(kb * LANE, LANE), LANE * nkb)
    else:
        qrows = pl.ds(r + n * (LANE * d), LANE, stride=d)
        krows = pl.ds(r + kb * (LANE * d), LANE * nkb, stride=d)
    qpos = n * LANE + lax.broadcasted_iota(jnp.int32, (LANE, LANE * nkb), 0)
    kpos = kb * LANE + lax.broadcasted_iota(jnp.int32, (LANE, LANE * nkb), 1)
    return qrows, krows, qpos - kpos


def _slab(col):
    return pl.BlockSpec((SEQ, LANE), lambda h, col=col: (0, col + h))


def _attn_fwd(proj):
    S = proj.shape[0]
    scale = HEAD_DIM ** -0.5

    def body(q0, q1, q2, k0, k1, k2, v0, v1, v2, o_ref, lse_ref, og, lg):
        qs, ks, vs = (q0, q1, q2), (k0, k1, k2), (v0, v1, v2)
        for g, (window, d) in enumerate(ATT_GROUPS):
            def blk(idx, carry, g=g, window=window, d=d):
                qrows, krows, dist = _att_rows(S, d, idx)
                q = qs[g][qrows, :].astype(BF)
                k = ks[g][krows, :].astype(BF)
                v = vs[g][krows, :].astype(BF)
                s = lax.dot_general(q, k, (((1,), (1,)), ((), ())), preferred_element_type=F32) * scale
                s = jnp.where((dist >= 0) & (dist <= window // d), s, NEG)
                m = jnp.max(s, axis=-1, keepdims=True)
                p = jnp.exp(s - m)
                l = jnp.sum(p, axis=-1, keepdims=True)
                o = jnp.dot((p / l).astype(BF), v, preferred_element_type=F32)
                og[g, qrows, :] = o
                lg[g, qrows, :] = jnp.broadcast_to(m + jnp.log(l), (LANE, LANE))
                return carry
            lax.fori_loop(0, S // LANE, blk, 0)

        def merge(t, carry):
            rows = pl.ds(pl.multiple_of(t * 256, 256), 256)
            l0, l1, l2 = lg[0, rows, :], lg[1, rows, :], lg[2, rows, :]
            m = jnp.maximum(jnp.maximum(l0, l1), l2)
            w0, w1, w2 = jnp.exp(l0 - m), jnp.exp(l1 - m), jnp.exp(l2 - m)
            den = w0 + w1 + w2
            o_ref[rows, :] = (w0 * og[0, rows, :] + w1 * og[1, rows, :] + w2 * og[2, rows, :]) / den
            lse_ref[rows, :] = m + jnp.log(den)
            return carry
        lax.fori_loop(0, S // 256, merge, 0)

    ins = [_slab(QA0 + 4 * g) for g in range(3)] + [_slab(KA0 + 4 * g) for g in range(3)] + [_slab(VA0 + 4 * g) for g in range(3)]
    out = pl.BlockSpec((SEQ, LANE), lambda h: (0, h))
    return _pcall(
        body, grid=(ATT_HEADS,), in_specs=ins, out_specs=[out, out],
        out_shape=[SDS((S, ATT_OUT), F32), SDS((S, ATT_OUT), F32)],
        scratch_shapes=[pltpu.VMEM((3, S, LANE), F32), pltpu.VMEM((3, S, LANE), F32)],
        compiler_params=_cp(VMEM_BIG), name="attn_fwd")(*([proj] * 9))


def _attn_bwd(proj, o_att, lse, do_att):
    S = proj.shape[0]
    scale = HEAD_DIM ** -0.5

    def body(q0, q1, q2, k0, k1, k2, v0, v1, v2, o_ref, lse_ref, do_ref,
             dq0, dq1, dq2, dk0, dk1, dk2, dv0, dv1, dv2, dqa, dka, dva, delta):
        qs, ks, vs = (q0, q1, q2), (k0, k1, k2), (v0, v1, v2)
        dqs, dks, dvs = (dq0, dq1, dq2), (dk0, dk1, dk2), (dv0, dv1, dv2)
        delta[...] = jnp.broadcast_to(jnp.sum(do_ref[...] * o_ref[...], axis=-1, keepdims=True), delta.shape)
        for g, (window, d) in enumerate(ATT_GROUPS):
            dka[...] = jnp.zeros_like(dka)
            dva[...] = jnp.zeros_like(dva)

            def blk(idx, carry, g=g, window=window, d=d):
                qrows, krows, dist = _att_rows(S, d, idx)
                q = qs[g][qrows, :].astype(BF)
                k = ks[g][krows, :].astype(BF)
                v = vs[g][krows, :].astype(BF)
                do = do_ref[qrows, :]
                s = lax.dot_general(q, k, (((1,), (1,)), ((), ())), preferred_element_type=F32) * scale
                p = jnp.exp(s - lse_ref[qrows, :][:, 0:1])
                p = jnp.where((dist >= 0) & (dist <= window // d), p, 0.0)
                dob = do.astype(BF)
                dva[krows, :] += lax.dot_general(p.astype(BF), dob, (((0,), (0,)), ((), ())), preferred_element_type=F32)
                dp = lax.dot_general(dob, v, (((1,), (1,)), ((), ())), preferred_element_type=F32)
                ds = (p * (dp - delta[qrows, :][:, 0:1]) * scale).astype(BF)
                dqa[qrows, :] = jnp.dot(ds, k, preferred_element_type=F32)
                dka[krows, :] += lax.dot_general(ds, q, (((0,), (0,)), ((), ())), preferred_element_type=F32)
                return carry
            lax.fori_loop(0, S // LANE, blk, 0)
            dqs[g][...] = dqa[...].astype(dqs[g].dtype)
            dks[g][...] = dka[...].astype(dks[g].dtype)
            dvs[g][...] = dva[...].astype(dvs[g].dtype)

    cols = [QA0 + 4 * g for g in range(3)] + [KA0 + 4 * g for g in range(3)] + [VA0 + 4 * g for g in range(3)]
    ins = [_slab(c) for c in cols]
    one = pl.BlockSpec((SEQ, LANE), lambda h: (0, h))
    return _pcall(
        body, grid=(ATT_HEADS,), in_specs=ins + [one, one, one], out_specs=[one] * 9,
        out_shape=[SDS((S, ATT_OUT), BF)] * 9,
        scratch_shapes=[pltpu.VMEM((S, LANE), F32)] * 4,
        compiler_params=_cp(VMEM_BIG), name="attn_bwd")(*([proj] * 9), o_att, lse, do_att)


def _bdot(a, b, dims):
    return lax.dot_general(a.astype(BF), b.astype(BF), (dims, ((), ())), preferred_element_type=F32)


def _hg_chunk(qh, fh, ih, gh, lbraw, nw, st):
    C = HG_CHUNK
    mx = jnp.maximum(lbraw[0:1], lbraw[1:2])
    e0 = jnp.exp(lbraw[0:1] - mx)
    e1 = jnp.exp(lbraw[1:2] - mx)
    lb = e0 / (e0 + e1)
    f = lb + (1.0 - lb) * jax.nn.sigmoid(fh)
    logf = jnp.log(f)
    k = 1.0 - f
    q = qh * jax.nn.sigmoid(qh)
    causal = lax.broadcasted_iota(jnp.int32, (C, C), 0) >= lax.broadcasted_iota(jnp.int32, (C, C), 1)
    b = lax.dot_general(causal.astype(F32), logf, (((1,), (0,)), ((), ())), precision=lax.Precision.HIGHEST,
                        preferred_element_type=F32)
    b_last = b[C - 1:C]
    st_out = st * jnp.exp(b_last) + _bdot(ih, k * jnp.exp(b_last - b), ((0,), (0,)))
    inter = _bdot(q * jnp.exp(b), st, ((1,), (1,)))
    b_ref = b[C // 2:C // 2 + 1]
    a = _bdot(q * jnp.exp(b - b_ref), k * jnp.exp(b_ref - b), ((1,), (1,)))
    a = jnp.where(causal, a, 0.0)
    o = inter + _bdot(a, ih, ((1,), (0,)))
    o = o * lax.rsqrt(jnp.mean(o * o, axis=-1, keepdims=True) + RMS_EPS) * nw
    return o * (gh * jax.nn.sigmoid(gh)), st_out


def _head_cols(h):
    return slice(h * LANE, (h + 1) * LANE)


def _hg_in_specs(chunk_of):
    half = 4 * LANE
    specs = []
    for base in (QH0, FH0, IH0, GH0):
        for part in range(2):
            specs.append(pl.BlockSpec((HG_CHUNK, half), lambda n, col=base // 4 + part: (chunk_of(n), col)))
    return specs


def _hg_fwd(proj, lb, nw):
    S = proj.shape[0]
    C = HG_CHUNK
    NC = S // C

    def body(*refs):
        sec = refs[:8]
        lb_ref, nw_ref, o_ref, st_ref, st_scr = refs[8:]

        @pl.when(pl.program_id(0) == 0)
        def _():
            st_scr[...] = jnp.zeros_like(st_scr)

        for h in range(HG_HEADS):
            half, hs = h // 4, _head_cols(h % 4)
            st = st_scr[h]
            st_ref[h, 0] = st
            o, st_out = _hg_chunk(sec[half][:, hs], sec[2 + half][:, hs], sec[4 + half][:, hs], sec[6 + half][:, hs],
                                  lb_ref[:, _head_cols(h)], nw_ref[...], st)
            o_ref[:, _head_cols(h)] = o.astype(o_ref.dtype)
            st_scr[h] = st_out

    return _pcall(
        body, grid=(NC,),
        in_specs=_hg_in_specs(lambda n: n) + [pl.BlockSpec((2, HG_WIDTH), lambda n: (0, 0)), pl.BlockSpec((1, LANE), lambda n: (0, 0))],
        out_specs=[pl.BlockSpec((C, HG_WIDTH), lambda n: (n, 0)), pl.BlockSpec((HG_HEADS, 1, LANE, LANE), lambda n: (0, n, 0, 0))],
        out_shape=[SDS((S, HG_WIDTH), BF), SDS((HG_HEADS, NC, LANE, LANE), F32)],
        scratch_shapes=[pltpu.VMEM((HG_HEADS, LANE, LANE), F32)], name="hg_fwd")(*([proj] * 8), lb, nw)


def _hg_bwd(proj, lb, nw, states, do_hg):
    S = proj.shape[0]
    C = HG_CHUNK
    NC = S // C

    def body(*refs):
        sec = refs[:8]
        lb_ref, nw_ref, st_ref, do_ref, dq_ref, df_ref, di_ref, dg_ref, dlb_ref, dnw_ref, dst_scr = refs[8:]

        @pl.when(pl.program_id(0) == 0)
        def _():
            dst_scr[...] = jnp.zeros_like(dst_scr)
            dlb_ref[...] = jnp.zeros_like(dlb_ref)
            dnw_ref[...] = jnp.zeros_like(dnw_ref)

        dnw_sum = jnp.zeros((1, LANE), F32)
        for h in range(HG_HEADS):
            half, hs, hc = h // 4, _head_cols(h % 4), _head_cols(h)
            _, vjp = jax.vjp(_hg_chunk, sec[half][:, hs], sec[2 + half][:, hs], sec[4 + half][:, hs], sec[6 + half][:, hs],
                             lb_ref[:, hc], nw_ref[...], st_ref[h, 0])
            dq, df, di, dg, dlb, dnw, dst = vjp((do_ref[:, hc].astype(F32), dst_scr[h]))
            dq_ref[:, hc] = dq.astype(dq_ref.dtype)
            df_ref[:, hc] = df.astype(df_ref.dtype)
            di_ref[:, hc] = di.astype(di_ref.dtype)
            dg_ref[:, hc] = dg.astype(dg_ref.dtype)
            dlb_ref[:, hc] += dlb
            dnw_sum = dnw_sum + dnw
            dst_scr[h] = dst
        dnw_ref[...] += dnw_sum

    rev = lambda n: NC - 1 - n
    ospec = pl.BlockSpec((C, HG_WIDTH), lambda n: (rev(n), 0))
    lbs = pl.BlockSpec((2, HG_WIDTH), lambda n: (0, 0))
    nws = pl.BlockSpec((1, LANE), lambda n: (0, 0))
    outs = _pcall(
        body, grid=(NC,),
        in_specs=_hg_in_specs(rev) + [lbs, nws, pl.BlockSpec((HG_HEADS, 1, LANE, LANE), lambda n: (0, rev(n), 0, 0)), ospec],
        out_specs=[ospec] * 4 + [lbs, nws],
        out_shape=[SDS((S, HG_WIDTH), BF)] * 4 + [SDS((2, HG_WIDTH), F32), SDS((1, LANE), F32)],
        scratch_shapes=[pltpu.VMEM((HG_HEADS, LANE, LANE), F32)], name="hg_bwd")(*([proj] * 8), lb, nw, states, do_hg)
    return outs


def _merge(ga, gb, ba, bb):
    return jax.nn.sigmoid(ga) * ba + jax.nn.sigmoid(gb) * bb


_MT = 512


def _gate_specs():
    ga = pl.BlockSpec((_MT, _MT), lambda i, j: (i, GA0 * LANE // _MT + j))
    gb = pl.BlockSpec((_MT, _MT), lambda i, j: (i, GB0 * LANE // _MT + j))
    t = pl.BlockSpec((_MT, _MT), lambda i, j: (i, j))
    return ga, gb, t


def _merge_fwd(proj, ba, bb):
    S, D = ba.shape
    ga, gb, t = _gate_specs()

    def body(ga_ref, gb_ref, ba_ref, bb_ref, o_ref):
        o_ref[...] = _merge(ga_ref[...], gb_ref[...], ba_ref[...], bb_ref[...]).astype(o_ref.dtype)

    return _pcall(body, grid=(S // _MT, D // _MT), in_specs=[ga, gb, t, t], out_specs=t,
                          out_shape=SDS((S, D), BF), name="merge_fwd")(proj, proj, ba, bb)


def _merge_bwd(proj, ba, bb, dm):
    S, D = ba.shape
    ga, gb, t = _gate_specs()

    def body(ga_ref, gb_ref, ba_ref, bb_ref, dm_ref, dga_ref, dgb_ref, dba_ref, dbb_ref):
        _, vjp = jax.vjp(_merge, ga_ref[...], gb_ref[...], ba_ref[...], bb_ref[...])
        dga, dgb, dba, dbb = vjp(dm_ref[...])
        dga_ref[...] = dga.astype(BF)
        dgb_ref[...] = dgb.astype(BF)
        dba_ref[...] = dba.astype(BF)
        dbb_ref[...] = dbb.astype(BF)

    return _pcall(body, grid=(S // _MT, D // _MT), in_specs=[ga, gb, t, t, t], out_specs=[t] * 4,
                          out_shape=[SDS((S, D), BF)] * 4, name="merge_bwd")(proj, proj, ba, bb, dm)


def _swiglu(a, b):
    return a * jax.nn.sigmoid(a) * b


def _swiglu_fwd(a, b):
    S, F = a.shape
    tf = _pick(F, (1408, 512))
    t = pl.BlockSpec((_MT, tf), lambda i, j: (i, j))

    def body(a_ref, b_ref, o_ref):
        o_ref[...] = _swiglu(a_ref[...], b_ref[...]).astype(o_ref.dtype)

    return _pcall(body, grid=(S // _MT, F // tf), in_specs=[t, t], out_specs=t,
                          out_shape=SDS((S, F), BF), name="swiglu_fwd")(a, b)


def _swiglu_bwd(a, b, dg):
    S, F = a.shape
    tf = _pick(F, (1408, 512))
    t = pl.BlockSpec((_MT, tf), lambda i, j: (i, j))

    def body(a_ref, b_ref, dg_ref, da_ref, db_ref):
        _, vjp = jax.vjp(_swiglu, a_ref[...], b_ref[...])
        da, db = vjp(dg_ref[...].astype(F32))
        da_ref[...] = da.astype(BF)
        db_ref[...] = db.astype(BF)

    return _pcall(body, grid=(S // _MT, F // tf), in_specs=[t, t, t], out_specs=[t, t],
                          out_shape=[SDS((S, F), BF)] * 2, name="swiglu_bwd")(a, b, dg)


def _cross(q, k, v):
    s = _bdot(q, k, ((1,), (1,))) * (HEAD_DIM ** -0.5)
    e = jnp.exp(s - jnp.max(s, axis=-1, keepdims=True))
    p = e / jnp.sum(e, axis=-1, keepdims=True)
    return _bdot(p, v, ((1,), (0,)))


def _cross_specs():
    q = pl.BlockSpec((_MT, LANE), lambda h, i: (i, h))
    k = pl.BlockSpec((MEM_LEN, LANE), lambda h, i: (0, h))
    v = pl.BlockSpec((MEM_LEN, LANE), lambda h, i: (0, CROSS_HEADS + h))
    return q, k, v


def _cross_fwd(qc, kvc):
    S = qc.shape[0]
    q, k, v = _cross_specs()

    def body(q_ref, k_ref, v_ref, o_ref):
        o_ref[...] = _cross(q_ref[...], k_ref[...], v_ref[...]).astype(o_ref.dtype)

    return _pcall(body, grid=(CROSS_HEADS, S // _MT), in_specs=[q, k, v], out_specs=q,
                          out_shape=SDS((S, CROSS_WIDTH), BF), name="cross_fwd")(qc, kvc, kvc)


def _cross_bwd(qc, kvc, doc):
    S = qc.shape[0]
    q, k, v = _cross_specs()

    def body(q_ref, k_ref, v_ref, do_ref, dq_ref, dk_ref, dv_ref):
        @pl.when(pl.program_id(1) == 0)
        def _():
            dk_ref[...] = jnp.zeros_like(dk_ref)
            dv_ref[...] = jnp.zeros_like(dv_ref)

        _, vjp = jax.vjp(_cross, q_ref[...], k_ref[...], v_ref[...])
        dq, dk, dv = vjp(do_ref[...])
        dq_ref[...] = dq.astype(dq_ref.dtype)
        dk_ref[...] += dk
        dv_ref[...] += dv

    return _pcall(body, grid=(CROSS_HEADS, S // _MT), in_specs=[q, k, v, q], out_specs=[q, k, k],
                          out_shape=[SDS((S, CROSS_WIDTH), BF), SDS((MEM_LEN, CROSS_WIDTH), F32), SDS((MEM_LEN, CROSS_WIDTH), F32)],
                          name="cross_bwd")(qc, kvc, kvc, doc)


def _local_step(x, mem, tgt, p, ready):
    h = _rms_fwd(x, p["ln_mix"], "rms_mix")
    proj = _mm(h, p["w_in"], "nt", F32, "mm_proj")
    o_att, lse = _attn_fwd(proj)
    o_hg, states = _hg_fwd(proj, p["hg_lb"], p["hg_norm"])
    ba = _mm(o_att, p["w_ba"], "nn", F32, "mm_ba")
    bb = _mm(o_hg, p["w_bb"], "nn", F32, "mm_bb")
    merged = _merge_fwd(proj, ba, bb)
    x1 = _mm(merged, p["w_out"], "nn", F32, "mm_out", res=x)
    hc = _rms_fwd(x1, p["ln_cross"], "rms_cross")
    qc = _mm(hc, p["wq"], "nn", F32, "mm_q")
    mn = _rms_fwd(mem, p["ln_mem"], "rms_mem")
    kvc = _mm(mn, p["wkv"], "nn", F32, "mm_kv")
    oc = _cross_fwd(qc, kvc)
    x2 = _mm(oc, p["wo"], "nn", F32, "mm_o", res=x1)
    hf = _rms_fwd(x2, p["ln_ffn"], "rms_ffn")
    a = _mm(hf, p["w1"], "nt", F32, "mm_w1")
    b = _mm(hf, p["w3"], "nt", F32, "mm_w3")
    g = _swiglu_fwd(a, b)
    x3 = _mm(g, p["w2"], "nn", F32, "mm_w2", res=x2)
    loss, dx3, dx3b, d_ln_final = _final_loss(x3, p["ln_final"], tgt)

    gw, gs = {}, {"ln_final": d_ln_final}
    dg = _mm(dx3b, p["w2"], "nt", BF, "mm_dg")
    gw["w2"] = _mm(g, dx3b, "tn", BF, "mm_dw2")
    da, db = _swiglu_bwd(a, b, dg)
    dhf = _mm(da, p["w1"], "nn", F32, "mm_dhf1")
    dhf = _mm(db, p["w3"], "nn", F32, "mm_dhf3", res=dhf)
    gw["w1"] = _mm(da, hf, "tn", BF, "mm_dw1")
    gw["w3"] = _mm(db, hf, "tn", BF, "mm_dw3")
    ready("ffn", gw)
    dx2, dx2b, gs["ln_ffn"] = _rms_bwd(x2, p["ln_ffn"], dhf, dx3, "rms_ffn_bwd")
    doc = _mm(dx2b, p["wo"], "nt", F32, "mm_doc")
    gw["wo"] = _mm(oc, dx2b, "tn", BF, "mm_dwo")
    dqc, dk, dv = _cross_bwd(qc, kvc, doc)
    dkvc = jnp.concatenate([dk, dv], axis=1)
    dhc = _mm(dqc, p["wq"], "nt", F32, "mm_dhc")
    gw["wq"] = _mm(hc, dqc, "tn", BF, "mm_dwq")
    dmn = _mm(dkvc, p["wkv"], "nt", F32, "mm_dmn")
    gw["wkv"] = _mm(mn, dkvc, "tn", BF, "mm_dwkv")
    ready("cross", gw)
    gs["ln_mem"] = _rms_bwd(mem, p["ln_mem"], dmn, None, "rms_mem_bwd")
    dx1, dx1b, gs["ln_cross"] = _rms_bwd(x1, p["ln_cross"], dhc, dx2, "rms_cross_bwd")
    dmerged = _mm(dx1b, p["w_out"], "nt", F32, "mm_dmerged")
    gw["w_out"] = _mm(merged, dx1b, "tn", BF, "mm_dwout")
    dga, dgb, dba, dbb = _merge_bwd(proj, ba, bb, dmerged)
    do_att = _mm(dba, p["w_ba"], "nt", F32, "mm_doatt")
    gw["w_ba"] = _mm(o_att, dba, "tn", BF, "mm_dwba")
    do_hg = _mm(dbb, p["w_bb"], "nt", F32, "mm_dohg")
    gw["w_bb"] = _mm(o_hg, dbb, "tn", BF, "mm_dwbb")
    ready("mix", gw)
    datt = _attn_bwd(proj, o_att, lse, do_att)
    dqh, dfh, dih, dgh, gs["hg_lb"], gs["hg_norm"] = _hg_bwd(proj, p["hg_lb"], p["hg_norm"], states, do_hg)
    ready("mixers_done", gw)
    dproj = jnp.concatenate([*datt, dqh, dfh, dih, dgh, dga, dgb], axis=1)
    gw["w_in"] = _mm(dproj, h, "tn", BF, "mm_dwin")
    ready("in", gw)
    dh = _mm(dproj, p["w_in"], "nn", F32, "mm_dh")
    grad_x, _, gs["ln_mix"] = _rms_bwd(x, p["ln_mix"], dh, dx1, "rms_mix_bwd")
    return loss, grad_x, gs


_MATS = (
    ("w_in", "w_in", IN_WIDTH, D_MODEL, "row", IN_WIDTH // N_DEV, True),
    ("w_ba", "w_branch_a", ATT_OUT, D_MODEL, "col", D_MODEL // N_DEV, False),
    ("w_bb", "w_branch_b", HG_WIDTH, D_MODEL, "col", D_MODEL // N_DEV, False),
    ("w_out", "w_out", D_MODEL, D_MODEL, "row", D_MODEL // N_DEV, False),
    ("wq", "wq_cross", D_MODEL, CROSS_WIDTH, "row", D_MODEL // N_DEV, False),
    ("wkv", "wkv_cross", D_MODEL, 2 * CROSS_WIDTH, "row", D_MODEL // N_DEV, False),
    ("wo", "wo_cross", CROSS_WIDTH, D_MODEL, "col", D_MODEL // N_DEV, False),
    ("w1", "w1", D_FF, D_MODEL, "row", D_FF // N_DEV, True),
    ("w3", "w3", D_FF, D_MODEL, "row", D_FF // N_DEV, True),
    ("w2", "w2", D_FF, D_MODEL, "row", D_FF // N_DEV, False),
)
_INDEX = {m[0]: i for i, m in enumerate(_MATS)}


def _unit_shape(i):
    _, _, K, N, kind, sh, _ = _MATS[i]
    return (sh, N) if kind == "row" else (K, sh)


def _win(ref, i, chip, par):
    _, _, _, _, kind, sh, _ = _MATS[i]
    if kind == "row":
        return ref.at[pl.ds(pl.multiple_of((2 * chip + par) * sh, 16), sh), :]
    return ref.at[:, pl.ds(pl.multiple_of((2 * chip + par) * sh, LANE), sh)]


def _my_place():
    x, y, c = lax.axis_index("x"), lax.axis_index("y"), lax.axis_index("c")
    chips = [(1 - x, y), (x, 1 - y), (1 - x, 1 - y)]
    return x, y, c, chips


def _cast_unit(w, name):
    K, N = w.shape
    tr = _pick(K, (256, 400, 352))

    def body(w_ref, o_ref):
        o_ref[...] = w_ref[...].astype(BF)

    blk = pl.BlockSpec((tr, N), lambda i: (i, 0))
    return _pcall(body, grid=(K // tr,), in_specs=[blk], out_specs=blk, out_shape=SDS((K, N), BF), name=name)(w)


def _handshake(peers):
    barrier = pltpu.get_barrier_semaphore()
    for peer in peers:
        pl.semaphore_signal(barrier, inc=1, device_id=peer, device_id_type=MESH)
    pl.semaphore_wait(barrier, len(peers))


def _sequencer(body, out_type, sems, cid, name):
    return pl.kernel(body, out_type=out_type, mesh=plsc.ScalarSubcoreMesh(axis_name="seq", num_cores=1),
                     scratch_types=sems, compiler_params=pltpu.CompilerParams(collective_id=cid), name=name)


def _all_gather(units, idx, cid, name):
    n = len(idx)

    def body(*refs):
        us, fulls = refs[:n], refs[n:2 * n]
        ssem, rsem, lsem = refs[2 * n:]
        x, y, c, chips = _my_place()
        _handshake([(x, y, 1 - c)] + [(*ch, c) for ch in chips])

        def win(i, chip_xy, par):
            return _win(fulls[i], idx[i], 2 * chip_xy[0] + chip_xy[1], par)

        def rcopy(i, k, src, dst, to):
            return pltpu.make_async_remote_copy(src_ref=src, dst_ref=dst, send_sem=ssem.at[i, k], recv_sem=rsem.at[i, k],
                                                device_id=to, device_id_type=MESH)

        sends, locs = [], []
        for i in range(n):
            mine = win(i, (x, y), c)
            loc = pltpu.make_async_copy(us[i], mine, lsem.at[i])
            loc.start()
            locs.append(loc)
            first = [rcopy(i, 0, us[i], mine, (x, y, 1 - c))]
            first += [rcopy(i, 1 + j, us[i], mine, (*chips[j], c)) for j in range(3)]
            for cp in first:
                cp.start()
            sends += first
        for i in range(n):
            for j in range(3):
                blk = win(i, chips[j], c)
                rcopy(i, 1 + j, blk, blk, (x, y, c)).wait_recv()
                fwd = rcopy(i, 4 + j, blk, blk, (x, y, 1 - c))
                fwd.start()
                sends.append(fwd)
        for i in range(n):
            blk = win(i, (x, y), 1 - c)
            rcopy(i, 0, blk, blk, (x, y, c)).wait_recv()
            for j in range(3):
                blk = win(i, chips[j], 1 - c)
                rcopy(i, 4 + j, blk, blk, (x, y, c)).wait_recv()
        for cp in sends:
            cp.wait_send()
        for loc in locs:
            loc.wait()

    out_type = [SDS((_MATS[i][2], _MATS[i][3]), BF) for i in idx]
    sems = [pltpu.SemaphoreType.DMA((n, 7)), pltpu.SemaphoreType.DMA((n, 7)), pltpu.SemaphoreType.DMA((n,))]
    return _sequencer(body, out_type, sems, cid, name)(*units)


def _rs_sibling(grads, idx, cid, name):
    n = len(idx)

    def body(*refs):
        gs, bufs = refs[:n], refs[n:2 * n]
        ssem, rsem = refs[2 * n:]
        x, y, c, _ = _my_place()
        _handshake([(x, y, 1 - c)])
        cps = []
        for i in range(n):
            for k in range(4):
                cp = pltpu.make_async_remote_copy(
                    src_ref=_win(gs[i], idx[i], k, 1 - c), dst_ref=bufs[i].at[k], send_sem=ssem.at[i, k], recv_sem=rsem.at[i, k],
                    device_id=(x, y, 1 - c), device_id_type=MESH)
                cp.start()
                cps.append(cp)
        for cp in cps:
            cp.wait()

    out_type = [SDS((4, *_unit_shape(i)), BF) for i in idx]
    sems = [pltpu.SemaphoreType.DMA((n, 4)), pltpu.SemaphoreType.DMA((n, 4))]
    return _sequencer(body, out_type, sems, cid, name)(*grads)


def _chip_sum(g, sib, i, place, name):
    _, _, K, N, kind, sh, _ = _MATS[i]

    def body(pref, g_ref, s_ref, o_ref):
        o_ref[...] = (g_ref[...].astype(F32) + s_ref[...].astype(F32)).astype(BF)

    if kind == "row":
        tr = _pick(sh, (800, 352, 256))
        per = sh // tr
        grid = (4, per)
        g_spec = pl.BlockSpec((tr, N), lambda k, r, pref: ((2 * k + pref[0]) * per + r, 0))
        u_spec = pl.BlockSpec((None, tr, N), lambda k, r, pref: (k, r, 0))
    else:
        grid = (4,)
        g_spec = pl.BlockSpec((K, sh), lambda k, pref: (0, 2 * k + pref[0]))
        u_spec = pl.BlockSpec((None, K, sh), lambda k, pref: (k, 0, 0))
    return _pcall(body, grid=grid, in_specs=[g_spec, u_spec], out_specs=u_spec, out_shape=SDS((4, *_unit_shape(i)), BF),
                  prefetch=True, name=name)(place, g, sib)


def _rs_chips(parts, idx, cid, name):
    n = len(idx)

    def body(*refs):
        ts, bufs = refs[:n], refs[n:2 * n]
        ssem, rsem = refs[2 * n:]
        x, y, c, chips = _my_place()
        _handshake([(*ch, c) for ch in chips])
        cps = []
        for i in range(n):
            for j in range(3):
                cp = pltpu.make_async_remote_copy(
                    src_ref=ts[i].at[2 * chips[j][0] + chips[j][1]], dst_ref=bufs[i].at[j], send_sem=ssem.at[i, j], recv_sem=rsem.at[i, j],
                    device_id=(*chips[j], c), device_id_type=MESH)
                cp.start()
                cps.append(cp)
        for cp in cps:
            cp.wait()

    out_type = [SDS((3, *_unit_shape(i)), BF) for i in idx]
    sems = [pltpu.SemaphoreType.DMA((n, 3)), pltpu.SemaphoreType.DMA((n, 3))]
    return _sequencer(body, out_type, sems, cid, name)(*parts)


def _adamw(w, g, m, v):
    m = ADAM_B1 * m + (1.0 - ADAM_B1) * g
    v = ADAM_B2 * v + (1.0 - ADAM_B2) * jnp.square(g)
    m_hat = m / (1.0 - ADAM_B1 ** ADAM_STEP)
    v_hat = v / (1.0 - ADAM_B2 ** ADAM_STEP)
    delta = -ADAM_LR * (m_hat / (jnp.sqrt(v_hat) + ADAM_EPS) + ADAM_WD * w)
    return delta, m, v


def _adam_mat(part, recv, w, m, v, place, name):
    R, C = w.shape
    tr = _pick(R, (256, 160, 176))

    def body(pref, t_ref, r_ref, w_ref, m_ref, v_ref, g_out, d_out, m_out, v_out):
        g = t_ref[...].astype(F32) + r_ref[0].astype(F32) + r_ref[1].astype(F32) + r_ref[2].astype(F32)
        d, mn, vn = _adamw(w_ref[...], g, m_ref[...], v_ref[...])
        g_out[...] = g
        d_out[...] = d
        m_out[...] = mn
        v_out[...] = vn

    t_spec = pl.BlockSpec((None, tr, C), lambda i, pref: (pref[1], i, 0))
    r_spec = pl.BlockSpec((3, tr, C), lambda i, pref: (0, i, 0))
    s_spec = pl.BlockSpec((tr, C), lambda i, pref: (i, 0))
    return _pcall(body, grid=(R // tr,), in_specs=[t_spec, r_spec, s_spec, s_spec, s_spec], out_specs=[s_spec] * 4,
                  out_shape=[SDS((R, C), F32)] * 4, prefetch=True, name=name)(place, part, recv, w, m, v)


_SMALL = (("ln_mix", "ln_mix_w", D_MODEL), ("ln_cross", "ln_cross_w", D_MODEL), ("ln_mem", "ln_mem_w", D_MODEL),
          ("ln_ffn", "ln_ffn_w", D_MODEL), ("ln_final", "ln_final_w", D_MODEL), ("hg_lb", "hg_lower_bounds", 2 * HG_WIDTH),
          ("hg_norm", "hg_norm_w", HEAD_DIM))
_PACK = sum(n for _, _, n in _SMALL) + LANE


def _small_sync(grow, wrow, mrow, vrow):
    def body(g_ref, w_ref, m_ref, v_ref, o_ref, gath, ssem, rsem):
        x, y, c, _ = _my_place()
        me = 4 * x + 2 * y + c
        gath[pl.ds(me, 1), :] = g_ref[...]
        cps = []
        for dlt in range(1, N_DEV):
            peer = (me + dlt) % N_DEV
            cp = pltpu.make_async_remote_copy(
                src_ref=g_ref, dst_ref=gath.at[pl.ds(me, 1), :], send_sem=ssem.at[peer], recv_sem=rsem.at[me],
                device_id=(peer // 4, (peer // 2) % 2, peer % 2), device_id_type=MESH)
            cp.start()
            cps.append(cp)
        for dlt in range(1, N_DEV):
            src = (me + dlt) % N_DEV
            pltpu.make_async_remote_copy(
                src_ref=g_ref, dst_ref=gath.at[pl.ds(src, 1), :], send_sem=ssem.at[src], recv_sem=rsem.at[src],
                device_id=(x, y, c), device_id_type=MESH).wait_recv()
        for cp in cps:
            cp.wait_send()
        g = gath[0:1, :]
        for j in range(1, N_DEV):
            g = g + gath[j:j + 1, :]
        d, mn, vn = _adamw(w_ref[...], g, m_ref[...], v_ref[...])
        o_ref[0:1, :] = g
        o_ref[1:2, :] = d
        o_ref[2:3, :] = mn
        o_ref[3:4, :] = vn

    vm = pl.BlockSpec(memory_space=pltpu.VMEM)
    return _pcall(
        body, in_specs=[vm] * 4, out_specs=vm, out_shape=SDS((4, _PACK), F32),
        scratch_shapes=[pltpu.VMEM((N_DEV, _PACK), F32), pltpu.SemaphoreType.DMA((N_DEV,)), pltpu.SemaphoreType.DMA((N_DEV,))],
        name="small_sync")(grow, wrow, mrow, vrow)


def _pack_small(d, loss_row=None):
    parts = [d[k].reshape(1, n).astype(F32) for k, _, n in _SMALL]
    parts.append(loss_row if loss_row is not None else jnp.ones((1, LANE), F32))
    return jnp.concatenate(parts, axis=1)


def kernel(x, mem, ln_mix_w, w_in, hg_norm_w, hg_lower_bounds, w_branch_a, w_branch_b, w_out, ln_cross_w, ln_mem_w, wq_cross, wkv_cross, wo_cross, ln_ffn_w, w1, w3, w2, ln_final_w, loss_target, m_ln_mix_w, m_w_in, m_hg_norm_w, m_hg_lower_bounds, m_w_branch_a, m_w_branch_b, m_w_out, m_ln_cross_w, m_ln_mem_w, m_wq_cross, m_wkv_cross, m_wo_cross, m_ln_ffn_w, m_w1, m_w3, m_w2, m_ln_final_w, v_ln_mix_w, v_w_in, v_hg_norm_w, v_hg_lower_bounds, v_w_branch_a, v_w_branch_b, v_w_out, v_ln_cross_w, v_ln_mem_w, v_wq_cross, v_wkv_cross, v_wo_cross, v_ln_ffn_w, v_w1, v_w3, v_w2, v_ln_final_w):
    given = dict(locals())
    place = jnp.stack([lax.axis_index("c"), 2 * lax.axis_index("x") + lax.axis_index("y")]).astype(jnp.int32)

    _ORDER[0] = None
    index = _INDEX

    def local_view(a, i):
        return a[0].T if _MATS[i][6] else a[0]

    shard = {key: local_view(given[ref], i) for i, (key, ref, *_) in enumerate(_MATS)}

    p = {}
    waves = (("in", ["w_in"], 1), ("mid", ["w_ba", "w_bb", "w_out", "wq", "wkv", "wo"], 2), ("ffn", ["w1", "w3", "w2"], 11))
    for wave, keys, cid in waves:
        units = [_cast_unit(shard[k], "cast_" + k) for k in keys]
        p.update(zip(keys, _all_gather(units, [index[k] for k in keys], cid, "all_gather_" + wave)))
    small_w = {"ln_mix": ln_mix_w, "ln_cross": ln_cross_w, "ln_mem": ln_mem_w, "ln_ffn": ln_ffn_w,
               "ln_final": ln_final_w.reshape(1, D_MODEL), "hg_lb": hg_lower_bounds, "hg_norm": hg_norm_w}
    p.update(small_w)

    groups = {"ffn": ["w2", "w1", "w3"], "cross": ["wo", "wq", "wkv"], "mix": ["w_out", "w_ba", "w_bb"], "in": ["w_in"]}
    cids = {"ffn": (3, 4), "cross": (5, 6), "mix": (7, 8), "in": (9, 10)}
    stage_order = list(groups)
    sib, parts, recv, out = {}, {}, {}, {}

    grads = {}

    def chip_stage(name):
        keys = groups[name]
        parts[name] = [_chip_sum(grads[k], s, index[k], place, "chip_sum_" + k) for k, s in zip(keys, sib[name])]
        recv[name] = _rs_chips(parts[name], [index[k] for k in keys], cids[name][1], "rs_chips_" + name)

    def adam_stage(name):
        for k, t, r in zip(groups[name], parts[name], recv[name]):
            i = index[k]
            ref = _MATS[i][1]
            res = _adam_mat(t, r, shard[k], local_view(given["m_" + ref], i), local_view(given["v_" + ref], i), place, "adam_" + k)
            out[ref] = [(a.T if _MATS[i][6] else a)[None] for a in res]

    def ready(name, gw):
        if name == "mixers_done":
            chip_stage("mix")
            adam_stage("ffn")
            adam_stage("cross")
            return
        keys = groups[name]
        grads.update({k: gw[k] for k in keys})
        sib[name] = _rs_sibling([gw[k] for k in keys], [index[k] for k in keys], cids[name][0], "rs_sibling_" + name)
        if name == "cross":
            chip_stage("ffn")
        if name == "mix":
            chip_stage("cross")

    loss_row, grad_x, gs = _local_step(x[0], mem[0], loss_target[0], p, ready)

    sm = {k: given["m_" + ref] for k, ref, _ in _SMALL}
    sv = {k: given["v_" + ref] for k, ref, _ in _SMALL}
    synced = _small_sync(_pack_small(gs, loss_row), _pack_small(small_w), _pack_small(sm), _pack_small(sv))
    chip_stage("in")
    adam_stage("mix")
    adam_stage("in")
    off = 0
    for k, ref, n in _SMALL:
        shape = given[ref].shape
        out[ref] = [synced[r, off:off + n].reshape(shape) for r in range(4)]
        off += n
    loss = synced[0, off]

    order = ["ln_mix_w", "w_in", "hg_norm_w", "hg_lower_bounds", "w_branch_a", "w_branch_b", "w_out", "ln_cross_w", "ln_mem_w",
             "wq_cross", "wkv_cross", "wo_cross", "ln_ffn_w", "w1", "w3", "w2", "ln_final_w"]
    return (loss, grad_x[None], *[out[n][0] for n in order], *[out[n][1] for n in order],
            *[out[n][2] for n in order], *[out[n][3] for n in order])
```

```python
import functools
import math

import jax
import jax.numpy as jnp
from jax import lax
from jax.experimental import pallas as pl
from jax.experimental.pallas import tpu as pltpu
from jax.experimental.pallas import tpu_sc as plsc

F32 = jnp.float32
BF = jnp.bfloat16
SDS = jax.ShapeDtypeStruct

D_MODEL = 2048
SEQ = 2048
HEAD_DIM = 128
MEM_LEN = 256
ATT_GROUPS = ((128, 1), (512, 4), (2048, 16))
ATT_HEADS = 4
ATT_WIDTH = 1536
ATT_OUT = 512
HG_HEADS = 8
HG_WIDTH = 1024
HG_CHUNK = 64
IN_WIDTH = 12800
CROSS_HEADS = 4
CROSS_WIDTH = 512
D_FF = 5632
RMS_EPS = 1e-6
ADAM_LR = 0.001
ADAM_B1 = 0.9
ADAM_B2 = 0.999
ADAM_EPS = 1e-08
ADAM_WD = 0.01
ADAM_STEP = 10
N_DEV = 8

LANE = 128
QA0, KA0, VA0 = 0, 12, 24
QH0, FH0, IH0, GH0 = 36, 44, 52, 60
GA0, GB0 = 68, 84

VMEM_BIG = 56 * 1024 * 1024
MESH = pl.DeviceIdType.MESH


def _cp(vmem=None, **kw):
    if vmem is not None:
        kw["vmem_limit_bytes"] = vmem
    return pltpu.CompilerParams(**kw)


def _pick(n, cands):
    for c in cands:
        if n % c == 0:
            return c
    raise ValueError(f"no tile for {n}")


_ANY = pl.BlockSpec(memory_space=pl.ANY)
_ORDER = [None]


def _pcall(body, *, name, in_specs, out_specs, out_shape, grid=(), scratch_shapes=(), compiler_params=None, prefetch=False):
    def run(*args):
        dep = _ORDER[0]
        specs, operands, kernel_fn = list(in_specs), list(args), body
        if dep is not None:
            at = len(operands)
            specs.append(_ANY)
            operands.append(dep)

            def kernel_fn(*refs):
                return body(*refs[:at], *refs[at + 1:])
        if prefetch:
            call = pl.pallas_call(
                kernel_fn, out_shape=out_shape, name=name, compiler_params=compiler_params,
                grid_spec=pltpu.PrefetchScalarGridSpec(num_scalar_prefetch=1, grid=grid, in_specs=specs, out_specs=out_specs,
                                                       scratch_shapes=scratch_shapes))
        else:
            call = pl.pallas_call(kernel_fn, grid=grid, in_specs=specs, out_specs=out_specs, out_shape=out_shape,
                                  scratch_shapes=scratch_shapes, compiler_params=compiler_params, name=name)
        out = call(*operands)
        _ORDER[0] = out[0] if isinstance(out, (list, tuple)) else out
        return out
    return run


def _mm(a, b, form, out_dtype, name, res=None, tiles=None, a_half=None, b_half=None):
    a_shape = a.shape if a_half is None else (a.shape[0], a.shape[1] // 2)
    b_shape = b.shape if b_half is None else (b.shape[0], b.shape[1] // 2)
    if form == "nn":
        (M, K), (K2, N) = a_shape, b_shape
    elif form == "nt":
        (M, K), (N, K2) = a_shape, b_shape
    else:
        (K, M), (K2, N) = a_shape, b_shape
    assert K == K2, (a.shape, b.shape, form)
    if tiles is None:
        tm = _pick(M, (1024, 1280, 1408, 512, 256))
        tn = _pick(N, (1280, 1024, 1408, 512, 256))
        tk = K if K <= 2048 else _pick(K, (2560, 1408))
    else:
        tm, tn, tk = tiles
    nk = K // tk
    ah, bh = (a_half or 0), (b_half or 0)
    if form == "nn":
        a_spec = pl.BlockSpec((tm, tk), lambda i, j, k: (i, k + ah * nk))
        b_spec = pl.BlockSpec((tk, tn), lambda i, j, k: (k, j + bh * (N // tn)))
        dims = ((1,), (0,))
    elif form == "nt":
        a_spec = pl.BlockSpec((tm, tk), lambda i, j, k: (i, k + ah * nk))
        b_spec = pl.BlockSpec((tn, tk), lambda i, j, k: (j, k + bh * nk))
        dims = ((1,), (1,))
    else:
        a_spec = pl.BlockSpec((tk, tm), lambda i, j, k: (k, i + ah * (M // tm)))
        b_spec = pl.BlockSpec((tk, tn), lambda i, j, k: (k, j + bh * (N // tn)))
        dims = ((0,), (0,))
    o_spec = pl.BlockSpec((tm, tn), lambda i, j, k: (i, j))
    in_specs = [a_spec, b_spec]
    args = [a, b]
    if res is not None:
        in_specs.append(o_spec)
        args.append(res)

    def body(*refs):
        a_ref, b_ref = refs[0], refs[1]
        r_ref = refs[2] if res is not None else None
        o_ref = refs[3] if res is not None else refs[2]
        acc = refs[-1] if nk > 1 else None
        k = pl.program_id(2)
        d = lax.dot_general(a_ref[...].astype(BF), b_ref[...].astype(BF), (dims, ((), ())), preferred_element_type=F32)

        def finish(r):
            if res is not None:
                r = r + r_ref[...].astype(F32)
            o_ref[...] = r.astype(o_ref.dtype)

        if nk == 1:
            finish(d)
        else:
            @pl.when(k == 0)
            def _():
                acc[...] = d

            @pl.when((k > 0) & (k < nk - 1))
            def _():
                acc[...] += d

            @pl.when(k == nk - 1)
            def _():
                finish(acc[...] + d)

    return _pcall(
        body, grid=(M // tm, N // tn, nk), in_specs=in_specs, out_specs=o_spec,
        out_shape=SDS((M, N), out_dtype), scratch_shapes=[pltpu.VMEM((tm, tn), F32)] if nk > 1 else [],
        compiler_params=_cp(VMEM_BIG, dimension_semantics=("parallel", "parallel", "arbitrary")),
        name=name)(*args)


def _rms(x, w):
    return x * lax.rsqrt(jnp.mean(x * x, axis=-1, keepdims=True) + RMS_EPS) * w


def _rms_fwd(x, w, name):
    R, D = x.shape
    tr = 256

    def body(x_ref, w_ref, o_ref):
        o_ref[...] = _rms(x_ref[...], w_ref[...]).astype(o_ref.dtype)

    return _pcall(
        body, grid=(R // tr,),
        in_specs=[pl.BlockSpec((tr, D), lambda i: (i, 0)), pl.BlockSpec((1, D), lambda i: (0, 0))],
        out_specs=pl.BlockSpec((tr, D), lambda i: (i, 0)), out_shape=SDS((R, D), BF), name=name)(x, w)


def _rms_bwd(x, w, dh, dres, name):
    R, D = x.shape
    tr = 256
    want_dx = dres is not None
    dhs = dh if isinstance(dh, (tuple, list)) else (dh,)
    nd = len(dhs)

    def body(*refs):
        x_ref, w_ref, dh_refs = refs[0], refs[1], refs[2:2 + nd]
        if want_dx:
            dr_ref, dx_ref, dxb_ref, dw_ref = refs[2 + nd:]
        else:
            dw_ref = refs[2 + nd]

        @pl.when(pl.program_id(0) == 0)
        def _():
            dw_ref[...] = jnp.zeros_like(dw_ref)

        dhv = jnp.concatenate([r[...].astype(F32) for r in dh_refs], axis=1) if nd > 1 else dh_refs[0][...].astype(F32)
        _, vjp = jax.vjp(_rms, x_ref[...], w_ref[...])
        dx, dw = vjp(dhv)
        dw_ref[...] += dw
        if want_dx:
            dx = dx + dr_ref[...]
            dx_ref[...] = dx
            dxb_ref[...] = dx.astype(BF)

    row = pl.BlockSpec((tr, D), lambda i: (i, 0))
    part = pl.BlockSpec((tr, D // nd), lambda i: (i, 0))
    vec = pl.BlockSpec((1, D), lambda i: (0, 0))
    if want_dx:
        return _pcall(body, grid=(R // tr,), in_specs=[row, vec] + [part] * nd + [row], out_specs=[row, row, vec],
                      out_shape=[SDS((R, D), F32), SDS((R, D), BF), SDS((1, D), F32)], name=name)(x, w, *dhs, dres)
    return _pcall(body, grid=(R // tr,), in_specs=[row, vec] + [part] * nd, out_specs=vec,
                  out_shape=SDS((1, D), F32), name=name)(x, w, *dhs)


def _final_loss(x3, w, tgt):
    R, D = x3.shape
    tr = 256

    def loss_fn(xv, wv, tv):
        err = _rms(xv, wv) - tv
        return 0.5 * jnp.sum(jnp.mean(err * err, axis=-1))

    def body(x_ref, w_ref, t_ref, loss_ref, dx_ref, dxb_ref, dw_ref):
        @pl.when(pl.program_id(0) == 0)
        def _():
            dw_ref[...] = jnp.zeros_like(dw_ref)
            loss_ref[...] = jnp.zeros_like(loss_ref)

        tv = t_ref[...]
        val, vjp = jax.vjp(lambda a, b: loss_fn(a, b, tv), x_ref[...], w_ref[...])
        dx, dw = vjp(jnp.ones((), F32))
        dx_ref[...] = dx
        dxb_ref[...] = dx.astype(BF)
        dw_ref[...] += dw
        loss_ref[...] += jnp.full(loss_ref.shape, val, F32)

    row = pl.BlockSpec((tr, D), lambda i: (i, 0))
    vec = pl.BlockSpec((1, D), lambda i: (0, 0))
    one = pl.BlockSpec((1, LANE), lambda i: (0, 0))
    return _pcall(body, grid=(R // tr,), in_specs=[row, vec, row], out_specs=[one, row, row, vec],
                          out_shape=[SDS((1, LANE), F32), SDS((R, D), F32), SDS((R, D), BF), SDS((1, D), F32)],
                          name="final_loss")(x3, w, tgt)


NEG = -1e30


def _att_blocks(S, d):
    L = S // d
    nb = L // LANE
    return nb, (2 if nb > 1 else 1)


def _att_rows(S, d, idx):
    nb, nkb = _att_blocks(S, d)
    r = idx // nb
    n = idx % nb
    kb = jnp.maximum(n - 1, 0)
    if d == 1:
        qrows = pl.ds(pl.multiple_of(n * LANE, LANE), LANE)
        krows = pl.ds(pl.multiple_of(kb * LANE, LANE), LANE * nkb)
    else:
        qrows = pl.ds(r + n * (LANE * d), LANE, stride=d)
        krows = pl.ds(r + kb * (LANE * d), LANE * nkb, stride=d)
    qpos = n * LANE + lax.broadcasted_iota(jnp.int32, (LANE, LANE * nkb), 0)
    kpos = kb * LANE + lax.broadcasted_iota(jnp.int32, (LANE, LANE * nkb), 1)
    return qrows, krows, qpos - kpos


def _slab(col):
    return pl.BlockSpec((SEQ, LANE), lambda h, col=col: (0, col + h))


def _attn_fwd(proj):
    S = proj.shape[0]
    scale = HEAD_DIM ** -0.5

    def body(q0, q1, q2, k0, k1, k2, v0, v1, v2, o_ref, lse_ref, og, lg):
        qs, ks, vs = (q0, q1, q2), (k0, k1, k2), (v0, v1, v2)
        for g, (window, d) in enumerate(ATT_GROUPS):
            def blk(idx, carry, g=g, window=window, d=d):
                qrows, krows, dist = _att_rows(S, d, idx)
                q = qs[g][qrows, :].astype(BF)
                k = ks[g][krows, :].astype(BF)
                v = vs[g][krows, :].astype(BF)
                s = lax.dot_general(q, k, (((1,), (1,)), ((), ())), preferred_element_type=F32) * scale
                s = jnp.where((dist >= 0) & (dist <= window // d), s, NEG)
                m = jnp.max(s, axis=-1, keepdims=True)
                p = jnp.exp(s - m)
                l = jnp.sum(p, axis=-1, keepdims=True)
                o = jnp.dot((p / l).astype(BF), v, preferred_element_type=F32)
                og[g, qrows, :] = o
                lg[g, qrows, :] = jnp.broadcast_to(m + jnp.log(l), (LANE, LANE))
                return carry
            lax.fori_loop(0, S // LANE, blk, 0)

        def merge(t, carry):
            rows = pl.ds(pl.multiple_of(t * 256, 256), 256)
            l0, l1, l2 = lg[0, rows, :], lg[1, rows, :], lg[2, rows, :]
            m = jnp.maximum(jnp.maximum(l0, l1), l2)
            w0, w1, w2 = jnp.exp(l0 - m), jnp.exp(l1 - m), jnp.exp(l2 - m)
            den = w0 + w1 + w2
            o_ref[rows, :] = (w0 * og[0, rows, :] + w1 * og[1, rows, :] + w2 * og[2, rows, :]) / den
            lse_ref[rows, :] = m + jnp.log(den)
            return carry
        lax.fori_loop(0, S // 256, merge, 0)

    ins = [_slab(QA0 + 4 * g) for g in range(3)] + [_slab(KA0 + 4 * g) for g in range(3)] + [_slab(VA0 + 4 * g) for g in range(3)]
    out = pl.BlockSpec((SEQ, LANE), lambda h: (0, h))
    return _pcall(
        body, grid=(ATT_HEADS,), in_specs=ins, out_specs=[out, out],
        out_shape=[SDS((S, ATT_OUT), F32), SDS((S, ATT_OUT), F32)],
        scratch_shapes=[pltpu.VMEM((3, S, LANE), F32), pltpu.VMEM((3, S, LANE), F32)],
        compiler_params=_cp(VMEM_BIG), name="attn_fwd")(*([proj] * 9))


def _attn_bwd(proj, o_att, lse, do_att):
    S = proj.shape[0]
    scale = HEAD_DIM ** -0.5

    def body(q0, q1, q2, k0, k1, k2, v0, v1, v2, o_ref, lse_ref, do_ref,
             dq0, dq1, dq2, dk0, dk1, dk2, dv0, dv1, dv2, dqa, dka, dva, delta):
        qs, ks, vs = (q0, q1, q2), (k0, k1, k2), (v0, v1, v2)
        dqs, dks, dvs = (dq0, dq1, dq2), (dk0, dk1, dk2), (dv0, dv1, dv2)
        delta[...] = jnp.broadcast_to(jnp.sum(do_ref[...] * o_ref[...], axis=-1, keepdims=True), delta.shape)
        for g, (window, d) in enumerate(ATT_GROUPS):
            dka[...] = jnp.zeros_like(dka)
            dva[...] = jnp.zeros_like(dva)

            def blk(idx, carry, g=g, window=window, d=d):
                qrows, krows, dist = _att_rows(S, d, idx)
                q = qs[g][qrows, :].astype(BF)
                k = ks[g][krows, :].astype(BF)
                v = vs[g][krows, :].astype(BF)
                do = do_ref[qrows, :]
                s = lax.dot_general(q, k, (((1,), (1,)), ((), ())), preferred_element_type=F32) * scale
                p = jnp.exp(s - lse_ref[qrows, :][:, 0:1])
                p = jnp.where((dist >= 0) & (dist <= window // d), p, 0.0)
                dob = do.astype(BF)
                dva[krows, :] += lax.dot_general(p.astype(BF), dob, (((0,), (0,)), ((), ())), preferred_element_type=F32)
                dp = lax.dot_general(dob, v, (((1,), (1,)), ((), ())), preferred_element_type=F32)
                ds = (p * (dp - delta[qrows, :][:, 0:1]) * scale).astype(BF)
                dqa[qrows, :] = jnp.dot(ds, k, preferred_element_type=F32)
                dka[krows, :] += lax.dot_general(ds, q, (((0,), (0,)), ((), ())), preferred_element_type=F32)
                return carry
            lax.fori_loop(0, S // LANE, blk, 0)
            dqs[g][...] = dqa[...].astype(dqs[g].dtype)
            dks[g][...] = dka[...].astype(dks[g].dtype)
            dvs[g][...] = dva[...].astype(dvs[g].dtype)

    cols = [QA0 + 4 * g for g in range(3)] + [KA0 + 4 * g for g in range(3)] + [VA0 + 4 * g for g in range(3)]
    ins = [_slab(c) for c in cols]
    one = pl.BlockSpec((SEQ, LANE), lambda h: (0, h))
    return _pcall(
        body, grid=(ATT_HEADS,), in_specs=ins + [one, one, one], out_specs=[one] * 9,
        out_shape=[SDS((S, ATT_OUT), BF)] * 9,
        scratch_shapes=[pltpu.VMEM((S, LANE), F32)] * 4,
        compiler_params=_cp(VMEM_BIG), name="attn_bwd")(*([proj] * 9), o_att, lse, do_att)


def _bdot(a, b, dims):
    return lax.dot_general(a.astype(BF), b.astype(BF), (dims, ((), ())), preferred_element_type=F32)


def _hg_chunk(qh, fh, ih, gh, lbraw, nw, st):
    C = HG_CHUNK
    mx = jnp.maximum(lbraw[0:1], lbraw[1:2])
    e0 = jnp.exp(lbraw[0:1] - mx)
    e1 = jnp.exp(lbraw[1:2] - mx)
    lb = e0 / (e0 + e1)
    f = lb + (1.0 - lb) * jax.nn.sigmoid(fh)
    logf = jnp.log(f)
    k = 1.0 - f
    q = qh * jax.nn.sigmoid(qh)
    causal = lax.broadcasted_iota(jnp.int32, (C, C), 0) >= lax.broadcasted_iota(jnp.int32, (C, C), 1)
    b = lax.dot_general(causal.astype(F32), logf, (((1,), (0,)), ((), ())), precision=lax.Precision.HIGHEST,
                        preferred_element_type=F32)
    b_last = b[C - 1:C]
    st_out = st * jnp.exp(b_last) + _bdot(ih, k * jnp.exp(b_last - b), ((0,), (0,)))
    inter = _bdot(q * jnp.exp(b), st, ((1,), (1,)))
    b_ref = b[C // 2:C // 2 + 1]
    a = _bdot(q * jnp.exp(b - b_ref), k * jnp.exp(b_ref - b), ((1,), (1,)))
    a = jnp.where(causal, a, 0.0)
    o = inter + _bdot(a, ih, ((1,), (0,)))
    o = o * lax.rsqrt(jnp.mean(o * o, axis=-1, keepdims=True) + RMS_EPS) * nw
    return o * (gh * jax.nn.sigmoid(gh)), st_out


def _head_cols(h):
    return slice(h * LANE, (h + 1) * LANE)


def _hg_in_specs(chunk_of):
    half = 4 * LANE
    specs = []
    for base in (QH0, FH0, IH0, GH0):
        for part in range(2):
            specs.append(pl.BlockSpec((HG_CHUNK, half), lambda n, col=base // 4 + part: (chunk_of(n), col)))
    return specs


def _hg_fwd(proj, lb, nw):
    S = proj.shape[0]
    C = HG_CHUNK
    NC = S // C

    def body(*refs):
        sec = refs[:8]
        lb_ref, nw_ref, o_ref, st_ref, st_scr = refs[8:]

        @pl.when(pl.program_id(0) == 0)
        def _():
            st_scr[...] = jnp.zeros_like(st_scr)

        for h in range(HG_HEADS):
            half, hs = h // 4, _head_cols(h % 4)
            st = st_scr[h]
            st_ref[h, 0] = st
            o, st_out = _hg_chunk(sec[half][:, hs], sec[2 + half][:, hs], sec[4 + half][:, hs], sec[6 + half][:, hs],
                                  lb_ref[:, _head_cols(h)], nw_ref[...], st)
            o_ref[:, _head_cols(h)] = o.astype(o_ref.dtype)
            st_scr[h] = st_out

    return _pcall(
        body, grid=(NC,),
        in_specs=_hg_in_specs(lambda n: n) + [pl.BlockSpec((2, HG_WIDTH), lambda n: (0, 0)), pl.BlockSpec((1, LANE), lambda n: (0, 0))],
        out_specs=[pl.BlockSpec((C, HG_WIDTH), lambda n: (n, 0)), pl.BlockSpec((HG_HEADS, 1, LANE, LANE), lambda n: (0, n, 0, 0))],
        out_shape=[SDS((S, HG_WIDTH), BF), SDS((HG_HEADS, NC, LANE, LANE), F32)],
        scratch_shapes=[pltpu.VMEM((HG_HEADS, LANE, LANE), F32)], name="hg_fwd")(*([proj] * 8), lb, nw)


def _hg_bwd(proj, lb, nw, states, do_hg):
    S = proj.shape[0]
    C = HG_CHUNK
    NC = S // C

    def body(*refs):
        sec = refs[:8]
        lb_ref, nw_ref, st_ref, do_ref, dq_ref, df_ref, di_ref, dg_ref, dlb_ref, dnw_ref, dst_scr = refs[8:]

        @pl.when(pl.program_id(0) == 0)
        def _():
            dst_scr[...] = jnp.zeros_like(dst_scr)
            dlb_ref[...] = jnp.zeros_like(dlb_ref)
            dnw_ref[...] = jnp.zeros_like(dnw_ref)

        dnw_sum = jnp.zeros((1, LANE), F32)
        for h in range(HG_HEADS):
            half, hs, hc = h // 4, _head_cols(h % 4), _head_cols(h)
            _, vjp = jax.vjp(_hg_chunk, sec[half][:, hs], sec[2 + half][:, hs], sec[4 + half][:, hs], sec[6 + half][:, hs],
                             lb_ref[:, hc], nw_ref[...], st_ref[h, 0])
            dq, df, di, dg, dlb, dnw, dst = vjp((do_ref[:, hc].astype(F32), dst_scr[h]))
            dq_ref[:, hc] = dq.astype(dq_ref.dtype)
            df_ref[:, hc] = df.astype(df_ref.dtype)
            di_ref[:, hc] = di.astype(di_ref.dtype)
            dg_ref[:, hc] = dg.astype(dg_ref.dtype)
            dlb_ref[:, hc] += dlb
            dnw_sum = dnw_sum + dnw
            dst_scr[h] = dst
        dnw_ref[...] += dnw_sum

    rev = lambda n: NC - 1 - n
    ospec = pl.BlockSpec((C, HG_WIDTH), lambda n: (rev(n), 0))
    lbs = pl.BlockSpec((2, HG_WIDTH), lambda n: (0, 0))
    nws = pl.BlockSpec((1, LANE), lambda n: (0, 0))
    outs = _pcall(
        body, grid=(NC,),
        in_specs=_hg_in_specs(rev) + [lbs, nws, pl.BlockSpec((HG_HEADS, 1, LANE, LANE), lambda n: (0, rev(n), 0, 0)), ospec],
        out_specs=[ospec] * 4 + [lbs, nws],
        out_shape=[SDS((S, HG_WIDTH), BF)] * 4 + [SDS((2, HG_WIDTH), F32), SDS((1, LANE), F32)],
        scratch_shapes=[pltpu.VMEM((HG_HEADS, LANE, LANE), F32)], name="hg_bwd")(*([proj] * 8), lb, nw, states, do_hg)
    return outs


def _merge(ga, gb, ba, bb):
    return jax.nn.sigmoid(ga) * ba + jax.nn.sigmoid(gb) * bb


_MT = 512


def _gate_specs():
    ga = pl.BlockSpec((_MT, _MT), lambda i, j: (i, GA0 * LANE // _MT + j))
    gb = pl.BlockSpec((_MT, _MT), lambda i, j: (i, GB0 * LANE // _MT + j))
    t = pl.BlockSpec((_MT, _MT), lambda i, j: (i, j))
    return ga, gb, t


def _merge_fwd(proj, ba, bb):
    S, D = ba.shape
    ga, gb, t = _gate_specs()

    def body(ga_ref, gb_ref, ba_ref, bb_ref, o_ref):
        o_ref[...] = _merge(ga_ref[...], gb_ref[...], ba_ref[...], bb_ref[...]).astype(o_ref.dtype)

    return _pcall(body, grid=(S // _MT, D // _MT), in_specs=[ga, gb, t, t], out_specs=t,
                          out_shape=SDS((S, D), BF), name="merge_fwd")(proj, proj, ba, bb)


def _merge_bwd(proj, ba, bb, dm):
    S, D = ba.shape
    ga, gb, t = _gate_specs()

    def body(ga_ref, gb_ref, ba_ref, bb_ref, dm_ref, dga_ref, dgb_ref, dba_ref, dbb_ref):
        _, vjp = jax.vjp(_merge, ga_ref[...], gb_ref[...], ba_ref[...], bb_ref[...])
        dga, dgb, dba, dbb = vjp(dm_ref[...])
        dga_ref[...] = dga.astype(BF)
        dgb_ref[...] = dgb.astype(BF)
        dba_ref[...] = dba.astype(BF)
        dbb_ref[...] = dbb.astype(BF)

    return _pcall(body, grid=(S // _MT, D // _MT), in_specs=[ga, gb, t, t, t], out_specs=[t] * 4,
                          out_shape=[SDS((S, D), BF)] * 4, name="merge_bwd")(proj, proj, ba, bb, dm)


def _swiglu(a, b):
    return a * jax.nn.sigmoid(a) * b


def _swiglu_fwd(a, b):
    S, F = a.shape
    tf = _pick(F, (1408, 512))
    t = pl.BlockSpec((_MT, tf), lambda i, j: (i, j))

    def body(a_ref, b_ref, o_ref):
        o_ref[...] = _swiglu(a_ref[...], b_ref[...]).astype(o_ref.dtype)

    return _pcall(body, grid=(S // _MT, F // tf), in_specs=[t, t], out_specs=t,
                          out_shape=SDS((S, F), BF), name="swiglu_fwd")(a, b)


def _swiglu_bwd(a, b, dg):
    S, F = a.shape
    tf = _pick(F, (1408, 512))
    t = pl.BlockSpec((_MT, tf), lambda i, j: (i, j))

    def body(a_ref, b_ref, dg_ref, da_ref, db_ref):
        _, vjp = jax.vjp(_swiglu, a_ref[...], b_ref[...])
        da, db = vjp(dg_ref[...].astype(F32))
        da_ref[...] = da.astype(BF)
        db_ref[...] = db.astype(BF)

    return _pcall(body, grid=(S // _MT, F // tf), in_specs=[t, t, t], out_specs=[t, t],
                          out_shape=[SDS((S, F), BF)] * 2, name="swiglu_bwd")(a, b, dg)


def _cross(q, k, v):
    s = _bdot(q, k, ((1,), (1,))) * (HEAD_DIM ** -0.5)
    e = jnp.exp(s - jnp.max(s, axis=-1, keepdims=True))
    p = e / jnp.sum(e, axis=-1, keepdims=True)
    return _bdot(p, v, ((1,), (0,)))


def _cross_specs():
    q = pl.BlockSpec((_MT, LANE), lambda h, i: (i, h))
    k = pl.BlockSpec((MEM_LEN, LANE), lambda h, i: (0, h))
    v = pl.BlockSpec((MEM_LEN, LANE), lambda h, i: (0, CROSS_HEADS + h))
    return q, k, v


def _cross_fwd(qc, kvc):
    S = qc.shape[0]
    q, k, v = _cross_specs()

    def body(q_ref, k_ref, v_ref, o_ref):
        o_ref[...] = _cross(q_ref[...], k_ref[...], v_ref[...]).astype(o_ref.dtype)

    return _pcall(body, grid=(CROSS_HEADS, S // _MT), in_specs=[q, k, v], out_specs=q,
                          out_shape=SDS((S, CROSS_WIDTH), BF), name="cross_fwd")(qc, kvc, kvc)


def _cross_bwd(qc, kvc, doc):
    S = qc.shape[0]
    q, k, v = _cross_specs()

    def body(q_ref, k_ref, v_ref, do_ref, dq_ref, dk_ref, dv_ref):
        @pl.when(pl.program_id(1) == 0)
        def _():
            dk_ref[...] = jnp.zeros_like(dk_ref)
            dv_ref[...] = jnp.zeros_like(dv_ref)

        _, vjp = jax.vjp(_cross, q_ref[...], k_ref[...], v_ref[...])
        dq, dk, dv = vjp(do_ref[...])
        dq_ref[...] = dq.astype(dq_ref.dtype)
        dk_ref[...] += dk
        dv_ref[...] += dv

    return _pcall(body, grid=(CROSS_HEADS, S // _MT), in_specs=[q, k, v, q], out_specs=[q, k, k],
                          out_shape=[SDS((S, CROSS_WIDTH), BF), SDS((MEM_LEN, CROSS_WIDTH), F32), SDS((MEM_LEN, CROSS_WIDTH), F32)],
                          name="cross_bwd")(qc, kvc, kvc, doc)


def _local_step(x, mem, tgt, p, ready):
    h = _rms_fwd(x, p["ln_mix"], "rms_mix")
    proj = _mm(h, p["w_in_a"], "nt", F32, "mm_proj_a", a_half=0)
    proj = _mm(h, p["w_in_b"], "nt", F32, "mm_proj_b", a_half=1, res=proj)
    o_att, lse = _attn_fwd(proj)
    o_hg, states = _hg_fwd(proj, p["hg_lb"], p["hg_norm"])
    ba = _mm(o_att, p["w_ba"], "nn", F32, "mm_ba")
    bb = _mm(o_hg, p["w_bb"], "nn", F32, "mm_bb")
    merged = _merge_fwd(proj, ba, bb)
    x1 = _mm(merged, p["w_out"], "nn", F32, "mm_out", res=x)
    hc = _rms_fwd(x1, p["ln_cross"], "rms_cross")
    qc = _mm(hc, p["wq"], "nn", F32, "mm_q")
    mn = _rms_fwd(mem, p["ln_mem"], "rms_mem")
    kvc = _mm(mn, p["wkv"], "nn", F32, "mm_kv")
    oc = _cross_fwd(qc, kvc)
    x2 = _mm(oc, p["wo"], "nn", F32, "mm_o", res=x1)
    hf = _rms_fwd(x2, p["ln_ffn"], "rms_ffn")
    a = _mm(hf, p["w1"], "nt", F32, "mm_w1")
    b = _mm(hf, p["w3"], "nt", F32, "mm_w3")
    g = _swiglu_fwd(a, b)
    x3 = _mm(g, p["w2"], "nn", F32, "mm_w2", res=x2)
    loss, dx3, dx3b, d_ln_final = _final_loss(x3, p["ln_final"], tgt)

    gw, gs = {}, {"ln_final": d_ln_final}
    dg = _mm(dx3b, p["w2"], "nt", BF, "mm_dg")
    gw["w2"] = _mm(g, dx3b, "tn", BF, "mm_dw2")
    da, db = _swiglu_bwd(a, b, dg)
    dhf = _mm(da, p["w1"], "nn", F32, "mm_dhf1")
    dhf = _mm(db, p["w3"], "nn", F32, "mm_dhf3", res=dhf)
    gw["w1"] = _mm(da, hf, "tn", BF, "mm_dw1")
    gw["w3"] = _mm(db, hf, "tn", BF, "mm_dw3")
    ready("ffn", gw)
    dx2, dx2b, gs["ln_ffn"] = _rms_bwd(x2, p["ln_ffn"], dhf, dx3, "rms_ffn_bwd")
    doc = _mm(dx2b, p["wo"], "nt", F32, "mm_doc")
    gw["wo"] = _mm(oc, dx2b, "tn", BF, "mm_dwo")
    dqc, dk, dv = _cross_bwd(qc, kvc, doc)
    dkvc = jnp.concatenate([dk, dv], axis=1)
    dhc = _mm(dqc, p["wq"], "nt", F32, "mm_dhc")
    gw["wq"] = _mm(hc, dqc, "tn", BF, "mm_dwq")
    dmn = _mm(dkvc, p["wkv"], "nt", F32, "mm_dmn")
    gw["wkv"] = _mm(mn, dkvc, "tn", BF, "mm_dwkv")
    ready("cross", gw)
    gs["ln_mem"] = _rms_bwd(mem, p["ln_mem"], dmn, None, "rms_mem_bwd")
    dx1, dx1b, gs["ln_cross"] = _rms_bwd(x1, p["ln_cross"], dhc, dx2, "rms_cross_bwd")
    dmerged = _mm(dx1b, p["w_out"], "nt", F32, "mm_dmerged")
    gw["w_out"] = _mm(merged, dx1b, "tn", BF, "mm_dwout")
    dga, dgb, dba, dbb = _merge_bwd(proj, ba, bb, dmerged)
    do_att = _mm(dba, p["w_ba"], "nt", F32, "mm_doatt")
    gw["w_ba"] = _mm(o_att, dba, "tn", BF, "mm_dwba")
    do_hg = _mm(dbb, p["w_bb"], "nt", F32, "mm_dohg")
    gw["w_bb"] = _mm(o_hg, dbb, "tn", BF, "mm_dwbb")
    ready("mix", gw)
    datt = _attn_bwd(proj, o_att, lse, do_att)
    dqh, dfh, dih, dgh, gs["hg_lb"], gs["hg_norm"] = _hg_bwd(proj, p["hg_lb"], p["hg_norm"], states, do_hg)
    ready("mixers_done", gw)
    dproj = jnp.concatenate([*datt, dqh, dfh, dih, dgh, dga, dgb], axis=1)
    gw["w_in_a"] = _mm(dproj, h, "tn", BF, "mm_dwin_a", b_half=0)
    ready("in_a", gw)
    gw["w_in_b"] = _mm(dproj, h, "tn", BF, "mm_dwin_b", b_half=1)
    ready("in_b", gw)
    dh_a = _mm(dproj, p["w_in_a"], "nn", F32, "mm_dh_a")
    ready("dh_half", gw)
    dh = (dh_a, _mm(dproj, p["w_in_b"], "nn", F32, "mm_dh_b"))
    grad_x, _, gs["ln_mix"] = _rms_bwd(x, p["ln_mix"], dh, dx1, "rms_mix_bwd")
    return loss, grad_x, gs


_MATS = (
    ("w_in_a", "w_in", IN_WIDTH, D_MODEL // 2, "row", IN_WIDTH // N_DEV, True),
    ("w_in_b", "w_in", IN_WIDTH, D_MODEL // 2, "row", IN_WIDTH // N_DEV, True),
    ("w_ba", "w_branch_a", ATT_OUT, D_MODEL, "col", D_MODEL // N_DEV, False),
    ("w_bb", "w_branch_b", HG_WIDTH, D_MODEL, "col", D_MODEL // N_DEV, False),
    ("w_out", "w_out", D_MODEL, D_MODEL, "row", D_MODEL // N_DEV, False),
    ("wq", "wq_cross", D_MODEL, CROSS_WIDTH, "row", D_MODEL // N_DEV, False),
    ("wkv", "wkv_cross", D_MODEL, 2 * CROSS_WIDTH, "row", D_MODEL // N_DEV, False),
    ("wo", "wo_cross", CROSS_WIDTH, D_MODEL, "col", D_MODEL // N_DEV, False),
    ("w1", "w1", D_FF, D_MODEL, "row", D_FF // N_DEV, True),
    ("w3", "w3", D_FF, D_MODEL, "row", D_FF // N_DEV, True),
    ("w2", "w2", D_FF, D_MODEL, "row", D_FF // N_DEV, False),
)
_INDEX = {m[0]: i for i, m in enumerate(_MATS)}


def _unit_shape(i):
    _, _, K, N, kind, sh, _ = _MATS[i]
    return (sh, N) if kind == "row" else (K, sh)


def _win(ref, i, chip, par):
    _, _, _, _, kind, sh, _ = _MATS[i]
    if kind == "row":
        return ref.at[pl.ds(pl.multiple_of((2 * chip + par) * sh, 16), sh), :]
    return ref.at[:, pl.ds(pl.multiple_of((2 * chip + par) * sh, LANE), sh)]


def _my_place():
    x, y, c = lax.axis_index("x"), lax.axis_index("y"), lax.axis_index("c")
    chips = [(1 - x, y), (x, 1 - y), (1 - x, 1 - y)]
    return x, y, c, chips


def _cast_unit(w, name, half=None):
    K, N = w.shape
    if half is not None:
        N = N // 2
    tr = _pick(K, (256, 400, 352))

    def body(w_ref, o_ref):
        o_ref[...] = w_ref[...].astype(BF)

    src = pl.BlockSpec((tr, N), lambda i: (i, half or 0))
    dst = pl.BlockSpec((tr, N), lambda i: (i, 0))
    return _pcall(body, grid=(K // tr,), in_specs=[src], out_specs=dst, out_shape=SDS((K, N), BF), name=name)(w)


def _handshake(peers):
    barrier = pltpu.get_barrier_semaphore()
    for peer in peers:
        pl.semaphore_signal(barrier, inc=1, device_id=peer, device_id_type=MESH)
    pl.semaphore_wait(barrier, len(peers))


def _sequencer(body, out_type, sems, cid, name):
    return pl.kernel(body, out_type=out_type, mesh=plsc.ScalarSubcoreMesh(axis_name="seq", num_cores=1),
                     scratch_types=sems, compiler_params=pltpu.CompilerParams(collective_id=cid), name=name)


def _all_gather(units, idx, cid, name):
    n = len(idx)

    def body(*refs):
        us, fulls = refs[:n], refs[n:2 * n]
        ssem, rsem, lsem = refs[2 * n:]
        x, y, c, chips = _my_place()
        _handshake([(x, y, 1 - c)] + [(*ch, c) for ch in chips])

        def win(i, chip_xy, par):
            return _win(fulls[i], idx[i], 2 * chip_xy[0] + chip_xy[1], par)

        def rcopy(i, k, src, dst, to):
            return pltpu.make_async_remote_copy(src_ref=src, dst_ref=dst, send_sem=ssem.at[i, k], recv_sem=rsem.at[i, k],
                                                device_id=to, device_id_type=MESH)

        sends, locs = [], []
        for i in range(n):
            mine = win(i, (x, y), c)
            loc = pltpu.make_async_copy(us[i], mine, lsem.at[i])
            loc.start()
            locs.append(loc)
            first = [rcopy(i, 0, us[i], mine, (x, y, 1 - c))]
            first += [rcopy(i, 1 + j, us[i], mine, (*chips[j], c)) for j in range(3)]
            for cp in first:
                cp.start()
            sends += first
        for i in range(n):
            for j in range(3):
                blk = win(i, chips[j], c)
                rcopy(i, 1 + j, blk, blk, (x, y, c)).wait_recv()
                fwd = rcopy(i, 4 + j, blk, blk, (x, y, 1 - c))
                fwd.start()
                sends.append(fwd)
        for i in range(n):
            blk = win(i, (x, y), 1 - c)
            rcopy(i, 0, blk, blk, (x, y, c)).wait_recv()
            for j in range(3):
                blk = win(i, chips[j], 1 - c)
                rcopy(i, 4 + j, blk, blk, (x, y, c)).wait_recv()
        for cp in sends:
            cp.wait_send()
        for loc in locs:
            loc.wait()

    out_type = [SDS((_MATS[i][2], _MATS[i][3]), BF) for i in idx]
    sems = [pltpu.SemaphoreType.DMA((n, 7)), pltpu.SemaphoreType.DMA((n, 7)), pltpu.SemaphoreType.DMA((n,))]
    return _sequencer(body, out_type, sems, cid, name)(*units)


def _rs_sibling(grads, idx, cid, name):
    n = len(idx)

    def body(*refs):
        gs, bufs = refs[:n], refs[n:2 * n]
        ssem, rsem = refs[2 * n:]
        x, y, c, _ = _my_place()
        _handshake([(x, y, 1 - c)])
        cps = []
        for i in range(n):
            for k in range(4):
                cp = pltpu.make_async_remote_copy(
                    src_ref=_win(gs[i], idx[i], k, 1 - c), dst_ref=bufs[i].at[k], send_sem=ssem.at[i, k], recv_sem=rsem.at[i, k],
                    device_id=(x, y, 1 - c), device_id_type=MESH)
                cp.start()
                cps.append(cp)
        for cp in cps:
            cp.wait()

    out_type = [SDS((4, *_unit_shape(i)), BF) for i in idx]
    sems = [pltpu.SemaphoreType.DMA((n, 4)), pltpu.SemaphoreType.DMA((n, 4))]
    return _sequencer(body, out_type, sems, cid, name)(*grads)


def _chip_sum(g, sib, i, place, name):
    _, _, K, N, kind, sh, _ = _MATS[i]

    def body(pref, g_ref, s_ref, o_ref):
        o_ref[...] = (g_ref[...].astype(F32) + s_ref[...].astype(F32)).astype(BF)

    if kind == "row":
        tr = _pick(sh, (800, 352, 256))
        per = sh // tr
        grid = (4, per)
        g_spec = pl.BlockSpec((tr, N), lambda k, r, pref: ((2 * k + pref[0]) * per + r, 0))
        u_spec = pl.BlockSpec((None, tr, N), lambda k, r, pref: (k, r, 0))
    else:
        grid = (4,)
        g_spec = pl.BlockSpec((K, sh), lambda k, pref: (0, 2 * k + pref[0]))
        u_spec = pl.BlockSpec((None, K, sh), lambda k, pref: (k, 0, 0))
    return _pcall(body, grid=grid, in_specs=[g_spec, u_spec], out_specs=u_spec, out_shape=SDS((4, *_unit_shape(i)), BF),
                  prefetch=True, name=name)(place, g, sib)


def _rs_chips(parts, idx, cid, name):
    n = len(idx)

    def body(*refs):
        ts, bufs = refs[:n], refs[n:2 * n]
        ssem, rsem = refs[2 * n:]
        x, y, c, chips = _my_place()
        _handshake([(*ch, c) for ch in chips])
        cps = []
        for i in range(n):
            for j in range(3):
                cp = pltpu.make_async_remote_copy(
                    src_ref=ts[i].at[2 * chips[j][0] + chips[j][1]], dst_ref=bufs[i].at[j], send_sem=ssem.at[i, j], recv_sem=rsem.at[i, j],
                    device_id=(*chips[j], c), device_id_type=MESH)
                cp.start()
                cps.append(cp)
        for cp in cps:
            cp.wait()

    out_type = [SDS((3, *_unit_shape(i)), BF) for i in idx]
    sems = [pltpu.SemaphoreType.DMA((n, 3)), pltpu.SemaphoreType.DMA((n, 3))]
    return _sequencer(body, out_type, sems, cid, name)(*parts)


def _adamw(w, g, m, v):
    m = ADAM_B1 * m + (1.0 - ADAM_B1) * g
    v = ADAM_B2 * v + (1.0 - ADAM_B2) * jnp.square(g)
    m_hat = m / (1.0 - ADAM_B1 ** ADAM_STEP)
    v_hat = v / (1.0 - ADAM_B2 ** ADAM_STEP)
    delta = -ADAM_LR * (m_hat / (jnp.sqrt(v_hat) + ADAM_EPS) + ADAM_WD * w)
    return delta, m, v


def _adam_mat(parts, recvs, w, m, v, place, name):
    R, C = w.shape
    n = len(parts)
    tr = _pick(R, (256, 160, 176))

    def body(pref, *refs):
        t_refs, r_refs = refs[:n], refs[n:2 * n]
        w_ref, m_ref, v_ref, g_out, d_out, m_out, v_out = refs[2 * n:]
        cols = [t[...].astype(F32) + r[0].astype(F32) + r[1].astype(F32) + r[2].astype(F32) for t, r in zip(t_refs, r_refs)]
        g = jnp.concatenate(cols, axis=1) if n > 1 else cols[0]
        d, mn, vn = _adamw(w_ref[...], g, m_ref[...], v_ref[...])
        g_out[...] = g
        d_out[...] = d
        m_out[...] = mn
        v_out[...] = vn

    t_spec = pl.BlockSpec((None, tr, C // n), lambda i, pref: (pref[1], i, 0))
    r_spec = pl.BlockSpec((3, tr, C // n), lambda i, pref: (0, i, 0))
    s_spec = pl.BlockSpec((tr, C), lambda i, pref: (i, 0))
    return _pcall(body, grid=(R // tr,), in_specs=[t_spec] * n + [r_spec] * n + [s_spec] * 3, out_specs=[s_spec] * 4,
                  out_shape=[SDS((R, C), F32)] * 4, prefetch=True, name=name)(place, *parts, *recvs, w, m, v)


_SMALL = (("ln_mix", "ln_mix_w", D_MODEL), ("ln_cross", "ln_cross_w", D_MODEL), ("ln_mem", "ln_mem_w", D_MODEL),
          ("ln_ffn", "ln_ffn_w", D_MODEL), ("ln_final", "ln_final_w", D_MODEL), ("hg_lb", "hg_lower_bounds", 2 * HG_WIDTH),
          ("hg_norm", "hg_norm_w", HEAD_DIM))
_PACK = sum(n for _, _, n in _SMALL) + LANE


def _small_sync(grow, wrow, mrow, vrow):
    def body(g_ref, w_ref, m_ref, v_ref, o_ref, gath, ssem, rsem):
        x, y, c, _ = _my_place()
        me = 4 * x + 2 * y + c
        gath[pl.ds(me, 1), :] = g_ref[...]
        cps = []
        for dlt in range(1, N_DEV):
            peer = (me + dlt) % N_DEV
            cp = pltpu.make_async_remote_copy(
                src_ref=g_ref, dst_ref=gath.at[pl.ds(me, 1), :], send_sem=ssem.at[peer], recv_sem=rsem.at[me],
                device_id=(peer // 4, (peer // 2) % 2, peer % 2), device_id_type=MESH)
            cp.start()
            cps.append(cp)
        for dlt in range(1, N_DEV):
            src = (me + dlt) % N_DEV
            pltpu.make_async_remote_copy(
                src_ref=g_ref, dst_ref=gath.at[pl.ds(src, 1), :], send_sem=ssem.at[src], recv_sem=rsem.at[src],
                device_id=(x, y, c), device_id_type=MESH).wait_recv()
        for cp in cps:
            cp.wait_send()
        g = gath[0:1, :]
        for j in range(1, N_DEV):
            g = g + gath[j:j + 1, :]
        d, mn, vn = _adamw(w_ref[...], g, m_ref[...], v_ref[...])
        o_ref[0:1, :] = g
        o_ref[1:2, :] = d
        o_ref[2:3, :] = mn
        o_ref[3:4, :] = vn

    vm = pl.BlockSpec(memory_space=pltpu.VMEM)
    return _pcall(
        body, in_specs=[vm] * 4, out_specs=vm, out_shape=SDS((4, _PACK), F32),
        scratch_shapes=[pltpu.VMEM((N_DEV, _PACK), F32), pltpu.SemaphoreType.DMA((N_DEV,)), pltpu.SemaphoreType.DMA((N_DEV,))],
        name="small_sync")(grow, wrow, mrow, vrow)


def _pack_small(d, loss_row=None):
    parts = [d[k].reshape(1, n).astype(F32) for k, _, n in _SMALL]
    parts.append(loss_row if loss_row is not None else jnp.ones((1, LANE), F32))
    return jnp.concatenate(parts, axis=1)


def kernel(x, mem, ln_mix_w, w_in, hg_norm_w, hg_lower_bounds, w_branch_a, w_branch_b, w_out, ln_cross_w, ln_mem_w, wq_cross, wkv_cross, wo_cross, ln_ffn_w, w1, w3, w2, ln_final_w, loss_target, m_ln_mix_w, m_w_in, m_hg_norm_w, m_hg_lower_bounds, m_w_branch_a, m_w_branch_b, m_w_out, m_ln_cross_w, m_ln_mem_w, m_wq_cross, m_wkv_cross, m_wo_cross, m_ln_ffn_w, m_w1, m_w3, m_w2, m_ln_final_w, v_ln_mix_w, v_w_in, v_hg_norm_w, v_hg_lower_bounds, v_w_branch_a, v_w_branch_b, v_w_out, v_ln_cross_w, v_ln_mem_w, v_wq_cross, v_wkv_cross, v_wo_cross, v_ln_ffn_w, v_w1, v_w3, v_w2, v_ln_final_w):
    given = dict(locals())
    place = jnp.stack([lax.axis_index("c"), 2 * lax.axis_index("x") + lax.axis_index("y")]).astype(jnp.int32)

    _ORDER[0] = None
    index = _INDEX

    def local_view(a, i):
        return a[0].T if _MATS[i][6] else a[0]

    shard = {key: local_view(given[ref], i) for i, (key, ref, *_) in enumerate(_MATS)}

    p = {}
    col_half = {"w_in_a": 0, "w_in_b": 1}
    waves = (("in_a", ["w_in_a"], 1), ("in_b", ["w_in_b"], 12), ("mid", ["w_ba", "w_bb", "w_out", "wq", "wkv", "wo"], 2),
             ("ffn", ["w1", "w3", "w2"], 11))
    for wave, keys, cid in waves:
        units = [_cast_unit(shard[k], "cast_" + k, col_half.get(k)) for k in keys]
        p.update(zip(keys, _all_gather(units, [index[k] for k in keys], cid, "all_gather_" + wave)))
    small_w = {"ln_mix": ln_mix_w, "ln_cross": ln_cross_w, "ln_mem": ln_mem_w, "ln_ffn": ln_ffn_w,
               "ln_final": ln_final_w.reshape(1, D_MODEL), "hg_lb": hg_lower_bounds, "hg_norm": hg_norm_w}
    p.update(small_w)

    groups = {"ffn": ["w2", "w1", "w3"], "cross": ["wo", "wq", "wkv"], "mix": ["w_out", "w_ba", "w_bb"],
              "in_a": ["w_in_a"], "in_b": ["w_in_b"]}
    cids = {"ffn": (3, 4), "cross": (5, 6), "mix": (7, 8), "in_a": (9, 10), "in_b": (13, 14)}
    sib, parts, recv, out = {}, {}, {}, {}

    grads = {}

    def chip_stage(name):
        keys = groups[name]
        parts[name] = [_chip_sum(grads[k], s, index[k], place, "chip_sum_" + k) for k, s in zip(keys, sib[name])]
        recv[name] = _rs_chips(parts[name], [index[k] for k in keys], cids[name][1], "rs_chips_" + name)

    def adam(ref, i, part_list, recv_list):
        res = _adam_mat(part_list, recv_list, shard[_MATS[i][0]], local_view(given["m_" + ref], i), local_view(given["v_" + ref], i),
                        place, "adam_" + ref)
        out[ref] = [(a.T if _MATS[i][6] else a)[None] for a in res]

    def adam_stage(name):
        for k, t, r in zip(groups[name], parts[name], recv[name]):
            adam(_MATS[index[k]][1], index[k], [t], [r])

    def ready(name, gw):
        if name == "mixers_done":
            chip_stage("mix")
            adam_stage("ffn")
            adam_stage("cross")
            return
        if name == "dh_half":
            chip_stage("in_b")
            return
        keys = groups[name]
        grads.update({k: gw[k] for k in keys})
        sib[name] = _rs_sibling([gw[k] for k in keys], [index[k] for k in keys], cids[name][0], "rs_sibling_" + name)
        if name == "cross":
            chip_stage("ffn")
        if name == "mix":
            chip_stage("cross")
        if name == "in_b":
            chip_stage("in_a")

    loss_row, grad_x, gs = _local_step(x[0], mem[0], loss_target[0], p, ready)

    sm = {k: given["m_" + ref] for k, ref, _ in _SMALL}
    sv = {k: given["v_" + ref] for k, ref, _ in _SMALL}
    synced = _small_sync(_pack_small(gs, loss_row), _pack_small(small_w), _pack_small(sm), _pack_small(sv))
    adam_stage("mix")
    adam("w_in", index["w_in_a"], parts["in_a"] + parts["in_b"], recv["in_a"] + recv["in_b"])
    off = 0
    for k, ref, n in _SMALL:
        shape = given[ref].shape
        out[ref] = [synced[r, off:off + n].reshape(shape) for r in range(4)]
        off += n
    loss = synced[0, off]

    order = ["ln_mix_w", "w_in", "hg_norm_w", "hg_lower_bounds", "w_branch_a", "w_branch_b", "w_out", "ln_cross_w", "ln_mem_w",
             "wq_cross", "wkv_cross", "wo_cross", "ln_ffn_w", "w1", "w3", "w2", "ln_final_w"]
    return (loss, grad_x[None], *[out[n][0] for n in order], *[out[n][1] for n in order],
            *[out[n][2] for n in order], *[out[n][3] for n in order])
```

```python
import functools
import math

import jax
import jax.numpy as jnp
from jax import lax
from jax.experimental import pallas as pl
from jax.experimental.pallas import tpu as pltpu
from jax.experimental.pallas import tpu_sc as plsc

F32 = jnp.float32
BF = jnp.bfloat16
SDS = jax.ShapeDtypeStruct

D_MODEL = 2048
SEQ = 2048
HEAD_DIM = 128
MEM_LEN = 256
ATT_GROUPS = ((128, 1), (512, 4), (2048, 16))
ATT_HEADS = 4
ATT_WIDTH = 1536
ATT_OUT = 512
HG_HEADS = 8
HG_WIDTH = 1024
HG_CHUNK = 64
IN_WIDTH = 12800
CROSS_HEADS = 4
CROSS_WIDTH = 512
D_FF = 5632
RMS_EPS = 1e-6
ADAM_LR = 0.001
ADAM_B1 = 0.9
ADAM_B2 = 0.999
ADAM_EPS = 1e-08
ADAM_WD = 0.01
ADAM_STEP = 10
N_DEV = 8

LANE = 128
QA0, KA0, VA0 = 0, 12, 24
QH0, FH0, IH0, GH0 = 36, 44, 52, 60
GA0, GB0 = 68, 84

VMEM_BIG = 56 * 1024 * 1024
MESH = pl.DeviceIdType.MESH


def _cp(vmem=None, **kw):
    if vmem is not None:
        kw["vmem_limit_bytes"] = vmem
    return pltpu.CompilerParams(**kw)


def _pick(n, cands):
    for c in cands:
        if n % c == 0:
            return c
    raise ValueError(f"no tile for {n}")


_ANY = pl.BlockSpec(memory_space=pl.ANY)
_ORDER = [None]


def _pcall(body, *, name, in_specs, out_specs, out_shape, grid=(), scratch_shapes=(), compiler_params=None, prefetch=False):
    def run(*args):
        dep = _ORDER[0]
        specs, operands, kernel_fn = list(in_specs), list(args), body
        if dep is not None:
            at = len(operands)
            specs.append(_ANY)
            operands.append(dep)

            def kernel_fn(*refs):
                return body(*refs[:at], *refs[at + 1:])
        if prefetch:
            call = pl.pallas_call(
                kernel_fn, out_shape=out_shape, name=name, compiler_params=compiler_params,
                grid_spec=pltpu.PrefetchScalarGridSpec(num_scalar_prefetch=1, grid=grid, in_specs=specs, out_specs=out_specs,
                                                       scratch_shapes=scratch_shapes))
        else:
            call = pl.pallas_call(kernel_fn, grid=grid, in_specs=specs, out_specs=out_specs, out_shape=out_shape,
                                  scratch_shapes=scratch_shapes, compiler_params=compiler_params, name=name)
        out = call(*operands)
        _ORDER[0] = out[0] if isinstance(out, (list, tuple)) else out
        return out
    return run


def _mm(a, b, form, out_dtype, name, res=None, tiles=None, a_half=None, b_half=None):
    a_shape = a.shape if a_half is None else (a.shape[0], a.shape[1] // 2)
    b_shape = b.shape if b_half is None else (b.shape[0], b.shape[1] // 2)
    if form == "nn":
        (M, K), (K2, N) = a_shape, b_shape
    elif form == "nt":
        (M, K), (N, K2) = a_shape, b_shape
    else:
        (K, M), (K2, N) = a_shape, b_shape
    assert K == K2, (a.shape, b.shape, form)
    if tiles is None:
        tm = _pick(M, (1024, 1280, 1408, 512, 256))
        tn = _pick(N, (1280, 1024, 1408, 512, 256))
        tk = K if K <= 2048 else _pick(K, (2560, 1408))
    else:
        tm, tn, tk = tiles
    nk = K // tk
    ah, bh = (a_half or 0), (b_half or 0)
    if form == "nn":
        a_spec = pl.BlockSpec((tm, tk), lambda i, j, k: (i, k + ah * nk))
        b_spec = pl.BlockSpec((tk, tn), lambda i, j, k: (k, j + bh * (N // tn)))
        dims = ((1,), (0,))
    elif form == "nt":
        a_spec = pl.BlockSpec((tm, tk), lambda i, j, k: (i, k + ah * nk))
        b_spec = pl.BlockSpec((tn, tk), lambda i, j, k: (j, k + bh * nk))
        dims = ((1,), (1,))
    else:
        a_spec = pl.BlockSpec((tk, tm), lambda i, j, k: (k, i + ah * (M // tm)))
        b_spec = pl.BlockSpec((tk, tn), lambda i, j, k: (k, j + bh * (N // tn)))
        dims = ((0,), (0,))
    o_spec = pl.BlockSpec((tm, tn), lambda i, j, k: (i, j))
    in_specs = [a_spec, b_spec]
    args = [a, b]
    if res is not None:
        in_specs.append(o_spec)
        args.append(res)

    def body(*refs):
        a_ref, b_ref = refs[0], refs[1]
        r_ref = refs[2] if res is not None else None
        o_ref = refs[3] if res is not None else refs[2]
        acc = refs[-1] if nk > 1 else None
        k = pl.program_id(2)
        d = lax.dot_general(a_ref[...].astype(BF), b_ref[...].astype(BF), (dims, ((), ())), preferred_element_type=F32)

        def finish(r):
            if res is not None:
                r = r + r_ref[...].astype(F32)
            o_ref[...] = r.astype(o_ref.dtype)

        if nk == 1:
            finish(d)
        else:
            @pl.when(k == 0)
            def _():
                acc[...] = d

            @pl.when((k > 0) & (k < nk - 1))
            def _():
                acc[...] += d

            @pl.when(k == nk - 1)
            def _():
                finish(acc[...] + d)

    return _pcall(
        body, grid=(M // tm, N // tn, nk), in_specs=in_specs, out_specs=o_spec,
        out_shape=SDS((M, N), out_dtype), scratch_shapes=[pltpu.VMEM((tm, tn), F32)] if nk > 1 else [],
        compiler_params=_cp(VMEM_BIG, dimension_semantics=("parallel", "parallel", "arbitrary")),
        name=name)(*args)


def _rms(x, w):
    return x * lax.rsqrt(jnp.mean(x * x, axis=-1, keepdims=True) + RMS_EPS) * w


def _rms_fwd(x, w, name):
    R, D = x.shape
    tr = 256

    def body(x_ref, w_ref, o_ref):
        o_ref[...] = _rms(x_ref[...], w_ref[...]).astype(o_ref.dtype)

    return _pcall(
        body, grid=(R // tr,),
        in_specs=[pl.BlockSpec((tr, D), lambda i: (i, 0)), pl.BlockSpec((1, D), lambda i: (0, 0))],
        out_specs=pl.BlockSpec((tr, D), lambda i: (i, 0)), out_shape=SDS((R, D), BF), name=name)(x, w)


def _rms_bwd(x, w, dh, dres, name):
    R, D = x.shape
    tr = 256
    want_dx = dres is not None
    dhs = dh if isinstance(dh, (tuple, list)) else (dh,)
    nd = len(dhs)

    def body(*refs):
        x_ref, w_ref, dh_refs = refs[0], refs[1], refs[2:2 + nd]
        if want_dx:
            dr_ref, dx_ref, dxb_ref, dw_ref = refs[2 + nd:]
        else:
            dw_ref = refs[2 + nd]

        @pl.when(pl.program_id(0) == 0)
        def _():
            dw_ref[...] = jnp.zeros_like(dw_ref)

        dhv = jnp.concatenate([r[...].astype(F32) for r in dh_refs], axis=1) if nd > 1 else dh_refs[0][...].astype(F32)
        _, vjp = jax.vjp(_rms, x_ref[...], w_ref[...])
        dx, dw = vjp(dhv)
        dw_ref[...] += dw
        if want_dx:
            dx = dx + dr_ref[...]
            dx_ref[...] = dx
            dxb_ref[...] = dx.astype(BF)

    row = pl.BlockSpec((tr, D), lambda i: (i, 0))
    part = pl.BlockSpec((tr, D // nd), lambda i: (i, 0))
    vec = pl.BlockSpec((1, D), lambda i: (0, 0))
    if want_dx:
        return _pcall(body, grid=(R // tr,), in_specs=[row, vec] + [part] * nd + [row], out_specs=[row, row, vec],
                      out_shape=[SDS((R, D), F32), SDS((R, D), BF), SDS((1, D), F32)], name=name)(x, w, *dhs, dres)
    return _pcall(body, grid=(R // tr,), in_specs=[row, vec] + [part] * nd, out_specs=vec,
                  out_shape=SDS((1, D), F32), name=name)(x, w, *dhs)


def _final_loss(x3, w, tgt):
    R, D = x3.shape
    tr = 256

    def loss_fn(xv, wv, tv):
        err = _rms(xv, wv) - tv
        return 0.5 * jnp.sum(jnp.mean(err * err, axis=-1))

    def body(x_ref, w_ref, t_ref, loss_ref, dx_ref, dxb_ref, dw_ref):
        @pl.when(pl.program_id(0) == 0)
        def _():
            dw_ref[...] = jnp.zeros_like(dw_ref)
            loss_ref[...] = jnp.zeros_like(loss_ref)

        tv = t_ref[...]
        val, vjp = jax.vjp(lambda a, b: loss_fn(a, b, tv), x_ref[...], w_ref[...])
        dx, dw = vjp(jnp.ones((), F32))
        dx_ref[...] = dx
        dxb_ref[...] = dx.astype(BF)
        dw_ref[...] += dw
        loss_ref[...] += jnp.full(loss_ref.shape, val, F32)

    row = pl.BlockSpec((tr, D), lambda i: (i, 0))
    vec = pl.BlockSpec((1, D), lambda i: (0, 0))
    one = pl.BlockSpec((1, LANE), lambda i: (0, 0))
    return _pcall(body, grid=(R // tr,), in_specs=[row, vec, row], out_specs=[one, row, row, vec],
                          out_shape=[SDS((1, LANE), F32), SDS((R, D), F32), SDS((R, D), BF), SDS((1, D), F32)],
                          name="final_loss")(x3, w, tgt)


NEG = -1e30


def _att_blocks(S, d):
    L = S // d
    nb = L // LANE
    return nb, (2 if nb > 1 else 1)


def _att_rows(S, d, idx):
    nb, nkb = _att_blocks(S, d)
    r = idx // nb
    n = idx % nb
    kb = jnp.maximum(n - 1, 0)
    if d == 1:
        qrows = pl.ds(pl.multiple_of(n * LANE, LANE), LANE)
        krows = pl.ds(pl.multiple_of(kb * LANE, LANE), LANE * nkb)
    else:
        qrows = pl.ds(r + n * (LANE * d), LANE, stride=d)
        krows = pl.ds(r + kb * (LANE * d), LANE * nkb, stride=d)
    qpos = n * LANE + lax.broadcasted_iota(jnp.int32, (LANE, LANE * nkb), 0)
    kpos = kb * LANE + lax.broadcasted_iota(jnp.int32, (LANE, LANE * nkb), 1)
    return qrows, krows, qpos - kpos


def _slab(col):
    return pl.BlockSpec((SEQ, LANE), lambda h, col=col: (0, col + h))


def _attn_fwd(proj):
    S = proj.shape[0]
    scale = HEAD_DIM ** -0.5

    def body(q0, q1, q2, k0, k1, k2, v0, v1, v2, o_ref, lse_ref, og, lg):
        qs, ks, vs = (q0, q1, q2), (k0, k1, k2), (v0, v1, v2)
        for g, (window, d) in enumerate(ATT_GROUPS):
            def blk(idx, carry, g=g, window=window, d=d):
                qrows, krows, dist = _att_rows(S, d, idx)
                q = qs[g][qrows, :].astype(BF)
                k = ks[g][krows, :].astype(BF)
                v = vs[g][krows, :].astype(BF)
                s = lax.dot_general(q, k, (((1,), (1,)), ((), ())), preferred_element_type=F32) * scale
                s = jnp.where((dist >= 0) & (dist <= window // d), s, NEG)
                m = jnp.max(s, axis=-1, keepdims=True)
                p = jnp.exp(s - m)
                l = jnp.sum(p, axis=-1, keepdims=True)
                o = jnp.dot((p / l).astype(BF), v, preferred_element_type=F32)
                og[g, qrows, :] = o
                lg[g, qrows, :] = jnp.broadcast_to(m + jnp.log(l), (LANE, LANE))
                return carry
            lax.fori_loop(0, S // LANE, blk, 0, unroll=4)

        def merge(t, carry):
            rows = pl.ds(pl.multiple_of(t * 256, 256), 256)
            l0, l1, l2 = lg[0, rows, :], lg[1, rows, :], lg[2, rows, :]
            m = jnp.maximum(jnp.maximum(l0, l1), l2)
            w0, w1, w2 = jnp.exp(l0 - m), jnp.exp(l1 - m), jnp.exp(l2 - m)
            den = w0 + w1 + w2
            o_ref[rows, :] = (w0 * og[0, rows, :] + w1 * og[1, rows, :] + w2 * og[2, rows, :]) / den
            lse_ref[rows, :] = m + jnp.log(den)
            return carry
        lax.fori_loop(0, S // 256, merge, 0)

    ins = [_slab(QA0 + 4 * g) for g in range(3)] + [_slab(KA0 + 4 * g) for g in range(3)] + [_slab(VA0 + 4 * g) for g in range(3)]
    out = pl.BlockSpec((SEQ, LANE), lambda h: (0, h))
    return _pcall(
        body, grid=(ATT_HEADS,), in_specs=ins, out_specs=[out, out],
        out_shape=[SDS((S, ATT_OUT), F32), SDS((S, ATT_OUT), F32)],
        scratch_shapes=[pltpu.VMEM((3, S, LANE), F32), pltpu.VMEM((3, S, LANE), F32)],
        compiler_params=_cp(VMEM_BIG), name="attn_fwd")(*([proj] * 9))


def _attn_bwd(proj, o_att, lse, do_att):
    S = proj.shape[0]
    scale = HEAD_DIM ** -0.5

    def body(q0, q1, q2, k0, k1, k2, v0, v1, v2, o_ref, lse_ref, do_ref,
             dq0, dq1, dq2, dk0, dk1, dk2, dv0, dv1, dv2, dqa, dka, dva, delta):
        qs, ks, vs = (q0, q1, q2), (k0, k1, k2), (v0, v1, v2)
        dqs, dks, dvs = (dq0, dq1, dq2), (dk0, dk1, dk2), (dv0, dv1, dv2)
        delta[...] = jnp.broadcast_to(jnp.sum(do_ref[...] * o_ref[...], axis=-1, keepdims=True), delta.shape)
        for g, (window, d) in enumerate(ATT_GROUPS):
            dka[...] = jnp.zeros_like(dka)
            dva[...] = jnp.zeros_like(dva)

            def blk(idx, carry, g=g, window=window, d=d):
                qrows, krows, dist = _att_rows(S, d, idx)
                q = qs[g][qrows, :].astype(BF)
                k = ks[g][krows, :].astype(BF)
                v = vs[g][krows, :].astype(BF)
                do = do_ref[qrows, :]
                s = lax.dot_general(q, k, (((1,), (1,)), ((), ())), preferred_element_type=F32) * scale
                p = jnp.exp(s - lse_ref[qrows, :][:, 0:1])
                p = jnp.where((dist >= 0) & (dist <= window // d), p, 0.0)
                dob = do.astype(BF)
                dva[krows, :] += lax.dot_general(p.astype(BF), dob, (((0,), (0,)), ((), ())), preferred_element_type=F32)
                dp = lax.dot_general(dob, v, (((1,), (1,)), ((), ())), preferred_element_type=F32)
                ds = (p * (dp - delta[qrows, :][:, 0:1]) * scale).astype(BF)
                dqa[qrows, :] = jnp.dot(ds, k, preferred_element_type=F32)
                dka[krows, :] += lax.dot_general(ds, q, (((0,), (0,)), ((), ())), preferred_element_type=F32)
                return carry
            lax.fori_loop(0, S // LANE, blk, 0, unroll=4)
            dqs[g][...] = dqa[...].astype(dqs[g].dtype)
            dks[g][...] = dka[...].astype(dks[g].dtype)
            dvs[g][...] = dva[...].astype(dvs[g].dtype)

    cols = [QA0 + 4 * g for g in range(3)] + [KA0 + 4 * g for g in range(3)] + [VA0 + 4 * g for g in range(3)]
    ins = [_slab(c) for c in cols]
    one = pl.BlockSpec((SEQ, LANE), lambda h: (0, h))
    return _pcall(
        body, grid=(ATT_HEADS,), in_specs=ins + [one, one, one], out_specs=[one] * 9,
        out_shape=[SDS((S, ATT_OUT), BF)] * 9,
        scratch_shapes=[pltpu.VMEM((S, LANE), F32)] * 4,
        compiler_params=_cp(VMEM_BIG), name="attn_bwd")(*([proj] * 9), o_att, lse, do_att)


def _bdot(a, b, dims):
    return lax.dot_general(a.astype(BF), b.astype(BF), (dims, ((), ())), preferred_element_type=F32)


def _hg_chunk(qh, fh, ih, gh, lbraw, nw, st):
    C = HG_CHUNK
    mx = jnp.maximum(lbraw[0:1], lbraw[1:2])
    e0 = jnp.exp(lbraw[0:1] - mx)
    e1 = jnp.exp(lbraw[1:2] - mx)
    lb = e0 / (e0 + e1)
    f = lb + (1.0 - lb) * jax.nn.sigmoid(fh)
    logf = jnp.log(f)
    k = 1.0 - f
    q = qh * jax.nn.sigmoid(qh)
    causal = lax.broadcasted_iota(jnp.int32, (C, C), 0) >= lax.broadcasted_iota(jnp.int32, (C, C), 1)
    b = lax.dot_general(causal.astype(F32), logf, (((1,), (0,)), ((), ())), precision=lax.Precision.HIGHEST,
                        preferred_element_type=F32)
    b_last = b[C - 1:C]
    st_out = st * jnp.exp(b_last) + _bdot(ih, k * jnp.exp(b_last - b), ((0,), (0,)))
    inter = _bdot(q * jnp.exp(b), st, ((1,), (1,)))
    b_ref = b[C // 2:C // 2 + 1]
    a = _bdot(q * jnp.exp(b - b_ref), k * jnp.exp(b_ref - b), ((1,), (1,)))
    a = jnp.where(causal, a, 0.0)
    o = inter + _bdot(a, ih, ((1,), (0,)))
    o = o * lax.rsqrt(jnp.mean(o * o, axis=-1, keepdims=True) + RMS_EPS) * nw
    return o * (gh * jax.nn.sigmoid(gh)), st_out


def _to_heads(x):
    return jnp.stack([x[:, h * LANE:(h + 1) * LANE] for h in range(HG_HEADS)])


def _from_heads(x):
    return jnp.concatenate([x[h] for h in range(HG_HEADS)], axis=1)


def _hdot(a, b, ca, cb):
    return lax.dot_general(a.astype(BF), b.astype(BF), (((ca,), (cb,)), ((0,), (0,))), preferred_element_type=F32)


def _hg_chunk_all(qh, fh, ih, gh, lbraw, nw, st):
    C = HG_CHUNK
    mx = jnp.maximum(lbraw[0:1], lbraw[1:2])
    e0 = jnp.exp(lbraw[0:1] - mx)
    e1 = jnp.exp(lbraw[1:2] - mx)
    lb = e0 / (e0 + e1)
    f = lb + (1.0 - lb) * jax.nn.sigmoid(fh)
    logf = jnp.log(f)
    k = 1.0 - f
    q = qh * jax.nn.sigmoid(qh)
    causal = lax.broadcasted_iota(jnp.int32, (C, C), 0) >= lax.broadcasted_iota(jnp.int32, (C, C), 1)
    b = lax.dot_general(causal.astype(F32), logf, (((1,), (0,)), ((), ())), precision=lax.Precision.HIGHEST,
                        preferred_element_type=F32)
    b_last = b[C - 1:C]
    b_ref = b[C // 2:C // 2 + 1]
    v3 = _to_heads(ih)
    st_out = st * _to_heads(jnp.exp(b_last)) + _hdot(v3, _to_heads(k * jnp.exp(b_last - b)), 1, 1)
    inter = _hdot(_to_heads(q * jnp.exp(b)), st, 2, 2)
    a = _hdot(_to_heads(q * jnp.exp(b - b_ref)), _to_heads(k * jnp.exp(b_ref - b)), 2, 2)
    a = jnp.where(causal[None], a, 0.0)
    o = inter + _hdot(a, v3, 2, 1)
    o = o * lax.rsqrt(jnp.mean(o * o, axis=-1, keepdims=True) + RMS_EPS) * nw
    return _from_heads(o) * (gh * jax.nn.sigmoid(gh)), st_out


def _head_cols(h):
    return slice(h * LANE, (h + 1) * LANE)


def _hg_in_specs(chunk_of):
    half = 4 * LANE
    specs = []
    for base in (QH0, FH0, IH0, GH0):
        for part in range(2):
            specs.append(pl.BlockSpec((HG_CHUNK, half), lambda n, col=base // 4 + part: (chunk_of(n), col)))
    return specs


def _hg_sections(sec):
    return [jnp.concatenate([sec[2 * s][...], sec[2 * s + 1][...]], axis=1) for s in range(4)]


def _hg_fwd(proj, lb, nw):
    S = proj.shape[0]
    C = HG_CHUNK
    NC = S // C

    def body(*refs):
        sec = refs[:8]
        lb_ref, nw_ref, o_ref, st_ref, st_scr = refs[8:]

        @pl.when(pl.program_id(0) == 0)
        def _():
            st_scr[...] = jnp.zeros_like(st_scr)

        st = st_scr[...]
        st_ref[:, 0] = st
        o, st_out = _hg_chunk_all(*_hg_sections(sec), lb_ref[...], nw_ref[...], st)
        o_ref[...] = o.astype(o_ref.dtype)
        st_scr[...] = st_out

    return _pcall(
        body, grid=(NC,),
        in_specs=_hg_in_specs(lambda n: n) + [pl.BlockSpec((2, HG_WIDTH), lambda n: (0, 0)), pl.BlockSpec((1, LANE), lambda n: (0, 0))],
        out_specs=[pl.BlockSpec((C, HG_WIDTH), lambda n: (n, 0)), pl.BlockSpec((HG_HEADS, 1, LANE, LANE), lambda n: (0, n, 0, 0))],
        out_shape=[SDS((S, HG_WIDTH), BF), SDS((HG_HEADS, NC, LANE, LANE), F32)],
        scratch_shapes=[pltpu.VMEM((HG_HEADS, LANE, LANE), F32)], name="hg_fwd")(*([proj] * 8), lb, nw)


def _hg_bwd(proj, lb, nw, states, do_hg):
    S = proj.shape[0]
    C = HG_CHUNK
    NC = S // C

    def body(*refs):
        sec = refs[:8]
        lb_ref, nw_ref, st_ref, do_ref, dq_ref, df_ref, di_ref, dg_ref, dlb_ref, dnw_ref, dst_scr = refs[8:]

        @pl.when(pl.program_id(0) == 0)
        def _():
            dst_scr[...] = jnp.zeros_like(dst_scr)
            dlb_ref[...] = jnp.zeros_like(dlb_ref)
            dnw_ref[...] = jnp.zeros_like(dnw_ref)

        _, vjp = jax.vjp(_hg_chunk_all, *_hg_sections(sec), lb_ref[...], nw_ref[...], st_ref[:, 0])
        dq, df, di, dg, dlb, dnw, dst = vjp((do_ref[...].astype(F32), dst_scr[...]))
        dq_ref[...] = dq.astype(dq_ref.dtype)
        df_ref[...] = df.astype(df_ref.dtype)
        di_ref[...] = di.astype(di_ref.dtype)
        dg_ref[...] = dg.astype(dg_ref.dtype)
        dlb_ref[...] += dlb
        dnw_ref[...] += dnw
        dst_scr[...] = dst

    rev = lambda n: NC - 1 - n
    ospec = pl.BlockSpec((C, HG_WIDTH), lambda n: (rev(n), 0))
    lbs = pl.BlockSpec((2, HG_WIDTH), lambda n: (0, 0))
    nws = pl.BlockSpec((1, LANE), lambda n: (0, 0))
    outs = _pcall(
        body, grid=(NC,),
        in_specs=_hg_in_specs(rev) + [lbs, nws, pl.BlockSpec((HG_HEADS, 1, LANE, LANE), lambda n: (0, rev(n), 0, 0)), ospec],
        out_specs=[ospec] * 4 + [lbs, nws],
        out_shape=[SDS((S, HG_WIDTH), BF)] * 4 + [SDS((2, HG_WIDTH), F32), SDS((1, LANE), F32)],
        scratch_shapes=[pltpu.VMEM((HG_HEADS, LANE, LANE), F32)], name="hg_bwd")(*([proj] * 8), lb, nw, states, do_hg)
    return outs


def _merge(ga, gb, ba, bb):
    return jax.nn.sigmoid(ga) * ba + jax.nn.sigmoid(gb) * bb


_MT = 512


def _gate_specs():
    ga = pl.BlockSpec((_MT, _MT), lambda i, j: (i, GA0 * LANE // _MT + j))
    gb = pl.BlockSpec((_MT, _MT), lambda i, j: (i, GB0 * LANE // _MT + j))
    t = pl.BlockSpec((_MT, _MT), lambda i, j: (i, j))
    return ga, gb, t


def _merge_fwd(proj, ba, bb):
    S, D = ba.shape
    ga, gb, t = _gate_specs()

    def body(ga_ref, gb_ref, ba_ref, bb_ref, o_ref):
        o_ref[...] = _merge(ga_ref[...], gb_ref[...], ba_ref[...], bb_ref[...]).astype(o_ref.dtype)

    return _pcall(body, grid=(S // _MT, D // _MT), in_specs=[ga, gb, t, t], out_specs=t,
                          out_shape=SDS((S, D), BF), name="merge_fwd")(proj, proj, ba, bb)


def _merge_bwd(proj, ba, bb, dm):
    S, D = ba.shape
    ga, gb, t = _gate_specs()

    def body(ga_ref, gb_ref, ba_ref, bb_ref, dm_ref, dga_ref, dgb_ref, dba_ref, dbb_ref):
        _, vjp = jax.vjp(_merge, ga_ref[...], gb_ref[...], ba_ref[...], bb_ref[...])
        dga, dgb, dba, dbb = vjp(dm_ref[...])
        dga_ref[...] = dga.astype(BF)
        dgb_ref[...] = dgb.astype(BF)
        dba_ref[...] = dba.astype(BF)
        dbb_ref[...] = dbb.astype(BF)

    return _pcall(body, grid=(S // _MT, D // _MT), in_specs=[ga, gb, t, t, t], out_specs=[t] * 4,
                          out_shape=[SDS((S, D), BF)] * 4, name="merge_bwd")(proj, proj, ba, bb, dm)


def _swiglu(a, b):
    return a * jax.nn.sigmoid(a) * b


def _swiglu_fwd(a, b):
    S, F = a.shape
    tf = _pick(F, (1408, 512))
    t = pl.BlockSpec((_MT, tf), lambda i, j: (i, j))

    def body(a_ref, b_ref, o_ref):
        o_ref[...] = _swiglu(a_ref[...], b_ref[...]).astype(o_ref.dtype)

    return _pcall(body, grid=(S // _MT, F // tf), in_specs=[t, t], out_specs=t,
                          out_shape=SDS((S, F), BF), name="swiglu_fwd")(a, b)


def _swiglu_bwd(a, b, dg):
    S, F = a.shape
    tf = _pick(F, (1408, 512))
    t = pl.BlockSpec((_MT, tf), lambda i, j: (i, j))

    def body(a_ref, b_ref, dg_ref, da_ref, db_ref):
        _, vjp = jax.vjp(_swiglu, a_ref[...], b_ref[...])
        da, db = vjp(dg_ref[...].astype(F32))
        da_ref[...] = da.astype(BF)
        db_ref[...] = db.astype(BF)

    return _pcall(body, grid=(S // _MT, F // tf), in_specs=[t, t, t], out_specs=[t, t],
                          out_shape=[SDS((S, F), BF)] * 2, name="swiglu_bwd")(a, b, dg)


def _cross(q, k, v):
    s = _bdot(q, k, ((1,), (1,))) * (HEAD_DIM ** -0.5)
    e = jnp.exp(s - jnp.max(s, axis=-1, keepdims=True))
    p = e / jnp.sum(e, axis=-1, keepdims=True)
    return _bdot(p, v, ((1,), (0,)))


def _cross_specs():
    q = pl.BlockSpec((_MT, LANE), lambda h, i: (i, h))
    k = pl.BlockSpec((MEM_LEN, LANE), lambda h, i: (0, h))
    v = pl.BlockSpec((MEM_LEN, LANE), lambda h, i: (0, CROSS_HEADS + h))
    return q, k, v


def _cross_fwd(qc, kvc):
    S = qc.shape[0]
    q, k, v = _cross_specs()

    def body(q_ref, k_ref, v_ref, o_ref):
        o_ref[...] = _cross(q_ref[...], k_ref[...], v_ref[...]).astype(o_ref.dtype)

    return _pcall(body, grid=(CROSS_HEADS, S // _MT), in_specs=[q, k, v], out_specs=q,
                          out_shape=SDS((S, CROSS_WIDTH), BF), name="cross_fwd")(qc, kvc, kvc)


def _cross_bwd(qc, kvc, doc):
    S = qc.shape[0]
    q, k, v = _cross_specs()

    def body(q_ref, k_ref, v_ref, do_ref, dq_ref, dk_ref, dv_ref):
        @pl.when(pl.program_id(1) == 0)
        def _():
            dk_ref[...] = jnp.zeros_like(dk_ref)
            dv_ref[...] = jnp.zeros_like(dv_ref)

        _, vjp = jax.vjp(_cross, q_ref[...], k_ref[...], v_ref[...])
        dq, dk, dv = vjp(do_ref[...])
        dq_ref[...] = dq.astype(dq_ref.dtype)
        dk_ref[...] += dk
        dv_ref[...] += dv

    return _pcall(body, grid=(CROSS_HEADS, S // _MT), in_specs=[q, k, v, q], out_specs=[q, k, k],
                          out_shape=[SDS((S, CROSS_WIDTH), BF), SDS((MEM_LEN, CROSS_WIDTH), F32), SDS((MEM_LEN, CROSS_WIDTH), F32)],
                          name="cross_bwd")(qc, kvc, kvc, doc)


def _local_step(x, mem, tgt, p, ready):
    h = _rms_fwd(x, p["ln_mix"], "rms_mix")
    proj = _mm(h, p["w_in_a"], "nt", F32, "mm_proj_a", a_half=0)
    proj = _mm(h, p["w_in_b"], "nt", F32, "mm_proj_b", a_half=1, res=proj)
    o_att, lse = _attn_fwd(proj)
    o_hg, states = _hg_fwd(proj, p["hg_lb"], p["hg_norm"])
    ba = _mm(o_att, p["w_ba"], "nn", F32, "mm_ba")
    bb = _mm(o_hg, p["w_bb"], "nn", F32, "mm_bb")
    merged = _merge_fwd(proj, ba, bb)
    x1 = _mm(merged, p["w_out"], "nn", F32, "mm_out", res=x)
    hc = _rms_fwd(x1, p["ln_cross"], "rms_cross")
    qc = _mm(hc, p["wq"], "nn", F32, "mm_q")
    mn = _rms_fwd(mem, p["ln_mem"], "rms_mem")
    kvc = _mm(mn, p["wkv"], "nn", F32, "mm_kv")
    oc = _cross_fwd(qc, kvc)
    x2 = _mm(oc, p["wo"], "nn", F32, "mm_o", res=x1)
    hf = _rms_fwd(x2, p["ln_ffn"], "rms_ffn")
    a = _mm(hf, p["w1"], "nt", F32, "mm_w1")
    b = _mm(hf, p["w3"], "nt", F32, "mm_w3")
    g = _swiglu_fwd(a, b)
    x3 = _mm(g, p["w2"], "nn", F32, "mm_w2", res=x2)
    loss, dx3, dx3b, d_ln_final = _final_loss(x3, p["ln_final"], tgt)

    gw, gs = {}, {"ln_final": d_ln_final}
    dg = _mm(dx3b, p["w2"], "nt", BF, "mm_dg")
    gw["w2"] = _mm(g, dx3b, "tn", BF, "mm_dw2")
    da, db = _swiglu_bwd(a, b, dg)
    dhf = _mm(da, p["w1"], "nn", F32, "mm_dhf1")
    dhf = _mm(db, p["w3"], "nn", F32, "mm_dhf3", res=dhf)
    gw["w1"] = _mm(da, hf, "tn", BF, "mm_dw1")
    gw["w3"] = _mm(db, hf, "tn", BF, "mm_dw3")
    ready("ffn", gw)
    dx2, dx2b, gs["ln_ffn"] = _rms_bwd(x2, p["ln_ffn"], dhf, dx3, "rms_ffn_bwd")
    doc = _mm(dx2b, p["wo"], "nt", F32, "mm_doc")
    gw["wo"] = _mm(oc, dx2b, "tn", BF, "mm_dwo")
    dqc, dk, dv = _cross_bwd(qc, kvc, doc)
    dkvc = jnp.concatenate([dk, dv], axis=1)
    dhc = _mm(dqc, p["wq"], "nt", F32, "mm_dhc")
    gw["wq"] = _mm(hc, dqc, "tn", BF, "mm_dwq")
    dmn = _mm(dkvc, p["wkv"], "nt", F32, "mm_dmn")
    gw["wkv"] = _mm(mn, dkvc, "tn", BF, "mm_dwkv")
    ready("cross", gw)
    gs["ln_mem"] = _rms_bwd(mem, p["ln_mem"], dmn, None, "rms_mem_bwd")
    dx1, dx1b, gs["ln_cross"] = _rms_bwd(x1, p["ln_cross"], dhc, dx2, "rms_cross_bwd")
    dmerged = _mm(dx1b, p["w_out"], "nt", F32, "mm_dmerged")
    gw["w_out"] = _mm(merged, dx1b, "tn", BF, "mm_dwout")
    dga, dgb, dba, dbb = _merge_bwd(proj, ba, bb, dmerged)
    do_att = _mm(dba, p["w_ba"], "nt", F32, "mm_doatt")
    gw["w_ba"] = _mm(o_att, dba, "tn", BF, "mm_dwba")
    do_hg = _mm(dbb, p["w_bb"], "nt", F32, "mm_dohg")
    gw["w_bb"] = _mm(o_hg, dbb, "tn", BF, "mm_dwbb")
    ready("mix", gw)
    datt = _attn_bwd(proj, o_att, lse, do_att)
    dqh, dfh, dih, dgh, gs["hg_lb"], gs["hg_norm"] = _hg_bwd(proj, p["hg_lb"], p["hg_norm"], states, do_hg)
    ready("mixers_done", gw)
    dproj = jnp.concatenate([*datt, dqh, dfh, dih, dgh, dga, dgb], axis=1)
    gw["w_in_a"] = _mm(dproj, h, "tn", BF, "mm_dwin_a", b_half=0)
    ready("in_a", gw)
    gw["w_in_b"] = _mm(dproj, h, "tn", BF, "mm_dwin_b", b_half=1)
    ready("in_b", gw)
    dh_a = _mm(dproj, p["w_in_a"], "nn", F32, "mm_dh_a")
    ready("dh_half", gw)
    dh = (dh_a, _mm(dproj, p["w_in_b"], "nn", F32, "mm_dh_b"))
    grad_x, _, gs["ln_mix"] = _rms_bwd(x, p["ln_mix"], dh, dx1, "rms_mix_bwd")
    return loss, grad_x, gs


_MATS = (
    ("w_in_a", "w_in", IN_WIDTH, D_MODEL // 2, "row", IN_WIDTH // N_DEV, True),
    ("w_in_b", "w_in", IN_WIDTH, D_MODEL // 2, "row", IN_WIDTH // N_DEV, True),
    ("w_ba", "w_branch_a", ATT_OUT, D_MODEL, "col", D_MODEL // N_DEV, False),
    ("w_bb", "w_branch_b", HG_WIDTH, D_MODEL, "col", D_MODEL // N_DEV, False),
    ("w_out", "w_out", D_MODEL, D_MODEL, "row", D_MODEL // N_DEV, False),
    ("wq", "wq_cross", D_MODEL, CROSS_WIDTH, "row", D_MODEL // N_DEV, False),
    ("wkv", "wkv_cross", D_MODEL, 2 * CROSS_WIDTH, "row", D_MODEL // N_DEV, False),
    ("wo", "wo_cross", CROSS_WIDTH, D_MODEL, "col", D_MODEL // N_DEV, False),
    ("w1", "w1", D_FF, D_MODEL, "row", D_FF // N_DEV, True),
    ("w3", "w3", D_FF, D_MODEL, "row", D_FF // N_DEV, True),
    ("w2", "w2", D_FF, D_MODEL, "row", D_FF // N_DEV, False),
)
_INDEX = {m[0]: i for i, m in enumerate(_MATS)}


def _unit_shape(i):
    _, _, K, N, kind, sh, _ = _MATS[i]
    return (sh, N) if kind == "row" else (K, sh)


def _win(ref, i, chip, par):
    _, _, _, _, kind, sh, _ = _MATS[i]
    if kind == "row":
        return ref.at[pl.ds(pl.multiple_of((2 * chip + par) * sh, 16), sh), :]
    return ref.at[:, pl.ds(pl.multiple_of((2 * chip + par) * sh, LANE), sh)]


def _my_place():
    x, y, c = lax.axis_index("x"), lax.axis_index("y"), lax.axis_index("c")
    chips = [(1 - x, y), (x, 1 - y), (1 - x, 1 - y)]
    return x, y, c, chips


def _cast_unit(w, name, half=None):
    K, N = w.shape
    if half is not None:
        N = N // 2
    tr = _pick(K, (256, 400, 352))

    def body(w_ref, o_ref):
        o_ref[...] = w_ref[...].astype(BF)

    src = pl.BlockSpec((tr, N), lambda i: (i, half or 0))
    dst = pl.BlockSpec((tr, N), lambda i: (i, 0))
    return _pcall(body, grid=(K // tr,), in_specs=[src], out_specs=dst, out_shape=SDS((K, N), BF), name=name)(w)


def _handshake(peers):
    barrier = pltpu.get_barrier_semaphore()
    for peer in peers:
        pl.semaphore_signal(barrier, inc=1, device_id=peer, device_id_type=MESH)
    pl.semaphore_wait(barrier, len(peers))


def _sequencer(body, out_type, sems, cid, name):
    return pl.kernel(body, out_type=out_type, mesh=plsc.ScalarSubcoreMesh(axis_name="seq", num_cores=1),
                     scratch_types=sems, compiler_params=pltpu.CompilerParams(collective_id=cid), name=name)


def _all_gather(units, idx, cid, name):
    n = len(idx)

    def body(*refs):
        us, fulls = refs[:n], refs[n:2 * n]
        ssem, rsem, lsem = refs[2 * n:]
        x, y, c, chips = _my_place()
        _handshake([(x, y, 1 - c)] + [(*ch, c) for ch in chips])

        def win(i, chip_xy, par):
            return _win(fulls[i], idx[i], 2 * chip_xy[0] + chip_xy[1], par)

        def rcopy(i, k, src, dst, to):
            return pltpu.make_async_remote_copy(src_ref=src, dst_ref=dst, send_sem=ssem.at[i, k], recv_sem=rsem.at[i, k],
                                                device_id=to, device_id_type=MESH)

        sends, locs = [], []
        for i in range(n):
            mine = win(i, (x, y), c)
            loc = pltpu.make_async_copy(us[i], mine, lsem.at[i])
            loc.start()
            locs.append(loc)
            first = [rcopy(i, 0, us[i], mine, (x, y, 1 - c))]
            first += [rcopy(i, 1 + j, us[i], mine, (*chips[j], c)) for j in range(3)]
            for cp in first:
                cp.start()
            sends += first
        for i in range(n):
            for j in range(3):
                blk = win(i, chips[j], c)
                rcopy(i, 1 + j, blk, blk, (x, y, c)).wait_recv()
                fwd = rcopy(i, 4 + j, blk, blk, (x, y, 1 - c))
                fwd.start()
                sends.append(fwd)
        for i in range(n):
            blk = win(i, (x, y), 1 - c)
            rcopy(i, 0, blk, blk, (x, y, c)).wait_recv()
            for j in range(3):
                blk = win(i, chips[j], 1 - c)
                rcopy(i, 4 + j, blk, blk, (x, y, c)).wait_recv()
        for cp in sends:
            cp.wait_send()
        for loc in locs:
            loc.wait()

    out_type = [SDS((_MATS[i][2], _MATS[i][3]), BF) for i in idx]
    sems = [pltpu.SemaphoreType.DMA((n, 7)), pltpu.SemaphoreType.DMA((n, 7)), pltpu.SemaphoreType.DMA((n,))]
    return _sequencer(body, out_type, sems, cid, name)(*units)


def _rs_sibling(grads, idx, cid, name):
    n = len(idx)

    def body(*refs):
        gs, bufs = refs[:n], refs[n:2 * n]
        ssem, rsem = refs[2 * n:]
        x, y, c, _ = _my_place()
        _handshake([(x, y, 1 - c)])
        cps = []
        for i in range(n):
            for k in range(4):
                cp = pltpu.make_async_remote_copy(
                    src_ref=_win(gs[i], idx[i], k, 1 - c), dst_ref=bufs[i].at[k], send_sem=ssem.at[i, k], recv_sem=rsem.at[i, k],
                    device_id=(x, y, 1 - c), device_id_type=MESH)
                cp.start()
                cps.append(cp)
        for cp in cps:
            cp.wait()

    out_type = [SDS((4, *_unit_shape(i)), BF) for i in idx]
    sems = [pltpu.SemaphoreType.DMA((n, 4)), pltpu.SemaphoreType.DMA((n, 4))]
    return _sequencer(body, out_type, sems, cid, name)(*grads)


def _chip_sum(g, sib, i, place, name):
    _, _, K, N, kind, sh, _ = _MATS[i]

    def body(pref, g_ref, s_ref, o_ref):
        o_ref[...] = (g_ref[...].astype(F32) + s_ref[...].astype(F32)).astype(BF)

    if kind == "row":
        tr = _pick(sh, (800, 352, 256))
        per = sh // tr
        grid = (4, per)
        g_spec = pl.BlockSpec((tr, N), lambda k, r, pref: ((2 * k + pref[0]) * per + r, 0))
        u_spec = pl.BlockSpec((None, tr, N), lambda k, r, pref: (k, r, 0))
    else:
        grid = (4,)
        g_spec = pl.BlockSpec((K, sh), lambda k, pref: (0, 2 * k + pref[0]))
        u_spec = pl.BlockSpec((None, K, sh), lambda k, pref: (k, 0, 0))
    return _pcall(body, grid=grid, in_specs=[g_spec, u_spec], out_specs=u_spec, out_shape=SDS((4, *_unit_shape(i)), BF),
                  prefetch=True, name=name)(place, g, sib)


def _rs_chips(parts, idx, cid, name):
    n = len(idx)

    def body(*refs):
        ts, bufs = refs[:n], refs[n:2 * n]
        ssem, rsem = refs[2 * n:]
        x, y, c, chips = _my_place()
        _handshake([(*ch, c) for ch in chips])
        cps = []
        for i in range(n):
            for j in range(3):
                cp = pltpu.make_async_remote_copy(
                    src_ref=ts[i].at[2 * chips[j][0] + chips[j][1]], dst_ref=bufs[i].at[j], send_sem=ssem.at[i, j], recv_sem=rsem.at[i, j],
                    device_id=(*chips[j], c), device_id_type=MESH)
                cp.start()
                cps.append(cp)
        for cp in cps:
            cp.wait()

    out_type = [SDS((3, *_unit_shape(i)), BF) for i in idx]
    sems = [pltpu.SemaphoreType.DMA((n, 3)), pltpu.SemaphoreType.DMA((n, 3))]
    return _sequencer(body, out_type, sems, cid, name)(*parts)


def _adamw(w, g, m, v):
    m = ADAM_B1 * m + (1.0 - ADAM_B1) * g
    v = ADAM_B2 * v + (1.0 - ADAM_B2) * jnp.square(g)
    m_hat = m / (1.0 - ADAM_B1 ** ADAM_STEP)
    v_hat = v / (1.0 - ADAM_B2 ** ADAM_STEP)
    delta = -ADAM_LR * (m_hat / (jnp.sqrt(v_hat) + ADAM_EPS) + ADAM_WD * w)
    return delta, m, v


def _adam_mat(parts, recvs, w, m, v, place, name):
    R, C = w.shape
    n = len(parts)
    tr = _pick(R, (256, 160, 176))

    def body(pref, *refs):
        t_refs, r_refs = refs[:n], refs[n:2 * n]
        w_ref, m_ref, v_ref, g_out, d_out, m_out, v_out = refs[2 * n:]
        cols = [t[...].astype(F32) + r[0].astype(F32) + r[1].astype(F32) + r[2].astype(F32) for t, r in zip(t_refs, r_refs)]
        g = jnp.concatenate(cols, axis=1) if n > 1 else cols[0]
        d, mn, vn = _adamw(w_ref[...], g, m_ref[...], v_ref[...])
        g_out[...] = g
        d_out[...] = d
        m_out[...] = mn
        v_out[...] = vn

    t_spec = pl.BlockSpec((None, tr, C // n), lambda i, pref: (pref[1], i, 0))
    r_spec = pl.BlockSpec((3, tr, C // n), lambda i, pref: (0, i, 0))
    s_spec = pl.BlockSpec((tr, C), lambda i, pref: (i, 0))
    return _pcall(body, grid=(R // tr,), in_specs=[t_spec] * n + [r_spec] * n + [s_spec] * 3, out_specs=[s_spec] * 4,
                  out_shape=[SDS((R, C), F32)] * 4, prefetch=True, name=name)(place, *parts, *recvs, w, m, v)


_SMALL = (("ln_mix", "ln_mix_w", D_MODEL), ("ln_cross", "ln_cross_w", D_MODEL), ("ln_mem", "ln_mem_w", D_MODEL),
          ("ln_ffn", "ln_ffn_w", D_MODEL), ("ln_final", "ln_final_w", D_MODEL), ("hg_lb", "hg_lower_bounds", 2 * HG_WIDTH),
          ("hg_norm", "hg_norm_w", HEAD_DIM))
_PACK = sum(n for _, _, n in _SMALL) + LANE


def _small_sync(grow, wrow, mrow, vrow):
    def body(g_ref, w_ref, m_ref, v_ref, o_ref, gath, ssem, rsem):
        x, y, c, _ = _my_place()
        me = 4 * x + 2 * y + c
        gath[pl.ds(me, 1), :] = g_ref[...]
        cps = []
        for dlt in range(1, N_DEV):
            peer = (me + dlt) % N_DEV
            cp = pltpu.make_async_remote_copy(
                src_ref=g_ref, dst_ref=gath.at[pl.ds(me, 1), :], send_sem=ssem.at[peer], recv_sem=rsem.at[me],
                device_id=(peer // 4, (peer // 2) % 2, peer % 2), device_id_type=MESH)
            cp.start()
            cps.append(cp)
        for dlt in range(1, N_DEV):
            src = (me + dlt) % N_DEV
            pltpu.make_async_remote_copy(
                src_ref=g_ref, dst_ref=gath.at[pl.ds(src, 1), :], send_sem=ssem.at[src], recv_sem=rsem.at[src],
                device_id=(x, y, c), device_id_type=MESH).wait_recv()
        for cp in cps:
            cp.wait_send()
        g = gath[0:1, :]
        for j in range(1, N_DEV):
            g = g + gath[j:j + 1, :]
        d, mn, vn = _adamw(w_ref[...], g, m_ref[...], v_ref[...])
        o_ref[0:1, :] = g
        o_ref[1:2, :] = d
        o_ref[2:3, :] = mn
        o_ref[3:4, :] = vn

    vm = pl.BlockSpec(memory_space=pltpu.VMEM)
    return _pcall(
        body, in_specs=[vm] * 4, out_specs=vm, out_shape=SDS((4, _PACK), F32),
        scratch_shapes=[pltpu.VMEM((N_DEV, _PACK), F32), pltpu.SemaphoreType.DMA((N_DEV,)), pltpu.SemaphoreType.DMA((N_DEV,))],
        name="small_sync")(grow, wrow, mrow, vrow)


def _pack_small(d, loss_row=None):
    parts = [d[k].reshape(1, n).astype(F32) for k, _, n in _SMALL]
    parts.append(loss_row if loss_row is not None else jnp.ones((1, LANE), F32))
    return jnp.concatenate(parts, axis=1)


def kernel(x, mem, ln_mix_w, w_in, hg_norm_w, hg_lower_bounds, w_branch_a, w_branch_b, w_out, ln_cross_w, ln_mem_w, wq_cross, wkv_cross, wo_cross, ln_ffn_w, w1, w3, w2, ln_final_w, loss_target, m_ln_mix_w, m_w_in, m_hg_norm_w, m_hg_lower_bounds, m_w_branch_a, m_w_branch_b, m_w_out, m_ln_cross_w, m_ln_mem_w, m_wq_cross, m_wkv_cross, m_wo_cross, m_ln_ffn_w, m_w1, m_w3, m_w2, m_ln_final_w, v_ln_mix_w, v_w_in, v_hg_norm_w, v_hg_lower_bounds, v_w_branch_a, v_w_branch_b, v_w_out, v_ln_cross_w, v_ln_mem_w, v_wq_cross, v_wkv_cross, v_wo_cross, v_ln_ffn_w, v_w1, v_w3, v_w2, v_ln_final_w):
    given = dict(locals())
    place = jnp.stack([lax.axis_index("c"), 2 * lax.axis_index("x") + lax.axis_index("y")]).astype(jnp.int32)

    _ORDER[0] = None
    index = _INDEX

    def local_view(a, i):
        return a[0].T if _MATS[i][6] else a[0]

    shard = {key: local_view(given[ref], i) for i, (key, ref, *_) in enumerate(_MATS)}

    p = {}
    col_half = {"w_in_a": 0, "w_in_b": 1}
    waves = (("in_a", ["w_in_a"], 1), ("in_b", ["w_in_b"], 12), ("mid", ["w_ba", "w_bb", "w_out", "wq", "wkv", "wo"], 2),
             ("ffn", ["w1", "w3", "w2"], 11))
    for wave, keys, cid in waves:
        units = [_cast_unit(shard[k], "cast_" + k, col_half.get(k)) for k in keys]
        p.update(zip(keys, _all_gather(units, [index[k] for k in keys], cid, "all_gather_" + wave)))
    small_w = {"ln_mix": ln_mix_w, "ln_cross": ln_cross_w, "ln_mem": ln_mem_w, "ln_ffn": ln_ffn_w,
               "ln_final": ln_final_w.reshape(1, D_MODEL), "hg_lb": hg_lower_bounds, "hg_norm": hg_norm_w}
    p.update(small_w)

    groups = {"ffn": ["w2", "w1", "w3"], "cross": ["wo", "wq", "wkv"], "mix": ["w_out", "w_ba", "w_bb"],
              "in_a": ["w_in_a"], "in_b": ["w_in_b"]}
    cids = {"ffn": (3, 4), "cross": (5, 6), "mix": (7, 8), "in_a": (9, 10), "in_b": (13, 14)}
    sib, parts, recv, out = {}, {}, {}, {}

    grads = {}

    def chip_stage(name):
        keys = groups[name]
        parts[name] = [_chip_sum(grads[k], s, index[k], place, "chip_sum_" + k) for k, s in zip(keys, sib[name])]
        recv[name] = _rs_chips(parts[name], [index[k] for k in keys], cids[name][1], "rs_chips_" + name)

    def adam(ref, i, part_list, recv_list):
        res = _adam_mat(part_list, recv_list, shard[_MATS[i][0]], local_view(given["m_" + ref], i), local_view(given["v_" + ref], i),
                        place, "adam_" + ref)
        out[ref] = [(a.T if _MATS[i][6] else a)[None] for a in res]

    def adam_stage(name):
        for k, t, r in zip(groups[name], parts[name], recv[name]):
            adam(_MATS[index[k]][1], index[k], [t], [r])

    def ready(name, gw):
        if name == "mixers_done":
            adam_stage("ffn")
            chip_stage("mix")
            adam_stage("cross")
            return
        if name == "dh_half":
            chip_stage("in_b")
            return
        keys = groups[name]
        grads.update({k: gw[k] for k in keys})
        sib[name] = _rs_sibling([gw[k] for k in keys], [index[k] for k in keys], cids[name][0], "rs_sibling_" + name)
        if name == "cross":
            chip_stage("ffn")
        if name == "mix":
            chip_stage("cross")
        if name == "in_b":
            adam_stage("mix")
            chip_stage("in_a")

    loss_row, grad_x, gs = _local_step(x[0], mem[0], loss_target[0], p, ready)

    sm = {k: given["m_" + ref] for k, ref, _ in _SMALL}
    sv = {k: given["v_" + ref] for k, ref, _ in _SMALL}
    synced = _small_sync(_pack_small(gs, loss_row), _pack_small(small_w), _pack_small(sm), _pack_small(sv))
    adam("w_in", index["w_in_a"], parts["in_a"] + parts["in_b"], recv["in_a"] + recv["in_b"])
    off = 0
    for k, ref, n in _SMALL:
        shape = given[ref].shape
        out[ref] = [synced[r, off:off + n].reshape(shape) for r in range(4)]
        off += n
    loss = synced[0, off]

    order = ["ln_mix_w", "w_in", "hg_norm_w", "hg_lower_bounds", "w_branch_a", "w_branch_b", "w_out", "ln_cross_w", "ln_mem_w",
             "wq_cross", "wkv_cross", "wo_cross", "ln_ffn_w", "w1", "w3", "w2", "ln_final_w"]
    return (loss, grad_x[None], *[out[n][0] for n in order], *[out[n][1] for n in order],
            *[out[n][2] for n in order], *[out[n][3] for n in order])
```

```python
import functools
import math

import jax
import jax.numpy as jnp
from jax import lax
from jax.experimental import pallas as pl
from jax.experimental.pallas import tpu as pltpu
from jax.experimental.pallas import tpu_sc as plsc

F32 = jnp.float32
BF = jnp.bfloat16
SDS = jax.ShapeDtypeStruct

D_MODEL = 2048
SEQ = 2048
HEAD_DIM = 128
MEM_LEN = 256
ATT_GROUPS = ((128, 1), (512, 4), (2048, 16))
ATT_HEADS = 4
ATT_WIDTH = 1536
ATT_OUT = 512
HG_HEADS = 8
HG_WIDTH = 1024
HG_CHUNK = 64
IN_WIDTH = 12800
CROSS_HEADS = 4
CROSS_WIDTH = 512
D_FF = 5632
RMS_EPS = 1e-6
ADAM_LR = 0.001
ADAM_B1 = 0.9
ADAM_B2 = 0.999
ADAM_EPS = 1e-08
ADAM_WD = 0.01
ADAM_STEP = 10
N_DEV = 8

LANE = 128
QA0, KA0, VA0 = 0, 12, 24
QH0, FH0, IH0, GH0 = 36, 44, 52, 60
GA0, GB0 = 68, 84

VMEM_BIG = 56 * 1024 * 1024
MESH = pl.DeviceIdType.MESH


def _cp(vmem=None, **kw):
    if vmem is not None:
        kw["vmem_limit_bytes"] = vmem
    return pltpu.CompilerParams(**kw)


def _pick(n, cands):
    for c in cands:
        if n % c == 0:
            return c
    raise ValueError(f"no tile for {n}")


_ANY = pl.BlockSpec(memory_space=pl.ANY)
_ORDER = [None]


def _pcall(body, *, name, in_specs, out_specs, out_shape, grid=(), scratch_shapes=(), compiler_params=None, prefetch=False):
    def run(*args):
        dep = _ORDER[0]
        specs, operands, kernel_fn = list(in_specs), list(args), body
        if dep is not None:
            at = len(operands)
            specs.append(_ANY)
            operands.append(dep)

            def kernel_fn(*refs):
                return body(*refs[:at], *refs[at + 1:])
        if prefetch:
            call = pl.pallas_call(
                kernel_fn, out_shape=out_shape, name=name, compiler_params=compiler_params,
                grid_spec=pltpu.PrefetchScalarGridSpec(num_scalar_prefetch=1, grid=grid, in_specs=specs, out_specs=out_specs,
                                                       scratch_shapes=scratch_shapes))
        else:
            call = pl.pallas_call(kernel_fn, grid=grid, in_specs=specs, out_specs=out_specs, out_shape=out_shape,
                                  scratch_shapes=scratch_shapes, compiler_params=compiler_params, name=name)
        out = call(*operands)
        _ORDER[0] = out[0] if isinstance(out, (list, tuple)) else out
        return out
    return run


def _mm(a, b, form, out_dtype, name, res=None, tiles=None, a_half=None, b_half=None):
    a_shape = a.shape if a_half is None else (a.shape[0], a.shape[1] // 2)
    b_shape = b.shape if b_half is None else (b.shape[0], b.shape[1] // 2)
    if form == "nn":
        (M, K), (K2, N) = a_shape, b_shape
    elif form == "nt":
        (M, K), (N, K2) = a_shape, b_shape
    else:
        (K, M), (K2, N) = a_shape, b_shape
    assert K == K2, (a.shape, b.shape, form)
    if tiles is None:
        tm = _pick(M, (1024, 1280, 1408, 512, 256))
        tn = _pick(N, (1280, 1024, 1408, 512, 256))
        tk = K if K <= 2048 else _pick(K, (2560, 1408))
    else:
        tm, tn, tk = tiles
    nk = K // tk
    ah, bh = (a_half or 0), (b_half or 0)
    if form == "nn":
        a_spec = pl.BlockSpec((tm, tk), lambda i, j, k: (i, k + ah * nk))
        b_spec = pl.BlockSpec((tk, tn), lambda i, j, k: (k, j + bh * (N // tn)))
        dims = ((1,), (0,))
    elif form == "nt":
        a_spec = pl.BlockSpec((tm, tk), lambda i, j, k: (i, k + ah * nk))
        b_spec = pl.BlockSpec((tn, tk), lambda i, j, k: (j, k + bh * nk))
        dims = ((1,), (1,))
    else:
        a_spec = pl.BlockSpec((tk, tm), lambda i, j, k: (k, i + ah * (M // tm)))
        b_spec = pl.BlockSpec((tk, tn), lambda i, j, k: (k, j + bh * (N // tn)))
        dims = ((0,), (0,))
    o_spec = pl.BlockSpec((tm, tn), lambda i, j, k: (i, j))
    in_specs = [a_spec, b_spec]
    args = [a, b]
    if res is not None:
        in_specs.append(o_spec)
        args.append(res)

    def body(*refs):
        a_ref, b_ref = refs[0], refs[1]
        r_ref = refs[2] if res is not None else None
        o_ref = refs[3] if res is not None else refs[2]
        acc = refs[-1] if nk > 1 else None
        k = pl.program_id(2)
        d = lax.dot_general(a_ref[...].astype(BF), b_ref[...].astype(BF), (dims, ((), ())), preferred_element_type=F32)

        def finish(r):
            if res is not None:
                r = r + r_ref[...].astype(F32)
            o_ref[...] = r.astype(o_ref.dtype)

        if nk == 1:
            finish(d)
        else:
            @pl.when(k == 0)
            def _():
                acc[...] = d

            @pl.when((k > 0) & (k < nk - 1))
            def _():
                acc[...] += d

            @pl.when(k == nk - 1)
            def _():
                finish(acc[...] + d)

    return _pcall(
        body, grid=(M // tm, N // tn, nk), in_specs=in_specs, out_specs=o_spec,
        out_shape=SDS((M, N), out_dtype), scratch_shapes=[pltpu.VMEM((tm, tn), F32)] if nk > 1 else [],
        compiler_params=_cp(VMEM_BIG, dimension_semantics=("parallel", "parallel", "arbitrary")),
        name=name)(*args)


def _rms(x, w):
    return x * lax.rsqrt(jnp.mean(x * x, axis=-1, keepdims=True) + RMS_EPS) * w


def _rms_fwd(x, w, name):
    R, D = x.shape
    tr = 256

    def body(x_ref, w_ref, o_ref):
        o_ref[...] = _rms(x_ref[...], w_ref[...]).astype(o_ref.dtype)

    return _pcall(
        body, grid=(R // tr,),
        in_specs=[pl.BlockSpec((tr, D), lambda i: (i, 0)), pl.BlockSpec((1, D), lambda i: (0, 0))],
        out_specs=pl.BlockSpec((tr, D), lambda i: (i, 0)), out_shape=SDS((R, D), BF), name=name)(x, w)


def _rms_bwd(x, w, dh, dres, name):
    R, D = x.shape
    tr = 256
    want_dx = dres is not None
    dhs = dh if isinstance(dh, (tuple, list)) else (dh,)
    nd = len(dhs)

    def body(*refs):
        x_ref, w_ref, dh_refs = refs[0], refs[1], refs[2:2 + nd]
        if want_dx:
            dr_ref, dx_ref, dxb_ref, dw_ref = refs[2 + nd:]
        else:
            dw_ref = refs[2 + nd]

        @pl.when(pl.program_id(0) == 0)
        def _():
            dw_ref[...] = jnp.zeros_like(dw_ref)

        dhv = jnp.concatenate([r[...].astype(F32) for r in dh_refs], axis=1) if nd > 1 else dh_refs[0][...].astype(F32)
        _, vjp = jax.vjp(_rms, x_ref[...], w_ref[...])
        dx, dw = vjp(dhv)
        dw_ref[...] += dw
        if want_dx:
            dx = dx + dr_ref[...]
            dx_ref[...] = dx
            dxb_ref[...] = dx.astype(BF)

    row = pl.BlockSpec((tr, D), lambda i: (i, 0))
    part = pl.BlockSpec((tr, D // nd), lambda i: (i, 0))
    vec = pl.BlockSpec((1, D), lambda i: (0, 0))
    if want_dx:
        return _pcall(body, grid=(R // tr,), in_specs=[row, vec] + [part] * nd + [row], out_specs=[row, row, vec],
                      out_shape=[SDS((R, D), F32), SDS((R, D), BF), SDS((1, D), F32)], name=name)(x, w, *dhs, dres)
    return _pcall(body, grid=(R // tr,), in_specs=[row, vec] + [part] * nd, out_specs=vec,
                  out_shape=SDS((1, D), F32), name=name)(x, w, *dhs)


def _final_loss(x3, w, tgt):
    R, D = x3.shape
    tr = 256

    def loss_fn(xv, wv, tv):
        err = _rms(xv, wv) - tv
        return 0.5 * jnp.sum(jnp.mean(err * err, axis=-1))

    def body(x_ref, w_ref, t_ref, loss_ref, dx_ref, dxb_ref, dw_ref):
        @pl.when(pl.program_id(0) == 0)
        def _():
            dw_ref[...] = jnp.zeros_like(dw_ref)
            loss_ref[...] = jnp.zeros_like(loss_ref)

        tv = t_ref[...]
        val, vjp = jax.vjp(lambda a, b: loss_fn(a, b, tv), x_ref[...], w_ref[...])
        dx, dw = vjp(jnp.ones((), F32))
        dx_ref[...] = dx
        dxb_ref[...] = dx.astype(BF)
        dw_ref[...] += dw
        loss_ref[...] += jnp.full(loss_ref.shape, val, F32)

    row = pl.BlockSpec((tr, D), lambda i: (i, 0))
    vec = pl.BlockSpec((1, D), lambda i: (0, 0))
    one = pl.BlockSpec((1, LANE), lambda i: (0, 0))
    return _pcall(body, grid=(R // tr,), in_specs=[row, vec, row], out_specs=[one, row, row, vec],
                          out_shape=[SDS((1, LANE), F32), SDS((R, D), F32), SDS((R, D), BF), SDS((1, D), F32)],
                          name="final_loss")(x3, w, tgt)


NEG = -1e30


def _att_blocks(S, d):
    L = S // d
    nb = L // LANE
    return nb, (2 if nb > 1 else 1)


def _att_rows(S, d, idx):
    nb, nkb = _att_blocks(S, d)
    r = idx // nb
    n = idx % nb
    kb = jnp.maximum(n - 1, 0)
    if d == 1:
        qrows = pl.ds(pl.multiple_of(n * LANE, LANE), LANE)
        krows = pl.ds(pl.multiple_of(kb * LANE, LANE), LANE * nkb)
    else:
        qrows = pl.ds(r + n * (LANE * d), LANE, stride=d)
        krows = pl.ds(r + kb * (LANE * d), LANE * nkb, stride=d)
    qpos = n * LANE + lax.broadcasted_iota(jnp.int32, (LANE, LANE * nkb), 0)
    kpos = kb * LANE + lax.broadcasted_iota(jnp.int32, (LANE, LANE * nkb), 1)
    return qrows, krows, qpos - kpos


def _slab(col):
    return pl.BlockSpec((SEQ, LANE), lambda h, col=col: (0, col + h))


def _attn_fwd(proj):
    S = proj.shape[0]
    scale = HEAD_DIM ** -0.5

    def body(q0, q1, q2, k0, k1, k2, v0, v1, v2, o_ref, lse_ref, og, lg):
        qs, ks, vs = (q0, q1, q2), (k0, k1, k2), (v0, v1, v2)
        for g, (window, d) in enumerate(ATT_GROUPS):
            def blk(idx, carry, g=g, window=window, d=d):
                qrows, krows, dist = _att_rows(S, d, idx)
                q = qs[g][qrows, :].astype(BF)
                k = ks[g][krows, :].astype(BF)
                v = vs[g][krows, :].astype(BF)
                s = lax.dot_general(q, k, (((1,), (1,)), ((), ())), preferred_element_type=F32) * scale
                s = jnp.where((dist >= 0) & (dist <= window // d), s, NEG)
                m = jnp.max(s, axis=-1, keepdims=True)
                p = jnp.exp(s - m)
                l = jnp.sum(p, axis=-1, keepdims=True)
                o = jnp.dot((p / l).astype(BF), v, preferred_element_type=F32)
                og[g, qrows, :] = o
                lg[g, qrows, :] = jnp.broadcast_to(m + jnp.log(l), (LANE, LANE))
                return carry
            lax.fori_loop(0, S // LANE, blk, 0, unroll=4)

        def merge(t, carry):
            rows = pl.ds(pl.multiple_of(t * 256, 256), 256)
            l0, l1, l2 = lg[0, rows, :], lg[1, rows, :], lg[2, rows, :]
            m = jnp.maximum(jnp.maximum(l0, l1), l2)
            w0, w1, w2 = jnp.exp(l0 - m), jnp.exp(l1 - m), jnp.exp(l2 - m)
            den = w0 + w1 + w2
            o_ref[rows, :] = (w0 * og[0, rows, :] + w1 * og[1, rows, :] + w2 * og[2, rows, :]) / den
            lse_ref[rows, :] = m + jnp.log(den)
            return carry
        lax.fori_loop(0, S // 256, merge, 0)

    ins = [_slab(QA0 + 4 * g) for g in range(3)] + [_slab(KA0 + 4 * g) for g in range(3)] + [_slab(VA0 + 4 * g) for g in range(3)]
    out = pl.BlockSpec((SEQ, LANE), lambda h: (0, h))
    return _pcall(
        body, grid=(ATT_HEADS,), in_specs=ins, out_specs=[out, out],
        out_shape=[SDS((S, ATT_OUT), F32), SDS((S, ATT_OUT), F32)],
        scratch_shapes=[pltpu.VMEM((3, S, LANE), F32), pltpu.VMEM((3, S, LANE), F32)],
        compiler_params=_cp(VMEM_BIG), name="attn_fwd")(*([proj] * 9))


def _attn_bwd(proj, o_att, lse, do_att):
    S = proj.shape[0]
    scale = HEAD_DIM ** -0.5

    def body(q0, q1, q2, k0, k1, k2, v0, v1, v2, o_ref, lse_ref, do_ref,
             dq0, dq1, dq2, dk0, dk1, dk2, dv0, dv1, dv2, dqa, dka, dva, delta):
        qs, ks, vs = (q0, q1, q2), (k0, k1, k2), (v0, v1, v2)
        dqs, dks, dvs = (dq0, dq1, dq2), (dk0, dk1, dk2), (dv0, dv1, dv2)
        delta[...] = jnp.broadcast_to(jnp.sum(do_ref[...] * o_ref[...], axis=-1, keepdims=True), delta.shape)
        for g, (window, d) in enumerate(ATT_GROUPS):
            dka[...] = jnp.zeros_like(dka)
            dva[...] = jnp.zeros_like(dva)

            def blk(idx, carry, g=g, window=window, d=d):
                qrows, krows, dist = _att_rows(S, d, idx)
                q = qs[g][qrows, :].astype(BF)
                k = ks[g][krows, :].astype(BF)
                v = vs[g][krows, :].astype(BF)
                do = do_ref[qrows, :]
                s = lax.dot_general(q, k, (((1,), (1,)), ((), ())), preferred_element_type=F32) * scale
                p = jnp.exp(s - lse_ref[qrows, :][:, 0:1])
                p = jnp.where((dist >= 0) & (dist <= window // d), p, 0.0)
                dob = do.astype(BF)
                dva[krows, :] += lax.dot_general(p.astype(BF), dob, (((0,), (0,)), ((), ())), preferred_element_type=F32)
                dp = lax.dot_general(dob, v, (((1,), (1,)), ((), ())), preferred_element_type=F32)
                ds = (p * (dp - delta[qrows, :][:, 0:1]) * scale).astype(BF)
                dqa[qrows, :] = jnp.dot(ds, k, preferred_element_type=F32)
                dka[krows, :] += lax.dot_general(ds, q, (((0,), (0,)), ((), ())), preferred_element_type=F32)
                return carry
            lax.fori_loop(0, S // LANE, blk, 0, unroll=4)
            dqs[g][...] = dqa[...].astype(dqs[g].dtype)
            dks[g][...] = dka[...].astype(dks[g].dtype)
            dvs[g][...] = dva[...].astype(dvs[g].dtype)

    cols = [QA0 + 4 * g for g in range(3)] + [KA0 + 4 * g for g in range(3)] + [VA0 + 4 * g for g in range(3)]
    ins = [_slab(c) for c in cols]
    one = pl.BlockSpec((SEQ, LANE), lambda h: (0, h))
    return _pcall(
        body, grid=(ATT_HEADS,), in_specs=ins + [one, one, one], out_specs=[one] * 9,
        out_shape=[SDS((S, ATT_OUT), BF)] * 9,
        scratch_shapes=[pltpu.VMEM((S, LANE), F32)] * 4,
        compiler_params=_cp(VMEM_BIG), name="attn_bwd")(*([proj] * 9), o_att, lse, do_att)


def _bdot(a, b, dims):
    return lax.dot_general(a.astype(BF), b.astype(BF), (dims, ((), ())), preferred_element_type=F32)


def _hg_chunk(qh, fh, ih, gh, lbraw, nw, st):
    C = HG_CHUNK
    mx = jnp.maximum(lbraw[0:1], lbraw[1:2])
    e0 = jnp.exp(lbraw[0:1] - mx)
    e1 = jnp.exp(lbraw[1:2] - mx)
    lb = e0 / (e0 + e1)
    f = lb + (1.0 - lb) * jax.nn.sigmoid(fh)
    logf = jnp.log(f)
    k = 1.0 - f
    q = qh * jax.nn.sigmoid(qh)
    causal = lax.broadcasted_iota(jnp.int32, (C, C), 0) >= lax.broadcasted_iota(jnp.int32, (C, C), 1)
    b = lax.dot_general(causal.astype(F32), logf, (((1,), (0,)), ((), ())), precision=lax.Precision.HIGHEST,
                        preferred_element_type=F32)
    b_last = b[C - 1:C]
    st_out = st * jnp.exp(b_last) + _bdot(ih, k * jnp.exp(b_last - b), ((0,), (0,)))
    inter = _bdot(q * jnp.exp(b), st, ((1,), (1,)))
    b_ref = b[C // 2:C // 2 + 1]
    a = _bdot(q * jnp.exp(b - b_ref), k * jnp.exp(b_ref - b), ((1,), (1,)))
    a = jnp.where(causal, a, 0.0)
    o = inter + _bdot(a, ih, ((1,), (0,)))
    o = o * lax.rsqrt(jnp.mean(o * o, axis=-1, keepdims=True) + RMS_EPS) * nw
    return o * (gh * jax.nn.sigmoid(gh)), st_out


def _to_heads(x):
    return jnp.concatenate([x[None, :, h * LANE:(h + 1) * LANE] for h in range(HG_HEADS)], axis=0)


def _from_heads(x):
    return jnp.concatenate([x[h] for h in range(HG_HEADS)], axis=1)


def _hdot(a, b, ca, cb):
    return lax.dot_general(a.astype(BF), b.astype(BF), (((ca,), (cb,)), ((0,), (0,))), preferred_element_type=F32)


def _hg_chunk_all(qh, fh, ih, gh, lbraw, nw, st):
    C = HG_CHUNK
    mx = jnp.maximum(lbraw[0:1], lbraw[1:2])
    e0 = jnp.exp(lbraw[0:1] - mx)
    e1 = jnp.exp(lbraw[1:2] - mx)
    lb = e0 / (e0 + e1)
    f = lb + (1.0 - lb) * jax.nn.sigmoid(fh)
    logf = jnp.log(f)
    k = 1.0 - f
    q = qh * jax.nn.sigmoid(qh)
    causal = lax.broadcasted_iota(jnp.int32, (C, C), 0) >= lax.broadcasted_iota(jnp.int32, (C, C), 1)
    b = lax.dot_general(causal.astype(F32), logf, (((1,), (0,)), ((), ())), precision=lax.Precision.HIGHEST,
                        preferred_element_type=F32)
    b_last = b[C - 1:C]
    b_ref = b[C // 2:C // 2 + 1]
    v3 = _to_heads(ih)
    st_out = st * _to_heads(jnp.exp(b_last)) + _hdot(v3, _to_heads(k * jnp.exp(b_last - b)), 1, 1)
    inter = _hdot(_to_heads(q * jnp.exp(b)), st, 2, 2)
    a = _hdot(_to_heads(q * jnp.exp(b - b_ref)), _to_heads(k * jnp.exp(b_ref - b)), 2, 2)
    a = jnp.where(causal[None], a, 0.0)
    o = inter + _hdot(a, v3, 2, 1)
    o = o * lax.rsqrt(jnp.mean(o * o, axis=-1, keepdims=True) + RMS_EPS) * nw
    return _from_heads(o) * (gh * jax.nn.sigmoid(gh)), st_out


def _head_cols(h):
    return slice(h * LANE, (h + 1) * LANE)


def _hg_in_specs(chunk_of):
    half = 4 * LANE
    specs = []
    for base in (QH0, FH0, IH0, GH0):
        for part in range(2):
            specs.append(pl.BlockSpec((HG_CHUNK, half), lambda n, col=base // 4 + part: (chunk_of(n), col)))
    return specs


def _hg_sections(sec):
    return [jnp.concatenate([sec[2 * s][...], sec[2 * s + 1][...]], axis=1) for s in range(4)]


def _hg_fwd(proj, lb, nw):
    S = proj.shape[0]
    C = HG_CHUNK
    NC = S // C

    def body(*refs):
        sec = refs[:8]
        lb_ref, nw_ref, o_ref, st_ref, st_scr = refs[8:]

        @pl.when(pl.program_id(0) == 0)
        def _():
            st_scr[...] = jnp.zeros_like(st_scr)

        st = st_scr[...]
        st_ref[:, 0] = st
        o, st_out = _hg_chunk_all(*_hg_sections(sec), lb_ref[...], nw_ref[...], st)
        o_ref[...] = o.astype(o_ref.dtype)
        st_scr[...] = st_out

    return _pcall(
        body, grid=(NC,),
        in_specs=_hg_in_specs(lambda n: n) + [pl.BlockSpec((2, HG_WIDTH), lambda n: (0, 0)), pl.BlockSpec((1, LANE), lambda n: (0, 0))],
        out_specs=[pl.BlockSpec((C, HG_WIDTH), lambda n: (n, 0)), pl.BlockSpec((HG_HEADS, 1, LANE, LANE), lambda n: (0, n, 0, 0))],
        out_shape=[SDS((S, HG_WIDTH), BF), SDS((HG_HEADS, NC, LANE, LANE), F32)],
        scratch_shapes=[pltpu.VMEM((HG_HEADS, LANE, LANE), F32)], name="hg_fwd")(*([proj] * 8), lb, nw)


def _hg_bwd(proj, lb, nw, states, do_hg):
    S = proj.shape[0]
    C = HG_CHUNK
    NC = S // C

    def body(*refs):
        sec = refs[:8]
        lb_ref, nw_ref, st_ref, do_ref, dq_ref, df_ref, di_ref, dg_ref, dlb_ref, dnw_ref, dst_scr = refs[8:]

        @pl.when(pl.program_id(0) == 0)
        def _():
            dst_scr[...] = jnp.zeros_like(dst_scr)
            dlb_ref[...] = jnp.zeros_like(dlb_ref)
            dnw_ref[...] = jnp.zeros_like(dnw_ref)

        _, vjp = jax.vjp(_hg_chunk_all, *_hg_sections(sec), lb_ref[...], nw_ref[...], st_ref[:, 0])
        dq, df, di, dg, dlb, dnw, dst = vjp((do_ref[...].astype(F32), dst_scr[...]))
        dq_ref[...] = dq.astype(dq_ref.dtype)
        df_ref[...] = df.astype(df_ref.dtype)
        di_ref[...] = di.astype(di_ref.dtype)
        dg_ref[...] = dg.astype(dg_ref.dtype)
        dlb_ref[...] += dlb
        dnw_ref[...] += dnw
        dst_scr[...] = dst

    rev = lambda n: NC - 1 - n
    ospec = pl.BlockSpec((C, HG_WIDTH), lambda n: (rev(n), 0))
    lbs = pl.BlockSpec((2, HG_WIDTH), lambda n: (0, 0))
    nws = pl.BlockSpec((1, LANE), lambda n: (0, 0))
    outs = _pcall(
        body, grid=(NC,),
        in_specs=_hg_in_specs(rev) + [lbs, nws, pl.BlockSpec((HG_HEADS, 1, LANE, LANE), lambda n: (0, rev(n), 0, 0)), ospec],
        out_specs=[ospec] * 4 + [lbs, nws],
        out_shape=[SDS((S, HG_WIDTH), BF)] * 4 + [SDS((2, HG_WIDTH), F32), SDS((1, LANE), F32)],
        scratch_shapes=[pltpu.VMEM((HG_HEADS, LANE, LANE), F32)], name="hg_bwd")(*([proj] * 8), lb, nw, states, do_hg)
    return outs


def _merge(ga, gb, ba, bb):
    return jax.nn.sigmoid(ga) * ba + jax.nn.sigmoid(gb) * bb


_MT = 512


def _gate_specs():
    ga = pl.BlockSpec((_MT, _MT), lambda i, j: (i, GA0 * LANE // _MT + j))
    gb = pl.BlockSpec((_MT, _MT), lambda i, j: (i, GB0 * LANE // _MT + j))
    t = pl.BlockSpec((_MT, _MT), lambda i, j: (i, j))
    return ga, gb, t


def _merge_fwd(proj, ba, bb):
    S, D = ba.shape
    ga, gb, t = _gate_specs()

    def body(ga_ref, gb_ref, ba_ref, bb_ref, o_ref):
        o_ref[...] = _merge(ga_ref[...], gb_ref[...], ba_ref[...], bb_ref[...]).astype(o_ref.dtype)

    return _pcall(body, grid=(S // _MT, D // _MT), in_specs=[ga, gb, t, t], out_specs=t,
                          out_shape=SDS((S, D), BF), name="merge_fwd")(proj, proj, ba, bb)


def _merge_bwd(proj, ba, bb, dm):
    S, D = ba.shape
    ga, gb, t = _gate_specs()

    def body(ga_ref, gb_ref, ba_ref, bb_ref, dm_ref, dga_ref, dgb_ref, dba_ref, dbb_ref):
        _, vjp = jax.vjp(_merge, ga_ref[...], gb_ref[...], ba_ref[...], bb_ref[...])
        dga, dgb, dba, dbb = vjp(dm_ref[...])
        dga_ref[...] = dga.astype(BF)
        dgb_ref[...] = dgb.astype(BF)
        dba_ref[...] = dba.astype(BF)
        dbb_ref[...] = dbb.astype(BF)

    return _pcall(body, grid=(S // _MT, D // _MT), in_specs=[ga, gb, t, t, t], out_specs=[t] * 4,
                          out_shape=[SDS((S, D), BF)] * 4, name="merge_bwd")(proj, proj, ba, bb, dm)


def _swiglu(a, b):
    return a * jax.nn.sigmoid(a) * b


def _swiglu_fwd(a, b):
    S, F = a.shape
    tf = _pick(F, (1408, 512))
    t = pl.BlockSpec((_MT, tf), lambda i, j: (i, j))

    def body(a_ref, b_ref, o_ref):
        o_ref[...] = _swiglu(a_ref[...], b_ref[...]).astype(o_ref.dtype)

    return _pcall(body, grid=(S // _MT, F // tf), in_specs=[t, t], out_specs=t,
                          out_shape=SDS((S, F), BF), name="swiglu_fwd")(a, b)


def _swiglu_bwd(a, b, dg):
    S, F = a.shape
    tf = _pick(F, (1408, 512))
    t = pl.BlockSpec((_MT, tf), lambda i, j: (i, j))

    def body(a_ref, b_ref, dg_ref, da_ref, db_ref):
        _, vjp = jax.vjp(_swiglu, a_ref[...], b_ref[...])
        da, db = vjp(dg_ref[...].astype(F32))
        da_ref[...] = da.astype(BF)
        db_ref[...] = db.astype(BF)

    return _pcall(body, grid=(S // _MT, F // tf), in_specs=[t, t, t], out_specs=[t, t],
                          out_shape=[SDS((S, F), BF)] * 2, name="swiglu_bwd")(a, b, dg)


def _cross(q, k, v):
    s = _bdot(q, k, ((1,), (1,))) * (HEAD_DIM ** -0.5)
    e = jnp.exp(s - jnp.max(s, axis=-1, keepdims=True))
    p = e / jnp.sum(e, axis=-1, keepdims=True)
    return _bdot(p, v, ((1,), (0,)))


def _cross_specs():
    q = pl.BlockSpec((_MT, LANE), lambda h, i: (i, h))
    k = pl.BlockSpec((MEM_LEN, LANE), lambda h, i: (0, h))
    v = pl.BlockSpec((MEM_LEN, LANE), lambda h, i: (0, CROSS_HEADS + h))
    return q, k, v


def _cross_fwd(qc, kvc):
    S = qc.shape[0]
    q, k, v = _cross_specs()

    def body(q_ref, k_ref, v_ref, o_ref):
        o_ref[...] = _cross(q_ref[...], k_ref[...], v_ref[...]).astype(o_ref.dtype)

    return _pcall(body, grid=(CROSS_HEADS, S // _MT), in_specs=[q, k, v], out_specs=q,
                          out_shape=SDS((S, CROSS_WIDTH), BF), name="cross_fwd")(qc, kvc, kvc)


def _cross_bwd(qc, kvc, doc):
    S = qc.shape[0]
    q, k, v = _cross_specs()

    def body(q_ref, k_ref, v_ref, do_ref, dq_ref, dk_ref, dv_ref):
        @pl.when(pl.program_id(1) == 0)
        def _():
            dk_ref[...] = jnp.zeros_like(dk_ref)
            dv_ref[...] = jnp.zeros_like(dv_ref)

        _, vjp = jax.vjp(_cross, q_ref[...], k_ref[...], v_ref[...])
        dq, dk, dv = vjp(do_ref[...])
        dq_ref[...] = dq.astype(dq_ref.dtype)
        dk_ref[...] += dk
        dv_ref[...] += dv

    return _pcall(body, grid=(CROSS_HEADS, S // _MT), in_specs=[q, k, v, q], out_specs=[q, k, k],
                          out_shape=[SDS((S, CROSS_WIDTH), BF), SDS((MEM_LEN, CROSS_WIDTH), F32), SDS((MEM_LEN, CROSS_WIDTH), F32)],
                          name="cross_bwd")(qc, kvc, kvc, doc)


def _local_step(x, mem, tgt, p, ready):
    h = _rms_fwd(x, p["ln_mix"], "rms_mix")
    proj = _mm(h, p["w_in_a"], "nt", F32, "mm_proj_a", a_half=0)
    proj = _mm(h, p["w_in_b"], "nt", F32, "mm_proj_b", a_half=1, res=proj)
    o_att, lse = _attn_fwd(proj)
    o_hg, states = _hg_fwd(proj, p["hg_lb"], p["hg_norm"])
    ba = _mm(o_att, p["w_ba"], "nn", F32, "mm_ba")
    bb = _mm(o_hg, p["w_bb"], "nn", F32, "mm_bb")
    merged = _merge_fwd(proj, ba, bb)
    x1 = _mm(merged, p["w_out"], "nn", F32, "mm_out", res=x)
    hc = _rms_fwd(x1, p["ln_cross"], "rms_cross")
    qc = _mm(hc, p["wq"], "nn", F32, "mm_q")
    mn = _rms_fwd(mem, p["ln_mem"], "rms_mem")
    kvc = _mm(mn, p["wkv"], "nn", F32, "mm_kv")
    oc = _cross_fwd(qc, kvc)
    x2 = _mm(oc, p["wo"], "nn", F32, "mm_o", res=x1)
    hf = _rms_fwd(x2, p["ln_ffn"], "rms_ffn")
    a = _mm(hf, p["w1"], "nt", F32, "mm_w1")
    b = _mm(hf, p["w3"], "nt", F32, "mm_w3")
    g = _swiglu_fwd(a, b)
    x3 = _mm(g, p["w2"], "nn", F32, "mm_w2", res=x2)
    loss, dx3, dx3b, d_ln_final = _final_loss(x3, p["ln_final"], tgt)

    gw, gs = {}, {"ln_final": d_ln_final}
    dg = _mm(dx3b, p["w2"], "nt", BF, "mm_dg")
    gw["w2"] = _mm(g, dx3b, "tn", BF, "mm_dw2")
    da, db = _swiglu_bwd(a, b, dg)
    dhf = _mm(da, p["w1"], "nn", F32, "mm_dhf1")
    dhf = _mm(db, p["w3"], "nn", F32, "mm_dhf3", res=dhf)
    gw["w1"] = _mm(da, hf, "tn", BF, "mm_dw1")
    gw["w3"] = _mm(db, hf, "tn", BF, "mm_dw3")
    ready("ffn", gw)
    dx2, dx2b, gs["ln_ffn"] = _rms_bwd(x2, p["ln_ffn"], dhf, dx3, "rms_ffn_bwd")
    doc = _mm(dx2b, p["wo"], "nt", F32, "mm_doc")
    gw["wo"] = _mm(oc, dx2b, "tn", BF, "mm_dwo")
    dqc, dk, dv = _cross_bwd(qc, kvc, doc)
    dkvc = jnp.concatenate([dk, dv], axis=1)
    dhc = _mm(dqc, p["wq"], "nt", F32, "mm_dhc")
    gw["wq"] = _mm(hc, dqc, "tn", BF, "mm_dwq")
    dmn = _mm(dkvc, p["wkv"], "nt", F32, "mm_dmn")
    gw["wkv"] = _mm(mn, dkvc, "tn", BF, "mm_dwkv")
    ready("cross", gw)
    gs["ln_mem"] = _rms_bwd(mem, p["ln_mem"], dmn, None, "rms_mem_bwd")
    dx1, dx1b, gs["ln_cross"] = _rms_bwd(x1, p["ln_cross"], dhc, dx2, "rms_cross_bwd")
    dmerged = _mm(dx1b, p["w_out"], "nt", F32, "mm_dmerged")
    gw["w_out"] = _mm(merged, dx1b, "tn", BF, "mm_dwout")
    dga, dgb, dba, dbb = _merge_bwd(proj, ba, bb, dmerged)
    do_att = _mm(dba, p["w_ba"], "nt", F32, "mm_doatt")
    gw["w_ba"] = _mm(o_att, dba, "tn", BF, "mm_dwba")
    do_hg = _mm(dbb, p["w_bb"], "nt", F32, "mm_dohg")
    gw["w_bb"] = _mm(o_hg, dbb, "tn", BF, "mm_dwbb")
    ready("mix", gw)
    datt = _attn_bwd(proj, o_att, lse, do_att)
    dqh, dfh, dih, dgh, gs["hg_lb"], gs["hg_norm"] = _hg_bwd(proj, p["hg_lb"], p["hg_norm"], states, do_hg)
    ready("mixers_done", gw)
    dproj = jnp.concatenate([*datt, dqh, dfh, dih, dgh, dga, dgb], axis=1)
    gw["w_in_a"] = _mm(dproj, h, "tn", BF, "mm_dwin_a", b_half=0)
    ready("in_a", gw)
    gw["w_in_b"] = _mm(dproj, h, "tn", BF, "mm_dwin_b", b_half=1)
    ready("in_b", gw)
    dh_a = _mm(dproj, p["w_in_a"], "nn", F32, "mm_dh_a")
    ready("dh_half", gw)
    dh = (dh_a, _mm(dproj, p["w_in_b"], "nn", F32, "mm_dh_b"))
    grad_x, _, gs["ln_mix"] = _rms_bwd(x, p["ln_mix"], dh, dx1, "rms_mix_bwd")
    return loss, grad_x, gs


_MATS = (
    ("w_in_a", "w_in", IN_WIDTH, D_MODEL // 2, "row", IN_WIDTH // N_DEV, True),
    ("w_in_b", "w_in", IN_WIDTH, D_MODEL // 2, "row", IN_WIDTH // N_DEV, True),
    ("w_ba", "w_branch_a", ATT_OUT, D_MODEL, "col", D_MODEL // N_DEV, False),
    ("w_bb", "w_branch_b", HG_WIDTH, D_MODEL, "col", D_MODEL // N_DEV, False),
    ("w_out", "w_out", D_MODEL, D_MODEL, "row", D_MODEL // N_DEV, False),
    ("wq", "wq_cross", D_MODEL, CROSS_WIDTH, "row", D_MODEL // N_DEV, False),
    ("wkv", "wkv_cross", D_MODEL, 2 * CROSS_WIDTH, "row", D_MODEL // N_DEV, False),
    ("wo", "wo_cross", CROSS_WIDTH, D_MODEL, "col", D_MODEL // N_DEV, False),
    ("w1", "w1", D_FF, D_MODEL, "row", D_FF // N_DEV, True),
    ("w3", "w3", D_FF, D_MODEL, "row", D_FF // N_DEV, True),
    ("w2", "w2", D_FF, D_MODEL, "row", D_FF // N_DEV, False),
)
_INDEX = {m[0]: i for i, m in enumerate(_MATS)}


def _unit_shape(i):
    _, _, K, N, kind, sh, _ = _MATS[i]
    return (sh, N) if kind == "row" else (K, sh)


def _win(ref, i, chip, par, half=None):
    _, _, K, _, kind, sh, _ = _MATS[i]
    if kind == "row":
        rows, start = (sh, 0) if half is None else (sh // 2, half * (sh // 2))
        return ref.at[pl.ds(pl.multiple_of((2 * chip + par) * sh + start, 16), rows), :]
    rows, start = (K, 0) if half is None else (K // 2, half * (K // 2))
    return ref.at[pl.ds(start, rows), pl.ds(pl.multiple_of((2 * chip + par) * sh, LANE), sh)]


def _my_place():
    x, y, c = lax.axis_index("x"), lax.axis_index("y"), lax.axis_index("c")
    chips = [(1 - x, y), (x, 1 - y), (1 - x, 1 - y)]
    return x, y, c, chips


def _cast_unit(w, name, half=None):
    K, N = w.shape
    if half is not None:
        N = N // 2
    tr = _pick(K, (256, 400, 352))

    def body(w_ref, o_ref):
        o_ref[...] = w_ref[...].astype(BF)

    src = pl.BlockSpec((tr, N), lambda i: (i, half or 0))
    dst = pl.BlockSpec((tr, N), lambda i: (i, 0))
    return _pcall(body, grid=(K // tr,), in_specs=[src], out_specs=dst, out_shape=SDS((K, N), BF), name=name)(w)


def _handshake(peers):
    barrier = pltpu.get_barrier_semaphore()
    for peer in peers:
        pl.semaphore_signal(barrier, inc=1, device_id=peer, device_id_type=MESH)
    pl.semaphore_wait(barrier, len(peers))


def _sequencer(body, out_type, sems, cid, name):
    return pl.kernel(body, out_type=out_type, mesh=plsc.ScalarSubcoreMesh(axis_name="seq", num_cores=1),
                     scratch_types=sems, compiler_params=pltpu.CompilerParams(collective_id=cid), name=name)


def _all_gather(units, idx, cid, name):
    n = len(idx)

    def body(*refs):
        us, fulls = refs[:n], refs[n:2 * n]
        ssem, rsem, lsem = refs[2 * n:]
        x, y, c, _ = _my_place()
        xn, yn, dg = (1 - x, y), (x, 1 - y), (1 - x, 1 - y)
        me, sib = (x, y, c), (x, y, 1 - c)
        _handshake([sib, (*xn, c), (*yn, c)])

        def win(i, chip_xy, par, half=None):
            return _win(fulls[i], idx[i], 2 * chip_xy[0] + chip_xy[1], par, half)

        def rcopy(i, k, src, dst, to):
            return pltpu.make_async_remote_copy(src_ref=src, dst_ref=dst, send_sem=ssem.at[i, k], recv_sem=rsem.at[i, k],
                                                device_id=to, device_id_type=MESH)

        def landed(i, k, blk):
            rcopy(i, k, blk, blk, me).wait_recv()

        sends, locs = [], []

        def send(i, k, blk, to, src=None):
            cp = rcopy(i, k, blk if src is None else src, blk, to)
            cp.start()
            sends.append(cp)

        for i in range(n):
            mine = win(i, (x, y), c)
            loc = pltpu.make_async_copy(us[i], mine, lsem.at[i])
            loc.start()
            locs.append(loc)
            send(i, 0, mine, sib, src=us[i])
            send(i, 1, mine, (*xn, c), src=us[i])
            send(i, 2, mine, (*yn, c), src=us[i])
        for i in range(n):
            landed(i, 1, win(i, xn, c))
            send(i, 3, win(i, xn, c, 0), (*yn, c))
            send(i, 5, win(i, xn, c), sib)
            landed(i, 2, win(i, yn, c))
            send(i, 4, win(i, yn, c, 1), (*xn, c))
            send(i, 6, win(i, yn, c), sib)
        for i in range(n):
            landed(i, 3, win(i, dg, c, 0))
            landed(i, 4, win(i, dg, c, 1))
            send(i, 7, win(i, dg, c), sib)
        for i in range(n):
            landed(i, 0, win(i, (x, y), 1 - c))
            landed(i, 5, win(i, xn, 1 - c))
            landed(i, 6, win(i, yn, 1 - c))
            landed(i, 7, win(i, dg, 1 - c))
        for cp in sends:
            cp.wait_send()
        for loc in locs:
            loc.wait()

    out_type = [SDS((_MATS[i][2], _MATS[i][3]), BF) for i in idx]
    sems = [pltpu.SemaphoreType.DMA((n, 8)), pltpu.SemaphoreType.DMA((n, 8)), pltpu.SemaphoreType.DMA((n,))]
    return _sequencer(body, out_type, sems, cid, name)(*units)


def _rs_sibling(grads, idx, cid, name):
    n = len(idx)

    def body(*refs):
        gs, bufs = refs[:n], refs[n:2 * n]
        ssem, rsem = refs[2 * n:]
        x, y, c, _ = _my_place()
        _handshake([(x, y, 1 - c)])
        cps = []
        for i in range(n):
            for k in range(4):
                cp = pltpu.make_async_remote_copy(
                    src_ref=_win(gs[i], idx[i], k, 1 - c), dst_ref=bufs[i].at[k], send_sem=ssem.at[i, k], recv_sem=rsem.at[i, k],
                    device_id=(x, y, 1 - c), device_id_type=MESH)
                cp.start()
                cps.append(cp)
        for cp in cps:
            cp.wait()

    out_type = [SDS((4, *_unit_shape(i)), BF) for i in idx]
    sems = [pltpu.SemaphoreType.DMA((n, 4)), pltpu.SemaphoreType.DMA((n, 4))]
    return _sequencer(body, out_type, sems, cid, name)(*grads)


def _chip_sum(g, sib, i, place, name):
    _, _, K, N, kind, sh, _ = _MATS[i]

    def body(pref, g_ref, s_ref, o_ref):
        o_ref[...] = (g_ref[...].astype(F32) + s_ref[...].astype(F32)).astype(BF)

    if kind == "row":
        tr = _pick(sh, (800, 352, 256))
        per = sh // tr
        grid = (4, per)
        g_spec = pl.BlockSpec((tr, N), lambda k, r, pref: ((2 * k + pref[0]) * per + r, 0))
        u_spec = pl.BlockSpec((None, tr, N), lambda k, r, pref: (k, r, 0))
    else:
        grid = (4,)
        g_spec = pl.BlockSpec((K, sh), lambda k, pref: (0, 2 * k + pref[0]))
        u_spec = pl.BlockSpec((None, K, sh), lambda k, pref: (k, 0, 0))
    return _pcall(body, grid=grid, in_specs=[g_spec, u_spec], out_specs=u_spec, out_shape=SDS((4, *_unit_shape(i)), BF),
                  prefetch=True, name=name)(place, g, sib)


def _rs_chips(parts, idx, cid, name):
    n = len(idx)

    def body(*refs):
        ts, bufs, relays = refs[:n], refs[n:2 * n], refs[2 * n:3 * n]
        ssem, rsem = refs[3 * n:]
        x, y, c, _ = _my_place()
        xn, yn, dg = (1 - x, y), (x, 1 - y), (1 - x, 1 - y)
        to_x, to_y, me = (*xn, c), (*yn, c), (x, y, c)
        _handshake([to_x, to_y])

        def rcopy(i, k, src, dst, to):
            return pltpu.make_async_remote_copy(src_ref=src, dst_ref=dst, send_sem=ssem.at[i, k], recv_sem=rsem.at[i, k],
                                                device_id=to, device_id_type=MESH)

        def halves(i):
            rows = _unit_shape(idx[i])[0] // 2
            return [pl.ds(0, rows), pl.ds(rows, rows)]

        sends = []

        def send(cp):
            cp.start()
            sends.append(cp)

        for i in range(n):
            h0, h1 = halves(i)
            diag = ts[i].at[2 * dg[0] + dg[1]]
            send(rcopy(i, 0, ts[i].at[2 * xn[0] + xn[1]], bufs[i].at[0], to_x))
            send(rcopy(i, 1, ts[i].at[2 * yn[0] + yn[1]], bufs[i].at[1], to_y))
            send(rcopy(i, 2, diag.at[h0], relays[i].at[0], to_x))
            send(rcopy(i, 3, diag.at[h1], relays[i].at[1], to_y))
        for i in range(n):
            h0, h1 = halves(i)
            rcopy(i, 2, relays[i].at[0], relays[i].at[0], me).wait_recv()
            send(rcopy(i, 4, relays[i].at[0], bufs[i].at[2, h0], to_y))
            rcopy(i, 3, relays[i].at[1], relays[i].at[1], me).wait_recv()
            send(rcopy(i, 5, relays[i].at[1], bufs[i].at[2, h1], to_x))
        for i in range(n):
            h0, h1 = halves(i)
            rcopy(i, 0, bufs[i].at[0], bufs[i].at[0], me).wait_recv()
            rcopy(i, 1, bufs[i].at[1], bufs[i].at[1], me).wait_recv()
            rcopy(i, 4, bufs[i].at[2, h0], bufs[i].at[2, h0], me).wait_recv()
            rcopy(i, 5, bufs[i].at[2, h1], bufs[i].at[2, h1], me).wait_recv()
        for cp in sends:
            cp.wait_send()

    def relay_shape(i):
        rows, cols = _unit_shape(i)
        return (2, rows // 2, cols)

    out_type = [SDS((3, *_unit_shape(i)), BF) for i in idx] + [SDS(relay_shape(i), BF) for i in idx]
    sems = [pltpu.SemaphoreType.DMA((n, 6)), pltpu.SemaphoreType.DMA((n, 6))]
    return _sequencer(body, out_type, sems, cid, name)(*parts)[:n]


def _adamw(w, g, m, v):
    m = ADAM_B1 * m + (1.0 - ADAM_B1) * g
    v = ADAM_B2 * v + (1.0 - ADAM_B2) * jnp.square(g)
    m_hat = m / (1.0 - ADAM_B1 ** ADAM_STEP)
    v_hat = v / (1.0 - ADAM_B2 ** ADAM_STEP)
    delta = -ADAM_LR * (m_hat / (jnp.sqrt(v_hat) + ADAM_EPS) + ADAM_WD * w)
    return delta, m, v


def _adam_mat(parts, recvs, w, m, v, place, name):
    R, C = w.shape
    n = len(parts)
    tr = _pick(R, (256, 160, 176))

    def body(pref, *refs):
        t_refs, r_refs = refs[:n], refs[n:2 * n]
        w_ref, m_ref, v_ref, g_out, d_out, m_out, v_out = refs[2 * n:]
        cols = [t[...].astype(F32) + r[0].astype(F32) + r[1].astype(F32) + r[2].astype(F32) for t, r in zip(t_refs, r_refs)]
        g = jnp.concatenate(cols, axis=1) if n > 1 else cols[0]
        d, mn, vn = _adamw(w_ref[...], g, m_ref[...], v_ref[...])
        g_out[...] = g
        d_out[...] = d
        m_out[...] = mn
        v_out[...] = vn

    t_spec = pl.BlockSpec((None, tr, C // n), lambda i, pref: (pref[1], i, 0))
    r_spec = pl.BlockSpec((3, tr, C // n), lambda i, pref: (0, i, 0))
    s_spec = pl.BlockSpec((tr, C), lambda i, pref: (i, 0))
    return _pcall(body, grid=(R // tr,), in_specs=[t_spec] * n + [r_spec] * n + [s_spec] * 3, out_specs=[s_spec] * 4,
                  out_shape=[SDS((R, C), F32)] * 4, prefetch=True, name=name)(place, *parts, *recvs, w, m, v)


_SMALL = (("ln_mix", "ln_mix_w", D_MODEL), ("ln_cross", "ln_cross_w", D_MODEL), ("ln_mem", "ln_mem_w", D_MODEL),
          ("ln_ffn", "ln_ffn_w", D_MODEL), ("ln_final", "ln_final_w", D_MODEL), ("hg_lb", "hg_lower_bounds", 2 * HG_WIDTH),
          ("hg_norm", "hg_norm_w", HEAD_DIM))
_PACK = sum(n for _, _, n in _SMALL) + LANE


def _small_sync(grow, wrow, mrow, vrow):
    def body(g_ref, w_ref, m_ref, v_ref, o_ref, gath, ssem, rsem):
        x, y, c, _ = _my_place()
        me = 4 * x + 2 * y + c
        gath[pl.ds(me, 1), :] = g_ref[...]
        cps = []
        for dlt in range(1, N_DEV):
            peer = (me + dlt) % N_DEV
            cp = pltpu.make_async_remote_copy(
                src_ref=g_ref, dst_ref=gath.at[pl.ds(me, 1), :], send_sem=ssem.at[peer], recv_sem=rsem.at[me],
                device_id=(peer // 4, (peer // 2) % 2, peer % 2), device_id_type=MESH)
            cp.start()
            cps.append(cp)
        for dlt in range(1, N_DEV):
            src = (me + dlt) % N_DEV
            pltpu.make_async_remote_copy(
                src_ref=g_ref, dst_ref=gath.at[pl.ds(src, 1), :], send_sem=ssem.at[src], recv_sem=rsem.at[src],
                device_id=(x, y, c), device_id_type=MESH).wait_recv()
        for cp in cps:
            cp.wait_send()
        g = gath[0:1, :]
        for j in range(1, N_DEV):
            g = g + gath[j:j + 1, :]
        d, mn, vn = _adamw(w_ref[...], g, m_ref[...], v_ref[...])
        o_ref[0:1, :] = g
        o_ref[1:2, :] = d
        o_ref[2:3, :] = mn
        o_ref[3:4, :] = vn

    vm = pl.BlockSpec(memory_space=pltpu.VMEM)
    return _pcall(
        body, in_specs=[vm] * 4, out_specs=vm, out_shape=SDS((4, _PACK), F32),
        scratch_shapes=[pltpu.VMEM((N_DEV, _PACK), F32), pltpu.SemaphoreType.DMA((N_DEV,)), pltpu.SemaphoreType.DMA((N_DEV,))],
        name="small_sync")(grow, wrow, mrow, vrow)


def _pack_small(d, loss_row=None):
    parts = [d[k].reshape(1, n).astype(F32) for k, _, n in _SMALL]
    parts.append(loss_row if loss_row is not None else jnp.ones((1, LANE), F32))
    return jnp.concatenate(parts, axis=1)


def kernel(x, mem, ln_mix_w, w_in, hg_norm_w, hg_lower_bounds, w_branch_a, w_branch_b, w_out, ln_cross_w, ln_mem_w, wq_cross, wkv_cross, wo_cross, ln_ffn_w, w1, w3, w2, ln_final_w, loss_target, m_ln_mix_w, m_w_in, m_hg_norm_w, m_hg_lower_bounds, m_w_branch_a, m_w_branch_b, m_w_out, m_ln_cross_w, m_ln_mem_w, m_wq_cross, m_wkv_cross, m_wo_cross, m_ln_ffn_w, m_w1, m_w3, m_w2, m_ln_final_w, v_ln_mix_w, v_w_in, v_hg_norm_w, v_hg_lower_bounds, v_w_branch_a, v_w_branch_b, v_w_out, v_ln_cross_w, v_ln_mem_w, v_wq_cross, v_wkv_cross, v_wo_cross, v_ln_ffn_w, v_w1, v_w3, v_w2, v_ln_final_w):
    given = dict(locals())
    place = jnp.stack([lax.axis_index("c"), 2 * lax.axis_index("x") + lax.axis_index("y")]).astype(jnp.int32)

    _ORDER[0] = None
    index = _INDEX

    def local_view(a, i):
        return a[0].T if _MATS[i][6] else a[0]

    shard = {key: local_view(given[ref], i) for i, (key, ref, *_) in enumerate(_MATS)}

    p = {}
    col_half = {"w_in_a": 0, "w_in_b": 1}
    waves = (("in_a", ["w_in_a"], 1), ("in_b", ["w_in_b"], 12), ("mid", ["w_ba", "w_bb", "w_out", "wq", "wkv", "wo"], 2),
             ("w1", ["w1"], 11), ("w3", ["w3"], 15), ("w2", ["w2"], 16))
    for wave, keys, cid in waves:
        units = [_cast_unit(shard[k], "cast_" + k, col_half.get(k)) for k in keys]
        p.update(zip(keys, _all_gather(units, [index[k] for k in keys], cid, "all_gather_" + wave)))
    small_w = {"ln_mix": ln_mix_w, "ln_cross": ln_cross_w, "ln_mem": ln_mem_w, "ln_ffn": ln_ffn_w,
               "ln_final": ln_final_w.reshape(1, D_MODEL), "hg_lb": hg_lower_bounds, "hg_norm": hg_norm_w}
    p.update(small_w)

    groups = {"ffn": ["w2", "w1", "w3"], "cross": ["wo", "wq", "wkv"], "mix": ["w_out", "w_ba", "w_bb"],
              "in_a": ["w_in_a"], "in_b": ["w_in_b"]}
    cids = {"ffn": (3, 4), "cross": (5, 6), "mix": (7, 8), "in_a": (9, 10), "in_b": (13, 14)}
    sib, parts, recv, out = {}, {}, {}, {}

    grads = {}

    def chip_stage(name):
        keys = groups[name]
        parts[name] = [_chip_sum(grads[k], s, index[k], place, "chip_sum_" + k) for k, s in zip(keys, sib[name])]
        recv[name] = _rs_chips(parts[name], [index[k] for k in keys], cids[name][1], "rs_chips_" + name)

    def adam(ref, i, part_list, recv_list):
        res = _adam_mat(part_list, recv_list, shard[_MATS[i][0]], local_view(given["m_" + ref], i), local_view(given["v_" + ref], i),
                        place, "adam_" + ref)
        out[ref] = [(a.T if _MATS[i][6] else a)[None] for a in res]

    def adam_stage(name):
        for k, t, r in zip(groups[name], parts[name], recv[name]):
            adam(_MATS[index[k]][1], index[k], [t], [r])

    def ready(name, gw):
        if name == "mixers_done":
            adam_stage("ffn")
            chip_stage("mix")
            adam_stage("cross")
            return
        if name == "dh_half":
            chip_stage("in_b")
            return
        keys = groups[name]
        grads.update({k: gw[k] for k in keys})
        sib[name] = _rs_sibling([gw[k] for k in keys], [index[k] for k in keys], cids[name][0], "rs_sibling_" + name)
        if name == "cross":
            chip_stage("ffn")
        if name == "mix":
            chip_stage("cross")
        if name == "in_b":
            adam_stage("mix")
            chip_stage("in_a")

    loss_row, grad_x, gs = _local_step(x[0], mem[0], loss_target[0], p, ready)

    sm = {k: given["m_" + ref] for k, ref, _ in _SMALL}
    sv = {k: given["v_" + ref] for k, ref, _ in _SMALL}
    synced = _small_sync(_pack_small(gs, loss_row), _pack_small(small_w), _pack_small(sm), _pack_small(sv))
    adam("w_in", index["w_in_a"], parts["in_a"] + parts["in_b"], recv["in_a"] + recv["in_b"])
    off = 0
    for k, ref, n in _SMALL:
        shape = given[ref].shape
        out[ref] = [synced[r, off:off + n].reshape(shape) for r in range(4)]
        off += n
    loss = synced[0, off]

    order = ["ln_mix_w", "w_in", "hg_norm_w", "hg_lower_bounds", "w_branch_a", "w_branch_b", "w_out", "ln_cross_w", "ln_mem_w",
             "wq_cross", "wkv_cross", "wo_cross", "ln_ffn_w", "w1", "w3", "w2", "ln_final_w"]
    return (loss, grad_x[None], *[out[n][0] for n in order], *[out[n][1] for n in order],
            *[out[n][2] for n in order], *[out[n][3] for n in order])
```

```python
import functools
import math

import jax
import jax.numpy as jnp
from jax import lax
from jax.experimental import pallas as pl
from jax.experimental.pallas import tpu as pltpu
from jax.experimental.pallas import tpu_sc as plsc

F32 = jnp.float32
BF = jnp.bfloat16
SDS = jax.ShapeDtypeStruct

D_MODEL = 2048
SEQ = 2048
HEAD_DIM = 128
MEM_LEN = 256
ATT_GROUPS = ((128, 1), (512, 4), (2048, 16))
ATT_HEADS = 4
ATT_WIDTH = 1536
ATT_OUT = 512
HG_HEADS = 8
HG_WIDTH = 1024
HG_CHUNK = 64
IN_WIDTH = 12800
CROSS_HEADS = 4
CROSS_WIDTH = 512
D_FF = 5632
RMS_EPS = 1e-6
ADAM_LR = 0.001
ADAM_B1 = 0.9
ADAM_B2 = 0.999
ADAM_EPS = 1e-08
ADAM_WD = 0.01
ADAM_STEP = 10
N_DEV = 8

LANE = 128
QA0, KA0, VA0 = 0, 12, 24
QH0, FH0, IH0, GH0 = 36, 44, 52, 60
GA0, GB0 = 68, 84

VMEM_BIG = 56 * 1024 * 1024
MESH = pl.DeviceIdType.MESH


def _cp(vmem=None, **kw):
    if vmem is not None:
        kw["vmem_limit_bytes"] = vmem
    return pltpu.CompilerParams(**kw)


def _pick(n, cands):
    for c in cands:
        if n % c == 0:
            return c
    raise ValueError(f"no tile for {n}")


_ANY = pl.BlockSpec(memory_space=pl.ANY)
_ORDER = [None]


def _pcall(body, *, name, in_specs, out_specs, out_shape, grid=(), scratch_shapes=(), compiler_params=None, prefetch=False):
    def run(*args):
        dep = _ORDER[0]
        specs, operands, kernel_fn = list(in_specs), list(args), body
        if dep is not None:
            at = len(operands)
            specs.append(_ANY)
            operands.append(dep)

            def kernel_fn(*refs):
                return body(*refs[:at], *refs[at + 1:])
        if prefetch:
            call = pl.pallas_call(
                kernel_fn, out_shape=out_shape, name=name, compiler_params=compiler_params,
                grid_spec=pltpu.PrefetchScalarGridSpec(num_scalar_prefetch=1, grid=grid, in_specs=specs, out_specs=out_specs,
                                                       scratch_shapes=scratch_shapes))
        else:
            call = pl.pallas_call(kernel_fn, grid=grid, in_specs=specs, out_specs=out_specs, out_shape=out_shape,
                                  scratch_shapes=scratch_shapes, compiler_params=compiler_params, name=name)
        out = call(*operands)
        _ORDER[0] = out[0] if isinstance(out, (list, tuple)) else out
        return out
    return run


def _mm(a, b, form, out_dtype, name, res=None, tiles=None, a_part=(0, 1), b_part=(0, 1)):
    a_shape = (a.shape[0], a.shape[1] // a_part[1])
    b_shape = (b.shape[0], b.shape[1] // b_part[1])
    if form == "nn":
        (M, K), (K2, N) = a_shape, b_shape
    elif form == "nt":
        (M, K), (N, K2) = a_shape, b_shape
    else:
        (K, M), (K2, N) = a_shape, b_shape
    assert K == K2, (a.shape, b.shape, form)
    if tiles is None:
        tm = _pick(M, (1024, 1280, 1408, 512, 256))
        tn = _pick(N, (1280, 1024, 1408, 512, 256))
        tk = K if K <= 2048 else _pick(K, (2560, 1408))
    else:
        tm, tn, tk = tiles
    nk = K // tk
    ah, bh = a_part[0], b_part[0]
    if form == "nn":
        a_spec = pl.BlockSpec((tm, tk), lambda i, j, k: (i, k + ah * nk))
        b_spec = pl.BlockSpec((tk, tn), lambda i, j, k: (k, j + bh * (N // tn)))
        dims = ((1,), (0,))
    elif form == "nt":
        a_spec = pl.BlockSpec((tm, tk), lambda i, j, k: (i, k + ah * nk))
        b_spec = pl.BlockSpec((tn, tk), lambda i, j, k: (j, k + bh * nk))
        dims = ((1,), (1,))
    else:
        a_spec = pl.BlockSpec((tk, tm), lambda i, j, k: (k, i + ah * (M // tm)))
        b_spec = pl.BlockSpec((tk, tn), lambda i, j, k: (k, j + bh * (N // tn)))
        dims = ((0,), (0,))
    o_spec = pl.BlockSpec((tm, tn), lambda i, j, k: (i, j))
    in_specs = [a_spec, b_spec]
    args = [a, b]
    if res is not None:
        in_specs.append(o_spec)
        args.append(res)

    def body(*refs):
        a_ref, b_ref = refs[0], refs[1]
        r_ref = refs[2] if res is not None else None
        o_ref = refs[3] if res is not None else refs[2]
        acc = refs[-1] if nk > 1 else None
        k = pl.program_id(2)
        d = lax.dot_general(a_ref[...].astype(BF), b_ref[...].astype(BF), (dims, ((), ())), preferred_element_type=F32)

        def finish(r):
            if res is not None:
                r = r + r_ref[...].astype(F32)
            o_ref[...] = r.astype(o_ref.dtype)

        if nk == 1:
            finish(d)
        else:
            @pl.when(k == 0)
            def _():
                acc[...] = d

            @pl.when((k > 0) & (k < nk - 1))
            def _():
                acc[...] += d

            @pl.when(k == nk - 1)
            def _():
                finish(acc[...] + d)

    return _pcall(
        body, grid=(M // tm, N // tn, nk), in_specs=in_specs, out_specs=o_spec,
        out_shape=SDS((M, N), out_dtype), scratch_shapes=[pltpu.VMEM((tm, tn), F32)] if nk > 1 else [],
        compiler_params=_cp(VMEM_BIG, dimension_semantics=("parallel", "parallel", "arbitrary")),
        name=name)(*args)


def _rms(x, w):
    return x * lax.rsqrt(jnp.mean(x * x, axis=-1, keepdims=True) + RMS_EPS) * w


def _rms_fwd(x, w, name):
    R, D = x.shape
    tr = 256

    def body(x_ref, w_ref, o_ref):
        o_ref[...] = _rms(x_ref[...], w_ref[...]).astype(o_ref.dtype)

    return _pcall(
        body, grid=(R // tr,),
        in_specs=[pl.BlockSpec((tr, D), lambda i: (i, 0)), pl.BlockSpec((1, D), lambda i: (0, 0))],
        out_specs=pl.BlockSpec((tr, D), lambda i: (i, 0)), out_shape=SDS((R, D), BF), name=name)(x, w)


def _rms_bwd(x, w, dh, dres, name):
    R, D = x.shape
    tr = 256
    want_dx = dres is not None
    dhs = dh if isinstance(dh, (tuple, list)) else (dh,)
    nd = len(dhs)

    def body(*refs):
        x_ref, w_ref, dh_refs = refs[0], refs[1], refs[2:2 + nd]
        if want_dx:
            dr_ref, dx_ref, dxb_ref, dw_ref = refs[2 + nd:]
        else:
            dw_ref = refs[2 + nd]

        @pl.when(pl.program_id(0) == 0)
        def _():
            dw_ref[...] = jnp.zeros_like(dw_ref)

        dhv = jnp.concatenate([r[...].astype(F32) for r in dh_refs], axis=1) if nd > 1 else dh_refs[0][...].astype(F32)
        _, vjp = jax.vjp(_rms, x_ref[...], w_ref[...])
        dx, dw = vjp(dhv)
        dw_ref[...] += dw
        if want_dx:
            dx = dx + dr_ref[...]
            dx_ref[...] = dx
            dxb_ref[...] = dx.astype(BF)

    row = pl.BlockSpec((tr, D), lambda i: (i, 0))
    part = pl.BlockSpec((tr, D // nd), lambda i: (i, 0))
    vec = pl.BlockSpec((1, D), lambda i: (0, 0))
    if want_dx:
        return _pcall(body, grid=(R // tr,), in_specs=[row, vec] + [part] * nd + [row], out_specs=[row, row, vec],
                      out_shape=[SDS((R, D), F32), SDS((R, D), BF), SDS((1, D), F32)], name=name)(x, w, *dhs, dres)
    return _pcall(body, grid=(R // tr,), in_specs=[row, vec] + [part] * nd, out_specs=vec,
                  out_shape=SDS((1, D), F32), name=name)(x, w, *dhs)


def _final_loss(x3, w, tgt):
    R, D = x3.shape
    tr = 256

    def loss_fn(xv, wv, tv):
        err = _rms(xv, wv) - tv
        return 0.5 * jnp.sum(jnp.mean(err * err, axis=-1))

    def body(x_ref, w_ref, t_ref, loss_ref, dx_ref, dxb_ref, dw_ref):
        @pl.when(pl.program_id(0) == 0)
        def _():
            dw_ref[...] = jnp.zeros_like(dw_ref)
            loss_ref[...] = jnp.zeros_like(loss_ref)

        tv = t_ref[...]
        val, vjp = jax.vjp(lambda a, b: loss_fn(a, b, tv), x_ref[...], w_ref[...])
        dx, dw = vjp(jnp.ones((), F32))
        dx_ref[...] = dx
        dxb_ref[...] = dx.astype(BF)
        dw_ref[...] += dw
        loss_ref[...] += jnp.full(loss_ref.shape, val, F32)

    row = pl.BlockSpec((tr, D), lambda i: (i, 0))
    vec = pl.BlockSpec((1, D), lambda i: (0, 0))
    one = pl.BlockSpec((1, LANE), lambda i: (0, 0))
    return _pcall(body, grid=(R // tr,), in_specs=[row, vec, row], out_specs=[one, row, row, vec],
                          out_shape=[SDS((1, LANE), F32), SDS((R, D), F32), SDS((R, D), BF), SDS((1, D), F32)],
                          name="final_loss")(x3, w, tgt)


NEG = -1e30


def _att_blocks(S, d):
    L = S // d
    nb = L // LANE
    return nb, (2 if nb > 1 else 1)


def _att_rows(S, d, idx):
    nb, nkb = _att_blocks(S, d)
    r = idx // nb
    n = idx % nb
    kb = jnp.maximum(n - 1, 0)
    if d == 1:
        qrows = pl.ds(pl.multiple_of(n * LANE, LANE), LANE)
        krows = pl.ds(pl.multiple_of(kb * LANE, LANE), LANE * nkb)
    else:
        qrows = pl.ds(r + n * (LANE * d), LANE, stride=d)
        krows = pl.ds(r + kb * (LANE * d), LANE * nkb, stride=d)
    qpos = n * LANE + lax.broadcasted_iota(jnp.int32, (LANE, LANE * nkb), 0)
    kpos = kb * LANE + lax.broadcasted_iota(jnp.int32, (LANE, LANE * nkb), 1)
    return qrows, krows, qpos - kpos


def _slab(col):
    return pl.BlockSpec((SEQ, LANE), lambda h, col=col: (0, col + h))


def _attn_fwd(proj):
    S = proj.shape[0]
    scale = HEAD_DIM ** -0.5

    def body(q0, q1, q2, k0, k1, k2, v0, v1, v2, o_ref, lse_ref, og, lg):
        qs, ks, vs = (q0, q1, q2), (k0, k1, k2), (v0, v1, v2)
        for g, (window, d) in enumerate(ATT_GROUPS):
            def blk(idx, carry, g=g, window=window, d=d):
                qrows, krows, dist = _att_rows(S, d, idx)
                q = qs[g][qrows, :].astype(BF)
                k = ks[g][krows, :].astype(BF)
                v = vs[g][krows, :].astype(BF)
                s = lax.dot_general(q, k, (((1,), (1,)), ((), ())), preferred_element_type=F32) * scale
                s = jnp.where((dist >= 0) & (dist <= window // d), s, NEG)
                m = jnp.max(s, axis=-1, keepdims=True)
                p = jnp.exp(s - m)
                l = jnp.sum(p, axis=-1, keepdims=True)
                o = jnp.dot((p / l).astype(BF), v, preferred_element_type=F32)
                og[g, qrows, :] = o
                lg[g, qrows, :] = jnp.broadcast_to(m + jnp.log(l), (LANE, LANE))
                return carry
            lax.fori_loop(0, S // LANE, blk, 0, unroll=4)

        def merge(t, carry):
            rows = pl.ds(pl.multiple_of(t * 256, 256), 256)
            l0, l1, l2 = lg[0, rows, :], lg[1, rows, :], lg[2, rows, :]
            m = jnp.maximum(jnp.maximum(l0, l1), l2)
            w0, w1, w2 = jnp.exp(l0 - m), jnp.exp(l1 - m), jnp.exp(l2 - m)
            den = w0 + w1 + w2
            o_ref[rows, :] = (w0 * og[0, rows, :] + w1 * og[1, rows, :] + w2 * og[2, rows, :]) / den
            lse_ref[rows, :] = m + jnp.log(den)
            return carry
        lax.fori_loop(0, S // 256, merge, 0)

    ins = [_slab(QA0 + 4 * g) for g in range(3)] + [_slab(KA0 + 4 * g) for g in range(3)] + [_slab(VA0 + 4 * g) for g in range(3)]
    out = pl.BlockSpec((SEQ, LANE), lambda h: (0, h))
    return _pcall(
        body, grid=(ATT_HEADS,), in_specs=ins, out_specs=[out, out],
        out_shape=[SDS((S, ATT_OUT), F32), SDS((S, ATT_OUT), F32)],
        scratch_shapes=[pltpu.VMEM((3, S, LANE), F32), pltpu.VMEM((3, S, LANE), F32)],
        compiler_params=_cp(VMEM_BIG), name="attn_fwd")(*([proj] * 9))


def _attn_bwd(proj, o_att, lse, do_att):
    S = proj.shape[0]
    scale = HEAD_DIM ** -0.5

    def body(q0, q1, q2, k0, k1, k2, v0, v1, v2, o_ref, lse_ref, do_ref,
             dq0, dq1, dq2, dk0, dk1, dk2, dv0, dv1, dv2, dqa, dka, dva, delta):
        qs, ks, vs = (q0, q1, q2), (k0, k1, k2), (v0, v1, v2)
        dqs, dks, dvs = (dq0, dq1, dq2), (dk0, dk1, dk2), (dv0, dv1, dv2)
        delta[...] = jnp.broadcast_to(jnp.sum(do_ref[...] * o_ref[...], axis=-1, keepdims=True), delta.shape)
        for g, (window, d) in enumerate(ATT_GROUPS):
            dka[...] = jnp.zeros_like(dka)
            dva[...] = jnp.zeros_like(dva)

            def blk(idx, carry, g=g, window=window, d=d):
                qrows, krows, dist = _att_rows(S, d, idx)
                q = qs[g][qrows, :].astype(BF)
                k = ks[g][krows, :].astype(BF)
                v = vs[g][krows, :].astype(BF)
                do = do_ref[qrows, :]
                s = lax.dot_general(q, k, (((1,), (1,)), ((), ())), preferred_element_type=F32) * scale
                p = jnp.exp(s - lse_ref[qrows, :][:, 0:1])
                p = jnp.where((dist >= 0) & (dist <= window // d), p, 0.0)
                dob = do.astype(BF)
                dva[krows, :] += lax.dot_general(p.astype(BF), dob, (((0,), (0,)), ((), ())), preferred_element_type=F32)
                dp = lax.dot_general(dob, v, (((1,), (1,)), ((), ())), preferred_element_type=F32)
                ds = (p * (dp - delta[qrows, :][:, 0:1]) * scale).astype(BF)
                dqa[qrows, :] = jnp.dot(ds, k, preferred_element_type=F32)
                dka[krows, :] += lax.dot_general(ds, q, (((0,), (0,)), ((), ())), preferred_element_type=F32)
                return carry
            lax.fori_loop(0, S // LANE, blk, 0, unroll=4)
            dqs[g][...] = dqa[...].astype(dqs[g].dtype)
            dks[g][...] = dka[...].astype(dks[g].dtype)
            dvs[g][...] = dva[...].astype(dvs[g].dtype)

    cols = [QA0 + 4 * g for g in range(3)] + [KA0 + 4 * g for g in range(3)] + [VA0 + 4 * g for g in range(3)]
    ins = [_slab(c) for c in cols]
    one = pl.BlockSpec((SEQ, LANE), lambda h: (0, h))
    return _pcall(
        body, grid=(ATT_HEADS,), in_specs=ins + [one, one, one], out_specs=[one] * 9,
        out_shape=[SDS((S, ATT_OUT), BF)] * 9,
        scratch_shapes=[pltpu.VMEM((S, LANE), F32)] * 4,
        compiler_params=_cp(VMEM_BIG), name="attn_bwd")(*([proj] * 9), o_att, lse, do_att)


def _bdot(a, b, dims):
    return lax.dot_general(a.astype(BF), b.astype(BF), (dims, ((), ())), preferred_element_type=F32)


def _hg_chunk(qh, fh, ih, gh, lbraw, nw, st):
    C = HG_CHUNK
    mx = jnp.maximum(lbraw[0:1], lbraw[1:2])
    e0 = jnp.exp(lbraw[0:1] - mx)
    e1 = jnp.exp(lbraw[1:2] - mx)
    lb = e0 / (e0 + e1)
    f = lb + (1.0 - lb) * jax.nn.sigmoid(fh)
    logf = jnp.log(f)
    k = 1.0 - f
    q = qh * jax.nn.sigmoid(qh)
    causal = lax.broadcasted_iota(jnp.int32, (C, C), 0) >= lax.broadcasted_iota(jnp.int32, (C, C), 1)
    b = lax.dot_general(causal.astype(F32), logf, (((1,), (0,)), ((), ())), precision=lax.Precision.HIGHEST,
                        preferred_element_type=F32)
    b_last = b[C - 1:C]
    st_out = st * jnp.exp(b_last) + _bdot(ih, k * jnp.exp(b_last - b), ((0,), (0,)))
    inter = _bdot(q * jnp.exp(b), st, ((1,), (1,)))
    b_ref = b[C // 2:C // 2 + 1]
    a = _bdot(q * jnp.exp(b - b_ref), k * jnp.exp(b_ref - b), ((1,), (1,)))
    a = jnp.where(causal, a, 0.0)
    o = inter + _bdot(a, ih, ((1,), (0,)))
    o = o * lax.rsqrt(jnp.mean(o * o, axis=-1, keepdims=True) + RMS_EPS) * nw
    return o * (gh * jax.nn.sigmoid(gh)), st_out


def _to_heads(x):
    return jnp.concatenate([x[None, :, h * LANE:(h + 1) * LANE] for h in range(HG_HEADS)], axis=0)


def _from_heads(x):
    return jnp.concatenate([x[h] for h in range(HG_HEADS)], axis=1)


def _hdot(a, b, ca, cb):
    return lax.dot_general(a.astype(BF), b.astype(BF), (((ca,), (cb,)), ((0,), (0,))), preferred_element_type=F32)


def _hg_chunk_all(qh, fh, ih, gh, lbraw, nw, st):
    C = HG_CHUNK
    mx = jnp.maximum(lbraw[0:1], lbraw[1:2])
    e0 = jnp.exp(lbraw[0:1] - mx)
    e1 = jnp.exp(lbraw[1:2] - mx)
    lb = e0 / (e0 + e1)
    f = lb + (1.0 - lb) * jax.nn.sigmoid(fh)
    logf = jnp.log(f)
    k = 1.0 - f
    q = qh * jax.nn.sigmoid(qh)
    causal = lax.broadcasted_iota(jnp.int32, (C, C), 0) >= lax.broadcasted_iota(jnp.int32, (C, C), 1)
    b = lax.dot_general(causal.astype(F32), logf, (((1,), (0,)), ((), ())), precision=lax.Precision.HIGHEST,
                        preferred_element_type=F32)
    b_last = b[C - 1:C]
    b_ref = b[C // 2:C // 2 + 1]
    v3 = _to_heads(ih)
    st_out = st * _to_heads(jnp.exp(b_last)) + _hdot(v3, _to_heads(k * jnp.exp(b_last - b)), 1, 1)
    inter = _hdot(_to_heads(q * jnp.exp(b)), st, 2, 2)
    a = _hdot(_to_heads(q * jnp.exp(b - b_ref)), _to_heads(k * jnp.exp(b_ref - b)), 2, 2)
    a = jnp.where(causal[None], a, 0.0)
    o = inter + _hdot(a, v3, 2, 1)
    o = o * lax.rsqrt(jnp.mean(o * o, axis=-1, keepdims=True) + RMS_EPS) * nw
    return _from_heads(o) * (gh * jax.nn.sigmoid(gh)), st_out


def _head_cols(h):
    return slice(h * LANE, (h + 1) * LANE)


def _hg_in_specs(chunk_of):
    half = 4 * LANE
    specs = []
    for base in (QH0, FH0, IH0, GH0):
        for part in range(2):
            specs.append(pl.BlockSpec((HG_CHUNK, half), lambda n, col=base // 4 + part: (chunk_of(n), col)))
    return specs


def _hg_sections(sec):
    return [jnp.concatenate([sec[2 * s][...], sec[2 * s + 1][...]], axis=1) for s in range(4)]


def _hg_fwd(proj, lb, nw):
    S = proj.shape[0]
    C = HG_CHUNK
    NC = S // C

    def body(*refs):
        sec = refs[:8]
        lb_ref, nw_ref, o_ref, st_ref, st_scr = refs[8:]

        @pl.when(pl.program_id(0) == 0)
        def _():
            st_scr[...] = jnp.zeros_like(st_scr)

        st = st_scr[...]
        st_ref[:, 0] = st
        o, st_out = _hg_chunk_all(*_hg_sections(sec), lb_ref[...], nw_ref[...], st)
        o_ref[...] = o.astype(o_ref.dtype)
        st_scr[...] = st_out

    return _pcall(
        body, grid=(NC,),
        in_specs=_hg_in_specs(lambda n: n) + [pl.BlockSpec((2, HG_WIDTH), lambda n: (0, 0)), pl.BlockSpec((1, LANE), lambda n: (0, 0))],
        out_specs=[pl.BlockSpec((C, HG_WIDTH), lambda n: (n, 0)), pl.BlockSpec((HG_HEADS, 1, LANE, LANE), lambda n: (0, n, 0, 0))],
        out_shape=[SDS((S, HG_WIDTH), BF), SDS((HG_HEADS, NC, LANE, LANE), F32)],
        scratch_shapes=[pltpu.VMEM((HG_HEADS, LANE, LANE), F32)], name="hg_fwd")(*([proj] * 8), lb, nw)


def _hg_bwd(proj, lb, nw, states, do_hg):
    S = proj.shape[0]
    C = HG_CHUNK
    NC = S // C

    def body(*refs):
        sec = refs[:8]
        lb_ref, nw_ref, st_ref, do_ref, dq_ref, df_ref, di_ref, dg_ref, dlb_ref, dnw_ref, dst_scr = refs[8:]

        @pl.when(pl.program_id(0) == 0)
        def _():
            dst_scr[...] = jnp.zeros_like(dst_scr)
            dlb_ref[...] = jnp.zeros_like(dlb_ref)
            dnw_ref[...] = jnp.zeros_like(dnw_ref)

        _, vjp = jax.vjp(_hg_chunk_all, *_hg_sections(sec), lb_ref[...], nw_ref[...], st_ref[:, 0])
        dq, df, di, dg, dlb, dnw, dst = vjp((do_ref[...].astype(F32), dst_scr[...]))
        dq_ref[...] = dq.astype(dq_ref.dtype)
        df_ref[...] = df.astype(df_ref.dtype)
        di_ref[...] = di.astype(di_ref.dtype)
        dg_ref[...] = dg.astype(dg_ref.dtype)
        dlb_ref[...] += dlb
        dnw_ref[...] += dnw
        dst_scr[...] = dst

    rev = lambda n: NC - 1 - n
    ospec = pl.BlockSpec((C, HG_WIDTH), lambda n: (rev(n), 0))
    lbs = pl.BlockSpec((2, HG_WIDTH), lambda n: (0, 0))
    nws = pl.BlockSpec((1, LANE), lambda n: (0, 0))
    outs = _pcall(
        body, grid=(NC,),
        in_specs=_hg_in_specs(rev) + [lbs, nws, pl.BlockSpec((HG_HEADS, 1, LANE, LANE), lambda n: (0, rev(n), 0, 0)), ospec],
        out_specs=[ospec] * 4 + [lbs, nws],
        out_shape=[SDS((S, HG_WIDTH), BF)] * 4 + [SDS((2, HG_WIDTH), F32), SDS((1, LANE), F32)],
        scratch_shapes=[pltpu.VMEM((HG_HEADS, LANE, LANE), F32)], name="hg_bwd")(*([proj] * 8), lb, nw, states, do_hg)
    return outs


def _merge(ga, gb, ba, bb):
    return jax.nn.sigmoid(ga) * ba + jax.nn.sigmoid(gb) * bb


_MT = 512


def _gate_specs():
    ga = pl.BlockSpec((_MT, _MT), lambda i, j: (i, GA0 * LANE // _MT + j))
    gb = pl.BlockSpec((_MT, _MT), lambda i, j: (i, GB0 * LANE // _MT + j))
    t = pl.BlockSpec((_MT, _MT), lambda i, j: (i, j))
    return ga, gb, t


def _merge_fwd(proj, ba, bb):
    S, D = ba.shape
    ga, gb, t = _gate_specs()

    def body(ga_ref, gb_ref, ba_ref, bb_ref, o_ref):
        o_ref[...] = _merge(ga_ref[...], gb_ref[...], ba_ref[...], bb_ref[...]).astype(o_ref.dtype)

    return _pcall(body, grid=(S // _MT, D // _MT), in_specs=[ga, gb, t, t], out_specs=t,
                          out_shape=SDS((S, D), BF), name="merge_fwd")(proj, proj, ba, bb)


def _merge_bwd(proj, ba, bb, dm):
    S, D = ba.shape
    ga, gb, t = _gate_specs()

    def body(ga_ref, gb_ref, ba_ref, bb_ref, dm_ref, dga_ref, dgb_ref, dba_ref, dbb_ref):
        _, vjp = jax.vjp(_merge, ga_ref[...], gb_ref[...], ba_ref[...], bb_ref[...])
        dga, dgb, dba, dbb = vjp(dm_ref[...])
        dga_ref[...] = dga.astype(BF)
        dgb_ref[...] = dgb.astype(BF)
        dba_ref[...] = dba.astype(BF)
        dbb_ref[...] = dbb.astype(BF)

    return _pcall(body, grid=(S // _MT, D // _MT), in_specs=[ga, gb, t, t, t], out_specs=[t] * 4,
                          out_shape=[SDS((S, D), BF)] * 4, name="merge_bwd")(proj, proj, ba, bb, dm)


def _swiglu(a, b):
    return a * jax.nn.sigmoid(a) * b


def _swiglu_fwd(a, b):
    S, F = a.shape
    tf = _pick(F, (1408, 512))
    t = pl.BlockSpec((_MT, tf), lambda i, j: (i, j))

    def body(a_ref, b_ref, o_ref):
        o_ref[...] = _swiglu(a_ref[...].astype(F32), b_ref[...].astype(F32)).astype(o_ref.dtype)

    return _pcall(body, grid=(S // _MT, F // tf), in_specs=[t, t], out_specs=t,
                          out_shape=SDS((S, F), BF), name="swiglu_fwd")(a, b)


def _swiglu_bwd(a, b, dg):
    S, F = a.shape
    tf = _pick(F, (1408, 512))
    t = pl.BlockSpec((_MT, tf), lambda i, j: (i, j))

    def body(a_ref, b_ref, dg_ref, da_ref, db_ref):
        _, vjp = jax.vjp(_swiglu, a_ref[...].astype(F32), b_ref[...].astype(F32))
        da, db = vjp(dg_ref[...].astype(F32))
        da_ref[...] = da.astype(BF)
        db_ref[...] = db.astype(BF)

    return _pcall(body, grid=(S // _MT, F // tf), in_specs=[t, t, t], out_specs=[t, t],
                          out_shape=[SDS((S, F), BF)] * 2, name="swiglu_bwd")(a, b, dg)


def _cross(q, k, v):
    s = _bdot(q, k, ((1,), (1,))) * (HEAD_DIM ** -0.5)
    e = jnp.exp(s - jnp.max(s, axis=-1, keepdims=True))
    p = e / jnp.sum(e, axis=-1, keepdims=True)
    return _bdot(p, v, ((1,), (0,)))


def _cross_specs():
    q = pl.BlockSpec((_MT, LANE), lambda h, i: (i, h))
    k = pl.BlockSpec((MEM_LEN, LANE), lambda h, i: (0, h))
    v = pl.BlockSpec((MEM_LEN, LANE), lambda h, i: (0, CROSS_HEADS + h))
    return q, k, v


def _cross_fwd(qc, kvc):
    S = qc.shape[0]
    q, k, v = _cross_specs()

    def body(q_ref, k_ref, v_ref, o_ref):
        o_ref[...] = _cross(q_ref[...], k_ref[...], v_ref[...]).astype(o_ref.dtype)

    return _pcall(body, grid=(CROSS_HEADS, S // _MT), in_specs=[q, k, v], out_specs=q,
                          out_shape=SDS((S, CROSS_WIDTH), BF), name="cross_fwd")(qc, kvc, kvc)


def _cross_bwd(qc, kvc, doc):
    S = qc.shape[0]
    q, k, v = _cross_specs()

    def body(q_ref, k_ref, v_ref, do_ref, dq_ref, dk_ref, dv_ref):
        @pl.when(pl.program_id(1) == 0)
        def _():
            dk_ref[...] = jnp.zeros_like(dk_ref)
            dv_ref[...] = jnp.zeros_like(dv_ref)

        _, vjp = jax.vjp(_cross, q_ref[...], k_ref[...], v_ref[...])
        dq, dk, dv = vjp(do_ref[...])
        dq_ref[...] = dq.astype(dq_ref.dtype)
        dk_ref[...] += dk
        dv_ref[...] += dv

    return _pcall(body, grid=(CROSS_HEADS, S // _MT), in_specs=[q, k, v, q], out_specs=[q, k, k],
                          out_shape=[SDS((S, CROSS_WIDTH), BF), SDS((MEM_LEN, CROSS_WIDTH), F32), SDS((MEM_LEN, CROSS_WIDTH), F32)],
                          name="cross_bwd")(qc, kvc, kvc, doc)


def _local_step(x, mem, tgt, p, ready):
    h = _rms_fwd(x, p["ln_mix"], "rms_mix")
    proj = _mm(h, p["w_in_a"], "nt", F32, "mm_proj_a", a_part=(0, 2))
    proj = _mm(h, p["w_in_b"], "nt", F32, "mm_proj_b", a_part=(1, 2), res=proj)
    o_att, lse = _attn_fwd(proj)
    o_hg, states = _hg_fwd(proj, p["hg_lb"], p["hg_norm"])
    ba = _mm(o_att, p["w_ba"], "nn", F32, "mm_ba")
    bb = _mm(o_hg, p["w_bb"], "nn", F32, "mm_bb")
    merged = _merge_fwd(proj, ba, bb)
    x1 = _mm(merged, p["w_out"], "nn", F32, "mm_out", res=x)
    hc = _rms_fwd(x1, p["ln_cross"], "rms_cross")
    qc = _mm(hc, p["wq"], "nn", F32, "mm_q")
    mn = _rms_fwd(mem, p["ln_mem"], "rms_mem")
    kvc = _mm(mn, p["wkv"], "nn", F32, "mm_kv")
    oc = _cross_fwd(qc, kvc)
    x2 = _mm(oc, p["wo"], "nn", F32, "mm_o", res=x1)
    hf = _rms_fwd(x2, p["ln_ffn"], "rms_ffn")
    a = _mm(hf, p["w1"], "nt", BF, "mm_w1")
    b = _mm(hf, p["w3"], "nt", BF, "mm_w3")
    g = _swiglu_fwd(a, b)
    x3 = _mm(g, p["w2"], "nn", F32, "mm_w2", res=x2)
    loss, dx3, dx3b, d_ln_final = _final_loss(x3, p["ln_final"], tgt)

    gw, gs = {}, {"ln_final": d_ln_final}
    dg = _mm(dx3b, p["w2"], "nt", BF, "mm_dg")
    gw["w2"] = _mm(g, dx3b, "tn", BF, "mm_dw2")
    da, db = _swiglu_bwd(a, b, dg)
    dhf = _mm(da, p["w1"], "nn", F32, "mm_dhf1")
    dhf = _mm(db, p["w3"], "nn", F32, "mm_dhf3", res=dhf)
    gw["w1"] = _mm(da, hf, "tn", BF, "mm_dw1")
    gw["w3"] = _mm(db, hf, "tn", BF, "mm_dw3")
    ready("ffn", gw)
    dx2, dx2b, gs["ln_ffn"] = _rms_bwd(x2, p["ln_ffn"], dhf, dx3, "rms_ffn_bwd")
    doc = _mm(dx2b, p["wo"], "nt", F32, "mm_doc")
    gw["wo"] = _mm(oc, dx2b, "tn", BF, "mm_dwo")
    dqc, dk, dv = _cross_bwd(qc, kvc, doc)
    dkvc = jnp.concatenate([dk, dv], axis=1)
    dhc = _mm(dqc, p["wq"], "nt", F32, "mm_dhc")
    gw["wq"] = _mm(hc, dqc, "tn", BF, "mm_dwq")
    dmn = _mm(dkvc, p["wkv"], "nt", F32, "mm_dmn")
    gw["wkv"] = _mm(mn, dkvc, "tn", BF, "mm_dwkv")
    ready("cross", gw)
    gs["ln_mem"] = _rms_bwd(mem, p["ln_mem"], dmn, None, "rms_mem_bwd")
    dx1, dx1b, gs["ln_cross"] = _rms_bwd(x1, p["ln_cross"], dhc, dx2, "rms_cross_bwd")
    dmerged = _mm(dx1b, p["w_out"], "nt", F32, "mm_dmerged")
    gw["w_out"] = _mm(merged, dx1b, "tn", BF, "mm_dwout")
    dga, dgb, dba, dbb = _merge_bwd(proj, ba, bb, dmerged)
    do_att = _mm(dba, p["w_ba"], "nt", F32, "mm_doatt")
    gw["w_ba"] = _mm(o_att, dba, "tn", BF, "mm_dwba")
    do_hg = _mm(dbb, p["w_bb"], "nt", F32, "mm_dohg")
    gw["w_bb"] = _mm(o_hg, dbb, "tn", BF, "mm_dwbb")
    ready("mix", gw)
    datt = _attn_bwd(proj, o_att, lse, do_att)
    dqh, dfh, dih, dgh, gs["hg_lb"], gs["hg_norm"] = _hg_bwd(proj, p["hg_lb"], p["hg_norm"], states, do_hg)
    ready("mixers_done", gw)
    dproj = jnp.concatenate([*datt, dqh, dfh, dih, dgh, dga, dgb], axis=1)
    for q in range(IN_PARTS):
        gw[f"w_in_q{q}"] = _mm(dproj, h, "tn", BF, f"mm_dwin_q{q}", b_part=(q, IN_PARTS))
        ready(f"in_q{q}", gw)
    dh_a = _mm(dproj, p["w_in_a"], "nn", F32, "mm_dh_a")
    ready("dh_half", gw)
    dh = (dh_a, _mm(dproj, p["w_in_b"], "nn", F32, "mm_dh_b"))
    grad_x, _, gs["ln_mix"] = _rms_bwd(x, p["ln_mix"], dh, dx1, "rms_mix_bwd")
    return loss, grad_x, gs


_MATS = (
    ("w_in_a", "w_in", IN_WIDTH, D_MODEL // 2, "row", IN_WIDTH // N_DEV, True),
    ("w_in_b", "w_in", IN_WIDTH, D_MODEL // 2, "row", IN_WIDTH // N_DEV, True),
    ("w_ba", "w_branch_a", ATT_OUT, D_MODEL, "col", D_MODEL // N_DEV, False),
    ("w_bb", "w_branch_b", HG_WIDTH, D_MODEL, "col", D_MODEL // N_DEV, False),
    ("w_out", "w_out", D_MODEL, D_MODEL, "row", D_MODEL // N_DEV, False),
    ("wq", "wq_cross", D_MODEL, CROSS_WIDTH, "row", D_MODEL // N_DEV, False),
    ("wkv", "wkv_cross", D_MODEL, 2 * CROSS_WIDTH, "row", D_MODEL // N_DEV, False),
    ("wo", "wo_cross", CROSS_WIDTH, D_MODEL, "col", D_MODEL // N_DEV, False),
    ("w1", "w1", D_FF, D_MODEL, "row", D_FF // N_DEV, True),
    ("w3", "w3", D_FF, D_MODEL, "row", D_FF // N_DEV, True),
    ("w2", "w2", D_FF, D_MODEL, "row", D_FF // N_DEV, False),
    ("w_in_q0", "w_in", IN_WIDTH, D_MODEL // 4, "row", IN_WIDTH // N_DEV, True),
    ("w_in_q1", "w_in", IN_WIDTH, D_MODEL // 4, "row", IN_WIDTH // N_DEV, True),
    ("w_in_q2", "w_in", IN_WIDTH, D_MODEL // 4, "row", IN_WIDTH // N_DEV, True),
    ("w_in_q3", "w_in", IN_WIDTH, D_MODEL // 4, "row", IN_WIDTH // N_DEV, True),
)
IN_PARTS = 4
_INDEX = {m[0]: i for i, m in enumerate(_MATS)}


def _unit_shape(i):
    _, _, K, N, kind, sh, _ = _MATS[i]
    return (sh, N) if kind == "row" else (K, sh)


def _win(ref, i, chip, par, half=None):
    _, _, K, _, kind, sh, _ = _MATS[i]
    if kind == "row":
        rows, start = (sh, 0) if half is None else (sh // 2, half * (sh // 2))
        return ref.at[pl.ds(pl.multiple_of((2 * chip + par) * sh + start, 16), rows), :]
    rows, start = (K, 0) if half is None else (K // 2, half * (K // 2))
    return ref.at[pl.ds(start, rows), pl.ds(pl.multiple_of((2 * chip + par) * sh, LANE), sh)]


def _my_place():
    x, y, c = lax.axis_index("x"), lax.axis_index("y"), lax.axis_index("c")
    chips = [(1 - x, y), (x, 1 - y), (1 - x, 1 - y)]
    return x, y, c, chips


def _cast_unit(w, name, half=None):
    K, N = w.shape
    if half is not None:
        N = N // 2
    tr = _pick(K, (256, 400, 352))

    def body(w_ref, o_ref):
        o_ref[...] = w_ref[...].astype(BF)

    src = pl.BlockSpec((tr, N), lambda i: (i, half or 0))
    dst = pl.BlockSpec((tr, N), lambda i: (i, 0))
    return _pcall(body, grid=(K // tr,), in_specs=[src], out_specs=dst, out_shape=SDS((K, N), BF), name=name)(w)


def _handshake(peers):
    barrier = pltpu.get_barrier_semaphore()
    for peer in peers:
        pl.semaphore_signal(barrier, inc=1, device_id=peer, device_id_type=MESH)
    pl.semaphore_wait(barrier, len(peers))


def _sequencer(body, out_type, sems, cid, name):
    return pl.kernel(body, out_type=out_type, mesh=plsc.ScalarSubcoreMesh(axis_name="seq", num_cores=1),
                     scratch_types=sems, compiler_params=pltpu.CompilerParams(collective_id=cid), name=name)


def _all_gather(units, idx, cid, name):
    n = len(idx)

    def body(*refs):
        us, fulls = refs[:n], refs[n:2 * n]
        ssem, rsem, lsem = refs[2 * n:]
        x, y, c, _ = _my_place()
        xn, yn, dg = (1 - x, y), (x, 1 - y), (1 - x, 1 - y)
        me, sib = (x, y, c), (x, y, 1 - c)
        _handshake([sib, (*xn, c), (*yn, c)])

        def win(i, chip_xy, par, half=None):
            return _win(fulls[i], idx[i], 2 * chip_xy[0] + chip_xy[1], par, half)

        def rcopy(i, k, src, dst, to):
            return pltpu.make_async_remote_copy(src_ref=src, dst_ref=dst, send_sem=ssem.at[i, k], recv_sem=rsem.at[i, k],
                                                device_id=to, device_id_type=MESH)

        def landed(i, k, blk):
            rcopy(i, k, blk, blk, me).wait_recv()

        sends, locs = [], []

        def send(i, k, blk, to, src=None):
            cp = rcopy(i, k, blk if src is None else src, blk, to)
            cp.start()
            sends.append(cp)

        for i in range(n):
            mine = win(i, (x, y), c)
            loc = pltpu.make_async_copy(us[i], mine, lsem.at[i])
            loc.start()
            locs.append(loc)
            send(i, 0, mine, sib, src=us[i])
            send(i, 1, mine, (*xn, c), src=us[i])
            send(i, 2, mine, (*yn, c), src=us[i])
        for i in range(n):
            landed(i, 1, win(i, xn, c))
            send(i, 3, win(i, xn, c, 0), (*yn, c))
            send(i, 5, win(i, xn, c), sib)
            landed(i, 2, win(i, yn, c))
            send(i, 4, win(i, yn, c, 1), (*xn, c))
            send(i, 6, win(i, yn, c), sib)
        for i in range(n):
            landed(i, 3, win(i, dg, c, 0))
            landed(i, 4, win(i, dg, c, 1))
            send(i, 7, win(i, dg, c), sib)
        for i in range(n):
            landed(i, 0, win(i, (x, y), 1 - c))
            landed(i, 5, win(i, xn, 1 - c))
            landed(i, 6, win(i, yn, 1 - c))
            landed(i, 7, win(i, dg, 1 - c))
        for cp in sends:
            cp.wait_send()
        for loc in locs:
            loc.wait()

    out_type = [SDS((_MATS[i][2], _MATS[i][3]), BF) for i in idx]
    sems = [pltpu.SemaphoreType.DMA((n, 8)), pltpu.SemaphoreType.DMA((n, 8)), pltpu.SemaphoreType.DMA((n,))]
    return _sequencer(body, out_type, sems, cid, name)(*units)


def _rs_sibling(grads, idx, cid, name):
    n = len(idx)

    def body(*refs):
        gs, bufs = refs[:n], refs[n:2 * n]
        ssem, rsem = refs[2 * n:]
        x, y, c, _ = _my_place()
        _handshake([(x, y, 1 - c)])
        cps = []
        for i in range(n):
            for k in range(4):
                cp = pltpu.make_async_remote_copy(
                    src_ref=_win(gs[i], idx[i], k, 1 - c), dst_ref=bufs[i].at[k], send_sem=ssem.at[i, k], recv_sem=rsem.at[i, k],
                    device_id=(x, y, 1 - c), device_id_type=MESH)
                cp.start()
                cps.append(cp)
        for cp in cps:
            cp.wait()

    out_type = [SDS((4, *_unit_shape(i)), BF) for i in idx]
    sems = [pltpu.SemaphoreType.DMA((n, 4)), pltpu.SemaphoreType.DMA((n, 4))]
    return _sequencer(body, out_type, sems, cid, name)(*grads)


def _chip_sum(g, sib, i, place, name):
    _, _, K, N, kind, sh, _ = _MATS[i]

    def body(pref, g_ref, s_ref, o_ref):
        o_ref[...] = (g_ref[...].astype(F32) + s_ref[...].astype(F32)).astype(BF)

    if kind == "row":
        tr = _pick(sh, (800, 352, 256))
        per = sh // tr
        grid = (4, per)
        g_spec = pl.BlockSpec((tr, N), lambda k, r, pref: ((2 * k + pref[0]) * per + r, 0))
        u_spec = pl.BlockSpec((None, tr, N), lambda k, r, pref: (k, r, 0))
    else:
        grid = (4,)
        g_spec = pl.BlockSpec((K, sh), lambda k, pref: (0, 2 * k + pref[0]))
        u_spec = pl.BlockSpec((None, K, sh), lambda k, pref: (k, 0, 0))
    return _pcall(body, grid=grid, in_specs=[g_spec, u_spec], out_specs=u_spec, out_shape=SDS((4, *_unit_shape(i)), BF),
                  prefetch=True, name=name)(place, g, sib)


def _rs_chips(parts, idx, cid, name):
    n = len(idx)

    def body(*refs):
        ts, bufs, relays = refs[:n], refs[n:2 * n], refs[2 * n:3 * n]
        ssem, rsem = refs[3 * n:]
        x, y, c, _ = _my_place()
        xn, yn, dg = (1 - x, y), (x, 1 - y), (1 - x, 1 - y)
        to_x, to_y, me = (*xn, c), (*yn, c), (x, y, c)
        _handshake([to_x, to_y])

        def rcopy(i, k, src, dst, to):
            return pltpu.make_async_remote_copy(src_ref=src, dst_ref=dst, send_sem=ssem.at[i, k], recv_sem=rsem.at[i, k],
                                                device_id=to, device_id_type=MESH)

        def halves(i):
            rows = _unit_shape(idx[i])[0] // 2
            return [pl.ds(0, rows), pl.ds(rows, rows)]

        sends = []

        def send(cp):
            cp.start()
            sends.append(cp)

        for i in range(n):
            h0, h1 = halves(i)
            diag = ts[i].at[2 * dg[0] + dg[1]]
            send(rcopy(i, 0, ts[i].at[2 * xn[0] + xn[1]], bufs[i].at[0], to_x))
            send(rcopy(i, 1, ts[i].at[2 * yn[0] + yn[1]], bufs[i].at[1], to_y))
            send(rcopy(i, 2, diag.at[h0], relays[i].at[0], to_x))
            send(rcopy(i, 3, diag.at[h1], relays[i].at[1], to_y))
        for i in range(n):
            h0, h1 = halves(i)
            rcopy(i, 2, relays[i].at[0], relays[i].at[0], me).wait_recv()
            send(rcopy(i, 4, relays[i].at[0], bufs[i].at[2, h0], to_y))
            rcopy(i, 3, relays[i].at[1], relays[i].at[1], me).wait_recv()
            send(rcopy(i, 5, relays[i].at[1], bufs[i].at[2, h1], to_x))
        for i in range(n):
            h0, h1 = halves(i)
            rcopy(i, 0, bufs[i].at[0], bufs[i].at[0], me).wait_recv()
            rcopy(i, 1, bufs[i].at[1], bufs[i].at[1], me).wait_recv()
            rcopy(i, 4, bufs[i].at[2, h0], bufs[i].at[2, h0], me).wait_recv()
            rcopy(i, 5, bufs[i].at[2, h1], bufs[i].at[2, h1], me).wait_recv()
        for cp in sends:
            cp.wait_send()

    def relay_shape(i):
        rows, cols = _unit_shape(i)
        return (2, rows // 2, cols)

    out_type = [SDS((3, *_unit_shape(i)), BF) for i in idx] + [SDS(relay_shape(i), BF) for i in idx]
    sems = [pltpu.SemaphoreType.DMA((n, 6)), pltpu.SemaphoreType.DMA((n, 6))]
    return _sequencer(body, out_type, sems, cid, name)(*parts)[:n]


def _adamw(w, g, m, v):
    m = ADAM_B1 * m + (1.0 - ADAM_B1) * g
    v = ADAM_B2 * v + (1.0 - ADAM_B2) * jnp.square(g)
    m_hat = m / (1.0 - ADAM_B1 ** ADAM_STEP)
    v_hat = v / (1.0 - ADAM_B2 ** ADAM_STEP)
    delta = -ADAM_LR * (m_hat / (jnp.sqrt(v_hat) + ADAM_EPS) + ADAM_WD * w)
    return delta, m, v


def _adam_mat(parts, recvs, w, m, v, place, name):
    R, C = w.shape
    n = len(parts)
    tr = _pick(R, (256, 160, 176))

    def body(pref, *refs):
        t_refs, r_refs = refs[:n], refs[n:2 * n]
        w_ref, m_ref, v_ref, g_out, d_out, m_out, v_out = refs[2 * n:]
        cols = [t[...].astype(F32) + r[0].astype(F32) + r[1].astype(F32) + r[2].astype(F32) for t, r in zip(t_refs, r_refs)]
        g = jnp.concatenate(cols, axis=1) if n > 1 else cols[0]
        d, mn, vn = _adamw(w_ref[...], g, m_ref[...], v_ref[...])
        g_out[...] = g
        d_out[...] = d
        m_out[...] = mn
        v_out[...] = vn

    t_spec = pl.BlockSpec((None, tr, C // n), lambda i, pref: (pref[1], i, 0))
    r_spec = pl.BlockSpec((3, tr, C // n), lambda i, pref: (0, i, 0))
    s_spec = pl.BlockSpec((tr, C), lambda i, pref: (i, 0))
    return _pcall(body, grid=(R // tr,), in_specs=[t_spec] * n + [r_spec] * n + [s_spec] * 3, out_specs=[s_spec] * 4,
                  out_shape=[SDS((R, C), F32)] * 4, prefetch=True, name=name)(place, *parts, *recvs, w, m, v)


_SMALL = (("ln_mix", "ln_mix_w", D_MODEL), ("ln_cross", "ln_cross_w", D_MODEL), ("ln_mem", "ln_mem_w", D_MODEL),
          ("ln_ffn", "ln_ffn_w", D_MODEL), ("ln_final", "ln_final_w", D_MODEL), ("hg_lb", "hg_lower_bounds", 2 * HG_WIDTH),
          ("hg_norm", "hg_norm_w", HEAD_DIM))
_PACK = sum(n for _, _, n in _SMALL) + LANE


def _small_sync(grow, wrow, mrow, vrow):
    def body(g_ref, w_ref, m_ref, v_ref, o_ref, gath, ssem, rsem):
        x, y, c, _ = _my_place()
        me = 4 * x + 2 * y + c
        gath[pl.ds(me, 1), :] = g_ref[...]
        cps = []
        for dlt in range(1, N_DEV):
            peer = (me + dlt) % N_DEV
            cp = pltpu.make_async_remote_copy(
                src_ref=g_ref, dst_ref=gath.at[pl.ds(me, 1), :], send_sem=ssem.at[peer], recv_sem=rsem.at[me],
                device_id=(peer // 4, (peer // 2) % 2, peer % 2), device_id_type=MESH)
            cp.start()
            cps.append(cp)
        for dlt in range(1, N_DEV):
            src = (me + dlt) % N_DEV
            pltpu.make_async_remote_copy(
                src_ref=g_ref, dst_ref=gath.at[pl.ds(src, 1), :], send_sem=ssem.at[src], recv_sem=rsem.at[src],
                device_id=(x, y, c), device_id_type=MESH).wait_recv()
        for cp in cps:
            cp.wait_send()
        g = gath[0:1, :]
        for j in range(1, N_DEV):
            g = g + gath[j:j + 1, :]
        d, mn, vn = _adamw(w_ref[...], g, m_ref[...], v_ref[...])
        o_ref[0:1, :] = g
        o_ref[1:2, :] = d
        o_ref[2:3, :] = mn
        o_ref[3:4, :] = vn

    vm = pl.BlockSpec(memory_space=pltpu.VMEM)
    return _pcall(
        body, in_specs=[vm] * 4, out_specs=vm, out_shape=SDS((4, _PACK), F32),
        scratch_shapes=[pltpu.VMEM((N_DEV, _PACK), F32), pltpu.SemaphoreType.DMA((N_DEV,)), pltpu.SemaphoreType.DMA((N_DEV,))],
        name="small_sync")(grow, wrow, mrow, vrow)


def _pack_small(d, loss_row=None):
    parts = [d[k].reshape(1, n).astype(F32) for k, _, n in _SMALL]
    parts.append(loss_row if loss_row is not None else jnp.ones((1, LANE), F32))
    return jnp.concatenate(parts, axis=1)


def kernel(x, mem, ln_mix_w, w_in, hg_norm_w, hg_lower_bounds, w_branch_a, w_branch_b, w_out, ln_cross_w, ln_mem_w, wq_cross, wkv_cross, wo_cross, ln_ffn_w, w1, w3, w2, ln_final_w, loss_target, m_ln_mix_w, m_w_in, m_hg_norm_w, m_hg_lower_bounds, m_w_branch_a, m_w_branch_b, m_w_out, m_ln_cross_w, m_ln_mem_w, m_wq_cross, m_wkv_cross, m_wo_cross, m_ln_ffn_w, m_w1, m_w3, m_w2, m_ln_final_w, v_ln_mix_w, v_w_in, v_hg_norm_w, v_hg_lower_bounds, v_w_branch_a, v_w_branch_b, v_w_out, v_ln_cross_w, v_ln_mem_w, v_wq_cross, v_wkv_cross, v_wo_cross, v_ln_ffn_w, v_w1, v_w3, v_w2, v_ln_final_w):
    given = dict(locals())
    place = jnp.stack([lax.axis_index("c"), 2 * lax.axis_index("x") + lax.axis_index("y")]).astype(jnp.int32)

    _ORDER[0] = None
    index = _INDEX

    def local_view(a, i):
        return a[0].T if _MATS[i][6] else a[0]

    shard = {key: local_view(given[ref], i) for i, (key, ref, *_) in enumerate(_MATS)}

    p = {}
    col_half = {"w_in_a": 0, "w_in_b": 1}
    waves = (("in_a", ["w_in_a"], 1), ("in_b", ["w_in_b"], 12), ("mid", ["w_ba", "w_bb", "w_out", "wq", "wkv", "wo"], 2),
             ("w1", ["w1"], 11), ("w3", ["w3"], 15), ("w2", ["w2"], 16))
    for wave, keys, cid in waves:
        units = [_cast_unit(shard[k], "cast_" + k, col_half.get(k)) for k in keys]
        p.update(zip(keys, _all_gather(units, [index[k] for k in keys], cid, "all_gather_" + wave)))
    small_w = {"ln_mix": ln_mix_w, "ln_cross": ln_cross_w, "ln_mem": ln_mem_w, "ln_ffn": ln_ffn_w,
               "ln_final": ln_final_w.reshape(1, D_MODEL), "hg_lb": hg_lower_bounds, "hg_norm": hg_norm_w}
    p.update(small_w)

    groups = {"ffn": ["w2", "w1", "w3"], "cross": ["wo", "wq", "wkv"], "mix": ["w_out", "w_ba", "w_bb"]}
    cids = {"ffn": (3, 4), "cross": (5, 6), "mix": (7, 8)}
    for q in range(IN_PARTS):
        groups[f"in_q{q}"] = [f"w_in_q{q}"]
        cids[f"in_q{q}"] = ((9, 10), (13, 14), (17, 18), (19, 20))[q]
    sib, parts, recv, out = {}, {}, {}, {}

    grads = {}

    def chip_stage(name):
        keys = groups[name]
        parts[name] = [_chip_sum(grads[k], s, index[k], place, "chip_sum_" + k) for k, s in zip(keys, sib[name])]
        recv[name] = list(_rs_chips(parts[name], [index[k] for k in keys], cids[name][1], "rs_chips_" + name))

    def adam(ref, i, part_list, recv_list):
        res = _adam_mat(part_list, recv_list, shard[_MATS[i][0]], local_view(given["m_" + ref], i), local_view(given["v_" + ref], i),
                        place, "adam_" + ref)
        out[ref] = [(a.T if _MATS[i][6] else a)[None] for a in res]

    def adam_stage(name):
        for k, t, r in zip(groups[name], parts[name], recv[name]):
            adam(_MATS[index[k]][1], index[k], [t], [r])

    def ready(name, gw):
        if name == "mixers_done":
            adam_stage("ffn")
            chip_stage("mix")
            adam_stage("cross")
            return
        if name == "dh_half":
            chip_stage(f"in_q{IN_PARTS - 1}")
            return
        keys = groups[name]
        grads.update({k: gw[k] for k in keys})
        sib[name] = _rs_sibling([gw[k] for k in keys], [index[k] for k in keys], cids[name][0], "rs_sibling_" + name)
        if name == "cross":
            chip_stage("ffn")
        if name == "mix":
            chip_stage("cross")
        if name == "in_q1":
            adam_stage("mix")
        if name.startswith("in_q") and name != "in_q0":
            chip_stage(f"in_q{int(name[4:]) - 1}")

    loss_row, grad_x, gs = _local_step(x[0], mem[0], loss_target[0], p, ready)

    sm = {k: given["m_" + ref] for k, ref, _ in _SMALL}
    sv = {k: given["v_" + ref] for k, ref, _ in _SMALL}
    synced = _small_sync(_pack_small(gs, loss_row), _pack_small(small_w), _pack_small(sm), _pack_small(sv))
    in_groups = [f"in_q{q}" for q in range(IN_PARTS)]
    adam("w_in", index["w_in_a"], sum((parts[g] for g in in_groups), []), sum((recv[g] for g in in_groups), []))
    off = 0
    for k, ref, n in _SMALL:
        shape = given[ref].shape
        out[ref] = [synced[r, off:off + n].reshape(shape) for r in range(4)]
        off += n
    loss = synced[0, off]

    order = ["ln_mix_w", "w_in", "hg_norm_w", "hg_lower_bounds", "w_branch_a", "w_branch_b", "w_out", "ln_cross_w", "ln_mem_w",
             "wq_cross", "wkv_cross", "wo_cross", "ln_ffn_w", "w1", "w3", "w2", "ln_final_w"]
    return (loss, grad_x[None], *[out[n][0] for n in order], *[out[n][1] for n in order],
            *[out[n][2] for n in order], *[out[n][3] for n in order])
```

```python
import functools
import math

import jax
import jax.numpy as jnp
from jax import lax
from jax.experimental import pallas as pl
from jax.experimental.pallas import tpu as pltpu
from jax.experimental.pallas import tpu_sc as plsc

F32 = jnp.float32
BF = jnp.bfloat16
SDS = jax.ShapeDtypeStruct

D_MODEL = 2048
SEQ = 2048
HEAD_DIM = 128
MEM_LEN = 256
ATT_GROUPS = ((128, 1), (512, 4), (2048, 16))
ATT_HEADS = 4
ATT_WIDTH = 1536
ATT_OUT = 512
HG_HEADS = 8
HG_WIDTH = 1024
HG_CHUNK = 64
IN_WIDTH = 12800
CROSS_HEADS = 4
CROSS_WIDTH = 512
D_FF = 5632
RMS_EPS = 1e-6
ADAM_LR = 0.001
ADAM_B1 = 0.9
ADAM_B2 = 0.999
ADAM_EPS = 1e-08
ADAM_WD = 0.01
ADAM_STEP = 10
N_DEV = 8

LANE = 128
QA0, KA0, VA0 = 0, 12, 24
QH0, FH0, IH0, GH0 = 36, 44, 52, 60
GA0, GB0 = 68, 84

VMEM_BIG = 56 * 1024 * 1024
MESH = pl.DeviceIdType.MESH


def _cp(vmem=None, **kw):
    if vmem is not None:
        kw["vmem_limit_bytes"] = vmem
    return pltpu.CompilerParams(**kw)


def _pick(n, cands):
    for c in cands:
        if n % c == 0:
            return c
    raise ValueError(f"no tile for {n}")


_ANY = pl.BlockSpec(memory_space=pl.ANY)
_ORDER = [None]


def _pcall(body, *, name, in_specs, out_specs, out_shape, grid=(), scratch_shapes=(), compiler_params=None, prefetch=False):
    def run(*args):
        dep = _ORDER[0]
        specs, operands, kernel_fn = list(in_specs), list(args), body
        if dep is not None:
            at = len(operands)
            specs.append(_ANY)
            operands.append(dep)

            def kernel_fn(*refs):
                return body(*refs[:at], *refs[at + 1:])
        if prefetch:
            call = pl.pallas_call(
                kernel_fn, out_shape=out_shape, name=name, compiler_params=compiler_params,
                grid_spec=pltpu.PrefetchScalarGridSpec(num_scalar_prefetch=1, grid=grid, in_specs=specs, out_specs=out_specs,
                                                       scratch_shapes=scratch_shapes))
        else:
            call = pl.pallas_call(kernel_fn, grid=grid, in_specs=specs, out_specs=out_specs, out_shape=out_shape,
                                  scratch_shapes=scratch_shapes, compiler_params=compiler_params, name=name)
        out = call(*operands)
        _ORDER[0] = out[0] if isinstance(out, (list, tuple)) else out
        return out
    return run


def _mm(a, b, form, out_dtype, name, res=None, tiles=None, a_part=(0, 1), b_part=(0, 1)):
    a_shape = (a.shape[0], a.shape[1] // a_part[1])
    b_shape = (b.shape[0], b.shape[1] // b_part[1])
    if form == "nn":
        (M, K), (K2, N) = a_shape, b_shape
    elif form == "nt":
        (M, K), (N, K2) = a_shape, b_shape
    else:
        (K, M), (K2, N) = a_shape, b_shape
    assert K == K2, (a.shape, b.shape, form)
    if tiles is None:
        tm = _pick(M, (1024, 1280, 1408, 512, 256))
        tn = _pick(N, (1280, 1024, 1408, 512, 256))
        tk = K if K <= 2048 else _pick(K, (2560, 1408))
    else:
        tm, tn, tk = tiles
    nk = K // tk
    ah, bh = a_part[0], b_part[0]
    if form == "nn":
        a_spec = pl.BlockSpec((tm, tk), lambda i, j, k: (i, k + ah * nk))
        b_spec = pl.BlockSpec((tk, tn), lambda i, j, k: (k, j + bh * (N // tn)))
        dims = ((1,), (0,))
    elif form == "nt":
        a_spec = pl.BlockSpec((tm, tk), lambda i, j, k: (i, k + ah * nk))
        b_spec = pl.BlockSpec((tn, tk), lambda i, j, k: (j, k + bh * nk))
        dims = ((1,), (1,))
    else:
        a_spec = pl.BlockSpec((tk, tm), lambda i, j, k: (k, i + ah * (M // tm)))
        b_spec = pl.BlockSpec((tk, tn), lambda i, j, k: (k, j + bh * (N // tn)))
        dims = ((0,), (0,))
    o_spec = pl.BlockSpec((tm, tn), lambda i, j, k: (i, j))
    in_specs = [a_spec, b_spec]
    args = [a, b]
    if res is not None:
        in_specs.append(o_spec)
        args.append(res)

    def body(*refs):
        a_ref, b_ref = refs[0], refs[1]
        r_ref = refs[2] if res is not None else None
        o_ref = refs[3] if res is not None else refs[2]
        acc = refs[-1] if nk > 1 else None
        k = pl.program_id(2)
        d = lax.dot_general(a_ref[...].astype(BF), b_ref[...].astype(BF), (dims, ((), ())), preferred_element_type=F32)

        def finish(r):
            if res is not None:
                r = r + r_ref[...].astype(F32)
            o_ref[...] = r.astype(o_ref.dtype)

        if nk == 1:
            finish(d)
        else:
            @pl.when(k == 0)
            def _():
                acc[...] = d

            @pl.when((k > 0) & (k < nk - 1))
            def _():
                acc[...] += d

            @pl.when(k == nk - 1)
            def _():
                finish(acc[...] + d)

    return _pcall(
        body, grid=(M // tm, N // tn, nk), in_specs=in_specs, out_specs=o_spec,
        out_shape=SDS((M, N), out_dtype), scratch_shapes=[pltpu.VMEM((tm, tn), F32)] if nk > 1 else [],
        compiler_params=_cp(VMEM_BIG, dimension_semantics=("parallel", "parallel", "arbitrary")),
        name=name)(*args)


def _rms(x, w):
    return x * lax.rsqrt(jnp.mean(x * x, axis=-1, keepdims=True) + RMS_EPS) * w


def _rms_fwd(x, w, name):
    R, D = x.shape
    tr = 256

    def body(x_ref, w_ref, o_ref):
        o_ref[...] = _rms(x_ref[...], w_ref[...]).astype(o_ref.dtype)

    return _pcall(
        body, grid=(R // tr,),
        in_specs=[pl.BlockSpec((tr, D), lambda i: (i, 0)), pl.BlockSpec((1, D), lambda i: (0, 0))],
        out_specs=pl.BlockSpec((tr, D), lambda i: (i, 0)), out_shape=SDS((R, D), BF), name=name)(x, w)


def _rms_bwd(x, w, dh, dres, name):
    R, D = x.shape
    tr = 256
    want_dx = dres is not None
    dhs = dh if isinstance(dh, (tuple, list)) else (dh,)
    nd = len(dhs)

    def body(*refs):
        x_ref, w_ref, dh_refs = refs[0], refs[1], refs[2:2 + nd]
        if want_dx:
            dr_ref, dx_ref, dxb_ref, dw_ref = refs[2 + nd:]
        else:
            dw_ref = refs[2 + nd]

        @pl.when(pl.program_id(0) == 0)
        def _():
            dw_ref[...] = jnp.zeros_like(dw_ref)

        dhv = jnp.concatenate([r[...].astype(F32) for r in dh_refs], axis=1) if nd > 1 else dh_refs[0][...].astype(F32)
        _, vjp = jax.vjp(_rms, x_ref[...], w_ref[...])
        dx, dw = vjp(dhv)
        dw_ref[...] += dw
        if want_dx:
            dx = dx + dr_ref[...]
            dx_ref[...] = dx
            dxb_ref[...] = dx.astype(BF)

    row = pl.BlockSpec((tr, D), lambda i: (i, 0))
    part = pl.BlockSpec((tr, D // nd), lambda i: (i, 0))
    vec = pl.BlockSpec((1, D), lambda i: (0, 0))
    if want_dx:
        return _pcall(body, grid=(R // tr,), in_specs=[row, vec] + [part] * nd + [row], out_specs=[row, row, vec],
                      out_shape=[SDS((R, D), F32), SDS((R, D), BF), SDS((1, D), F32)], name=name)(x, w, *dhs, dres)
    return _pcall(body, grid=(R // tr,), in_specs=[row, vec] + [part] * nd, out_specs=vec,
                  out_shape=SDS((1, D), F32), name=name)(x, w, *dhs)


def _final_loss(x3, w, tgt):
    R, D = x3.shape
    tr = 256

    def loss_fn(xv, wv, tv):
        err = _rms(xv, wv) - tv
        return 0.5 * jnp.sum(jnp.mean(err * err, axis=-1))

    def body(x_ref, w_ref, t_ref, loss_ref, dx_ref, dxb_ref, dw_ref):
        @pl.when(pl.program_id(0) == 0)
        def _():
            dw_ref[...] = jnp.zeros_like(dw_ref)
            loss_ref[...] = jnp.zeros_like(loss_ref)

        tv = t_ref[...]
        val, vjp = jax.vjp(lambda a, b: loss_fn(a, b, tv), x_ref[...], w_ref[...])
        dx, dw = vjp(jnp.ones((), F32))
        dx_ref[...] = dx
        dxb_ref[...] = dx.astype(BF)
        dw_ref[...] += dw
        loss_ref[...] += jnp.full(loss_ref.shape, val, F32)

    row = pl.BlockSpec((tr, D), lambda i: (i, 0))
    vec = pl.BlockSpec((1, D), lambda i: (0, 0))
    one = pl.BlockSpec((1, LANE), lambda i: (0, 0))
    return _pcall(body, grid=(R // tr,), in_specs=[row, vec, row], out_specs=[one, row, row, vec],
                          out_shape=[SDS((1, LANE), F32), SDS((R, D), F32), SDS((R, D), BF), SDS((1, D), F32)],
                          name="final_loss")(x3, w, tgt)


NEG = -1e30


def _att_blocks(S, d):
    L = S // d
    nb = L // LANE
    return nb, (2 if nb > 1 else 1)


def _att_rows(S, d, idx):
    nb, nkb = _att_blocks(S, d)
    r = idx // nb
    n = idx % nb
    kb = jnp.maximum(n - 1, 0)
    if d == 1:
        qrows = pl.ds(pl.multiple_of(n * LANE, LANE), LANE)
        krows = pl.ds(pl.multiple_of(kb * LANE, LANE), LANE * nkb)
    else:
        qrows = pl.ds(r + n * (LANE * d), LANE, stride=d)
        krows = pl.ds(r + kb * (LANE * d), LANE * nkb, stride=d)
    qpos = n * LANE + lax.broadcasted_iota(jnp.int32, (LANE, LANE * nkb), 0)
    kpos = kb * LANE + lax.broadcasted_iota(jnp.int32, (LANE, LANE * nkb), 1)
    return qrows, krows, qpos - kpos


def _slab(col):
    return pl.BlockSpec((SEQ, LANE), lambda h, col=col: (0, col + h))


def _attn_fwd(proj):
    S = proj.shape[0]
    scale = HEAD_DIM ** -0.5

    def body(q0, q1, q2, k0, k1, k2, v0, v1, v2, o_ref, lse_ref, og, lg):
        qs, ks, vs = (q0, q1, q2), (k0, k1, k2), (v0, v1, v2)
        for g, (window, d) in enumerate(ATT_GROUPS):
            def blk(idx, carry, g=g, window=window, d=d):
                qrows, krows, dist = _att_rows(S, d, idx)
                q = qs[g][qrows, :].astype(BF)
                k = ks[g][krows, :].astype(BF)
                v = vs[g][krows, :].astype(BF)
                s = lax.dot_general(q, k, (((1,), (1,)), ((), ())), preferred_element_type=F32) * scale
                s = jnp.where((dist >= 0) & (dist <= window // d), s, NEG)
                m = jnp.max(s, axis=-1, keepdims=True)
                p = jnp.exp(s - m)
                l = jnp.sum(p, axis=-1, keepdims=True)
                o = jnp.dot((p / l).astype(BF), v, preferred_element_type=F32)
                og[g, qrows, :] = o
                lg[g, qrows, :] = jnp.broadcast_to(m + jnp.log(l), (LANE, LANE))
                return carry
            lax.fori_loop(0, S // LANE, blk, 0, unroll=8)

        def merge(t, carry):
            rows = pl.ds(pl.multiple_of(t * 256, 256), 256)
            l0, l1, l2 = lg[0, rows, :], lg[1, rows, :], lg[2, rows, :]
            m = jnp.maximum(jnp.maximum(l0, l1), l2)
            w0, w1, w2 = jnp.exp(l0 - m), jnp.exp(l1 - m), jnp.exp(l2 - m)
            den = w0 + w1 + w2
            o_ref[rows, :] = (w0 * og[0, rows, :] + w1 * og[1, rows, :] + w2 * og[2, rows, :]) / den
            lse_ref[rows, :] = m + jnp.log(den)
            return carry
        lax.fori_loop(0, S // 256, merge, 0)

    ins = [_slab(QA0 + 4 * g) for g in range(3)] + [_slab(KA0 + 4 * g) for g in range(3)] + [_slab(VA0 + 4 * g) for g in range(3)]
    out = pl.BlockSpec((SEQ, LANE), lambda h: (0, h))
    return _pcall(
        body, grid=(ATT_HEADS,), in_specs=ins, out_specs=[out, out],
        out_shape=[SDS((S, ATT_OUT), F32), SDS((S, ATT_OUT), F32)],
        scratch_shapes=[pltpu.VMEM((3, S, LANE), F32), pltpu.VMEM((3, S, LANE), F32)],
        compiler_params=_cp(VMEM_BIG), name="attn_fwd")(*([proj] * 9))


def _attn_bwd(proj, o_att, lse, do_att):
    S = proj.shape[0]
    scale = HEAD_DIM ** -0.5

    def body(q0, q1, q2, k0, k1, k2, v0, v1, v2, o_ref, lse_ref, do_ref,
             dq0, dq1, dq2, dk0, dk1, dk2, dv0, dv1, dv2, dqa, dka, dva, delta):
        qs, ks, vs = (q0, q1, q2), (k0, k1, k2), (v0, v1, v2)
        dqs, dks, dvs = (dq0, dq1, dq2), (dk0, dk1, dk2), (dv0, dv1, dv2)
        delta[...] = jnp.broadcast_to(jnp.sum(do_ref[...] * o_ref[...], axis=-1, keepdims=True), delta.shape)
        for g, (window, d) in enumerate(ATT_GROUPS):
            dka[...] = jnp.zeros_like(dka)
            dva[...] = jnp.zeros_like(dva)

            def blk(idx, carry, g=g, window=window, d=d):
                qrows, krows, dist = _att_rows(S, d, idx)
                q = qs[g][qrows, :].astype(BF)
                k = ks[g][krows, :].astype(BF)
                v = vs[g][krows, :].astype(BF)
                do = do_ref[qrows, :]
                s = lax.dot_general(q, k, (((1,), (1,)), ((), ())), preferred_element_type=F32) * scale
                p = jnp.exp(s - lse_ref[qrows, :][:, 0:1])
                p = jnp.where((dist >= 0) & (dist <= window // d), p, 0.0)
                dob = do.astype(BF)
                dva[krows, :] += lax.dot_general(p.astype(BF), dob, (((0,), (0,)), ((), ())), preferred_element_type=F32)
                dp = lax.dot_general(dob, v, (((1,), (1,)), ((), ())), preferred_element_type=F32)
                ds = (p * (dp - delta[qrows, :][:, 0:1]) * scale).astype(BF)
                dqa[qrows, :] = jnp.dot(ds, k, preferred_element_type=F32)
                dka[krows, :] += lax.dot_general(ds, q, (((0,), (0,)), ((), ())), preferred_element_type=F32)
                return carry
            lax.fori_loop(0, S // LANE, blk, 0, unroll=8)
            dqs[g][...] = dqa[...].astype(dqs[g].dtype)
            dks[g][...] = dka[...].astype(dks[g].dtype)
            dvs[g][...] = dva[...].astype(dvs[g].dtype)

    cols = [QA0 + 4 * g for g in range(3)] + [KA0 + 4 * g for g in range(3)] + [VA0 + 4 * g for g in range(3)]
    ins = [_slab(c) for c in cols]
    one = pl.BlockSpec((SEQ, LANE), lambda h: (0, h))
    return _pcall(
        body, grid=(ATT_HEADS,), in_specs=ins + [one, one, one], out_specs=[one] * 9,
        out_shape=[SDS((S, ATT_OUT), BF)] * 9,
        scratch_shapes=[pltpu.VMEM((S, LANE), F32)] * 4,
        compiler_params=_cp(VMEM_BIG), name="attn_bwd")(*([proj] * 9), o_att, lse, do_att)


def _bdot(a, b, dims):
    return lax.dot_general(a.astype(BF), b.astype(BF), (dims, ((), ())), preferred_element_type=F32)


def _hg_chunk(qh, fh, ih, gh, lbraw, nw, st):
    C = HG_CHUNK
    mx = jnp.maximum(lbraw[0:1], lbraw[1:2])
    e0 = jnp.exp(lbraw[0:1] - mx)
    e1 = jnp.exp(lbraw[1:2] - mx)
    lb = e0 / (e0 + e1)
    f = lb + (1.0 - lb) * jax.nn.sigmoid(fh)
    logf = jnp.log(f)
    k = 1.0 - f
    q = qh * jax.nn.sigmoid(qh)
    causal = lax.broadcasted_iota(jnp.int32, (C, C), 0) >= lax.broadcasted_iota(jnp.int32, (C, C), 1)
    b = lax.dot_general(causal.astype(F32), logf, (((1,), (0,)), ((), ())), precision=lax.Precision.HIGHEST,
                        preferred_element_type=F32)
    b_last = b[C - 1:C]
    st_out = st * jnp.exp(b_last) + _bdot(ih, k * jnp.exp(b_last - b), ((0,), (0,)))
    inter = _bdot(q * jnp.exp(b), st, ((1,), (1,)))
    b_ref = b[C // 2:C // 2 + 1]
    a = _bdot(q * jnp.exp(b - b_ref), k * jnp.exp(b_ref - b), ((1,), (1,)))
    a = jnp.where(causal, a, 0.0)
    o = inter + _bdot(a, ih, ((1,), (0,)))
    o = o * lax.rsqrt(jnp.mean(o * o, axis=-1, keepdims=True) + RMS_EPS) * nw
    return o * (gh * jax.nn.sigmoid(gh)), st_out


def _to_heads(x):
    return jnp.concatenate([x[None, :, h * LANE:(h + 1) * LANE] for h in range(HG_HEADS)], axis=0)


def _from_heads(x):
    return jnp.concatenate([x[h] for h in range(HG_HEADS)], axis=1)


def _hdot(a, b, ca, cb):
    return lax.dot_general(a.astype(BF), b.astype(BF), (((ca,), (cb,)), ((0,), (0,))), preferred_element_type=F32)


def _hg_chunk_all(qh, fh, ih, gh, lbraw, nw, st):
    C = HG_CHUNK
    mx = jnp.maximum(lbraw[0:1], lbraw[1:2])
    e0 = jnp.exp(lbraw[0:1] - mx)
    e1 = jnp.exp(lbraw[1:2] - mx)
    lb = e0 / (e0 + e1)
    f = lb + (1.0 - lb) * jax.nn.sigmoid(fh)
    logf = jnp.log(f)
    k = 1.0 - f
    q = qh * jax.nn.sigmoid(qh)
    causal = lax.broadcasted_iota(jnp.int32, (C, C), 0) >= lax.broadcasted_iota(jnp.int32, (C, C), 1)
    b = lax.dot_general(causal.astype(F32), logf, (((1,), (0,)), ((), ())), precision=lax.Precision.HIGHEST,
                        preferred_element_type=F32)
    b_last = b[C - 1:C]
    b_ref = b[C // 2:C // 2 + 1]
    v3 = _to_heads(ih)
    st_out = st * _to_heads(jnp.exp(b_last)) + _hdot(v3, _to_heads(k * jnp.exp(b_last - b)), 1, 1)
    inter = _hdot(_to_heads(q * jnp.exp(b)), st, 2, 2)
    a = _hdot(_to_heads(q * jnp.exp(b - b_ref)), _to_heads(k * jnp.exp(b_ref - b)), 2, 2)
    a = jnp.where(causal[None], a, 0.0)
    o = inter + _hdot(a, v3, 2, 1)
    o = o * lax.rsqrt(jnp.mean(o * o, axis=-1, keepdims=True) + RMS_EPS) * nw
    return _from_heads(o) * (gh * jax.nn.sigmoid(gh)), st_out


def _head_cols(h):
    return slice(h * LANE, (h + 1) * LANE)


def _hg_in_specs(chunk_of):
    half = 4 * LANE
    specs = []
    for base in (QH0, FH0, IH0, GH0):
        for part in range(2):
            specs.append(pl.BlockSpec((HG_CHUNK, half), lambda n, col=base // 4 + part: (chunk_of(n), col)))
    return specs


def _hg_sections(sec):
    return [jnp.concatenate([sec[2 * s][...], sec[2 * s + 1][...]], axis=1) for s in range(4)]


def _hg_fwd(proj, lb, nw):
    S = proj.shape[0]
    C = HG_CHUNK
    NC = S // C

    def body(*refs):
        sec = refs[:8]
        lb_ref, nw_ref, o_ref, st_ref, st_scr = refs[8:]

        @pl.when(pl.program_id(0) == 0)
        def _():
            st_scr[...] = jnp.zeros_like(st_scr)

        st = st_scr[...]
        st_ref[:, 0] = st
        o, st_out = _hg_chunk_all(*_hg_sections(sec), lb_ref[...], nw_ref[...], st)
        o_ref[...] = o.astype(o_ref.dtype)
        st_scr[...] = st_out

    return _pcall(
        body, grid=(NC,),
        in_specs=_hg_in_specs(lambda n: n) + [pl.BlockSpec((2, HG_WIDTH), lambda n: (0, 0)), pl.BlockSpec((1, LANE), lambda n: (0, 0))],
        out_specs=[pl.BlockSpec((C, HG_WIDTH), lambda n: (n, 0)), pl.BlockSpec((HG_HEADS, 1, LANE, LANE), lambda n: (0, n, 0, 0))],
        out_shape=[SDS((S, HG_WIDTH), BF), SDS((HG_HEADS, NC, LANE, LANE), F32)],
        scratch_shapes=[pltpu.VMEM((HG_HEADS, LANE, LANE), F32)], name="hg_fwd")(*([proj] * 8), lb, nw)


def _hg_bwd(proj, lb, nw, states, do_hg):
    S = proj.shape[0]
    C = HG_CHUNK
    NC = S // C

    def body(*refs):
        sec = refs[:8]
        lb_ref, nw_ref, st_ref, do_ref, dq_ref, df_ref, di_ref, dg_ref, dlb_ref, dnw_ref, dst_scr = refs[8:]

        @pl.when(pl.program_id(0) == 0)
        def _():
            dst_scr[...] = jnp.zeros_like(dst_scr)
            dlb_ref[...] = jnp.zeros_like(dlb_ref)
            dnw_ref[...] = jnp.zeros_like(dnw_ref)

        _, vjp = jax.vjp(_hg_chunk_all, *_hg_sections(sec), lb_ref[...], nw_ref[...], st_ref[:, 0])
        dq, df, di, dg, dlb, dnw, dst = vjp((do_ref[...].astype(F32), dst_scr[...]))
        dq_ref[...] = dq.astype(dq_ref.dtype)
        df_ref[...] = df.astype(df_ref.dtype)
        di_ref[...] = di.astype(di_ref.dtype)
        dg_ref[...] = dg.astype(dg_ref.dtype)
        dlb_ref[...] += dlb
        dnw_ref[...] += dnw
        dst_scr[...] = dst

    rev = lambda n: NC - 1 - n
    ospec = pl.BlockSpec((C, HG_WIDTH), lambda n: (rev(n), 0))
    lbs = pl.BlockSpec((2, HG_WIDTH), lambda n: (0, 0))
    nws = pl.BlockSpec((1, LANE), lambda n: (0, 0))
    outs = _pcall(
        body, grid=(NC,),
        in_specs=_hg_in_specs(rev) + [lbs, nws, pl.BlockSpec((HG_HEADS, 1, LANE, LANE), lambda n: (0, rev(n), 0, 0)), ospec],
        out_specs=[ospec] * 4 + [lbs, nws],
        out_shape=[SDS((S, HG_WIDTH), BF)] * 4 + [SDS((2, HG_WIDTH), F32), SDS((1, LANE), F32)],
        scratch_shapes=[pltpu.VMEM((HG_HEADS, LANE, LANE), F32)], name="hg_bwd")(*([proj] * 8), lb, nw, states, do_hg)
    return outs


def _merge(ga, gb, ba, bb):
    return jax.nn.sigmoid(ga) * ba + jax.nn.sigmoid(gb) * bb


_MT = 512


def _gate_specs():
    ga = pl.BlockSpec((_MT, _MT), lambda i, j: (i, GA0 * LANE // _MT + j))
    gb = pl.BlockSpec((_MT, _MT), lambda i, j: (i, GB0 * LANE // _MT + j))
    t = pl.BlockSpec((_MT, _MT), lambda i, j: (i, j))
    return ga, gb, t


def _merge_fwd(proj, ba, bb):
    S, D = ba.shape
    ga, gb, t = _gate_specs()

    def body(ga_ref, gb_ref, ba_ref, bb_ref, o_ref):
        o_ref[...] = _merge(ga_ref[...], gb_ref[...], ba_ref[...], bb_ref[...]).astype(o_ref.dtype)

    return _pcall(body, grid=(S // _MT, D // _MT), in_specs=[ga, gb, t, t], out_specs=t,
                          out_shape=SDS((S, D), BF), name="merge_fwd")(proj, proj, ba, bb)


def _merge_bwd(proj, ba, bb, dm):
    S, D = ba.shape
    ga, gb, t = _gate_specs()

    def body(ga_ref, gb_ref, ba_ref, bb_ref, dm_ref, dga_ref, dgb_ref, dba_ref, dbb_ref):
        _, vjp = jax.vjp(_merge, ga_ref[...], gb_ref[...], ba_ref[...], bb_ref[...])
        dga, dgb, dba, dbb = vjp(dm_ref[...])
        dga_ref[...] = dga.astype(BF)
        dgb_ref[...] = dgb.astype(BF)
        dba_ref[...] = dba.astype(BF)
        dbb_ref[...] = dbb.astype(BF)

    return _pcall(body, grid=(S // _MT, D // _MT), in_specs=[ga, gb, t, t, t], out_specs=[t] * 4,
                          out_shape=[SDS((S, D), BF)] * 4, name="merge_bwd")(proj, proj, ba, bb, dm)


def _swiglu(a, b):
    return a * jax.nn.sigmoid(a) * b


def _swiglu_fwd(a, b):
    S, F = a.shape
    tf = _pick(F, (1408, 512))
    t = pl.BlockSpec((_MT, tf), lambda i, j: (i, j))

    def body(a_ref, b_ref, o_ref):
        o_ref[...] = _swiglu(a_ref[...].astype(F32), b_ref[...].astype(F32)).astype(o_ref.dtype)

    return _pcall(body, grid=(S // _MT, F // tf), in_specs=[t, t], out_specs=t,
                          out_shape=SDS((S, F), BF), name="swiglu_fwd")(a, b)


def _swiglu_bwd(a, b, dg):
    S, F = a.shape
    tf = _pick(F, (1408, 512))
    t = pl.BlockSpec((_MT, tf), lambda i, j: (i, j))

    def body(a_ref, b_ref, dg_ref, da_ref, db_ref):
        _, vjp = jax.vjp(_swiglu, a_ref[...].astype(F32), b_ref[...].astype(F32))
        da, db = vjp(dg_ref[...].astype(F32))
        da_ref[...] = da.astype(BF)
        db_ref[...] = db.astype(BF)

    return _pcall(body, grid=(S // _MT, F // tf), in_specs=[t, t, t], out_specs=[t, t],
                          out_shape=[SDS((S, F), BF)] * 2, name="swiglu_bwd")(a, b, dg)


def _cross(q, k, v):
    s = _bdot(q, k, ((1,), (1,))) * (HEAD_DIM ** -0.5)
    e = jnp.exp(s - jnp.max(s, axis=-1, keepdims=True))
    p = e / jnp.sum(e, axis=-1, keepdims=True)
    return _bdot(p, v, ((1,), (0,)))


def _cross_specs():
    q = pl.BlockSpec((_MT, LANE), lambda h, i: (i, h))
    k = pl.BlockSpec((MEM_LEN, LANE), lambda h, i: (0, h))
    v = pl.BlockSpec((MEM_LEN, LANE), lambda h, i: (0, CROSS_HEADS + h))
    return q, k, v


def _cross_fwd(qc, kvc):
    S = qc.shape[0]
    q, k, v = _cross_specs()

    def body(q_ref, k_ref, v_ref, o_ref):
        o_ref[...] = _cross(q_ref[...], k_ref[...], v_ref[...]).astype(o_ref.dtype)

    return _pcall(body, grid=(CROSS_HEADS, S // _MT), in_specs=[q, k, v], out_specs=q,
                          out_shape=SDS((S, CROSS_WIDTH), BF), name="cross_fwd")(qc, kvc, kvc)


def _cross_bwd(qc, kvc, doc):
    S = qc.shape[0]
    q, k, v = _cross_specs()

    def body(q_ref, k_ref, v_ref, do_ref, dq_ref, dk_ref, dv_ref):
        @pl.when(pl.program_id(1) == 0)
        def _():
            dk_ref[...] = jnp.zeros_like(dk_ref)
            dv_ref[...] = jnp.zeros_like(dv_ref)

        _, vjp = jax.vjp(_cross, q_ref[...], k_ref[...], v_ref[...])
        dq, dk, dv = vjp(do_ref[...])
        dq_ref[...] = dq.astype(dq_ref.dtype)
        dk_ref[...] += dk
        dv_ref[...] += dv

    return _pcall(body, grid=(CROSS_HEADS, S // _MT), in_specs=[q, k, v, q], out_specs=[q, k, k],
                          out_shape=[SDS((S, CROSS_WIDTH), BF), SDS((MEM_LEN, CROSS_WIDTH), F32), SDS((MEM_LEN, CROSS_WIDTH), F32)],
                          name="cross_bwd")(qc, kvc, kvc, doc)


def _local_step(x, mem, tgt, p, ready):
    h = _rms_fwd(x, p["ln_mix"], "rms_mix")
    proj = _mm(h, p["w_in_a"], "nt", F32, "mm_proj_a", a_part=(0, 2))
    proj = _mm(h, p["w_in_b"], "nt", F32, "mm_proj_b", a_part=(1, 2), res=proj)
    o_att, lse = _attn_fwd(proj)
    o_hg, states = _hg_fwd(proj, p["hg_lb"], p["hg_norm"])
    ba = _mm(o_att, p["w_ba"], "nn", F32, "mm_ba")
    bb = _mm(o_hg, p["w_bb"], "nn", F32, "mm_bb")
    merged = _merge_fwd(proj, ba, bb)
    x1 = _mm(merged, p["w_out"], "nn", F32, "mm_out", res=x)
    hc = _rms_fwd(x1, p["ln_cross"], "rms_cross")
    qc = _mm(hc, p["wq"], "nn", F32, "mm_q")
    mn = _rms_fwd(mem, p["ln_mem"], "rms_mem")
    kvc = _mm(mn, p["wkv"], "nn", F32, "mm_kv")
    oc = _cross_fwd(qc, kvc)
    x2 = _mm(oc, p["wo"], "nn", F32, "mm_o", res=x1)
    hf = _rms_fwd(x2, p["ln_ffn"], "rms_ffn")
    a = _mm(hf, p["w1"], "nt", BF, "mm_w1")
    b = _mm(hf, p["w3"], "nt", BF, "mm_w3")
    g = _swiglu_fwd(a, b)
    x3 = _mm(g, p["w2"], "nn", F32, "mm_w2", res=x2)
    loss, dx3, dx3b, d_ln_final = _final_loss(x3, p["ln_final"], tgt)

    gw, gs = {}, {"ln_final": d_ln_final}
    dg = _mm(dx3b, p["w2"], "nt", BF, "mm_dg")
    gw["w2"] = _mm(g, dx3b, "tn", BF, "mm_dw2")
    da, db = _swiglu_bwd(a, b, dg)
    dhf = _mm(da, p["w1"], "nn", F32, "mm_dhf1")
    dhf = _mm(db, p["w3"], "nn", F32, "mm_dhf3", res=dhf)
    gw["w1"] = _mm(da, hf, "tn", BF, "mm_dw1")
    gw["w3"] = _mm(db, hf, "tn", BF, "mm_dw3")
    ready("ffn", gw)
    dx2, dx2b, gs["ln_ffn"] = _rms_bwd(x2, p["ln_ffn"], dhf, dx3, "rms_ffn_bwd")
    doc = _mm(dx2b, p["wo"], "nt", F32, "mm_doc")
    gw["wo"] = _mm(oc, dx2b, "tn", BF, "mm_dwo")
    dqc, dk, dv = _cross_bwd(qc, kvc, doc)
    dkvc = jnp.concatenate([dk, dv], axis=1)
    dhc = _mm(dqc, p["wq"], "nt", F32, "mm_dhc")
    gw["wq"] = _mm(hc, dqc, "tn", BF, "mm_dwq")
    dmn = _mm(dkvc, p["wkv"], "nt", F32, "mm_dmn")
    gw["wkv"] = _mm(mn, dkvc, "tn", BF, "mm_dwkv")
    ready("cross", gw)
    gs["ln_mem"] = _rms_bwd(mem, p["ln_mem"], dmn, None, "rms_mem_bwd")
    dx1, dx1b, gs["ln_cross"] = _rms_bwd(x1, p["ln_cross"], dhc, dx2, "rms_cross_bwd")
    dmerged = _mm(dx1b, p["w_out"], "nt", F32, "mm_dmerged")
    gw["w_out"] = _mm(merged, dx1b, "tn", BF, "mm_dwout")
    dga, dgb, dba, dbb = _merge_bwd(proj, ba, bb, dmerged)
    do_att = _mm(dba, p["w_ba"], "nt", F32, "mm_doatt")
    gw["w_ba"] = _mm(o_att, dba, "tn", BF, "mm_dwba")
    do_hg = _mm(dbb, p["w_bb"], "nt", F32, "mm_dohg")
    gw["w_bb"] = _mm(o_hg, dbb, "tn", BF, "mm_dwbb")
    ready("mix", gw)
    datt = _attn_bwd(proj, o_att, lse, do_att)
    dqh, dfh, dih, dgh, gs["hg_lb"], gs["hg_norm"] = _hg_bwd(proj, p["hg_lb"], p["hg_norm"], states, do_hg)
    ready("mixers_done", gw)
    dproj = jnp.concatenate([*datt, dqh, dfh, dih, dgh, dga, dgb], axis=1)
    for q in range(IN_PARTS):
        gw[f"w_in_q{q}"] = _mm(dproj, h, "tn", BF, f"mm_dwin_q{q}", b_part=(q, IN_PARTS))
        ready(f"in_q{q}", gw)
    dh_a = _mm(dproj, p["w_in_a"], "nn", F32, "mm_dh_a")
    ready("dh_half", gw)
    dh = (dh_a, _mm(dproj, p["w_in_b"], "nn", F32, "mm_dh_b"))
    grad_x, _, gs["ln_mix"] = _rms_bwd(x, p["ln_mix"], dh, dx1, "rms_mix_bwd")
    return loss, grad_x, gs


_MATS = (
    ("w_in_a", "w_in", IN_WIDTH, D_MODEL // 2, "row", IN_WIDTH // N_DEV, True),
    ("w_in_b", "w_in", IN_WIDTH, D_MODEL // 2, "row", IN_WIDTH // N_DEV, True),
    ("w_ba", "w_branch_a", ATT_OUT, D_MODEL, "col", D_MODEL // N_DEV, False),
    ("w_bb", "w_branch_b", HG_WIDTH, D_MODEL, "col", D_MODEL // N_DEV, False),
    ("w_out", "w_out", D_MODEL, D_MODEL, "row", D_MODEL // N_DEV, False),
    ("wq", "wq_cross", D_MODEL, CROSS_WIDTH, "row", D_MODEL // N_DEV, False),
    ("wkv", "wkv_cross", D_MODEL, 2 * CROSS_WIDTH, "row", D_MODEL // N_DEV, False),
    ("wo", "wo_cross", CROSS_WIDTH, D_MODEL, "col", D_MODEL // N_DEV, False),
    ("w1", "w1", D_FF, D_MODEL, "row", D_FF // N_DEV, True),
    ("w3", "w3", D_FF, D_MODEL, "row", D_FF // N_DEV, True),
    ("w2", "w2", D_FF, D_MODEL, "row", D_FF // N_DEV, False),
    ("w_in_q0", "w_in", IN_WIDTH, D_MODEL // 2, "row", IN_WIDTH // N_DEV, True),
    ("w_in_q1", "w_in", IN_WIDTH, D_MODEL // 2, "row", IN_WIDTH // N_DEV, True),
)
IN_PARTS = 2
_INDEX = {m[0]: i for i, m in enumerate(_MATS)}


def _unit_shape(i):
    _, _, K, N, kind, sh, _ = _MATS[i]
    return (sh, N) if kind == "row" else (K, sh)


def _win(ref, i, chip, par, half=None):
    _, _, K, _, kind, sh, _ = _MATS[i]
    if kind == "row":
        rows, start = (sh, 0) if half is None else (sh // 2, half * (sh // 2))
        return ref.at[pl.ds(pl.multiple_of((2 * chip + par) * sh + start, 16), rows), :]
    rows, start = (K, 0) if half is None else (K // 2, half * (K // 2))
    return ref.at[pl.ds(start, rows), pl.ds(pl.multiple_of((2 * chip + par) * sh, LANE), sh)]


def _my_place():
    x, y, c = lax.axis_index("x"), lax.axis_index("y"), lax.axis_index("c")
    chips = [(1 - x, y), (x, 1 - y), (1 - x, 1 - y)]
    return x, y, c, chips


def _cast_unit(w, name, half=None):
    K, N = w.shape
    if half is not None:
        N = N // 2
    tr = _pick(K, (256, 400, 352))

    def body(w_ref, o_ref):
        o_ref[...] = w_ref[...].astype(BF)

    src = pl.BlockSpec((tr, N), lambda i: (i, half or 0))
    dst = pl.BlockSpec((tr, N), lambda i: (i, 0))
    return _pcall(body, grid=(K // tr,), in_specs=[src], out_specs=dst, out_shape=SDS((K, N), BF), name=name)(w)


def _handshake(peers):
    barrier = pltpu.get_barrier_semaphore()
    for peer in peers:
        pl.semaphore_signal(barrier, inc=1, device_id=peer, device_id_type=MESH)
    pl.semaphore_wait(barrier, len(peers))


def _sequencer(body, out_type, sems, cid, name):
    return pl.kernel(body, out_type=out_type, mesh=plsc.ScalarSubcoreMesh(axis_name="seq", num_cores=1),
                     scratch_types=sems, compiler_params=pltpu.CompilerParams(collective_id=cid), name=name)


def _all_gather(units, idx, cid, name):
    n = len(idx)

    def body(*refs):
        us, fulls = refs[:n], refs[n:2 * n]
        ssem, rsem, lsem = refs[2 * n:]
        x, y, c, _ = _my_place()
        xn, yn, dg = (1 - x, y), (x, 1 - y), (1 - x, 1 - y)
        me, sib = (x, y, c), (x, y, 1 - c)
        _handshake([sib, (*xn, c), (*yn, c)])

        def win(i, chip_xy, par, half=None):
            return _win(fulls[i], idx[i], 2 * chip_xy[0] + chip_xy[1], par, half)

        def rcopy(i, k, src, dst, to):
            return pltpu.make_async_remote_copy(src_ref=src, dst_ref=dst, send_sem=ssem.at[i, k], recv_sem=rsem.at[i, k],
                                                device_id=to, device_id_type=MESH)

        def landed(i, k, blk):
            rcopy(i, k, blk, blk, me).wait_recv()

        sends, locs = [], []

        def send(i, k, blk, to, src=None):
            cp = rcopy(i, k, blk if src is None else src, blk, to)
            cp.start()
            sends.append(cp)

        for i in range(n):
            mine = win(i, (x, y), c)
            loc = pltpu.make_async_copy(us[i], mine, lsem.at[i])
            loc.start()
            locs.append(loc)
            send(i, 0, mine, sib, src=us[i])
            send(i, 1, mine, (*xn, c), src=us[i])
            send(i, 2, mine, (*yn, c), src=us[i])
        for i in range(n):
            landed(i, 1, win(i, xn, c))
            send(i, 3, win(i, xn, c, 0), (*yn, c))
            send(i, 5, win(i, xn, c), sib)
            landed(i, 2, win(i, yn, c))
            send(i, 4, win(i, yn, c, 1), (*xn, c))
            send(i, 6, win(i, yn, c), sib)
        for i in range(n):
            landed(i, 3, win(i, dg, c, 0))
            landed(i, 4, win(i, dg, c, 1))
            send(i, 7, win(i, dg, c), sib)
        for i in range(n):
            landed(i, 0, win(i, (x, y), 1 - c))
            landed(i, 5, win(i, xn, 1 - c))
            landed(i, 6, win(i, yn, 1 - c))
            landed(i, 7, win(i, dg, 1 - c))
        for cp in sends:
            cp.wait_send()
        for loc in locs:
            loc.wait()

    out_type = [SDS((_MATS[i][2], _MATS[i][3]), BF) for i in idx]
    sems = [pltpu.SemaphoreType.DMA((n, 8)), pltpu.SemaphoreType.DMA((n, 8)), pltpu.SemaphoreType.DMA((n,))]
    return _sequencer(body, out_type, sems, cid, name)(*units)


def _rs_sibling(grads, idx, cid, name):
    n = len(idx)

    def body(*refs):
        gs, bufs = refs[:n], refs[n:2 * n]
        ssem, rsem = refs[2 * n:]
        x, y, c, _ = _my_place()
        _handshake([(x, y, 1 - c)])
        cps = []
        for i in range(n):
            for k in range(4):
                cp = pltpu.make_async_remote_copy(
                    src_ref=_win(gs[i], idx[i], k, 1 - c), dst_ref=bufs[i].at[k], send_sem=ssem.at[i, k], recv_sem=rsem.at[i, k],
                    device_id=(x, y, 1 - c), device_id_type=MESH)
                cp.start()
                cps.append(cp)
        for cp in cps:
            cp.wait()

    out_type = [SDS((4, *_unit_shape(i)), BF) for i in idx]
    sems = [pltpu.SemaphoreType.DMA((n, 4)), pltpu.SemaphoreType.DMA((n, 4))]
    return _sequencer(body, out_type, sems, cid, name)(*grads)


def _rs_sibling_tc(grad, i, cid, name):
    def body(g_ref, buf_ref, ssem, rsem):
        x, y, c, _ = _my_place()
        _handshake([(x, y, 1 - c)])
        cps = []
        for k in range(4):
            cp = pltpu.make_async_remote_copy(
                src_ref=_win(g_ref, i, k, 1 - c), dst_ref=buf_ref.at[k], send_sem=ssem.at[k], recv_sem=rsem.at[k],
                device_id=(x, y, 1 - c), device_id_type=MESH)
            cp.start()
            cps.append(cp)
        for cp in cps:
            cp.wait()

    return _pcall(body, in_specs=[_ANY], out_specs=_ANY, out_shape=SDS((4, *_unit_shape(i)), BF),
                  scratch_shapes=[pltpu.SemaphoreType.DMA((4,)), pltpu.SemaphoreType.DMA((4,))],
                  compiler_params=_cp(collective_id=cid), name=name)(grad)


def _chip_sum(g, sib, i, place, name):
    _, _, K, N, kind, sh, _ = _MATS[i]

    def body(pref, g_ref, s_ref, o_ref):
        o_ref[...] = (g_ref[...].astype(F32) + s_ref[...].astype(F32)).astype(BF)

    if kind == "row":
        tr = _pick(sh, (800, 352, 256))
        per = sh // tr
        grid = (4, per)
        g_spec = pl.BlockSpec((tr, N), lambda k, r, pref: ((2 * k + pref[0]) * per + r, 0))
        u_spec = pl.BlockSpec((None, tr, N), lambda k, r, pref: (k, r, 0))
    else:
        grid = (4,)
        g_spec = pl.BlockSpec((K, sh), lambda k, pref: (0, 2 * k + pref[0]))
        u_spec = pl.BlockSpec((None, K, sh), lambda k, pref: (k, 0, 0))
    return _pcall(body, grid=grid, in_specs=[g_spec, u_spec], out_specs=u_spec, out_shape=SDS((4, *_unit_shape(i)), BF),
                  prefetch=True, name=name)(place, g, sib)


def _rs_chips(parts, idx, cid, name):
    n = len(idx)

    def body(*refs):
        ts, bufs, relays = refs[:n], refs[n:2 * n], refs[2 * n:3 * n]
        ssem, rsem = refs[3 * n:]
        x, y, c, _ = _my_place()
        xn, yn, dg = (1 - x, y), (x, 1 - y), (1 - x, 1 - y)
        to_x, to_y, me = (*xn, c), (*yn, c), (x, y, c)
        _handshake([to_x, to_y])

        def rcopy(i, k, src, dst, to):
            return pltpu.make_async_remote_copy(src_ref=src, dst_ref=dst, send_sem=ssem.at[i, k], recv_sem=rsem.at[i, k],
                                                device_id=to, device_id_type=MESH)

        def halves(i):
            rows = _unit_shape(idx[i])[0] // 2
            return [pl.ds(0, rows), pl.ds(rows, rows)]

        sends = []

        def send(cp):
            cp.start()
            sends.append(cp)

        for i in range(n):
            h0, h1 = halves(i)
            diag = ts[i].at[2 * dg[0] + dg[1]]
            send(rcopy(i, 0, ts[i].at[2 * xn[0] + xn[1]], bufs[i].at[0], to_x))
            send(rcopy(i, 1, ts[i].at[2 * yn[0] + yn[1]], bufs[i].at[1], to_y))
            send(rcopy(i, 2, diag.at[h0], relays[i].at[0], to_x))
            send(rcopy(i, 3, diag.at[h1], relays[i].at[1], to_y))
        for i in range(n):
            h0, h1 = halves(i)
            rcopy(i, 2, relays[i].at[0], relays[i].at[0], me).wait_recv()
            send(rcopy(i, 4, relays[i].at[0], bufs[i].at[2, h0], to_y))
            rcopy(i, 3, relays[i].at[1], relays[i].at[1], me).wait_recv()
            send(rcopy(i, 5, relays[i].at[1], bufs[i].at[2, h1], to_x))
        for i in range(n):
            h0, h1 = halves(i)
            rcopy(i, 0, bufs[i].at[0], bufs[i].at[0], me).wait_recv()
            rcopy(i, 1, bufs[i].at[1], bufs[i].at[1], me).wait_recv()
            rcopy(i, 4, bufs[i].at[2, h0], bufs[i].at[2, h0], me).wait_recv()
            rcopy(i, 5, bufs[i].at[2, h1], bufs[i].at[2, h1], me).wait_recv()
        for cp in sends:
            cp.wait_send()

    def relay_shape(i):
        rows, cols = _unit_shape(i)
        return (2, rows // 2, cols)

    out_type = [SDS((3, *_unit_shape(i)), BF) for i in idx] + [SDS(relay_shape(i), BF) for i in idx]
    sems = [pltpu.SemaphoreType.DMA((n, 6)), pltpu.SemaphoreType.DMA((n, 6))]
    return _sequencer(body, out_type, sems, cid, name)(*parts)[:n]


def _adamw(w, g, m, v):
    m = ADAM_B1 * m + (1.0 - ADAM_B1) * g
    v = ADAM_B2 * v + (1.0 - ADAM_B2) * jnp.square(g)
    m_hat = m / (1.0 - ADAM_B1 ** ADAM_STEP)
    v_hat = v / (1.0 - ADAM_B2 ** ADAM_STEP)
    delta = -ADAM_LR * (m_hat / (jnp.sqrt(v_hat) + ADAM_EPS) + ADAM_WD * w)
    return delta, m, v


def _adam_mat(parts, recvs, w, m, v, place, name):
    R, C = w.shape
    n = len(parts)
    tr = _pick(R, (256, 160, 176))

    def body(pref, *refs):
        t_refs, r_refs = refs[:n], refs[n:2 * n]
        w_ref, m_ref, v_ref, g_out, d_out, m_out, v_out = refs[2 * n:]
        cols = [t[...].astype(F32) + r[0].astype(F32) + r[1].astype(F32) + r[2].astype(F32) for t, r in zip(t_refs, r_refs)]
        g = jnp.concatenate(cols, axis=1) if n > 1 else cols[0]
        d, mn, vn = _adamw(w_ref[...], g, m_ref[...], v_ref[...])
        g_out[...] = g
        d_out[...] = d
        m_out[...] = mn
        v_out[...] = vn

    t_spec = pl.BlockSpec((None, tr, C // n), lambda i, pref: (pref[1], i, 0))
    r_spec = pl.BlockSpec((3, tr, C // n), lambda i, pref: (0, i, 0))
    s_spec = pl.BlockSpec((tr, C), lambda i, pref: (i, 0))
    return _pcall(body, grid=(R // tr,), in_specs=[t_spec] * n + [r_spec] * n + [s_spec] * 3, out_specs=[s_spec] * 4,
                  out_shape=[SDS((R, C), F32)] * 4, prefetch=True, name=name)(place, *parts, *recvs, w, m, v)


_SMALL = (("ln_mix", "ln_mix_w", D_MODEL), ("ln_cross", "ln_cross_w", D_MODEL), ("ln_mem", "ln_mem_w", D_MODEL),
          ("ln_ffn", "ln_ffn_w", D_MODEL), ("ln_final", "ln_final_w", D_MODEL), ("hg_lb", "hg_lower_bounds", 2 * HG_WIDTH),
          ("hg_norm", "hg_norm_w", HEAD_DIM))
_PACK = sum(n for _, _, n in _SMALL) + LANE


def _small_sync(grow, wrow, mrow, vrow):
    def body(g_ref, w_ref, m_ref, v_ref, o_ref, gath, ssem, rsem):
        x, y, c, _ = _my_place()
        me = 4 * x + 2 * y + c
        gath[pl.ds(me, 1), :] = g_ref[...]
        cps = []
        for dlt in range(1, N_DEV):
            peer = (me + dlt) % N_DEV
            cp = pltpu.make_async_remote_copy(
                src_ref=g_ref, dst_ref=gath.at[pl.ds(me, 1), :], send_sem=ssem.at[peer], recv_sem=rsem.at[me],
                device_id=(peer // 4, (peer // 2) % 2, peer % 2), device_id_type=MESH)
            cp.start()
            cps.append(cp)
        for dlt in range(1, N_DEV):
            src = (me + dlt) % N_DEV
            pltpu.make_async_remote_copy(
                src_ref=g_ref, dst_ref=gath.at[pl.ds(src, 1), :], send_sem=ssem.at[src], recv_sem=rsem.at[src],
                device_id=(x, y, c), device_id_type=MESH).wait_recv()
        for cp in cps:
            cp.wait_send()
        g = gath[0:1, :]
        for j in range(1, N_DEV):
            g = g + gath[j:j + 1, :]
        d, mn, vn = _adamw(w_ref[...], g, m_ref[...], v_ref[...])
        o_ref[0:1, :] = g
        o_ref[1:2, :] = d
        o_ref[2:3, :] = mn
        o_ref[3:4, :] = vn

    vm = pl.BlockSpec(memory_space=pltpu.VMEM)
    return _pcall(
        body, in_specs=[vm] * 4, out_specs=vm, out_shape=SDS((4, _PACK), F32),
        scratch_shapes=[pltpu.VMEM((N_DEV, _PACK), F32), pltpu.SemaphoreType.DMA((N_DEV,)), pltpu.SemaphoreType.DMA((N_DEV,))],
        name="small_sync")(grow, wrow, mrow, vrow)


def _pack_small(d, loss_row=None):
    parts = [d[k].reshape(1, n).astype(F32) for k, _, n in _SMALL]
    parts.append(loss_row if loss_row is not None else jnp.ones((1, LANE), F32))
    return jnp.concatenate(parts, axis=1)


def kernel(x, mem, ln_mix_w, w_in, hg_norm_w, hg_lower_bounds, w_branch_a, w_branch_b, w_out, ln_cross_w, ln_mem_w, wq_cross, wkv_cross, wo_cross, ln_ffn_w, w1, w3, w2, ln_final_w, loss_target, m_ln_mix_w, m_w_in, m_hg_norm_w, m_hg_lower_bounds, m_w_branch_a, m_w_branch_b, m_w_out, m_ln_cross_w, m_ln_mem_w, m_wq_cross, m_wkv_cross, m_wo_cross, m_ln_ffn_w, m_w1, m_w3, m_w2, m_ln_final_w, v_ln_mix_w, v_w_in, v_hg_norm_w, v_hg_lower_bounds, v_w_branch_a, v_w_branch_b, v_w_out, v_ln_cross_w, v_ln_mem_w, v_wq_cross, v_wkv_cross, v_wo_cross, v_ln_ffn_w, v_w1, v_w3, v_w2, v_ln_final_w):
    given = dict(locals())
    place = jnp.stack([lax.axis_index("c"), 2 * lax.axis_index("x") + lax.axis_index("y")]).astype(jnp.int32)

    _ORDER[0] = None
    index = _INDEX

    def local_view(a, i):
        return a[0].T if _MATS[i][6] else a[0]

    shard = {key: local_view(given[ref], i) for i, (key, ref, *_) in enumerate(_MATS)}

    p = {}
    col_half = {"w_in_a": 0, "w_in_b": 1}
    waves = (("in_a", ["w_in_a"], 1), ("in_b", ["w_in_b"], 12), ("mid", ["w_ba", "w_bb", "w_out", "wq", "wkv", "wo"], 2),
             ("w1", ["w1"], 11), ("w3", ["w3"], 15), ("w2", ["w2"], 16))
    for wave, keys, cid in waves:
        units = [_cast_unit(shard[k], "cast_" + k, col_half.get(k)) for k in keys]
        p.update(zip(keys, _all_gather(units, [index[k] for k in keys], cid, "all_gather_" + wave)))
    small_w = {"ln_mix": ln_mix_w, "ln_cross": ln_cross_w, "ln_mem": ln_mem_w, "ln_ffn": ln_ffn_w,
               "ln_final": ln_final_w.reshape(1, D_MODEL), "hg_lb": hg_lower_bounds, "hg_norm": hg_norm_w}
    p.update(small_w)

    groups = {"ffn": ["w2", "w1", "w3"], "cross": ["wo", "wq", "wkv"], "mix": ["w_out", "w_ba", "w_bb"]}
    cids = {"ffn": (3, 4), "cross": (5, 6), "mix": (7, 8)}
    for q in range(IN_PARTS):
        groups[f"in_q{q}"] = [f"w_in_q{q}"]
        cids[f"in_q{q}"] = ((9, 10), (13, 14), (17, 18), (19, 20))[q]
    sib, parts, recv, out = {}, {}, {}, {}

    grads = {}

    def chip_stage(name):
        keys = groups[name]
        parts[name] = [_chip_sum(grads[k], s, index[k], place, "chip_sum_" + k) for k, s in zip(keys, sib[name])]
        recv[name] = list(_rs_chips(parts[name], [index[k] for k in keys], cids[name][1], "rs_chips_" + name))

    def adam(ref, i, part_list, recv_list):
        res = _adam_mat(part_list, recv_list, shard[_MATS[i][0]], local_view(given["m_" + ref], i), local_view(given["v_" + ref], i),
                        place, "adam_" + ref)
        out[ref] = [(a.T if _MATS[i][6] else a)[None] for a in res]

    def adam_stage(name):
        for k, t, r in zip(groups[name], parts[name], recv[name]):
            adam(_MATS[index[k]][1], index[k], [t], [r])

    def ready(name, gw):
        if name == "mixers_done":
            adam_stage("ffn")
            chip_stage("mix")
            adam_stage("cross")
            return
        if name == "dh_half":
            return
        keys = groups[name]
        grads.update({k: gw[k] for k in keys})
        if name.startswith("in_q"):
            if name == "in_q1":
                adam_stage("mix")
            sib[name] = [_rs_sibling_tc(gw[k], index[k], cids[name][0], "rs_sibling_" + k) for k in keys]
            chip_stage(name)
            return
        sib[name] = _rs_sibling([gw[k] for k in keys], [index[k] for k in keys], cids[name][0], "rs_sibling_" + name)
        if name == "cross":
            chip_stage("ffn")
        if name == "mix":
            chip_stage("cross")

    loss_row, grad_x, gs = _local_step(x[0], mem[0], loss_target[0], p, ready)

    sm = {k: given["m_" + ref] for k, ref, _ in _SMALL}
    sv = {k: given["v_" + ref] for k, ref, _ in _SMALL}
    synced = _small_sync(_pack_small(gs, loss_row), _pack_small(small_w), _pack_small(sm), _pack_small(sv))
    in_groups = [f"in_q{q}" for q in range(IN_PARTS)]
    adam("w_in", index["w_in_a"], sum((parts[g] for g in in_groups), []), sum((recv[g] for g in in_groups), []))
    off = 0
    for k, ref, n in _SMALL:
        shape = given[ref].shape
        out[ref] = [synced[r, off:off + n].reshape(shape) for r in range(4)]
        off += n
    loss = synced[0, off]

    order = ["ln_mix_w", "w_in", "hg_norm_w", "hg_lower_bounds", "w_branch_a", "w_branch_b", "w_out", "ln_cross_w", "ln_mem_w",
             "wq_cross", "wkv_cross", "wo_cross", "ln_ffn_w", "w1", "w3", "w2", "ln_final_w"]
    return (loss, grad_x[None], *[out[n][0] for n in order], *[out[n][1] for n in order],
            *[out[n][2] for n in order], *[out[n][3] for n in order])
```

```python
import functools
import math

import jax
import jax.numpy as jnp
from jax import lax
from jax.experimental import pallas as pl
from jax.experimental.pallas import tpu as pltpu
from jax.experimental.pallas import tpu_sc as plsc

F32 = jnp.float32
BF = jnp.bfloat16
SDS = jax.ShapeDtypeStruct

D_MODEL = 2048
SEQ = 2048
HEAD_DIM = 128
MEM_LEN = 256
ATT_GROUPS = ((128, 1), (512, 4), (2048, 16))
ATT_HEADS = 4
ATT_WIDTH = 1536
ATT_OUT = 512
HG_HEADS = 8
HG_WIDTH = 1024
HG_CHUNK = 64
IN_WIDTH = 12800
CROSS_HEADS = 4
CROSS_WIDTH = 512
D_FF = 5632
RMS_EPS = 1e-6
ADAM_LR = 0.001
ADAM_B1 = 0.9
ADAM_B2 = 0.999
ADAM_EPS = 1e-08
ADAM_WD = 0.01
ADAM_STEP = 10
N_DEV = 8

LANE = 128
QA0, KA0, VA0 = 0, 12, 24
QH0, FH0, IH0, GH0 = 36, 44, 52, 60
GA0, GB0 = 68, 84

VMEM_BIG = 56 * 1024 * 1024
MESH = pl.DeviceIdType.MESH


def _cp(vmem=None, **kw):
    if vmem is not None:
        kw["vmem_limit_bytes"] = vmem
    return pltpu.CompilerParams(**kw)


def _pick(n, cands):
    for c in cands:
        if n % c == 0:
            return c
    raise ValueError(f"no tile for {n}")


_ANY = pl.BlockSpec(memory_space=pl.ANY)
_ORDER = [None]


def _pcall(body, *, name, in_specs, out_specs, out_shape, grid=(), scratch_shapes=(), compiler_params=None, prefetch=False):
    def run(*args):
        dep = _ORDER[0]
        specs, operands, kernel_fn = list(in_specs), list(args), body
        if dep is not None:
            at = len(operands)
            specs.append(_ANY)
            operands.append(dep)

            def kernel_fn(*refs):
                return body(*refs[:at], *refs[at + 1:])
        if prefetch:
            call = pl.pallas_call(
                kernel_fn, out_shape=out_shape, name=name, compiler_params=compiler_params,
                grid_spec=pltpu.PrefetchScalarGridSpec(num_scalar_prefetch=1, grid=grid, in_specs=specs, out_specs=out_specs,
                                                       scratch_shapes=scratch_shapes))
        else:
            call = pl.pallas_call(kernel_fn, grid=grid, in_specs=specs, out_specs=out_specs, out_shape=out_shape,
                                  scratch_shapes=scratch_shapes, compiler_params=compiler_params, name=name)
        out = call(*operands)
        _ORDER[0] = out[0] if isinstance(out, (list, tuple)) else out
        return out
    return run


def _mm(a, b, form, out_dtype, name, res=None, tiles=None, a_part=(0, 1), b_part=(0, 1)):
    a_shape = (a.shape[0], a.shape[1] // a_part[1])
    b_shape = (b.shape[0], b.shape[1] // b_part[1])
    if form == "nn":
        (M, K), (K2, N) = a_shape, b_shape
    elif form == "nt":
        (M, K), (N, K2) = a_shape, b_shape
    else:
        (K, M), (K2, N) = a_shape, b_shape
    assert K == K2, (a.shape, b.shape, form)
    if tiles is None:
        tm = _pick(M, (1024, 1280, 1408, 512, 256))
        tn = _pick(N, (1280, 1024, 1408, 512, 256))
        tk = K if K <= 2048 else _pick(K, (D_FF, 2560))
        if tk == D_FF:
            tm = _pick(M, (512, 256))
    else:
        tm, tn, tk = tiles
    nk = K // tk
    ah, bh = a_part[0], b_part[0]
    if form == "nn":
        a_spec = pl.BlockSpec((tm, tk), lambda i, j, k: (i, k + ah * nk))
        b_spec = pl.BlockSpec((tk, tn), lambda i, j, k: (k, j + bh * (N // tn)))
        dims = ((1,), (0,))
    elif form == "nt":
        a_spec = pl.BlockSpec((tm, tk), lambda i, j, k: (i, k + ah * nk))
        b_spec = pl.BlockSpec((tn, tk), lambda i, j, k: (j, k + bh * nk))
        dims = ((1,), (1,))
    else:
        a_spec = pl.BlockSpec((tk, tm), lambda i, j, k: (k, i + ah * (M // tm)))
        b_spec = pl.BlockSpec((tk, tn), lambda i, j, k: (k, j + bh * (N // tn)))
        dims = ((0,), (0,))
    o_spec = pl.BlockSpec((tm, tn), lambda i, j, k: (i, j))
    in_specs = [a_spec, b_spec]
    args = [a, b]
    if res is not None:
        in_specs.append(o_spec)
        args.append(res)

    def body(*refs):
        a_ref, b_ref = refs[0], refs[1]
        r_ref = refs[2] if res is not None else None
        o_ref = refs[3] if res is not None else refs[2]
        acc = refs[-1] if nk > 1 else None
        k = pl.program_id(2)
        d = lax.dot_general(a_ref[...].astype(BF), b_ref[...].astype(BF), (dims, ((), ())), preferred_element_type=F32)

        def finish(r):
            if res is not None:
                r = r + r_ref[...].astype(F32)
            o_ref[...] = r.astype(o_ref.dtype)

        if nk == 1:
            finish(d)
        else:
            @pl.when(k == 0)
            def _():
                acc[...] = d

            @pl.when((k > 0) & (k < nk - 1))
            def _():
                acc[...] += d

            @pl.when(k == nk - 1)
            def _():
                finish(acc[...] + d)

    return _pcall(
        body, grid=(M // tm, N // tn, nk), in_specs=in_specs, out_specs=o_spec,
        out_shape=SDS((M, N), out_dtype), scratch_shapes=[pltpu.VMEM((tm, tn), F32)] if nk > 1 else [],
        compiler_params=_cp(VMEM_BIG, dimension_semantics=("parallel", "parallel", "arbitrary")),
        name=name)(*args)


def _rms(x, w):
    return x * lax.rsqrt(jnp.mean(x * x, axis=-1, keepdims=True) + RMS_EPS) * w


def _rms_fwd(x, w, name):
    R, D = x.shape
    tr = 256

    def body(x_ref, w_ref, o_ref):
        o_ref[...] = _rms(x_ref[...], w_ref[...]).astype(o_ref.dtype)

    return _pcall(
        body, grid=(R // tr,),
        in_specs=[pl.BlockSpec((tr, D), lambda i: (i, 0)), pl.BlockSpec((1, D), lambda i: (0, 0))],
        out_specs=pl.BlockSpec((tr, D), lambda i: (i, 0)), out_shape=SDS((R, D), BF), name=name)(x, w)


def _rms_bwd(x, w, dh, dres, name):
    R, D = x.shape
    tr = 256
    want_dx = dres is not None
    dhs = dh if isinstance(dh, (tuple, list)) else (dh,)
    nd = len(dhs)

    def body(*refs):
        x_ref, w_ref, dh_refs = refs[0], refs[1], refs[2:2 + nd]
        if want_dx:
            dr_ref, dx_ref, dxb_ref, dw_ref = refs[2 + nd:]
        else:
            dw_ref = refs[2 + nd]

        @pl.when(pl.program_id(0) == 0)
        def _():
            dw_ref[...] = jnp.zeros_like(dw_ref)

        dhv = jnp.concatenate([r[...].astype(F32) for r in dh_refs], axis=1) if nd > 1 else dh_refs[0][...].astype(F32)
        _, vjp = jax.vjp(_rms, x_ref[...], w_ref[...])
        dx, dw = vjp(dhv)
        dw_ref[...] += dw
        if want_dx:
            dx = dx + dr_ref[...]
            dx_ref[...] = dx
            dxb_ref[...] = dx.astype(BF)

    row = pl.BlockSpec((tr, D), lambda i: (i, 0))
    part = pl.BlockSpec((tr, D // nd), lambda i: (i, 0))
    vec = pl.BlockSpec((1, D), lambda i: (0, 0))
    if want_dx:
        return _pcall(body, grid=(R // tr,), in_specs=[row, vec] + [part] * nd + [row], out_specs=[row, row, vec],
                      out_shape=[SDS((R, D), F32), SDS((R, D), BF), SDS((1, D), F32)], name=name)(x, w, *dhs, dres)
    return _pcall(body, grid=(R // tr,), in_specs=[row, vec] + [part] * nd, out_specs=vec,
                  out_shape=SDS((1, D), F32), name=name)(x, w, *dhs)


def _final_loss(x3, w, tgt):
    R, D = x3.shape
    tr = 256

    def loss_fn(xv, wv, tv):
        err = _rms(xv, wv) - tv
        return 0.5 * jnp.sum(jnp.mean(err * err, axis=-1))

    def body(x_ref, w_ref, t_ref, loss_ref, dx_ref, dxb_ref, dw_ref):
        @pl.when(pl.program_id(0) == 0)
        def _():
            dw_ref[...] = jnp.zeros_like(dw_ref)
            loss_ref[...] = jnp.zeros_like(loss_ref)

        tv = t_ref[...]
        val, vjp = jax.vjp(lambda a, b: loss_fn(a, b, tv), x_ref[...], w_ref[...])
        dx, dw = vjp(jnp.ones((), F32))
        dx_ref[...] = dx
        dxb_ref[...] = dx.astype(BF)
        dw_ref[...] += dw
        loss_ref[...] += jnp.full(loss_ref.shape, val, F32)

    row = pl.BlockSpec((tr, D), lambda i: (i, 0))
    vec = pl.BlockSpec((1, D), lambda i: (0, 0))
    one = pl.BlockSpec((1, LANE), lambda i: (0, 0))
    return _pcall(body, grid=(R // tr,), in_specs=[row, vec, row], out_specs=[one, row, row, vec],
                          out_shape=[SDS((1, LANE), F32), SDS((R, D), F32), SDS((R, D), BF), SDS((1, D), F32)],
                          name="final_loss")(x3, w, tgt)


NEG = -1e30


def _att_blocks(S, d):
    L = S // d
    nb = L // LANE
    return nb, (2 if nb > 1 else 1)


def _att_rows(S, d, idx):
    nb, nkb = _att_blocks(S, d)
    r = idx // nb
    n = idx % nb
    kb = jnp.maximum(n - 1, 0)
    if d == 1:
        qrows = pl.ds(pl.multiple_of(n * LANE, LANE), LANE)
        krows = pl.ds(pl.multiple_of(kb * LANE, LANE), LANE * nkb)
    else:
        qrows = pl.ds(r + n * (LANE * d), LANE, stride=d)
        krows = pl.ds(r + kb * (LANE * d), LANE * nkb, stride=d)
    qpos = n * LANE + lax.broadcasted_iota(jnp.int32, (LANE, LANE * nkb), 0)
    kpos = kb * LANE + lax.broadcasted_iota(jnp.int32, (LANE, LANE * nkb), 1)
    return qrows, krows, qpos - kpos


def _slab(col):
    return pl.BlockSpec((SEQ, LANE), lambda h, col=col: (0, col + h))


def _attn_fwd(proj):
    S = proj.shape[0]
    scale = HEAD_DIM ** -0.5

    def body(q0, q1, q2, k0, k1, k2, v0, v1, v2, o_ref, lse_ref, og, lg):
        qs, ks, vs = (q0, q1, q2), (k0, k1, k2), (v0, v1, v2)
        for g, (window, d) in enumerate(ATT_GROUPS):
            def blk(idx, carry, g=g, window=window, d=d):
                qrows, krows, dist = _att_rows(S, d, idx)
                q = qs[g][qrows, :].astype(BF)
                k = ks[g][krows, :].astype(BF)
                v = vs[g][krows, :].astype(BF)
                s = lax.dot_general(q, k, (((1,), (1,)), ((), ())), preferred_element_type=F32) * scale
                s = jnp.where((dist >= 0) & (dist <= window // d), s, NEG)
                m = jnp.max(s, axis=-1, keepdims=True)
                p = jnp.exp(s - m)
                l = jnp.sum(p, axis=-1, keepdims=True)
                o = jnp.dot((p / l).astype(BF), v, preferred_element_type=F32)
                og[g, qrows, :] = o
                lg[g, qrows, :] = jnp.broadcast_to(m + jnp.log(l), (LANE, LANE))
                return carry
            lax.fori_loop(0, S // LANE, blk, 0, unroll=8)

        def merge(t, carry):
            rows = pl.ds(pl.multiple_of(t * 256, 256), 256)
            l0, l1, l2 = lg[0, rows, :], lg[1, rows, :], lg[2, rows, :]
            m = jnp.maximum(jnp.maximum(l0, l1), l2)
            w0, w1, w2 = jnp.exp(l0 - m), jnp.exp(l1 - m), jnp.exp(l2 - m)
            den = w0 + w1 + w2
            o_ref[rows, :] = (w0 * og[0, rows, :] + w1 * og[1, rows, :] + w2 * og[2, rows, :]) / den
            lse_ref[rows, :] = m + jnp.log(den)
            return carry
        lax.fori_loop(0, S // 256, merge, 0)

    ins = [_slab(QA0 + 4 * g) for g in range(3)] + [_slab(KA0 + 4 * g) for g in range(3)] + [_slab(VA0 + 4 * g) for g in range(3)]
    out = pl.BlockSpec((SEQ, LANE), lambda h: (0, h))
    return _pcall(
        body, grid=(ATT_HEADS,), in_specs=ins, out_specs=[out, out],
        out_shape=[SDS((S, ATT_OUT), F32), SDS((S, ATT_OUT), F32)],
        scratch_shapes=[pltpu.VMEM((3, S, LANE), F32), pltpu.VMEM((3, S, LANE), F32)],
        compiler_params=_cp(VMEM_BIG), name="attn_fwd")(*([proj] * 9))


def _attn_bwd(proj, o_att, lse, do_att):
    S = proj.shape[0]
    scale = HEAD_DIM ** -0.5

    def body(q0, q1, q2, k0, k1, k2, v0, v1, v2, o_ref, lse_ref, do_ref,
             dq0, dq1, dq2, dk0, dk1, dk2, dv0, dv1, dv2, dqa, dka, dva, delta):
        qs, ks, vs = (q0, q1, q2), (k0, k1, k2), (v0, v1, v2)
        dqs, dks, dvs = (dq0, dq1, dq2), (dk0, dk1, dk2), (dv0, dv1, dv2)
        delta[...] = jnp.broadcast_to(jnp.sum(do_ref[...] * o_ref[...], axis=-1, keepdims=True), delta.shape)
        for g, (window, d) in enumerate(ATT_GROUPS):
            dka[...] = jnp.zeros_like(dka)
            dva[...] = jnp.zeros_like(dva)

            def blk(idx, carry, g=g, window=window, d=d):
                qrows, krows, dist = _att_rows(S, d, idx)
                q = qs[g][qrows, :].astype(BF)
                k = ks[g][krows, :].astype(BF)
                v = vs[g][krows, :].astype(BF)
                do = do_ref[qrows, :]
                s = lax.dot_general(q, k, (((1,), (1,)), ((), ())), preferred_element_type=F32) * scale
                p = jnp.exp(s - lse_ref[qrows, :][:, 0:1])
                p = jnp.where((dist >= 0) & (dist <= window // d), p, 0.0)
                dob = do.astype(BF)
                dva[krows, :] += lax.dot_general(p.astype(BF), dob, (((0,), (0,)), ((), ())), preferred_element_type=F32)
                dp = lax.dot_general(dob, v, (((1,), (1,)), ((), ())), preferred_element_type=F32)
                ds = (p * (dp - delta[qrows, :][:, 0:1]) * scale).astype(BF)
                dqa[qrows, :] = jnp.dot(ds, k, preferred_element_type=F32)
                dka[krows, :] += lax.dot_general(ds, q, (((0,), (0,)), ((), ())), preferred_element_type=F32)
                return carry
            lax.fori_loop(0, S // LANE, blk, 0, unroll=8)
            dqs[g][...] = dqa[...].astype(dqs[g].dtype)
            dks[g][...] = dka[...].astype(dks[g].dtype)
            dvs[g][...] = dva[...].astype(dvs[g].dtype)

    cols = [QA0 + 4 * g for g in range(3)] + [KA0 + 4 * g for g in range(3)] + [VA0 + 4 * g for g in range(3)]
    ins = [_slab(c) for c in cols]
    one = pl.BlockSpec((SEQ, LANE), lambda h: (0, h))
    return _pcall(
        body, grid=(ATT_HEADS,), in_specs=ins + [one, one, one], out_specs=[one] * 9,
        out_shape=[SDS((S, ATT_OUT), BF)] * 9,
        scratch_shapes=[pltpu.VMEM((S, LANE), F32)] * 4,
        compiler_params=_cp(VMEM_BIG), name="attn_bwd")(*([proj] * 9), o_att, lse, do_att)


def _bdot(a, b, dims):
    return lax.dot_general(a.astype(BF), b.astype(BF), (dims, ((), ())), preferred_element_type=F32)


def _hg_chunk(qh, fh, ih, gh, lbraw, nw, st):
    C = HG_CHUNK
    mx = jnp.maximum(lbraw[0:1], lbraw[1:2])
    e0 = jnp.exp(lbraw[0:1] - mx)
    e1 = jnp.exp(lbraw[1:2] - mx)
    lb = e0 / (e0 + e1)
    f = lb + (1.0 - lb) * jax.nn.sigmoid(fh)
    logf = jnp.log(f)
    k = 1.0 - f
    q = qh * jax.nn.sigmoid(qh)
    causal = lax.broadcasted_iota(jnp.int32, (C, C), 0) >= lax.broadcasted_iota(jnp.int32, (C, C), 1)
    b = lax.dot_general(causal.astype(F32), logf, (((1,), (0,)), ((), ())), precision=lax.Precision.HIGHEST,
                        preferred_element_type=F32)
    b_last = b[C - 1:C]
    st_out = st * jnp.exp(b_last) + _bdot(ih, k * jnp.exp(b_last - b), ((0,), (0,)))
    inter = _bdot(q * jnp.exp(b), st, ((1,), (1,)))
    b_ref = b[C // 2:C // 2 + 1]
    a = _bdot(q * jnp.exp(b - b_ref), k * jnp.exp(b_ref - b), ((1,), (1,)))
    a = jnp.where(causal, a, 0.0)
    o = inter + _bdot(a, ih, ((1,), (0,)))
    o = o * lax.rsqrt(jnp.mean(o * o, axis=-1, keepdims=True) + RMS_EPS) * nw
    return o * (gh * jax.nn.sigmoid(gh)), st_out


def _to_heads(x):
    return jnp.concatenate([x[None, :, h * LANE:(h + 1) * LANE] for h in range(HG_HEADS)], axis=0)


def _from_heads(x):
    return jnp.concatenate([x[h] for h in range(HG_HEADS)], axis=1)


def _hdot(a, b, ca, cb):
    return lax.dot_general(a.astype(BF), b.astype(BF), (((ca,), (cb,)), ((0,), (0,))), preferred_element_type=F32)


def _hg_chunk_all(qh, fh, ih, gh, lbraw, nw, st):
    C = HG_CHUNK
    mx = jnp.maximum(lbraw[0:1], lbraw[1:2])
    e0 = jnp.exp(lbraw[0:1] - mx)
    e1 = jnp.exp(lbraw[1:2] - mx)
    lb = e0 / (e0 + e1)
    f = lb + (1.0 - lb) * jax.nn.sigmoid(fh)
    logf = jnp.log(f)
    k = 1.0 - f
    q = qh * jax.nn.sigmoid(qh)
    causal = lax.broadcasted_iota(jnp.int32, (C, C), 0) >= lax.broadcasted_iota(jnp.int32, (C, C), 1)
    b = lax.dot_general(causal.astype(F32), logf, (((1,), (0,)), ((), ())), precision=lax.Precision.HIGHEST,
                        preferred_element_type=F32)
    b_last = b[C - 1:C]
    b_ref = b[C // 2:C // 2 + 1]
    v3 = _to_heads(ih)
    st_out = st * _to_heads(jnp.exp(b_last)) + _hdot(v3, _to_heads(k * jnp.exp(b_last - b)), 1, 1)
    inter = _hdot(_to_heads(q * jnp.exp(b)), st, 2, 2)
    a = _hdot(_to_heads(q * jnp.exp(b - b_ref)), _to_heads(k * jnp.exp(b_ref - b)), 2, 2)
    a = jnp.where(causal[None], a, 0.0)
    o = inter + _hdot(a, v3, 2, 1)
    o = o * lax.rsqrt(jnp.mean(o * o, axis=-1, keepdims=True) + RMS_EPS) * nw
    return _from_heads(o) * (gh * jax.nn.sigmoid(gh)), st_out


def _head_cols(h):
    return slice(h * LANE, (h + 1) * LANE)


def _hg_in_specs(chunk_of):
    half = 4 * LANE
    specs = []
    for base in (QH0, FH0, IH0, GH0):
        for part in range(2):
            specs.append(pl.BlockSpec((HG_CHUNK, half), lambda n, col=base // 4 + part: (chunk_of(n), col)))
    return specs


def _hg_sections(sec):
    return [jnp.concatenate([sec[2 * s][...], sec[2 * s + 1][...]], axis=1) for s in range(4)]


def _hg_fwd(proj, lb, nw):
    S = proj.shape[0]
    C = HG_CHUNK
    NC = S // C

    def body(*refs):
        sec = refs[:8]
        lb_ref, nw_ref, o_ref, st_ref, st_scr = refs[8:]

        @pl.when(pl.program_id(0) == 0)
        def _():
            st_scr[...] = jnp.zeros_like(st_scr)

        st = st_scr[...]
        st_ref[:, 0] = st
        o, st_out = _hg_chunk_all(*_hg_sections(sec), lb_ref[...], nw_ref[...], st)
        o_ref[...] = o.astype(o_ref.dtype)
        st_scr[...] = st_out

    return _pcall(
        body, grid=(NC,),
        in_specs=_hg_in_specs(lambda n: n) + [pl.BlockSpec((2, HG_WIDTH), lambda n: (0, 0)), pl.BlockSpec((1, LANE), lambda n: (0, 0))],
        out_specs=[pl.BlockSpec((C, HG_WIDTH), lambda n: (n, 0)), pl.BlockSpec((HG_HEADS, 1, LANE, LANE), lambda n: (0, n, 0, 0))],
        out_shape=[SDS((S, HG_WIDTH), BF), SDS((HG_HEADS, NC, LANE, LANE), F32)],
        scratch_shapes=[pltpu.VMEM((HG_HEADS, LANE, LANE), F32)], name="hg_fwd")(*([proj] * 8), lb, nw)


def _hg_bwd(proj, lb, nw, states, do_hg):
    S = proj.shape[0]
    C = HG_CHUNK
    NC = S // C

    def body(*refs):
        sec = refs[:8]
        lb_ref, nw_ref, st_ref, do_ref, dq_ref, df_ref, di_ref, dg_ref, dlb_ref, dnw_ref, dst_scr = refs[8:]

        @pl.when(pl.program_id(0) == 0)
        def _():
            dst_scr[...] = jnp.zeros_like(dst_scr)
            dlb_ref[...] = jnp.zeros_like(dlb_ref)
            dnw_ref[...] = jnp.zeros_like(dnw_ref)

        _, vjp = jax.vjp(_hg_chunk_all, *_hg_sections(sec), lb_ref[...], nw_ref[...], st_ref[:, 0])
        dq, df, di, dg, dlb, dnw, dst = vjp((do_ref[...].astype(F32), dst_scr[...]))
        dq_ref[...] = dq.astype(dq_ref.dtype)
        df_ref[...] = df.astype(df_ref.dtype)
        di_ref[...] = di.astype(di_ref.dtype)
        dg_ref[...] = dg.astype(dg_ref.dtype)
        dlb_ref[...] += dlb
        dnw_ref[...] += dnw
        dst_scr[...] = dst

    rev = lambda n: NC - 1 - n
    ospec = pl.BlockSpec((C, HG_WIDTH), lambda n: (rev(n), 0))
    lbs = pl.BlockSpec((2, HG_WIDTH), lambda n: (0, 0))
    nws = pl.BlockSpec((1, LANE), lambda n: (0, 0))
    outs = _pcall(
        body, grid=(NC,),
        in_specs=_hg_in_specs(rev) + [lbs, nws, pl.BlockSpec((HG_HEADS, 1, LANE, LANE), lambda n: (0, rev(n), 0, 0)), ospec],
        out_specs=[ospec] * 4 + [lbs, nws],
        out_shape=[SDS((S, HG_WIDTH), BF)] * 4 + [SDS((2, HG_WIDTH), F32), SDS((1, LANE), F32)],
        scratch_shapes=[pltpu.VMEM((HG_HEADS, LANE, LANE), F32)], name="hg_bwd")(*([proj] * 8), lb, nw, states, do_hg)
    return outs


def _merge(ga, gb, ba, bb):
    return jax.nn.sigmoid(ga) * ba + jax.nn.sigmoid(gb) * bb


_MT = 512


def _gate_specs():
    ga = pl.BlockSpec((_MT, _MT), lambda i, j: (i, GA0 * LANE // _MT + j))
    gb = pl.BlockSpec((_MT, _MT), lambda i, j: (i, GB0 * LANE // _MT + j))
    t = pl.BlockSpec((_MT, _MT), lambda i, j: (i, j))
    return ga, gb, t


def _merge_fwd(proj, ba, bb):
    S, D = ba.shape
    ga, gb, t = _gate_specs()

    def body(ga_ref, gb_ref, ba_ref, bb_ref, o_ref):
        o_ref[...] = _merge(ga_ref[...], gb_ref[...], ba_ref[...].astype(F32), bb_ref[...].astype(F32)).astype(o_ref.dtype)

    return _pcall(body, grid=(S // _MT, D // _MT), in_specs=[ga, gb, t, t], out_specs=t,
                          out_shape=SDS((S, D), BF), name="merge_fwd")(proj, proj, ba, bb)


def _merge_bwd(proj, ba, bb, dm):
    S, D = ba.shape
    ga, gb, t = _gate_specs()

    def body(ga_ref, gb_ref, ba_ref, bb_ref, dm_ref, dga_ref, dgb_ref, dba_ref, dbb_ref):
        _, vjp = jax.vjp(_merge, ga_ref[...], gb_ref[...], ba_ref[...].astype(F32), bb_ref[...].astype(F32))
        dga, dgb, dba, dbb = vjp(dm_ref[...].astype(F32))
        dga_ref[...] = dga.astype(BF)
        dgb_ref[...] = dgb.astype(BF)
        dba_ref[...] = dba.astype(BF)
        dbb_ref[...] = dbb.astype(BF)

    return _pcall(body, grid=(S // _MT, D // _MT), in_specs=[ga, gb, t, t, t], out_specs=[t] * 4,
                          out_shape=[SDS((S, D), BF)] * 4, name="merge_bwd")(proj, proj, ba, bb, dm)


def _swiglu(a, b):
    return a * jax.nn.sigmoid(a) * b


def _swiglu_fwd(a, b):
    S, F = a.shape
    tf = _pick(F, (1408, 512))
    t = pl.BlockSpec((_MT, tf), lambda i, j: (i, j))

    def body(a_ref, b_ref, o_ref):
        o_ref[...] = _swiglu(a_ref[...].astype(F32), b_ref[...].astype(F32)).astype(o_ref.dtype)

    return _pcall(body, grid=(S // _MT, F // tf), in_specs=[t, t], out_specs=t,
                          out_shape=SDS((S, F), BF), name="swiglu_fwd")(a, b)


def _swiglu_bwd(a, b, dg):
    S, F = a.shape
    tf = _pick(F, (1408, 512))
    t = pl.BlockSpec((_MT, tf), lambda i, j: (i, j))

    def body(a_ref, b_ref, dg_ref, da_ref, db_ref):
        _, vjp = jax.vjp(_swiglu, a_ref[...].astype(F32), b_ref[...].astype(F32))
        da, db = vjp(dg_ref[...].astype(F32))
        da_ref[...] = da.astype(BF)
        db_ref[...] = db.astype(BF)

    return _pcall(body, grid=(S // _MT, F // tf), in_specs=[t, t, t], out_specs=[t, t],
                          out_shape=[SDS((S, F), BF)] * 2, name="swiglu_bwd")(a, b, dg)


def _cross(q, k, v):
    s = _bdot(q, k, ((1,), (1,))) * (HEAD_DIM ** -0.5)
    e = jnp.exp(s - jnp.max(s, axis=-1, keepdims=True))
    p = e / jnp.sum(e, axis=-1, keepdims=True)
    return _bdot(p, v, ((1,), (0,)))


def _cross_specs():
    q = pl.BlockSpec((_MT, LANE), lambda h, i: (i, h))
    k = pl.BlockSpec((MEM_LEN, LANE), lambda h, i: (0, h))
    v = pl.BlockSpec((MEM_LEN, LANE), lambda h, i: (0, CROSS_HEADS + h))
    return q, k, v


def _cross_fwd(qc, kvc):
    S = qc.shape[0]
    q, k, v = _cross_specs()

    def body(q_ref, k_ref, v_ref, o_ref):
        o_ref[...] = _cross(q_ref[...], k_ref[...], v_ref[...]).astype(o_ref.dtype)

    return _pcall(body, grid=(CROSS_HEADS, S // _MT), in_specs=[q, k, v], out_specs=q,
                          out_shape=SDS((S, CROSS_WIDTH), BF), name="cross_fwd")(qc, kvc, kvc)


def _cross_bwd(qc, kvc, doc):
    S = qc.shape[0]
    q, k, v = _cross_specs()

    def body(q_ref, k_ref, v_ref, do_ref, dq_ref, dk_ref, dv_ref):
        @pl.when(pl.program_id(1) == 0)
        def _():
            dk_ref[...] = jnp.zeros_like(dk_ref)
            dv_ref[...] = jnp.zeros_like(dv_ref)

        _, vjp = jax.vjp(_cross, q_ref[...], k_ref[...], v_ref[...])
        dq, dk, dv = vjp(do_ref[...])
        dq_ref[...] = dq.astype(dq_ref.dtype)
        dk_ref[...] += dk
        dv_ref[...] += dv

    return _pcall(body, grid=(CROSS_HEADS, S // _MT), in_specs=[q, k, v, q], out_specs=[q, k, k],
                          out_shape=[SDS((S, CROSS_WIDTH), BF), SDS((MEM_LEN, CROSS_WIDTH), F32), SDS((MEM_LEN, CROSS_WIDTH), F32)],
                          name="cross_bwd")(qc, kvc, kvc, doc)


def _local_step(x, mem, tgt, p, ready):
    h = _rms_fwd(x, p["ln_mix"], "rms_mix")
    proj = _mm(h, p["w_in_a"], "nt", F32, "mm_proj_a", a_part=(0, 2))
    proj = _mm(h, p["w_in_b"], "nt", F32, "mm_proj_b", a_part=(1, 2), res=proj)
    o_att, lse = _attn_fwd(proj)
    o_hg, states = _hg_fwd(proj, p["hg_lb"], p["hg_norm"])
    ba = _mm(o_att, p["w_ba"], "nn", BF, "mm_ba")
    bb = _mm(o_hg, p["w_bb"], "nn", BF, "mm_bb")
    merged = _merge_fwd(proj, ba, bb)
    x1 = _mm(merged, p["w_out"], "nn", F32, "mm_out", res=x)
    hc = _rms_fwd(x1, p["ln_cross"], "rms_cross")
    qc = _mm(hc, p["wq"], "nn", F32, "mm_q")
    mn = _rms_fwd(mem, p["ln_mem"], "rms_mem")
    kvc = _mm(mn, p["wkv"], "nn", F32, "mm_kv")
    oc = _cross_fwd(qc, kvc)
    x2 = _mm(oc, p["wo"], "nn", F32, "mm_o", res=x1)
    hf = _rms_fwd(x2, p["ln_ffn"], "rms_ffn")
    a = _mm(hf, p["w1"], "nt", BF, "mm_w1")
    b = _mm(hf, p["w3"], "nt", BF, "mm_w3")
    g = _swiglu_fwd(a, b)
    x3 = _mm(g, p["w2"], "nn", F32, "mm_w2", res=x2)
    loss, dx3, dx3b, d_ln_final = _final_loss(x3, p["ln_final"], tgt)

    gw, gs = {}, {"ln_final": d_ln_final}
    dg = _mm(dx3b, p["w2"], "nt", BF, "mm_dg")
    gw["w2"] = _mm(g, dx3b, "tn", BF, "mm_dw2")
    da, db = _swiglu_bwd(a, b, dg)
    dhf = _mm(da, p["w1"], "nn", F32, "mm_dhf1")
    dhf = _mm(db, p["w3"], "nn", F32, "mm_dhf3", res=dhf)
    gw["w1"] = _mm(da, hf, "tn", BF, "mm_dw1")
    gw["w3"] = _mm(db, hf, "tn", BF, "mm_dw3")
    ready("ffn", gw)
    dx2, dx2b, gs["ln_ffn"] = _rms_bwd(x2, p["ln_ffn"], dhf, dx3, "rms_ffn_bwd")
    doc = _mm(dx2b, p["wo"], "nt", F32, "mm_doc")
    gw["wo"] = _mm(oc, dx2b, "tn", BF, "mm_dwo")
    dqc, dk, dv = _cross_bwd(qc, kvc, doc)
    dkvc = jnp.concatenate([dk, dv], axis=1)
    dhc = _mm(dqc, p["wq"], "nt", F32, "mm_dhc")
    gw["wq"] = _mm(hc, dqc, "tn", BF, "mm_dwq")
    dmn = _mm(dkvc, p["wkv"], "nt", F32, "mm_dmn")
    gw["wkv"] = _mm(mn, dkvc, "tn", BF, "mm_dwkv")
    ready("cross", gw)
    gs["ln_mem"] = _rms_bwd(mem, p["ln_mem"], dmn, None, "rms_mem_bwd")
    dx1, dx1b, gs["ln_cross"] = _rms_bwd(x1, p["ln_cross"], dhc, dx2, "rms_cross_bwd")
    dmerged = _mm(dx1b, p["w_out"], "nt", BF, "mm_dmerged")
    gw["w_out"] = _mm(merged, dx1b, "tn", BF, "mm_dwout")
    dga, dgb, dba, dbb = _merge_bwd(proj, ba, bb, dmerged)
    do_att = _mm(dba, p["w_ba"], "nt", F32, "mm_doatt")
    gw["w_ba"] = _mm(o_att, dba, "tn", BF, "mm_dwba")
    do_hg = _mm(dbb, p["w_bb"], "nt", F32, "mm_dohg")
    gw["w_bb"] = _mm(o_hg, dbb, "tn", BF, "mm_dwbb")
    ready("mix", gw)
    datt = _attn_bwd(proj, o_att, lse, do_att)
    dqh, dfh, dih, dgh, gs["hg_lb"], gs["hg_norm"] = _hg_bwd(proj, p["hg_lb"], p["hg_norm"], states, do_hg)
    ready("mixers_done", gw)
    dproj = jnp.concatenate([*datt, dqh, dfh, dih, dgh, dga, dgb], axis=1)
    for q in range(IN_PARTS):
        gw[f"w_in_q{q}"] = _mm(dproj, h, "tn", BF, f"mm_dwin_q{q}", b_part=(q, IN_PARTS))
        ready(f"in_q{q}", gw)
    dh_a = _mm(dproj, p["w_in_a"], "nn", F32, "mm_dh_a")
    ready("dh_half", gw)
    dh = (dh_a, _mm(dproj, p["w_in_b"], "nn", F32, "mm_dh_b"))
    grad_x, _, gs["ln_mix"] = _rms_bwd(x, p["ln_mix"], dh, dx1, "rms_mix_bwd")
    return loss, grad_x, gs


_MATS = (
    ("w_in_a", "w_in", IN_WIDTH, D_MODEL // 2, "row", IN_WIDTH // N_DEV, True),
    ("w_in_b", "w_in", IN_WIDTH, D_MODEL // 2, "row", IN_WIDTH // N_DEV, True),
    ("w_ba", "w_branch_a", ATT_OUT, D_MODEL, "col", D_MODEL // N_DEV, False),
    ("w_bb", "w_branch_b", HG_WIDTH, D_MODEL, "col", D_MODEL // N_DEV, False),
    ("w_out", "w_out", D_MODEL, D_MODEL, "row", D_MODEL // N_DEV, False),
    ("wq", "wq_cross", D_MODEL, CROSS_WIDTH, "row", D_MODEL // N_DEV, False),
    ("wkv", "wkv_cross", D_MODEL, 2 * CROSS_WIDTH, "row", D_MODEL // N_DEV, False),
    ("wo", "wo_cross", CROSS_WIDTH, D_MODEL, "col", D_MODEL // N_DEV, False),
    ("w1", "w1", D_FF, D_MODEL, "row", D_FF // N_DEV, True),
    ("w3", "w3", D_FF, D_MODEL, "row", D_FF // N_DEV, True),
    ("w2", "w2", D_FF, D_MODEL, "row", D_FF // N_DEV, False),
    ("w_in_q0", "w_in", IN_WIDTH, D_MODEL // 2, "row", IN_WIDTH // N_DEV, True),
    ("w_in_q1", "w_in", IN_WIDTH, D_MODEL // 2, "row", IN_WIDTH // N_DEV, True),
)
IN_PARTS = 2
_INDEX = {m[0]: i for i, m in enumerate(_MATS)}


def _unit_shape(i):
    _, _, K, N, kind, sh, _ = _MATS[i]
    return (sh, N) if kind == "row" else (K, sh)


def _win(ref, i, chip, par, half=None):
    _, _, K, _, kind, sh, _ = _MATS[i]
    if kind == "row":
        rows, start = (sh, 0) if half is None else (sh // 2, half * (sh // 2))
        return ref.at[pl.ds(pl.multiple_of((2 * chip + par) * sh + start, 16), rows), :]
    rows, start = (K, 0) if half is None else (K // 2, half * (K // 2))
    return ref.at[pl.ds(start, rows), pl.ds(pl.multiple_of((2 * chip + par) * sh, LANE), sh)]


def _my_place():
    x, y, c = lax.axis_index("x"), lax.axis_index("y"), lax.axis_index("c")
    chips = [(1 - x, y), (x, 1 - y), (1 - x, 1 - y)]
    return x, y, c, chips


def _cast_unit(w, name, half=None):
    K, N = w.shape
    if half is not None:
        N = N // 2
    tr = _pick(K, (256, 400, 352))

    def body(w_ref, o_ref):
        o_ref[...] = w_ref[...].astype(BF)

    src = pl.BlockSpec((tr, N), lambda i: (i, half or 0))
    dst = pl.BlockSpec((tr, N), lambda i: (i, 0))
    return _pcall(body, grid=(K // tr,), in_specs=[src], out_specs=dst, out_shape=SDS((K, N), BF), name=name)(w)


def _handshake(peers):
    barrier = pltpu.get_barrier_semaphore()
    for peer in peers:
        pl.semaphore_signal(barrier, inc=1, device_id=peer, device_id_type=MESH)
    pl.semaphore_wait(barrier, len(peers))


def _sequencer(body, out_type, sems, cid, name):
    return pl.kernel(body, out_type=out_type, mesh=plsc.ScalarSubcoreMesh(axis_name="seq", num_cores=1),
                     scratch_types=sems, compiler_params=pltpu.CompilerParams(collective_id=cid), name=name)


def _all_gather(units, idx, cid, name):
    n = len(idx)

    def body(*refs):
        us, fulls = refs[:n], refs[n:2 * n]
        ssem, rsem, lsem = refs[2 * n:]
        x, y, c, _ = _my_place()
        xn, yn, dg = (1 - x, y), (x, 1 - y), (1 - x, 1 - y)
        me, sib = (x, y, c), (x, y, 1 - c)
        _handshake([sib, (*xn, c), (*yn, c)])

        def win(i, chip_xy, par, half=None):
            return _win(fulls[i], idx[i], 2 * chip_xy[0] + chip_xy[1], par, half)

        def rcopy(i, k, src, dst, to):
            return pltpu.make_async_remote_copy(src_ref=src, dst_ref=dst, send_sem=ssem.at[i, k], recv_sem=rsem.at[i, k],
                                                device_id=to, device_id_type=MESH)

        def landed(i, k, blk):
            rcopy(i, k, blk, blk, me).wait_recv()

        sends, locs = [], []

        def send(i, k, blk, to, src=None):
            cp = rcopy(i, k, blk if src is None else src, blk, to)
            cp.start()
            sends.append(cp)

        for i in range(n):
            mine = win(i, (x, y), c)
            loc = pltpu.make_async_copy(us[i], mine, lsem.at[i])
            loc.start()
            locs.append(loc)
            send(i, 0, mine, sib, src=us[i])
            send(i, 1, mine, (*xn, c), src=us[i])
            send(i, 2, mine, (*yn, c), src=us[i])
        for i in range(n):
            landed(i, 1, win(i, xn, c))
            send(i, 3, win(i, xn, c, 0), (*yn, c))
            send(i, 5, win(i, xn, c), sib)
            landed(i, 2, win(i, yn, c))
            send(i, 4, win(i, yn, c, 1), (*xn, c))
            send(i, 6, win(i, yn, c), sib)
        for i in range(n):
            landed(i, 3, win(i, dg, c, 0))
            landed(i, 4, win(i, dg, c, 1))
            send(i, 7, win(i, dg, c), sib)
        for i in range(n):
            landed(i, 0, win(i, (x, y), 1 - c))
            landed(i, 5, win(i, xn, 1 - c))
            landed(i, 6, win(i, yn, 1 - c))
            landed(i, 7, win(i, dg, 1 - c))
        for cp in sends:
            cp.wait_send()
        for loc in locs:
            loc.wait()

    out_type = [SDS((_MATS[i][2], _MATS[i][3]), BF) for i in idx]
    sems = [pltpu.SemaphoreType.DMA((n, 8)), pltpu.SemaphoreType.DMA((n, 8)), pltpu.SemaphoreType.DMA((n,))]
    return _sequencer(body, out_type, sems, cid, name)(*units)


def _rs_sibling(grads, idx, cid, name):
    n = len(idx)

    def body(*refs):
        gs, bufs = refs[:n], refs[n:2 * n]
        ssem, rsem = refs[2 * n:]
        x, y, c, _ = _my_place()
        _handshake([(x, y, 1 - c)])
        cps = []
        for i in range(n):
            for k in range(4):
                cp = pltpu.make_async_remote_copy(
                    src_ref=_win(gs[i], idx[i], k, 1 - c), dst_ref=bufs[i].at[k], send_sem=ssem.at[i, k], recv_sem=rsem.at[i, k],
                    device_id=(x, y, 1 - c), device_id_type=MESH)
                cp.start()
                cps.append(cp)
        for cp in cps:
            cp.wait()

    out_type = [SDS((4, *_unit_shape(i)), BF) for i in idx]
    sems = [pltpu.SemaphoreType.DMA((n, 4)), pltpu.SemaphoreType.DMA((n, 4))]
    return _sequencer(body, out_type, sems, cid, name)(*grads)


def _rs_sibling_tc(grad, i, cid, name):
    def body(g_ref, buf_ref, ssem, rsem):
        x, y, c, _ = _my_place()
        _handshake([(x, y, 1 - c)])
        cps = []
        for k in range(4):
            cp = pltpu.make_async_remote_copy(
                src_ref=_win(g_ref, i, k, 1 - c), dst_ref=buf_ref.at[k], send_sem=ssem.at[k], recv_sem=rsem.at[k],
                device_id=(x, y, 1 - c), device_id_type=MESH)
            cp.start()
            cps.append(cp)
        for cp in cps:
            cp.wait()

    return _pcall(body, in_specs=[_ANY], out_specs=_ANY, out_shape=SDS((4, *_unit_shape(i)), BF),
                  scratch_shapes=[pltpu.SemaphoreType.DMA((4,)), pltpu.SemaphoreType.DMA((4,))],
                  compiler_params=_cp(collective_id=cid), name=name)(grad)


def _chip_sum(g, sib, i, place, name):
    _, _, K, N, kind, sh, _ = _MATS[i]

    def body(pref, g_ref, s_ref, o_ref):
        o_ref[...] = (g_ref[...].astype(F32) + s_ref[...].astype(F32)).astype(BF)

    if kind == "row":
        tr = _pick(sh, (800, 352, 256))
        per = sh // tr
        grid = (4, per)
        g_spec = pl.BlockSpec((tr, N), lambda k, r, pref: ((2 * k + pref[0]) * per + r, 0))
        u_spec = pl.BlockSpec((None, tr, N), lambda k, r, pref: (k, r, 0))
    else:
        grid = (4,)
        g_spec = pl.BlockSpec((K, sh), lambda k, pref: (0, 2 * k + pref[0]))
        u_spec = pl.BlockSpec((None, K, sh), lambda k, pref: (k, 0, 0))
    return _pcall(body, grid=grid, in_specs=[g_spec, u_spec], out_specs=u_spec, out_shape=SDS((4, *_unit_shape(i)), BF),
                  prefetch=True, name=name)(place, g, sib)


def _rs_chips(parts, idx, cid, name):
    n = len(idx)

    def body(*refs):
        ts, bufs, relays = refs[:n], refs[n:2 * n], refs[2 * n:3 * n]
        ssem, rsem = refs[3 * n:]
        x, y, c, _ = _my_place()
        xn, yn, dg = (1 - x, y), (x, 1 - y), (1 - x, 1 - y)
        to_x, to_y, me = (*xn, c), (*yn, c), (x, y, c)
        _handshake([to_x, to_y])

        def rcopy(i, k, src, dst, to):
            return pltpu.make_async_remote_copy(src_ref=src, dst_ref=dst, send_sem=ssem.at[i, k], recv_sem=rsem.at[i, k],
                                                device_id=to, device_id_type=MESH)

        def halves(i):
            rows = _unit_shape(idx[i])[0] // 2
            return [pl.ds(0, rows), pl.ds(rows, rows)]

        sends = []

        def send(cp):
            cp.start()
            sends.append(cp)

        for i in range(n):
            h0, h1 = halves(i)
            diag = ts[i].at[2 * dg[0] + dg[1]]
            send(rcopy(i, 0, ts[i].at[2 * xn[0] + xn[1]], bufs[i].at[0], to_x))
            send(rcopy(i, 1, ts[i].at[2 * yn[0] + yn[1]], bufs[i].at[1], to_y))
            send(rcopy(i, 2, diag.at[h0], relays[i].at[0], to_x))
            send(rcopy(i, 3, diag.at[h1], relays[i].at[1], to_y))
        for i in range(n):
            h0, h1 = halves(i)
            rcopy(i, 2, relays[i].at[0], relays[i].at[0], me).wait_recv()
            send(rcopy(i, 4, relays[i].at[0], bufs[i].at[2, h0], to_y))
            rcopy(i, 3, relays[i].at[1], relays[i].at[1], me).wait_recv()
            send(rcopy(i, 5, relays[i].at[1], bufs[i].at[2, h1], to_x))
        for i in range(n):
            h0, h1 = halves(i)
            rcopy(i, 0, bufs[i].at[0], bufs[i].at[0], me).wait_recv()
            rcopy(i, 1, bufs[i].at[1], bufs[i].at[1], me).wait_recv()
            rcopy(i, 4, bufs[i].at[2, h0], bufs[i].at[2, h0], me).wait_recv()
            rcopy(i, 5, bufs[i].at[2, h1], bufs[i].at[2, h1], me).wait_recv()
        for cp in sends:
            cp.wait_send()

    def relay_shape(i):
        rows, cols = _unit_shape(i)
        return (2, rows // 2, cols)

    out_type = [SDS((3, *_unit_shape(i)), BF) for i in idx] + [SDS(relay_shape(i), BF) for i in idx]
    sems = [pltpu.SemaphoreType.DMA((n, 6)), pltpu.SemaphoreType.DMA((n, 6))]
    return _sequencer(body, out_type, sems, cid, name)(*parts)[:n]


def _adamw(w, g, m, v):
    m = ADAM_B1 * m + (1.0 - ADAM_B1) * g
    v = ADAM_B2 * v + (1.0 - ADAM_B2) * jnp.square(g)
    m_hat = m / (1.0 - ADAM_B1 ** ADAM_STEP)
    v_hat = v / (1.0 - ADAM_B2 ** ADAM_STEP)
    delta = -ADAM_LR * (m_hat / (jnp.sqrt(v_hat) + ADAM_EPS) + ADAM_WD * w)
    return delta, m, v


def _adam_mat(parts, recvs, w, m, v, place, name):
    R, C = w.shape
    n = len(parts)
    tr = _pick(R, (256, 160, 176))

    def body(pref, *refs):
        t_refs, r_refs = refs[:n], refs[n:2 * n]
        w_ref, m_ref, v_ref, g_out, d_out, m_out, v_out = refs[2 * n:]
        cols = [t[...].astype(F32) + r[0].astype(F32) + r[1].astype(F32) + r[2].astype(F32) for t, r in zip(t_refs, r_refs)]
        g = jnp.concatenate(cols, axis=1) if n > 1 else cols[0]
        d, mn, vn = _adamw(w_ref[...], g, m_ref[...], v_ref[...])
        g_out[...] = g
        d_out[...] = d
        m_out[...] = mn
        v_out[...] = vn

    t_spec = pl.BlockSpec((None, tr, C // n), lambda i, pref: (pref[1], i, 0))
    r_spec = pl.BlockSpec((3, tr, C // n), lambda i, pref: (0, i, 0))
    s_spec = pl.BlockSpec((tr, C), lambda i, pref: (i, 0))
    return _pcall(body, grid=(R // tr,), in_specs=[t_spec] * n + [r_spec] * n + [s_spec] * 3, out_specs=[s_spec] * 4,
                  out_shape=[SDS((R, C), F32)] * 4, prefetch=True, name=name)(place, *parts, *recvs, w, m, v)


_SMALL = (("ln_mix", "ln_mix_w", D_MODEL), ("ln_cross", "ln_cross_w", D_MODEL), ("ln_mem", "ln_mem_w", D_MODEL),
          ("ln_ffn", "ln_ffn_w", D_MODEL), ("ln_final", "ln_final_w", D_MODEL), ("hg_lb", "hg_lower_bounds", 2 * HG_WIDTH),
          ("hg_norm", "hg_norm_w", HEAD_DIM))
_PACK = sum(n for _, _, n in _SMALL) + LANE


def _small_sync(grow, wrow, mrow, vrow):
    def body(g_ref, w_ref, m_ref, v_ref, o_ref, gath, ssem, rsem):
        x, y, c, _ = _my_place()
        me = 4 * x + 2 * y + c
        gath[pl.ds(me, 1), :] = g_ref[...]
        cps = []
        for dlt in range(1, N_DEV):
            peer = (me + dlt) % N_DEV
            cp = pltpu.make_async_remote_copy(
                src_ref=g_ref, dst_ref=gath.at[pl.ds(me, 1), :], send_sem=ssem.at[peer], recv_sem=rsem.at[me],
                device_id=(peer // 4, (peer // 2) % 2, peer % 2), device_id_type=MESH)
            cp.start()
            cps.append(cp)
        for dlt in range(1, N_DEV):
            src = (me + dlt) % N_DEV
            pltpu.make_async_remote_copy(
                src_ref=g_ref, dst_ref=gath.at[pl.ds(src, 1), :], send_sem=ssem.at[src], recv_sem=rsem.at[src],
                device_id=(x, y, c), device_id_type=MESH).wait_recv()
        for cp in cps:
            cp.wait_send()
        g = gath[0:1, :]
        for j in range(1, N_DEV):
            g = g + gath[j:j + 1, :]
        d, mn, vn = _adamw(w_ref[...], g, m_ref[...], v_ref[...])
        o_ref[0:1, :] = g
        o_ref[1:2, :] = d
        o_ref[2:3, :] = mn
        o_ref[3:4, :] = vn

    vm = pl.BlockSpec(memory_space=pltpu.VMEM)
    return _pcall(
        body, in_specs=[vm] * 4, out_specs=vm, out_shape=SDS((4, _PACK), F32),
        scratch_shapes=[pltpu.VMEM((N_DEV, _PACK), F32), pltpu.SemaphoreType.DMA((N_DEV,)), pltpu.SemaphoreType.DMA((N_DEV,))],
        name="small_sync")(grow, wrow, mrow, vrow)


def _pack_small(d, loss_row=None):
    parts = [d[k].reshape(1, n).astype(F32) for k, _, n in _SMALL]
    parts.append(loss_row if loss_row is not None else jnp.ones((1, LANE), F32))
    return jnp.concatenate(parts, axis=1)


def kernel(x, mem, ln_mix_w, w_in, hg_norm_w, hg_lower_bounds, w_branch_a, w_branch_b, w_out, ln_cross_w, ln_mem_w, wq_cross, wkv_cross, wo_cross, ln_ffn_w, w1, w3, w2, ln_final_w, loss_target, m_ln_mix_w, m_w_in, m_hg_norm_w, m_hg_lower_bounds, m_w_branch_a, m_w_branch_b, m_w_out, m_ln_cross_w, m_ln_mem_w, m_wq_cross, m_wkv_cross, m_wo_cross, m_ln_ffn_w, m_w1, m_w3, m_w2, m_ln_final_w, v_ln_mix_w, v_w_in, v_hg_norm_w, v_hg_lower_bounds, v_w_branch_a, v_w_branch_b, v_w_out, v_ln_cross_w, v_ln_mem_w, v_wq_cross, v_wkv_cross, v_wo_cross, v_ln_ffn_w, v_w1, v_w3, v_w2, v_ln_final_w):
    given = dict(locals())
    place = jnp.stack([lax.axis_index("c"), 2 * lax.axis_index("x") + lax.axis_index("y")]).astype(jnp.int32)

    _ORDER[0] = None
    index = _INDEX

    def local_view(a, i):
        return a[0].T if _MATS[i][6] else a[0]

    shard = {key: local_view(given[ref], i) for i, (key, ref, *_) in enumerate(_MATS)}

    p = {}
    col_half = {"w_in_a": 0, "w_in_b": 1}
    waves = (("in_a", ["w_in_a"], 1), ("in_b", ["w_in_b"], 12), ("mid", ["w_ba", "w_bb", "w_out", "wq", "wkv", "wo"], 2),
             ("w1", ["w1"], 11), ("w3", ["w3"], 15), ("w2", ["w2"], 16))
    for wave, keys, cid in waves:
        units = [_cast_unit(shard[k], "cast_" + k, col_half.get(k)) for k in keys]
        p.update(zip(keys, _all_gather(units, [index[k] for k in keys], cid, "all_gather_" + wave)))
    small_w = {"ln_mix": ln_mix_w, "ln_cross": ln_cross_w, "ln_mem": ln_mem_w, "ln_ffn": ln_ffn_w,
               "ln_final": ln_final_w.reshape(1, D_MODEL), "hg_lb": hg_lower_bounds, "hg_norm": hg_norm_w}
    p.update(small_w)

    groups = {"ffn": ["w2", "w1", "w3"], "cross": ["wo", "wq", "wkv"], "mix": ["w_out", "w_ba", "w_bb"]}
    cids = {"ffn": (3, 4), "cross": (5, 6), "mix": (7, 8)}
    for q in range(IN_PARTS):
        groups[f"in_q{q}"] = [f"w_in_q{q}"]
        cids[f"in_q{q}"] = ((9, 10), (13, 14), (17, 18), (19, 20))[q]
    sib, parts, recv, out = {}, {}, {}, {}

    grads = {}

    def chip_stage(name):
        keys = groups[name]
        parts[name] = [_chip_sum(grads[k], s, index[k], place, "chip_sum_" + k) for k, s in zip(keys, sib[name])]
        recv[name] = list(_rs_chips(parts[name], [index[k] for k in keys], cids[name][1], "rs_chips_" + name))

    def adam(ref, i, part_list, recv_list):
        res = _adam_mat(part_list, recv_list, shard[_MATS[i][0]], local_view(given["m_" + ref], i), local_view(given["v_" + ref], i),
                        place, "adam_" + ref)
        out[ref] = [(a.T if _MATS[i][6] else a)[None] for a in res]

    def adam_stage(name):
        for k, t, r in zip(groups[name], parts[name], recv[name]):
            adam(_MATS[index[k]][1], index[k], [t], [r])

    def ready(name, gw):
        if name == "mixers_done":
            adam_stage("ffn")
            chip_stage("mix")
            adam_stage("cross")
            return
        if name == "dh_half":
            return
        keys = groups[name]
        grads.update({k: gw[k] for k in keys})
        if name.startswith("in_q"):
            if name == "in_q1":
                adam_stage("mix")
            sib[name] = [_rs_sibling_tc(gw[k], index[k], cids[name][0], "rs_sibling_" + k) for k in keys]
            chip_stage(name)
            return
        sib[name] = _rs_sibling([gw[k] for k in keys], [index[k] for k in keys], cids[name][0], "rs_sibling_" + name)
        if name == "cross":
            chip_stage("ffn")
        if name == "mix":
            chip_stage("cross")

    loss_row, grad_x, gs = _local_step(x[0], mem[0], loss_target[0], p, ready)

    sm = {k: given["m_" + ref] for k, ref, _ in _SMALL}
    sv = {k: given["v_" + ref] for k, ref, _ in _SMALL}
    synced = _small_sync(_pack_small(gs, loss_row), _pack_small(small_w), _pack_small(sm), _pack_small(sv))
    in_groups = [f"in_q{q}" for q in range(IN_PARTS)]
    adam("w_in", index["w_in_a"], sum((parts[g] for g in in_groups), []), sum((recv[g] for g in in_groups), []))
    off = 0
    for k, ref, n in _SMALL:
        shape = given[ref].shape
        out[ref] = [synced[r, off:off + n].reshape(shape) for r in range(4)]
        off += n
    loss = synced[0, off]

    order = ["ln_mix_w", "w_in", "hg_norm_w", "hg_lower_bounds", "w_branch_a", "w_branch_b", "w_out", "ln_cross_w", "ln_mem_w",
             "wq_cross", "wkv_cross", "wo_cross", "ln_ffn_w", "w1", "w3", "w2", "ln_final_w"]
    return (loss, grad_x[None], *[out[n][0] for n in order], *[out[n][1] for n in order],
            *[out[n][2] for n in order], *[out[n][3] for n in order])
```

```python
import functools
import math

import jax
import jax.numpy as jnp
from jax import lax
from jax.experimental import pallas as pl
from jax.experimental.pallas import tpu as pltpu
from jax.experimental.pallas import tpu_sc as plsc

F32 = jnp.float32
BF = jnp.bfloat16
SDS = jax.ShapeDtypeStruct

D_MODEL = 2048
SEQ = 2048
HEAD_DIM = 128
MEM_LEN = 256
ATT_GROUPS = ((128, 1), (512, 4), (2048, 16))
ATT_HEADS = 4
ATT_WIDTH = 1536
ATT_OUT = 512
HG_HEADS = 8
HG_WIDTH = 1024
HG_CHUNK = 64
IN_WIDTH = 12800
CROSS_HEADS = 4
CROSS_WIDTH = 512
D_FF = 5632
RMS_EPS = 1e-6
ADAM_LR = 0.001
ADAM_B1 = 0.9
ADAM_B2 = 0.999
ADAM_EPS = 1e-08
ADAM_WD = 0.01
ADAM_STEP = 10
N_DEV = 8

LANE = 128
QA0, KA0, VA0 = 0, 12, 24
QH0, FH0, IH0, GH0 = 36, 44, 52, 60
GA0, GB0 = 68, 84

VMEM_BIG = 56 * 1024 * 1024
MESH = pl.DeviceIdType.MESH


def _cp(vmem=None, **kw):
    if vmem is not None:
        kw["vmem_limit_bytes"] = vmem
    return pltpu.CompilerParams(**kw)


def _pick(n, cands):
    for c in cands:
        if n % c == 0:
            return c
    raise ValueError(f"no tile for {n}")


_ANY = pl.BlockSpec(memory_space=pl.ANY)
_ORDER = [None]


def _pcall(body, *, name, in_specs, out_specs, out_shape, grid=(), scratch_shapes=(), compiler_params=None, prefetch=False,
           aliases=None):
    def run(*args):
        dep = _ORDER[0]
        specs, operands, kernel_fn = list(in_specs), list(args), body
        if dep is not None and not any(dep is a for a in operands):
            at = len(operands)
            specs.append(_ANY)
            operands.append(dep)

            def kernel_fn(*refs):
                return body(*refs[:at], *refs[at + 1:])
        if prefetch:
            call = pl.pallas_call(
                kernel_fn, out_shape=out_shape, name=name, compiler_params=compiler_params,
                grid_spec=pltpu.PrefetchScalarGridSpec(num_scalar_prefetch=1, grid=grid, in_specs=specs, out_specs=out_specs,
                                                       scratch_shapes=scratch_shapes))
        else:
            call = pl.pallas_call(kernel_fn, grid=grid, in_specs=specs, out_specs=out_specs, out_shape=out_shape,
                                  scratch_shapes=scratch_shapes, compiler_params=compiler_params, name=name,
                                  input_output_aliases=aliases or {})
        out = call(*operands)
        _ORDER[0] = out[0] if isinstance(out, (list, tuple)) else out
        return out
    return run


def _mm(a, b, form, out_dtype, name, res=None, tiles=None, a_part=(0, 1), b_part=(0, 1)):
    a_shape = (a.shape[0], a.shape[1] // a_part[1])
    b_shape = (b.shape[0], b.shape[1] // b_part[1])
    if form == "nn":
        (M, K), (K2, N) = a_shape, b_shape
    elif form == "nt":
        (M, K), (N, K2) = a_shape, b_shape
    else:
        (K, M), (K2, N) = a_shape, b_shape
    assert K == K2, (a.shape, b.shape, form)
    if tiles is None:
        tm = _pick(M, (1024, 1280, 1408, 512, 256))
        tn = _pick(N, (1280, 1024, 1408, 512, 256))
        tk = K if K <= 2048 else _pick(K, (D_FF, 2560))
        if tk == D_FF:
            tm = _pick(M, (512, 256))
    else:
        tm, tn, tk = tiles
    nk = K // tk
    ah, bh = a_part[0], b_part[0]
    if form == "nn":
        a_spec = pl.BlockSpec((tm, tk), lambda i, j, k: (i, k + ah * nk))
        b_spec = pl.BlockSpec((tk, tn), lambda i, j, k: (k, j + bh * (N // tn)))
        dims = ((1,), (0,))
    elif form == "nt":
        a_spec = pl.BlockSpec((tm, tk), lambda i, j, k: (i, k + ah * nk))
        b_spec = pl.BlockSpec((tn, tk), lambda i, j, k: (j, k + bh * nk))
        dims = ((1,), (1,))
    else:
        a_spec = pl.BlockSpec((tk, tm), lambda i, j, k: (k, i + ah * (M // tm)))
        b_spec = pl.BlockSpec((tk, tn), lambda i, j, k: (k, j + bh * (N // tn)))
        dims = ((0,), (0,))
    o_spec = pl.BlockSpec((tm, tn), lambda i, j, k: (i, j))
    in_specs = [a_spec, b_spec]
    args = [a, b]
    if res is not None:
        in_specs.append(o_spec)
        args.append(res)

    def body(*refs):
        a_ref, b_ref = refs[0], refs[1]
        r_ref = refs[2] if res is not None else None
        o_ref = refs[3] if res is not None else refs[2]
        acc = refs[-1] if nk > 1 else None
        k = pl.program_id(2)
        d = lax.dot_general(a_ref[...].astype(BF), b_ref[...].astype(BF), (dims, ((), ())), preferred_element_type=F32)

        def finish(r):
            if res is not None:
                r = r + r_ref[...].astype(F32)
            o_ref[...] = r.astype(o_ref.dtype)

        if nk == 1:
            finish(d)
        else:
            @pl.when(k == 0)
            def _():
                acc[...] = d

            @pl.when((k > 0) & (k < nk - 1))
            def _():
                acc[...] += d

            @pl.when(k == nk - 1)
            def _():
                finish(acc[...] + d)

    return _pcall(
        body, grid=(M // tm, N // tn, nk), in_specs=in_specs, out_specs=o_spec,
        out_shape=SDS((M, N), out_dtype), scratch_shapes=[pltpu.VMEM((tm, tn), F32)] if nk > 1 else [],
        compiler_params=_cp(VMEM_BIG, dimension_semantics=("parallel", "parallel", "arbitrary")),
        name=name)(*args)


def _rms(x, w):
    return x * lax.rsqrt(jnp.mean(x * x, axis=-1, keepdims=True) + RMS_EPS) * w


def _rms_fwd(x, w, name):
    R, D = x.shape
    tr = 256

    def body(x_ref, w_ref, o_ref):
        o_ref[...] = _rms(x_ref[...], w_ref[...]).astype(o_ref.dtype)

    return _pcall(
        body, grid=(R // tr,),
        in_specs=[pl.BlockSpec((tr, D), lambda i: (i, 0)), pl.BlockSpec((1, D), lambda i: (0, 0))],
        out_specs=pl.BlockSpec((tr, D), lambda i: (i, 0)), out_shape=SDS((R, D), BF), name=name)(x, w)


def _rms_bwd(x, w, dh, dres, name):
    R, D = x.shape
    tr = 256
    want_dx = dres is not None
    dhs = dh if isinstance(dh, (tuple, list)) else (dh,)
    nd = len(dhs)

    def body(*refs):
        x_ref, w_ref, dh_refs = refs[0], refs[1], refs[2:2 + nd]
        if want_dx:
            dr_ref, dx_ref, dxb_ref, dw_ref = refs[2 + nd:]
        else:
            dw_ref = refs[2 + nd]

        @pl.when(pl.program_id(0) == 0)
        def _():
            dw_ref[...] = jnp.zeros_like(dw_ref)

        dhv = jnp.concatenate([r[...].astype(F32) for r in dh_refs], axis=1) if nd > 1 else dh_refs[0][...].astype(F32)
        _, vjp = jax.vjp(_rms, x_ref[...], w_ref[...])
        dx, dw = vjp(dhv)
        dw_ref[...] += dw
        if want_dx:
            dx = dx + dr_ref[...]
            dx_ref[...] = dx
            dxb_ref[...] = dx.astype(BF)

    row = pl.BlockSpec((tr, D), lambda i: (i, 0))
    part = pl.BlockSpec((tr, D // nd), lambda i: (i, 0))
    vec = pl.BlockSpec((1, D), lambda i: (0, 0))
    if want_dx:
        return _pcall(body, grid=(R // tr,), in_specs=[row, vec] + [part] * nd + [row], out_specs=[row, row, vec],
                      out_shape=[SDS((R, D), F32), SDS((R, D), BF), SDS((1, D), F32)], name=name)(x, w, *dhs, dres)
    return _pcall(body, grid=(R // tr,), in_specs=[row, vec] + [part] * nd, out_specs=vec,
                  out_shape=SDS((1, D), F32), name=name)(x, w, *dhs)


def _final_loss(x3, w, tgt):
    R, D = x3.shape
    tr = 256

    def loss_fn(xv, wv, tv):
        err = _rms(xv, wv) - tv
        return 0.5 * jnp.sum(jnp.mean(err * err, axis=-1))

    def body(x_ref, w_ref, t_ref, loss_ref, dx_ref, dxb_ref, dw_ref):
        @pl.when(pl.program_id(0) == 0)
        def _():
            dw_ref[...] = jnp.zeros_like(dw_ref)
            loss_ref[...] = jnp.zeros_like(loss_ref)

        tv = t_ref[...]
        val, vjp = jax.vjp(lambda a, b: loss_fn(a, b, tv), x_ref[...], w_ref[...])
        dx, dw = vjp(jnp.ones((), F32))
        dx_ref[...] = dx
        dxb_ref[...] = dx.astype(BF)
        dw_ref[...] += dw
        loss_ref[...] += jnp.full(loss_ref.shape, val, F32)

    row = pl.BlockSpec((tr, D), lambda i: (i, 0))
    vec = pl.BlockSpec((1, D), lambda i: (0, 0))
    one = pl.BlockSpec((1, LANE), lambda i: (0, 0))
    return _pcall(body, grid=(R // tr,), in_specs=[row, vec, row], out_specs=[one, row, row, vec],
                          out_shape=[SDS((1, LANE), F32), SDS((R, D), F32), SDS((R, D), BF), SDS((1, D), F32)],
                          name="final_loss")(x3, w, tgt)


NEG = -1e30


def _att_blocks(S, d):
    L = S // d
    nb = L // LANE
    return nb, (2 if nb > 1 else 1)


def _att_rows(S, d, idx):
    nb, nkb = _att_blocks(S, d)
    r = idx // nb
    n = idx % nb
    kb = jnp.maximum(n - 1, 0)
    if d == 1:
        qrows = pl.ds(pl.multiple_of(n * LANE, LANE), LANE)
        krows = pl.ds(pl.multiple_of(kb * LANE, LANE), LANE * nkb)
    else:
        qrows = pl.ds(r + n * (LANE * d), LANE, stride=d)
        krows = pl.ds(r + kb * (LANE * d), LANE * nkb, stride=d)
    qpos = n * LANE + lax.broadcasted_iota(jnp.int32, (LANE, LANE * nkb), 0)
    kpos = kb * LANE + lax.broadcasted_iota(jnp.int32, (LANE, LANE * nkb), 1)
    return qrows, krows, qpos - kpos


def _slab(col):
    return pl.BlockSpec((SEQ, LANE), lambda h, col=col: (0, col + h))


def _attn_fwd(proj):
    S = proj.shape[0]
    scale = HEAD_DIM ** -0.5

    def body(q0, q1, q2, k0, k1, k2, v0, v1, v2, o_ref, lse_ref, og, lg):
        qs, ks, vs = (q0, q1, q2), (k0, k1, k2), (v0, v1, v2)
        for g, (window, d) in enumerate(ATT_GROUPS):
            def blk(idx, carry, g=g, window=window, d=d):
                qrows, krows, dist = _att_rows(S, d, idx)
                q = qs[g][qrows, :].astype(BF)
                k = ks[g][krows, :].astype(BF)
                v = vs[g][krows, :].astype(BF)
                s = lax.dot_general(q, k, (((1,), (1,)), ((), ())), preferred_element_type=F32) * scale
                s = jnp.where((dist >= 0) & (dist <= window // d), s, NEG)
                m = jnp.max(s, axis=-1, keepdims=True)
                p = jnp.exp(s - m)
                l = jnp.sum(p, axis=-1, keepdims=True)
                o = jnp.dot((p / l).astype(BF), v, preferred_element_type=F32)
                og[g, qrows, :] = o
                lg[g, qrows, :] = jnp.broadcast_to(m + jnp.log(l), (LANE, LANE))
                return carry
            lax.fori_loop(0, S // LANE, blk, 0, unroll=8)

        def merge(t, carry):
            rows = pl.ds(pl.multiple_of(t * 256, 256), 256)
            l0, l1, l2 = lg[0, rows, :], lg[1, rows, :], lg[2, rows, :]
            m = jnp.maximum(jnp.maximum(l0, l1), l2)
            w0, w1, w2 = jnp.exp(l0 - m), jnp.exp(l1 - m), jnp.exp(l2 - m)
            den = w0 + w1 + w2
            o_ref[rows, :] = (w0 * og[0, rows, :] + w1 * og[1, rows, :] + w2 * og[2, rows, :]) / den
            lse_ref[rows, :] = m + jnp.log(den)
            return carry
        lax.fori_loop(0, S // 256, merge, 0)

    ins = [_slab(QA0 + 4 * g) for g in range(3)] + [_slab(KA0 + 4 * g) for g in range(3)] + [_slab(VA0 + 4 * g) for g in range(3)]
    out = pl.BlockSpec((SEQ, LANE), lambda h: (0, h))
    return _pcall(
        body, grid=(ATT_HEADS,), in_specs=ins, out_specs=[out, out],
        out_shape=[SDS((S, ATT_OUT), F32), SDS((S, ATT_OUT), F32)],
        scratch_shapes=[pltpu.VMEM((3, S, LANE), F32), pltpu.VMEM((3, S, LANE), F32)],
        compiler_params=_cp(VMEM_BIG), name="attn_fwd")(*([proj] * 9))


def _attn_bwd(proj, o_att, lse, do_att, dproj):
    S = proj.shape[0]
    scale = HEAD_DIM ** -0.5

    def body(q0, q1, q2, k0, k1, k2, v0, v1, v2, o_ref, lse_ref, do_ref, dp_in, dp_ref, dqa, dka, dva, delta, stage, sem):
        qs, ks, vs = (q0, q1, q2), (k0, k1, k2), (v0, v1, v2)
        head = pl.program_id(0)

        def slab_copies(g):
            return [pltpu.make_async_copy(stage.at[t], dp_ref.at[:, pl.ds(pl.multiple_of((base + 4 * g + head) * LANE, LANE), LANE)],
                                          sem.at[t]) for t, base in enumerate((QA0, KA0, VA0))]

        delta[...] = jnp.broadcast_to(jnp.sum(do_ref[...] * o_ref[...], axis=-1, keepdims=True), delta.shape)
        for g, (window, d) in enumerate(ATT_GROUPS):
            dka[...] = jnp.zeros_like(dka)
            dva[...] = jnp.zeros_like(dva)

            def blk(idx, carry, g=g, window=window, d=d):
                qrows, krows, dist = _att_rows(S, d, idx)
                q = qs[g][qrows, :].astype(BF)
                k = ks[g][krows, :].astype(BF)
                v = vs[g][krows, :].astype(BF)
                do = do_ref[qrows, :]
                s = lax.dot_general(q, k, (((1,), (1,)), ((), ())), preferred_element_type=F32) * scale
                p = jnp.exp(s - lse_ref[qrows, :][:, 0:1])
                p = jnp.where((dist >= 0) & (dist <= window // d), p, 0.0)
                dob = do.astype(BF)
                dva[krows, :] += lax.dot_general(p.astype(BF), dob, (((0,), (0,)), ((), ())), preferred_element_type=F32)
                dp = lax.dot_general(dob, v, (((1,), (1,)), ((), ())), preferred_element_type=F32)
                ds = (p * (dp - delta[qrows, :][:, 0:1]) * scale).astype(BF)
                dqa[qrows, :] = jnp.dot(ds, k, preferred_element_type=F32)
                dka[krows, :] += lax.dot_general(ds, q, (((0,), (0,)), ((), ())), preferred_element_type=F32)
                return carry
            lax.fori_loop(0, S // LANE, blk, 0, unroll=8)
            if g > 0:
                for cp in slab_copies(g - 1):
                    cp.wait()
            stage[0] = dqa[...].astype(BF)
            stage[1] = dka[...].astype(BF)
            stage[2] = dva[...].astype(BF)
            for cp in slab_copies(g):
                cp.start()
        for cp in slab_copies(len(ATT_GROUPS) - 1):
            cp.wait()

    cols = [QA0 + 4 * g for g in range(3)] + [KA0 + 4 * g for g in range(3)] + [VA0 + 4 * g for g in range(3)]
    ins = [_slab(c) for c in cols]
    one = pl.BlockSpec((SEQ, LANE), lambda h: (0, h))
    return _pcall(
        body, grid=(ATT_HEADS,), in_specs=ins + [one, one, one, _ANY], out_specs=_ANY,
        out_shape=SDS((S, IN_WIDTH), BF), aliases={12: 0},
        scratch_shapes=[pltpu.VMEM((S, LANE), F32)] * 4 + [pltpu.VMEM((3, S, LANE), BF), pltpu.SemaphoreType.DMA((3,))],
        compiler_params=_cp(VMEM_BIG), name="attn_bwd")(*([proj] * 9), o_att, lse, do_att, dproj)


def _bdot(a, b, dims):
    return lax.dot_general(a.astype(BF), b.astype(BF), (dims, ((), ())), preferred_element_type=F32)


def _hg_chunk(qh, fh, ih, gh, lbraw, nw, st):
    C = HG_CHUNK
    mx = jnp.maximum(lbraw[0:1], lbraw[1:2])
    e0 = jnp.exp(lbraw[0:1] - mx)
    e1 = jnp.exp(lbraw[1:2] - mx)
    lb = e0 / (e0 + e1)
    f = lb + (1.0 - lb) * jax.nn.sigmoid(fh)
    logf = jnp.log(f)
    k = 1.0 - f
    q = qh * jax.nn.sigmoid(qh)
    causal = lax.broadcasted_iota(jnp.int32, (C, C), 0) >= lax.broadcasted_iota(jnp.int32, (C, C), 1)
    b = lax.dot_general(causal.astype(F32), logf, (((1,), (0,)), ((), ())), precision=lax.Precision.HIGHEST,
                        preferred_element_type=F32)
    b_last = b[C - 1:C]
    st_out = st * jnp.exp(b_last) + _bdot(ih, k * jnp.exp(b_last - b), ((0,), (0,)))
    inter = _bdot(q * jnp.exp(b), st, ((1,), (1,)))
    b_ref = b[C // 2:C // 2 + 1]
    a = _bdot(q * jnp.exp(b - b_ref), k * jnp.exp(b_ref - b), ((1,), (1,)))
    a = jnp.where(causal, a, 0.0)
    o = inter + _bdot(a, ih, ((1,), (0,)))
    o = o * lax.rsqrt(jnp.mean(o * o, axis=-1, keepdims=True) + RMS_EPS) * nw
    return o * (gh * jax.nn.sigmoid(gh)), st_out


def _to_heads(x):
    return jnp.concatenate([x[None, :, h * LANE:(h + 1) * LANE] for h in range(HG_HEADS)], axis=0)


def _from_heads(x):
    return jnp.concatenate([x[h] for h in range(HG_HEADS)], axis=1)


def _hdot(a, b, ca, cb):
    return lax.dot_general(a.astype(BF), b.astype(BF), (((ca,), (cb,)), ((0,), (0,))), preferred_element_type=F32)


def _hg_chunk_all(qh, fh, ih, gh, lbraw, nw, st):
    C = HG_CHUNK
    mx = jnp.maximum(lbraw[0:1], lbraw[1:2])
    e0 = jnp.exp(lbraw[0:1] - mx)
    e1 = jnp.exp(lbraw[1:2] - mx)
    lb = e0 / (e0 + e1)
    f = lb + (1.0 - lb) * jax.nn.sigmoid(fh)
    logf = jnp.log(f)
    k = 1.0 - f
    q = qh * jax.nn.sigmoid(qh)
    causal = lax.broadcasted_iota(jnp.int32, (C, C), 0) >= lax.broadcasted_iota(jnp.int32, (C, C), 1)
    b = lax.dot_general(causal.astype(F32), logf, (((1,), (0,)), ((), ())), precision=lax.Precision.HIGHEST,
                        preferred_element_type=F32)
    b_last = b[C - 1:C]
    b_ref = b[C // 2:C // 2 + 1]
    v3 = _to_heads(ih)
    st_out = st * _to_heads(jnp.exp(b_last)) + _hdot(v3, _to_heads(k * jnp.exp(b_last - b)), 1, 1)
    inter = _hdot(_to_heads(q * jnp.exp(b)), st, 2, 2)
    a = _hdot(_to_heads(q * jnp.exp(b - b_ref)), _to_heads(k * jnp.exp(b_ref - b)), 2, 2)
    a = jnp.where(causal[None], a, 0.0)
    o = inter + _hdot(a, v3, 2, 1)
    o = o * lax.rsqrt(jnp.mean(o * o, axis=-1, keepdims=True) + RMS_EPS) * nw
    return _from_heads(o) * (gh * jax.nn.sigmoid(gh)), st_out


def _head_cols(h):
    return slice(h * LANE, (h + 1) * LANE)


def _hg_in_specs(chunk_of):
    half = 4 * LANE
    specs = []
    for base in (QH0, FH0, IH0, GH0):
        for part in range(2):
            specs.append(pl.BlockSpec((HG_CHUNK, half), lambda n, col=base // 4 + part: (chunk_of(n), col)))
    return specs


def _hg_sections(sec):
    return [jnp.concatenate([sec[2 * s][...], sec[2 * s + 1][...]], axis=1) for s in range(4)]


def _hg_fwd(proj, lb, nw):
    S = proj.shape[0]
    C = HG_CHUNK
    NC = S // C

    def body(*refs):
        sec = refs[:8]
        lb_ref, nw_ref, o_ref, st_ref, st_scr = refs[8:]

        @pl.when(pl.program_id(0) == 0)
        def _():
            st_scr[...] = jnp.zeros_like(st_scr)

        st = st_scr[...]
        st_ref[:, 0] = st
        o, st_out = _hg_chunk_all(*_hg_sections(sec), lb_ref[...], nw_ref[...], st)
        o_ref[...] = o.astype(o_ref.dtype)
        st_scr[...] = st_out

    return _pcall(
        body, grid=(NC,),
        in_specs=_hg_in_specs(lambda n: n) + [pl.BlockSpec((2, HG_WIDTH), lambda n: (0, 0)), pl.BlockSpec((1, LANE), lambda n: (0, 0))],
        out_specs=[pl.BlockSpec((C, HG_WIDTH), lambda n: (n, 0)), pl.BlockSpec((HG_HEADS, 1, LANE, LANE), lambda n: (0, n, 0, 0))],
        out_shape=[SDS((S, HG_WIDTH), BF), SDS((HG_HEADS, NC, LANE, LANE), F32)],
        scratch_shapes=[pltpu.VMEM((HG_HEADS, LANE, LANE), F32)], name="hg_fwd")(*([proj] * 8), lb, nw)


def _hg_bwd(proj, lb, nw, states, do_hg, dproj):
    S = proj.shape[0]
    C = HG_CHUNK
    NC = S // C

    def body(*refs):
        sec = refs[:8]
        lb_ref, nw_ref, st_ref, do_ref, dp_in, dp_ref, dlb_ref, dnw_ref, dst_scr, stage, sem = refs[8:]
        n = pl.program_id(0)
        slot = n % 2

        def chunk_copies(s):
            rows = pl.ds(pl.multiple_of((NC - 1 - n) * C, C), C)
            return [pltpu.make_async_copy(stage.at[s, t], dp_ref.at[rows, pl.ds((QH0 + 8 * t) * LANE, HG_WIDTH)], sem.at[s, t])
                    for t in range(4)]

        @pl.when(n == 0)
        def _():
            dst_scr[...] = jnp.zeros_like(dst_scr)
            dlb_ref[...] = jnp.zeros_like(dlb_ref)
            dnw_ref[...] = jnp.zeros_like(dnw_ref)

        _, vjp = jax.vjp(_hg_chunk_all, *_hg_sections(sec), lb_ref[...], nw_ref[...], st_ref[:, 0])
        dq, df, di, dg, dlb, dnw, dst = vjp((do_ref[...].astype(F32), dst_scr[...]))
        dlb_ref[...] += dlb
        dnw_ref[...] += dnw
        dst_scr[...] = dst

        @pl.when(n >= 2)
        def _():
            for cp in chunk_copies(slot):
                cp.wait()

        for t, val in enumerate((dq, df, di, dg)):
            stage[slot, t] = val.astype(BF)
        for cp in chunk_copies(slot):
            cp.start()

        @pl.when(n == NC - 1)
        def _():
            for cp in chunk_copies(1 - slot) + chunk_copies(slot):
                cp.wait()

    rev = lambda n: NC - 1 - n
    ospec = pl.BlockSpec((C, HG_WIDTH), lambda n: (rev(n), 0))
    lbs = pl.BlockSpec((2, HG_WIDTH), lambda n: (0, 0))
    nws = pl.BlockSpec((1, LANE), lambda n: (0, 0))
    outs = _pcall(
        body, grid=(NC,),
        in_specs=_hg_in_specs(rev) + [lbs, nws, pl.BlockSpec((HG_HEADS, 1, LANE, LANE), lambda n: (0, rev(n), 0, 0)), ospec, _ANY],
        out_specs=[_ANY, lbs, nws],
        out_shape=[SDS((S, IN_WIDTH), BF), SDS((2, HG_WIDTH), F32), SDS((1, LANE), F32)], aliases={12: 0},
        scratch_shapes=[pltpu.VMEM((HG_HEADS, LANE, LANE), F32), pltpu.VMEM((2, 4, C, HG_WIDTH), BF), pltpu.SemaphoreType.DMA((2, 4))],
        name="hg_bwd")(*([proj] * 8), lb, nw, states, do_hg, dproj)
    return outs


def _merge(ga, gb, ba, bb):
    return jax.nn.sigmoid(ga) * ba + jax.nn.sigmoid(gb) * bb


_MT = 512


def _gate_specs():
    ga = pl.BlockSpec((_MT, _MT), lambda i, j: (i, GA0 * LANE // _MT + j))
    gb = pl.BlockSpec((_MT, _MT), lambda i, j: (i, GB0 * LANE // _MT + j))
    t = pl.BlockSpec((_MT, _MT), lambda i, j: (i, j))
    return ga, gb, t


def _merge_fwd(proj, ba, bb):
    S, D = ba.shape
    ga, gb, t = _gate_specs()

    def body(ga_ref, gb_ref, ba_ref, bb_ref, o_ref):
        o_ref[...] = _merge(ga_ref[...], gb_ref[...], ba_ref[...].astype(F32), bb_ref[...].astype(F32)).astype(o_ref.dtype)

    return _pcall(body, grid=(S // _MT, D // _MT), in_specs=[ga, gb, t, t], out_specs=t,
                          out_shape=SDS((S, D), BF), name="merge_fwd")(proj, proj, ba, bb)


def _merge_bwd(proj, ba, bb, dm):
    S, D = ba.shape
    ga, gb, t = _gate_specs()

    def body(ga_ref, gb_ref, ba_ref, bb_ref, dm_ref, dba_ref, dbb_ref, dp_ref, stage, sem):
        _, vjp = jax.vjp(_merge, ga_ref[...], gb_ref[...], ba_ref[...].astype(F32), bb_ref[...].astype(F32))
        dga, dgb, dba, dbb = vjp(dm_ref[...].astype(F32))
        dba_ref[...] = dba.astype(BF)
        dbb_ref[...] = dbb.astype(BF)
        stage[0] = dga.astype(BF)
        stage[1] = dgb.astype(BF)
        rows = pl.ds(pl.multiple_of(pl.program_id(0) * _MT, _MT), _MT)
        cps = []
        for k, base in enumerate((GA0, GB0)):
            cols = pl.ds(pl.multiple_of(base * LANE + pl.program_id(1) * _MT, LANE), _MT)
            cps.append(pltpu.make_async_copy(stage.at[k], dp_ref.at[rows, cols], sem.at[k]))
            cps[-1].start()
        for cp in cps:
            cp.wait()

    return _pcall(body, grid=(S // _MT, D // _MT), in_specs=[ga, gb, t, t, t], out_specs=[t, t, _ANY],
                  out_shape=[SDS((S, D), BF), SDS((S, D), BF), SDS((S, IN_WIDTH), BF)],
                  scratch_shapes=[pltpu.VMEM((2, _MT, _MT), BF), pltpu.SemaphoreType.DMA((2,))],
                  name="merge_bwd")(proj, proj, ba, bb, dm)


def _swiglu(a, b):
    return a * jax.nn.sigmoid(a) * b


def _swiglu_fwd(a, b):
    S, F = a.shape
    tf = _pick(F, (1408, 512))
    t = pl.BlockSpec((_MT, tf), lambda i, j: (i, j))

    def body(a_ref, b_ref, o_ref):
        o_ref[...] = _swiglu(a_ref[...].astype(F32), b_ref[...].astype(F32)).astype(o_ref.dtype)

    return _pcall(body, grid=(S // _MT, F // tf), in_specs=[t, t], out_specs=t,
                          out_shape=SDS((S, F), BF), name="swiglu_fwd")(a, b)


def _swiglu_bwd(a, b, dg):
    S, F = a.shape
    tf = _pick(F, (1408, 512))
    t = pl.BlockSpec((_MT, tf), lambda i, j: (i, j))

    def body(a_ref, b_ref, dg_ref, da_ref, db_ref):
        _, vjp = jax.vjp(_swiglu, a_ref[...].astype(F32), b_ref[...].astype(F32))
        da, db = vjp(dg_ref[...].astype(F32))
        da_ref[...] = da.astype(BF)
        db_ref[...] = db.astype(BF)

    return _pcall(body, grid=(S // _MT, F // tf), in_specs=[t, t, t], out_specs=[t, t],
                          out_shape=[SDS((S, F), BF)] * 2, name="swiglu_bwd")(a, b, dg)


def _cross(q, k, v):
    s = _bdot(q, k, ((1,), (1,))) * (HEAD_DIM ** -0.5)
    e = jnp.exp(s - jnp.max(s, axis=-1, keepdims=True))
    p = e / jnp.sum(e, axis=-1, keepdims=True)
    return _bdot(p, v, ((1,), (0,)))


def _cross_specs():
    q = pl.BlockSpec((_MT, LANE), lambda h, i: (i, h))
    k = pl.BlockSpec((MEM_LEN, LANE), lambda h, i: (0, h))
    v = pl.BlockSpec((MEM_LEN, LANE), lambda h, i: (0, CROSS_HEADS + h))
    return q, k, v


def _cross_fwd(qc, kvc):
    S = qc.shape[0]
    q, k, v = _cross_specs()

    def body(q_ref, k_ref, v_ref, o_ref):
        o_ref[...] = _cross(q_ref[...], k_ref[...], v_ref[...]).astype(o_ref.dtype)

    return _pcall(body, grid=(CROSS_HEADS, S // _MT), in_specs=[q, k, v], out_specs=q,
                          out_shape=SDS((S, CROSS_WIDTH), BF), name="cross_fwd")(qc, kvc, kvc)


def _cross_bwd(qc, kvc, doc):
    S = qc.shape[0]
    q, k, v = _cross_specs()

    def body(q_ref, k_ref, v_ref, do_ref, dq_ref, dk_ref, dv_ref):
        @pl.when(pl.program_id(1) == 0)
        def _():
            dk_ref[...] = jnp.zeros_like(dk_ref)
            dv_ref[...] = jnp.zeros_like(dv_ref)

        _, vjp = jax.vjp(_cross, q_ref[...], k_ref[...], v_ref[...])
        dq, dk, dv = vjp(do_ref[...])
        dq_ref[...] = dq.astype(dq_ref.dtype)
        dk_ref[...] += dk
        dv_ref[...] += dv

    return _pcall(body, grid=(CROSS_HEADS, S // _MT), in_specs=[q, k, v, q], out_specs=[q, k, k],
                          out_shape=[SDS((S, CROSS_WIDTH), BF), SDS((MEM_LEN, CROSS_WIDTH), F32), SDS((MEM_LEN, CROSS_WIDTH), F32)],
                          name="cross_bwd")(qc, kvc, kvc, doc)


def _local_step(x, mem, tgt, p, ready):
    h = _rms_fwd(x, p["ln_mix"], "rms_mix")
    proj = _mm(h, p["w_in_a"], "nt", F32, "mm_proj_a", a_part=(0, 2))
    proj = _mm(h, p["w_in_b"], "nt", F32, "mm_proj_b", a_part=(1, 2), res=proj)
    o_att, lse = _attn_fwd(proj)
    o_hg, states = _hg_fwd(proj, p["hg_lb"], p["hg_norm"])
    ba = _mm(o_att, p["w_ba"], "nn", BF, "mm_ba")
    bb = _mm(o_hg, p["w_bb"], "nn", BF, "mm_bb")
    merged = _merge_fwd(proj, ba, bb)
    x1 = _mm(merged, p["w_out"], "nn", F32, "mm_out", res=x)
    hc = _rms_fwd(x1, p["ln_cross"], "rms_cross")
    qc = _mm(hc, p["wq"], "nn", F32, "mm_q")
    mn = _rms_fwd(mem, p["ln_mem"], "rms_mem")
    kvc = _mm(mn, p["wkv"], "nn", F32, "mm_kv")
    oc = _cross_fwd(qc, kvc)
    x2 = _mm(oc, p["wo"], "nn", F32, "mm_o", res=x1)
    hf = _rms_fwd(x2, p["ln_ffn"], "rms_ffn")
    a = _mm(hf, p["w1"], "nt", BF, "mm_w1")
    b = _mm(hf, p["w3"], "nt", BF, "mm_w3")
    g = _swiglu_fwd(a, b)
    x3 = _mm(g, p["w2"], "nn", F32, "mm_w2", res=x2)
    loss, dx3, dx3b, d_ln_final = _final_loss(x3, p["ln_final"], tgt)

    gw, gs = {}, {"ln_final": d_ln_final}
    dg = _mm(dx3b, p["w2"], "nt", BF, "mm_dg")
    gw["w2"] = _mm(g, dx3b, "tn", BF, "mm_dw2")
    da, db = _swiglu_bwd(a, b, dg)
    dhf = _mm(da, p["w1"], "nn", F32, "mm_dhf1")
    dhf = _mm(db, p["w3"], "nn", F32, "mm_dhf3", res=dhf)
    gw["w1"] = _mm(da, hf, "tn", BF, "mm_dw1")
    gw["w3"] = _mm(db, hf, "tn", BF, "mm_dw3")
    ready("ffn", gw)
    dx2, dx2b, gs["ln_ffn"] = _rms_bwd(x2, p["ln_ffn"], dhf, dx3, "rms_ffn_bwd")
    doc = _mm(dx2b, p["wo"], "nt", F32, "mm_doc")
    gw["wo"] = _mm(oc, dx2b, "tn", BF, "mm_dwo")
    dqc, dk, dv = _cross_bwd(qc, kvc, doc)
    dkvc = jnp.concatenate([dk, dv], axis=1)
    dhc = _mm(dqc, p["wq"], "nt", F32, "mm_dhc")
    gw["wq"] = _mm(hc, dqc, "tn", BF, "mm_dwq")
    dmn = _mm(dkvc, p["wkv"], "nt", F32, "mm_dmn")
    gw["wkv"] = _mm(mn, dkvc, "tn", BF, "mm_dwkv")
    ready("cross", gw)
    gs["ln_mem"] = _rms_bwd(mem, p["ln_mem"], dmn, None, "rms_mem_bwd")
    dx1, dx1b, gs["ln_cross"] = _rms_bwd(x1, p["ln_cross"], dhc, dx2, "rms_cross_bwd")
    dmerged = _mm(dx1b, p["w_out"], "nt", BF, "mm_dmerged")
    gw["w_out"] = _mm(merged, dx1b, "tn", BF, "mm_dwout")
    dba, dbb, dproj = _merge_bwd(proj, ba, bb, dmerged)
    do_att = _mm(dba, p["w_ba"], "nt", F32, "mm_doatt")
    gw["w_ba"] = _mm(o_att, dba, "tn", BF, "mm_dwba")
    do_hg = _mm(dbb, p["w_bb"], "nt", F32, "mm_dohg")
    gw["w_bb"] = _mm(o_hg, dbb, "tn", BF, "mm_dwbb")
    ready("mix", gw)
    dproj = _attn_bwd(proj, o_att, lse, do_att, dproj)
    dproj, gs["hg_lb"], gs["hg_norm"] = _hg_bwd(proj, p["hg_lb"], p["hg_norm"], states, do_hg, dproj)
    ready("mixers_done", gw)
    for q in range(IN_PARTS):
        gw[f"w_in_q{q}"] = _mm(dproj, h, "tn", BF, f"mm_dwin_q{q}", b_part=(q, IN_PARTS))
        ready(f"in_q{q}", gw)
    dh_a = _mm(dproj, p["w_in_a"], "nn", F32, "mm_dh_a")
    ready("dh_half", gw)
    dh = (dh_a, _mm(dproj, p["w_in_b"], "nn", F32, "mm_dh_b"))
    grad_x, _, gs["ln_mix"] = _rms_bwd(x, p["ln_mix"], dh, dx1, "rms_mix_bwd")
    return loss, grad_x, gs


_MATS = (
    ("w_in_a", "w_in", IN_WIDTH, D_MODEL // 2, "row", IN_WIDTH // N_DEV, True),
    ("w_in_b", "w_in", IN_WIDTH, D_MODEL // 2, "row", IN_WIDTH // N_DEV, True),
    ("w_ba", "w_branch_a", ATT_OUT, D_MODEL, "col", D_MODEL // N_DEV, False),
    ("w_bb", "w_branch_b", HG_WIDTH, D_MODEL, "col", D_MODEL // N_DEV, False),
    ("w_out", "w_out", D_MODEL, D_MODEL, "row", D_MODEL // N_DEV, False),
    ("wq", "wq_cross", D_MODEL, CROSS_WIDTH, "row", D_MODEL // N_DEV, False),
    ("wkv", "wkv_cross", D_MODEL, 2 * CROSS_WIDTH, "row", D_MODEL // N_DEV, False),
    ("wo", "wo_cross", CROSS_WIDTH, D_MODEL, "col", D_MODEL // N_DEV, False),
    ("w1", "w1", D_FF, D_MODEL, "row", D_FF // N_DEV, True),
    ("w3", "w3", D_FF, D_MODEL, "row", D_FF // N_DEV, True),
    ("w2", "w2", D_FF, D_MODEL, "row", D_FF // N_DEV, False),
    ("w_in_q0", "w_in", IN_WIDTH, D_MODEL // 2, "row", IN_WIDTH // N_DEV, True),
    ("w_in_q1", "w_in", IN_WIDTH, D_MODEL // 2, "row", IN_WIDTH // N_DEV, True),
)
IN_PARTS = 2
_INDEX = {m[0]: i for i, m in enumerate(_MATS)}


def _unit_shape(i):
    _, _, K, N, kind, sh, _ = _MATS[i]
    return (sh, N) if kind == "row" else (K, sh)


def _win(ref, i, chip, par, half=None):
    _, _, K, _, kind, sh, _ = _MATS[i]
    if kind == "row":
        rows, start = (sh, 0) if half is None else (sh // 2, half * (sh // 2))
        return ref.at[pl.ds(pl.multiple_of((2 * chip + par) * sh + start, 16), rows), :]
    rows, start = (K, 0) if half is None else (K // 2, half * (K // 2))
    return ref.at[pl.ds(start, rows), pl.ds(pl.multiple_of((2 * chip + par) * sh, LANE), sh)]


def _my_place():
    x, y, c = lax.axis_index("x"), lax.axis_index("y"), lax.axis_index("c")
    chips = [(1 - x, y), (x, 1 - y), (1 - x, 1 - y)]
    return x, y, c, chips


def _cast_unit(w, name, half=None):
    K, N = w.shape
    if half is not None:
        N = N // 2
    tr = _pick(K, (256, 400, 352))

    def body(w_ref, o_ref):
        o_ref[...] = w_ref[...].astype(BF)

    src = pl.BlockSpec((tr, N), lambda i: (i, half or 0))
    dst = pl.BlockSpec((tr, N), lambda i: (i, 0))
    return _pcall(body, grid=(K // tr,), in_specs=[src], out_specs=dst, out_shape=SDS((K, N), BF), name=name)(w)


def _handshake(peers):
    barrier = pltpu.get_barrier_semaphore()
    for peer in peers:
        pl.semaphore_signal(barrier, inc=1, device_id=peer, device_id_type=MESH)
    pl.semaphore_wait(barrier, len(peers))


def _sequencer(body, out_type, sems, cid, name):
    return pl.kernel(body, out_type=out_type, mesh=plsc.ScalarSubcoreMesh(axis_name="seq", num_cores=1),
                     scratch_types=sems, compiler_params=pltpu.CompilerParams(collective_id=cid), name=name)


def _all_gather(units, idx, cid, name):
    n = len(idx)

    def body(*refs):
        us, fulls = refs[:n], refs[n:2 * n]
        ssem, rsem, lsem = refs[2 * n:]
        x, y, c, _ = _my_place()
        xn, yn, dg = (1 - x, y), (x, 1 - y), (1 - x, 1 - y)
        me, sib = (x, y, c), (x, y, 1 - c)
        _handshake([sib, (*xn, c), (*yn, c)])

        def win(i, chip_xy, par, half=None):
            return _win(fulls[i], idx[i], 2 * chip_xy[0] + chip_xy[1], par, half)

        def rcopy(i, k, src, dst, to):
            return pltpu.make_async_remote_copy(src_ref=src, dst_ref=dst, send_sem=ssem.at[i, k], recv_sem=rsem.at[i, k],
                                                device_id=to, device_id_type=MESH)

        def landed(i, k, blk):
            rcopy(i, k, blk, blk, me).wait_recv()

        sends, locs = [], []

        def send(i, k, blk, to, src=None):
            cp = rcopy(i, k, blk if src is None else src, blk, to)
            cp.start()
            sends.append(cp)

        for i in range(n):
            mine = win(i, (x, y), c)
            loc = pltpu.make_async_copy(us[i], mine, lsem.at[i])
            loc.start()
            locs.append(loc)
            send(i, 0, mine, sib, src=us[i])
            send(i, 1, mine, (*xn, c), src=us[i])
            send(i, 2, mine, (*yn, c), src=us[i])
        for i in range(n):
            landed(i, 1, win(i, xn, c))
            send(i, 3, win(i, xn, c, 0), (*yn, c))
            send(i, 5, win(i, xn, c), sib)
            landed(i, 2, win(i, yn, c))
            send(i, 4, win(i, yn, c, 1), (*xn, c))
            send(i, 6, win(i, yn, c), sib)
        for i in range(n):
            landed(i, 3, win(i, dg, c, 0))
            landed(i, 4, win(i, dg, c, 1))
            send(i, 7, win(i, dg, c), sib)
        for i in range(n):
            landed(i, 0, win(i, (x, y), 1 - c))
            landed(i, 5, win(i, xn, 1 - c))
            landed(i, 6, win(i, yn, 1 - c))
            landed(i, 7, win(i, dg, 1 - c))
        for cp in sends:
            cp.wait_send()
        for loc in locs:
            loc.wait()

    out_type = [SDS((_MATS[i][2], _MATS[i][3]), BF) for i in idx]
    sems = [pltpu.SemaphoreType.DMA((n, 8)), pltpu.SemaphoreType.DMA((n, 8)), pltpu.SemaphoreType.DMA((n,))]
    return _sequencer(body, out_type, sems, cid, name)(*units)


def _rs_sibling(grads, idx, cid, name):
    n = len(idx)

    def body(*refs):
        gs, bufs = refs[:n], refs[n:2 * n]
        ssem, rsem = refs[2 * n:]
        x, y, c, _ = _my_place()
        _handshake([(x, y, 1 - c)])
        cps = []
        for i in range(n):
            for k in range(4):
                cp = pltpu.make_async_remote_copy(
                    src_ref=_win(gs[i], idx[i], k, 1 - c), dst_ref=bufs[i].at[k], send_sem=ssem.at[i, k], recv_sem=rsem.at[i, k],
                    device_id=(x, y, 1 - c), device_id_type=MESH)
                cp.start()
                cps.append(cp)
        for cp in cps:
            cp.wait()

    out_type = [SDS((4, *_unit_shape(i)), BF) for i in idx]
    sems = [pltpu.SemaphoreType.DMA((n, 4)), pltpu.SemaphoreType.DMA((n, 4))]
    return _sequencer(body, out_type, sems, cid, name)(*grads)


def _rs_sibling_tc(grad, i, cid, name):
    def body(g_ref, buf_ref, ssem, rsem):
        x, y, c, _ = _my_place()
        _handshake([(x, y, 1 - c)])
        cps = []
        for k in range(4):
            cp = pltpu.make_async_remote_copy(
                src_ref=_win(g_ref, i, k, 1 - c), dst_ref=buf_ref.at[k], send_sem=ssem.at[k], recv_sem=rsem.at[k],
                device_id=(x, y, 1 - c), device_id_type=MESH)
            cp.start()
            cps.append(cp)
        for cp in cps:
            cp.wait()

    return _pcall(body, in_specs=[_ANY], out_specs=_ANY, out_shape=SDS((4, *_unit_shape(i)), BF),
                  scratch_shapes=[pltpu.SemaphoreType.DMA((4,)), pltpu.SemaphoreType.DMA((4,))],
                  compiler_params=_cp(collective_id=cid), name=name)(grad)


def _chip_sum(g, sib, i, place, name):
    _, _, K, N, kind, sh, _ = _MATS[i]

    def body(pref, g_ref, s_ref, o_ref):
        o_ref[...] = (g_ref[...].astype(F32) + s_ref[...].astype(F32)).astype(BF)

    if kind == "row":
        tr = _pick(sh, (800, 352, 256))
        per = sh // tr
        grid = (4, per)
        g_spec = pl.BlockSpec((tr, N), lambda k, r, pref: ((2 * k + pref[0]) * per + r, 0))
        u_spec = pl.BlockSpec((None, tr, N), lambda k, r, pref: (k, r, 0))
    else:
        grid = (4,)
        g_spec = pl.BlockSpec((K, sh), lambda k, pref: (0, 2 * k + pref[0]))
        u_spec = pl.BlockSpec((None, K, sh), lambda k, pref: (k, 0, 0))
    return _pcall(body, grid=grid, in_specs=[g_spec, u_spec], out_specs=u_spec, out_shape=SDS((4, *_unit_shape(i)), BF),
                  prefetch=True, name=name)(place, g, sib)


def _rs_chips(parts, idx, cid, name):
    n = len(idx)

    def body(*refs):
        ts, bufs, relays = refs[:n], refs[n:2 * n], refs[2 * n:3 * n]
        ssem, rsem = refs[3 * n:]
        x, y, c, _ = _my_place()
        xn, yn, dg = (1 - x, y), (x, 1 - y), (1 - x, 1 - y)
        to_x, to_y, me = (*xn, c), (*yn, c), (x, y, c)
        _handshake([to_x, to_y])

        def rcopy(i, k, src, dst, to):
            return pltpu.make_async_remote_copy(src_ref=src, dst_ref=dst, send_sem=ssem.at[i, k], recv_sem=rsem.at[i, k],
                                                device_id=to, device_id_type=MESH)

        def halves(i):
            rows = _unit_shape(idx[i])[0] // 2
            return [pl.ds(0, rows), pl.ds(rows, rows)]

        sends = []

        def send(cp):
            cp.start()
            sends.append(cp)

        for i in range(n):
            h0, h1 = halves(i)
            diag = ts[i].at[2 * dg[0] + dg[1]]
            send(rcopy(i, 0, ts[i].at[2 * xn[0] + xn[1]], bufs[i].at[0], to_x))
            send(rcopy(i, 1, ts[i].at[2 * yn[0] + yn[1]], bufs[i].at[1], to_y))
            send(rcopy(i, 2, diag.at[h0], relays[i].at[0], to_x))
            send(rcopy(i, 3, diag.at[h1], relays[i].at[1], to_y))
        for i in range(n):
            h0, h1 = halves(i)
            rcopy(i, 2, relays[i].at[0], relays[i].at[0], me).wait_recv()
            send(rcopy(i, 4, relays[i].at[0], bufs[i].at[2, h0], to_y))
            rcopy(i, 3, relays[i].at[1], relays[i].at[1], me).wait_recv()
            send(rcopy(i, 5, relays[i].at[1], bufs[i].at[2, h1], to_x))
        for i in range(n):
            h0, h1 = halves(i)
            rcopy(i, 0, bufs[i].at[0], bufs[i].at[0], me).wait_recv()
            rcopy(i, 1, bufs[i].at[1], bufs[i].at[1], me).wait_recv()
            rcopy(i, 4, bufs[i].at[2, h0], bufs[i].at[2, h0], me).wait_recv()
            rcopy(i, 5, bufs[i].at[2, h1], bufs[i].at[2, h1], me).wait_recv()
        for cp in sends:
            cp.wait_send()

    def relay_shape(i):
        rows, cols = _unit_shape(i)
        return (2, rows // 2, cols)

    out_type = [SDS((3, *_unit_shape(i)), BF) for i in idx] + [SDS(relay_shape(i), BF) for i in idx]
    sems = [pltpu.SemaphoreType.DMA((n, 6)), pltpu.SemaphoreType.DMA((n, 6))]
    return _sequencer(body, out_type, sems, cid, name)(*parts)[:n]


def _adamw(w, g, m, v):
    m = ADAM_B1 * m + (1.0 - ADAM_B1) * g
    v = ADAM_B2 * v + (1.0 - ADAM_B2) * jnp.square(g)
    m_hat = m / (1.0 - ADAM_B1 ** ADAM_STEP)
    v_hat = v / (1.0 - ADAM_B2 ** ADAM_STEP)
    delta = -ADAM_LR * (m_hat / (jnp.sqrt(v_hat) + ADAM_EPS) + ADAM_WD * w)
    return delta, m, v


def _adam_mat(parts, recvs, w, m, v, place, name):
    R, C = w.shape
    n = len(parts)
    tr = _pick(R, (256, 160, 176))

    def body(pref, *refs):
        t_refs, r_refs = refs[:n], refs[n:2 * n]
        w_ref, m_ref, v_ref, g_out, d_out, m_out, v_out = refs[2 * n:]
        cols = [t[...].astype(F32) + r[0].astype(F32) + r[1].astype(F32) + r[2].astype(F32) for t, r in zip(t_refs, r_refs)]
        g = jnp.concatenate(cols, axis=1) if n > 1 else cols[0]
        d, mn, vn = _adamw(w_ref[...], g, m_ref[...], v_ref[...])
        g_out[...] = g
        d_out[...] = d
        m_out[...] = mn
        v_out[...] = vn

    t_spec = pl.BlockSpec((None, tr, C // n), lambda i, pref: (pref[1], i, 0))
    r_spec = pl.BlockSpec((3, tr, C // n), lambda i, pref: (0, i, 0))
    s_spec = pl.BlockSpec((tr, C), lambda i, pref: (i, 0))
    return _pcall(body, grid=(R // tr,), in_specs=[t_spec] * n + [r_spec] * n + [s_spec] * 3, out_specs=[s_spec] * 4,
                  out_shape=[SDS((R, C), F32)] * 4, prefetch=True, name=name)(place, *parts, *recvs, w, m, v)


_SMALL = (("ln_mix", "ln_mix_w", D_MODEL), ("ln_cross", "ln_cross_w", D_MODEL), ("ln_mem", "ln_mem_w", D_MODEL),
          ("ln_ffn", "ln_ffn_w", D_MODEL), ("ln_final", "ln_final_w", D_MODEL), ("hg_lb", "hg_lower_bounds", 2 * HG_WIDTH),
          ("hg_norm", "hg_norm_w", HEAD_DIM))
_PACK = sum(n for _, _, n in _SMALL) + LANE


def _small_sync(grow, wrow, mrow, vrow):
    def body(g_ref, w_ref, m_ref, v_ref, o_ref, gath, ssem, rsem):
        x, y, c, _ = _my_place()
        me = 4 * x + 2 * y + c
        gath[pl.ds(me, 1), :] = g_ref[...]
        cps = []
        for dlt in range(1, N_DEV):
            peer = (me + dlt) % N_DEV
            cp = pltpu.make_async_remote_copy(
                src_ref=g_ref, dst_ref=gath.at[pl.ds(me, 1), :], send_sem=ssem.at[peer], recv_sem=rsem.at[me],
                device_id=(peer // 4, (peer // 2) % 2, peer % 2), device_id_type=MESH)
            cp.start()
            cps.append(cp)
        for dlt in range(1, N_DEV):
            src = (me + dlt) % N_DEV
            pltpu.make_async_remote_copy(
                src_ref=g_ref, dst_ref=gath.at[pl.ds(src, 1), :], send_sem=ssem.at[src], recv_sem=rsem.at[src],
                device_id=(x, y, c), device_id_type=MESH).wait_recv()
        for cp in cps:
            cp.wait_send()
        g = gath[0:1, :]
        for j in range(1, N_DEV):
            g = g + gath[j:j + 1, :]
        d, mn, vn = _adamw(w_ref[...], g, m_ref[...], v_ref[...])
        o_ref[0:1, :] = g
        o_ref[1:2, :] = d
        o_ref[2:3, :] = mn
        o_ref[3:4, :] = vn

    vm = pl.BlockSpec(memory_space=pltpu.VMEM)
    return _pcall(
        body, in_specs=[vm] * 4, out_specs=vm, out_shape=SDS((4, _PACK), F32),
        scratch_shapes=[pltpu.VMEM((N_DEV, _PACK), F32), pltpu.SemaphoreType.DMA((N_DEV,)), pltpu.SemaphoreType.DMA((N_DEV,))],
        name="small_sync")(grow, wrow, mrow, vrow)


def _pack_small(d, loss_row=None):
    parts = [d[k].reshape(1, n).astype(F32) for k, _, n in _SMALL]
    parts.append(loss_row if loss_row is not None else jnp.ones((1, LANE), F32))
    return jnp.concatenate(parts, axis=1)


def kernel(x, mem, ln_mix_w, w_in, hg_norm_w, hg_lower_bounds, w_branch_a, w_branch_b, w_out, ln_cross_w, ln_mem_w, wq_cross, wkv_cross, wo_cross, ln_ffn_w, w1, w3, w2, ln_final_w, loss_target, m_ln_mix_w, m_w_in, m_hg_norm_w, m_hg_lower_bounds, m_w_branch_a, m_w_branch_b, m_w_out, m_ln_cross_w, m_ln_mem_w, m_wq_cross, m_wkv_cross, m_wo_cross, m_ln_ffn_w, m_w1, m_w3, m_w2, m_ln_final_w, v_ln_mix_w, v_w_in, v_hg_norm_w, v_hg_lower_bounds, v_w_branch_a, v_w_branch_b, v_w_out, v_ln_cross_w, v_ln_mem_w, v_wq_cross, v_wkv_cross, v_wo_cross, v_ln_ffn_w, v_w1, v_w3, v_w2, v_ln_final_w):
    given = dict(locals())
    place = jnp.stack([lax.axis_index("c"), 2 * lax.axis_index("x") + lax.axis_index("y")]).astype(jnp.int32)

    _ORDER[0] = None
    index = _INDEX

    def local_view(a, i):
        return a[0].T if _MATS[i][6] else a[0]

    shard = {key: local_view(given[ref], i) for i, (key, ref, *_) in enumerate(_MATS)}

    p = {}
    col_half = {"w_in_a": 0, "w_in_b": 1}
    waves = (("in_a", ["w_in_a"], 1), ("in_b", ["w_in_b"], 12), ("mid", ["w_ba", "w_bb", "w_out", "wq", "wkv", "wo"], 2),
             ("w1", ["w1"], 11), ("w3", ["w3"], 15), ("w2", ["w2"], 16))
    for wave, keys, cid in waves:
        units = [_cast_unit(shard[k], "cast_" + k, col_half.get(k)) for k in keys]
        p.update(zip(keys, _all_gather(units, [index[k] for k in keys], cid, "all_gather_" + wave)))
    small_w = {"ln_mix": ln_mix_w, "ln_cross": ln_cross_w, "ln_mem": ln_mem_w, "ln_ffn": ln_ffn_w,
               "ln_final": ln_final_w.reshape(1, D_MODEL), "hg_lb": hg_lower_bounds, "hg_norm": hg_norm_w}
    p.update(small_w)

    groups = {"ffn": ["w2", "w1", "w3"], "cross": ["wo", "wq", "wkv"], "mix": ["w_out", "w_ba", "w_bb"]}
    cids = {"ffn": (3, 4), "cross": (5, 6), "mix": (7, 8)}
    for q in range(IN_PARTS):
        groups[f"in_q{q}"] = [f"w_in_q{q}"]
        cids[f"in_q{q}"] = ((9, 10), (13, 14), (17, 18), (19, 20))[q]
    sib, parts, recv, out = {}, {}, {}, {}

    grads = {}

    def chip_stage(name):
        keys = groups[name]
        parts[name] = [_chip_sum(grads[k], s, index[k], place, "chip_sum_" + k) for k, s in zip(keys, sib[name])]
        recv[name] = list(_rs_chips(parts[name], [index[k] for k in keys], cids[name][1], "rs_chips_" + name))

    def adam(ref, i, part_list, recv_list):
        res = _adam_mat(part_list, recv_list, shard[_MATS[i][0]], local_view(given["m_" + ref], i), local_view(given["v_" + ref], i),
                        place, "adam_" + ref)
        out[ref] = [(a.T if _MATS[i][6] else a)[None] for a in res]

    def adam_stage(name):
        for k, t, r in zip(groups[name], parts[name], recv[name]):
            adam(_MATS[index[k]][1], index[k], [t], [r])

    def ready(name, gw):
        if name == "mixers_done":
            adam_stage("ffn")
            chip_stage("mix")
            adam_stage("cross")
            return
        if name == "dh_half":
            return
        keys = groups[name]
        grads.update({k: gw[k] for k in keys})
        if name.startswith("in_q"):
            if name == "in_q1":
                adam_stage("mix")
            sib[name] = [_rs_sibling_tc(gw[k], index[k], cids[name][0], "rs_sibling_" + k) for k in keys]
            chip_stage(name)
            return
        sib[name] = _rs_sibling([gw[k] for k in keys], [index[k] for k in keys], cids[name][0], "rs_sibling_" + name)
        if name == "cross":
            chip_stage("ffn")
        if name == "mix":
            chip_stage("cross")

    loss_row, grad_x, gs = _local_step(x[0], mem[0], loss_target[0], p, ready)

    sm = {k: given["m_" + ref] for k, ref, _ in _SMALL}
    sv = {k: given["v_" + ref] for k, ref, _ in _SMALL}
    synced = _small_sync(_pack_small(gs, loss_row), _pack_small(small_w), _pack_small(sm), _pack_small(sv))
    in_groups = [f"in_q{q}" for q in range(IN_PARTS)]
    adam("w_in", index["w_in_a"], sum((parts[g] for g in in_groups), []), sum((recv[g] for g in in_groups), []))
    off = 0
    for k, ref, n in _SMALL:
        shape = given[ref].shape
        out[ref] = [synced[r, off:off + n].reshape(shape) for r in range(4)]
        off += n
    loss = synced[0, off]

    order = ["ln_mix_w", "w_in", "hg_norm_w", "hg_lower_bounds", "w_branch_a", "w_branch_b", "w_out", "ln_cross_w", "ln_mem_w",
             "wq_cross", "wkv_cross", "wo_cross", "ln_ffn_w", "w1", "w3", "w2", "ln_final_w"]
    return (loss, grad_x[None], *[out[n][0] for n in order], *[out[n][1] for n in order],
            *[out[n][2] for n in order], *[out[n][3] for n in order])
```

```python
import functools
import math

import jax
import jax.numpy as jnp
from jax import lax
from jax.experimental import pallas as pl
from jax.experimental.pallas import tpu as pltpu
from jax.experimental.pallas import tpu_sc as plsc

F32 = jnp.float32
BF = jnp.bfloat16
SDS = jax.ShapeDtypeStruct

D_MODEL = 2048
SEQ = 2048
HEAD_DIM = 128
MEM_LEN = 256
ATT_GROUPS = ((128, 1), (512, 4), (2048, 16))
ATT_HEADS = 4
ATT_WIDTH = 1536
ATT_OUT = 512
HG_HEADS = 8
HG_WIDTH = 1024
HG_CHUNK = 64
IN_WIDTH = 12800
CROSS_HEADS = 4
CROSS_WIDTH = 512
D_FF = 5632
RMS_EPS = 1e-6
ADAM_LR = 0.001
ADAM_B1 = 0.9
ADAM_B2 = 0.999
ADAM_EPS = 1e-08
ADAM_WD = 0.01
ADAM_STEP = 10
N_DEV = 8

LANE = 128
QA0, KA0, VA0 = 0, 12, 24
QH0, FH0, IH0, GH0 = 36, 44, 52, 60
GA0, GB0 = 68, 84

VMEM_BIG = 56 * 1024 * 1024
MESH = pl.DeviceIdType.MESH


def _cp(vmem=None, **kw):
    if vmem is not None:
        kw["vmem_limit_bytes"] = vmem
    return pltpu.CompilerParams(**kw)


def _pick(n, cands):
    for c in cands:
        if n % c == 0:
            return c
    raise ValueError(f"no tile for {n}")


_ANY = pl.BlockSpec(memory_space=pl.ANY)
_ORDER = [None]


def _pcall(body, *, name, in_specs, out_specs, out_shape, grid=(), scratch_shapes=(), compiler_params=None, prefetch=False,
           aliases=None):
    def run(*args):
        dep = _ORDER[0]
        specs, operands, kernel_fn = list(in_specs), list(args), body
        if dep is not None and not any(dep is a for a in operands):
            at = len(operands)
            specs.append(_ANY)
            operands.append(dep)

            def kernel_fn(*refs):
                return body(*refs[:at], *refs[at + 1:])
        if prefetch:
            call = pl.pallas_call(
                kernel_fn, out_shape=out_shape, name=name, compiler_params=compiler_params,
                grid_spec=pltpu.PrefetchScalarGridSpec(num_scalar_prefetch=1, grid=grid, in_specs=specs, out_specs=out_specs,
                                                       scratch_shapes=scratch_shapes))
        else:
            call = pl.pallas_call(kernel_fn, grid=grid, in_specs=specs, out_specs=out_specs, out_shape=out_shape,
                                  scratch_shapes=scratch_shapes, compiler_params=compiler_params, name=name,
                                  input_output_aliases=aliases or {})
        out = call(*operands)
        _ORDER[0] = out[0] if isinstance(out, (list, tuple)) else out
        return out
    return run


def _mm(a, b, form, out_dtype, name, res=None, tiles=None, a_part=(0, 1), b_part=(0, 1)):
    a_shape = (a.shape[0], a.shape[1] // a_part[1])
    b_shape = (b.shape[0], b.shape[1] // b_part[1])
    if form == "nn":
        (M, K), (K2, N) = a_shape, b_shape
    elif form == "nt":
        (M, K), (N, K2) = a_shape, b_shape
    else:
        (K, M), (K2, N) = a_shape, b_shape
    assert K == K2, (a.shape, b.shape, form)
    if tiles is None:
        tm = _pick(M, (1024, 1280, 1408, 512, 256))
        tn = _pick(N, (1280, 1024, 1408, 512, 256))
        tk = K if K <= 2048 else _pick(K, (D_FF, 2560))
        if tk == D_FF:
            tm = _pick(M, (512, 256))
    else:
        tm, tn, tk = tiles
    nk = K // tk
    ah, bh = a_part[0], b_part[0]
    if form == "nn":
        a_spec = pl.BlockSpec((tm, tk), lambda i, j, k: (i, k + ah * nk))
        b_spec = pl.BlockSpec((tk, tn), lambda i, j, k: (k, j + bh * (N // tn)))
        dims = ((1,), (0,))
    elif form == "nt":
        a_spec = pl.BlockSpec((tm, tk), lambda i, j, k: (i, k + ah * nk))
        b_spec = pl.BlockSpec((tn, tk), lambda i, j, k: (j, k + bh * nk))
        dims = ((1,), (1,))
    else:
        a_spec = pl.BlockSpec((tk, tm), lambda i, j, k: (k, i + ah * (M // tm)))
        b_spec = pl.BlockSpec((tk, tn), lambda i, j, k: (k, j + bh * (N // tn)))
        dims = ((0,), (0,))
    o_spec = pl.BlockSpec((tm, tn), lambda i, j, k: (i, j))
    in_specs = [a_spec, b_spec]
    args = [a, b]
    if res is not None:
        in_specs.append(o_spec)
        args.append(res)

    def body(*refs):
        a_ref, b_ref = refs[0], refs[1]
        r_ref = refs[2] if res is not None else None
        o_ref = refs[3] if res is not None else refs[2]
        acc = refs[-1] if nk > 1 else None
        k = pl.program_id(2)
        d = lax.dot_general(a_ref[...].astype(BF), b_ref[...].astype(BF), (dims, ((), ())), preferred_element_type=F32)

        def finish(r):
            if res is not None:
                r = r + r_ref[...].astype(F32)
            o_ref[...] = r.astype(o_ref.dtype)

        if nk == 1:
            finish(d)
        else:
            @pl.when(k == 0)
            def _():
                acc[...] = d

            @pl.when((k > 0) & (k < nk - 1))
            def _():
                acc[...] += d

            @pl.when(k == nk - 1)
            def _():
                finish(acc[...] + d)

    return _pcall(
        body, grid=(M // tm, N // tn, nk), in_specs=in_specs, out_specs=o_spec,
        out_shape=SDS((M, N), out_dtype), scratch_shapes=[pltpu.VMEM((tm, tn), F32)] if nk > 1 else [],
        compiler_params=_cp(VMEM_BIG, dimension_semantics=("parallel", "parallel", "arbitrary")),
        name=name)(*args)


def _rms(x, w):
    return x * lax.rsqrt(jnp.mean(x * x, axis=-1, keepdims=True) + RMS_EPS) * w


def _rms_fwd(x, w, name):
    R, D = x.shape
    tr = 256

    def body(x_ref, w_ref, o_ref):
        o_ref[...] = _rms(x_ref[...], w_ref[...]).astype(o_ref.dtype)

    return _pcall(
        body, grid=(R // tr,),
        in_specs=[pl.BlockSpec((tr, D), lambda i: (i, 0)), pl.BlockSpec((1, D), lambda i: (0, 0))],
        out_specs=pl.BlockSpec((tr, D), lambda i: (i, 0)), out_shape=SDS((R, D), BF), name=name)(x, w)


def _rms_bwd(x, w, dh, dres, name):
    R, D = x.shape
    tr = 256
    want_dx = dres is not None
    dhs = dh if isinstance(dh, (tuple, list)) else (dh,)
    nd = len(dhs)

    def body(*refs):
        x_ref, w_ref, dh_refs = refs[0], refs[1], refs[2:2 + nd]
        if want_dx:
            dr_ref, dx_ref, dxb_ref, dw_ref = refs[2 + nd:]
        else:
            dw_ref = refs[2 + nd]

        @pl.when(pl.program_id(0) == 0)
        def _():
            dw_ref[...] = jnp.zeros_like(dw_ref)

        dhv = jnp.concatenate([r[...].astype(F32) for r in dh_refs], axis=1) if nd > 1 else dh_refs[0][...].astype(F32)
        _, vjp = jax.vjp(_rms, x_ref[...], w_ref[...])
        dx, dw = vjp(dhv)
        dw_ref[...] += dw
        if want_dx:
            dx = dx + dr_ref[...]
            dx_ref[...] = dx
            dxb_ref[...] = dx.astype(BF)

    row = pl.BlockSpec((tr, D), lambda i: (i, 0))
    part = pl.BlockSpec((tr, D // nd), lambda i: (i, 0))
    vec = pl.BlockSpec((1, D), lambda i: (0, 0))
    if want_dx:
        return _pcall(body, grid=(R // tr,), in_specs=[row, vec] + [part] * nd + [row], out_specs=[row, row, vec],
                      out_shape=[SDS((R, D), F32), SDS((R, D), BF), SDS((1, D), F32)], name=name)(x, w, *dhs, dres)
    return _pcall(body, grid=(R // tr,), in_specs=[row, vec] + [part] * nd, out_specs=vec,
                  out_shape=SDS((1, D), F32), name=name)(x, w, *dhs)


def _final_loss(x3, w, tgt):
    R, D = x3.shape
    tr = 256

    def loss_fn(xv, wv, tv):
        err = _rms(xv, wv) - tv
        return 0.5 * jnp.sum(jnp.mean(err * err, axis=-1))

    def body(x_ref, w_ref, t_ref, loss_ref, dx_ref, dxb_ref, dw_ref):
        @pl.when(pl.program_id(0) == 0)
        def _():
            dw_ref[...] = jnp.zeros_like(dw_ref)
            loss_ref[...] = jnp.zeros_like(loss_ref)

        tv = t_ref[...]
        val, vjp = jax.vjp(lambda a, b: loss_fn(a, b, tv), x_ref[...], w_ref[...])
        dx, dw = vjp(jnp.ones((), F32))
        dx_ref[...] = dx
        dxb_ref[...] = dx.astype(BF)
        dw_ref[...] += dw
        loss_ref[...] += jnp.full(loss_ref.shape, val, F32)

    row = pl.BlockSpec((tr, D), lambda i: (i, 0))
    vec = pl.BlockSpec((1, D), lambda i: (0, 0))
    one = pl.BlockSpec((1, LANE), lambda i: (0, 0))
    return _pcall(body, grid=(R // tr,), in_specs=[row, vec, row], out_specs=[one, row, row, vec],
                          out_shape=[SDS((1, LANE), F32), SDS((R, D), F32), SDS((R, D), BF), SDS((1, D), F32)],
                          name="final_loss")(x3, w, tgt)


NEG = -1e30


def _att_blocks(S, d):
    L = S // d
    nb = L // LANE
    return nb, (2 if nb > 1 else 1)


def _att_rows(S, d, idx):
    nb, nkb = _att_blocks(S, d)
    r = idx // nb
    n = idx % nb
    kb = jnp.maximum(n - 1, 0)
    if d == 1:
        qrows = pl.ds(pl.multiple_of(n * LANE, LANE), LANE)
        krows = pl.ds(pl.multiple_of(kb * LANE, LANE), LANE * nkb)
    else:
        qrows = pl.ds(r + n * (LANE * d), LANE, stride=d)
        krows = pl.ds(r + kb * (LANE * d), LANE * nkb, stride=d)
    qpos = n * LANE + lax.broadcasted_iota(jnp.int32, (LANE, LANE * nkb), 0)
    kpos = kb * LANE + lax.broadcasted_iota(jnp.int32, (LANE, LANE * nkb), 1)
    return qrows, krows, qpos - kpos


def _slab(col):
    return pl.BlockSpec((SEQ, LANE), lambda h, col=col: (0, col + h))


def _widen(slabs, wide):
    for t, ref in enumerate(slabs):
        wide[t] = ref[...].astype(F32)
    return tuple(tuple(wide.at[3 * kind + g] for g in range(3)) for kind in range(3))


def _attn_fwd(proj):
    S = proj.shape[0]
    scale = HEAD_DIM ** -0.5

    def body(q0, q1, q2, k0, k1, k2, v0, v1, v2, o_ref, lse_ref, og, lg, wide):
        qs, ks, vs = _widen((q0, q1, q2, k0, k1, k2, v0, v1, v2), wide)
        for g, (window, d) in enumerate(ATT_GROUPS):
            def blk(idx, carry, g=g, window=window, d=d):
                qrows, krows, dist = _att_rows(S, d, idx)
                q = qs[g][qrows, :].astype(BF)
                k = ks[g][krows, :].astype(BF)
                v = vs[g][krows, :].astype(BF)
                s = lax.dot_general(q, k, (((1,), (1,)), ((), ())), preferred_element_type=F32) * scale
                s = jnp.where((dist >= 0) & (dist <= window // d), s, NEG)
                m = jnp.max(s, axis=-1, keepdims=True)
                p = jnp.exp(s - m)
                l = jnp.sum(p, axis=-1, keepdims=True)
                o = jnp.dot((p / l).astype(BF), v, preferred_element_type=F32)
                og[g, qrows, :] = o
                lg[g, qrows, :] = jnp.broadcast_to(m + jnp.log(l), (LANE, LANE))
                return carry
            lax.fori_loop(0, S // LANE, blk, 0, unroll=8)

        def merge(t, carry):
            rows = pl.ds(pl.multiple_of(t * 256, 256), 256)
            l0, l1, l2 = lg[0, rows, :], lg[1, rows, :], lg[2, rows, :]
            m = jnp.maximum(jnp.maximum(l0, l1), l2)
            w0, w1, w2 = jnp.exp(l0 - m), jnp.exp(l1 - m), jnp.exp(l2 - m)
            den = w0 + w1 + w2
            o_ref[rows, :] = (w0 * og[0, rows, :] + w1 * og[1, rows, :] + w2 * og[2, rows, :]) / den
            lse_ref[rows, :] = m + jnp.log(den)
            return carry
        lax.fori_loop(0, S // 256, merge, 0)

    ins = [_slab(QA0 + 4 * g) for g in range(3)] + [_slab(KA0 + 4 * g) for g in range(3)] + [_slab(VA0 + 4 * g) for g in range(3)]
    out = pl.BlockSpec((SEQ, LANE), lambda h: (0, h))
    return _pcall(
        body, grid=(ATT_HEADS,), in_specs=ins, out_specs=[out, out],
        out_shape=[SDS((S, ATT_OUT), F32), SDS((S, ATT_OUT), F32)],
        scratch_shapes=[pltpu.VMEM((3, S, LANE), F32), pltpu.VMEM((3, S, LANE), F32), pltpu.VMEM((9, S, LANE), F32)],
        compiler_params=_cp(VMEM_BIG), name="attn_fwd")(*([proj] * 9))


def _attn_bwd(proj, o_att, lse, do_att, dproj):
    S = proj.shape[0]
    scale = HEAD_DIM ** -0.5

    def body(q0, q1, q2, k0, k1, k2, v0, v1, v2, o_ref, lse_ref, do_ref, dp_in, dp_ref, dqa, dka, dva, delta, stage, sem, wide):
        qs, ks, vs = _widen((q0, q1, q2, k0, k1, k2, v0, v1, v2), wide)
        head = pl.program_id(0)

        def slab_copies(g):
            return [pltpu.make_async_copy(stage.at[t], dp_ref.at[:, pl.ds(pl.multiple_of((base + 4 * g + head) * LANE, LANE), LANE)],
                                          sem.at[t]) for t, base in enumerate((QA0, KA0, VA0))]

        delta[...] = jnp.broadcast_to(jnp.sum(do_ref[...] * o_ref[...], axis=-1, keepdims=True), delta.shape)
        for g, (window, d) in enumerate(ATT_GROUPS):
            dka[...] = jnp.zeros_like(dka)
            dva[...] = jnp.zeros_like(dva)

            def blk(idx, carry, g=g, window=window, d=d):
                qrows, krows, dist = _att_rows(S, d, idx)
                q = qs[g][qrows, :].astype(BF)
                k = ks[g][krows, :].astype(BF)
                v = vs[g][krows, :].astype(BF)
                do = do_ref[qrows, :]
                s = lax.dot_general(q, k, (((1,), (1,)), ((), ())), preferred_element_type=F32) * scale
                p = jnp.exp(s - lse_ref[qrows, :][:, 0:1])
                p = jnp.where((dist >= 0) & (dist <= window // d), p, 0.0)
                dob = do.astype(BF)
                dva[krows, :] += lax.dot_general(p.astype(BF), dob, (((0,), (0,)), ((), ())), preferred_element_type=F32)
                dp = lax.dot_general(dob, v, (((1,), (1,)), ((), ())), preferred_element_type=F32)
                ds = (p * (dp - delta[qrows, :][:, 0:1]) * scale).astype(BF)
                dqa[qrows, :] = jnp.dot(ds, k, preferred_element_type=F32)
                dka[krows, :] += lax.dot_general(ds, q, (((0,), (0,)), ((), ())), preferred_element_type=F32)
                return carry
            lax.fori_loop(0, S // LANE, blk, 0, unroll=8)
            if g > 0:
                for cp in slab_copies(g - 1):
                    cp.wait()
            stage[0] = dqa[...].astype(BF)
            stage[1] = dka[...].astype(BF)
            stage[2] = dva[...].astype(BF)
            for cp in slab_copies(g):
                cp.start()
        for cp in slab_copies(len(ATT_GROUPS) - 1):
            cp.wait()

    cols = [QA0 + 4 * g for g in range(3)] + [KA0 + 4 * g for g in range(3)] + [VA0 + 4 * g for g in range(3)]
    ins = [_slab(c) for c in cols]
    one = pl.BlockSpec((SEQ, LANE), lambda h: (0, h))
    return _pcall(
        body, grid=(ATT_HEADS,), in_specs=ins + [one, one, one, _ANY], out_specs=_ANY,
        out_shape=SDS((S, IN_WIDTH), BF), aliases={12: 0},
        scratch_shapes=[pltpu.VMEM((S, LANE), F32)] * 4 + [pltpu.VMEM((3, S, LANE), BF), pltpu.SemaphoreType.DMA((3,)),
                                                           pltpu.VMEM((9, S, LANE), F32)],
        compiler_params=_cp(VMEM_BIG), name="attn_bwd")(*([proj] * 9), o_att, lse, do_att, dproj)


def _bdot(a, b, dims):
    return lax.dot_general(a.astype(BF), b.astype(BF), (dims, ((), ())), preferred_element_type=F32)


def _hg_chunk(qh, fh, ih, gh, lbraw, nw, st):
    C = HG_CHUNK
    mx = jnp.maximum(lbraw[0:1], lbraw[1:2])
    e0 = jnp.exp(lbraw[0:1] - mx)
    e1 = jnp.exp(lbraw[1:2] - mx)
    lb = e0 / (e0 + e1)
    f = lb + (1.0 - lb) * jax.nn.sigmoid(fh)
    logf = jnp.log(f)
    k = 1.0 - f
    q = qh * jax.nn.sigmoid(qh)
    causal = lax.broadcasted_iota(jnp.int32, (C, C), 0) >= lax.broadcasted_iota(jnp.int32, (C, C), 1)
    b = lax.dot_general(causal.astype(F32), logf, (((1,), (0,)), ((), ())), precision=lax.Precision.HIGHEST,
                        preferred_element_type=F32)
    b_last = b[C - 1:C]
    st_out = st * jnp.exp(b_last) + _bdot(ih, k * jnp.exp(b_last - b), ((0,), (0,)))
    inter = _bdot(q * jnp.exp(b), st, ((1,), (1,)))
    b_ref = b[C // 2:C // 2 + 1]
    a = _bdot(q * jnp.exp(b - b_ref), k * jnp.exp(b_ref - b), ((1,), (1,)))
    a = jnp.where(causal, a, 0.0)
    o = inter + _bdot(a, ih, ((1,), (0,)))
    o = o * lax.rsqrt(jnp.mean(o * o, axis=-1, keepdims=True) + RMS_EPS) * nw
    return o * (gh * jax.nn.sigmoid(gh)), st_out


def _to_heads(x):
    return jnp.concatenate([x[None, :, h * LANE:(h + 1) * LANE] for h in range(HG_HEADS)], axis=0)


def _from_heads(x):
    return jnp.concatenate([x[h] for h in range(HG_HEADS)], axis=1)


def _hdot(a, b, ca, cb):
    return lax.dot_general(a.astype(BF), b.astype(BF), (((ca,), (cb,)), ((0,), (0,))), preferred_element_type=F32)


def _hg_chunk_all(qh, fh, ih, gh, lbraw, nw, st):
    C = HG_CHUNK
    mx = jnp.maximum(lbraw[0:1], lbraw[1:2])
    e0 = jnp.exp(lbraw[0:1] - mx)
    e1 = jnp.exp(lbraw[1:2] - mx)
    lb = e0 / (e0 + e1)
    f = lb + (1.0 - lb) * jax.nn.sigmoid(fh)
    logf = jnp.log(f)
    k = 1.0 - f
    q = qh * jax.nn.sigmoid(qh)
    causal = lax.broadcasted_iota(jnp.int32, (C, C), 0) >= lax.broadcasted_iota(jnp.int32, (C, C), 1)
    b = lax.dot_general(causal.astype(F32), logf, (((1,), (0,)), ((), ())), precision=lax.Precision.HIGHEST,
                        preferred_element_type=F32)
    b_last = b[C - 1:C]
    b_ref = b[C // 2:C // 2 + 1]
    v3 = _to_heads(ih)
    st_out = st * _to_heads(jnp.exp(b_last)) + _hdot(v3, _to_heads(k * jnp.exp(b_last - b)), 1, 1)
    inter = _hdot(_to_heads(q * jnp.exp(b)), st, 2, 2)
    a = _hdot(_to_heads(q * jnp.exp(b - b_ref)), _to_heads(k * jnp.exp(b_ref - b)), 2, 2)
    a = jnp.where(causal[None], a, 0.0)
    o = inter + _hdot(a, v3, 2, 1)
    o = o * lax.rsqrt(jnp.mean(o * o, axis=-1, keepdims=True) + RMS_EPS) * nw
    return _from_heads(o) * (gh * jax.nn.sigmoid(gh)), st_out


def _head_cols(h):
    return slice(h * LANE, (h + 1) * LANE)


def _hg_in_specs(chunk_of):
    half = 4 * LANE
    specs = []
    for base in (QH0, FH0, IH0, GH0):
        for part in range(2):
            specs.append(pl.BlockSpec((HG_CHUNK, half), lambda n, col=base // 4 + part: (chunk_of(n), col)))
    return specs


def _hg_sections(sec):
    return [jnp.concatenate([sec[2 * s][...], sec[2 * s + 1][...]], axis=1).astype(F32) for s in range(4)]


def _hg_fwd(proj, lb, nw):
    S = proj.shape[0]
    C = HG_CHUNK
    NC = S // C

    def body(*refs):
        sec = refs[:8]
        lb_ref, nw_ref, o_ref, st_ref, st_scr = refs[8:]

        @pl.when(pl.program_id(0) == 0)
        def _():
            st_scr[...] = jnp.zeros_like(st_scr)

        st = st_scr[...]
        st_ref[:, 0] = st
        o, st_out = _hg_chunk_all(*_hg_sections(sec), lb_ref[...], nw_ref[...], st)
        o_ref[...] = o.astype(o_ref.dtype)
        st_scr[...] = st_out

    return _pcall(
        body, grid=(NC,),
        in_specs=_hg_in_specs(lambda n: n) + [pl.BlockSpec((2, HG_WIDTH), lambda n: (0, 0)), pl.BlockSpec((1, LANE), lambda n: (0, 0))],
        out_specs=[pl.BlockSpec((C, HG_WIDTH), lambda n: (n, 0)), pl.BlockSpec((HG_HEADS, 1, LANE, LANE), lambda n: (0, n, 0, 0))],
        out_shape=[SDS((S, HG_WIDTH), BF), SDS((HG_HEADS, NC, LANE, LANE), F32)],
        scratch_shapes=[pltpu.VMEM((HG_HEADS, LANE, LANE), F32)], name="hg_fwd")(*([proj] * 8), lb, nw)


def _hg_bwd(proj, lb, nw, states, do_hg, dproj):
    S = proj.shape[0]
    C = HG_CHUNK
    NC = S // C

    def body(*refs):
        sec = refs[:8]
        lb_ref, nw_ref, st_ref, do_ref, dp_in, dp_ref, dlb_ref, dnw_ref, dst_scr, stage, sem = refs[8:]
        n = pl.program_id(0)
        slot = n % 2

        def chunk_copies(s):
            rows = pl.ds(pl.multiple_of((NC - 1 - n) * C, C), C)
            return [pltpu.make_async_copy(stage.at[s, t], dp_ref.at[rows, pl.ds((QH0 + 8 * t) * LANE, HG_WIDTH)], sem.at[s, t])
                    for t in range(4)]

        @pl.when(n == 0)
        def _():
            dst_scr[...] = jnp.zeros_like(dst_scr)
            dlb_ref[...] = jnp.zeros_like(dlb_ref)
            dnw_ref[...] = jnp.zeros_like(dnw_ref)

        _, vjp = jax.vjp(_hg_chunk_all, *_hg_sections(sec), lb_ref[...], nw_ref[...], st_ref[:, 0])
        dq, df, di, dg, dlb, dnw, dst = vjp((do_ref[...].astype(F32), dst_scr[...]))
        dlb_ref[...] += dlb
        dnw_ref[...] += dnw
        dst_scr[...] = dst

        @pl.when(n >= 2)
        def _():
            for cp in chunk_copies(slot):
                cp.wait()

        for t, val in enumerate((dq, df, di, dg)):
            stage[slot, t] = val.astype(BF)
        for cp in chunk_copies(slot):
            cp.start()

        @pl.when(n == NC - 1)
        def _():
            for cp in chunk_copies(1 - slot) + chunk_copies(slot):
                cp.wait()

    rev = lambda n: NC - 1 - n
    ospec = pl.BlockSpec((C, HG_WIDTH), lambda n: (rev(n), 0))
    lbs = pl.BlockSpec((2, HG_WIDTH), lambda n: (0, 0))
    nws = pl.BlockSpec((1, LANE), lambda n: (0, 0))
    outs = _pcall(
        body, grid=(NC,),
        in_specs=_hg_in_specs(rev) + [lbs, nws, pl.BlockSpec((HG_HEADS, 1, LANE, LANE), lambda n: (0, rev(n), 0, 0)), ospec, _ANY],
        out_specs=[_ANY, lbs, nws],
        out_shape=[SDS((S, IN_WIDTH), BF), SDS((2, HG_WIDTH), F32), SDS((1, LANE), F32)], aliases={12: 0},
        scratch_shapes=[pltpu.VMEM((HG_HEADS, LANE, LANE), F32), pltpu.VMEM((2, 4, C, HG_WIDTH), BF), pltpu.SemaphoreType.DMA((2, 4))],
        name="hg_bwd")(*([proj] * 8), lb, nw, states, do_hg, dproj)
    return outs


def _merge(ga, gb, ba, bb):
    return jax.nn.sigmoid(ga) * ba + jax.nn.sigmoid(gb) * bb


_MT = 512


def _gate_specs():
    ga = pl.BlockSpec((_MT, _MT), lambda i, j: (i, GA0 * LANE // _MT + j))
    gb = pl.BlockSpec((_MT, _MT), lambda i, j: (i, GB0 * LANE // _MT + j))
    t = pl.BlockSpec((_MT, _MT), lambda i, j: (i, j))
    return ga, gb, t


def _merge_fwd(proj, ba, bb):
    S, D = ba.shape
    ga, gb, t = _gate_specs()

    def body(ga_ref, gb_ref, ba_ref, bb_ref, o_ref):
        o_ref[...] = _merge(ga_ref[...].astype(F32), gb_ref[...].astype(F32), ba_ref[...].astype(F32),
                            bb_ref[...].astype(F32)).astype(o_ref.dtype)

    return _pcall(body, grid=(S // _MT, D // _MT), in_specs=[ga, gb, t, t], out_specs=t,
                          out_shape=SDS((S, D), BF), name="merge_fwd")(proj, proj, ba, bb)


def _merge_bwd(proj, ba, bb, dm):
    S, D = ba.shape
    ga, gb, t = _gate_specs()

    def body(ga_ref, gb_ref, ba_ref, bb_ref, dm_ref, dba_ref, dbb_ref, dp_ref, stage, sem):
        _, vjp = jax.vjp(_merge, ga_ref[...].astype(F32), gb_ref[...].astype(F32), ba_ref[...].astype(F32),
                         bb_ref[...].astype(F32))
        dga, dgb, dba, dbb = vjp(dm_ref[...].astype(F32))
        dba_ref[...] = dba.astype(BF)
        dbb_ref[...] = dbb.astype(BF)
        stage[0] = dga.astype(BF)
        stage[1] = dgb.astype(BF)
        rows = pl.ds(pl.multiple_of(pl.program_id(0) * _MT, _MT), _MT)
        cps = []
        for k, base in enumerate((GA0, GB0)):
            cols = pl.ds(pl.multiple_of(base * LANE + pl.program_id(1) * _MT, LANE), _MT)
            cps.append(pltpu.make_async_copy(stage.at[k], dp_ref.at[rows, cols], sem.at[k]))
            cps[-1].start()
        for cp in cps:
            cp.wait()

    return _pcall(body, grid=(S // _MT, D // _MT), in_specs=[ga, gb, t, t, t], out_specs=[t, t, _ANY],
                  out_shape=[SDS((S, D), BF), SDS((S, D), BF), SDS((S, IN_WIDTH), BF)],
                  scratch_shapes=[pltpu.VMEM((2, _MT, _MT), BF), pltpu.SemaphoreType.DMA((2,))],
                  name="merge_bwd")(proj, proj, ba, bb, dm)


def _swiglu(a, b):
    return a * jax.nn.sigmoid(a) * b


def _swiglu_fwd(a, b):
    S, F = a.shape
    tf = _pick(F, (1408, 512))
    t = pl.BlockSpec((_MT, tf), lambda i, j: (i, j))

    def body(a_ref, b_ref, o_ref):
        o_ref[...] = _swiglu(a_ref[...].astype(F32), b_ref[...].astype(F32)).astype(o_ref.dtype)

    return _pcall(body, grid=(S // _MT, F // tf), in_specs=[t, t], out_specs=t,
                          out_shape=SDS((S, F), BF), name="swiglu_fwd")(a, b)


def _swiglu_bwd(a, b, dg):
    S, F = a.shape
    tf = _pick(F, (1408, 512))
    t = pl.BlockSpec((_MT, tf), lambda i, j: (i, j))

    def body(a_ref, b_ref, dg_ref, da_ref, db_ref):
        _, vjp = jax.vjp(_swiglu, a_ref[...].astype(F32), b_ref[...].astype(F32))
        da, db = vjp(dg_ref[...].astype(F32))
        da_ref[...] = da.astype(BF)
        db_ref[...] = db.astype(BF)

    return _pcall(body, grid=(S // _MT, F // tf), in_specs=[t, t, t], out_specs=[t, t],
                          out_shape=[SDS((S, F), BF)] * 2, name="swiglu_bwd")(a, b, dg)


def _cross(q, k, v):
    s = _bdot(q, k, ((1,), (1,))) * (HEAD_DIM ** -0.5)
    e = jnp.exp(s - jnp.max(s, axis=-1, keepdims=True))
    p = e / jnp.sum(e, axis=-1, keepdims=True)
    return _bdot(p, v, ((1,), (0,)))


def _cross_specs():
    q = pl.BlockSpec((_MT, LANE), lambda h, i: (i, h))
    k = pl.BlockSpec((MEM_LEN, LANE), lambda h, i: (0, h))
    v = pl.BlockSpec((MEM_LEN, LANE), lambda h, i: (0, CROSS_HEADS + h))
    return q, k, v


def _cross_fwd(qc, kvc):
    S = qc.shape[0]
    q, k, v = _cross_specs()

    def body(q_ref, k_ref, v_ref, o_ref):
        o_ref[...] = _cross(q_ref[...], k_ref[...], v_ref[...]).astype(o_ref.dtype)

    return _pcall(body, grid=(CROSS_HEADS, S // _MT), in_specs=[q, k, v], out_specs=q,
                          out_shape=SDS((S, CROSS_WIDTH), BF), name="cross_fwd")(qc, kvc, kvc)


def _cross_bwd(qc, kvc, doc):
    S = qc.shape[0]
    q, k, v = _cross_specs()

    def body(q_ref, k_ref, v_ref, do_ref, dq_ref, dk_ref, dv_ref):
        @pl.when(pl.program_id(1) == 0)
        def _():
            dk_ref[...] = jnp.zeros_like(dk_ref)
            dv_ref[...] = jnp.zeros_like(dv_ref)

        _, vjp = jax.vjp(_cross, q_ref[...], k_ref[...], v_ref[...])
        dq, dk, dv = vjp(do_ref[...])
        dq_ref[...] = dq.astype(dq_ref.dtype)
        dk_ref[...] += dk
        dv_ref[...] += dv

    return _pcall(body, grid=(CROSS_HEADS, S // _MT), in_specs=[q, k, v, q], out_specs=[q, k, k],
                          out_shape=[SDS((S, CROSS_WIDTH), BF), SDS((MEM_LEN, CROSS_WIDTH), F32), SDS((MEM_LEN, CROSS_WIDTH), F32)],
                          name="cross_bwd")(qc, kvc, kvc, doc)


def _local_step(x, mem, tgt, p, ready):
    h = _rms_fwd(x, p["ln_mix"], "rms_mix")
    proj = _mm(h, p["w_in_a"], "nt", F32, "mm_proj_a", a_part=(0, 2))
    proj = _mm(h, p["w_in_b"], "nt", BF, "mm_proj_b", a_part=(1, 2), res=proj)
    o_att, lse = _attn_fwd(proj)
    o_hg, states = _hg_fwd(proj, p["hg_lb"], p["hg_norm"])
    ba = _mm(o_att, p["w_ba"], "nn", BF, "mm_ba")
    bb = _mm(o_hg, p["w_bb"], "nn", BF, "mm_bb")
    merged = _merge_fwd(proj, ba, bb)
    x1 = _mm(merged, p["w_out"], "nn", F32, "mm_out", res=x)
    hc = _rms_fwd(x1, p["ln_cross"], "rms_cross")
    qc = _mm(hc, p["wq"], "nn", F32, "mm_q")
    mn = _rms_fwd(mem, p["ln_mem"], "rms_mem")
    kvc = _mm(mn, p["wkv"], "nn", F32, "mm_kv")
    oc = _cross_fwd(qc, kvc)
    x2 = _mm(oc, p["wo"], "nn", F32, "mm_o", res=x1)
    hf = _rms_fwd(x2, p["ln_ffn"], "rms_ffn")
    a = _mm(hf, p["w1"], "nt", BF, "mm_w1")
    b = _mm(hf, p["w3"], "nt", BF, "mm_w3")
    g = _swiglu_fwd(a, b)
    x3 = _mm(g, p["w2"], "nn", F32, "mm_w2", res=x2)
    loss, dx3, dx3b, d_ln_final = _final_loss(x3, p["ln_final"], tgt)

    gw, gs = {}, {"ln_final": d_ln_final}
    dg = _mm(dx3b, p["w2"], "nt", BF, "mm_dg")
    gw["w2"] = _mm(g, dx3b, "tn", BF, "mm_dw2")
    da, db = _swiglu_bwd(a, b, dg)
    dhf = _mm(da, p["w1"], "nn", F32, "mm_dhf1")
    dhf = _mm(db, p["w3"], "nn", F32, "mm_dhf3", res=dhf)
    gw["w1"] = _mm(da, hf, "tn", BF, "mm_dw1")
    gw["w3"] = _mm(db, hf, "tn", BF, "mm_dw3")
    ready("ffn", gw)
    dx2, dx2b, gs["ln_ffn"] = _rms_bwd(x2, p["ln_ffn"], dhf, dx3, "rms_ffn_bwd")
    doc = _mm(dx2b, p["wo"], "nt", F32, "mm_doc")
    gw["wo"] = _mm(oc, dx2b, "tn", BF, "mm_dwo")
    dqc, dk, dv = _cross_bwd(qc, kvc, doc)
    dkvc = jnp.concatenate([dk, dv], axis=1)
    dhc = _mm(dqc, p["wq"], "nt", F32, "mm_dhc")
    gw["wq"] = _mm(hc, dqc, "tn", BF, "mm_dwq")
    dmn = _mm(dkvc, p["wkv"], "nt", F32, "mm_dmn")
    gw["wkv"] = _mm(mn, dkvc, "tn", BF, "mm_dwkv")
    ready("cross", gw)
    gs["ln_mem"] = _rms_bwd(mem, p["ln_mem"], dmn, None, "rms_mem_bwd")
    dx1, dx1b, gs["ln_cross"] = _rms_bwd(x1, p["ln_cross"], dhc, dx2, "rms_cross_bwd")
    dmerged = _mm(dx1b, p["w_out"], "nt", BF, "mm_dmerged")
    gw["w_out"] = _mm(merged, dx1b, "tn", BF, "mm_dwout")
    dba, dbb, dproj = _merge_bwd(proj, ba, bb, dmerged)
    do_att = _mm(dba, p["w_ba"], "nt", F32, "mm_doatt")
    gw["w_ba"] = _mm(o_att, dba, "tn", BF, "mm_dwba")
    do_hg = _mm(dbb, p["w_bb"], "nt", F32, "mm_dohg")
    gw["w_bb"] = _mm(o_hg, dbb, "tn", BF, "mm_dwbb")
    ready("mix", gw)
    dproj = _attn_bwd(proj, o_att, lse, do_att, dproj)
    dproj, gs["hg_lb"], gs["hg_norm"] = _hg_bwd(proj, p["hg_lb"], p["hg_norm"], states, do_hg, dproj)
    ready("mixers_done", gw)
    for q in range(IN_PARTS):
        gw[f"w_in_q{q}"] = _mm(dproj, h, "tn", BF, f"mm_dwin_q{q}", b_part=(q, IN_PARTS))
        ready(f"in_q{q}", gw)
    dh_a = _mm(dproj, p["w_in_a"], "nn", F32, "mm_dh_a")
    ready("dh_half", gw)
    dh = (dh_a, _mm(dproj, p["w_in_b"], "nn", F32, "mm_dh_b"))
    grad_x, _, gs["ln_mix"] = _rms_bwd(x, p["ln_mix"], dh, dx1, "rms_mix_bwd")
    return loss, grad_x, gs


_MATS = (
    ("w_in_a", "w_in", IN_WIDTH, D_MODEL // 2, "row", IN_WIDTH // N_DEV, True),
    ("w_in_b", "w_in", IN_WIDTH, D_MODEL // 2, "row", IN_WIDTH // N_DEV, True),
    ("w_ba", "w_branch_a", ATT_OUT, D_MODEL, "col", D_MODEL // N_DEV, False),
    ("w_bb", "w_branch_b", HG_WIDTH, D_MODEL, "col", D_MODEL // N_DEV, False),
    ("w_out", "w_out", D_MODEL, D_MODEL, "row", D_MODEL // N_DEV, False),
    ("wq", "wq_cross", D_MODEL, CROSS_WIDTH, "row", D_MODEL // N_DEV, False),
    ("wkv", "wkv_cross", D_MODEL, 2 * CROSS_WIDTH, "row", D_MODEL // N_DEV, False),
    ("wo", "wo_cross", CROSS_WIDTH, D_MODEL, "col", D_MODEL // N_DEV, False),
    ("w1", "w1", D_FF, D_MODEL, "row", D_FF // N_DEV, True),
    ("w3", "w3", D_FF, D_MODEL, "row", D_FF // N_DEV, True),
    ("w2", "w2", D_FF, D_MODEL, "row", D_FF // N_DEV, False),
    ("w_in_q0", "w_in", IN_WIDTH, D_MODEL // 2, "row", IN_WIDTH // N_DEV, True),
    ("w_in_q1", "w_in", IN_WIDTH, D_MODEL // 2, "row", IN_WIDTH // N_DEV, True),
)
IN_PARTS = 2
_INDEX = {m[0]: i for i, m in enumerate(_MATS)}


def _unit_shape(i):
    _, _, K, N, kind, sh, _ = _MATS[i]
    return (sh, N) if kind == "row" else (K, sh)


def _win(ref, i, chip, par, half=None):
    _, _, K, _, kind, sh, _ = _MATS[i]
    if kind == "row":
        rows, start = (sh, 0) if half is None else (sh // 2, half * (sh // 2))
        return ref.at[pl.ds(pl.multiple_of((2 * chip + par) * sh + start, 16), rows), :]
    rows, start = (K, 0) if half is None else (K // 2, half * (K // 2))
    return ref.at[pl.ds(start, rows), pl.ds(pl.multiple_of((2 * chip + par) * sh, LANE), sh)]


def _my_place():
    x, y, c = lax.axis_index("x"), lax.axis_index("y"), lax.axis_index("c")
    chips = [(1 - x, y), (x, 1 - y), (1 - x, 1 - y)]
    return x, y, c, chips


def _cast_unit(w, name, half=None):
    K, N = w.shape
    if half is not None:
        N = N // 2
    tr = _pick(K, (256, 400, 352))

    def body(w_ref, o_ref):
        o_ref[...] = w_ref[...].astype(BF)

    src = pl.BlockSpec((tr, N), lambda i: (i, half or 0))
    dst = pl.BlockSpec((tr, N), lambda i: (i, 0))
    return _pcall(body, grid=(K // tr,), in_specs=[src], out_specs=dst, out_shape=SDS((K, N), BF), name=name)(w)


def _handshake(peers):
    barrier = pltpu.get_barrier_semaphore()
    for peer in peers:
        pl.semaphore_signal(barrier, inc=1, device_id=peer, device_id_type=MESH)
    pl.semaphore_wait(barrier, len(peers))


def _sequencer(body, out_type, sems, cid, name):
    return pl.kernel(body, out_type=out_type, mesh=plsc.ScalarSubcoreMesh(axis_name="seq", num_cores=1),
                     scratch_types=sems, compiler_params=pltpu.CompilerParams(collective_id=cid), name=name)


def _all_gather(units, idx, cid, name):
    n = len(idx)

    def body(*refs):
        us, fulls = refs[:n], refs[n:2 * n]
        ssem, rsem, lsem = refs[2 * n:]
        x, y, c, _ = _my_place()
        xn, yn, dg = (1 - x, y), (x, 1 - y), (1 - x, 1 - y)
        me, sib = (x, y, c), (x, y, 1 - c)
        _handshake([sib, (*xn, c), (*yn, c)])

        def win(i, chip_xy, par, half=None):
            return _win(fulls[i], idx[i], 2 * chip_xy[0] + chip_xy[1], par, half)

        def rcopy(i, k, src, dst, to):
            return pltpu.make_async_remote_copy(src_ref=src, dst_ref=dst, send_sem=ssem.at[i, k], recv_sem=rsem.at[i, k],
                                                device_id=to, device_id_type=MESH)

        def landed(i, k, blk):
            rcopy(i, k, blk, blk, me).wait_recv()

        sends, locs = [], []

        def send(i, k, blk, to, src=None):
            cp = rcopy(i, k, blk if src is None else src, blk, to)
            cp.start()
            sends.append(cp)

        for i in range(n):
            mine = win(i, (x, y), c)
            loc = pltpu.make_async_copy(us[i], mine, lsem.at[i])
            loc.start()
            locs.append(loc)
            send(i, 0, mine, sib, src=us[i])
            send(i, 1, mine, (*xn, c), src=us[i])
            send(i, 2, mine, (*yn, c), src=us[i])
        for i in range(n):
            landed(i, 1, win(i, xn, c))
            send(i, 3, win(i, xn, c, 0), (*yn, c))
            send(i, 5, win(i, xn, c), sib)
            landed(i, 2, win(i, yn, c))
            send(i, 4, win(i, yn, c, 1), (*xn, c))
            send(i, 6, win(i, yn, c), sib)
        for i in range(n):
            landed(i, 3, win(i, dg, c, 0))
            landed(i, 4, win(i, dg, c, 1))
            send(i, 7, win(i, dg, c), sib)
        for i in range(n):
            landed(i, 0, win(i, (x, y), 1 - c))
            landed(i, 5, win(i, xn, 1 - c))
            landed(i, 6, win(i, yn, 1 - c))
            landed(i, 7, win(i, dg, 1 - c))
        for cp in sends:
            cp.wait_send()
        for loc in locs:
            loc.wait()

    out_type = [SDS((_MATS[i][2], _MATS[i][3]), BF) for i in idx]
    sems = [pltpu.SemaphoreType.DMA((n, 8)), pltpu.SemaphoreType.DMA((n, 8)), pltpu.SemaphoreType.DMA((n,))]
    return _sequencer(body, out_type, sems, cid, name)(*units)


def _rs_sibling(grads, idx, cid, name):
    n = len(idx)

    def body(*refs):
        gs, bufs = refs[:n], refs[n:2 * n]
        ssem, rsem = refs[2 * n:]
        x, y, c, _ = _my_place()
        _handshake([(x, y, 1 - c)])
        cps = []
        for i in range(n):
            for k in range(4):
                cp = pltpu.make_async_remote_copy(
                    src_ref=_win(gs[i], idx[i], k, 1 - c), dst_ref=bufs[i].at[k], send_sem=ssem.at[i, k], recv_sem=rsem.at[i, k],
                    device_id=(x, y, 1 - c), device_id_type=MESH)
                cp.start()
                cps.append(cp)
        for cp in cps:
            cp.wait()

    out_type = [SDS((4, *_unit_shape(i)), BF) for i in idx]
    sems = [pltpu.SemaphoreType.DMA((n, 4)), pltpu.SemaphoreType.DMA((n, 4))]
    return _sequencer(body, out_type, sems, cid, name)(*grads)


def _rs_sibling_tc(grad, i, cid, name):
    def body(g_ref, buf_ref, ssem, rsem):
        x, y, c, _ = _my_place()
        _handshake([(x, y, 1 - c)])
        cps = []
        for k in range(4):
            cp = pltpu.make_async_remote_copy(
                src_ref=_win(g_ref, i, k, 1 - c), dst_ref=buf_ref.at[k], send_sem=ssem.at[k], recv_sem=rsem.at[k],
                device_id=(x, y, 1 - c), device_id_type=MESH)
            cp.start()
            cps.append(cp)
        for cp in cps:
            cp.wait()

    return _pcall(body, in_specs=[_ANY], out_specs=_ANY, out_shape=SDS((4, *_unit_shape(i)), BF),
                  scratch_shapes=[pltpu.SemaphoreType.DMA((4,)), pltpu.SemaphoreType.DMA((4,))],
                  compiler_params=_cp(collective_id=cid), name=name)(grad)


def _chip_sum(g, sib, i, place, name):
    _, _, K, N, kind, sh, _ = _MATS[i]

    def body(pref, g_ref, s_ref, o_ref):
        o_ref[...] = (g_ref[...].astype(F32) + s_ref[...].astype(F32)).astype(BF)

    if kind == "row":
        tr = _pick(sh, (800, 352, 256))
        per = sh // tr
        grid = (4, per)
        g_spec = pl.BlockSpec((tr, N), lambda k, r, pref: ((2 * k + pref[0]) * per + r, 0))
        u_spec = pl.BlockSpec((None, tr, N), lambda k, r, pref: (k, r, 0))
    else:
        grid = (4,)
        g_spec = pl.BlockSpec((K, sh), lambda k, pref: (0, 2 * k + pref[0]))
        u_spec = pl.BlockSpec((None, K, sh), lambda k, pref: (k, 0, 0))
    return _pcall(body, grid=grid, in_specs=[g_spec, u_spec], out_specs=u_spec, out_shape=SDS((4, *_unit_shape(i)), BF),
                  prefetch=True, name=name)(place, g, sib)


def _rs_chips(parts, idx, cid, name):
    n = len(idx)

    def body(*refs):
        ts, bufs, relays = refs[:n], refs[n:2 * n], refs[2 * n:3 * n]
        ssem, rsem = refs[3 * n:]
        x, y, c, _ = _my_place()
        xn, yn, dg = (1 - x, y), (x, 1 - y), (1 - x, 1 - y)
        to_x, to_y, me = (*xn, c), (*yn, c), (x, y, c)
        _handshake([to_x, to_y])

        def rcopy(i, k, src, dst, to):
            return pltpu.make_async_remote_copy(src_ref=src, dst_ref=dst, send_sem=ssem.at[i, k], recv_sem=rsem.at[i, k],
                                                device_id=to, device_id_type=MESH)

        def halves(i):
            rows = _unit_shape(idx[i])[0] // 2
            return [pl.ds(0, rows), pl.ds(rows, rows)]

        sends = []

        def send(cp):
            cp.start()
            sends.append(cp)

        for i in range(n):
            h0, h1 = halves(i)
            diag = ts[i].at[2 * dg[0] + dg[1]]
            send(rcopy(i, 0, ts[i].at[2 * xn[0] + xn[1]], bufs[i].at[0], to_x))
            send(rcopy(i, 1, ts[i].at[2 * yn[0] + yn[1]], bufs[i].at[1], to_y))
            send(rcopy(i, 2, diag.at[h0], relays[i].at[0], to_x))
            send(rcopy(i, 3, diag.at[h1], relays[i].at[1], to_y))
        for i in range(n):
            h0, h1 = halves(i)
            rcopy(i, 2, relays[i].at[0], relays[i].at[0], me).wait_recv()
            send(rcopy(i, 4, relays[i].at[0], bufs[i].at[2, h0], to_y))
            rcopy(i, 3, relays[i].at[1], relays[i].at[1], me).wait_recv()
            send(rcopy(i, 5, relays[i].at[1], bufs[i].at[2, h1], to_x))
        for i in range(n):
            h0, h1 = halves(i)
            rcopy(i, 0, bufs[i].at[0], bufs[i].at[0], me).wait_recv()
            rcopy(i, 1, bufs[i].at[1], bufs[i].at[1], me).wait_recv()
            rcopy(i, 4, bufs[i].at[2, h0], bufs[i].at[2, h0], me).wait_recv()
            rcopy(i, 5, bufs[i].at[2, h1], bufs[i].at[2, h1], me).wait_recv()
        for cp in sends:
            cp.wait_send()

    def relay_shape(i):
        rows, cols = _unit_shape(i)
        return (2, rows // 2, cols)

    out_type = [SDS((3, *_unit_shape(i)), BF) for i in idx] + [SDS(relay_shape(i), BF) for i in idx]
    sems = [pltpu.SemaphoreType.DMA((n, 6)), pltpu.SemaphoreType.DMA((n, 6))]
    return _sequencer(body, out_type, sems, cid, name)(*parts)[:n]


def _adamw(w, g, m, v):
    m = ADAM_B1 * m + (1.0 - ADAM_B1) * g
    v = ADAM_B2 * v + (1.0 - ADAM_B2) * jnp.square(g)
    m_hat = m / (1.0 - ADAM_B1 ** ADAM_STEP)
    v_hat = v / (1.0 - ADAM_B2 ** ADAM_STEP)
    delta = -ADAM_LR * (m_hat / (jnp.sqrt(v_hat) + ADAM_EPS) + ADAM_WD * w)
    return delta, m, v


def _adam_mat(parts, recvs, w, m, v, place, name):
    R, C = w.shape
    n = len(parts)
    tr = _pick(R, (256, 160, 176))

    def body(pref, *refs):
        t_refs, r_refs = refs[:n], refs[n:2 * n]
        w_ref, m_ref, v_ref, g_out, d_out, m_out, v_out = refs[2 * n:]
        cols = [t[...].astype(F32) + r[0].astype(F32) + r[1].astype(F32) + r[2].astype(F32) for t, r in zip(t_refs, r_refs)]
        g = jnp.concatenate(cols, axis=1) if n > 1 else cols[0]
        d, mn, vn = _adamw(w_ref[...], g, m_ref[...], v_ref[...])
        g_out[...] = g
        d_out[...] = d
        m_out[...] = mn
        v_out[...] = vn

    t_spec = pl.BlockSpec((None, tr, C // n), lambda i, pref: (pref[1], i, 0))
    r_spec = pl.BlockSpec((3, tr, C // n), lambda i, pref: (0, i, 0))
    s_spec = pl.BlockSpec((tr, C), lambda i, pref: (i, 0))
    return _pcall(body, grid=(R // tr,), in_specs=[t_spec] * n + [r_spec] * n + [s_spec] * 3, out_specs=[s_spec] * 4,
                  out_shape=[SDS((R, C), F32)] * 4, prefetch=True, name=name)(place, *parts, *recvs, w, m, v)


_SMALL = (("ln_mix", "ln_mix_w", D_MODEL), ("ln_cross", "ln_cross_w", D_MODEL), ("ln_mem", "ln_mem_w", D_MODEL),
          ("ln_ffn", "ln_ffn_w", D_MODEL), ("ln_final", "ln_final_w", D_MODEL), ("hg_lb", "hg_lower_bounds", 2 * HG_WIDTH),
          ("hg_norm", "hg_norm_w", HEAD_DIM))
_PACK = sum(n for _, _, n in _SMALL) + LANE


def _small_sync(grow, wrow, mrow, vrow):
    def body(g_ref, w_ref, m_ref, v_ref, o_ref, gath, ssem, rsem):
        x, y, c, _ = _my_place()
        me = 4 * x + 2 * y + c
        gath[pl.ds(me, 1), :] = g_ref[...]
        cps = []
        for dlt in range(1, N_DEV):
            peer = (me + dlt) % N_DEV
            cp = pltpu.make_async_remote_copy(
                src_ref=g_ref, dst_ref=gath.at[pl.ds(me, 1), :], send_sem=ssem.at[peer], recv_sem=rsem.at[me],
                device_id=(peer // 4, (peer // 2) % 2, peer % 2), device_id_type=MESH)
            cp.start()
            cps.append(cp)
        for dlt in range(1, N_DEV):
            src = (me + dlt) % N_DEV
            pltpu.make_async_remote_copy(
                src_ref=g_ref, dst_ref=gath.at[pl.ds(src, 1), :], send_sem=ssem.at[src], recv_sem=rsem.at[src],
                device_id=(x, y, c), device_id_type=MESH).wait_recv()
        for cp in cps:
            cp.wait_send()
        g = gath[0:1, :]
        for j in range(1, N_DEV):
            g = g + gath[j:j + 1, :]
        d, mn, vn = _adamw(w_ref[...], g, m_ref[...], v_ref[...])
        o_ref[0:1, :] = g
        o_ref[1:2, :] = d
        o_ref[2:3, :] = mn
        o_ref[3:4, :] = vn

    vm = pl.BlockSpec(memory_space=pltpu.VMEM)
    return _pcall(
        body, in_specs=[vm] * 4, out_specs=vm, out_shape=SDS((4, _PACK), F32),
        scratch_shapes=[pltpu.VMEM((N_DEV, _PACK), F32), pltpu.SemaphoreType.DMA((N_DEV,)), pltpu.SemaphoreType.DMA((N_DEV,))],
        name="small_sync")(grow, wrow, mrow, vrow)


def _pack_small(d, loss_row=None):
    parts = [d[k].reshape(1, n).astype(F32) for k, _, n in _SMALL]
    parts.append(loss_row if loss_row is not None else jnp.ones((1, LANE), F32))
    return jnp.concatenate(parts, axis=1)


def kernel(x, mem, ln_mix_w, w_in, hg_norm_w, hg_lower_bounds, w_branch_a, w_branch_b, w_out, ln_cross_w, ln_mem_w, wq_cross, wkv_cross, wo_cross, ln_ffn_w, w1, w3, w2, ln_final_w, loss_target, m_ln_mix_w, m_w_in, m_hg_norm_w, m_hg_lower_bounds, m_w_branch_a, m_w_branch_b, m_w_out, m_ln_cross_w, m_ln_mem_w, m_wq_cross, m_wkv_cross, m_wo_cross, m_ln_ffn_w, m_w1, m_w3, m_w2, m_ln_final_w, v_ln_mix_w, v_w_in, v_hg_norm_w, v_hg_lower_bounds, v_w_branch_a, v_w_branch_b, v_w_out, v_ln_cross_w, v_ln_mem_w, v_wq_cross, v_wkv_cross, v_wo_cross, v_ln_ffn_w, v_w1, v_w3, v_w2, v_ln_final_w):
    given = dict(locals())
    place = jnp.stack([lax.axis_index("c"), 2 * lax.axis_index("x") + lax.axis_index("y")]).astype(jnp.int32)

    _ORDER[0] = None
    index = _INDEX

    def local_view(a, i):
        return a[0].T if _MATS[i][6] else a[0]

    shard = {key: local_view(given[ref], i) for i, (key, ref, *_) in enumerate(_MATS)}

    p = {}
    col_half = {"w_in_a": 0, "w_in_b": 1}
    waves = (("in_a", ["w_in_a"], 1), ("in_b", ["w_in_b"], 12), ("mid", ["w_ba", "w_bb", "w_out", "wq", "wkv", "wo"], 2),
             ("w1", ["w1"], 11), ("w3", ["w3"], 15), ("w2", ["w2"], 16))
    for wave, keys, cid in waves:
        units = [_cast_unit(shard[k], "cast_" + k, col_half.get(k)) for k in keys]
        p.update(zip(keys, _all_gather(units, [index[k] for k in keys], cid, "all_gather_" + wave)))
    small_w = {"ln_mix": ln_mix_w, "ln_cross": ln_cross_w, "ln_mem": ln_mem_w, "ln_ffn": ln_ffn_w,
               "ln_final": ln_final_w.reshape(1, D_MODEL), "hg_lb": hg_lower_bounds, "hg_norm": hg_norm_w}
    p.update(small_w)

    groups = {"ffn": ["w2", "w1", "w3"], "cross": ["wo", "wq", "wkv"], "mix": ["w_out", "w_ba", "w_bb"]}
    cids = {"ffn": (3, 4), "cross": (5, 6), "mix": (7, 8)}
    for q in range(IN_PARTS):
        groups[f"in_q{q}"] = [f"w_in_q{q}"]
        cids[f"in_q{q}"] = ((9, 10), (13, 14), (17, 18), (19, 20))[q]
    sib, parts, recv, out = {}, {}, {}, {}

    grads = {}

    def chip_stage(name):
        keys = groups[name]
        parts[name] = [_chip_sum(grads[k], s, index[k], place, "chip_sum_" + k) for k, s in zip(keys, sib[name])]
        recv[name] = list(_rs_chips(parts[name], [index[k] for k in keys], cids[name][1], "rs_chips_" + name))

    def adam(ref, i, part_list, recv_list):
        res = _adam_mat(part_list, recv_list, shard[_MATS[i][0]], local_view(given["m_" + ref], i), local_view(given["v_" + ref], i),
                        place, "adam_" + ref)
        out[ref] = [(a.T if _MATS[i][6] else a)[None] for a in res]

    def adam_stage(name):
        for k, t, r in zip(groups[name], parts[name], recv[name]):
            adam(_MATS[index[k]][1], index[k], [t], [r])

    def ready(name, gw):
        if name == "mixers_done":
            adam_stage("ffn")
            chip_stage("mix")
            adam_stage("cross")
            return
        if name == "dh_half":
            return
        keys = groups[name]
        grads.update({k: gw[k] for k in keys})
        if name.startswith("in_q"):
            if name == "in_q1":
                adam_stage("mix")
            sib[name] = [_rs_sibling_tc(gw[k], index[k], cids[name][0], "rs_sibling_" + k) for k in keys]
            chip_stage(name)
            return
        sib[name] = _rs_sibling([gw[k] for k in keys], [index[k] for k in keys], cids[name][0], "rs_sibling_" + name)
        if name == "cross":
            chip_stage("ffn")
        if name == "mix":
            chip_stage("cross")

    loss_row, grad_x, gs = _local_step(x[0], mem[0], loss_target[0], p, ready)

    sm = {k: given["m_" + ref] for k, ref, _ in _SMALL}
    sv = {k: given["v_" + ref] for k, ref, _ in _SMALL}
    synced = _small_sync(_pack_small(gs, loss_row), _pack_small(small_w), _pack_small(sm), _pack_small(sv))
    in_groups = [f"in_q{q}" for q in range(IN_PARTS)]
    adam("w_in", index["w_in_a"], sum((parts[g] for g in in_groups), []), sum((recv[g] for g in in_groups), []))
    off = 0
    for k, ref, n in _SMALL:
        shape = given[ref].shape
        out[ref] = [synced[r, off:off + n].reshape(shape) for r in range(4)]
        off += n
    loss = synced[0, off]

    order = ["ln_mix_w", "w_in", "hg_norm_w", "hg_lower_bounds", "w_branch_a", "w_branch_b", "w_out", "ln_cross_w", "ln_mem_w",
             "wq_cross", "wkv_cross", "wo_cross", "ln_ffn_w", "w1", "w3", "w2", "ln_final_w"]
    return (loss, grad_x[None], *[out[n][0] for n in order], *[out[n][1] for n in order],
            *[out[n][2] for n in order], *[out[n][3] for n in order])
```

```python
import functools
import math

import jax
import jax.numpy as jnp
from jax import lax
from jax.experimental import pallas as pl
from jax.experimental.pallas import tpu as pltpu
from jax.experimental.pallas import tpu_sc as plsc

F32 = jnp.float32
BF = jnp.bfloat16
SDS = jax.ShapeDtypeStruct

D_MODEL = 2048
SEQ = 2048
HEAD_DIM = 128
MEM_LEN = 256
ATT_GROUPS = ((128, 1), (512, 4), (2048, 16))
ATT_HEADS = 4
ATT_WIDTH = 1536
ATT_OUT = 512
HG_HEADS = 8
HG_WIDTH = 1024
HG_CHUNK = 64
IN_WIDTH = 12800
CROSS_HEADS = 4
CROSS_WIDTH = 512
D_FF = 5632
RMS_EPS = 1e-6
ADAM_LR = 0.001
ADAM_B1 = 0.9
ADAM_B2 = 0.999
ADAM_EPS = 1e-08
ADAM_WD = 0.01
ADAM_STEP = 10
N_DEV = 8

LANE = 128
QA0, KA0, VA0 = 0, 12, 24
QH0, FH0, IH0, GH0 = 36, 44, 52, 60
GA0, GB0 = 68, 84

VMEM_BIG = 56 * 1024 * 1024
MESH = pl.DeviceIdType.MESH


def _cp(vmem=None, **kw):
    if vmem is not None:
        kw["vmem_limit_bytes"] = vmem
    return pltpu.CompilerParams(**kw)


def _pick(n, cands):
    for c in cands:
        if n % c == 0:
            return c
    raise ValueError(f"no tile for {n}")


_ANY = pl.BlockSpec(memory_space=pl.ANY)
_ORDER = [None]


def _pcall(body, *, name, in_specs, out_specs, out_shape, grid=(), scratch_shapes=(), compiler_params=None, prefetch=False,
           aliases=None):
    def run(*args):
        dep = _ORDER[0]
        specs, operands, kernel_fn = list(in_specs), list(args), body
        if dep is not None and not any(dep is a for a in operands):
            at = len(operands)
            specs.append(_ANY)
            operands.append(dep)

            def kernel_fn(*refs):
                return body(*refs[:at], *refs[at + 1:])
        if prefetch:
            call = pl.pallas_call(
                kernel_fn, out_shape=out_shape, name=name, compiler_params=compiler_params,
                grid_spec=pltpu.PrefetchScalarGridSpec(num_scalar_prefetch=1, grid=grid, in_specs=specs, out_specs=out_specs,
                                                       scratch_shapes=scratch_shapes))
        else:
            call = pl.pallas_call(kernel_fn, grid=grid, in_specs=specs, out_specs=out_specs, out_shape=out_shape,
                                  scratch_shapes=scratch_shapes, compiler_params=compiler_params, name=name,
                                  input_output_aliases=aliases or {})
        out = call(*operands)
        _ORDER[0] = out[0] if isinstance(out, (list, tuple)) else out
        return out
    return run


def _mm(a, b, form, out_dtype, name, res=None, tiles=None, a_part=(0, 1), b_part=(0, 1)):
    a_shape = (a.shape[0], a.shape[1] // a_part[1])
    b_shape = (b.shape[0], b.shape[1] // b_part[1])
    if form == "nn":
        (M, K), (K2, N) = a_shape, b_shape
    elif form == "nt":
        (M, K), (N, K2) = a_shape, b_shape
    else:
        (K, M), (K2, N) = a_shape, b_shape
    assert K == K2, (a.shape, b.shape, form)
    if tiles is None:
        tm = _pick(M, (1024, 1280, 1408, 512, 256))
        tn = _pick(N, (1280, 1024, 1408, 512, 256))
        tk = K if K <= 2048 else _pick(K, (D_FF, 2560))
        if tk == D_FF:
            tm = _pick(M, (512, 256))
    else:
        tm, tn, tk = tiles
    nk = K // tk
    ah, bh = a_part[0], b_part[0]
    if form == "nn":
        a_spec = pl.BlockSpec((tm, tk), lambda i, j, k: (i, k + ah * nk))
        b_spec = pl.BlockSpec((tk, tn), lambda i, j, k: (k, j + bh * (N // tn)))
        dims = ((1,), (0,))
    elif form == "nt":
        a_spec = pl.BlockSpec((tm, tk), lambda i, j, k: (i, k + ah * nk))
        b_spec = pl.BlockSpec((tn, tk), lambda i, j, k: (j, k + bh * nk))
        dims = ((1,), (1,))
    else:
        a_spec = pl.BlockSpec((tk, tm), lambda i, j, k: (k, i + ah * (M // tm)))
        b_spec = pl.BlockSpec((tk, tn), lambda i, j, k: (k, j + bh * (N // tn)))
        dims = ((0,), (0,))
    o_spec = pl.BlockSpec((tm, tn), lambda i, j, k: (i, j))
    in_specs = [a_spec, b_spec]
    args = [a, b]
    if res is not None:
        in_specs.append(o_spec)
        args.append(res)

    def body(*refs):
        a_ref, b_ref = refs[0], refs[1]
        r_ref = refs[2] if res is not None else None
        o_ref = refs[3] if res is not None else refs[2]
        acc = refs[-1] if nk > 1 else None
        k = pl.program_id(2)
        d = lax.dot_general(a_ref[...].astype(BF), b_ref[...].astype(BF), (dims, ((), ())), preferred_element_type=F32)

        def finish(r):
            if res is not None:
                r = r + r_ref[...].astype(F32)
            o_ref[...] = r.astype(o_ref.dtype)

        if nk == 1:
            finish(d)
        else:
            @pl.when(k == 0)
            def _():
                acc[...] = d

            @pl.when((k > 0) & (k < nk - 1))
            def _():
                acc[...] += d

            @pl.when(k == nk - 1)
            def _():
                finish(acc[...] + d)

    return _pcall(
        body, grid=(M // tm, N // tn, nk), in_specs=in_specs, out_specs=o_spec,
        out_shape=SDS((M, N), out_dtype), scratch_shapes=[pltpu.VMEM((tm, tn), F32)] if nk > 1 else [],
        compiler_params=_cp(VMEM_BIG, dimension_semantics=("parallel", "parallel", "arbitrary")),
        name=name)(*args)


def _rms(x, w):
    return x * lax.rsqrt(jnp.mean(x * x, axis=-1, keepdims=True) + RMS_EPS) * w


def _rms_fwd(x, w, name):
    R, D = x.shape
    tr = 256

    def body(x_ref, w_ref, o_ref):
        o_ref[...] = _rms(x_ref[...], w_ref[...]).astype(o_ref.dtype)

    return _pcall(
        body, grid=(R // tr,),
        in_specs=[pl.BlockSpec((tr, D), lambda i: (i, 0)), pl.BlockSpec((1, D), lambda i: (0, 0))],
        out_specs=pl.BlockSpec((tr, D), lambda i: (i, 0)), out_shape=SDS((R, D), BF), name=name)(x, w)


def _rms_bwd(x, w, dh, dres, name):
    R, D = x.shape
    tr = 256
    want_dx = dres is not None
    dhs = dh if isinstance(dh, (tuple, list)) else (dh,)
    nd = len(dhs)

    def body(*refs):
        x_ref, w_ref, dh_refs = refs[0], refs[1], refs[2:2 + nd]
        if want_dx:
            dr_ref, dx_ref, dxb_ref, dw_ref = refs[2 + nd:]
        else:
            dw_ref = refs[2 + nd]

        @pl.when(pl.program_id(0) == 0)
        def _():
            dw_ref[...] = jnp.zeros_like(dw_ref)

        dhv = jnp.concatenate([r[...].astype(F32) for r in dh_refs], axis=1) if nd > 1 else dh_refs[0][...].astype(F32)
        _, vjp = jax.vjp(_rms, x_ref[...], w_ref[...])
        dx, dw = vjp(dhv)
        dw_ref[...] += dw
        if want_dx:
            dx = dx + dr_ref[...]
            dx_ref[...] = dx
            dxb_ref[...] = dx.astype(BF)

    row = pl.BlockSpec((tr, D), lambda i: (i, 0))
    part = pl.BlockSpec((tr, D // nd), lambda i: (i, 0))
    vec = pl.BlockSpec((1, D), lambda i: (0, 0))
    if want_dx:
        return _pcall(body, grid=(R // tr,), in_specs=[row, vec] + [part] * nd + [row], out_specs=[row, row, vec],
                      out_shape=[SDS((R, D), F32), SDS((R, D), BF), SDS((1, D), F32)], name=name)(x, w, *dhs, dres)
    return _pcall(body, grid=(R // tr,), in_specs=[row, vec] + [part] * nd, out_specs=vec,
                  out_shape=SDS((1, D), F32), name=name)(x, w, *dhs)


def _final_loss(x3, w, tgt):
    R, D = x3.shape
    tr = 256

    def loss_fn(xv, wv, tv):
        err = _rms(xv, wv) - tv
        return 0.5 * jnp.sum(jnp.mean(err * err, axis=-1))

    def body(x_ref, w_ref, t_ref, loss_ref, dx_ref, dxb_ref, dw_ref):
        @pl.when(pl.program_id(0) == 0)
        def _():
            dw_ref[...] = jnp.zeros_like(dw_ref)
            loss_ref[...] = jnp.zeros_like(loss_ref)

        tv = t_ref[...]
        val, vjp = jax.vjp(lambda a, b: loss_fn(a, b, tv), x_ref[...], w_ref[...])
        dx, dw = vjp(jnp.ones((), F32))
        dx_ref[...] = dx
        dxb_ref[...] = dx.astype(BF)
        dw_ref[...] += dw
        loss_ref[...] += jnp.full(loss_ref.shape, val, F32)

    row = pl.BlockSpec((tr, D), lambda i: (i, 0))
    vec = pl.BlockSpec((1, D), lambda i: (0, 0))
    one = pl.BlockSpec((1, LANE), lambda i: (0, 0))
    return _pcall(body, grid=(R // tr,), in_specs=[row, vec, row], out_specs=[one, row, row, vec],
                          out_shape=[SDS((1, LANE), F32), SDS((R, D), F32), SDS((R, D), BF), SDS((1, D), F32)],
                          name="final_loss")(x3, w, tgt)


NEG = -1e30


def _att_blocks(S, d):
    L = S // d
    nb = L // LANE
    return nb, (2 if nb > 1 else 1)


def _att_rows(S, d, idx):
    nb, nkb = _att_blocks(S, d)
    r = idx // nb
    n = idx % nb
    kb = jnp.maximum(n - 1, 0)
    if d == 1:
        qrows = pl.ds(pl.multiple_of(n * LANE, LANE), LANE)
        krows = pl.ds(pl.multiple_of(kb * LANE, LANE), LANE * nkb)
    else:
        qrows = pl.ds(r + n * (LANE * d), LANE, stride=d)
        krows = pl.ds(r + kb * (LANE * d), LANE * nkb, stride=d)
    qpos = n * LANE + lax.broadcasted_iota(jnp.int32, (LANE, LANE * nkb), 0)
    kpos = kb * LANE + lax.broadcasted_iota(jnp.int32, (LANE, LANE * nkb), 1)
    return qrows, krows, qpos - kpos


def _slab(col):
    return pl.BlockSpec((SEQ, LANE), lambda h, col=col: (0, col + h))


def _widen(slabs, wide):
    for t, ref in enumerate(slabs):
        wide[t] = ref[...].astype(F32)
    return tuple(tuple(wide.at[3 * kind + g] for g in range(3)) for kind in range(3))


def _attn_fwd(proj):
    S = proj.shape[0]
    scale = HEAD_DIM ** -0.5

    def body(q0, q1, q2, k0, k1, k2, v0, v1, v2, o_ref, lse_ref, og, lg, wide):
        qs, ks, vs = _widen((q0, q1, q2, k0, k1, k2, v0, v1, v2), wide)
        for g, (window, d) in enumerate(ATT_GROUPS):
            def blk(idx, carry, g=g, window=window, d=d):
                qrows, krows, dist = _att_rows(S, d, idx)
                q = qs[g][qrows, :].astype(BF)
                k = ks[g][krows, :].astype(BF)
                v = vs[g][krows, :].astype(BF)
                s = lax.dot_general(q, k, (((1,), (1,)), ((), ())), preferred_element_type=F32) * scale
                s = jnp.where((dist >= 0) & (dist <= window // d), s, NEG)
                m = jnp.max(s, axis=-1, keepdims=True)
                p = jnp.exp(s - m)
                l = jnp.sum(p, axis=-1, keepdims=True)
                o = jnp.dot((p / l).astype(BF), v, preferred_element_type=F32)
                og[g, qrows, :] = o
                lg[g, qrows, :] = jnp.broadcast_to(m + jnp.log(l), (LANE, LANE))
                return carry
            lax.fori_loop(0, S // LANE, blk, 0, unroll=8)

        def merge(t, carry):
            rows = pl.ds(pl.multiple_of(t * 256, 256), 256)
            l0, l1, l2 = lg[0, rows, :], lg[1, rows, :], lg[2, rows, :]
            m = jnp.maximum(jnp.maximum(l0, l1), l2)
            w0, w1, w2 = jnp.exp(l0 - m), jnp.exp(l1 - m), jnp.exp(l2 - m)
            den = w0 + w1 + w2
            o_ref[rows, :] = (w0 * og[0, rows, :] + w1 * og[1, rows, :] + w2 * og[2, rows, :]) / den
            lse_ref[rows, :] = m + jnp.log(den)
            return carry
        lax.fori_loop(0, S // 256, merge, 0)

    ins = [_slab(QA0 + 4 * g) for g in range(3)] + [_slab(KA0 + 4 * g) for g in range(3)] + [_slab(VA0 + 4 * g) for g in range(3)]
    out = pl.BlockSpec((SEQ, LANE), lambda h: (0, h))
    return _pcall(
        body, grid=(ATT_HEADS,), in_specs=ins, out_specs=[out, out],
        out_shape=[SDS((S, ATT_OUT), F32), SDS((S, ATT_OUT), F32)],
        scratch_shapes=[pltpu.VMEM((3, S, LANE), F32), pltpu.VMEM((3, S, LANE), F32), pltpu.VMEM((9, S, LANE), F32)],
        compiler_params=_cp(VMEM_BIG), name="attn_fwd")(*([proj] * 9))


def _attn_bwd(proj, o_att, lse, do_att, dproj):
    S = proj.shape[0]
    scale = HEAD_DIM ** -0.5

    def body(q0, q1, q2, k0, k1, k2, v0, v1, v2, o_ref, lse_ref, do_ref, dp_in, dp_ref, dqa, dka, dva, delta, stage, sem, wide):
        qs, ks, vs = _widen((q0, q1, q2, k0, k1, k2, v0, v1, v2), wide)
        head = pl.program_id(0)

        def slab_copies(g):
            return [pltpu.make_async_copy(stage.at[t], dp_ref.at[:, pl.ds(pl.multiple_of((base + 4 * g + head) * LANE, LANE), LANE)],
                                          sem.at[t]) for t, base in enumerate((QA0, KA0, VA0))]

        delta[...] = jnp.broadcast_to(jnp.sum(do_ref[...] * o_ref[...], axis=-1, keepdims=True), delta.shape)
        for g, (window, d) in enumerate(ATT_GROUPS):
            dka[...] = jnp.zeros_like(dka)
            dva[...] = jnp.zeros_like(dva)

            def blk(idx, carry, g=g, window=window, d=d):
                qrows, krows, dist = _att_rows(S, d, idx)
                q = qs[g][qrows, :].astype(BF)
                k = ks[g][krows, :].astype(BF)
                v = vs[g][krows, :].astype(BF)
                do = do_ref[qrows, :]
                s = lax.dot_general(q, k, (((1,), (1,)), ((), ())), preferred_element_type=F32) * scale
                p = jnp.exp(s - lse_ref[qrows, :][:, 0:1])
                p = jnp.where((dist >= 0) & (dist <= window // d), p, 0.0)
                dob = do.astype(BF)
                dva[krows, :] += lax.dot_general(p.astype(BF), dob, (((0,), (0,)), ((), ())), preferred_element_type=F32)
                dp = lax.dot_general(dob, v, (((1,), (1,)), ((), ())), preferred_element_type=F32)
                ds = (p * (dp - delta[qrows, :][:, 0:1]) * scale).astype(BF)
                dqa[qrows, :] = jnp.dot(ds, k, preferred_element_type=F32)
                dka[krows, :] += lax.dot_general(ds, q, (((0,), (0,)), ((), ())), preferred_element_type=F32)
                return carry
            lax.fori_loop(0, S // LANE, blk, 0, unroll=8)
            if g > 0:
                for cp in slab_copies(g - 1):
                    cp.wait()
            stage[0] = dqa[...].astype(BF)
            stage[1] = dka[...].astype(BF)
            stage[2] = dva[...].astype(BF)
            for cp in slab_copies(g):
                cp.start()
        for cp in slab_copies(len(ATT_GROUPS) - 1):
            cp.wait()

    cols = [QA0 + 4 * g for g in range(3)] + [KA0 + 4 * g for g in range(3)] + [VA0 + 4 * g for g in range(3)]
    ins = [_slab(c) for c in cols]
    one = pl.BlockSpec((SEQ, LANE), lambda h: (0, h))
    return _pcall(
        body, grid=(ATT_HEADS,), in_specs=ins + [one, one, one, _ANY], out_specs=_ANY,
        out_shape=SDS((S, IN_WIDTH), BF), aliases={12: 0},
        scratch_shapes=[pltpu.VMEM((S, LANE), F32)] * 4 + [pltpu.VMEM((3, S, LANE), BF), pltpu.SemaphoreType.DMA((3,)),
                                                           pltpu.VMEM((9, S, LANE), F32)],
        compiler_params=_cp(VMEM_BIG), name="attn_bwd")(*([proj] * 9), o_att, lse, do_att, dproj)


def _bdot(a, b, dims):
    return lax.dot_general(a.astype(BF), b.astype(BF), (dims, ((), ())), preferred_element_type=F32)


def _hg_chunk(qh, fh, ih, gh, lbraw, nw, st):
    C = HG_CHUNK
    mx = jnp.maximum(lbraw[0:1], lbraw[1:2])
    e0 = jnp.exp(lbraw[0:1] - mx)
    e1 = jnp.exp(lbraw[1:2] - mx)
    lb = e0 / (e0 + e1)
    f = lb + (1.0 - lb) * jax.nn.sigmoid(fh)
    logf = jnp.log(f)
    k = 1.0 - f
    q = qh * jax.nn.sigmoid(qh)
    causal = lax.broadcasted_iota(jnp.int32, (C, C), 0) >= lax.broadcasted_iota(jnp.int32, (C, C), 1)
    b = lax.dot_general(causal.astype(F32), logf, (((1,), (0,)), ((), ())), precision=lax.Precision.HIGHEST,
                        preferred_element_type=F32)
    b_last = b[C - 1:C]
    st_out = st * jnp.exp(b_last) + _bdot(ih, k * jnp.exp(b_last - b), ((0,), (0,)))
    inter = _bdot(q * jnp.exp(b), st, ((1,), (1,)))
    b_ref = b[C // 2:C // 2 + 1]
    a = _bdot(q * jnp.exp(b - b_ref), k * jnp.exp(b_ref - b), ((1,), (1,)))
    a = jnp.where(causal, a, 0.0)
    o = inter + _bdot(a, ih, ((1,), (0,)))
    o = o * lax.rsqrt(jnp.mean(o * o, axis=-1, keepdims=True) + RMS_EPS) * nw
    return o * (gh * jax.nn.sigmoid(gh)), st_out


def _to_heads(x):
    return jnp.concatenate([x[None, :, h * LANE:(h + 1) * LANE] for h in range(HG_HEADS)], axis=0)


def _from_heads(x):
    return jnp.concatenate([x[h] for h in range(HG_HEADS)], axis=1)


def _hdot(a, b, ca, cb):
    return lax.dot_general(a.astype(BF), b.astype(BF), (((ca,), (cb,)), ((0,), (0,))), preferred_element_type=F32)


def _hg_chunk_all(qh, fh, ih, gh, lbraw, nw, st):
    C = HG_CHUNK
    mx = jnp.maximum(lbraw[0:1], lbraw[1:2])
    e0 = jnp.exp(lbraw[0:1] - mx)
    e1 = jnp.exp(lbraw[1:2] - mx)
    lb = e0 / (e0 + e1)
    f = lb + (1.0 - lb) * jax.nn.sigmoid(fh)
    logf = jnp.log(f)
    k = 1.0 - f
    q = qh * jax.nn.sigmoid(qh)
    causal = lax.broadcasted_iota(jnp.int32, (C, C), 0) >= lax.broadcasted_iota(jnp.int32, (C, C), 1)
    b = lax.dot_general(causal.astype(F32), logf, (((1,), (0,)), ((), ())), precision=lax.Precision.HIGHEST,
                        preferred_element_type=F32)
    b_last = b[C - 1:C]
    b_ref = b[C // 2:C // 2 + 1]
    v3 = _to_heads(ih)
    st_out = st * _to_heads(jnp.exp(b_last)) + _hdot(v3, _to_heads(k * jnp.exp(b_last - b)), 1, 1)
    inter = _hdot(_to_heads(q * jnp.exp(b)), st, 2, 2)
    a = _hdot(_to_heads(q * jnp.exp(b - b_ref)), _to_heads(k * jnp.exp(b_ref - b)), 2, 2)
    a = jnp.where(causal[None], a, 0.0)
    o = inter + _hdot(a, v3, 2, 1)
    o = o * lax.rsqrt(jnp.mean(o * o, axis=-1, keepdims=True) + RMS_EPS) * nw
    return _from_heads(o) * (gh * jax.nn.sigmoid(gh)), st_out


def _head_cols(h):
    return slice(h * LANE, (h + 1) * LANE)


def _hg_in_specs(chunk_of):
    half = 4 * LANE
    specs = []
    for base in (QH0, FH0, IH0, GH0):
        for part in range(2):
            specs.append(pl.BlockSpec((HG_CHUNK, half), lambda n, col=base // 4 + part: (chunk_of(n), col)))
    return specs


def _hg_sections(sec):
    return [jnp.concatenate([sec[2 * s][...], sec[2 * s + 1][...]], axis=1).astype(F32) for s in range(4)]


def _hg_fwd(proj, lb, nw):
    S = proj.shape[0]
    C = HG_CHUNK
    NC = S // C

    def body(*refs):
        sec = refs[:8]
        lb_ref, nw_ref, o_ref, st_ref, st_scr = refs[8:]

        @pl.when(pl.program_id(0) == 0)
        def _():
            st_scr[...] = jnp.zeros_like(st_scr)

        st = st_scr[...]
        st_ref[:, 0] = st
        o, st_out = _hg_chunk_all(*_hg_sections(sec), lb_ref[...], nw_ref[...], st)
        o_ref[...] = o.astype(o_ref.dtype)
        st_scr[...] = st_out

    return _pcall(
        body, grid=(NC,),
        in_specs=_hg_in_specs(lambda n: n) + [pl.BlockSpec((2, HG_WIDTH), lambda n: (0, 0)), pl.BlockSpec((1, LANE), lambda n: (0, 0))],
        out_specs=[pl.BlockSpec((C, HG_WIDTH), lambda n: (n, 0)), pl.BlockSpec((HG_HEADS, 1, LANE, LANE), lambda n: (0, n, 0, 0))],
        out_shape=[SDS((S, HG_WIDTH), BF), SDS((HG_HEADS, NC, LANE, LANE), F32)],
        scratch_shapes=[pltpu.VMEM((HG_HEADS, LANE, LANE), F32)], name="hg_fwd")(*([proj] * 8), lb, nw)


def _hg_bwd(proj, lb, nw, states, do_hg, dproj):
    S = proj.shape[0]
    C = HG_CHUNK
    NC = S // C

    def body(*refs):
        sec = refs[:8]
        lb_ref, nw_ref, st_ref, do_ref, dp_in, dp_ref, dlb_ref, dnw_ref, dst_scr, stage, sem = refs[8:]
        n = pl.program_id(0)
        slot = n % 2

        def chunk_copies(s):
            rows = pl.ds(pl.multiple_of((NC - 1 - n) * C, C), C)
            return [pltpu.make_async_copy(stage.at[s, t], dp_ref.at[rows, pl.ds((QH0 + 8 * t) * LANE, HG_WIDTH)], sem.at[s, t])
                    for t in range(4)]

        @pl.when(n == 0)
        def _():
            dst_scr[...] = jnp.zeros_like(dst_scr)
            dlb_ref[...] = jnp.zeros_like(dlb_ref)
            dnw_ref[...] = jnp.zeros_like(dnw_ref)

        _, vjp = jax.vjp(_hg_chunk_all, *_hg_sections(sec), lb_ref[...], nw_ref[...], st_ref[:, 0])
        dq, df, di, dg, dlb, dnw, dst = vjp((do_ref[...].astype(F32), dst_scr[...]))
        dlb_ref[...] += dlb
        dnw_ref[...] += dnw
        dst_scr[...] = dst

        @pl.when(n >= 2)
        def _():
            for cp in chunk_copies(slot):
                cp.wait()

        for t, val in enumerate((dq, df, di, dg)):
            stage[slot, t] = val.astype(BF)
        for cp in chunk_copies(slot):
            cp.start()

        @pl.when(n == NC - 1)
        def _():
            for cp in chunk_copies(1 - slot) + chunk_copies(slot):
                cp.wait()

    rev = lambda n: NC - 1 - n
    ospec = pl.BlockSpec((C, HG_WIDTH), lambda n: (rev(n), 0))
    lbs = pl.BlockSpec((2, HG_WIDTH), lambda n: (0, 0))
    nws = pl.BlockSpec((1, LANE), lambda n: (0, 0))
    outs = _pcall(
        body, grid=(NC,),
        in_specs=_hg_in_specs(rev) + [lbs, nws, pl.BlockSpec((HG_HEADS, 1, LANE, LANE), lambda n: (0, rev(n), 0, 0)), ospec, _ANY],
        out_specs=[_ANY, lbs, nws],
        out_shape=[SDS((S, IN_WIDTH), BF), SDS((2, HG_WIDTH), F32), SDS((1, LANE), F32)], aliases={12: 0},
        scratch_shapes=[pltpu.VMEM((HG_HEADS, LANE, LANE), F32), pltpu.VMEM((2, 4, C, HG_WIDTH), BF), pltpu.SemaphoreType.DMA((2, 4))],
        name="hg_bwd")(*([proj] * 8), lb, nw, states, do_hg, dproj)
    return outs


def _merge(ga, gb, ba, bb):
    return jax.nn.sigmoid(ga) * ba + jax.nn.sigmoid(gb) * bb


_MT = 512


def _gate_specs():
    ga = pl.BlockSpec((_MT, _MT), lambda i, j: (i, GA0 * LANE // _MT + j))
    gb = pl.BlockSpec((_MT, _MT), lambda i, j: (i, GB0 * LANE // _MT + j))
    t = pl.BlockSpec((_MT, _MT), lambda i, j: (i, j))
    return ga, gb, t


def _merge_fwd(proj, ba, bb):
    S, D = ba.shape
    ga, gb, t = _gate_specs()

    def body(ga_ref, gb_ref, ba_ref, bb_ref, o_ref):
        o_ref[...] = _merge(ga_ref[...].astype(F32), gb_ref[...].astype(F32), ba_ref[...].astype(F32),
                            bb_ref[...].astype(F32)).astype(o_ref.dtype)

    return _pcall(body, grid=(S // _MT, D // _MT), in_specs=[ga, gb, t, t], out_specs=t,
                          out_shape=SDS((S, D), BF), name="merge_fwd")(proj, proj, ba, bb)


def _merge_bwd(proj, ba, bb, dm):
    S, D = ba.shape
    ga, gb, t = _gate_specs()

    def body(ga_ref, gb_ref, ba_ref, bb_ref, dm_ref, dba_ref, dbb_ref, dp_ref, stage, sem):
        _, vjp = jax.vjp(_merge, ga_ref[...].astype(F32), gb_ref[...].astype(F32), ba_ref[...].astype(F32),
                         bb_ref[...].astype(F32))
        dga, dgb, dba, dbb = vjp(dm_ref[...].astype(F32))
        dba_ref[...] = dba.astype(BF)
        dbb_ref[...] = dbb.astype(BF)
        stage[0] = dga.astype(BF)
        stage[1] = dgb.astype(BF)
        rows = pl.ds(pl.multiple_of(pl.program_id(0) * _MT, _MT), _MT)
        cps = []
        for k, base in enumerate((GA0, GB0)):
            cols = pl.ds(pl.multiple_of(base * LANE + pl.program_id(1) * _MT, LANE), _MT)
            cps.append(pltpu.make_async_copy(stage.at[k], dp_ref.at[rows, cols], sem.at[k]))
            cps[-1].start()
        for cp in cps:
            cp.wait()

    return _pcall(body, grid=(S // _MT, D // _MT), in_specs=[ga, gb, t, t, t], out_specs=[t, t, _ANY],
                  out_shape=[SDS((S, D), BF), SDS((S, D), BF), SDS((S, IN_WIDTH), BF)],
                  scratch_shapes=[pltpu.VMEM((2, _MT, _MT), BF), pltpu.SemaphoreType.DMA((2,))],
                  name="merge_bwd")(proj, proj, ba, bb, dm)


def _swiglu(a, b):
    return a * jax.nn.sigmoid(a) * b


def _swiglu_fwd(a, b):
    S, F = a.shape
    tf = _pick(F, (1408, 512))
    t = pl.BlockSpec((_MT, tf), lambda i, j: (i, j))

    def body(a_ref, b_ref, o_ref):
        o_ref[...] = _swiglu(a_ref[...].astype(F32), b_ref[...].astype(F32)).astype(o_ref.dtype)

    return _pcall(body, grid=(S // _MT, F // tf), in_specs=[t, t], out_specs=t,
                          out_shape=SDS((S, F), BF), name="swiglu_fwd")(a, b)


def _swiglu_bwd(a, b, dg):
    S, F = a.shape
    tf = _pick(F, (1408, 512))
    t = pl.BlockSpec((_MT, tf), lambda i, j: (i, j))

    def body(a_ref, b_ref, dg_ref, da_ref, db_ref):
        _, vjp = jax.vjp(_swiglu, a_ref[...].astype(F32), b_ref[...].astype(F32))
        da, db = vjp(dg_ref[...].astype(F32))
        da_ref[...] = da.astype(BF)
        db_ref[...] = db.astype(BF)

    return _pcall(body, grid=(S // _MT, F // tf), in_specs=[t, t, t], out_specs=[t, t],
                          out_shape=[SDS((S, F), BF)] * 2, name="swiglu_bwd")(a, b, dg)


def _cross(q, k, v):
    s = _bdot(q, k, ((1,), (1,))) * (HEAD_DIM ** -0.5)
    e = jnp.exp(s - jnp.max(s, axis=-1, keepdims=True))
    p = e / jnp.sum(e, axis=-1, keepdims=True)
    return _bdot(p, v, ((1,), (0,)))


def _cross_specs():
    q = pl.BlockSpec((_MT, LANE), lambda h, i: (i, h))
    k = pl.BlockSpec((MEM_LEN, LANE), lambda h, i: (0, h))
    v = pl.BlockSpec((MEM_LEN, LANE), lambda h, i: (0, CROSS_HEADS + h))
    return q, k, v


def _cross_fwd(qc, kvc):
    S = qc.shape[0]
    q, k, v = _cross_specs()

    def body(q_ref, k_ref, v_ref, o_ref):
        o_ref[...] = _cross(q_ref[...], k_ref[...], v_ref[...]).astype(o_ref.dtype)

    return _pcall(body, grid=(CROSS_HEADS, S // _MT), in_specs=[q, k, v], out_specs=q,
                          out_shape=SDS((S, CROSS_WIDTH), BF), name="cross_fwd")(qc, kvc, kvc)


def _cross_bwd(qc, kvc, doc):
    S = qc.shape[0]
    q, k, v = _cross_specs()

    def body(q_ref, k_ref, v_ref, do_ref, dq_ref, dk_ref, dv_ref):
        @pl.when(pl.program_id(1) == 0)
        def _():
            dk_ref[...] = jnp.zeros_like(dk_ref)
            dv_ref[...] = jnp.zeros_like(dv_ref)

        _, vjp = jax.vjp(_cross, q_ref[...], k_ref[...], v_ref[...])
        dq, dk, dv = vjp(do_ref[...])
        dq_ref[...] = dq.astype(dq_ref.dtype)
        dk_ref[...] += dk
        dv_ref[...] += dv

    return _pcall(body, grid=(CROSS_HEADS, S // _MT), in_specs=[q, k, v, q], out_specs=[q, k, k],
                          out_shape=[SDS((S, CROSS_WIDTH), BF), SDS((MEM_LEN, CROSS_WIDTH), F32), SDS((MEM_LEN, CROSS_WIDTH), F32)],
                          name="cross_bwd")(qc, kvc, kvc, doc)


def _local_step(x, mem, tgt, p, ready):
    h = _rms_fwd(x, p["ln_mix"], "rms_mix")
    proj = _mm(h, p["w_in_a"], "nt", BF, "mm_proj_a", a_part=(0, 2))
    proj = _mm(h, p["w_in_b"], "nt", BF, "mm_proj_b", a_part=(1, 2), res=proj)
    o_att, lse = _attn_fwd(proj)
    o_hg, states = _hg_fwd(proj, p["hg_lb"], p["hg_norm"])
    ba = _mm(o_att, p["w_ba"], "nn", BF, "mm_ba")
    bb = _mm(o_hg, p["w_bb"], "nn", BF, "mm_bb")
    merged = _merge_fwd(proj, ba, bb)
    x1 = _mm(merged, p["w_out"], "nn", F32, "mm_out", res=x)
    hc = _rms_fwd(x1, p["ln_cross"], "rms_cross")
    qc = _mm(hc, p["wq"], "nn", F32, "mm_q")
    mn = _rms_fwd(mem, p["ln_mem"], "rms_mem")
    kvc = _mm(mn, p["wkv"], "nn", F32, "mm_kv")
    oc = _cross_fwd(qc, kvc)
    x2 = _mm(oc, p["wo"], "nn", F32, "mm_o", res=x1)
    hf = _rms_fwd(x2, p["ln_ffn"], "rms_ffn")
    a = _mm(hf, p["w1"], "nt", BF, "mm_w1")
    b = _mm(hf, p["w3"], "nt", BF, "mm_w3")
    g = _swiglu_fwd(a, b)
    x3 = _mm(g, p["w2"], "nn", F32, "mm_w2", res=x2)
    loss, dx3, dx3b, d_ln_final = _final_loss(x3, p["ln_final"], tgt)

    gw, gs = {}, {"ln_final": d_ln_final}
    dg = _mm(dx3b, p["w2"], "nt", BF, "mm_dg")
    gw["w2"] = _mm(g, dx3b, "tn", BF, "mm_dw2")
    da, db = _swiglu_bwd(a, b, dg)
    dhf = _mm(da, p["w1"], "nn", F32, "mm_dhf1")
    dhf = _mm(db, p["w3"], "nn", BF, "mm_dhf3", res=dhf)
    gw["w1"] = _mm(da, hf, "tn", BF, "mm_dw1")
    gw["w3"] = _mm(db, hf, "tn", BF, "mm_dw3")
    ready("ffn", gw)
    dx2, dx2b, gs["ln_ffn"] = _rms_bwd(x2, p["ln_ffn"], dhf, dx3, "rms_ffn_bwd")
    doc = _mm(dx2b, p["wo"], "nt", F32, "mm_doc")
    gw["wo"] = _mm(oc, dx2b, "tn", BF, "mm_dwo")
    dqc, dk, dv = _cross_bwd(qc, kvc, doc)
    dkvc = jnp.concatenate([dk, dv], axis=1)
    dhc = _mm(dqc, p["wq"], "nt", BF, "mm_dhc")
    gw["wq"] = _mm(hc, dqc, "tn", BF, "mm_dwq")
    dmn = _mm(dkvc, p["wkv"], "nt", F32, "mm_dmn")
    gw["wkv"] = _mm(mn, dkvc, "tn", BF, "mm_dwkv")
    ready("cross", gw)
    gs["ln_mem"] = _rms_bwd(mem, p["ln_mem"], dmn, None, "rms_mem_bwd")
    dx1, dx1b, gs["ln_cross"] = _rms_bwd(x1, p["ln_cross"], dhc, dx2, "rms_cross_bwd")
    dmerged = _mm(dx1b, p["w_out"], "nt", BF, "mm_dmerged")
    gw["w_out"] = _mm(merged, dx1b, "tn", BF, "mm_dwout")
    dba, dbb, dproj = _merge_bwd(proj, ba, bb, dmerged)
    do_att = _mm(dba, p["w_ba"], "nt", F32, "mm_doatt")
    gw["w_ba"] = _mm(o_att, dba, "tn", BF, "mm_dwba")
    do_hg = _mm(dbb, p["w_bb"], "nt", F32, "mm_dohg")
    gw["w_bb"] = _mm(o_hg, dbb, "tn", BF, "mm_dwbb")
    ready("mix", gw)
    dproj = _attn_bwd(proj, o_att, lse, do_att, dproj)
    dproj, gs["hg_lb"], gs["hg_norm"] = _hg_bwd(proj, p["hg_lb"], p["hg_norm"], states, do_hg, dproj)
    ready("mixers_done", gw)
    for q in range(IN_PARTS):
        gw[f"w_in_q{q}"] = _mm(dproj, h, "tn", BF, f"mm_dwin_q{q}", b_part=(q, IN_PARTS))
        ready(f"in_q{q}", gw)
    dh_a = _mm(dproj, p["w_in_a"], "nn", BF, "mm_dh_a")
    ready("dh_half", gw)
    dh = (dh_a, _mm(dproj, p["w_in_b"], "nn", BF, "mm_dh_b"))
    grad_x, _, gs["ln_mix"] = _rms_bwd(x, p["ln_mix"], dh, dx1, "rms_mix_bwd")
    return loss, grad_x, gs


_MATS = (
    ("w_in_a", "w_in", IN_WIDTH, D_MODEL // 2, "row", IN_WIDTH // N_DEV, True),
    ("w_in_b", "w_in", IN_WIDTH, D_MODEL // 2, "row", IN_WIDTH // N_DEV, True),
    ("w_ba", "w_branch_a", ATT_OUT, D_MODEL, "col", D_MODEL // N_DEV, False),
    ("w_bb", "w_branch_b", HG_WIDTH, D_MODEL, "col", D_MODEL // N_DEV, False),
    ("w_out", "w_out", D_MODEL, D_MODEL, "row", D_MODEL // N_DEV, False),
    ("wq", "wq_cross", D_MODEL, CROSS_WIDTH, "row", D_MODEL // N_DEV, False),
    ("wkv", "wkv_cross", D_MODEL, 2 * CROSS_WIDTH, "row", D_MODEL // N_DEV, False),
    ("wo", "wo_cross", CROSS_WIDTH, D_MODEL, "col", D_MODEL // N_DEV, False),
    ("w1", "w1", D_FF, D_MODEL, "row", D_FF // N_DEV, True),
    ("w3", "w3", D_FF, D_MODEL, "row", D_FF // N_DEV, True),
    ("w2", "w2", D_FF, D_MODEL, "row", D_FF // N_DEV, False),
    ("w_in_q0", "w_in", IN_WIDTH, D_MODEL // 2, "row", IN_WIDTH // N_DEV, True),
    ("w_in_q1", "w_in", IN_WIDTH, D_MODEL // 2, "row", IN_WIDTH // N_DEV, True),
)
IN_PARTS = 2
_INDEX = {m[0]: i for i, m in enumerate(_MATS)}


def _unit_shape(i):
    _, _, K, N, kind, sh, _ = _MATS[i]
    return (sh, N) if kind == "row" else (K, sh)


def _win(ref, i, chip, par, half=None):
    _, _, K, _, kind, sh, _ = _MATS[i]
    if kind == "row":
        rows, start = (sh, 0) if half is None else (sh // 2, half * (sh // 2))
        return ref.at[pl.ds(pl.multiple_of((2 * chip + par) * sh + start, 16), rows), :]
    rows, start = (K, 0) if half is None else (K // 2, half * (K // 2))
    return ref.at[pl.ds(start, rows), pl.ds(pl.multiple_of((2 * chip + par) * sh, LANE), sh)]


def _my_place():
    x, y, c = lax.axis_index("x"), lax.axis_index("y"), lax.axis_index("c")
    chips = [(1 - x, y), (x, 1 - y), (1 - x, 1 - y)]
    return x, y, c, chips


def _cast_unit(w, name, half=None):
    K, N = w.shape
    if half is not None:
        N = N // 2
    tr = _pick(K, (256, 400, 352))

    def body(w_ref, o_ref):
        o_ref[...] = w_ref[...].astype(BF)

    src = pl.BlockSpec((tr, N), lambda i: (i, half or 0))
    dst = pl.BlockSpec((tr, N), lambda i: (i, 0))
    return _pcall(body, grid=(K // tr,), in_specs=[src], out_specs=dst, out_shape=SDS((K, N), BF), name=name)(w)


def _handshake(peers):
    barrier = pltpu.get_barrier_semaphore()
    for peer in peers:
        pl.semaphore_signal(barrier, inc=1, device_id=peer, device_id_type=MESH)
    pl.semaphore_wait(barrier, len(peers))


def _sequencer(body, out_type, sems, cid, name):
    return pl.kernel(body, out_type=out_type, mesh=plsc.ScalarSubcoreMesh(axis_name="seq", num_cores=1),
                     scratch_types=sems, compiler_params=pltpu.CompilerParams(collective_id=cid), name=name)


def _all_gather(units, idx, cid, name):
    n = len(idx)

    def body(*refs):
        us, fulls = refs[:n], refs[n:2 * n]
        ssem, rsem, lsem = refs[2 * n:]
        x, y, c, _ = _my_place()
        xn, yn, dg = (1 - x, y), (x, 1 - y), (1 - x, 1 - y)
        me, sib = (x, y, c), (x, y, 1 - c)
        _handshake([sib, (*xn, c), (*yn, c)])

        def win(i, chip_xy, par, half=None):
            return _win(fulls[i], idx[i], 2 * chip_xy[0] + chip_xy[1], par, half)

        def rcopy(i, k, src, dst, to):
            return pltpu.make_async_remote_copy(src_ref=src, dst_ref=dst, send_sem=ssem.at[i, k], recv_sem=rsem.at[i, k],
                                                device_id=to, device_id_type=MESH)

        def landed(i, k, blk):
            rcopy(i, k, blk, blk, me).wait_recv()

        sends, locs = [], []

        def send(i, k, blk, to, src=None):
            cp = rcopy(i, k, blk if src is None else src, blk, to)
            cp.start()
            sends.append(cp)

        for i in range(n):
            mine = win(i, (x, y), c)
            loc = pltpu.make_async_copy(us[i], mine, lsem.at[i])
            loc.start()
            locs.append(loc)
            send(i, 0, mine, sib, src=us[i])
            send(i, 1, mine, (*xn, c), src=us[i])
            send(i, 2, mine, (*yn, c), src=us[i])
        for i in range(n):
            landed(i, 1, win(i, xn, c))
            send(i, 3, win(i, xn, c, 0), (*yn, c))
            send(i, 5, win(i, xn, c), sib)
            landed(i, 2, win(i, yn, c))
            send(i, 4, win(i, yn, c, 1), (*xn, c))
            send(i, 6, win(i, yn, c), sib)
        for i in range(n):
            landed(i, 3, win(i, dg, c, 0))
            landed(i, 4, win(i, dg, c, 1))
            send(i, 7, win(i, dg, c), sib)
        for i in range(n):
            landed(i, 0, win(i, (x, y), 1 - c))
            landed(i, 5, win(i, xn, 1 - c))
            landed(i, 6, win(i, yn, 1 - c))
            landed(i, 7, win(i, dg, 1 - c))
        for cp in sends:
            cp.wait_send()
        for loc in locs:
            loc.wait()

    out_type = [SDS((_MATS[i][2], _MATS[i][3]), BF) for i in idx]
    sems = [pltpu.SemaphoreType.DMA((n, 8)), pltpu.SemaphoreType.DMA((n, 8)), pltpu.SemaphoreType.DMA((n,))]
    return _sequencer(body, out_type, sems, cid, name)(*units)


def _rs_sibling(grads, idx, cid, name):
    n = len(idx)

    def body(*refs):
        gs, bufs = refs[:n], refs[n:2 * n]
        ssem, rsem = refs[2 * n:]
        x, y, c, _ = _my_place()
        _handshake([(x, y, 1 - c)])
        cps = []
        for i in range(n):
            for k in range(4):
                cp = pltpu.make_async_remote_copy(
                    src_ref=_win(gs[i], idx[i], k, 1 - c), dst_ref=bufs[i].at[k], send_sem=ssem.at[i, k], recv_sem=rsem.at[i, k],
                    device_id=(x, y, 1 - c), device_id_type=MESH)
                cp.start()
                cps.append(cp)
        for cp in cps:
            cp.wait()

    out_type = [SDS((4, *_unit_shape(i)), BF) for i in idx]
    sems = [pltpu.SemaphoreType.DMA((n, 4)), pltpu.SemaphoreType.DMA((n, 4))]
    return _sequencer(body, out_type, sems, cid, name)(*grads)


def _rs_sibling_tc(grad, i, cid, name):
    def body(g_ref, buf_ref, ssem, rsem):
        x, y, c, _ = _my_place()
        _handshake([(x, y, 1 - c)])
        cps = []
        for k in range(4):
            cp = pltpu.make_async_remote_copy(
                src_ref=_win(g_ref, i, k, 1 - c), dst_ref=buf_ref.at[k], send_sem=ssem.at[k], recv_sem=rsem.at[k],
                device_id=(x, y, 1 - c), device_id_type=MESH)
            cp.start()
            cps.append(cp)
        for cp in cps:
            cp.wait()

    return _pcall(body, in_specs=[_ANY], out_specs=_ANY, out_shape=SDS((4, *_unit_shape(i)), BF),
                  scratch_shapes=[pltpu.SemaphoreType.DMA((4,)), pltpu.SemaphoreType.DMA((4,))],
                  compiler_params=_cp(collective_id=cid), name=name)(grad)


def _chip_sum(g, sib, i, place, name):
    _, _, K, N, kind, sh, _ = _MATS[i]

    def body(pref, g_ref, s_ref, o_ref):
        o_ref[...] = (g_ref[...].astype(F32) + s_ref[...].astype(F32)).astype(BF)

    if kind == "row":
        tr = _pick(sh, (800, 352, 256))
        per = sh // tr
        grid = (4, per)
        g_spec = pl.BlockSpec((tr, N), lambda k, r, pref: ((2 * k + pref[0]) * per + r, 0))
        u_spec = pl.BlockSpec((None, tr, N), lambda k, r, pref: (k, r, 0))
    else:
        grid = (4,)
        g_spec = pl.BlockSpec((K, sh), lambda k, pref: (0, 2 * k + pref[0]))
        u_spec = pl.BlockSpec((None, K, sh), lambda k, pref: (k, 0, 0))
    return _pcall(body, grid=grid, in_specs=[g_spec, u_spec], out_specs=u_spec, out_shape=SDS((4, *_unit_shape(i)), BF),
                  prefetch=True, name=name)(place, g, sib)


def _rs_chips(parts, idx, cid, name):
    n = len(idx)

    def body(*refs):
        ts, bufs, relays = refs[:n], refs[n:2 * n], refs[2 * n:3 * n]
        ssem, rsem = refs[3 * n:]
        x, y, c, _ = _my_place()
        xn, yn, dg = (1 - x, y), (x, 1 - y), (1 - x, 1 - y)
        to_x, to_y, me = (*xn, c), (*yn, c), (x, y, c)
        _handshake([to_x, to_y])

        def rcopy(i, k, src, dst, to):
            return pltpu.make_async_remote_copy(src_ref=src, dst_ref=dst, send_sem=ssem.at[i, k], recv_sem=rsem.at[i, k],
                                                device_id=to, device_id_type=MESH)

        def halves(i):
            rows = _unit_shape(idx[i])[0] // 2
            return [pl.ds(0, rows), pl.ds(rows, rows)]

        sends = []

        def send(cp):
            cp.start()
            sends.append(cp)

        for i in range(n):
            h0, h1 = halves(i)
            diag = ts[i].at[2 * dg[0] + dg[1]]
            send(rcopy(i, 0, ts[i].at[2 * xn[0] + xn[1]], bufs[i].at[0], to_x))
            send(rcopy(i, 1, ts[i].at[2 * yn[0] + yn[1]], bufs[i].at[1], to_y))
            send(rcopy(i, 2, diag.at[h0], relays[i].at[0], to_x))
            send(rcopy(i, 3, diag.at[h1], relays[i].at[1], to_y))
        for i in range(n):
            h0, h1 = halves(i)
            rcopy(i, 2, relays[i].at[0], relays[i].at[0], me).wait_recv()
            send(rcopy(i, 4, relays[i].at[0], bufs[i].at[2, h0], to_y))
            rcopy(i, 3, relays[i].at[1], relays[i].at[1], me).wait_recv()
            send(rcopy(i, 5, relays[i].at[1], bufs[i].at[2, h1], to_x))
        for i in range(n):
            h0, h1 = halves(i)
            rcopy(i, 0, bufs[i].at[0], bufs[i].at[0], me).wait_recv()
            rcopy(i, 1, bufs[i].at[1], bufs[i].at[1], me).wait_recv()
            rcopy(i, 4, bufs[i].at[2, h0], bufs[i].at[2, h0], me).wait_recv()
            rcopy(i, 5, bufs[i].at[2, h1], bufs[i].at[2, h1], me).wait_recv()
        for cp in sends:
            cp.wait_send()

    def relay_shape(i):
        rows, cols = _unit_shape(i)
        return (2, rows // 2, cols)

    out_type = [SDS((3, *_unit_shape(i)), BF) for i in idx] + [SDS(relay_shape(i), BF) for i in idx]
    sems = [pltpu.SemaphoreType.DMA((n, 6)), pltpu.SemaphoreType.DMA((n, 6))]
    return _sequencer(body, out_type, sems, cid, name)(*parts)[:n]


def _adamw(w, g, m, v):
    m = ADAM_B1 * m + (1.0 - ADAM_B1) * g
    v = ADAM_B2 * v + (1.0 - ADAM_B2) * jnp.square(g)
    m_hat = m / (1.0 - ADAM_B1 ** ADAM_STEP)
    v_hat = v / (1.0 - ADAM_B2 ** ADAM_STEP)
    delta = -ADAM_LR * (m_hat / (jnp.sqrt(v_hat) + ADAM_EPS) + ADAM_WD * w)
    return delta, m, v


def _adam_mat(parts, recvs, w, m, v, place, name):
    R, C = w.shape
    n = len(parts)
    tr = _pick(R, (256, 160, 176))

    def body(pref, *refs):
        t_refs, r_refs = refs[:n], refs[n:2 * n]
        w_ref, m_ref, v_ref, g_out, d_out, m_out, v_out = refs[2 * n:]
        cols = [t[...].astype(F32) + r[0].astype(F32) + r[1].astype(F32) + r[2].astype(F32) for t, r in zip(t_refs, r_refs)]
        g = jnp.concatenate(cols, axis=1) if n > 1 else cols[0]
        d, mn, vn = _adamw(w_ref[...], g, m_ref[...], v_ref[...])
        g_out[...] = g
        d_out[...] = d
        m_out[...] = mn
        v_out[...] = vn

    t_spec = pl.BlockSpec((None, tr, C // n), lambda i, pref: (pref[1], i, 0))
    r_spec = pl.BlockSpec((3, tr, C // n), lambda i, pref: (0, i, 0))
    s_spec = pl.BlockSpec((tr, C), lambda i, pref: (i, 0))
    return _pcall(body, grid=(R // tr,), in_specs=[t_spec] * n + [r_spec] * n + [s_spec] * 3, out_specs=[s_spec] * 4,
                  out_shape=[SDS((R, C), F32)] * 4, prefetch=True, name=name)(place, *parts, *recvs, w, m, v)


_SMALL = (("ln_mix", "ln_mix_w", D_MODEL), ("ln_cross", "ln_cross_w", D_MODEL), ("ln_mem", "ln_mem_w", D_MODEL),
          ("ln_ffn", "ln_ffn_w", D_MODEL), ("ln_final", "ln_final_w", D_MODEL), ("hg_lb", "hg_lower_bounds", 2 * HG_WIDTH),
          ("hg_norm", "hg_norm_w", HEAD_DIM))
_PACK = sum(n for _, _, n in _SMALL) + LANE


def _small_sync(grow, wrow, mrow, vrow):
    def body(g_ref, w_ref, m_ref, v_ref, o_ref, gath, ssem, rsem):
        x, y, c, _ = _my_place()
        me = 4 * x + 2 * y + c
        gath[pl.ds(me, 1), :] = g_ref[...]
        cps = []
        for dlt in range(1, N_DEV):
            peer = (me + dlt) % N_DEV
            cp = pltpu.make_async_remote_copy(
                src_ref=g_ref, dst_ref=gath.at[pl.ds(me, 1), :], send_sem=ssem.at[peer], recv_sem=rsem.at[me],
                device_id=(peer // 4, (peer // 2) % 2, peer % 2), device_id_type=MESH)
            cp.start()
            cps.append(cp)
        for dlt in range(1, N_DEV):
            src = (me + dlt) % N_DEV
            pltpu.make_async_remote_copy(
                src_ref=g_ref, dst_ref=gath.at[pl.ds(src, 1), :], send_sem=ssem.at[src], recv_sem=rsem.at[src],
                device_id=(x, y, c), device_id_type=MESH).wait_recv()
        for cp in cps:
            cp.wait_send()
        g = gath[0:1, :]
        for j in range(1, N_DEV):
            g = g + gath[j:j + 1, :]
        d, mn, vn = _adamw(w_ref[...], g, m_ref[...], v_ref[...])
        o_ref[0:1, :] = g
        o_ref[1:2, :] = d
        o_ref[2:3, :] = mn
        o_ref[3:4, :] = vn

    vm = pl.BlockSpec(memory_space=pltpu.VMEM)
    return _pcall(
        body, in_specs=[vm] * 4, out_specs=vm, out_shape=SDS((4, _PACK), F32),
        scratch_shapes=[pltpu.VMEM((N_DEV, _PACK), F32), pltpu.SemaphoreType.DMA((N_DEV,)), pltpu.SemaphoreType.DMA((N_DEV,))],
        name="small_sync")(grow, wrow, mrow, vrow)


def _pack_small(d, loss_row=None):
    parts = [d[k].reshape(1, n).astype(F32) for k, _, n in _SMALL]
    parts.append(loss_row if loss_row is not None else jnp.ones((1, LANE), F32))
    return jnp.concatenate(parts, axis=1)


def kernel(x, mem, ln_mix_w, w_in, hg_norm_w, hg_lower_bounds, w_branch_a, w_branch_b, w_out, ln_cross_w, ln_mem_w, wq_cross, wkv_cross, wo_cross, ln_ffn_w, w1, w3, w2, ln_final_w, loss_target, m_ln_mix_w, m_w_in, m_hg_norm_w, m_hg_lower_bounds, m_w_branch_a, m_w_branch_b, m_w_out, m_ln_cross_w, m_ln_mem_w, m_wq_cross, m_wkv_cross, m_wo_cross, m_ln_ffn_w, m_w1, m_w3, m_w2, m_ln_final_w, v_ln_mix_w, v_w_in, v_hg_norm_w, v_hg_lower_bounds, v_w_branch_a, v_w_branch_b, v_w_out, v_ln_cross_w, v_ln_mem_w, v_wq_cross, v_wkv_cross, v_wo_cross, v_ln_ffn_w, v_w1, v_w3, v_w2, v_ln_final_w):
    given = dict(locals())
    place = jnp.stack([lax.axis_index("c"), 2 * lax.axis_index("x") + lax.axis_index("y")]).astype(jnp.int32)

    _ORDER[0] = None
    index = _INDEX

    def local_view(a, i):
        return a[0].T if _MATS[i][6] else a[0]

    shard = {key: local_view(given[ref], i) for i, (key, ref, *_) in enumerate(_MATS)}

    p = {}
    col_half = {"w_in_a": 0, "w_in_b": 1}
    waves = (("in_a", ["w_in_a"], 1), ("in_b", ["w_in_b"], 12), ("mid", ["w_ba", "w_bb", "w_out", "wq", "wkv", "wo"], 2),
             ("w1", ["w1"], 11), ("w3", ["w3"], 15), ("w2", ["w2"], 16))
    for wave, keys, cid in waves:
        units = [_cast_unit(shard[k], "cast_" + k, col_half.get(k)) for k in keys]
        p.update(zip(keys, _all_gather(units, [index[k] for k in keys], cid, "all_gather_" + wave)))
    small_w = {"ln_mix": ln_mix_w, "ln_cross": ln_cross_w, "ln_mem": ln_mem_w, "ln_ffn": ln_ffn_w,
               "ln_final": ln_final_w.reshape(1, D_MODEL), "hg_lb": hg_lower_bounds, "hg_norm": hg_norm_w}
    p.update(small_w)

    groups = {"ffn": ["w2", "w1", "w3"], "cross": ["wo", "wq", "wkv"], "mix": ["w_out", "w_ba", "w_bb"]}
    cids = {"ffn": (3, 4), "cross": (5, 6), "mix": (7, 8)}
    for q in range(IN_PARTS):
        groups[f"in_q{q}"] = [f"w_in_q{q}"]
        cids[f"in_q{q}"] = ((9, 10), (13, 14), (17, 18), (19, 20))[q]
    sib, parts, recv, out = {}, {}, {}, {}

    grads = {}

    def chip_stage(name):
        keys = groups[name]
        parts[name] = [_chip_sum(grads[k], s, index[k], place, "chip_sum_" + k) for k, s in zip(keys, sib[name])]
        recv[name] = list(_rs_chips(parts[name], [index[k] for k in keys], cids[name][1], "rs_chips_" + name))

    def adam(ref, i, part_list, recv_list):
        res = _adam_mat(part_list, recv_list, shard[_MATS[i][0]], local_view(given["m_" + ref], i), local_view(given["v_" + ref], i),
                        place, "adam_" + ref)
        out[ref] = [(a.T if _MATS[i][6] else a)[None] for a in res]

    def adam_stage(name):
        for k, t, r in zip(groups[name], parts[name], recv[name]):
            adam(_MATS[index[k]][1], index[k], [t], [r])

    def ready(name, gw):
        if name == "mixers_done":
            adam_stage("ffn")
            chip_stage("mix")
            adam_stage("cross")
            return
        if name == "dh_half":
            return
        keys = groups[name]
        grads.update({k: gw[k] for k in keys})
        if name.startswith("in_q"):
            if name == "in_q1":
                adam_stage("mix")
            sib[name] = [_rs_sibling_tc(gw[k], index[k], cids[name][0], "rs_sibling_" + k) for k in keys]
            chip_stage(name)
            return
        sib[name] = _rs_sibling([gw[k] for k in keys], [index[k] for k in keys], cids[name][0], "rs_sibling_" + name)
        if name == "cross":
            chip_stage("ffn")
        if name == "mix":
            chip_stage("cross")

    loss_row, grad_x, gs = _local_step(x[0], mem[0], loss_target[0], p, ready)

    sm = {k: given["m_" + ref] for k, ref, _ in _SMALL}
    sv = {k: given["v_" + ref] for k, ref, _ in _SMALL}
    synced = _small_sync(_pack_small(gs, loss_row), _pack_small(small_w), _pack_small(sm), _pack_small(sv))
    in_groups = [f"in_q{q}" for q in range(IN_PARTS)]
    adam("w_in", index["w_in_a"], sum((parts[g] for g in in_groups), []), sum((recv[g] for g in in_groups), []))
    off = 0
    for k, ref, n in _SMALL:
        shape = given[ref].shape
        out[ref] = [synced[r, off:off + n].reshape(shape) for r in range(4)]
        off += n
    loss = synced[0, off]

    order = ["ln_mix_w", "w_in", "hg_norm_w", "hg_lower_bounds", "w_branch_a", "w_branch_b", "w_out", "ln_cross_w", "ln_mem_w",
             "wq_cross", "wkv_cross", "wo_cross", "ln_ffn_w", "w1", "w3", "w2", "ln_final_w"]
    return (loss, grad_x[None], *[out[n][0] for n in order], *[out[n][1] for n in order],
            *[out[n][2] for n in order], *[out[n][3] for n in order])
```

```python
import functools
import math

import jax
import jax.numpy as jnp
from jax import lax
from jax.experimental import pallas as pl
from jax.experimental.pallas import tpu as pltpu
from jax.experimental.pallas import tpu_sc as plsc

F32 = jnp.float32
BF = jnp.bfloat16
SDS = jax.ShapeDtypeStruct

D_MODEL = 2048
SEQ = 2048
HEAD_DIM = 128
MEM_LEN = 256
ATT_GROUPS = ((128, 1), (512, 4), (2048, 16))
ATT_HEADS = 4
ATT_WIDTH = 1536
ATT_OUT = 512
HG_HEADS = 8
HG_WIDTH = 1024
HG_CHUNK = 64
IN_WIDTH = 12800
CROSS_HEADS = 4
CROSS_WIDTH = 512
D_FF = 5632
RMS_EPS = 1e-6
ADAM_LR = 0.001
ADAM_B1 = 0.9
ADAM_B2 = 0.999
ADAM_EPS = 1e-08
ADAM_WD = 0.01
ADAM_STEP = 10
N_DEV = 8

LANE = 128
QA0, KA0, VA0 = 0, 12, 24
QH0, FH0, IH0, GH0 = 36, 44, 52, 60
GA0, GB0 = 68, 84

VMEM_BIG = 56 * 1024 * 1024
MESH = pl.DeviceIdType.MESH


def _cp(vmem=None, **kw):
    if vmem is not None:
        kw["vmem_limit_bytes"] = vmem
    return pltpu.CompilerParams(**kw)


def _pick(n, cands):
    for c in cands:
        if n % c == 0:
            return c
    raise ValueError(f"no tile for {n}")


_ANY = pl.BlockSpec(memory_space=pl.ANY)
_ORDER = [None]


def _pcall(body, *, name, in_specs, out_specs, out_shape, grid=(), scratch_shapes=(), compiler_params=None, prefetch=False,
           aliases=None):
    def run(*args):
        dep = _ORDER[0]
        specs, operands, kernel_fn = list(in_specs), list(args), body
        if dep is not None and not any(dep is a for a in operands):
            at = len(operands)
            specs.append(_ANY)
            operands.append(dep)

            def kernel_fn(*refs):
                return body(*refs[:at], *refs[at + 1:])
        if prefetch:
            call = pl.pallas_call(
                kernel_fn, out_shape=out_shape, name=name, compiler_params=compiler_params,
                grid_spec=pltpu.PrefetchScalarGridSpec(num_scalar_prefetch=1, grid=grid, in_specs=specs, out_specs=out_specs,
                                                       scratch_shapes=scratch_shapes))
        else:
            call = pl.pallas_call(kernel_fn, grid=grid, in_specs=specs, out_specs=out_specs, out_shape=out_shape,
                                  scratch_shapes=scratch_shapes, compiler_params=compiler_params, name=name,
                                  input_output_aliases=aliases or {})
        out = call(*operands)
        _ORDER[0] = out[0] if isinstance(out, (list, tuple)) else out
        return out
    return run


def _mm(a, b, form, out_dtype, name, res=None, tiles=None, a_part=(0, 1), b_part=(0, 1)):
    a_shape = (a.shape[0], a.shape[1] // a_part[1])
    b_shape = (b.shape[0], b.shape[1] // b_part[1])
    if form == "nn":
        (M, K), (K2, N) = a_shape, b_shape
    elif form == "nt":
        (M, K), (N, K2) = a_shape, b_shape
    else:
        (K, M), (K2, N) = a_shape, b_shape
    assert K == K2, (a.shape, b.shape, form)
    if tiles is None:
        tm = _pick(M, (1024, 1280, 1408, 512, 256))
        tn = _pick(N, (1280, 1024, 1408, 512, 256))
        tk = K if K <= 2048 else _pick(K, (D_FF, 2560))
        if tk == D_FF:
            tm = _pick(M, (512, 256))
    else:
        tm, tn, tk = tiles
    nk = K // tk
    ah, bh = a_part[0], b_part[0]
    if form == "nn":
        a_spec = pl.BlockSpec((tm, tk), lambda i, j, k: (i, k + ah * nk))
        b_spec = pl.BlockSpec((tk, tn), lambda i, j, k: (k, j + bh * (N // tn)))
        dims = ((1,), (0,))
    elif form == "nt":
        a_spec = pl.BlockSpec((tm, tk), lambda i, j, k: (i, k + ah * nk))
        b_spec = pl.BlockSpec((tn, tk), lambda i, j, k: (j, k + bh * nk))
        dims = ((1,), (1,))
    else:
        a_spec = pl.BlockSpec((tk, tm), lambda i, j, k: (k, i + ah * (M // tm)))
        b_spec = pl.BlockSpec((tk, tn), lambda i, j, k: (k, j + bh * (N // tn)))
        dims = ((0,), (0,))
    o_spec = pl.BlockSpec((tm, tn), lambda i, j, k: (i, j))
    in_specs = [a_spec, b_spec]
    args = [a, b]
    if res is not None:
        in_specs.append(o_spec)
        args.append(res)

    def body(*refs):
        a_ref, b_ref = refs[0], refs[1]
        r_ref = refs[2] if res is not None else None
        o_ref = refs[3] if res is not None else refs[2]
        acc = refs[-1] if nk > 1 else None
        k = pl.program_id(2)
        d = lax.dot_general(a_ref[...].astype(BF), b_ref[...].astype(BF), (dims, ((), ())), preferred_element_type=F32)

        def finish(r):
            if res is not None:
                r = r + r_ref[...].astype(F32)
            o_ref[...] = r.astype(o_ref.dtype)

        if nk == 1:
            finish(d)
        else:
            @pl.when(k == 0)
            def _():
                acc[...] = d

            @pl.when((k > 0) & (k < nk - 1))
            def _():
                acc[...] += d

            @pl.when(k == nk - 1)
            def _():
                finish(acc[...] + d)

    return _pcall(
        body, grid=(M // tm, N // tn, nk), in_specs=in_specs, out_specs=o_spec,
        out_shape=SDS((M, N), out_dtype), scratch_shapes=[pltpu.VMEM((tm, tn), F32)] if nk > 1 else [],
        compiler_params=_cp(VMEM_BIG, dimension_semantics=("parallel", "parallel", "arbitrary")),
        name=name)(*args)


def _rms(x, w):
    return x * lax.rsqrt(jnp.mean(x * x, axis=-1, keepdims=True) + RMS_EPS) * w


def _rms_fwd(x, w, name):
    R, D = x.shape
    tr = 256

    def body(x_ref, w_ref, o_ref):
        o_ref[...] = _rms(x_ref[...], w_ref[...]).astype(o_ref.dtype)

    return _pcall(
        body, grid=(R // tr,),
        in_specs=[pl.BlockSpec((tr, D), lambda i: (i, 0)), pl.BlockSpec((1, D), lambda i: (0, 0))],
        out_specs=pl.BlockSpec((tr, D), lambda i: (i, 0)), out_shape=SDS((R, D), BF), name=name)(x, w)


def _rms_bwd(x, w, dh, dres, name):
    R, D = x.shape
    tr = 256
    want_dx = dres is not None
    dhs = dh if isinstance(dh, (tuple, list)) else (dh,)
    nd = len(dhs)

    def body(*refs):
        x_ref, w_ref, dh_refs = refs[0], refs[1], refs[2:2 + nd]
        if want_dx:
            dr_ref, dx_ref, dxb_ref, dw_ref = refs[2 + nd:]
        else:
            dw_ref = refs[2 + nd]

        @pl.when(pl.program_id(0) == 0)
        def _():
            dw_ref[...] = jnp.zeros_like(dw_ref)

        dhv = jnp.concatenate([r[...].astype(F32) for r in dh_refs], axis=1) if nd > 1 else dh_refs[0][...].astype(F32)
        _, vjp = jax.vjp(_rms, x_ref[...], w_ref[...])
        dx, dw = vjp(dhv)
        dw_ref[...] += dw
        if want_dx:
            dx = dx + dr_ref[...]
            dx_ref[...] = dx
            dxb_ref[...] = dx.astype(BF)

    row = pl.BlockSpec((tr, D), lambda i: (i, 0))
    part = pl.BlockSpec((tr, D // nd), lambda i: (i, 0))
    vec = pl.BlockSpec((1, D), lambda i: (0, 0))
    if want_dx:
        return _pcall(body, grid=(R // tr,), in_specs=[row, vec] + [part] * nd + [row], out_specs=[row, row, vec],
                      out_shape=[SDS((R, D), F32), SDS((R, D), BF), SDS((1, D), F32)], name=name)(x, w, *dhs, dres)
    return _pcall(body, grid=(R // tr,), in_specs=[row, vec] + [part] * nd, out_specs=vec,
                  out_shape=SDS((1, D), F32), name=name)(x, w, *dhs)


def _final_loss(x3, w, tgt):
    R, D = x3.shape
    tr = 256

    def loss_fn(xv, wv, tv):
        err = _rms(xv, wv) - tv
        return 0.5 * jnp.sum(jnp.mean(err * err, axis=-1))

    def body(x_ref, w_ref, t_ref, loss_ref, dx_ref, dxb_ref, dw_ref):
        @pl.when(pl.program_id(0) == 0)
        def _():
            dw_ref[...] = jnp.zeros_like(dw_ref)
            loss_ref[...] = jnp.zeros_like(loss_ref)

        tv = t_ref[...]
        val, vjp = jax.vjp(lambda a, b: loss_fn(a, b, tv), x_ref[...], w_ref[...])
        dx, dw = vjp(jnp.ones((), F32))
        dx_ref[...] = dx
        dxb_ref[...] = dx.astype(BF)
        dw_ref[...] += dw
        loss_ref[...] += jnp.full(loss_ref.shape, val, F32)

    row = pl.BlockSpec((tr, D), lambda i: (i, 0))
    vec = pl.BlockSpec((1, D), lambda i: (0, 0))
    one = pl.BlockSpec((1, LANE), lambda i: (0, 0))
    return _pcall(body, grid=(R // tr,), in_specs=[row, vec, row], out_specs=[one, row, row, vec],
                          out_shape=[SDS((1, LANE), F32), SDS((R, D), F32), SDS((R, D), BF), SDS((1, D), F32)],
                          name="final_loss")(x3, w, tgt)


NEG = -1e30


def _att_blocks(S, d):
    L = S // d
    nb = L // LANE
    return nb, (2 if nb > 1 else 1)


def _att_rows(S, d, idx):
    nb, nkb = _att_blocks(S, d)
    r = idx // nb
    n = idx % nb
    kb = jnp.maximum(n - 1, 0)
    if d == 1:
        qrows = pl.ds(pl.multiple_of(n * LANE, LANE), LANE)
        krows = pl.ds(pl.multiple_of(kb * LANE, LANE), LANE * nkb)
    else:
        qrows = pl.ds(r + n * (LANE * d), LANE, stride=d)
        krows = pl.ds(r + kb * (LANE * d), LANE * nkb, stride=d)
    qpos = n * LANE + lax.broadcasted_iota(jnp.int32, (LANE, LANE * nkb), 0)
    kpos = kb * LANE + lax.broadcasted_iota(jnp.int32, (LANE, LANE * nkb), 1)
    return qrows, krows, qpos - kpos


def _slab(col):
    return pl.BlockSpec((SEQ, LANE), lambda h, col=col: (0, col + h))


def _widen(slabs, wide):
    for t, ref in enumerate(slabs):
        wide[t] = ref[...].astype(F32)
    return tuple(tuple(wide.at[3 * kind + g] for g in range(3)) for kind in range(3))


def _attn_fwd(proj):
    S = proj.shape[0]
    scale = HEAD_DIM ** -0.5

    def body(q0, q1, q2, k0, k1, k2, v0, v1, v2, o_ref, lse_ref, og, lg, wide):
        qs, ks, vs = _widen((q0, q1, q2, k0, k1, k2, v0, v1, v2), wide)
        for g, (window, d) in enumerate(ATT_GROUPS):
            def blk(idx, carry, g=g, window=window, d=d):
                qrows, krows, dist = _att_rows(S, d, idx)
                q = qs[g][qrows, :].astype(BF)
                k = ks[g][krows, :].astype(BF)
                v = vs[g][krows, :].astype(BF)
                s = lax.dot_general(q, k, (((1,), (1,)), ((), ())), preferred_element_type=F32) * scale
                s = jnp.where((dist >= 0) & (dist <= window // d), s, NEG)
                m = jnp.max(s, axis=-1, keepdims=True)
                p = jnp.exp(s - m)
                l = jnp.sum(p, axis=-1, keepdims=True)
                o = jnp.dot((p / l).astype(BF), v, preferred_element_type=F32)
                og[g, qrows, :] = o
                lg[g, qrows, :] = jnp.broadcast_to(m + jnp.log(l), (LANE, LANE))
                return carry
            lax.fori_loop(0, S // LANE, blk, 0, unroll=8)

        def merge(t, carry):
            rows = pl.ds(pl.multiple_of(t * 256, 256), 256)
            l0, l1, l2 = lg[0, rows, :], lg[1, rows, :], lg[2, rows, :]
            m = jnp.maximum(jnp.maximum(l0, l1), l2)
            w0, w1, w2 = jnp.exp(l0 - m), jnp.exp(l1 - m), jnp.exp(l2 - m)
            den = w0 + w1 + w2
            o_ref[rows, :] = (w0 * og[0, rows, :] + w1 * og[1, rows, :] + w2 * og[2, rows, :]) / den
            lse_ref[rows, :] = m + jnp.log(den)
            return carry
        lax.fori_loop(0, S // 256, merge, 0)

    ins = [_slab(QA0 + 4 * g) for g in range(3)] + [_slab(KA0 + 4 * g) for g in range(3)] + [_slab(VA0 + 4 * g) for g in range(3)]
    out = pl.BlockSpec((SEQ, LANE), lambda h: (0, h))
    return _pcall(
        body, grid=(ATT_HEADS,), in_specs=ins, out_specs=[out, out],
        out_shape=[SDS((S, ATT_OUT), F32), SDS((S, ATT_OUT), F32)],
        scratch_shapes=[pltpu.VMEM((3, S, LANE), F32), pltpu.VMEM((3, S, LANE), F32), pltpu.VMEM((9, S, LANE), F32)],
        compiler_params=_cp(VMEM_BIG), name="attn_fwd")(*([proj] * 9))


def _attn_bwd(proj, o_att, lse, do_att, dproj):
    S = proj.shape[0]
    scale = HEAD_DIM ** -0.5

    def body(q0, q1, q2, k0, k1, k2, v0, v1, v2, o_ref, lse_ref, do_ref, dp_in, dp_ref, dqa, dka, dva, delta, stage, sem, wide):
        qs, ks, vs = _widen((q0, q1, q2, k0, k1, k2, v0, v1, v2), wide)
        head = pl.program_id(0)

        def slab_copies(g):
            return [pltpu.make_async_copy(stage.at[t], dp_ref.at[:, pl.ds(pl.multiple_of((base + 4 * g + head) * LANE, LANE), LANE)],
                                          sem.at[t]) for t, base in enumerate((QA0, KA0, VA0))]

        delta[...] = jnp.broadcast_to(jnp.sum(do_ref[...] * o_ref[...], axis=-1, keepdims=True), delta.shape)
        for g, (window, d) in enumerate(ATT_GROUPS):
            dka[...] = jnp.zeros_like(dka)
            dva[...] = jnp.zeros_like(dva)

            def blk(idx, carry, g=g, window=window, d=d):
                qrows, krows, dist = _att_rows(S, d, idx)
                q = qs[g][qrows, :].astype(BF)
                k = ks[g][krows, :].astype(BF)
                v = vs[g][krows, :].astype(BF)
                do = do_ref[qrows, :]
                s = lax.dot_general(q, k, (((1,), (1,)), ((), ())), preferred_element_type=F32) * scale
                p = jnp.exp(s - lse_ref[qrows, :][:, 0:1])
                p = jnp.where((dist >= 0) & (dist <= window // d), p, 0.0)
                dob = do.astype(BF)
                dva[krows, :] += lax.dot_general(p.astype(BF), dob, (((0,), (0,)), ((), ())), preferred_element_type=F32)
                dp = lax.dot_general(dob, v, (((1,), (1,)), ((), ())), preferred_element_type=F32)
                ds = (p * (dp - delta[qrows, :][:, 0:1]) * scale).astype(BF)
                dqa[qrows, :] = jnp.dot(ds, k, preferred_element_type=F32)
                dka[krows, :] += lax.dot_general(ds, q, (((0,), (0,)), ((), ())), preferred_element_type=F32)
                return carry
            lax.fori_loop(0, S // LANE, blk, 0, unroll=8)
            if g > 0:
                for cp in slab_copies(g - 1):
                    cp.wait()
            stage[0] = dqa[...].astype(BF)
            stage[1] = dka[...].astype(BF)
            stage[2] = dva[...].astype(BF)
            for cp in slab_copies(g):
                cp.start()
        for cp in slab_copies(len(ATT_GROUPS) - 1):
            cp.wait()

    cols = [QA0 + 4 * g for g in range(3)] + [KA0 + 4 * g for g in range(3)] + [VA0 + 4 * g for g in range(3)]
    ins = [_slab(c) for c in cols]
    one = pl.BlockSpec((SEQ, LANE), lambda h: (0, h))
    return _pcall(
        body, grid=(ATT_HEADS,), in_specs=ins + [one, one, one, _ANY], out_specs=_ANY,
        out_shape=SDS((S, IN_WIDTH), BF), aliases={12: 0},
        scratch_shapes=[pltpu.VMEM((S, LANE), F32)] * 4 + [pltpu.VMEM((3, S, LANE), BF), pltpu.SemaphoreType.DMA((3,)),
                                                           pltpu.VMEM((9, S, LANE), F32)],
        compiler_params=_cp(VMEM_BIG), name="attn_bwd")(*([proj] * 9), o_att, lse, do_att, dproj)


def _bdot(a, b, dims):
    return lax.dot_general(a.astype(BF), b.astype(BF), (dims, ((), ())), preferred_element_type=F32)


def _hg_chunk(qh, fh, ih, gh, lbraw, nw, st):
    C = HG_CHUNK
    mx = jnp.maximum(lbraw[0:1], lbraw[1:2])
    e0 = jnp.exp(lbraw[0:1] - mx)
    e1 = jnp.exp(lbraw[1:2] - mx)
    lb = e0 / (e0 + e1)
    f = lb + (1.0 - lb) * jax.nn.sigmoid(fh)
    logf = jnp.log(f)
    k = 1.0 - f
    q = qh * jax.nn.sigmoid(qh)
    causal = lax.broadcasted_iota(jnp.int32, (C, C), 0) >= lax.broadcasted_iota(jnp.int32, (C, C), 1)
    b = lax.dot_general(causal.astype(F32), logf, (((1,), (0,)), ((), ())), precision=lax.Precision.HIGHEST,
                        preferred_element_type=F32)
    b_last = b[C - 1:C]
    st_out = st * jnp.exp(b_last) + _bdot(ih, k * jnp.exp(b_last - b), ((0,), (0,)))
    inter = _bdot(q * jnp.exp(b), st, ((1,), (1,)))
    b_ref = b[C // 2:C // 2 + 1]
    a = _bdot(q * jnp.exp(b - b_ref), k * jnp.exp(b_ref - b), ((1,), (1,)))
    a = jnp.where(causal, a, 0.0)
    o = inter + _bdot(a, ih, ((1,), (0,)))
    o = o * lax.rsqrt(jnp.mean(o * o, axis=-1, keepdims=True) + RMS_EPS) * nw
    return o * (gh * jax.nn.sigmoid(gh)), st_out


def _to_heads(x):
    return jnp.concatenate([x[None, :, h * LANE:(h + 1) * LANE] for h in range(HG_HEADS)], axis=0)


def _from_heads(x):
    return jnp.concatenate([x[h] for h in range(HG_HEADS)], axis=1)


def _hdot(a, b, ca, cb):
    return lax.dot_general(a.astype(BF), b.astype(BF), (((ca,), (cb,)), ((0,), (0,))), preferred_element_type=F32)


def _hg_chunk_all(qh, fh, ih, gh, lbraw, nw, st):
    C = HG_CHUNK
    mx = jnp.maximum(lbraw[0:1], lbraw[1:2])
    e0 = jnp.exp(lbraw[0:1] - mx)
    e1 = jnp.exp(lbraw[1:2] - mx)
    lb = e0 / (e0 + e1)
    f = lb + (1.0 - lb) * jax.nn.sigmoid(fh)
    logf = jnp.log(f)
    k = 1.0 - f
    q = qh * jax.nn.sigmoid(qh)
    causal = lax.broadcasted_iota(jnp.int32, (C, C), 0) >= lax.broadcasted_iota(jnp.int32, (C, C), 1)
    b = lax.dot_general(causal.astype(F32), logf, (((1,), (0,)), ((), ())), precision=lax.Precision.HIGHEST,
                        preferred_element_type=F32)
    b_last = b[C - 1:C]
    b_ref = b[C // 2:C // 2 + 1]
    v3 = _to_heads(ih)
    st_out = st * _to_heads(jnp.exp(b_last)) + _hdot(v3, _to_heads(k * jnp.exp(b_last - b)), 1, 1)
    inter = _hdot(_to_heads(q * jnp.exp(b)), st, 2, 2)
    a = _hdot(_to_heads(q * jnp.exp(b - b_ref)), _to_heads(k * jnp.exp(b_ref - b)), 2, 2)
    a = jnp.where(causal[None], a, 0.0)
    o = inter + _hdot(a, v3, 2, 1)
    o = o * lax.rsqrt(jnp.mean(o * o, axis=-1, keepdims=True) + RMS_EPS) * nw
    return _from_heads(o) * (gh * jax.nn.sigmoid(gh)), st_out


def _head_cols(h):
    return slice(h * LANE, (h + 1) * LANE)


def _hg_in_specs(chunk_of):
    half = 4 * LANE
    specs = []
    for base in (QH0, FH0, IH0, GH0):
        for part in range(2):
            specs.append(pl.BlockSpec((HG_CHUNK, half), lambda n, col=base // 4 + part: (chunk_of(n), col)))
    return specs


def _hg_sections(sec):
    return [jnp.concatenate([sec[2 * s][...], sec[2 * s + 1][...]], axis=1).astype(F32) for s in range(4)]


def _hg_fwd(proj, lb, nw):
    S = proj.shape[0]
    C = HG_CHUNK
    NC = S // C

    def body(*refs):
        sec = refs[:8]
        lb_ref, nw_ref, o_ref, st_ref, st_scr = refs[8:]

        @pl.when(pl.program_id(0) == 0)
        def _():
            st_scr[...] = jnp.zeros_like(st_scr)

        st = st_scr[...]
        st_ref[:, 0] = st
        o, st_out = _hg_chunk_all(*_hg_sections(sec), lb_ref[...], nw_ref[...], st)
        o_ref[...] = o.astype(o_ref.dtype)
        st_scr[...] = st_out

    return _pcall(
        body, grid=(NC,),
        in_specs=_hg_in_specs(lambda n: n) + [pl.BlockSpec((2, HG_WIDTH), lambda n: (0, 0)), pl.BlockSpec((1, LANE), lambda n: (0, 0))],
        out_specs=[pl.BlockSpec((C, HG_WIDTH), lambda n: (n, 0)), pl.BlockSpec((HG_HEADS, 1, LANE, LANE), lambda n: (0, n, 0, 0))],
        out_shape=[SDS((S, HG_WIDTH), BF), SDS((HG_HEADS, NC, LANE, LANE), F32)],
        scratch_shapes=[pltpu.VMEM((HG_HEADS, LANE, LANE), F32)], name="hg_fwd")(*([proj] * 8), lb, nw)


def _hg_bwd(proj, lb, nw, states, do_hg, dproj):
    S = proj.shape[0]
    C = HG_CHUNK
    NC = S // C

    def body(*refs):
        sec = refs[:8]
        lb_ref, nw_ref, st_ref, do_ref, dp_in, dp_ref, dlb_ref, dnw_ref, dst_scr, stage, sem = refs[8:]
        n = pl.program_id(0)
        slot = n % 2

        def chunk_copies(s):
            rows = pl.ds(pl.multiple_of((NC - 1 - n) * C, C), C)
            return [pltpu.make_async_copy(stage.at[s, t], dp_ref.at[rows, pl.ds((QH0 + 8 * t) * LANE, HG_WIDTH)], sem.at[s, t])
                    for t in range(4)]

        @pl.when(n == 0)
        def _():
            dst_scr[...] = jnp.zeros_like(dst_scr)
            dlb_ref[...] = jnp.zeros_like(dlb_ref)
            dnw_ref[...] = jnp.zeros_like(dnw_ref)

        _, vjp = jax.vjp(_hg_chunk_all, *_hg_sections(sec), lb_ref[...], nw_ref[...], st_ref[:, 0])
        dq, df, di, dg, dlb, dnw, dst = vjp((do_ref[...].astype(F32), dst_scr[...]))
        dlb_ref[...] += dlb
        dnw_ref[...] += dnw
        dst_scr[...] = dst

        @pl.when(n >= 2)
        def _():
            for cp in chunk_copies(slot):
                cp.wait()

        for t, val in enumerate((dq, df, di, dg)):
            stage[slot, t] = val.astype(BF)
        for cp in chunk_copies(slot):
            cp.start()

        @pl.when(n == NC - 1)
        def _():
            for cp in chunk_copies(1 - slot) + chunk_copies(slot):
                cp.wait()

    rev = lambda n: NC - 1 - n
    ospec = pl.BlockSpec((C, HG_WIDTH), lambda n: (rev(n), 0))
    lbs = pl.BlockSpec((2, HG_WIDTH), lambda n: (0, 0))
    nws = pl.BlockSpec((1, LANE), lambda n: (0, 0))
    outs = _pcall(
        body, grid=(NC,),
        in_specs=_hg_in_specs(rev) + [lbs, nws, pl.BlockSpec((HG_HEADS, 1, LANE, LANE), lambda n: (0, rev(n), 0, 0)), ospec, _ANY],
        out_specs=[_ANY, lbs, nws],
        out_shape=[SDS((S, IN_WIDTH), BF), SDS((2, HG_WIDTH), F32), SDS((1, LANE), F32)], aliases={12: 0},
        scratch_shapes=[pltpu.VMEM((HG_HEADS, LANE, LANE), F32), pltpu.VMEM((2, 4, C, HG_WIDTH), BF), pltpu.SemaphoreType.DMA((2, 4))],
        name="hg_bwd")(*([proj] * 8), lb, nw, states, do_hg, dproj)
    return outs


def _merge(ga, gb, ba, bb):
    return jax.nn.sigmoid(ga) * ba + jax.nn.sigmoid(gb) * bb


_MT = 512


def _gate_specs():
    ga = pl.BlockSpec((_MT, _MT), lambda i, j: (i, GA0 * LANE // _MT + j))
    gb = pl.BlockSpec((_MT, _MT), lambda i, j: (i, GB0 * LANE // _MT + j))
    t = pl.BlockSpec((_MT, _MT), lambda i, j: (i, j))
    return ga, gb, t


def _merge_fwd(proj, ba, bb):
    S, D = ba.shape
    ga, gb, t = _gate_specs()

    def body(ga_ref, gb_ref, ba_ref, bb_ref, o_ref):
        o_ref[...] = _merge(ga_ref[...].astype(F32), gb_ref[...].astype(F32), ba_ref[...].astype(F32),
                            bb_ref[...].astype(F32)).astype(o_ref.dtype)

    return _pcall(body, grid=(S // _MT, D // _MT), in_specs=[ga, gb, t, t], out_specs=t,
                          out_shape=SDS((S, D), BF), name="merge_fwd")(proj, proj, ba, bb)


def _merge_bwd(proj, ba, bb, dm):
    S, D = ba.shape
    ga, gb, t = _gate_specs()

    def body(ga_ref, gb_ref, ba_ref, bb_ref, dm_ref, dba_ref, dbb_ref, dp_ref, stage, sem):
        _, vjp = jax.vjp(_merge, ga_ref[...].astype(F32), gb_ref[...].astype(F32), ba_ref[...].astype(F32),
                         bb_ref[...].astype(F32))
        dga, dgb, dba, dbb = vjp(dm_ref[...].astype(F32))
        dba_ref[...] = dba.astype(BF)
        dbb_ref[...] = dbb.astype(BF)
        stage[0] = dga.astype(BF)
        stage[1] = dgb.astype(BF)
        rows = pl.ds(pl.multiple_of(pl.program_id(0) * _MT, _MT), _MT)
        cps = []
        for k, base in enumerate((GA0, GB0)):
            cols = pl.ds(pl.multiple_of(base * LANE + pl.program_id(1) * _MT, LANE), _MT)
            cps.append(pltpu.make_async_copy(stage.at[k], dp_ref.at[rows, cols], sem.at[k]))
            cps[-1].start()
        for cp in cps:
            cp.wait()

    return _pcall(body, grid=(S // _MT, D // _MT), in_specs=[ga, gb, t, t, t], out_specs=[t, t, _ANY],
                  out_shape=[SDS((S, D), BF), SDS((S, D), BF), SDS((S, IN_WIDTH), BF)],
                  scratch_shapes=[pltpu.VMEM((2, _MT, _MT), BF), pltpu.SemaphoreType.DMA((2,))],
                  name="merge_bwd")(proj, proj, ba, bb, dm)


def _swiglu(a, b):
    return a * jax.nn.sigmoid(a) * b


def _swiglu_fwd(a, b):
    S, F = a.shape
    tf = _pick(F, (1408, 512))
    t = pl.BlockSpec((_MT, tf), lambda i, j: (i, j))

    def body(a_ref, b_ref, o_ref):
        o_ref[...] = _swiglu(a_ref[...].astype(F32), b_ref[...].astype(F32)).astype(o_ref.dtype)

    return _pcall(body, grid=(S // _MT, F // tf), in_specs=[t, t], out_specs=t,
                          out_shape=SDS((S, F), BF), name="swiglu_fwd")(a, b)


def _swiglu_bwd(a, b, dg):
    S, F = a.shape
    tf = _pick(F, (1408, 512))
    t = pl.BlockSpec((_MT, tf), lambda i, j: (i, j))

    def body(a_ref, b_ref, dg_ref, da_ref, db_ref):
        _, vjp = jax.vjp(_swiglu, a_ref[...].astype(F32), b_ref[...].astype(F32))
        da, db = vjp(dg_ref[...].astype(F32))
        da_ref[...] = da.astype(BF)
        db_ref[...] = db.astype(BF)

    return _pcall(body, grid=(S // _MT, F // tf), in_specs=[t, t, t], out_specs=[t, t],
                          out_shape=[SDS((S, F), BF)] * 2, name="swiglu_bwd")(a, b, dg)


def _cross(q, k, v):
    s = _bdot(q, k, ((1,), (1,))) * (HEAD_DIM ** -0.5)
    e = jnp.exp(s - jnp.max(s, axis=-1, keepdims=True))
    p = e / jnp.sum(e, axis=-1, keepdims=True)
    return _bdot(p, v, ((1,), (0,)))


def _cross_specs():
    q = pl.BlockSpec((_MT, LANE), lambda h, i: (i, h))
    k = pl.BlockSpec((MEM_LEN, LANE), lambda h, i: (0, h))
    v = pl.BlockSpec((MEM_LEN, LANE), lambda h, i: (0, CROSS_HEADS + h))
    return q, k, v


def _cross_fwd(qc, kvc):
    S = qc.shape[0]
    q, k, v = _cross_specs()

    def body(q_ref, k_ref, v_ref, o_ref):
        o_ref[...] = _cross(q_ref[...], k_ref[...], v_ref[...]).astype(o_ref.dtype)

    return _pcall(body, grid=(CROSS_HEADS, S // _MT), in_specs=[q, k, v], out_specs=q,
                          out_shape=SDS((S, CROSS_WIDTH), BF), name="cross_fwd")(qc, kvc, kvc)


def _cross_bwd(qc, kvc, doc):
    S = qc.shape[0]
    q, k, v = _cross_specs()

    def body(q_ref, k_ref, v_ref, do_ref, dq_ref, dk_ref, dv_ref):
        @pl.when(pl.program_id(1) == 0)
        def _():
            dk_ref[...] = jnp.zeros_like(dk_ref)
            dv_ref[...] = jnp.zeros_like(dv_ref)

        _, vjp = jax.vjp(_cross, q_ref[...], k_ref[...], v_ref[...])
        dq, dk, dv = vjp(do_ref[...])
        dq_ref[...] = dq.astype(dq_ref.dtype)
        dk_ref[...] += dk
        dv_ref[...] += dv

    return _pcall(body, grid=(CROSS_HEADS, S // _MT), in_specs=[q, k, v, q], out_specs=[q, k, k],
                          out_shape=[SDS((S, CROSS_WIDTH), BF), SDS((MEM_LEN, CROSS_WIDTH), F32), SDS((MEM_LEN, CROSS_WIDTH), F32)],
                          name="cross_bwd")(qc, kvc, kvc, doc)


def _local_step(x, mem, tgt, p, ready):
    h = _rms_fwd(x, p["ln_mix"], "rms_mix")
    proj = _mm(h, p["w_in_a"], "nt", BF, "mm_proj_a", a_part=(0, 2))
    proj = _mm(h, p["w_in_b"], "nt", BF, "mm_proj_b", a_part=(1, 2), res=proj)
    o_att, lse = _attn_fwd(proj)
    o_hg, states = _hg_fwd(proj, p["hg_lb"], p["hg_norm"])
    ba = _mm(o_att, p["w_ba"], "nn", BF, "mm_ba")
    bb = _mm(o_hg, p["w_bb"], "nn", BF, "mm_bb")
    merged = _merge_fwd(proj, ba, bb)
    x1 = _mm(merged, p["w_out"], "nn", F32, "mm_out", res=x)
    hc = _rms_fwd(x1, p["ln_cross"], "rms_cross")
    qc = _mm(hc, p["wq"], "nn", F32, "mm_q")
    mn = _rms_fwd(mem, p["ln_mem"], "rms_mem")
    kvc = _mm(mn, p["wkv"], "nn", F32, "mm_kv")
    oc = _cross_fwd(qc, kvc)
    x2 = _mm(oc, p["wo"], "nn", F32, "mm_o", res=x1)
    hf = _rms_fwd(x2, p["ln_ffn"], "rms_ffn")
    a = _mm(hf, p["w1"], "nt", BF, "mm_w1")
    b = _mm(hf, p["w3"], "nt", BF, "mm_w3")
    g = _swiglu_fwd(a, b)
    x3 = _mm(g, p["w2"], "nn", F32, "mm_w2", res=x2)
    loss, dx3, dx3b, d_ln_final = _final_loss(x3, p["ln_final"], tgt)

    gw, gs = {}, {"ln_final": d_ln_final}
    dg = _mm(dx3b, p["w2"], "nt", BF, "mm_dg")
    gw["w2"] = _mm(g, dx3b, "tn", BF, "mm_dw2")
    da, db = _swiglu_bwd(a, b, dg)
    dhf = _mm(da, p["w1"], "nn", F32, "mm_dhf1")
    dhf = _mm(db, p["w3"], "nn", BF, "mm_dhf3", res=dhf)
    gw["w1"] = _mm(da, hf, "tn", BF, "mm_dw1")
    gw["w3"] = _mm(db, hf, "tn", BF, "mm_dw3")
    ready("ffn", gw)
    dx2, dx2b, gs["ln_ffn"] = _rms_bwd(x2, p["ln_ffn"], dhf, dx3, "rms_ffn_bwd")
    doc = _mm(dx2b, p["wo"], "nt", F32, "mm_doc")
    gw["wo"] = _mm(oc, dx2b, "tn", BF, "mm_dwo")
    dqc, dk, dv = _cross_bwd(qc, kvc, doc)
    dkvc = jnp.concatenate([dk, dv], axis=1)
    dhc = _mm(dqc, p["wq"], "nt", BF, "mm_dhc")
    gw["wq"] = _mm(hc, dqc, "tn", BF, "mm_dwq")
    dmn = _mm(dkvc, p["wkv"], "nt", F32, "mm_dmn")
    gw["wkv"] = _mm(mn, dkvc, "tn", BF, "mm_dwkv")
    ready("cross", gw)
    gs["ln_mem"] = _rms_bwd(mem, p["ln_mem"], dmn, None, "rms_mem_bwd")
    dx1, dx1b, gs["ln_cross"] = _rms_bwd(x1, p["ln_cross"], dhc, dx2, "rms_cross_bwd")
    dmerged = _mm(dx1b, p["w_out"], "nt", BF, "mm_dmerged")
    gw["w_out"] = _mm(merged, dx1b, "tn", BF, "mm_dwout")
    dba, dbb, dproj = _merge_bwd(proj, ba, bb, dmerged)
    do_att = _mm(dba, p["w_ba"], "nt", F32, "mm_doatt")
    gw["w_ba"] = _mm(o_att, dba, "tn", BF, "mm_dwba")
    do_hg = _mm(dbb, p["w_bb"], "nt", F32, "mm_dohg")
    gw["w_bb"] = _mm(o_hg, dbb, "tn", BF, "mm_dwbb")
    ready("mix", gw)
    dproj = _attn_bwd(proj, o_att, lse, do_att, dproj)
    dproj, gs["hg_lb"], gs["hg_norm"] = _hg_bwd(proj, p["hg_lb"], p["hg_norm"], states, do_hg, dproj)
    ready("mixers_done", gw)
    for q in range(IN_PARTS):
        gw[f"w_in_q{q}"] = _mm(dproj, h, "tn", BF, f"mm_dwin_q{q}", b_part=(q, IN_PARTS))
        ready(f"in_q{q}", gw)
    dh_a = _mm(dproj, p["w_in_a"], "nn", BF, "mm_dh_a")
    ready("dh_half", gw)
    dh = (dh_a, _mm(dproj, p["w_in_b"], "nn", BF, "mm_dh_b"))
    grad_x, _, gs["ln_mix"] = _rms_bwd(x, p["ln_mix"], dh, dx1, "rms_mix_bwd")
    return loss, grad_x, gs


_MATS = (
    ("w_in_a", "w_in", IN_WIDTH, D_MODEL // 2, "row", IN_WIDTH // N_DEV, True),
    ("w_in_b", "w_in", IN_WIDTH, D_MODEL // 2, "row", IN_WIDTH // N_DEV, True),
    ("w_ba", "w_branch_a", ATT_OUT, D_MODEL, "col", D_MODEL // N_DEV, False),
    ("w_bb", "w_branch_b", HG_WIDTH, D_MODEL, "col", D_MODEL // N_DEV, False),
    ("w_out", "w_out", D_MODEL, D_MODEL, "row", D_MODEL // N_DEV, False),
    ("wq", "wq_cross", D_MODEL, CROSS_WIDTH, "row", D_MODEL // N_DEV, False),
    ("wkv", "wkv_cross", D_MODEL, 2 * CROSS_WIDTH, "row", D_MODEL // N_DEV, False),
    ("wo", "wo_cross", CROSS_WIDTH, D_MODEL, "col", D_MODEL // N_DEV, False),
    ("w1", "w1", D_FF, D_MODEL, "row", D_FF // N_DEV, True),
    ("w3", "w3", D_FF, D_MODEL, "row", D_FF // N_DEV, True),
    ("w2", "w2", D_FF, D_MODEL, "row", D_FF // N_DEV, False),
    ("w_in_q0", "w_in", IN_WIDTH, D_MODEL // 2, "row", IN_WIDTH // N_DEV, True),
    ("w_in_q1", "w_in", IN_WIDTH, D_MODEL // 2, "row", IN_WIDTH // N_DEV, True),
)
IN_PARTS = 2
_INDEX = {m[0]: i for i, m in enumerate(_MATS)}


def _unit_shape(i):
    _, _, K, N, kind, sh, _ = _MATS[i]
    return (sh, N) if kind == "row" else (K, sh)


def _win(ref, i, chip, par, half=None):
    _, _, K, _, kind, sh, _ = _MATS[i]
    if kind == "row":
        rows, start = (sh, 0) if half is None else (sh // 2, half * (sh // 2))
        return ref.at[pl.ds(pl.multiple_of((2 * chip + par) * sh + start, 16), rows), :]
    rows, start = (K, 0) if half is None else (K // 2, half * (K // 2))
    return ref.at[pl.ds(start, rows), pl.ds(pl.multiple_of((2 * chip + par) * sh, LANE), sh)]


def _my_place():
    x, y, c = lax.axis_index("x"), lax.axis_index("y"), lax.axis_index("c")
    chips = [(1 - x, y), (x, 1 - y), (1 - x, 1 - y)]
    return x, y, c, chips


def _cast_unit(w, name, half=None):
    K, N = w.shape
    if half is not None:
        N = N // 2
    tr = _pick(K, (256, 400, 352))

    def body(w_ref, o_ref):
        o_ref[...] = w_ref[...].astype(BF)

    src = pl.BlockSpec((tr, N), lambda i: (i, half or 0))
    dst = pl.BlockSpec((tr, N), lambda i: (i, 0))
    return _pcall(body, grid=(K // tr,), in_specs=[src], out_specs=dst, out_shape=SDS((K, N), BF), name=name)(w)


def _handshake(peers):
    barrier = pltpu.get_barrier_semaphore()
    for peer in peers:
        pl.semaphore_signal(barrier, inc=1, device_id=peer, device_id_type=MESH)
    pl.semaphore_wait(barrier, len(peers))


def _sequencer(body, out_type, sems, cid, name):
    return pl.kernel(body, out_type=out_type, mesh=plsc.ScalarSubcoreMesh(axis_name="seq", num_cores=1),
                     scratch_types=sems, compiler_params=pltpu.CompilerParams(collective_id=cid), name=name)


def _all_gather(units, idx, cid, name):
    n = len(idx)

    def body(*refs):
        us, fulls = refs[:n], refs[n:2 * n]
        ssem, rsem, lsem = refs[2 * n:]
        x, y, c, _ = _my_place()
        xn, yn, dg = (1 - x, y), (x, 1 - y), (1 - x, 1 - y)
        me, sib = (x, y, c), (x, y, 1 - c)
        _handshake([sib, (*xn, c), (*yn, c)])

        def win(i, chip_xy, par, half=None):
            return _win(fulls[i], idx[i], 2 * chip_xy[0] + chip_xy[1], par, half)

        def rcopy(i, k, src, dst, to):
            return pltpu.make_async_remote_copy(src_ref=src, dst_ref=dst, send_sem=ssem.at[i, k], recv_sem=rsem.at[i, k],
                                                device_id=to, device_id_type=MESH)

        def landed(i, k, blk):
            rcopy(i, k, blk, blk, me).wait_recv()

        sends, locs = [], []

        def send(i, k, blk, to, src=None):
            cp = rcopy(i, k, blk if src is None else src, blk, to)
            cp.start()
            sends.append(cp)

        for i in range(n):
            mine = win(i, (x, y), c)
            loc = pltpu.make_async_copy(us[i], mine, lsem.at[i])
            loc.start()
            locs.append(loc)
            send(i, 0, mine, sib, src=us[i])
            send(i, 1, mine, (*xn, c), src=us[i])
            send(i, 2, mine, (*yn, c), src=us[i])
        for i in range(n):
            landed(i, 1, win(i, xn, c))
            send(i, 3, win(i, xn, c, 0), (*yn, c))
            send(i, 5, win(i, xn, c), sib)
            landed(i, 2, win(i, yn, c))
            send(i, 4, win(i, yn, c, 1), (*xn, c))
            send(i, 6, win(i, yn, c), sib)
        for i in range(n):
            landed(i, 3, win(i, dg, c, 0))
            landed(i, 4, win(i, dg, c, 1))
            send(i, 7, win(i, dg, c), sib)
        for i in range(n):
            landed(i, 0, win(i, (x, y), 1 - c))
            landed(i, 5, win(i, xn, 1 - c))
            landed(i, 6, win(i, yn, 1 - c))
            landed(i, 7, win(i, dg, 1 - c))
        for cp in sends:
            cp.wait_send()
        for loc in locs:
            loc.wait()

    out_type = [SDS((_MATS[i][2], _MATS[i][3]), BF) for i in idx]
    sems = [pltpu.SemaphoreType.DMA((n, 8)), pltpu.SemaphoreType.DMA((n, 8)), pltpu.SemaphoreType.DMA((n,))]
    return _sequencer(body, out_type, sems, cid, name)(*units)


def _rs_sibling(grads, idx, cid, name):
    n = len(idx)

    def body(*refs):
        gs, bufs = refs[:n], refs[n:2 * n]
        ssem, rsem = refs[2 * n:]
        x, y, c, _ = _my_place()
        _handshake([(x, y, 1 - c)])
        cps = []
        for i in range(n):
            for k in range(4):
                cp = pltpu.make_async_remote_copy(
                    src_ref=_win(gs[i], idx[i], k, 1 - c), dst_ref=bufs[i].at[k], send_sem=ssem.at[i, k], recv_sem=rsem.at[i, k],
                    device_id=(x, y, 1 - c), device_id_type=MESH)
                cp.start()
                cps.append(cp)
        for cp in cps:
            cp.wait()

    out_type = [SDS((4, *_unit_shape(i)), BF) for i in idx]
    sems = [pltpu.SemaphoreType.DMA((n, 4)), pltpu.SemaphoreType.DMA((n, 4))]
    return _sequencer(body, out_type, sems, cid, name)(*grads)


def _sibling_sum_tc(grad, i, cid, name):
    unit = _unit_shape(i)

    def body(g_ref, t_ref, theirs, mine, summed, ssem, rsem, lsem, osem):
        x, y, c, _ = _my_place()
        _handshake([(x, y, 1 - c)])
        sends, loads = [], []
        for k in range(4):
            sends.append(pltpu.make_async_remote_copy(
                src_ref=_win(g_ref, i, k, 1 - c), dst_ref=theirs.at[k], send_sem=ssem.at[k], recv_sem=rsem.at[k],
                device_id=(x, y, 1 - c), device_id_type=MESH))
            loads.append(pltpu.make_async_copy(_win(g_ref, i, k, c), mine.at[k], lsem.at[k]))
            sends[-1].start()
            loads[-1].start()
        stores = []
        for k in range(4):
            loads[k].wait()
            sends[k].wait_recv()
            summed[k] = (mine[k].astype(F32) + theirs[k].astype(F32)).astype(BF)
            stores.append(pltpu.make_async_copy(summed.at[k], t_ref.at[k], osem.at[k]))
            stores[-1].start()
        for k in range(4):
            sends[k].wait_send()
            stores[k].wait()

    block = pltpu.VMEM((4, *unit), BF)
    return _pcall(body, in_specs=[_ANY], out_specs=_ANY, out_shape=SDS((4, *unit), BF),
                  scratch_shapes=[block, block, block] + [pltpu.SemaphoreType.DMA((4,))] * 4,
                  compiler_params=_cp(VMEM_BIG, collective_id=cid), name=name)(grad)


def _chip_sum(g, sib, i, place, name):
    _, _, K, N, kind, sh, _ = _MATS[i]

    def body(pref, g_ref, s_ref, o_ref):
        o_ref[...] = (g_ref[...].astype(F32) + s_ref[...].astype(F32)).astype(BF)

    if kind == "row":
        tr = _pick(sh, (800, 352, 256))
        per = sh // tr
        grid = (4, per)
        g_spec = pl.BlockSpec((tr, N), lambda k, r, pref: ((2 * k + pref[0]) * per + r, 0))
        u_spec = pl.BlockSpec((None, tr, N), lambda k, r, pref: (k, r, 0))
    else:
        grid = (4,)
        g_spec = pl.BlockSpec((K, sh), lambda k, pref: (0, 2 * k + pref[0]))
        u_spec = pl.BlockSpec((None, K, sh), lambda k, pref: (k, 0, 0))
    return _pcall(body, grid=grid, in_specs=[g_spec, u_spec], out_specs=u_spec, out_shape=SDS((4, *_unit_shape(i)), BF),
                  prefetch=True, name=name)(place, g, sib)


def _rs_chips(parts, idx, cid, name):
    n = len(idx)

    def body(*refs):
        ts, bufs, relays = refs[:n], refs[n:2 * n], refs[2 * n:3 * n]
        ssem, rsem = refs[3 * n:]
        x, y, c, _ = _my_place()
        xn, yn, dg = (1 - x, y), (x, 1 - y), (1 - x, 1 - y)
        to_x, to_y, me = (*xn, c), (*yn, c), (x, y, c)
        _handshake([to_x, to_y])

        def rcopy(i, k, src, dst, to):
            return pltpu.make_async_remote_copy(src_ref=src, dst_ref=dst, send_sem=ssem.at[i, k], recv_sem=rsem.at[i, k],
                                                device_id=to, device_id_type=MESH)

        def halves(i):
            rows = _unit_shape(idx[i])[0] // 2
            return [pl.ds(0, rows), pl.ds(rows, rows)]

        sends = []

        def send(cp):
            cp.start()
            sends.append(cp)

        for i in range(n):
            h0, h1 = halves(i)
            diag = ts[i].at[2 * dg[0] + dg[1]]
            send(rcopy(i, 0, ts[i].at[2 * xn[0] + xn[1]], bufs[i].at[0], to_x))
            send(rcopy(i, 1, ts[i].at[2 * yn[0] + yn[1]], bufs[i].at[1], to_y))
            send(rcopy(i, 2, diag.at[h0], relays[i].at[0], to_x))
            send(rcopy(i, 3, diag.at[h1], relays[i].at[1], to_y))
        for i in range(n):
            h0, h1 = halves(i)
            rcopy(i, 2, relays[i].at[0], relays[i].at[0], me).wait_recv()
            send(rcopy(i, 4, relays[i].at[0], bufs[i].at[2, h0], to_y))
            rcopy(i, 3, relays[i].at[1], relays[i].at[1], me).wait_recv()
            send(rcopy(i, 5, relays[i].at[1], bufs[i].at[2, h1], to_x))
        for i in range(n):
            h0, h1 = halves(i)
            rcopy(i, 0, bufs[i].at[0], bufs[i].at[0], me).wait_recv()
            rcopy(i, 1, bufs[i].at[1], bufs[i].at[1], me).wait_recv()
            rcopy(i, 4, bufs[i].at[2, h0], bufs[i].at[2, h0], me).wait_recv()
            rcopy(i, 5, bufs[i].at[2, h1], bufs[i].at[2, h1], me).wait_recv()
        for cp in sends:
            cp.wait_send()

    def relay_shape(i):
        rows, cols = _unit_shape(i)
        return (2, rows // 2, cols)

    out_type = [SDS((3, *_unit_shape(i)), BF) for i in idx] + [SDS(relay_shape(i), BF) for i in idx]
    sems = [pltpu.SemaphoreType.DMA((n, 6)), pltpu.SemaphoreType.DMA((n, 6))]
    return _sequencer(body, out_type, sems, cid, name)(*parts)[:n]


def _adamw(w, g, m, v):
    m = ADAM_B1 * m + (1.0 - ADAM_B1) * g
    v = ADAM_B2 * v + (1.0 - ADAM_B2) * jnp.square(g)
    m_hat = m / (1.0 - ADAM_B1 ** ADAM_STEP)
    v_hat = v / (1.0 - ADAM_B2 ** ADAM_STEP)
    delta = -ADAM_LR * (m_hat / (jnp.sqrt(v_hat) + ADAM_EPS) + ADAM_WD * w)
    return delta, m, v


def _adam_mat(parts, recvs, w, m, v, place, name):
    R, C = w.shape
    n = len(parts)
    tr = _pick(R, (256, 160, 176))

    def body(pref, *refs):
        t_refs, r_refs = refs[:n], refs[n:2 * n]
        w_ref, m_ref, v_ref, g_out, d_out, m_out, v_out = refs[2 * n:]
        cols = [t[...].astype(F32) + r[0].astype(F32) + r[1].astype(F32) + r[2].astype(F32) for t, r in zip(t_refs, r_refs)]
        g = jnp.concatenate(cols, axis=1) if n > 1 else cols[0]
        d, mn, vn = _adamw(w_ref[...], g, m_ref[...], v_ref[...])
        g_out[...] = g
        d_out[...] = d
        m_out[...] = mn
        v_out[...] = vn

    t_spec = pl.BlockSpec((None, tr, C // n), lambda i, pref: (pref[1], i, 0))
    r_spec = pl.BlockSpec((3, tr, C // n), lambda i, pref: (0, i, 0))
    s_spec = pl.BlockSpec((tr, C), lambda i, pref: (i, 0))
    return _pcall(body, grid=(R // tr,), in_specs=[t_spec] * n + [r_spec] * n + [s_spec] * 3, out_specs=[s_spec] * 4,
                  out_shape=[SDS((R, C), F32)] * 4, prefetch=True, name=name)(place, *parts, *recvs, w, m, v)


_SMALL = (("ln_mix", "ln_mix_w", D_MODEL), ("ln_cross", "ln_cross_w", D_MODEL), ("ln_mem", "ln_mem_w", D_MODEL),
          ("ln_ffn", "ln_ffn_w", D_MODEL), ("ln_final", "ln_final_w", D_MODEL), ("hg_lb", "hg_lower_bounds", 2 * HG_WIDTH),
          ("hg_norm", "hg_norm_w", HEAD_DIM))
_PACK = sum(n for _, _, n in _SMALL) + LANE


def _small_sync(grow, wrow, mrow, vrow):
    def body(g_ref, w_ref, m_ref, v_ref, o_ref, gath, ssem, rsem):
        x, y, c, _ = _my_place()
        me = 4 * x + 2 * y + c
        gath[pl.ds(me, 1), :] = g_ref[...]
        cps = []
        for dlt in range(1, N_DEV):
            peer = (me + dlt) % N_DEV
            cp = pltpu.make_async_remote_copy(
                src_ref=g_ref, dst_ref=gath.at[pl.ds(me, 1), :], send_sem=ssem.at[peer], recv_sem=rsem.at[me],
                device_id=(peer // 4, (peer // 2) % 2, peer % 2), device_id_type=MESH)
            cp.start()
            cps.append(cp)
        for dlt in range(1, N_DEV):
            src = (me + dlt) % N_DEV
            pltpu.make_async_remote_copy(
                src_ref=g_ref, dst_ref=gath.at[pl.ds(src, 1), :], send_sem=ssem.at[src], recv_sem=rsem.at[src],
                device_id=(x, y, c), device_id_type=MESH).wait_recv()
        for cp in cps:
            cp.wait_send()
        g = gath[0:1, :]
        for j in range(1, N_DEV):
            g = g + gath[j:j + 1, :]
        d, mn, vn = _adamw(w_ref[...], g, m_ref[...], v_ref[...])
        o_ref[0:1, :] = g
        o_ref[1:2, :] = d
        o_ref[2:3, :] = mn
        o_ref[3:4, :] = vn

    vm = pl.BlockSpec(memory_space=pltpu.VMEM)
    return _pcall(
        body, in_specs=[vm] * 4, out_specs=vm, out_shape=SDS((4, _PACK), F32),
        scratch_shapes=[pltpu.VMEM((N_DEV, _PACK), F32), pltpu.SemaphoreType.DMA((N_DEV,)), pltpu.SemaphoreType.DMA((N_DEV,))],
        name="small_sync")(grow, wrow, mrow, vrow)


def _pack_small(d, loss_row=None):
    parts = [d[k].reshape(1, n).astype(F32) for k, _, n in _SMALL]
    parts.append(loss_row if loss_row is not None else jnp.ones((1, LANE), F32))
    return jnp.concatenate(parts, axis=1)


def kernel(x, mem, ln_mix_w, w_in, hg_norm_w, hg_lower_bounds, w_branch_a, w_branch_b, w_out, ln_cross_w, ln_mem_w, wq_cross, wkv_cross, wo_cross, ln_ffn_w, w1, w3, w2, ln_final_w, loss_target, m_ln_mix_w, m_w_in, m_hg_norm_w, m_hg_lower_bounds, m_w_branch_a, m_w_branch_b, m_w_out, m_ln_cross_w, m_ln_mem_w, m_wq_cross, m_wkv_cross, m_wo_cross, m_ln_ffn_w, m_w1, m_w3, m_w2, m_ln_final_w, v_ln_mix_w, v_w_in, v_hg_norm_w, v_hg_lower_bounds, v_w_branch_a, v_w_branch_b, v_w_out, v_ln_cross_w, v_ln_mem_w, v_wq_cross, v_wkv_cross, v_wo_cross, v_ln_ffn_w, v_w1, v_w3, v_w2, v_ln_final_w):
    given = dict(locals())
    place = jnp.stack([lax.axis_index("c"), 2 * lax.axis_index("x") + lax.axis_index("y")]).astype(jnp.int32)

    _ORDER[0] = None
    index = _INDEX

    def local_view(a, i):
        return a[0].T if _MATS[i][6] else a[0]

    shard = {key: local_view(given[ref], i) for i, (key, ref, *_) in enumerate(_MATS)}

    p = {}
    col_half = {"w_in_a": 0, "w_in_b": 1}
    waves = (("in_a", ["w_in_a"], 1), ("in_b", ["w_in_b"], 12), ("mid", ["w_ba", "w_bb", "w_out", "wq", "wkv", "wo"], 2),
             ("w1", ["w1"], 11), ("w3", ["w3"], 15), ("w2", ["w2"], 16))
    for wave, keys, cid in waves:
        units = [_cast_unit(shard[k], "cast_" + k, col_half.get(k)) for k in keys]
        p.update(zip(keys, _all_gather(units, [index[k] for k in keys], cid, "all_gather_" + wave)))
    small_w = {"ln_mix": ln_mix_w, "ln_cross": ln_cross_w, "ln_mem": ln_mem_w, "ln_ffn": ln_ffn_w,
               "ln_final": ln_final_w.reshape(1, D_MODEL), "hg_lb": hg_lower_bounds, "hg_norm": hg_norm_w}
    p.update(small_w)

    groups = {"ffn": ["w2", "w1", "w3"], "cross": ["wo", "wq", "wkv"], "mix": ["w_out", "w_ba", "w_bb"]}
    cids = {"ffn": (3, 4), "cross": (5, 6), "mix": (7, 8)}
    for q in range(IN_PARTS):
        groups[f"in_q{q}"] = [f"w_in_q{q}"]
        cids[f"in_q{q}"] = ((9, 10), (13, 14), (17, 18), (19, 20))[q]
    sib, parts, recv, out = {}, {}, {}, {}

    grads = {}

    def chip_stage(name):
        keys = groups[name]
        parts[name] = [_chip_sum(grads[k], s, index[k], place, "chip_sum_" + k) for k, s in zip(keys, sib[name])]
        recv[name] = list(_rs_chips(parts[name], [index[k] for k in keys], cids[name][1], "rs_chips_" + name))

    def adam(ref, i, part_list, recv_list):
        res = _adam_mat(part_list, recv_list, shard[_MATS[i][0]], local_view(given["m_" + ref], i), local_view(given["v_" + ref], i),
                        place, "adam_" + ref)
        out[ref] = [(a.T if _MATS[i][6] else a)[None] for a in res]

    def adam_stage(name):
        for k, t, r in zip(groups[name], parts[name], recv[name]):
            adam(_MATS[index[k]][1], index[k], [t], [r])

    def ready(name, gw):
        if name == "mixers_done":
            adam_stage("ffn")
            chip_stage("mix")
            adam_stage("cross")
            return
        if name == "dh_half":
            return
        keys = groups[name]
        grads.update({k: gw[k] for k in keys})
        if name.startswith("in_q"):
            if name == "in_q1":
                adam_stage("mix")
            parts[name] = [_sibling_sum_tc(gw[k], index[k], cids[name][0], "sibling_sum_" + k) for k in keys]
            recv[name] = list(_rs_chips(parts[name], [index[k] for k in keys], cids[name][1], "rs_chips_" + name))
            return
        sib[name] = _rs_sibling([gw[k] for k in keys], [index[k] for k in keys], cids[name][0], "rs_sibling_" + name)
        if name == "cross":
            chip_stage("ffn")
        if name == "mix":
            chip_stage("cross")

    loss_row, grad_x, gs = _local_step(x[0], mem[0], loss_target[0], p, ready)

    sm = {k: given["m_" + ref] for k, ref, _ in _SMALL}
    sv = {k: given["v_" + ref] for k, ref, _ in _SMALL}
    synced = _small_sync(_pack_small(gs, loss_row), _pack_small(small_w), _pack_small(sm), _pack_small(sv))
    in_groups = [f"in_q{q}" for q in range(IN_PARTS)]
    adam("w_in", index["w_in_a"], sum((parts[g] for g in in_groups), []), sum((recv[g] for g in in_groups), []))
    off = 0
    for k, ref, n in _SMALL:
        shape = given[ref].shape
        out[ref] = [synced[r, off:off + n].reshape(shape) for r in range(4)]
        off += n
    loss = synced[0, off]

    order = ["ln_mix_w", "w_in", "hg_norm_w", "hg_lower_bounds", "w_branch_a", "w_branch_b", "w_out", "ln_cross_w", "ln_mem_w",
             "wq_cross", "wkv_cross", "wo_cross", "ln_ffn_w", "w1", "w3", "w2", "ln_final_w"]
    return (loss, grad_x[None], *[out[n][0] for n in order], *[out[n][1] for n in order],
            *[out[n][2] for n in order], *[out[n][3] for n in order])
```

```python
import functools
import math

import jax
import jax.numpy as jnp
from jax import lax
from jax.experimental import pallas as pl
from jax.experimental.pallas import tpu as pltpu
from jax.experimental.pallas import tpu_sc as plsc

F32 = jnp.float32
BF = jnp.bfloat16
SDS = jax.ShapeDtypeStruct

D_MODEL = 2048
SEQ = 2048
HEAD_DIM = 128
MEM_LEN = 256
ATT_GROUPS = ((128, 1), (512, 4), (2048, 16))
ATT_HEADS = 4
ATT_WIDTH = 1536
ATT_OUT = 512
HG_HEADS = 8
HG_WIDTH = 1024
HG_CHUNK = 64
IN_WIDTH = 12800
CROSS_HEADS = 4
CROSS_WIDTH = 512
D_FF = 5632
RMS_EPS = 1e-6
ADAM_LR = 0.001
ADAM_B1 = 0.9
ADAM_B2 = 0.999
ADAM_EPS = 1e-08
ADAM_WD = 0.01
ADAM_STEP = 10
N_DEV = 8

LANE = 128
QA0, KA0, VA0 = 0, 12, 24
QH0, FH0, IH0, GH0 = 36, 44, 52, 60
GA0, GB0 = 68, 84

VMEM_BIG = 56 * 1024 * 1024
MESH = pl.DeviceIdType.MESH


def _cp(vmem=None, **kw):
    if vmem is not None:
        kw["vmem_limit_bytes"] = vmem
    return pltpu.CompilerParams(**kw)


def _pick(n, cands):
    for c in cands:
        if n % c == 0:
            return c
    raise ValueError(f"no tile for {n}")


_ANY = pl.BlockSpec(memory_space=pl.ANY)
_ORDER = [None]


def _pcall(body, *, name, in_specs, out_specs, out_shape, grid=(), scratch_shapes=(), compiler_params=None, prefetch=False,
           aliases=None):
    def run(*args):
        dep = _ORDER[0]
        specs, operands, kernel_fn = list(in_specs), list(args), body
        if dep is not None and not any(dep is a for a in operands):
            at = len(operands)
            specs.append(_ANY)
            operands.append(dep)

            def kernel_fn(*refs):
                return body(*refs[:at], *refs[at + 1:])
        if prefetch:
            call = pl.pallas_call(
                kernel_fn, out_shape=out_shape, name=name, compiler_params=compiler_params,
                grid_spec=pltpu.PrefetchScalarGridSpec(num_scalar_prefetch=1, grid=grid, in_specs=specs, out_specs=out_specs,
                                                       scratch_shapes=scratch_shapes))
        else:
            call = pl.pallas_call(kernel_fn, grid=grid, in_specs=specs, out_specs=out_specs, out_shape=out_shape,
                                  scratch_shapes=scratch_shapes, compiler_params=compiler_params, name=name,
                                  input_output_aliases=aliases or {})
        out = call(*operands)
        _ORDER[0] = out[0] if isinstance(out, (list, tuple)) else out
        return out
    return run


def _mm(a, b, form, out_dtype, name, res=None, tiles=None, a_part=(0, 1), b_part=(0, 1)):
    a_shape = (a.shape[0], a.shape[1] // a_part[1])
    b_shape = (b.shape[0], b.shape[1] // b_part[1])
    if form == "nn":
        (M, K), (K2, N) = a_shape, b_shape
    elif form == "nt":
        (M, K), (N, K2) = a_shape, b_shape
    else:
        (K, M), (K2, N) = a_shape, b_shape
    assert K == K2, (a.shape, b.shape, form)
    if tiles is None:
        tm = _pick(M, (1024, 1280, 1408, 512, 256))
        tn = _pick(N, (1280, 1024, 1408, 512, 256))
        tk = K if K <= 2048 else _pick(K, (D_FF, 2560))
        if tk == D_FF:
            tm = _pick(M, (512, 256))
    else:
        tm, tn, tk = tiles
    nk = K // tk
    ah, bh = a_part[0], b_part[0]
    if form == "nn":
        a_spec = pl.BlockSpec((tm, tk), lambda i, j, k: (i, k + ah * nk))
        b_spec = pl.BlockSpec((tk, tn), lambda i, j, k: (k, j + bh * (N // tn)))
        dims = ((1,), (0,))
    elif form == "nt":
        a_spec = pl.BlockSpec((tm, tk), lambda i, j, k: (i, k + ah * nk))
        b_spec = pl.BlockSpec((tn, tk), lambda i, j, k: (j, k + bh * nk))
        dims = ((1,), (1,))
    else:
        a_spec = pl.BlockSpec((tk, tm), lambda i, j, k: (k, i + ah * (M // tm)))
        b_spec = pl.BlockSpec((tk, tn), lambda i, j, k: (k, j + bh * (N // tn)))
        dims = ((0,), (0,))
    o_spec = pl.BlockSpec((tm, tn), lambda i, j, k: (i, j))
    in_specs = [a_spec, b_spec]
    args = [a, b]
    if res is not None:
        in_specs.append(o_spec)
        args.append(res)

    def body(*refs):
        a_ref, b_ref = refs[0], refs[1]
        r_ref = refs[2] if res is not None else None
        o_ref = refs[3] if res is not None else refs[2]
        acc = refs[-1] if nk > 1 else None
        k = pl.program_id(2)
        d = lax.dot_general(a_ref[...].astype(BF), b_ref[...].astype(BF), (dims, ((), ())), preferred_element_type=F32)

        def finish(r):
            if res is not None:
                r = r + r_ref[...].astype(F32)
            o_ref[...] = r.astype(o_ref.dtype)

        if nk == 1:
            finish(d)
        else:
            @pl.when(k == 0)
            def _():
                acc[...] = d

            @pl.when((k > 0) & (k < nk - 1))
            def _():
                acc[...] += d

            @pl.when(k == nk - 1)
            def _():
                finish(acc[...] + d)

    return _pcall(
        body, grid=(M // tm, N // tn, nk), in_specs=in_specs, out_specs=o_spec,
        out_shape=SDS((M, N), out_dtype), scratch_shapes=[pltpu.VMEM((tm, tn), F32)] if nk > 1 else [],
        compiler_params=_cp(VMEM_BIG, dimension_semantics=("parallel", "parallel", "arbitrary")),
        name=name)(*args)


_MXU_COLS = 256


def _mm_fused(a, b, form, name, extras, out_dtypes, epi):
    M, K = a.shape
    N = b.shape[1] if form == "nn" else b.shape[0]
    assert K <= 2048 and K == (b.shape[0] if form == "nn" else b.shape[1])
    tm = _pick(M, (1024, 512, 256))
    tn = _pick(N, (1280, 1024, 1408, 512, 256))
    ne = len(extras)
    dims = ((1,), (0,)) if form == "nn" else ((1,), (1,))

    def body(*refs):
        a_ref, b_ref, ex, outs = refs[0], refs[1], refs[2:2 + ne], refs[2 + ne:]
        av = a_ref[...].astype(BF)
        for c0 in range(0, tn, _MXU_COLS):
            cols = slice(c0, min(c0 + _MXU_COLS, tn))
            bc = b_ref[:, cols] if form == "nn" else b_ref[cols, :]
            d = lax.dot_general(av, bc.astype(BF), (dims, ((), ())), preferred_element_type=F32)
            for o_ref, r in zip(outs, epi(d, *[e[:, cols] for e in ex])):
                o_ref[:, cols] = r.astype(o_ref.dtype)

    a_spec = pl.BlockSpec((tm, K), lambda i, j: (i, 0))
    b_spec = pl.BlockSpec((K, tn), lambda i, j: (0, j)) if form == "nn" else pl.BlockSpec((tn, K), lambda i, j: (j, 0))
    o_spec = pl.BlockSpec((tm, tn), lambda i, j: (i, j))
    return _pcall(
        body, grid=(M // tm, N // tn), in_specs=[a_spec, b_spec] + [o_spec] * ne, out_specs=[o_spec] * len(out_dtypes),
        out_shape=[SDS((M, N), dt) for dt in out_dtypes],
        compiler_params=_cp(VMEM_BIG, dimension_semantics=("parallel", "parallel")), name=name)(a, b, *extras)


def _rms(x, w):
    return x * lax.rsqrt(jnp.mean(x * x, axis=-1, keepdims=True) + RMS_EPS) * w


def _rms_fwd(x, w, name):
    R, D = x.shape
    tr = 256

    def body(x_ref, w_ref, o_ref):
        o_ref[...] = _rms(x_ref[...], w_ref[...]).astype(o_ref.dtype)

    return _pcall(
        body, grid=(R // tr,),
        in_specs=[pl.BlockSpec((tr, D), lambda i: (i, 0)), pl.BlockSpec((1, D), lambda i: (0, 0))],
        out_specs=pl.BlockSpec((tr, D), lambda i: (i, 0)), out_shape=SDS((R, D), BF), name=name)(x, w)


def _rms_bwd(x, w, dh, dres, name):
    R, D = x.shape
    tr = 256
    want_dx = dres is not None
    dhs = dh if isinstance(dh, (tuple, list)) else (dh,)
    nd = len(dhs)

    def body(*refs):
        x_ref, w_ref, dh_refs = refs[0], refs[1], refs[2:2 + nd]
        if want_dx:
            dr_ref, dx_ref, dxb_ref, dw_ref = refs[2 + nd:]
        else:
            dw_ref = refs[2 + nd]

        @pl.when(pl.program_id(0) == 0)
        def _():
            dw_ref[...] = jnp.zeros_like(dw_ref)

        dhv = jnp.concatenate([r[...].astype(F32) for r in dh_refs], axis=1) if nd > 1 else dh_refs[0][...].astype(F32)
        _, vjp = jax.vjp(_rms, x_ref[...], w_ref[...])
        dx, dw = vjp(dhv)
        dw_ref[...] += dw
        if want_dx:
            dx = dx + dr_ref[...]
            dx_ref[...] = dx
            dxb_ref[...] = dx.astype(BF)

    row = pl.BlockSpec((tr, D), lambda i: (i, 0))
    part = pl.BlockSpec((tr, D // nd), lambda i: (i, 0))
    vec = pl.BlockSpec((1, D), lambda i: (0, 0))
    if want_dx:
        return _pcall(body, grid=(R // tr,), in_specs=[row, vec] + [part] * nd + [row], out_specs=[row, row, vec],
                      out_shape=[SDS((R, D), F32), SDS((R, D), BF), SDS((1, D), F32)], name=name)(x, w, *dhs, dres)
    return _pcall(body, grid=(R // tr,), in_specs=[row, vec] + [part] * nd, out_specs=vec,
                  out_shape=SDS((1, D), F32), name=name)(x, w, *dhs)


def _final_loss(x3, w, tgt):
    R, D = x3.shape
    tr = 256

    def loss_fn(xv, wv, tv):
        err = _rms(xv, wv) - tv
        return 0.5 * jnp.sum(jnp.mean(err * err, axis=-1))

    def body(x_ref, w_ref, t_ref, loss_ref, dx_ref, dxb_ref, dw_ref):
        @pl.when(pl.program_id(0) == 0)
        def _():
            dw_ref[...] = jnp.zeros_like(dw_ref)
            loss_ref[...] = jnp.zeros_like(loss_ref)

        tv = t_ref[...]
        val, vjp = jax.vjp(lambda a, b: loss_fn(a, b, tv), x_ref[...], w_ref[...])
        dx, dw = vjp(jnp.ones((), F32))
        dx_ref[...] = dx
        dxb_ref[...] = dx.astype(BF)
        dw_ref[...] += dw
        loss_ref[...] += jnp.full(loss_ref.shape, val, F32)

    row = pl.BlockSpec((tr, D), lambda i: (i, 0))
    vec = pl.BlockSpec((1, D), lambda i: (0, 0))
    one = pl.BlockSpec((1, LANE), lambda i: (0, 0))
    return _pcall(body, grid=(R // tr,), in_specs=[row, vec, row], out_specs=[one, row, row, vec],
                          out_shape=[SDS((1, LANE), F32), SDS((R, D), F32), SDS((R, D), BF), SDS((1, D), F32)],
                          name="final_loss")(x3, w, tgt)


NEG = -1e30


def _att_blocks(S, d):
    L = S // d
    nb = L // LANE
    return nb, (2 if nb > 1 else 1)


def _att_rows(S, d, idx):
    nb, nkb = _att_blocks(S, d)
    r = idx // nb
    n = idx % nb
    kb = jnp.maximum(n - 1, 0)
    if d == 1:
        qrows = pl.ds(pl.multiple_of(n * LANE, LANE), LANE)
        krows = pl.ds(pl.multiple_of(kb * LANE, LANE), LANE * nkb)
    else:
        qrows = pl.ds(r + n * (LANE * d), LANE, stride=d)
        krows = pl.ds(r + kb * (LANE * d), LANE * nkb, stride=d)
    qpos = n * LANE + lax.broadcasted_iota(jnp.int32, (LANE, LANE * nkb), 0)
    kpos = kb * LANE + lax.broadcasted_iota(jnp.int32, (LANE, LANE * nkb), 1)
    return qrows, krows, qpos - kpos


def _slab(col):
    return pl.BlockSpec((SEQ, LANE), lambda h, col=col: (0, col + h))


def _widen(slabs, wide):
    for t, ref in enumerate(slabs):
        wide[t] = ref[...].astype(F32)
    return tuple(tuple(wide.at[3 * kind + g] for g in range(3)) for kind in range(3))


def _attn_fwd(proj):
    S = proj.shape[0]
    scale = HEAD_DIM ** -0.5

    def body(q0, q1, q2, k0, k1, k2, v0, v1, v2, o_ref, lse_ref, og, lg, wide):
        qs, ks, vs = _widen((q0, q1, q2, k0, k1, k2, v0, v1, v2), wide)
        for g, (window, d) in enumerate(ATT_GROUPS):
            def blk(idx, carry, g=g, window=window, d=d):
                qrows, krows, dist = _att_rows(S, d, idx)
                q = qs[g][qrows, :].astype(BF)
                k = ks[g][krows, :].astype(BF)
                v = vs[g][krows, :].astype(BF)
                s = lax.dot_general(q, k, (((1,), (1,)), ((), ())), preferred_element_type=F32) * scale
                s = jnp.where((dist >= 0) & (dist <= window // d), s, NEG)
                m = jnp.max(s, axis=-1, keepdims=True)
                p = jnp.exp(s - m)
                l = jnp.sum(p, axis=-1, keepdims=True)
                o = jnp.dot((p / l).astype(BF), v, preferred_element_type=F32)
                og[g, qrows, :] = o
                lg[g, qrows, :] = jnp.broadcast_to(m + jnp.log(l), (LANE, LANE))
                return carry
            lax.fori_loop(0, S // LANE, blk, 0, unroll=8)

        def merge(t, carry):
            rows = pl.ds(pl.multiple_of(t * 256, 256), 256)
            l0, l1, l2 = lg[0, rows, :], lg[1, rows, :], lg[2, rows, :]
            m = jnp.maximum(jnp.maximum(l0, l1), l2)
            w0, w1, w2 = jnp.exp(l0 - m), jnp.exp(l1 - m), jnp.exp(l2 - m)
            den = w0 + w1 + w2
            o_ref[rows, :] = (w0 * og[0, rows, :] + w1 * og[1, rows, :] + w2 * og[2, rows, :]) / den
            lse_ref[rows, :] = m + jnp.log(den)
            return carry
        lax.fori_loop(0, S // 256, merge, 0)

    ins = [_slab(QA0 + 4 * g) for g in range(3)] + [_slab(KA0 + 4 * g) for g in range(3)] + [_slab(VA0 + 4 * g) for g in range(3)]
    out = pl.BlockSpec((SEQ, LANE), lambda h: (0, h))
    return _pcall(
        body, grid=(ATT_HEADS,), in_specs=ins, out_specs=[out, out],
        out_shape=[SDS((S, ATT_OUT), F32), SDS((S, ATT_OUT), F32)],
        scratch_shapes=[pltpu.VMEM((3, S, LANE), F32), pltpu.VMEM((3, S, LANE), F32), pltpu.VMEM((9, S, LANE), F32)],
        compiler_params=_cp(VMEM_BIG), name="attn_fwd")(*([proj] * 9))


def _attn_bwd(proj, o_att, lse, do_att, dproj):
    S = proj.shape[0]
    scale = HEAD_DIM ** -0.5

    def body(q0, q1, q2, k0, k1, k2, v0, v1, v2, o_ref, lse_ref, do_ref, dp_in, dp_ref, dqa, dka, dva, delta, stage, sem, wide):
        qs, ks, vs = _widen((q0, q1, q2, k0, k1, k2, v0, v1, v2), wide)
        head = pl.program_id(0)

        def slab_copies(g):
            return [pltpu.make_async_copy(stage.at[t], dp_ref.at[:, pl.ds(pl.multiple_of((base + 4 * g + head) * LANE, LANE), LANE)],
                                          sem.at[t]) for t, base in enumerate((QA0, KA0, VA0))]

        delta[...] = jnp.broadcast_to(jnp.sum(do_ref[...] * o_ref[...], axis=-1, keepdims=True), delta.shape)
        for g, (window, d) in enumerate(ATT_GROUPS):
            dka[...] = jnp.zeros_like(dka)
            dva[...] = jnp.zeros_like(dva)

            def blk(idx, carry, g=g, window=window, d=d):
                qrows, krows, dist = _att_rows(S, d, idx)
                q = qs[g][qrows, :].astype(BF)
                k = ks[g][krows, :].astype(BF)
                v = vs[g][krows, :].astype(BF)
                do = do_ref[qrows, :]
                s = lax.dot_general(q, k, (((1,), (1,)), ((), ())), preferred_element_type=F32) * scale
                p = jnp.exp(s - lse_ref[qrows, :][:, 0:1])
                p = jnp.where((dist >= 0) & (dist <= window // d), p, 0.0)
                dob = do.astype(BF)
                dva[krows, :] += lax.dot_general(p.astype(BF), dob, (((0,), (0,)), ((), ())), preferred_element_type=F32)
                dp = lax.dot_general(dob, v, (((1,), (1,)), ((), ())), preferred_element_type=F32)
                ds = (p * (dp - delta[qrows, :][:, 0:1]) * scale).astype(BF)
                dqa[qrows, :] = jnp.dot(ds, k, preferred_element_type=F32)
                dka[krows, :] += lax.dot_general(ds, q, (((0,), (0,)), ((), ())), preferred_element_type=F32)
                return carry
            lax.fori_loop(0, S // LANE, blk, 0, unroll=8)
            if g > 0:
                for cp in slab_copies(g - 1):
                    cp.wait()
            stage[0] = dqa[...].astype(BF)
            stage[1] = dka[...].astype(BF)
            stage[2] = dva[...].astype(BF)
            for cp in slab_copies(g):
                cp.start()
        for cp in slab_copies(len(ATT_GROUPS) - 1):
            cp.wait()

    cols = [QA0 + 4 * g for g in range(3)] + [KA0 + 4 * g for g in range(3)] + [VA0 + 4 * g for g in range(3)]
    ins = [_slab(c) for c in cols]
    one = pl.BlockSpec((SEQ, LANE), lambda h: (0, h))
    return _pcall(
        body, grid=(ATT_HEADS,), in_specs=ins + [one, one, one, _ANY], out_specs=_ANY,
        out_shape=SDS((S, IN_WIDTH), BF), aliases={12: 0},
        scratch_shapes=[pltpu.VMEM((S, LANE), F32)] * 4 + [pltpu.VMEM((3, S, LANE), BF), pltpu.SemaphoreType.DMA((3,)),
                                                           pltpu.VMEM((9, S, LANE), F32)],
        compiler_params=_cp(VMEM_BIG), name="attn_bwd")(*([proj] * 9), o_att, lse, do_att, dproj)


def _bdot(a, b, dims):
    return lax.dot_general(a.astype(BF), b.astype(BF), (dims, ((), ())), preferred_element_type=F32)


def _hg_chunk(qh, fh, ih, gh, lbraw, nw, st):
    C = HG_CHUNK
    mx = jnp.maximum(lbraw[0:1], lbraw[1:2])
    e0 = jnp.exp(lbraw[0:1] - mx)
    e1 = jnp.exp(lbraw[1:2] - mx)
    lb = e0 / (e0 + e1)
    f = lb + (1.0 - lb) * jax.nn.sigmoid(fh)
    logf = jnp.log(f)
    k = 1.0 - f
    q = qh * jax.nn.sigmoid(qh)
    causal = lax.broadcasted_iota(jnp.int32, (C, C), 0) >= lax.broadcasted_iota(jnp.int32, (C, C), 1)
    b = lax.dot_general(causal.astype(F32), logf, (((1,), (0,)), ((), ())), precision=lax.Precision.HIGHEST,
                        preferred_element_type=F32)
    b_last = b[C - 1:C]
    st_out = st * jnp.exp(b_last) + _bdot(ih, k * jnp.exp(b_last - b), ((0,), (0,)))
    inter = _bdot(q * jnp.exp(b), st, ((1,), (1,)))
    b_ref = b[C // 2:C // 2 + 1]
    a = _bdot(q * jnp.exp(b - b_ref), k * jnp.exp(b_ref - b), ((1,), (1,)))
    a = jnp.where(causal, a, 0.0)
    o = inter + _bdot(a, ih, ((1,), (0,)))
    o = o * lax.rsqrt(jnp.mean(o * o, axis=-1, keepdims=True) + RMS_EPS) * nw
    return o * (gh * jax.nn.sigmoid(gh)), st_out


def _to_heads(x):
    return jnp.concatenate([x[None, :, h * LANE:(h + 1) * LANE] for h in range(HG_HEADS)], axis=0)


def _from_heads(x):
    return jnp.concatenate([x[h] for h in range(HG_HEADS)], axis=1)


def _hdot(a, b, ca, cb):
    return lax.dot_general(a.astype(BF), b.astype(BF), (((ca,), (cb,)), ((0,), (0,))), preferred_element_type=F32)


def _hg_chunk_all(qh, fh, ih, gh, lbraw, nw, st):
    C = HG_CHUNK
    mx = jnp.maximum(lbraw[0:1], lbraw[1:2])
    e0 = jnp.exp(lbraw[0:1] - mx)
    e1 = jnp.exp(lbraw[1:2] - mx)
    lb = e0 / (e0 + e1)
    f = lb + (1.0 - lb) * jax.nn.sigmoid(fh)
    logf = jnp.log(f)
    k = 1.0 - f
    q = qh * jax.nn.sigmoid(qh)
    causal = lax.broadcasted_iota(jnp.int32, (C, C), 0) >= lax.broadcasted_iota(jnp.int32, (C, C), 1)
    b = lax.dot_general(causal.astype(F32), logf, (((1,), (0,)), ((), ())), precision=lax.Precision.HIGHEST,
                        preferred_element_type=F32)
    b_last = b[C - 1:C]
    b_ref = b[C // 2:C // 2 + 1]
    v3 = _to_heads(ih)
    st_out = st * _to_heads(jnp.exp(b_last)) + _hdot(v3, _to_heads(k * jnp.exp(b_last - b)), 1, 1)
    inter = _hdot(_to_heads(q * jnp.exp(b)), st, 2, 2)
    a = _hdot(_to_heads(q * jnp.exp(b - b_ref)), _to_heads(k * jnp.exp(b_ref - b)), 2, 2)
    a = jnp.where(causal[None], a, 0.0)
    o = inter + _hdot(a, v3, 2, 1)
    o = o * lax.rsqrt(jnp.mean(o * o, axis=-1, keepdims=True) + RMS_EPS) * nw
    return _from_heads(o) * (gh * jax.nn.sigmoid(gh)), st_out


def _head_cols(h):
    return slice(h * LANE, (h + 1) * LANE)


def _hg_in_specs(chunk_of):
    half = 4 * LANE
    specs = []
    for base in (QH0, FH0, IH0, GH0):
        for part in range(2):
            specs.append(pl.BlockSpec((HG_CHUNK, half), lambda n, col=base // 4 + part: (chunk_of(n), col)))
    return specs


def _hg_sections(sec):
    return [jnp.concatenate([sec[2 * s][...], sec[2 * s + 1][...]], axis=1).astype(F32) for s in range(4)]


def _hg_fwd(proj, lb, nw):
    S = proj.shape[0]
    C = HG_CHUNK
    NC = S // C

    def body(*refs):
        sec = refs[:8]
        lb_ref, nw_ref, o_ref, st_ref, st_scr = refs[8:]

        @pl.when(pl.program_id(0) == 0)
        def _():
            st_scr[...] = jnp.zeros_like(st_scr)

        st = st_scr[...]
        st_ref[:, 0] = st
        o, st_out = _hg_chunk_all(*_hg_sections(sec), lb_ref[...], nw_ref[...], st)
        o_ref[...] = o.astype(o_ref.dtype)
        st_scr[...] = st_out

    return _pcall(
        body, grid=(NC,),
        in_specs=_hg_in_specs(lambda n: n) + [pl.BlockSpec((2, HG_WIDTH), lambda n: (0, 0)), pl.BlockSpec((1, LANE), lambda n: (0, 0))],
        out_specs=[pl.BlockSpec((C, HG_WIDTH), lambda n: (n, 0)), pl.BlockSpec((HG_HEADS, 1, LANE, LANE), lambda n: (0, n, 0, 0))],
        out_shape=[SDS((S, HG_WIDTH), BF), SDS((HG_HEADS, NC, LANE, LANE), F32)],
        scratch_shapes=[pltpu.VMEM((HG_HEADS, LANE, LANE), F32)], name="hg_fwd")(*([proj] * 8), lb, nw)


def _hg_bwd(proj, lb, nw, states, do_hg, dproj):
    S = proj.shape[0]
    C = HG_CHUNK
    NC = S // C

    def body(*refs):
        sec = refs[:8]
        lb_ref, nw_ref, st_ref, do_ref, dp_in, dp_ref, dlb_ref, dnw_ref, dst_scr, stage, sem = refs[8:]
        n = pl.program_id(0)
        slot = n % 2

        def chunk_copies(s):
            rows = pl.ds(pl.multiple_of((NC - 1 - n) * C, C), C)
            return [pltpu.make_async_copy(stage.at[s, t], dp_ref.at[rows, pl.ds((QH0 + 8 * t) * LANE, HG_WIDTH)], sem.at[s, t])
                    for t in range(4)]

        @pl.when(n == 0)
        def _():
            dst_scr[...] = jnp.zeros_like(dst_scr)
            dlb_ref[...] = jnp.zeros_like(dlb_ref)
            dnw_ref[...] = jnp.zeros_like(dnw_ref)

        _, vjp = jax.vjp(_hg_chunk_all, *_hg_sections(sec), lb_ref[...], nw_ref[...], st_ref[:, 0])
        dq, df, di, dg, dlb, dnw, dst = vjp((do_ref[...].astype(F32), dst_scr[...]))
        dlb_ref[...] += dlb
        dnw_ref[...] += dnw
        dst_scr[...] = dst

        @pl.when(n >= 2)
        def _():
            for cp in chunk_copies(slot):
                cp.wait()

        for t, val in enumerate((dq, df, di, dg)):
            stage[slot, t] = val.astype(BF)
        for cp in chunk_copies(slot):
            cp.start()

        @pl.when(n == NC - 1)
        def _():
            for cp in chunk_copies(1 - slot) + chunk_copies(slot):
                cp.wait()

    rev = lambda n: NC - 1 - n
    ospec = pl.BlockSpec((C, HG_WIDTH), lambda n: (rev(n), 0))
    lbs = pl.BlockSpec((2, HG_WIDTH), lambda n: (0, 0))
    nws = pl.BlockSpec((1, LANE), lambda n: (0, 0))
    outs = _pcall(
        body, grid=(NC,),
        in_specs=_hg_in_specs(rev) + [lbs, nws, pl.BlockSpec((HG_HEADS, 1, LANE, LANE), lambda n: (0, rev(n), 0, 0)), ospec, _ANY],
        out_specs=[_ANY, lbs, nws],
        out_shape=[SDS((S, IN_WIDTH), BF), SDS((2, HG_WIDTH), F32), SDS((1, LANE), F32)], aliases={12: 0},
        scratch_shapes=[pltpu.VMEM((HG_HEADS, LANE, LANE), F32), pltpu.VMEM((2, 4, C, HG_WIDTH), BF), pltpu.SemaphoreType.DMA((2, 4))],
        name="hg_bwd")(*([proj] * 8), lb, nw, states, do_hg, dproj)
    return outs


def _merge(ga, gb, ba, bb):
    return jax.nn.sigmoid(ga) * ba + jax.nn.sigmoid(gb) * bb


_MT = 512


def _gate_specs():
    ga = pl.BlockSpec((_MT, _MT), lambda i, j: (i, GA0 * LANE // _MT + j))
    gb = pl.BlockSpec((_MT, _MT), lambda i, j: (i, GB0 * LANE // _MT + j))
    t = pl.BlockSpec((_MT, _MT), lambda i, j: (i, j))
    return ga, gb, t


def _merge_fwd(proj, ba, bb):
    S, D = ba.shape
    ga, gb, t = _gate_specs()

    def body(ga_ref, gb_ref, ba_ref, bb_ref, o_ref):
        o_ref[...] = _merge(ga_ref[...].astype(F32), gb_ref[...].astype(F32), ba_ref[...].astype(F32),
                            bb_ref[...].astype(F32)).astype(o_ref.dtype)

    return _pcall(body, grid=(S // _MT, D // _MT), in_specs=[ga, gb, t, t], out_specs=t,
                          out_shape=SDS((S, D), BF), name="merge_fwd")(proj, proj, ba, bb)


def _merge_bwd(proj, ba, bb, dm):
    S, D = ba.shape
    ga, gb, t = _gate_specs()

    def body(ga_ref, gb_ref, ba_ref, bb_ref, dm_ref, dba_ref, dbb_ref, dp_ref, stage, sem):
        _, vjp = jax.vjp(_merge, ga_ref[...].astype(F32), gb_ref[...].astype(F32), ba_ref[...].astype(F32),
                         bb_ref[...].astype(F32))
        dga, dgb, dba, dbb = vjp(dm_ref[...].astype(F32))
        dba_ref[...] = dba.astype(BF)
        dbb_ref[...] = dbb.astype(BF)
        stage[0] = dga.astype(BF)
        stage[1] = dgb.astype(BF)
        rows = pl.ds(pl.multiple_of(pl.program_id(0) * _MT, _MT), _MT)
        cps = []
        for k, base in enumerate((GA0, GB0)):
            cols = pl.ds(pl.multiple_of(base * LANE + pl.program_id(1) * _MT, LANE), _MT)
            cps.append(pltpu.make_async_copy(stage.at[k], dp_ref.at[rows, cols], sem.at[k]))
            cps[-1].start()
        for cp in cps:
            cp.wait()

    return _pcall(body, grid=(S // _MT, D // _MT), in_specs=[ga, gb, t, t, t], out_specs=[t, t, _ANY],
                  out_shape=[SDS((S, D), BF), SDS((S, D), BF), SDS((S, IN_WIDTH), BF)],
                  scratch_shapes=[pltpu.VMEM((2, _MT, _MT), BF), pltpu.SemaphoreType.DMA((2,))],
                  name="merge_bwd")(proj, proj, ba, bb, dm)


def _swiglu(a, b):
    return a * jax.nn.sigmoid(a) * b


def _swiglu_fwd(a, b):
    S, F = a.shape
    tf = _pick(F, (1408, 512))
    t = pl.BlockSpec((_MT, tf), lambda i, j: (i, j))

    def body(a_ref, b_ref, o_ref):
        o_ref[...] = _swiglu(a_ref[...].astype(F32), b_ref[...].astype(F32)).astype(o_ref.dtype)

    return _pcall(body, grid=(S // _MT, F // tf), in_specs=[t, t], out_specs=t,
                          out_shape=SDS((S, F), BF), name="swiglu_fwd")(a, b)


def _swiglu_bwd(a, b, dg):
    S, F = a.shape
    tf = _pick(F, (1408, 512))
    t = pl.BlockSpec((_MT, tf), lambda i, j: (i, j))

    def body(a_ref, b_ref, dg_ref, da_ref, db_ref):
        _, vjp = jax.vjp(_swiglu, a_ref[...].astype(F32), b_ref[...].astype(F32))
        da, db = vjp(dg_ref[...].astype(F32))
        da_ref[...] = da.astype(BF)
        db_ref[...] = db.astype(BF)

    return _pcall(body, grid=(S // _MT, F // tf), in_specs=[t, t, t], out_specs=[t, t],
                          out_shape=[SDS((S, F), BF)] * 2, name="swiglu_bwd")(a, b, dg)


def _cross(q, k, v):
    s = _bdot(q, k, ((1,), (1,))) * (HEAD_DIM ** -0.5)
    e = jnp.exp(s - jnp.max(s, axis=-1, keepdims=True))
    p = e / jnp.sum(e, axis=-1, keepdims=True)
    return _bdot(p, v, ((1,), (0,)))


def _cross_specs():
    q = pl.BlockSpec((_MT, LANE), lambda h, i: (i, h))
    k = pl.BlockSpec((MEM_LEN, LANE), lambda h, i: (0, h))
    v = pl.BlockSpec((MEM_LEN, LANE), lambda h, i: (0, CROSS_HEADS + h))
    return q, k, v


def _cross_fwd(qc, kvc):
    S = qc.shape[0]
    q, k, v = _cross_specs()

    def body(q_ref, k_ref, v_ref, o_ref):
        o_ref[...] = _cross(q_ref[...], k_ref[...], v_ref[...]).astype(o_ref.dtype)

    return _pcall(body, grid=(CROSS_HEADS, S // _MT), in_specs=[q, k, v], out_specs=q,
                          out_shape=SDS((S, CROSS_WIDTH), BF), name="cross_fwd")(qc, kvc, kvc)


def _cross_bwd(qc, kvc, doc):
    S = qc.shape[0]
    q, k, v = _cross_specs()

    def body(q_ref, k_ref, v_ref, do_ref, dq_ref, dk_ref, dv_ref):
        @pl.when(pl.program_id(1) == 0)
        def _():
            dk_ref[...] = jnp.zeros_like(dk_ref)
            dv_ref[...] = jnp.zeros_like(dv_ref)

        _, vjp = jax.vjp(_cross, q_ref[...], k_ref[...], v_ref[...])
        dq, dk, dv = vjp(do_ref[...])
        dq_ref[...] = dq.astype(dq_ref.dtype)
        dk_ref[...] += dk
        dv_ref[...] += dv

    return _pcall(body, grid=(CROSS_HEADS, S // _MT), in_specs=[q, k, v, q], out_specs=[q, k, k],
                          out_shape=[SDS((S, CROSS_WIDTH), BF), SDS((MEM_LEN, CROSS_WIDTH), F32), SDS((MEM_LEN, CROSS_WIDTH), F32)],
                          name="cross_bwd")(qc, kvc, kvc, doc)


def _local_step(x, mem, tgt, p, ready):
    h = _rms_fwd(x, p["ln_mix"], "rms_mix")
    proj = _mm(h, p["w_in_a"], "nt", BF, "mm_proj_a", a_part=(0, 2))
    proj = _mm(h, p["w_in_b"], "nt", BF, "mm_proj_b", a_part=(1, 2), res=proj)
    o_att, lse = _attn_fwd(proj)
    o_hg, states = _hg_fwd(proj, p["hg_lb"], p["hg_norm"])
    ba = _mm(o_att, p["w_ba"], "nn", BF, "mm_ba")
    bb = _mm(o_hg, p["w_bb"], "nn", BF, "mm_bb")
    merged = _merge_fwd(proj, ba, bb)
    x1 = _mm(merged, p["w_out"], "nn", F32, "mm_out", res=x)
    hc = _rms_fwd(x1, p["ln_cross"], "rms_cross")
    qc = _mm(hc, p["wq"], "nn", F32, "mm_q")
    mn = _rms_fwd(mem, p["ln_mem"], "rms_mem")
    kvc = _mm(mn, p["wkv"], "nn", F32, "mm_kv")
    oc = _cross_fwd(qc, kvc)
    x2 = _mm(oc, p["wo"], "nn", F32, "mm_o", res=x1)
    hf = _rms_fwd(x2, p["ln_ffn"], "rms_ffn")
    a = _mm(hf, p["w1"], "nt", BF, "mm_w1")
    b, g = _mm_fused(hf, p["w3"], "nt", "mm_w3_swiglu", [a], [BF, BF], lambda d, av: (d, _swiglu(av.astype(F32), d)))
    x3 = _mm(g, p["w2"], "nn", F32, "mm_w2", res=x2)
    loss, dx3, dx3b, d_ln_final = _final_loss(x3, p["ln_final"], tgt)

    gw, gs = {}, {"ln_final": d_ln_final}
    da, db = _mm_fused(dx3b, p["w2"], "nt", "mm_dg_swiglu", [a, b], [BF, BF],
                       lambda d, av, bv: jax.vjp(_swiglu, av.astype(F32), bv.astype(F32))[1](d))
    gw["w2"] = _mm(g, dx3b, "tn", BF, "mm_dw2")
    dhf = _mm(da, p["w1"], "nn", F32, "mm_dhf1")
    dhf = _mm(db, p["w3"], "nn", BF, "mm_dhf3", res=dhf)
    gw["w1"] = _mm(da, hf, "tn", BF, "mm_dw1")
    gw["w3"] = _mm(db, hf, "tn", BF, "mm_dw3")
    ready("ffn", gw)
    dx2, dx2b, gs["ln_ffn"] = _rms_bwd(x2, p["ln_ffn"], dhf, dx3, "rms_ffn_bwd")
    doc = _mm(dx2b, p["wo"], "nt", F32, "mm_doc")
    gw["wo"] = _mm(oc, dx2b, "tn", BF, "mm_dwo")
    dqc, dk, dv = _cross_bwd(qc, kvc, doc)
    dkvc = jnp.concatenate([dk, dv], axis=1)
    dhc = _mm(dqc, p["wq"], "nt", BF, "mm_dhc")
    gw["wq"] = _mm(hc, dqc, "tn", BF, "mm_dwq")
    dmn = _mm(dkvc, p["wkv"], "nt", F32, "mm_dmn")
    gw["wkv"] = _mm(mn, dkvc, "tn", BF, "mm_dwkv")
    ready("cross", gw)
    gs["ln_mem"] = _rms_bwd(mem, p["ln_mem"], dmn, None, "rms_mem_bwd")
    dx1, dx1b, gs["ln_cross"] = _rms_bwd(x1, p["ln_cross"], dhc, dx2, "rms_cross_bwd")
    dmerged = _mm(dx1b, p["w_out"], "nt", BF, "mm_dmerged")
    gw["w_out"] = _mm(merged, dx1b, "tn", BF, "mm_dwout")
    dba, dbb, dproj = _merge_bwd(proj, ba, bb, dmerged)
    do_att = _mm(dba, p["w_ba"], "nt", F32, "mm_doatt")
    gw["w_ba"] = _mm(o_att, dba, "tn", BF, "mm_dwba")
    do_hg = _mm(dbb, p["w_bb"], "nt", F32, "mm_dohg")
    gw["w_bb"] = _mm(o_hg, dbb, "tn", BF, "mm_dwbb")
    ready("mix", gw)
    dproj = _attn_bwd(proj, o_att, lse, do_att, dproj)
    dproj, gs["hg_lb"], gs["hg_norm"] = _hg_bwd(proj, p["hg_lb"], p["hg_norm"], states, do_hg, dproj)
    ready("mixers_done", gw)
    for q in range(IN_PARTS):
        gw[f"w_in_q{q}"] = _mm(dproj, h, "tn", BF, f"mm_dwin_q{q}", b_part=(q, IN_PARTS))
        ready(f"in_q{q}", gw)
    dh_a = _mm(dproj, p["w_in_a"], "nn", BF, "mm_dh_a")
    ready("dh_half", gw)
    dh = (dh_a, _mm(dproj, p["w_in_b"], "nn", BF, "mm_dh_b"))
    grad_x, _, gs["ln_mix"] = _rms_bwd(x, p["ln_mix"], dh, dx1, "rms_mix_bwd")
    return loss, grad_x, gs


_MATS = (
    ("w_in_a", "w_in", IN_WIDTH, D_MODEL // 2, "row", IN_WIDTH // N_DEV, True),
    ("w_in_b", "w_in", IN_WIDTH, D_MODEL // 2, "row", IN_WIDTH // N_DEV, True),
    ("w_ba", "w_branch_a", ATT_OUT, D_MODEL, "col", D_MODEL // N_DEV, False),
    ("w_bb", "w_branch_b", HG_WIDTH, D_MODEL, "col", D_MODEL // N_DEV, False),
    ("w_out", "w_out", D_MODEL, D_MODEL, "row", D_MODEL // N_DEV, False),
    ("wq", "wq_cross", D_MODEL, CROSS_WIDTH, "row", D_MODEL // N_DEV, False),
    ("wkv", "wkv_cross", D_MODEL, 2 * CROSS_WIDTH, "row", D_MODEL // N_DEV, False),
    ("wo", "wo_cross", CROSS_WIDTH, D_MODEL, "col", D_MODEL // N_DEV, False),
    ("w1", "w1", D_FF, D_MODEL, "row", D_FF // N_DEV, True),
    ("w3", "w3", D_FF, D_MODEL, "row", D_FF // N_DEV, True),
    ("w2", "w2", D_FF, D_MODEL, "row", D_FF // N_DEV, False),
    ("w_in_q0", "w_in", IN_WIDTH, D_MODEL // 2, "row", IN_WIDTH // N_DEV, True),
    ("w_in_q1", "w_in", IN_WIDTH, D_MODEL // 2, "row", IN_WIDTH // N_DEV, True),
)
IN_PARTS = 2
_INDEX = {m[0]: i for i, m in enumerate(_MATS)}


def _unit_shape(i):
    _, _, K, N, kind, sh, _ = _MATS[i]
    return (sh, N) if kind == "row" else (K, sh)


def _win(ref, i, chip, par, half=None):
    _, _, K, _, kind, sh, _ = _MATS[i]
    if kind == "row":
        rows, start = (sh, 0) if half is None else (sh // 2, half * (sh // 2))
        return ref.at[pl.ds(pl.multiple_of((2 * chip + par) * sh + start, 16), rows), :]
    rows, start = (K, 0) if half is None else (K // 2, half * (K // 2))
    return ref.at[pl.ds(start, rows), pl.ds(pl.multiple_of((2 * chip + par) * sh, LANE), sh)]


def _my_place():
    x, y, c = lax.axis_index("x"), lax.axis_index("y"), lax.axis_index("c")
    chips = [(1 - x, y), (x, 1 - y), (1 - x, 1 - y)]
    return x, y, c, chips


def _cast_unit(w, name, half=None):
    K, N = w.shape
    if half is not None:
        N = N // 2
    tr = _pick(K, (256, 400, 352))

    def body(w_ref, o_ref):
        o_ref[...] = w_ref[...].astype(BF)

    src = pl.BlockSpec((tr, N), lambda i: (i, half or 0))
    dst = pl.BlockSpec((tr, N), lambda i: (i, 0))
    return _pcall(body, grid=(K // tr,), in_specs=[src], out_specs=dst, out_shape=SDS((K, N), BF), name=name)(w)


def _handshake(peers):
    barrier = pltpu.get_barrier_semaphore()
    for peer in peers:
        pl.semaphore_signal(barrier, inc=1, device_id=peer, device_id_type=MESH)
    pl.semaphore_wait(barrier, len(peers))


def _sequencer(body, out_type, sems, cid, name):
    return pl.kernel(body, out_type=out_type, mesh=plsc.ScalarSubcoreMesh(axis_name="seq", num_cores=1),
                     scratch_types=sems, compiler_params=pltpu.CompilerParams(collective_id=cid), name=name)


def _all_gather(units, idx, cid, name):
    n = len(idx)

    def body(*refs):
        us, fulls = refs[:n], refs[n:2 * n]
        ssem, rsem, lsem = refs[2 * n:]
        x, y, c, _ = _my_place()
        xn, yn, dg = (1 - x, y), (x, 1 - y), (1 - x, 1 - y)
        me, sib = (x, y, c), (x, y, 1 - c)
        _handshake([sib, (*xn, c), (*yn, c)])

        def win(i, chip_xy, par, half=None):
            return _win(fulls[i], idx[i], 2 * chip_xy[0] + chip_xy[1], par, half)

        def rcopy(i, k, src, dst, to):
            return pltpu.make_async_remote_copy(src_ref=src, dst_ref=dst, send_sem=ssem.at[i, k], recv_sem=rsem.at[i, k],
                                                device_id=to, device_id_type=MESH)

        def landed(i, k, blk):
            rcopy(i, k, blk, blk, me).wait_recv()

        sends, locs = [], []

        def send(i, k, blk, to, src=None):
            cp = rcopy(i, k, blk if src is None else src, blk, to)
            cp.start()
            sends.append(cp)

        for i in range(n):
            mine = win(i, (x, y), c)
            loc = pltpu.make_async_copy(us[i], mine, lsem.at[i])
            loc.start()
            locs.append(loc)
            send(i, 0, mine, sib, src=us[i])
            send(i, 1, mine, (*xn, c), src=us[i])
            send(i, 2, mine, (*yn, c), src=us[i])
        for i in range(n):
            landed(i, 1, win(i, xn, c))
            send(i, 3, win(i, xn, c, 0), (*yn, c))
            send(i, 5, win(i, xn, c), sib)
            landed(i, 2, win(i, yn, c))
            send(i, 4, win(i, yn, c, 1), (*xn, c))
            send(i, 6, win(i, yn, c), sib)
        for i in range(n):
            landed(i, 3, win(i, dg, c, 0))
            landed(i, 4, win(i, dg, c, 1))
            send(i, 7, win(i, dg, c), sib)
        for i in range(n):
            landed(i, 0, win(i, (x, y), 1 - c))
            landed(i, 5, win(i, xn, 1 - c))
            landed(i, 6, win(i, yn, 1 - c))
            landed(i, 7, win(i, dg, 1 - c))
        for cp in sends:
            cp.wait_send()
        for loc in locs:
            loc.wait()

    out_type = [SDS((_MATS[i][2], _MATS[i][3]), BF) for i in idx]
    sems = [pltpu.SemaphoreType.DMA((n, 8)), pltpu.SemaphoreType.DMA((n, 8)), pltpu.SemaphoreType.DMA((n,))]
    return _sequencer(body, out_type, sems, cid, name)(*units)


def _rs_sibling(grads, idx, cid, name):
    n = len(idx)

    def body(*refs):
        gs, bufs = refs[:n], refs[n:2 * n]
        ssem, rsem = refs[2 * n:]
        x, y, c, _ = _my_place()
        _handshake([(x, y, 1 - c)])
        cps = []
        for i in range(n):
            for k in range(4):
                cp = pltpu.make_async_remote_copy(
                    src_ref=_win(gs[i], idx[i], k, 1 - c), dst_ref=bufs[i].at[k], send_sem=ssem.at[i, k], recv_sem=rsem.at[i, k],
                    device_id=(x, y, 1 - c), device_id_type=MESH)
                cp.start()
                cps.append(cp)
        for cp in cps:
            cp.wait()

    out_type = [SDS((4, *_unit_shape(i)), BF) for i in idx]
    sems = [pltpu.SemaphoreType.DMA((n, 4)), pltpu.SemaphoreType.DMA((n, 4))]
    return _sequencer(body, out_type, sems, cid, name)(*grads)


def _sibling_sum_tc(grad, i, cid, name):
    unit = _unit_shape(i)

    def body(g_ref, t_ref, theirs, mine, summed, ssem, rsem, lsem, osem):
        x, y, c, _ = _my_place()
        _handshake([(x, y, 1 - c)])
        sends, loads = [], []
        for k in range(4):
            sends.append(pltpu.make_async_remote_copy(
                src_ref=_win(g_ref, i, k, 1 - c), dst_ref=theirs.at[k], send_sem=ssem.at[k], recv_sem=rsem.at[k],
                device_id=(x, y, 1 - c), device_id_type=MESH))
            loads.append(pltpu.make_async_copy(_win(g_ref, i, k, c), mine.at[k], lsem.at[k]))
            sends[-1].start()
            loads[-1].start()
        stores = []
        for k in range(4):
            loads[k].wait()
            sends[k].wait_recv()
            summed[k] = (mine[k].astype(F32) + theirs[k].astype(F32)).astype(BF)
            stores.append(pltpu.make_async_copy(summed.at[k], t_ref.at[k], osem.at[k]))
            stores[-1].start()
        for k in range(4):
            sends[k].wait_send()
            stores[k].wait()

    block = pltpu.VMEM((4, *unit), BF)
    return _pcall(body, in_specs=[_ANY], out_specs=_ANY, out_shape=SDS((4, *unit), BF),
                  scratch_shapes=[block, block, block] + [pltpu.SemaphoreType.DMA((4,))] * 4,
                  compiler_params=_cp(VMEM_BIG, collective_id=cid), name=name)(grad)


def _chip_sum(g, sib, i, place, name):
    _, _, K, N, kind, sh, _ = _MATS[i]

    def body(pref, g_ref, s_ref, o_ref):
        o_ref[...] = (g_ref[...].astype(F32) + s_ref[...].astype(F32)).astype(BF)

    if kind == "row":
        tr = _pick(sh, (800, 352, 256))
        per = sh // tr
        grid = (4, per)
        g_spec = pl.BlockSpec((tr, N), lambda k, r, pref: ((2 * k + pref[0]) * per + r, 0))
        u_spec = pl.BlockSpec((None, tr, N), lambda k, r, pref: (k, r, 0))
    else:
        grid = (4,)
        g_spec = pl.BlockSpec((K, sh), lambda k, pref: (0, 2 * k + pref[0]))
        u_spec = pl.BlockSpec((None, K, sh), lambda k, pref: (k, 0, 0))
    return _pcall(body, grid=grid, in_specs=[g_spec, u_spec], out_specs=u_spec, out_shape=SDS((4, *_unit_shape(i)), BF),
                  prefetch=True, name=name)(place, g, sib)


def _rs_chips(parts, idx, cid, name):
    n = len(idx)

    def body(*refs):
        ts, bufs, relays = refs[:n], refs[n:2 * n], refs[2 * n:3 * n]
        ssem, rsem = refs[3 * n:]
        x, y, c, _ = _my_place()
        xn, yn, dg = (1 - x, y), (x, 1 - y), (1 - x, 1 - y)
        to_x, to_y, me = (*xn, c), (*yn, c), (x, y, c)
        _handshake([to_x, to_y])

        def rcopy(i, k, src, dst, to):
            return pltpu.make_async_remote_copy(src_ref=src, dst_ref=dst, send_sem=ssem.at[i, k], recv_sem=rsem.at[i, k],
                                                device_id=to, device_id_type=MESH)

        def halves(i):
            rows = _unit_shape(idx[i])[0] // 2
            return [pl.ds(0, rows), pl.ds(rows, rows)]

        sends = []

        def send(cp):
            cp.start()
            sends.append(cp)

        for i in range(n):
            h0, h1 = halves(i)
            diag = ts[i].at[2 * dg[0] + dg[1]]
            send(rcopy(i, 0, ts[i].at[2 * xn[0] + xn[1]], bufs[i].at[0], to_x))
            send(rcopy(i, 1, ts[i].at[2 * yn[0] + yn[1]], bufs[i].at[1], to_y))
            send(rcopy(i, 2, diag.at[h0], relays[i].at[0], to_x))
            send(rcopy(i, 3, diag.at[h1], relays[i].at[1], to_y))
        for i in range(n):
            h0, h1 = halves(i)
            rcopy(i, 2, relays[i].at[0], relays[i].at[0], me).wait_recv()
            send(rcopy(i, 4, relays[i].at[0], bufs[i].at[2, h0], to_y))
            rcopy(i, 3, relays[i].at[1], relays[i].at[1], me).wait_recv()
            send(rcopy(i, 5, relays[i].at[1], bufs[i].at[2, h1], to_x))
        for i in range(n):
            h0, h1 = halves(i)
            rcopy(i, 0, bufs[i].at[0], bufs[i].at[0], me).wait_recv()
            rcopy(i, 1, bufs[i].at[1], bufs[i].at[1], me).wait_recv()
            rcopy(i, 4, bufs[i].at[2, h0], bufs[i].at[2, h0], me).wait_recv()
            rcopy(i, 5, bufs[i].at[2, h1], bufs[i].at[2, h1], me).wait_recv()
        for cp in sends:
            cp.wait_send()

    def relay_shape(i):
        rows, cols = _unit_shape(i)
        return (2, rows // 2, cols)

    out_type = [SDS((3, *_unit_shape(i)), BF) for i in idx] + [SDS(relay_shape(i), BF) for i in idx]
    sems = [pltpu.SemaphoreType.DMA((n, 6)), pltpu.SemaphoreType.DMA((n, 6))]
    return _sequencer(body, out_type, sems, cid, name)(*parts)[:n]


def _adamw(w, g, m, v):
    m = ADAM_B1 * m + (1.0 - ADAM_B1) * g
    v = ADAM_B2 * v + (1.0 - ADAM_B2) * jnp.square(g)
    m_hat = m / (1.0 - ADAM_B1 ** ADAM_STEP)
    v_hat = v / (1.0 - ADAM_B2 ** ADAM_STEP)
    delta = -ADAM_LR * (m_hat / (jnp.sqrt(v_hat) + ADAM_EPS) + ADAM_WD * w)
    return delta, m, v


def _adam_mat(parts, recvs, w, m, v, place, name):
    R, C = w.shape
    n = len(parts)
    tr = _pick(R, (256, 160, 176))

    def body(pref, *refs):
        t_refs, r_refs = refs[:n], refs[n:2 * n]
        w_ref, m_ref, v_ref, g_out, d_out, m_out, v_out = refs[2 * n:]
        cols = [t[...].astype(F32) + r[0].astype(F32) + r[1].astype(F32) + r[2].astype(F32) for t, r in zip(t_refs, r_refs)]
        g = jnp.concatenate(cols, axis=1) if n > 1 else cols[0]
        d, mn, vn = _adamw(w_ref[...], g, m_ref[...], v_ref[...])
        g_out[...] = g
        d_out[...] = d
        m_out[...] = mn
        v_out[...] = vn

    t_spec = pl.BlockSpec((None, tr, C // n), lambda i, pref: (pref[1], i, 0))
    r_spec = pl.BlockSpec((3, tr, C // n), lambda i, pref: (0, i, 0))
    s_spec = pl.BlockSpec((tr, C), lambda i, pref: (i, 0))
    return _pcall(body, grid=(R // tr,), in_specs=[t_spec] * n + [r_spec] * n + [s_spec] * 3, out_specs=[s_spec] * 4,
                  out_shape=[SDS((R, C), F32)] * 4, prefetch=True, name=name)(place, *parts, *recvs, w, m, v)


_SMALL = (("ln_mix", "ln_mix_w", D_MODEL), ("ln_cross", "ln_cross_w", D_MODEL), ("ln_mem", "ln_mem_w", D_MODEL),
          ("ln_ffn", "ln_ffn_w", D_MODEL), ("ln_final", "ln_final_w", D_MODEL), ("hg_lb", "hg_lower_bounds", 2 * HG_WIDTH),
          ("hg_norm", "hg_norm_w", HEAD_DIM))
_PACK = sum(n for _, _, n in _SMALL) + LANE


def _small_sync(grow, wrow, mrow, vrow):
    def body(g_ref, w_ref, m_ref, v_ref, o_ref, gath, ssem, rsem):
        x, y, c, _ = _my_place()
        me = 4 * x + 2 * y + c
        gath[pl.ds(me, 1), :] = g_ref[...]
        cps = []
        for dlt in range(1, N_DEV):
            peer = (me + dlt) % N_DEV
            cp = pltpu.make_async_remote_copy(
                src_ref=g_ref, dst_ref=gath.at[pl.ds(me, 1), :], send_sem=ssem.at[peer], recv_sem=rsem.at[me],
                device_id=(peer // 4, (peer // 2) % 2, peer % 2), device_id_type=MESH)
            cp.start()
            cps.append(cp)
        for dlt in range(1, N_DEV):
            src = (me + dlt) % N_DEV
            pltpu.make_async_remote_copy(
                src_ref=g_ref, dst_ref=gath.at[pl.ds(src, 1), :], send_sem=ssem.at[src], recv_sem=rsem.at[src],
                device_id=(x, y, c), device_id_type=MESH).wait_recv()
        for cp in cps:
            cp.wait_send()
        g = gath[0:1, :]
        for j in range(1, N_DEV):
            g = g + gath[j:j + 1, :]
        d, mn, vn = _adamw(w_ref[...], g, m_ref[...], v_ref[...])
        o_ref[0:1, :] = g
        o_ref[1:2, :] = d
        o_ref[2:3, :] = mn
        o_ref[3:4, :] = vn

    vm = pl.BlockSpec(memory_space=pltpu.VMEM)
    return _pcall(
        body, in_specs=[vm] * 4, out_specs=vm, out_shape=SDS((4, _PACK), F32),
        scratch_shapes=[pltpu.VMEM((N_DEV, _PACK), F32), pltpu.SemaphoreType.DMA((N_DEV,)), pltpu.SemaphoreType.DMA((N_DEV,))],
        name="small_sync")(grow, wrow, mrow, vrow)


def _pack_small(d, loss_row=None):
    parts = [d[k].reshape(1, n).astype(F32) for k, _, n in _SMALL]
    parts.append(loss_row if loss_row is not None else jnp.ones((1, LANE), F32))
    return jnp.concatenate(parts, axis=1)


def kernel(x, mem, ln_mix_w, w_in, hg_norm_w, hg_lower_bounds, w_branch_a, w_branch_b, w_out, ln_cross_w, ln_mem_w, wq_cross, wkv_cross, wo_cross, ln_ffn_w, w1, w3, w2, ln_final_w, loss_target, m_ln_mix_w, m_w_in, m_hg_norm_w, m_hg_lower_bounds, m_w_branch_a, m_w_branch_b, m_w_out, m_ln_cross_w, m_ln_mem_w, m_wq_cross, m_wkv_cross, m_wo_cross, m_ln_ffn_w, m_w1, m_w3, m_w2, m_ln_final_w, v_ln_mix_w, v_w_in, v_hg_norm_w, v_hg_lower_bounds, v_w_branch_a, v_w_branch_b, v_w_out, v_ln_cross_w, v_ln_mem_w, v_wq_cross, v_wkv_cross, v_wo_cross, v_ln_ffn_w, v_w1, v_w3, v_w2, v_ln_final_w):
    given = dict(locals())
    place = jnp.stack([lax.axis_index("c"), 2 * lax.axis_index("x") + lax.axis_index("y")]).astype(jnp.int32)

    _ORDER[0] = None
    index = _INDEX

    def local_view(a, i):
        return a[0].T if _MATS[i][6] else a[0]

    shard = {key: local_view(given[ref], i) for i, (key, ref, *_) in enumerate(_MATS)}

    p = {}
    col_half = {"w_in_a": 0, "w_in_b": 1}
    waves = (("in_a", ["w_in_a"], 1), ("in_b", ["w_in_b"], 12), ("mid", ["w_ba", "w_bb", "w_out", "wq", "wkv", "wo"], 2),
             ("w1", ["w1"], 11), ("w3", ["w3"], 15), ("w2", ["w2"], 16))
    for wave, keys, cid in waves:
        units = [_cast_unit(shard[k], "cast_" + k, col_half.get(k)) for k in keys]
        p.update(zip(keys, _all_gather(units, [index[k] for k in keys], cid, "all_gather_" + wave)))
    small_w = {"ln_mix": ln_mix_w, "ln_cross": ln_cross_w, "ln_mem": ln_mem_w, "ln_ffn": ln_ffn_w,
               "ln_final": ln_final_w.reshape(1, D_MODEL), "hg_lb": hg_lower_bounds, "hg_norm": hg_norm_w}
    p.update(small_w)

    groups = {"ffn": ["w2", "w1", "w3"], "cross": ["wo", "wq", "wkv"], "mix": ["w_out", "w_ba", "w_bb"]}
    cids = {"ffn": (3, 4), "cross": (5, 6), "mix": (7, 8)}
    for q in range(IN_PARTS):
        groups[f"in_q{q}"] = [f"w_in_q{q}"]
        cids[f"in_q{q}"] = ((9, 10), (13, 14), (17, 18), (19, 20))[q]
    sib, parts, recv, out = {}, {}, {}, {}

    grads = {}

    def chip_stage(name):
        keys = groups[name]
        parts[name] = [_chip_sum(grads[k], s, index[k], place, "chip_sum_" + k) for k, s in zip(keys, sib[name])]
        recv[name] = list(_rs_chips(parts[name], [index[k] for k in keys], cids[name][1], "rs_chips_" + name))

    def adam(ref, i, part_list, recv_list):
        res = _adam_mat(part_list, recv_list, shard[_MATS[i][0]], local_view(given["m_" + ref], i), local_view(given["v_" + ref], i),
                        place, "adam_" + ref)
        out[ref] = [(a.T if _MATS[i][6] else a)[None] for a in res]

    def adam_stage(name):
        for k, t, r in zip(groups[name], parts[name], recv[name]):
            adam(_MATS[index[k]][1], index[k], [t], [r])

    def ready(name, gw):
        if name == "mixers_done":
            adam_stage("ffn")
            chip_stage("mix")
            adam_stage("cross")
            return
        if name == "dh_half":
            return
        keys = groups[name]
        grads.update({k: gw[k] for k in keys})
        if name.startswith("in_q"):
            if name == "in_q1":
                adam_stage("mix")
            parts[name] = [_sibling_sum_tc(gw[k], index[k], cids[name][0], "sibling_sum_" + k) for k in keys]
            recv[name] = list(_rs_chips(parts[name], [index[k] for k in keys], cids[name][1], "rs_chips_" + name))
            return
        sib[name] = _rs_sibling([gw[k] for k in keys], [index[k] for k in keys], cids[name][0], "rs_sibling_" + name)
        if name == "cross":
            chip_stage("ffn")
        if name == "mix":
            chip_stage("cross")

    loss_row, grad_x, gs = _local_step(x[0], mem[0], loss_target[0], p, ready)

    sm = {k: given["m_" + ref] for k, ref, _ in _SMALL}
    sv = {k: given["v_" + ref] for k, ref, _ in _SMALL}
    synced = _small_sync(_pack_small(gs, loss_row), _pack_small(small_w), _pack_small(sm), _pack_small(sv))
    in_groups = [f"in_q{q}" for q in range(IN_PARTS)]
    adam("w_in", index["w_in_a"], sum((parts[g] for g in in_groups), []), sum((recv[g] for g in in_groups), []))
    off = 0
    for k, ref, n in _SMALL:
        shape = given[ref].shape
        out[ref] = [synced[r, off:off + n].reshape(shape) for r in range(4)]
        off += n
    loss = synced[0, off]

    order = ["ln_mix_w", "w_in", "hg_norm_w", "hg_lower_bounds", "w_branch_a", "w_branch_b", "w_out", "ln_cross_w", "ln_mem_w",
             "wq_cross", "wkv_cross", "wo_cross", "ln_ffn_w", "w1", "w3", "w2", "ln_final_w"]
    return (loss, grad_x[None], *[out[n][0] for n in order], *[out[n][1] for n in order],
            *[out[n][2] for n in order], *[out[n][3] for n in order])
```

```python
import functools
import math

import jax
import jax.numpy as jnp
from jax import lax
from jax.experimental import pallas as pl
from jax.experimental.pallas import tpu as pltpu
from jax.experimental.pallas import tpu_sc as plsc

F32 = jnp.float32
BF = jnp.bfloat16
SDS = jax.ShapeDtypeStruct

D_MODEL = 2048
SEQ = 2048
HEAD_DIM = 128
MEM_LEN = 256
ATT_GROUPS = ((128, 1), (512, 4), (2048, 16))
ATT_HEADS = 4
ATT_WIDTH = 1536
ATT_OUT = 512
HG_HEADS = 8
HG_WIDTH = 1024
HG_CHUNK = 64
IN_WIDTH = 12800
CROSS_HEADS = 4
CROSS_WIDTH = 512
D_FF = 5632
RMS_EPS = 1e-6
ADAM_LR = 0.001
ADAM_B1 = 0.9
ADAM_B2 = 0.999
ADAM_EPS = 1e-08
ADAM_WD = 0.01
ADAM_STEP = 10
N_DEV = 8

LANE = 128
QA0, KA0, VA0 = 0, 12, 24
QH0, FH0, IH0, GH0 = 36, 44, 52, 60
GA0, GB0 = 68, 84

VMEM_BIG = 56 * 1024 * 1024
MESH = pl.DeviceIdType.MESH


def _cp(vmem=None, **kw):
    if vmem is not None:
        kw["vmem_limit_bytes"] = vmem
    return pltpu.CompilerParams(**kw)


def _pick(n, cands):
    for c in cands:
        if n % c == 0:
            return c
    raise ValueError(f"no tile for {n}")


_ANY = pl.BlockSpec(memory_space=pl.ANY)
_ORDER = [None]


def _pcall(body, *, name, in_specs, out_specs, out_shape, grid=(), scratch_shapes=(), compiler_params=None, prefetch=False,
           aliases=None):
    def run(*args):
        dep = _ORDER[0]
        specs, operands, kernel_fn = list(in_specs), list(args), body
        if dep is not None and not any(dep is a for a in operands):
            at = len(operands)
            specs.append(_ANY)
            operands.append(dep)

            def kernel_fn(*refs):
                return body(*refs[:at], *refs[at + 1:])
        if prefetch:
            call = pl.pallas_call(
                kernel_fn, out_shape=out_shape, name=name, compiler_params=compiler_params,
                grid_spec=pltpu.PrefetchScalarGridSpec(num_scalar_prefetch=1, grid=grid, in_specs=specs, out_specs=out_specs,
                                                       scratch_shapes=scratch_shapes))
        else:
            call = pl.pallas_call(kernel_fn, grid=grid, in_specs=specs, out_specs=out_specs, out_shape=out_shape,
                                  scratch_shapes=scratch_shapes, compiler_params=compiler_params, name=name,
                                  input_output_aliases=aliases or {})
        out = call(*operands)
        _ORDER[0] = out[0] if isinstance(out, (list, tuple)) else out
        return out
    return run


def _mm(a, b, form, out_dtype, name, res=None, tiles=None, a_part=(0, 1), b_part=(0, 1)):
    a_shape = (a.shape[0], a.shape[1] // a_part[1])
    b_shape = (b.shape[0], b.shape[1] // b_part[1])
    if form == "nn":
        (M, K), (K2, N) = a_shape, b_shape
    elif form == "nt":
        (M, K), (N, K2) = a_shape, b_shape
    else:
        (K, M), (K2, N) = a_shape, b_shape
    assert K == K2, (a.shape, b.shape, form)
    if tiles is None:
        tm = _pick(M, (1024, 1280, 1408, 512, 256))
        tn = _pick(N, (1280, 1024, 1408, 512, 256))
        tk = K if K <= 2048 else _pick(K, (D_FF, 2560))
        if tk == D_FF:
            tm = _pick(M, (512, 256))
    else:
        tm, tn, tk = tiles
    nk = K // tk
    ah, bh = a_part[0], b_part[0]
    if form == "nn":
        a_spec = pl.BlockSpec((tm, tk), lambda i, j, k: (i, k + ah * nk))
        b_spec = pl.BlockSpec((tk, tn), lambda i, j, k: (k, j + bh * (N // tn)))
        dims = ((1,), (0,))
    elif form == "nt":
        a_spec = pl.BlockSpec((tm, tk), lambda i, j, k: (i, k + ah * nk))
        b_spec = pl.BlockSpec((tn, tk), lambda i, j, k: (j, k + bh * nk))
        dims = ((1,), (1,))
    else:
        a_spec = pl.BlockSpec((tk, tm), lambda i, j, k: (k, i + ah * (M // tm)))
        b_spec = pl.BlockSpec((tk, tn), lambda i, j, k: (k, j + bh * (N // tn)))
        dims = ((0,), (0,))
    o_spec = pl.BlockSpec((tm, tn), lambda i, j, k: (i, j))
    in_specs = [a_spec, b_spec]
    args = [a, b]
    if res is not None:
        in_specs.append(o_spec)
        args.append(res)

    def body(*refs):
        a_ref, b_ref = refs[0], refs[1]
        r_ref = refs[2] if res is not None else None
        o_ref = refs[3] if res is not None else refs[2]
        acc = refs[-1] if nk > 1 else None
        k = pl.program_id(2)
        d = lax.dot_general(a_ref[...].astype(BF), b_ref[...].astype(BF), (dims, ((), ())), preferred_element_type=F32)

        def finish(r):
            if res is not None:
                r = r + r_ref[...].astype(F32)
            o_ref[...] = r.astype(o_ref.dtype)

        if nk == 1:
            finish(d)
        else:
            @pl.when(k == 0)
            def _():
                acc[...] = d

            @pl.when((k > 0) & (k < nk - 1))
            def _():
                acc[...] += d

            @pl.when(k == nk - 1)
            def _():
                finish(acc[...] + d)

    return _pcall(
        body, grid=(M // tm, N // tn, nk), in_specs=in_specs, out_specs=o_spec,
        out_shape=SDS((M, N), out_dtype), scratch_shapes=[pltpu.VMEM((tm, tn), F32)] if nk > 1 else [],
        compiler_params=_cp(VMEM_BIG, dimension_semantics=("parallel", "parallel", "arbitrary")),
        name=name)(*args)


_MXU_COLS = 256


def _mm_fused(a, b, form, name, extras, out_dtypes, epi, tn=None, scatter=None):
    M, K = a.shape
    N = b.shape[1] if form == "nn" else b.shape[0]
    assert K <= 2048 and K == (b.shape[0] if form == "nn" else b.shape[1])
    tm = _pick(M, (1024, 512, 256))
    tn = tn or _pick(N, (1280, 1024, 1408, 512, 256))
    ne, no = len(extras), len(out_dtypes)
    bases = scatter[1] if scatter else []
    dims = ((1,), (0,)) if form == "nn" else ((1,), (1,))

    def body(*refs):
        a_ref, b_ref, ex, outs = refs[0], refs[1], refs[2:2 + ne], refs[2 + ne:2 + ne + no]
        av = a_ref[...].astype(BF)
        if scatter:
            wide_ref, stage, sem = refs[2 + ne + no:]
        for c0 in range(0, tn, _MXU_COLS):
            cols = slice(c0, min(c0 + _MXU_COLS, tn))
            bc = b_ref[:, cols] if form == "nn" else b_ref[cols, :]
            d = lax.dot_general(av, bc.astype(BF), (dims, ((), ())), preferred_element_type=F32)
            res = epi(d, *[e[:, cols] for e in ex])
            for o_ref, r in zip(outs, res[:no]):
                o_ref[:, cols] = r.astype(o_ref.dtype)
            for s, r in enumerate(res[no:]):
                stage[s, :, cols] = r.astype(BF)
        if scatter:
            rows = pl.ds(pl.multiple_of(pl.program_id(0) * tm, tm), tm)
            cps = [pltpu.make_async_copy(stage.at[s], wide_ref.at[rows, pl.ds(pl.multiple_of(base + pl.program_id(1) * tn, LANE), tn)],
                                         sem.at[s]) for s, base in enumerate(bases)]
            for cp in cps:
                cp.start()
            for cp in cps:
                cp.wait()

    a_spec = pl.BlockSpec((tm, K), lambda i, j: (i, 0))
    b_spec = pl.BlockSpec((K, tn), lambda i, j: (0, j)) if form == "nn" else pl.BlockSpec((tn, K), lambda i, j: (j, 0))
    o_spec = pl.BlockSpec((tm, tn), lambda i, j: (i, j))
    ex_specs, ex_args = [], []
    for e in extras:
        arr, off = e if isinstance(e, tuple) else (e, 0)
        ex_specs.append(pl.BlockSpec((tm, tn), lambda i, j, off=off: (i, off + j)))
        ex_args.append(arr)
    out_specs, out_shape, scratch = [o_spec] * no, [SDS((M, N), dt) for dt in out_dtypes], []
    if scatter:
        out_specs, out_shape = out_specs + [_ANY], out_shape + [scatter[0]]
        scratch = [pltpu.VMEM((len(bases), tm, tn), BF), pltpu.SemaphoreType.DMA((len(bases),))]
    return _pcall(
        body, grid=(M // tm, N // tn), in_specs=[a_spec, b_spec] + ex_specs, out_specs=out_specs, out_shape=out_shape,
        scratch_shapes=scratch, compiler_params=_cp(VMEM_BIG, dimension_semantics=("parallel", "parallel")),
        name=name)(a, b, *ex_args)


def _rms(x, w):
    return x * lax.rsqrt(jnp.mean(x * x, axis=-1, keepdims=True) + RMS_EPS) * w


def _rms_fwd(x, w, name):
    R, D = x.shape
    tr = 256

    def body(x_ref, w_ref, o_ref):
        o_ref[...] = _rms(x_ref[...], w_ref[...]).astype(o_ref.dtype)

    return _pcall(
        body, grid=(R // tr,),
        in_specs=[pl.BlockSpec((tr, D), lambda i: (i, 0)), pl.BlockSpec((1, D), lambda i: (0, 0))],
        out_specs=pl.BlockSpec((tr, D), lambda i: (i, 0)), out_shape=SDS((R, D), BF), name=name)(x, w)


def _rms_bwd(x, w, dh, dres, name):
    R, D = x.shape
    tr = 256
    want_dx = dres is not None
    dhs = dh if isinstance(dh, (tuple, list)) else (dh,)
    nd = len(dhs)

    def body(*refs):
        x_ref, w_ref, dh_refs = refs[0], refs[1], refs[2:2 + nd]
        if want_dx:
            dr_ref, dx_ref, dxb_ref, dw_ref = refs[2 + nd:]
        else:
            dw_ref = refs[2 + nd]

        @pl.when(pl.program_id(0) == 0)
        def _():
            dw_ref[...] = jnp.zeros_like(dw_ref)

        dhv = jnp.concatenate([r[...].astype(F32) for r in dh_refs], axis=1) if nd > 1 else dh_refs[0][...].astype(F32)
        _, vjp = jax.vjp(_rms, x_ref[...], w_ref[...])
        dx, dw = vjp(dhv)
        dw_ref[...] += dw
        if want_dx:
            dx = dx + dr_ref[...]
            dx_ref[...] = dx
            dxb_ref[...] = dx.astype(BF)

    row = pl.BlockSpec((tr, D), lambda i: (i, 0))
    part = pl.BlockSpec((tr, D // nd), lambda i: (i, 0))
    vec = pl.BlockSpec((1, D), lambda i: (0, 0))
    if want_dx:
        return _pcall(body, grid=(R // tr,), in_specs=[row, vec] + [part] * nd + [row], out_specs=[row, row, vec],
                      out_shape=[SDS((R, D), F32), SDS((R, D), BF), SDS((1, D), F32)], name=name)(x, w, *dhs, dres)
    return _pcall(body, grid=(R // tr,), in_specs=[row, vec] + [part] * nd, out_specs=vec,
                  out_shape=SDS((1, D), F32), name=name)(x, w, *dhs)


def _final_loss(x3, w, tgt):
    R, D = x3.shape
    tr = 256

    def loss_fn(xv, wv, tv):
        err = _rms(xv, wv) - tv
        return 0.5 * jnp.sum(jnp.mean(err * err, axis=-1))

    def body(x_ref, w_ref, t_ref, loss_ref, dx_ref, dxb_ref, dw_ref):
        @pl.when(pl.program_id(0) == 0)
        def _():
            dw_ref[...] = jnp.zeros_like(dw_ref)
            loss_ref[...] = jnp.zeros_like(loss_ref)

        tv = t_ref[...]
        val, vjp = jax.vjp(lambda a, b: loss_fn(a, b, tv), x_ref[...], w_ref[...])
        dx, dw = vjp(jnp.ones((), F32))
        dx_ref[...] = dx
        dxb_ref[...] = dx.astype(BF)
        dw_ref[...] += dw
        loss_ref[...] += jnp.full(loss_ref.shape, val, F32)

    row = pl.BlockSpec((tr, D), lambda i: (i, 0))
    vec = pl.BlockSpec((1, D), lambda i: (0, 0))
    one = pl.BlockSpec((1, LANE), lambda i: (0, 0))
    return _pcall(body, grid=(R // tr,), in_specs=[row, vec, row], out_specs=[one, row, row, vec],
                          out_shape=[SDS((1, LANE), F32), SDS((R, D), F32), SDS((R, D), BF), SDS((1, D), F32)],
                          name="final_loss")(x3, w, tgt)


NEG = -1e30


def _att_blocks(S, d):
    L = S // d
    nb = L // LANE
    return nb, (2 if nb > 1 else 1)


def _att_rows(S, d, idx):
    nb, nkb = _att_blocks(S, d)
    r = idx // nb
    n = idx % nb
    kb = jnp.maximum(n - 1, 0)
    if d == 1:
        qrows = pl.ds(pl.multiple_of(n * LANE, LANE), LANE)
        krows = pl.ds(pl.multiple_of(kb * LANE, LANE), LANE * nkb)
    else:
        qrows = pl.ds(r + n * (LANE * d), LANE, stride=d)
        krows = pl.ds(r + kb * (LANE * d), LANE * nkb, stride=d)
    qpos = n * LANE + lax.broadcasted_iota(jnp.int32, (LANE, LANE * nkb), 0)
    kpos = kb * LANE + lax.broadcasted_iota(jnp.int32, (LANE, LANE * nkb), 1)
    return qrows, krows, qpos - kpos


def _slab(col):
    return pl.BlockSpec((SEQ, LANE), lambda h, col=col: (0, col + h))


def _widen(slabs, wide):
    for t, ref in enumerate(slabs):
        wide[t] = ref[...].astype(F32)
    return tuple(tuple(wide.at[3 * kind + g] for g in range(3)) for kind in range(3))


def _attn_fwd(proj):
    S = proj.shape[0]
    scale = HEAD_DIM ** -0.5

    def body(q0, q1, q2, k0, k1, k2, v0, v1, v2, o_ref, lse_ref, og, lg, wide):
        qs, ks, vs = _widen((q0, q1, q2, k0, k1, k2, v0, v1, v2), wide)
        for g, (window, d) in enumerate(ATT_GROUPS):
            def blk(idx, carry, g=g, window=window, d=d):
                qrows, krows, dist = _att_rows(S, d, idx)
                q = qs[g][qrows, :].astype(BF)
                k = ks[g][krows, :].astype(BF)
                v = vs[g][krows, :].astype(BF)
                s = lax.dot_general(q, k, (((1,), (1,)), ((), ())), preferred_element_type=F32) * scale
                s = jnp.where((dist >= 0) & (dist <= window // d), s, NEG)
                m = jnp.max(s, axis=-1, keepdims=True)
                p = jnp.exp(s - m)
                l = jnp.sum(p, axis=-1, keepdims=True)
                o = jnp.dot((p / l).astype(BF), v, preferred_element_type=F32)
                og[g, qrows, :] = o
                lg[g, qrows, :] = jnp.broadcast_to(m + jnp.log(l), (LANE, LANE))
                return carry
            lax.fori_loop(0, S // LANE, blk, 0, unroll=8)

        def merge(t, carry):
            rows = pl.ds(pl.multiple_of(t * 256, 256), 256)
            l0, l1, l2 = lg[0, rows, :], lg[1, rows, :], lg[2, rows, :]
            m = jnp.maximum(jnp.maximum(l0, l1), l2)
            w0, w1, w2 = jnp.exp(l0 - m), jnp.exp(l1 - m), jnp.exp(l2 - m)
            den = w0 + w1 + w2
            o_ref[rows, :] = (w0 * og[0, rows, :] + w1 * og[1, rows, :] + w2 * og[2, rows, :]) / den
            lse_ref[rows, :] = m + jnp.log(den)
            return carry
        lax.fori_loop(0, S // 256, merge, 0)

    ins = [_slab(QA0 + 4 * g) for g in range(3)] + [_slab(KA0 + 4 * g) for g in range(3)] + [_slab(VA0 + 4 * g) for g in range(3)]
    out = pl.BlockSpec((SEQ, LANE), lambda h: (0, h))
    return _pcall(
        body, grid=(ATT_HEADS,), in_specs=ins, out_specs=[out, out],
        out_shape=[SDS((S, ATT_OUT), F32), SDS((S, ATT_OUT), F32)],
        scratch_shapes=[pltpu.VMEM((3, S, LANE), F32), pltpu.VMEM((3, S, LANE), F32), pltpu.VMEM((9, S, LANE), F32)],
        compiler_params=_cp(VMEM_BIG), name="attn_fwd")(*([proj] * 9))


def _attn_bwd(proj, o_att, lse, do_att, dproj):
    S = proj.shape[0]
    scale = HEAD_DIM ** -0.5

    def body(q0, q1, q2, k0, k1, k2, v0, v1, v2, o_ref, lse_ref, do_ref, dp_in, dp_ref, dqa, dka, dva, delta, stage, sem, wide):
        qs, ks, vs = _widen((q0, q1, q2, k0, k1, k2, v0, v1, v2), wide)
        head = pl.program_id(0)

        def slab_copies(g):
            return [pltpu.make_async_copy(stage.at[t], dp_ref.at[:, pl.ds(pl.multiple_of((base + 4 * g + head) * LANE, LANE), LANE)],
                                          sem.at[t]) for t, base in enumerate((QA0, KA0, VA0))]

        delta[...] = jnp.broadcast_to(jnp.sum(do_ref[...] * o_ref[...], axis=-1, keepdims=True), delta.shape)
        for g, (window, d) in enumerate(ATT_GROUPS):
            dka[...] = jnp.zeros_like(dka)
            dva[...] = jnp.zeros_like(dva)

            def blk(idx, carry, g=g, window=window, d=d):
                qrows, krows, dist = _att_rows(S, d, idx)
                q = qs[g][qrows, :].astype(BF)
                k = ks[g][krows, :].astype(BF)
                v = vs[g][krows, :].astype(BF)
                do = do_ref[qrows, :]
                s = lax.dot_general(q, k, (((1,), (1,)), ((), ())), preferred_element_type=F32) * scale
                p = jnp.exp(s - lse_ref[qrows, :][:, 0:1])
                p = jnp.where((dist >= 0) & (dist <= window // d), p, 0.0)
                dob = do.astype(BF)
                dva[krows, :] += lax.dot_general(p.astype(BF), dob, (((0,), (0,)), ((), ())), preferred_element_type=F32)
                dp = lax.dot_general(dob, v, (((1,), (1,)), ((), ())), preferred_element_type=F32)
                ds = (p * (dp - delta[qrows, :][:, 0:1]) * scale).astype(BF)
                dqa[qrows, :] = jnp.dot(ds, k, preferred_element_type=F32)
                dka[krows, :] += lax.dot_general(ds, q, (((0,), (0,)), ((), ())), preferred_element_type=F32)
                return carry
            lax.fori_loop(0, S // LANE, blk, 0, unroll=8)
            if g > 0:
                for cp in slab_copies(g - 1):
                    cp.wait()
            stage[0] = dqa[...].astype(BF)
            stage[1] = dka[...].astype(BF)
            stage[2] = dva[...].astype(BF)
            for cp in slab_copies(g):
                cp.start()
        for cp in slab_copies(len(ATT_GROUPS) - 1):
            cp.wait()

    cols = [QA0 + 4 * g for g in range(3)] + [KA0 + 4 * g for g in range(3)] + [VA0 + 4 * g for g in range(3)]
    ins = [_slab(c) for c in cols]
    one = pl.BlockSpec((SEQ, LANE), lambda h: (0, h))
    return _pcall(
        body, grid=(ATT_HEADS,), in_specs=ins + [one, one, one, _ANY], out_specs=_ANY,
        out_shape=SDS((S, IN_WIDTH), BF), aliases={12: 0},
        scratch_shapes=[pltpu.VMEM((S, LANE), F32)] * 4 + [pltpu.VMEM((3, S, LANE), BF), pltpu.SemaphoreType.DMA((3,)),
                                                           pltpu.VMEM((9, S, LANE), F32)],
        compiler_params=_cp(VMEM_BIG), name="attn_bwd")(*([proj] * 9), o_att, lse, do_att, dproj)


def _bdot(a, b, dims):
    return lax.dot_general(a.astype(BF), b.astype(BF), (dims, ((), ())), preferred_element_type=F32)


def _hg_chunk(qh, fh, ih, gh, lbraw, nw, st):
    C = HG_CHUNK
    mx = jnp.maximum(lbraw[0:1], lbraw[1:2])
    e0 = jnp.exp(lbraw[0:1] - mx)
    e1 = jnp.exp(lbraw[1:2] - mx)
    lb = e0 / (e0 + e1)
    f = lb + (1.0 - lb) * jax.nn.sigmoid(fh)
    logf = jnp.log(f)
    k = 1.0 - f
    q = qh * jax.nn.sigmoid(qh)
    causal = lax.broadcasted_iota(jnp.int32, (C, C), 0) >= lax.broadcasted_iota(jnp.int32, (C, C), 1)
    b = lax.dot_general(causal.astype(F32), logf, (((1,), (0,)), ((), ())), precision=lax.Precision.HIGHEST,
                        preferred_element_type=F32)
    b_last = b[C - 1:C]
    st_out = st * jnp.exp(b_last) + _bdot(ih, k * jnp.exp(b_last - b), ((0,), (0,)))
    inter = _bdot(q * jnp.exp(b), st, ((1,), (1,)))
    b_ref = b[C // 2:C // 2 + 1]
    a = _bdot(q * jnp.exp(b - b_ref), k * jnp.exp(b_ref - b), ((1,), (1,)))
    a = jnp.where(causal, a, 0.0)
    o = inter + _bdot(a, ih, ((1,), (0,)))
    o = o * lax.rsqrt(jnp.mean(o * o, axis=-1, keepdims=True) + RMS_EPS) * nw
    return o * (gh * jax.nn.sigmoid(gh)), st_out


def _to_heads(x):
    return jnp.concatenate([x[None, :, h * LANE:(h + 1) * LANE] for h in range(HG_HEADS)], axis=0)


def _from_heads(x):
    return jnp.concatenate([x[h] for h in range(HG_HEADS)], axis=1)


def _hdot(a, b, ca, cb):
    return lax.dot_general(a.astype(BF), b.astype(BF), (((ca,), (cb,)), ((0,), (0,))), preferred_element_type=F32)


def _hg_chunk_all(qh, fh, ih, gh, lbraw, nw, st):
    C = HG_CHUNK
    mx = jnp.maximum(lbraw[0:1], lbraw[1:2])
    e0 = jnp.exp(lbraw[0:1] - mx)
    e1 = jnp.exp(lbraw[1:2] - mx)
    lb = e0 / (e0 + e1)
    f = lb + (1.0 - lb) * jax.nn.sigmoid(fh)
    logf = jnp.log(f)
    k = 1.0 - f
    q = qh * jax.nn.sigmoid(qh)
    causal = lax.broadcasted_iota(jnp.int32, (C, C), 0) >= lax.broadcasted_iota(jnp.int32, (C, C), 1)
    b = lax.dot_general(causal.astype(F32), logf, (((1,), (0,)), ((), ())), precision=lax.Precision.HIGHEST,
                        preferred_element_type=F32)
    b_last = b[C - 1:C]
    b_ref = b[C // 2:C // 2 + 1]
    v3 = _to_heads(ih)
    st_out = st * _to_heads(jnp.exp(b_last)) + _hdot(v3, _to_heads(k * jnp.exp(b_last - b)), 1, 1)
    inter = _hdot(_to_heads(q * jnp.exp(b)), st, 2, 2)
    a = _hdot(_to_heads(q * jnp.exp(b - b_ref)), _to_heads(k * jnp.exp(b_ref - b)), 2, 2)
    a = jnp.where(causal[None], a, 0.0)
    o = inter + _hdot(a, v3, 2, 1)
    o = o * lax.rsqrt(jnp.mean(o * o, axis=-1, keepdims=True) + RMS_EPS) * nw
    return _from_heads(o) * (gh * jax.nn.sigmoid(gh)), st_out


def _head_cols(h):
    return slice(h * LANE, (h + 1) * LANE)


def _hg_in_specs(chunk_of):
    half = 4 * LANE
    specs = []
    for base in (QH0, FH0, IH0, GH0):
        for part in range(2):
            specs.append(pl.BlockSpec((HG_CHUNK, half), lambda n, col=base // 4 + part: (chunk_of(n), col)))
    return specs


def _hg_sections(sec):
    return [jnp.concatenate([sec[2 * s][...], sec[2 * s + 1][...]], axis=1).astype(F32) for s in range(4)]


def _hg_fwd(proj, lb, nw):
    S = proj.shape[0]
    C = HG_CHUNK
    NC = S // C

    def body(*refs):
        sec = refs[:8]
        lb_ref, nw_ref, o_ref, st_ref, st_scr = refs[8:]

        @pl.when(pl.program_id(0) == 0)
        def _():
            st_scr[...] = jnp.zeros_like(st_scr)

        st = st_scr[...]
        st_ref[:, 0] = st
        o, st_out = _hg_chunk_all(*_hg_sections(sec), lb_ref[...], nw_ref[...], st)
        o_ref[...] = o.astype(o_ref.dtype)
        st_scr[...] = st_out

    return _pcall(
        body, grid=(NC,),
        in_specs=_hg_in_specs(lambda n: n) + [pl.BlockSpec((2, HG_WIDTH), lambda n: (0, 0)), pl.BlockSpec((1, LANE), lambda n: (0, 0))],
        out_specs=[pl.BlockSpec((C, HG_WIDTH), lambda n: (n, 0)), pl.BlockSpec((HG_HEADS, 1, LANE, LANE), lambda n: (0, n, 0, 0))],
        out_shape=[SDS((S, HG_WIDTH), BF), SDS((HG_HEADS, NC, LANE, LANE), F32)],
        scratch_shapes=[pltpu.VMEM((HG_HEADS, LANE, LANE), F32)], name="hg_fwd")(*([proj] * 8), lb, nw)


def _hg_bwd(proj, lb, nw, states, do_hg, dproj):
    S = proj.shape[0]
    C = HG_CHUNK
    NC = S // C

    def body(*refs):
        sec = refs[:8]
        lb_ref, nw_ref, st_ref, do_ref, dp_in, dp_ref, dlb_ref, dnw_ref, dst_scr, stage, sem = refs[8:]
        n = pl.program_id(0)
        slot = n % 2

        def chunk_copies(s):
            rows = pl.ds(pl.multiple_of((NC - 1 - n) * C, C), C)
            return [pltpu.make_async_copy(stage.at[s, t], dp_ref.at[rows, pl.ds((QH0 + 8 * t) * LANE, HG_WIDTH)], sem.at[s, t])
                    for t in range(4)]

        @pl.when(n == 0)
        def _():
            dst_scr[...] = jnp.zeros_like(dst_scr)
            dlb_ref[...] = jnp.zeros_like(dlb_ref)
            dnw_ref[...] = jnp.zeros_like(dnw_ref)

        _, vjp = jax.vjp(_hg_chunk_all, *_hg_sections(sec), lb_ref[...], nw_ref[...], st_ref[:, 0])
        dq, df, di, dg, dlb, dnw, dst = vjp((do_ref[...].astype(F32), dst_scr[...]))
        dlb_ref[...] += dlb
        dnw_ref[...] += dnw
        dst_scr[...] = dst

        @pl.when(n >= 2)
        def _():
            for cp in chunk_copies(slot):
                cp.wait()

        for t, val in enumerate((dq, df, di, dg)):
            stage[slot, t] = val.astype(BF)
        for cp in chunk_copies(slot):
            cp.start()

        @pl.when(n == NC - 1)
        def _():
            for cp in chunk_copies(1 - slot) + chunk_copies(slot):
                cp.wait()

    rev = lambda n: NC - 1 - n
    ospec = pl.BlockSpec((C, HG_WIDTH), lambda n: (rev(n), 0))
    lbs = pl.BlockSpec((2, HG_WIDTH), lambda n: (0, 0))
    nws = pl.BlockSpec((1, LANE), lambda n: (0, 0))
    outs = _pcall(
        body, grid=(NC,),
        in_specs=_hg_in_specs(rev) + [lbs, nws, pl.BlockSpec((HG_HEADS, 1, LANE, LANE), lambda n: (0, rev(n), 0, 0)), ospec, _ANY],
        out_specs=[_ANY, lbs, nws],
        out_shape=[SDS((S, IN_WIDTH), BF), SDS((2, HG_WIDTH), F32), SDS((1, LANE), F32)], aliases={12: 0},
        scratch_shapes=[pltpu.VMEM((HG_HEADS, LANE, LANE), F32), pltpu.VMEM((2, 4, C, HG_WIDTH), BF), pltpu.SemaphoreType.DMA((2, 4))],
        name="hg_bwd")(*([proj] * 8), lb, nw, states, do_hg, dproj)
    return outs


def _merge(ga, gb, ba, bb):
    return jax.nn.sigmoid(ga) * ba + jax.nn.sigmoid(gb) * bb


_MT = 512


def _swiglu(a, b):
    return a * jax.nn.sigmoid(a) * b


def _cross(q, k, v):
    s = _bdot(q, k, ((1,), (1,))) * (HEAD_DIM ** -0.5)
    e = jnp.exp(s - jnp.max(s, axis=-1, keepdims=True))
    p = e / jnp.sum(e, axis=-1, keepdims=True)
    return _bdot(p, v, ((1,), (0,)))


def _cross_specs():
    q = pl.BlockSpec((_MT, LANE), lambda h, i: (i, h))
    k = pl.BlockSpec((MEM_LEN, LANE), lambda h, i: (0, h))
    v = pl.BlockSpec((MEM_LEN, LANE), lambda h, i: (0, CROSS_HEADS + h))
    return q, k, v


def _cross_fwd(qc, kvc):
    S = qc.shape[0]
    q, k, v = _cross_specs()

    def body(q_ref, k_ref, v_ref, o_ref):
        o_ref[...] = _cross(q_ref[...], k_ref[...], v_ref[...]).astype(o_ref.dtype)

    return _pcall(body, grid=(CROSS_HEADS, S // _MT), in_specs=[q, k, v], out_specs=q,
                          out_shape=SDS((S, CROSS_WIDTH), BF), name="cross_fwd")(qc, kvc, kvc)


def _cross_bwd(qc, kvc, doc):
    S = qc.shape[0]
    q, k, v = _cross_specs()

    def body(q_ref, k_ref, v_ref, do_ref, dq_ref, dk_ref, dv_ref):
        @pl.when(pl.program_id(1) == 0)
        def _():
            dk_ref[...] = jnp.zeros_like(dk_ref)
            dv_ref[...] = jnp.zeros_like(dv_ref)

        _, vjp = jax.vjp(_cross, q_ref[...], k_ref[...], v_ref[...])
        dq, dk, dv = vjp(do_ref[...])
        dq_ref[...] = dq.astype(dq_ref.dtype)
        dk_ref[...] += dk
        dv_ref[...] += dv

    return _pcall(body, grid=(CROSS_HEADS, S // _MT), in_specs=[q, k, v, q], out_specs=[q, k, k],
                          out_shape=[SDS((S, CROSS_WIDTH), BF), SDS((MEM_LEN, CROSS_WIDTH), F32), SDS((MEM_LEN, CROSS_WIDTH), F32)],
                          name="cross_bwd")(qc, kvc, kvc, doc)


def _local_step(x, mem, tgt, p, ready):
    h = _rms_fwd(x, p["ln_mix"], "rms_mix")
    proj = _mm(h, p["w_in_a"], "nt", BF, "mm_proj_a", a_part=(0, 2))
    proj = _mm(h, p["w_in_b"], "nt", BF, "mm_proj_b", a_part=(1, 2), res=proj)
    o_att, lse = _attn_fwd(proj)
    o_hg, states = _hg_fwd(proj, p["hg_lb"], p["hg_norm"])
    ba = _mm(o_att, p["w_ba"], "nn", BF, "mm_ba")
    gates = [(proj, GA0 * LANE // _MT), (proj, GB0 * LANE // _MT)]

    def f32s(*vals):
        return [v.astype(F32) for v in vals]

    bb, merged = _mm_fused(o_hg, p["w_bb"], "nn", "mm_bb_merge", gates + [ba], [BF, BF], tn=_MT,
                           epi=lambda d, ga, gb, bav: (d, _merge(*f32s(ga, gb, bav), d)))
    x1 = _mm(merged, p["w_out"], "nn", F32, "mm_out", res=x)
    hc = _rms_fwd(x1, p["ln_cross"], "rms_cross")
    qc = _mm(hc, p["wq"], "nn", F32, "mm_q")
    mn = _rms_fwd(mem, p["ln_mem"], "rms_mem")
    kvc = _mm(mn, p["wkv"], "nn", F32, "mm_kv")
    oc = _cross_fwd(qc, kvc)
    x2 = _mm(oc, p["wo"], "nn", F32, "mm_o", res=x1)
    hf = _rms_fwd(x2, p["ln_ffn"], "rms_ffn")
    a = _mm(hf, p["w1"], "nt", BF, "mm_w1")
    b, g = _mm_fused(hf, p["w3"], "nt", "mm_w3_swiglu", [a], [BF, BF], lambda d, av: (d, _swiglu(av.astype(F32), d)))
    x3 = _mm(g, p["w2"], "nn", F32, "mm_w2", res=x2)
    loss, dx3, dx3b, d_ln_final = _final_loss(x3, p["ln_final"], tgt)

    gw, gs = {}, {"ln_final": d_ln_final}
    da, db = _mm_fused(dx3b, p["w2"], "nt", "mm_dg_swiglu", [a, b], [BF, BF],
                       lambda d, av, bv: jax.vjp(_swiglu, av.astype(F32), bv.astype(F32))[1](d))
    gw["w2"] = _mm(g, dx3b, "tn", BF, "mm_dw2")
    dhf = _mm(da, p["w1"], "nn", F32, "mm_dhf1")
    dhf = _mm(db, p["w3"], "nn", BF, "mm_dhf3", res=dhf)
    gw["w1"] = _mm(da, hf, "tn", BF, "mm_dw1")
    gw["w3"] = _mm(db, hf, "tn", BF, "mm_dw3")
    ready("ffn", gw)
    dx2, dx2b, gs["ln_ffn"] = _rms_bwd(x2, p["ln_ffn"], dhf, dx3, "rms_ffn_bwd")
    doc = _mm(dx2b, p["wo"], "nt", F32, "mm_doc")
    gw["wo"] = _mm(oc, dx2b, "tn", BF, "mm_dwo")
    dqc, dk, dv = _cross_bwd(qc, kvc, doc)
    dkvc = jnp.concatenate([dk, dv], axis=1)
    dhc = _mm(dqc, p["wq"], "nt", BF, "mm_dhc")
    gw["wq"] = _mm(hc, dqc, "tn", BF, "mm_dwq")
    dmn = _mm(dkvc, p["wkv"], "nt", F32, "mm_dmn")
    gw["wkv"] = _mm(mn, dkvc, "tn", BF, "mm_dwkv")
    ready("cross", gw)
    gs["ln_mem"] = _rms_bwd(mem, p["ln_mem"], dmn, None, "rms_mem_bwd")
    dx1, dx1b, gs["ln_cross"] = _rms_bwd(x1, p["ln_cross"], dhc, dx2, "rms_cross_bwd")
    def merge_back(d, ga, gb, bav, bbv):
        dga, dgb, dba_, dbb_ = jax.vjp(_merge, *f32s(ga, gb, bav, bbv))[1](d)
        return dba_, dbb_, dga, dgb

    dba, dbb, dproj = _mm_fused(dx1b, p["w_out"], "nt", "mm_dmerged_merge", gates + [ba, bb], [BF, BF], merge_back, tn=_MT,
                                scatter=(SDS((x.shape[0], IN_WIDTH), BF), [GA0 * LANE, GB0 * LANE]))
    gw["w_out"] = _mm(merged, dx1b, "tn", BF, "mm_dwout")
    do_att = _mm(dba, p["w_ba"], "nt", F32, "mm_doatt")
    gw["w_ba"] = _mm(o_att, dba, "tn", BF, "mm_dwba")
    do_hg = _mm(dbb, p["w_bb"], "nt", F32, "mm_dohg")
    gw["w_bb"] = _mm(o_hg, dbb, "tn", BF, "mm_dwbb")
    ready("mix", gw)
    dproj = _attn_bwd(proj, o_att, lse, do_att, dproj)
    dproj, gs["hg_lb"], gs["hg_norm"] = _hg_bwd(proj, p["hg_lb"], p["hg_norm"], states, do_hg, dproj)
    ready("mixers_done", gw)
    for q in range(IN_PARTS):
        gw[f"w_in_q{q}"] = _mm(dproj, h, "tn", BF, f"mm_dwin_q{q}", b_part=(q, IN_PARTS))
        ready(f"in_q{q}", gw)
    dh_a = _mm(dproj, p["w_in_a"], "nn", BF, "mm_dh_a")
    ready("dh_half", gw)
    dh = (dh_a, _mm(dproj, p["w_in_b"], "nn", BF, "mm_dh_b"))
    grad_x, _, gs["ln_mix"] = _rms_bwd(x, p["ln_mix"], dh, dx1, "rms_mix_bwd")
    return loss, grad_x, gs


_MATS = (
    ("w_in_a", "w_in", IN_WIDTH, D_MODEL // 2, "row", IN_WIDTH // N_DEV, True),
    ("w_in_b", "w_in", IN_WIDTH, D_MODEL // 2, "row", IN_WIDTH // N_DEV, True),
    ("w_ba", "w_branch_a", ATT_OUT, D_MODEL, "col", D_MODEL // N_DEV, False),
    ("w_bb", "w_branch_b", HG_WIDTH, D_MODEL, "col", D_MODEL // N_DEV, False),
    ("w_out", "w_out", D_MODEL, D_MODEL, "row", D_MODEL // N_DEV, False),
    ("wq", "wq_cross", D_MODEL, CROSS_WIDTH, "row", D_MODEL // N_DEV, False),
    ("wkv", "wkv_cross", D_MODEL, 2 * CROSS_WIDTH, "row", D_MODEL // N_DEV, False),
    ("wo", "wo_cross", CROSS_WIDTH, D_MODEL, "col", D_MODEL // N_DEV, False),
    ("w1", "w1", D_FF, D_MODEL, "row", D_FF // N_DEV, True),
    ("w3", "w3", D_FF, D_MODEL, "row", D_FF // N_DEV, True),
    ("w2", "w2", D_FF, D_MODEL, "row", D_FF // N_DEV, False),
    ("w_in_q0", "w_in", IN_WIDTH, D_MODEL // 2, "row", IN_WIDTH // N_DEV, True),
    ("w_in_q1", "w_in", IN_WIDTH, D_MODEL // 2, "row", IN_WIDTH // N_DEV, True),
)
IN_PARTS = 2
_INDEX = {m[0]: i for i, m in enumerate(_MATS)}


def _unit_shape(i):
    _, _, K, N, kind, sh, _ = _MATS[i]
    return (sh, N) if kind == "row" else (K, sh)


def _win(ref, i, chip, par, half=None):
    _, _, K, _, kind, sh, _ = _MATS[i]
    if kind == "row":
        rows, start = (sh, 0) if half is None else (sh // 2, half * (sh // 2))
        return ref.at[pl.ds(pl.multiple_of((2 * chip + par) * sh + start, 16), rows), :]
    rows, start = (K, 0) if half is None else (K // 2, half * (K // 2))
    return ref.at[pl.ds(start, rows), pl.ds(pl.multiple_of((2 * chip + par) * sh, LANE), sh)]


def _my_place():
    x, y, c = lax.axis_index("x"), lax.axis_index("y"), lax.axis_index("c")
    chips = [(1 - x, y), (x, 1 - y), (1 - x, 1 - y)]
    return x, y, c, chips


def _cast_unit(w, name, half=None):
    K, N = w.shape
    if half is not None:
        N = N // 2
    tr = _pick(K, (256, 400, 352))

    def body(w_ref, o_ref):
        o_ref[...] = w_ref[...].astype(BF)

    src = pl.BlockSpec((tr, N), lambda i: (i, half or 0))
    dst = pl.BlockSpec((tr, N), lambda i: (i, 0))
    return _pcall(body, grid=(K // tr,), in_specs=[src], out_specs=dst, out_shape=SDS((K, N), BF), name=name)(w)


def _handshake(peers):
    barrier = pltpu.get_barrier_semaphore()
    for peer in peers:
        pl.semaphore_signal(barrier, inc=1, device_id=peer, device_id_type=MESH)
    pl.semaphore_wait(barrier, len(peers))


def _sequencer(body, out_type, sems, cid, name):
    return pl.kernel(body, out_type=out_type, mesh=plsc.ScalarSubcoreMesh(axis_name="seq", num_cores=1),
                     scratch_types=sems, compiler_params=pltpu.CompilerParams(collective_id=cid), name=name)


def _all_gather(units, idx, cid, name):
    n = len(idx)

    def body(*refs):
        us, fulls = refs[:n], refs[n:2 * n]
        ssem, rsem, lsem = refs[2 * n:]
        x, y, c, _ = _my_place()
        xn, yn, dg = (1 - x, y), (x, 1 - y), (1 - x, 1 - y)
        me, sib = (x, y, c), (x, y, 1 - c)
        _handshake([sib, (*xn, c), (*yn, c)])

        def win(i, chip_xy, par, half=None):
            return _win(fulls[i], idx[i], 2 * chip_xy[0] + chip_xy[1], par, half)

        def rcopy(i, k, src, dst, to):
            return pltpu.make_async_remote_copy(src_ref=src, dst_ref=dst, send_sem=ssem.at[i, k], recv_sem=rsem.at[i, k],
                                                device_id=to, device_id_type=MESH)

        def landed(i, k, blk):
            rcopy(i, k, blk, blk, me).wait_recv()

        sends, locs = [], []

        def send(i, k, blk, to, src=None):
            cp = rcopy(i, k, blk if src is None else src, blk, to)
            cp.start()
            sends.append(cp)

        for i in range(n):
            mine = win(i, (x, y), c)
            loc = pltpu.make_async_copy(us[i], mine, lsem.at[i])
            loc.start()
            locs.append(loc)
            send(i, 0, mine, sib, src=us[i])
            send(i, 1, mine, (*xn, c), src=us[i])
            send(i, 2, mine, (*yn, c), src=us[i])
        for i in range(n):
            landed(i, 1, win(i, xn, c))
            send(i, 3, win(i, xn, c, 0), (*yn, c))
            send(i, 5, win(i, xn, c), sib)
            landed(i, 2, win(i, yn, c))
            send(i, 4, win(i, yn, c, 1), (*xn, c))
            send(i, 6, win(i, yn, c), sib)
        for i in range(n):
            landed(i, 3, win(i, dg, c, 0))
            landed(i, 4, win(i, dg, c, 1))
            send(i, 7, win(i, dg, c), sib)
        for i in range(n):
            landed(i, 0, win(i, (x, y), 1 - c))
            landed(i, 5, win(i, xn, 1 - c))
            landed(i, 6, win(i, yn, 1 - c))
            landed(i, 7, win(i, dg, 1 - c))
        for cp in sends:
            cp.wait_send()
        for loc in locs:
            loc.wait()

    out_type = [SDS((_MATS[i][2], _MATS[i][3]), BF) for i in idx]
    sems = [pltpu.SemaphoreType.DMA((n, 8)), pltpu.SemaphoreType.DMA((n, 8)), pltpu.SemaphoreType.DMA((n,))]
    return _sequencer(body, out_type, sems, cid, name)(*units)


def _rs_sibling(grads, idx, cid, name):
    n = len(idx)

    def body(*refs):
        gs, bufs = refs[:n], refs[n:2 * n]
        ssem, rsem = refs[2 * n:]
        x, y, c, _ = _my_place()
        _handshake([(x, y, 1 - c)])
        cps = []
        for i in range(n):
            for k in range(4):
                cp = pltpu.make_async_remote_copy(
                    src_ref=_win(gs[i], idx[i], k, 1 - c), dst_ref=bufs[i].at[k], send_sem=ssem.at[i, k], recv_sem=rsem.at[i, k],
                    device_id=(x, y, 1 - c), device_id_type=MESH)
                cp.start()
                cps.append(cp)
        for cp in cps:
            cp.wait()

    out_type = [SDS((4, *_unit_shape(i)), BF) for i in idx]
    sems = [pltpu.SemaphoreType.DMA((n, 4)), pltpu.SemaphoreType.DMA((n, 4))]
    return _sequencer(body, out_type, sems, cid, name)(*grads)


def _sibling_sum_tc(grad, i, cid, name):
    unit = _unit_shape(i)

    def body(g_ref, t_ref, theirs, mine, summed, ssem, rsem, lsem, osem):
        x, y, c, _ = _my_place()
        _handshake([(x, y, 1 - c)])
        sends, loads = [], []
        for k in range(4):
            sends.append(pltpu.make_async_remote_copy(
                src_ref=_win(g_ref, i, k, 1 - c), dst_ref=theirs.at[k], send_sem=ssem.at[k], recv_sem=rsem.at[k],
                device_id=(x, y, 1 - c), device_id_type=MESH))
            loads.append(pltpu.make_async_copy(_win(g_ref, i, k, c), mine.at[k], lsem.at[k]))
            sends[-1].start()
            loads[-1].start()
        stores = []
        for k in range(4):
            loads[k].wait()
            sends[k].wait_recv()
            summed[k] = (mine[k].astype(F32) + theirs[k].astype(F32)).astype(BF)
            stores.append(pltpu.make_async_copy(summed.at[k], t_ref.at[k], osem.at[k]))
            stores[-1].start()
        for k in range(4):
            sends[k].wait_send()
            stores[k].wait()

    block = pltpu.VMEM((4, *unit), BF)
    return _pcall(body, in_specs=[_ANY], out_specs=_ANY, out_shape=SDS((4, *unit), BF),
                  scratch_shapes=[block, block, block] + [pltpu.SemaphoreType.DMA((4,))] * 4,
                  compiler_params=_cp(VMEM_BIG, collective_id=cid), name=name)(grad)


def _chip_sum(g, sib, i, place, name):
    _, _, K, N, kind, sh, _ = _MATS[i]

    def body(pref, g_ref, s_ref, o_ref):
        o_ref[...] = (g_ref[...].astype(F32) + s_ref[...].astype(F32)).astype(BF)

    if kind == "row":
        tr = _pick(sh, (800, 352, 256))
        per = sh // tr
        grid = (4, per)
        g_spec = pl.BlockSpec((tr, N), lambda k, r, pref: ((2 * k + pref[0]) * per + r, 0))
        u_spec = pl.BlockSpec((None, tr, N), lambda k, r, pref: (k, r, 0))
    else:
        grid = (4,)
        g_spec = pl.BlockSpec((K, sh), lambda k, pref: (0, 2 * k + pref[0]))
        u_spec = pl.BlockSpec((None, K, sh), lambda k, pref: (k, 0, 0))
    return _pcall(body, grid=grid, in_specs=[g_spec, u_spec], out_specs=u_spec, out_shape=SDS((4, *_unit_shape(i)), BF),
                  prefetch=True, name=name)(place, g, sib)


def _rs_chips(parts, idx, cid, name):
    n = len(idx)

    def body(*refs):
        ts, bufs, relays = refs[:n], refs[n:2 * n], refs[2 * n:3 * n]
        ssem, rsem = refs[3 * n:]
        x, y, c, _ = _my_place()
        xn, yn, dg = (1 - x, y), (x, 1 - y), (1 - x, 1 - y)
        to_x, to_y, me = (*xn, c), (*yn, c), (x, y, c)
        _handshake([to_x, to_y])

        def rcopy(i, k, src, dst, to):
            return pltpu.make_async_remote_copy(src_ref=src, dst_ref=dst, send_sem=ssem.at[i, k], recv_sem=rsem.at[i, k],
                                                device_id=to, device_id_type=MESH)

        def halves(i):
            rows = _unit_shape(idx[i])[0] // 2
            return [pl.ds(0, rows), pl.ds(rows, rows)]

        sends = []

        def send(cp):
            cp.start()
            sends.append(cp)

        for i in range(n):
            h0, h1 = halves(i)
            diag = ts[i].at[2 * dg[0] + dg[1]]
            send(rcopy(i, 0, ts[i].at[2 * xn[0] + xn[1]], bufs[i].at[0], to_x))
            send(rcopy(i, 1, ts[i].at[2 * yn[0] + yn[1]], bufs[i].at[1], to_y))
            send(rcopy(i, 2, diag.at[h0], relays[i].at[0], to_x))
            send(rcopy(i, 3, diag.at[h1], relays[i].at[1], to_y))
        for i in range(n):
            h0, h1 = halves(i)
            rcopy(i, 2, relays[i].at[0], relays[i].at[0], me).wait_recv()
            send(rcopy(i, 4, relays[i].at[0], bufs[i].at[2, h0], to_y))
            rcopy(i, 3, relays[i].at[1], relays[i].at[1], me).wait_recv()
            send(rcopy(i, 5, relays[i].at[1], bufs[i].at[2, h1], to_x))
        for i in range(n):
            h0, h1 = halves(i)
            rcopy(i, 0, bufs[i].at[0], bufs[i].at[0], me).wait_recv()
            rcopy(i, 1, bufs[i].at[1], bufs[i].at[1], me).wait_recv()
            rcopy(i, 4, bufs[i].at[2, h0], bufs[i].at[2, h0], me).wait_recv()
            rcopy(i, 5, bufs[i].at[2, h1], bufs[i].at[2, h1], me).wait_recv()
        for cp in sends:
            cp.wait_send()

    def relay_shape(i):
        rows, cols = _unit_shape(i)
        return (2, rows // 2, cols)

    out_type = [SDS((3, *_unit_shape(i)), BF) for i in idx] + [SDS(relay_shape(i), BF) for i in idx]
    sems = [pltpu.SemaphoreType.DMA((n, 6)), pltpu.SemaphoreType.DMA((n, 6))]
    return _sequencer(body, out_type, sems, cid, name)(*parts)[:n]


def _adamw(w, g, m, v):
    m = ADAM_B1 * m + (1.0 - ADAM_B1) * g
    v = ADAM_B2 * v + (1.0 - ADAM_B2) * jnp.square(g)
    m_hat = m / (1.0 - ADAM_B1 ** ADAM_STEP)
    v_hat = v / (1.0 - ADAM_B2 ** ADAM_STEP)
    delta = -ADAM_LR * (m_hat / (jnp.sqrt(v_hat) + ADAM_EPS) + ADAM_WD * w)
    return delta, m, v


def _adam_mat(parts, recvs, w, m, v, place, name):
    R, C = w.shape
    n = len(parts)
    tr = _pick(R, (256, 160, 176))

    def body(pref, *refs):
        t_refs, r_refs = refs[:n], refs[n:2 * n]
        w_ref, m_ref, v_ref, g_out, d_out, m_out, v_out = refs[2 * n:]
        cols = [t[...].astype(F32) + r[0].astype(F32) + r[1].astype(F32) + r[2].astype(F32) for t, r in zip(t_refs, r_refs)]
        g = jnp.concatenate(cols, axis=1) if n > 1 else cols[0]
        d, mn, vn = _adamw(w_ref[...], g, m_ref[...], v_ref[...])
        g_out[...] = g
        d_out[...] = d
        m_out[...] = mn
        v_out[...] = vn

    t_spec = pl.BlockSpec((None, tr, C // n), lambda i, pref: (pref[1], i, 0))
    r_spec = pl.BlockSpec((3, tr, C // n), lambda i, pref: (0, i, 0))
    s_spec = pl.BlockSpec((tr, C), lambda i, pref: (i, 0))
    return _pcall(body, grid=(R // tr,), in_specs=[t_spec] * n + [r_spec] * n + [s_spec] * 3, out_specs=[s_spec] * 4,
                  out_shape=[SDS((R, C), F32)] * 4, prefetch=True, name=name)(place, *parts, *recvs, w, m, v)


_SMALL = (("ln_mix", "ln_mix_w", D_MODEL), ("ln_cross", "ln_cross_w", D_MODEL), ("ln_mem", "ln_mem_w", D_MODEL),
          ("ln_ffn", "ln_ffn_w", D_MODEL), ("ln_final", "ln_final_w", D_MODEL), ("hg_lb", "hg_lower_bounds", 2 * HG_WIDTH),
          ("hg_norm", "hg_norm_w", HEAD_DIM))
_PACK = sum(n for _, _, n in _SMALL) + LANE


def _small_sync(grow, wrow, mrow, vrow):
    def body(g_ref, w_ref, m_ref, v_ref, o_ref, gath, ssem, rsem):
        x, y, c, _ = _my_place()
        me = 4 * x + 2 * y + c
        gath[pl.ds(me, 1), :] = g_ref[...]
        cps = []
        for dlt in range(1, N_DEV):
            peer = (me + dlt) % N_DEV
            cp = pltpu.make_async_remote_copy(
                src_ref=g_ref, dst_ref=gath.at[pl.ds(me, 1), :], send_sem=ssem.at[peer], recv_sem=rsem.at[me],
                device_id=(peer // 4, (peer // 2) % 2, peer % 2), device_id_type=MESH)
            cp.start()
            cps.append(cp)
        for dlt in range(1, N_DEV):
            src = (me + dlt) % N_DEV
            pltpu.make_async_remote_copy(
                src_ref=g_ref, dst_ref=gath.at[pl.ds(src, 1), :], send_sem=ssem.at[src], recv_sem=rsem.at[src],
                device_id=(x, y, c), device_id_type=MESH).wait_recv()
        for cp in cps:
            cp.wait_send()
        g = gath[0:1, :]
        for j in range(1, N_DEV):
            g = g + gath[j:j + 1, :]
        d, mn, vn = _adamw(w_ref[...], g, m_ref[...], v_ref[...])
        o_ref[0:1, :] = g
        o_ref[1:2, :] = d
        o_ref[2:3, :] = mn
        o_ref[3:4, :] = vn

    vm = pl.BlockSpec(memory_space=pltpu.VMEM)
    return _pcall(
        body, in_specs=[vm] * 4, out_specs=vm, out_shape=SDS((4, _PACK), F32),
        scratch_shapes=[pltpu.VMEM((N_DEV, _PACK), F32), pltpu.SemaphoreType.DMA((N_DEV,)), pltpu.SemaphoreType.DMA((N_DEV,))],
        name="small_sync")(grow, wrow, mrow, vrow)


def _pack_small(d, loss_row=None):
    parts = [d[k].reshape(1, n).astype(F32) for k, _, n in _SMALL]
    parts.append(loss_row if loss_row is not None else jnp.ones((1, LANE), F32))
    return jnp.concatenate(parts, axis=1)


def kernel(x, mem, ln_mix_w, w_in, hg_norm_w, hg_lower_bounds, w_branch_a, w_branch_b, w_out, ln_cross_w, ln_mem_w, wq_cross, wkv_cross, wo_cross, ln_ffn_w, w1, w3, w2, ln_final_w, loss_target, m_ln_mix_w, m_w_in, m_hg_norm_w, m_hg_lower_bounds, m_w_branch_a, m_w_branch_b, m_w_out, m_ln_cross_w, m_ln_mem_w, m_wq_cross, m_wkv_cross, m_wo_cross, m_ln_ffn_w, m_w1, m_w3, m_w2, m_ln_final_w, v_ln_mix_w, v_w_in, v_hg_norm_w, v_hg_lower_bounds, v_w_branch_a, v_w_branch_b, v_w_out, v_ln_cross_w, v_ln_mem_w, v_wq_cross, v_wkv_cross, v_wo_cross, v_ln_ffn_w, v_w1, v_w3, v_w2, v_ln_final_w):
    given = dict(locals())
    place = jnp.stack([lax.axis_index("c"), 2 * lax.axis_index("x") + lax.axis_index("y")]).astype(jnp.int32)

    _ORDER[0] = None
    index = _INDEX

    def local_view(a, i):
        return a[0].T if _MATS[i][6] else a[0]

    shard = {key: local_view(given[ref], i) for i, (key, ref, *_) in enumerate(_MATS)}

    p = {}
    col_half = {"w_in_a": 0, "w_in_b": 1}
    waves = (("in_a", ["w_in_a"], 1), ("in_b", ["w_in_b"], 12), ("mid", ["w_ba", "w_bb", "w_out", "wq", "wkv", "wo"], 2),
             ("w1", ["w1"], 11), ("w3", ["w3"], 15), ("w2", ["w2"], 16))
    for wave, keys, cid in waves:
        units = [_cast_unit(shard[k], "cast_" + k, col_half.get(k)) for k in keys]
        p.update(zip(keys, _all_gather(units, [index[k] for k in keys], cid, "all_gather_" + wave)))
    small_w = {"ln_mix": ln_mix_w, "ln_cross": ln_cross_w, "ln_mem": ln_mem_w, "ln_ffn": ln_ffn_w,
               "ln_final": ln_final_w.reshape(1, D_MODEL), "hg_lb": hg_lower_bounds, "hg_norm": hg_norm_w}
    p.update(small_w)

    groups = {"ffn": ["w2", "w1", "w3"], "cross": ["wo", "wq", "wkv"], "mix": ["w_out", "w_ba", "w_bb"]}
    cids = {"ffn": (3, 4), "cross": (5, 6), "mix": (7, 8)}
    for q in range(IN_PARTS):
        groups[f"in_q{q}"] = [f"w_in_q{q}"]
        cids[f"in_q{q}"] = ((9, 10), (13, 14), (17, 18), (19, 20))[q]
    sib, parts, recv, out = {}, {}, {}, {}

    grads = {}

    def chip_stage(name):
        keys = groups[name]
        parts[name] = [_chip_sum(grads[k], s, index[k], place, "chip_sum_" + k) for k, s in zip(keys, sib[name])]
        recv[name] = list(_rs_chips(parts[name], [index[k] for k in keys], cids[name][1], "rs_chips_" + name))

    def adam(ref, i, part_list, recv_list):
        res = _adam_mat(part_list, recv_list, shard[_MATS[i][0]], local_view(given["m_" + ref], i), local_view(given["v_" + ref], i),
                        place, "adam_" + ref)
        out[ref] = [(a.T if _MATS[i][6] else a)[None] for a in res]

    def adam_stage(name):
        for k, t, r in zip(groups[name], parts[name], recv[name]):
            adam(_MATS[index[k]][1], index[k], [t], [r])

    def ready(name, gw):
        if name == "mixers_done":
            adam_stage("ffn")
            chip_stage("mix")
            adam_stage("cross")
            return
        if name == "dh_half":
            return
        keys = groups[name]
        grads.update({k: gw[k] for k in keys})
        if name.startswith("in_q"):
            if name == "in_q1":
                adam_stage("mix")
            parts[name] = [_sibling_sum_tc(gw[k], index[k], cids[name][0], "sibling_sum_" + k) for k in keys]
            recv[name] = list(_rs_chips(parts[name], [index[k] for k in keys], cids[name][1], "rs_chips_" + name))
            return
        sib[name] = _rs_sibling([gw[k] for k in keys], [index[k] for k in keys], cids[name][0], "rs_sibling_" + name)
        if name == "cross":
            chip_stage("ffn")
        if name == "mix":
            chip_stage("cross")

    loss_row, grad_x, gs = _local_step(x[0], mem[0], loss_target[0], p, ready)

    sm = {k: given["m_" + ref] for k, ref, _ in _SMALL}
    sv = {k: given["v_" + ref] for k, ref, _ in _SMALL}
    synced = _small_sync(_pack_small(gs, loss_row), _pack_small(small_w), _pack_small(sm), _pack_small(sv))
    in_groups = [f"in_q{q}" for q in range(IN_PARTS)]
    adam("w_in", index["w_in_a"], sum((parts[g] for g in in_groups), []), sum((recv[g] for g in in_groups), []))
    off = 0
    for k, ref, n in _SMALL:
        shape = given[ref].shape
        out[ref] = [synced[r, off:off + n].reshape(shape) for r in range(4)]
        off += n
    loss = synced[0, off]

    order = ["ln_mix_w", "w_in", "hg_norm_w", "hg_lower_bounds", "w_branch_a", "w_branch_b", "w_out", "ln_cross_w", "ln_mem_w",
             "wq_cross", "wkv_cross", "wo_cross", "ln_ffn_w", "w1", "w3", "w2", "ln_final_w"]
    return (loss, grad_x[None], *[out[n][0] for n in order], *[out[n][1] for n in order],
            *[out[n][2] for n in order], *[out[n][3] for n in order])
```

```python
import functools
import math

import jax
import jax.numpy as jnp
from jax import lax
from jax.experimental import pallas as pl
from jax.experimental.pallas import tpu as pltpu
from jax.experimental.pallas import tpu_sc as plsc

F32 = jnp.float32
BF = jnp.bfloat16
SDS = jax.ShapeDtypeStruct

D_MODEL = 2048
SEQ = 2048
HEAD_DIM = 128
MEM_LEN = 256
ATT_GROUPS = ((128, 1), (512, 4), (2048, 16))
ATT_HEADS = 4
ATT_WIDTH = 1536
ATT_OUT = 512
HG_HEADS = 8
HG_WIDTH = 1024
HG_CHUNK = 64
IN_WIDTH = 12800
CROSS_HEADS = 4
CROSS_WIDTH = 512
D_FF = 5632
RMS_EPS = 1e-6
ADAM_LR = 0.001
ADAM_B1 = 0.9
ADAM_B2 = 0.999
ADAM_EPS = 1e-08
ADAM_WD = 0.01
ADAM_STEP = 10
N_DEV = 8

LANE = 128
QA0, KA0, VA0 = 0, 12, 24
QH0, FH0, IH0, GH0 = 36, 44, 52, 60
GA0, GB0 = 68, 84

VMEM_BIG = 56 * 1024 * 1024
MESH = pl.DeviceIdType.MESH


def _cp(vmem=None, **kw):
    if vmem is not None:
        kw["vmem_limit_bytes"] = vmem
    return pltpu.CompilerParams(**kw)


def _pick(n, cands):
    for c in cands:
        if n % c == 0:
            return c
    raise ValueError(f"no tile for {n}")


_ANY = pl.BlockSpec(memory_space=pl.ANY)
_ORDER = [None]


def _pcall(body, *, name, in_specs, out_specs, out_shape, grid=(), scratch_shapes=(), compiler_params=None, prefetch=False,
           aliases=None):
    def run(*args):
        dep = _ORDER[0]
        specs, operands, kernel_fn = list(in_specs), list(args), body
        if dep is not None and not any(dep is a for a in operands):
            at = len(operands)
            specs.append(_ANY)
            operands.append(dep)

            def kernel_fn(*refs):
                return body(*refs[:at], *refs[at + 1:])
        if prefetch:
            call = pl.pallas_call(
                kernel_fn, out_shape=out_shape, name=name, compiler_params=compiler_params,
                grid_spec=pltpu.PrefetchScalarGridSpec(num_scalar_prefetch=1, grid=grid, in_specs=specs, out_specs=out_specs,
                                                       scratch_shapes=scratch_shapes))
        else:
            call = pl.pallas_call(kernel_fn, grid=grid, in_specs=specs, out_specs=out_specs, out_shape=out_shape,
                                  scratch_shapes=scratch_shapes, compiler_params=compiler_params, name=name,
                                  input_output_aliases=aliases or {})
        out = call(*operands)
        _ORDER[0] = out[0] if isinstance(out, (list, tuple)) else out
        return out
    return run


def _mm(a, b, form, out_dtype, name, res=None, tiles=None, a_part=(0, 1), b_part=(0, 1)):
    a_shape = (a.shape[0], a.shape[1] // a_part[1])
    b_shape = (b.shape[0], b.shape[1] // b_part[1])
    if form == "nn":
        (M, K), (K2, N) = a_shape, b_shape
    elif form == "nt":
        (M, K), (N, K2) = a_shape, b_shape
    else:
        (K, M), (K2, N) = a_shape, b_shape
    assert K == K2, (a.shape, b.shape, form)
    if tiles is None:
        tm = _pick(M, (1024, 1280, 1408, 512, 256))
        tn = _pick(N, (1280, 1024, 1408, 512, 256))
        tk = K if K <= 2048 else _pick(K, (D_FF, 2560))
        if tk == D_FF:
            tm = _pick(M, (512, 256))
    else:
        tm, tn, tk = tiles
    nk = K // tk
    ah, bh = a_part[0], b_part[0]
    if form == "nn":
        a_spec = pl.BlockSpec((tm, tk), lambda i, j, k: (i, k + ah * nk))
        b_spec = pl.BlockSpec((tk, tn), lambda i, j, k: (k, j + bh * (N // tn)))
        dims = ((1,), (0,))
    elif form == "nt":
        a_spec = pl.BlockSpec((tm, tk), lambda i, j, k: (i, k + ah * nk))
        b_spec = pl.BlockSpec((tn, tk), lambda i, j, k: (j, k + bh * nk))
        dims = ((1,), (1,))
    else:
        a_spec = pl.BlockSpec((tk, tm), lambda i, j, k: (k, i + ah * (M // tm)))
        b_spec = pl.BlockSpec((tk, tn), lambda i, j, k: (k, j + bh * (N // tn)))
        dims = ((0,), (0,))
    o_spec = pl.BlockSpec((tm, tn), lambda i, j, k: (i, j))
    in_specs = [a_spec, b_spec]
    args = [a, b]
    if res is not None:
        in_specs.append(o_spec)
        args.append(res)

    def body(*refs):
        a_ref, b_ref = refs[0], refs[1]
        r_ref = refs[2] if res is not None else None
        o_ref = refs[3] if res is not None else refs[2]
        acc = refs[-1] if nk > 1 else None
        k = pl.program_id(2)
        d = lax.dot_general(a_ref[...].astype(BF), b_ref[...].astype(BF), (dims, ((), ())), preferred_element_type=F32)

        def finish(r):
            if res is not None:
                r = r + r_ref[...].astype(F32)
            o_ref[...] = r.astype(o_ref.dtype)

        if nk == 1:
            finish(d)
        else:
            @pl.when(k == 0)
            def _():
                acc[...] = d

            @pl.when((k > 0) & (k < nk - 1))
            def _():
                acc[...] += d

            @pl.when(k == nk - 1)
            def _():
                finish(acc[...] + d)

    return _pcall(
        body, grid=(M // tm, N // tn, nk), in_specs=in_specs, out_specs=o_spec,
        out_shape=SDS((M, N), out_dtype), scratch_shapes=[pltpu.VMEM((tm, tn), F32)] if nk > 1 else [],
        compiler_params=_cp(VMEM_BIG, dimension_semantics=("parallel", "parallel", "arbitrary")),
        name=name)(*args)


_MXU_COLS = 256


def _mm_fused(a, b, form, name, extras, out_dtypes, epi, tn=None, scatter=None):
    M, K = a.shape
    N = b.shape[1] if form == "nn" else b.shape[0]
    assert K <= 2048 and K == (b.shape[0] if form == "nn" else b.shape[1])
    tm = _pick(M, (1024, 512, 256))
    tn = tn or _pick(N, (1280, 1024, 1408, 512, 256))
    ne, no = len(extras), len(out_dtypes)
    bases = scatter[1] if scatter else []
    dims = ((1,), (0,)) if form == "nn" else ((1,), (1,))

    def body(*refs):
        a_ref, b_ref, ex, outs = refs[0], refs[1], refs[2:2 + ne], refs[2 + ne:2 + ne + no]
        av = a_ref[...].astype(BF)
        if scatter:
            wide_ref, stage, sem = refs[2 + ne + no:]
        for c0 in range(0, tn, _MXU_COLS):
            cols = slice(c0, min(c0 + _MXU_COLS, tn))
            bc = b_ref[:, cols] if form == "nn" else b_ref[cols, :]
            d = lax.dot_general(av, bc.astype(BF), (dims, ((), ())), preferred_element_type=F32)
            res = epi(d, *[e[:, cols] for e in ex])
            for o_ref, r in zip(outs, res[:no]):
                o_ref[:, cols] = r.astype(o_ref.dtype)
            for s, r in enumerate(res[no:]):
                stage[s, :, cols] = r.astype(BF)
        if scatter:
            rows = pl.ds(pl.multiple_of(pl.program_id(0) * tm, tm), tm)
            cps = [pltpu.make_async_copy(stage.at[s], wide_ref.at[rows, pl.ds(pl.multiple_of(base + pl.program_id(1) * tn, LANE), tn)],
                                         sem.at[s]) for s, base in enumerate(bases)]
            for cp in cps:
                cp.start()
            for cp in cps:
                cp.wait()

    a_spec = pl.BlockSpec((tm, K), lambda i, j: (i, 0))
    b_spec = pl.BlockSpec((K, tn), lambda i, j: (0, j)) if form == "nn" else pl.BlockSpec((tn, K), lambda i, j: (j, 0))
    o_spec = pl.BlockSpec((tm, tn), lambda i, j: (i, j))
    ex_specs, ex_args = [], []
    for e in extras:
        arr, off = e if isinstance(e, tuple) else (e, 0)
        ex_specs.append(pl.BlockSpec((tm, tn), lambda i, j, off=off: (i, off + j)))
        ex_args.append(arr)
    out_specs, out_shape, scratch = [o_spec] * no, [SDS((M, N), dt) for dt in out_dtypes], []
    if scatter:
        out_specs, out_shape = out_specs + [_ANY], out_shape + [scatter[0]]
        scratch = [pltpu.VMEM((len(bases), tm, tn), BF), pltpu.SemaphoreType.DMA((len(bases),))]
    return _pcall(
        body, grid=(M // tm, N // tn), in_specs=[a_spec, b_spec] + ex_specs, out_specs=out_specs, out_shape=out_shape,
        scratch_shapes=scratch, compiler_params=_cp(VMEM_BIG, dimension_semantics=("parallel", "parallel")),
        name=name)(a, b, *ex_args)


def _rms(x, w):
    return x * lax.rsqrt(jnp.mean(x * x, axis=-1, keepdims=True) + RMS_EPS) * w


def _rms_fwd(x, w, name):
    R, D = x.shape
    tr = 256

    def body(x_ref, w_ref, o_ref):
        o_ref[...] = _rms(x_ref[...], w_ref[...]).astype(o_ref.dtype)

    return _pcall(
        body, grid=(R // tr,),
        in_specs=[pl.BlockSpec((tr, D), lambda i: (i, 0)), pl.BlockSpec((1, D), lambda i: (0, 0))],
        out_specs=pl.BlockSpec((tr, D), lambda i: (i, 0)), out_shape=SDS((R, D), BF), name=name)(x, w)


def _rms_bwd(x, w, dh, dres, name):
    R, D = x.shape
    tr = 256
    want_dx = dres is not None
    dhs = dh if isinstance(dh, (tuple, list)) else (dh,)
    nd = len(dhs)

    def body(*refs):
        x_ref, w_ref, dh_refs = refs[0], refs[1], refs[2:2 + nd]
        if want_dx:
            dr_ref, dx_ref, dxb_ref, dw_ref = refs[2 + nd:]
        else:
            dw_ref = refs[2 + nd]

        @pl.when(pl.program_id(0) == 0)
        def _():
            dw_ref[...] = jnp.zeros_like(dw_ref)

        dhv = jnp.concatenate([r[...].astype(F32) for r in dh_refs], axis=1) if nd > 1 else dh_refs[0][...].astype(F32)
        _, vjp = jax.vjp(_rms, x_ref[...], w_ref[...])
        dx, dw = vjp(dhv)
        dw_ref[...] += dw
        if want_dx:
            dx = dx + dr_ref[...]
            dx_ref[...] = dx
            dxb_ref[...] = dx.astype(BF)

    row = pl.BlockSpec((tr, D), lambda i: (i, 0))
    part = pl.BlockSpec((tr, D // nd), lambda i: (i, 0))
    vec = pl.BlockSpec((1, D), lambda i: (0, 0))
    if want_dx:
        return _pcall(body, grid=(R // tr,), in_specs=[row, vec] + [part] * nd + [row], out_specs=[row, row, vec],
                      out_shape=[SDS((R, D), F32), SDS((R, D), BF), SDS((1, D), F32)], name=name)(x, w, *dhs, dres)
    return _pcall(body, grid=(R // tr,), in_specs=[row, vec] + [part] * nd, out_specs=vec,
                  out_shape=SDS((1, D), F32), name=name)(x, w, *dhs)


def _final_loss(x3, w, tgt):
    R, D = x3.shape
    tr = 256

    def loss_fn(xv, wv, tv):
        err = _rms(xv, wv) - tv
        return 0.5 * jnp.sum(jnp.mean(err * err, axis=-1))

    def body(x_ref, w_ref, t_ref, loss_ref, dx_ref, dxb_ref, dw_ref):
        @pl.when(pl.program_id(0) == 0)
        def _():
            dw_ref[...] = jnp.zeros_like(dw_ref)
            loss_ref[...] = jnp.zeros_like(loss_ref)

        tv = t_ref[...]
        val, vjp = jax.vjp(lambda a, b: loss_fn(a, b, tv), x_ref[...], w_ref[...])
        dx, dw = vjp(jnp.ones((), F32))
        dx_ref[...] = dx
        dxb_ref[...] = dx.astype(BF)
        dw_ref[...] += dw
        loss_ref[...] += jnp.full(loss_ref.shape, val, F32)

    row = pl.BlockSpec((tr, D), lambda i: (i, 0))
    vec = pl.BlockSpec((1, D), lambda i: (0, 0))
    one = pl.BlockSpec((1, LANE), lambda i: (0, 0))
    return _pcall(body, grid=(R // tr,), in_specs=[row, vec, row], out_specs=[one, row, row, vec],
                          out_shape=[SDS((1, LANE), F32), SDS((R, D), F32), SDS((R, D), BF), SDS((1, D), F32)],
                          name="final_loss")(x3, w, tgt)


NEG = -1e30


def _att_blocks(S, d):
    L = S // d
    nb = L // LANE
    return nb, (2 if nb > 1 else 1)


def _att_rows(S, d, idx):
    nb, nkb = _att_blocks(S, d)
    r = idx // nb
    n = idx % nb
    kb = jnp.maximum(n - 1, 0)
    if d == 1:
        qrows = pl.ds(pl.multiple_of(n * LANE, LANE), LANE)
        krows = pl.ds(pl.multiple_of(kb * LANE, LANE), LANE * nkb)
    else:
        qrows = pl.ds(r + n * (LANE * d), LANE, stride=d)
        krows = pl.ds(r + kb * (LANE * d), LANE * nkb, stride=d)
    qpos = n * LANE + lax.broadcasted_iota(jnp.int32, (LANE, LANE * nkb), 0)
    kpos = kb * LANE + lax.broadcasted_iota(jnp.int32, (LANE, LANE * nkb), 1)
    return qrows, krows, qpos - kpos


def _slab(col):
    return pl.BlockSpec((SEQ, LANE), lambda h, col=col: (0, col + h))


def _widen(slabs, wide):
    for t, ref in enumerate(slabs):
        wide[t] = ref[...].astype(F32)
    return tuple(tuple(wide.at[3 * kind + g] for g in range(3)) for kind in range(3))


def _attn_fwd(proj):
    S = proj.shape[0]
    scale = HEAD_DIM ** -0.5

    def body(q0, q1, q2, k0, k1, k2, v0, v1, v2, o_ref, lse_ref, og, lg, wide):
        qs, ks, vs = _widen((q0, q1, q2, k0, k1, k2, v0, v1, v2), wide)
        for g, (window, d) in enumerate(ATT_GROUPS):
            def blk(idx, carry, g=g, window=window, d=d):
                qrows, krows, dist = _att_rows(S, d, idx)
                q = qs[g][qrows, :].astype(BF)
                k = ks[g][krows, :].astype(BF)
                v = vs[g][krows, :].astype(BF)
                s = lax.dot_general(q, k, (((1,), (1,)), ((), ())), preferred_element_type=F32) * scale
                s = jnp.where((dist >= 0) & (dist <= window // d), s, NEG)
                m = jnp.max(s, axis=-1, keepdims=True)
                p = jnp.exp(s - m)
                l = jnp.sum(p, axis=-1, keepdims=True)
                o = jnp.dot((p / l).astype(BF), v, preferred_element_type=F32)
                og[g, qrows, :] = o
                lg[g, qrows, :] = jnp.broadcast_to(m + jnp.log(l), (LANE, LANE))
                return carry
            lax.fori_loop(0, S // LANE, blk, 0, unroll=8)

        def merge(t, carry):
            rows = pl.ds(pl.multiple_of(t * 256, 256), 256)
            l0, l1, l2 = lg[0, rows, :], lg[1, rows, :], lg[2, rows, :]
            m = jnp.maximum(jnp.maximum(l0, l1), l2)
            w0, w1, w2 = jnp.exp(l0 - m), jnp.exp(l1 - m), jnp.exp(l2 - m)
            den = w0 + w1 + w2
            o_ref[rows, :] = (w0 * og[0, rows, :] + w1 * og[1, rows, :] + w2 * og[2, rows, :]) / den
            lse_ref[rows, :] = m + jnp.log(den)
            return carry
        lax.fori_loop(0, S // 256, merge, 0)

    ins = [_slab(QA0 + 4 * g) for g in range(3)] + [_slab(KA0 + 4 * g) for g in range(3)] + [_slab(VA0 + 4 * g) for g in range(3)]
    out = pl.BlockSpec((SEQ, LANE), lambda h: (0, h))
    return _pcall(
        body, grid=(ATT_HEADS,), in_specs=ins, out_specs=[out, out],
        out_shape=[SDS((S, ATT_OUT), F32), SDS((S, ATT_OUT), F32)],
        scratch_shapes=[pltpu.VMEM((3, S, LANE), F32), pltpu.VMEM((3, S, LANE), F32), pltpu.VMEM((9, S, LANE), F32)],
        compiler_params=_cp(VMEM_BIG), name="attn_fwd")(*([proj] * 9))


def _attn_bwd(proj, o_att, lse, do_att, dproj):
    S = proj.shape[0]
    scale = HEAD_DIM ** -0.5

    def body(q0, q1, q2, k0, k1, k2, v0, v1, v2, o_ref, lse_ref, do_ref, dp_in, dp_ref, dqa, dka, dva, delta, stage, sem, wide):
        qs, ks, vs = _widen((q0, q1, q2, k0, k1, k2, v0, v1, v2), wide)
        head = pl.program_id(0)

        def slab_copies(g):
            return [pltpu.make_async_copy(stage.at[t], dp_ref.at[:, pl.ds(pl.multiple_of((base + 4 * g + head) * LANE, LANE), LANE)],
                                          sem.at[t]) for t, base in enumerate((QA0, KA0, VA0))]

        delta[...] = jnp.broadcast_to(jnp.sum(do_ref[...] * o_ref[...], axis=-1, keepdims=True), delta.shape)
        for g, (window, d) in enumerate(ATT_GROUPS):
            dka[...] = jnp.zeros_like(dka)
            dva[...] = jnp.zeros_like(dva)

            def blk(idx, carry, g=g, window=window, d=d):
                qrows, krows, dist = _att_rows(S, d, idx)
                q = qs[g][qrows, :].astype(BF)
                k = ks[g][krows, :].astype(BF)
                v = vs[g][krows, :].astype(BF)
                do = do_ref[qrows, :]
                s = lax.dot_general(q, k, (((1,), (1,)), ((), ())), preferred_element_type=F32) * scale
                p = jnp.exp(s - lse_ref[qrows, :][:, 0:1])
                p = jnp.where((dist >= 0) & (dist <= window // d), p, 0.0)
                dob = do.astype(BF)
                dva[krows, :] += lax.dot_general(p.astype(BF), dob, (((0,), (0,)), ((), ())), preferred_element_type=F32)
                dp = lax.dot_general(dob, v, (((1,), (1,)), ((), ())), preferred_element_type=F32)
                ds = (p * (dp - delta[qrows, :][:, 0:1]) * scale).astype(BF)
                dqa[qrows, :] = jnp.dot(ds, k, preferred_element_type=F32)
                dka[krows, :] += lax.dot_general(ds, q, (((0,), (0,)), ((), ())), preferred_element_type=F32)
                return carry
            lax.fori_loop(0, S // LANE, blk, 0, unroll=8)
            if g > 0:
                for cp in slab_copies(g - 1):
                    cp.wait()
            stage[0] = dqa[...].astype(BF)
            stage[1] = dka[...].astype(BF)
            stage[2] = dva[...].astype(BF)
            for cp in slab_copies(g):
                cp.start()
        for cp in slab_copies(len(ATT_GROUPS) - 1):
            cp.wait()

    cols = [QA0 + 4 * g for g in range(3)] + [KA0 + 4 * g for g in range(3)] + [VA0 + 4 * g for g in range(3)]
    ins = [_slab(c) for c in cols]
    one = pl.BlockSpec((SEQ, LANE), lambda h: (0, h))
    return _pcall(
        body, grid=(ATT_HEADS,), in_specs=ins + [one, one, one, _ANY], out_specs=_ANY,
        out_shape=SDS((S, IN_WIDTH), BF), aliases={12: 0},
        scratch_shapes=[pltpu.VMEM((S, LANE), F32)] * 4 + [pltpu.VMEM((3, S, LANE), BF), pltpu.SemaphoreType.DMA((3,)),
                                                           pltpu.VMEM((9, S, LANE), F32)],
        compiler_params=_cp(VMEM_BIG), name="attn_bwd")(*([proj] * 9), o_att, lse, do_att, dproj)


def _bdot(a, b, dims):
    return lax.dot_general(a.astype(BF), b.astype(BF), (dims, ((), ())), preferred_element_type=F32)


def _hg_chunk(qh, fh, ih, gh, lbraw, nw, st):
    C = HG_CHUNK
    mx = jnp.maximum(lbraw[0:1], lbraw[1:2])
    e0 = jnp.exp(lbraw[0:1] - mx)
    e1 = jnp.exp(lbraw[1:2] - mx)
    lb = e0 / (e0 + e1)
    f = lb + (1.0 - lb) * jax.nn.sigmoid(fh)
    logf = jnp.log(f)
    k = 1.0 - f
    q = qh * jax.nn.sigmoid(qh)
    causal = lax.broadcasted_iota(jnp.int32, (C, C), 0) >= lax.broadcasted_iota(jnp.int32, (C, C), 1)
    b = lax.dot_general(causal.astype(F32), logf, (((1,), (0,)), ((), ())), precision=lax.Precision.HIGHEST,
                        preferred_element_type=F32)
    b_last = b[C - 1:C]
    st_out = st * jnp.exp(b_last) + _bdot(ih, k * jnp.exp(b_last - b), ((0,), (0,)))
    inter = _bdot(q * jnp.exp(b), st, ((1,), (1,)))
    b_ref = b[C // 2:C // 2 + 1]
    a = _bdot(q * jnp.exp(b - b_ref), k * jnp.exp(b_ref - b), ((1,), (1,)))
    a = jnp.where(causal, a, 0.0)
    o = inter + _bdot(a, ih, ((1,), (0,)))
    o = o * lax.rsqrt(jnp.mean(o * o, axis=-1, keepdims=True) + RMS_EPS) * nw
    return o * (gh * jax.nn.sigmoid(gh)), st_out


def _to_heads(x):
    return jnp.concatenate([x[None, :, h * LANE:(h + 1) * LANE] for h in range(HG_HEADS)], axis=0)


def _from_heads(x):
    return jnp.concatenate([x[h] for h in range(HG_HEADS)], axis=1)


def _hdot(a, b, ca, cb):
    return lax.dot_general(a.astype(BF), b.astype(BF), (((ca,), (cb,)), ((0,), (0,))), preferred_element_type=F32)


def _hg_chunk_all(qh, fh, ih, gh, lbraw, nw, st):
    C = HG_CHUNK
    mx = jnp.maximum(lbraw[0:1], lbraw[1:2])
    e0 = jnp.exp(lbraw[0:1] - mx)
    e1 = jnp.exp(lbraw[1:2] - mx)
    lb = e0 / (e0 + e1)
    f = lb + (1.0 - lb) * jax.nn.sigmoid(fh)
    logf = jnp.log(f)
    k = 1.0 - f
    q = qh * jax.nn.sigmoid(qh)
    causal = lax.broadcasted_iota(jnp.int32, (C, C), 0) >= lax.broadcasted_iota(jnp.int32, (C, C), 1)
    b = lax.dot_general(causal.astype(F32), logf, (((1,), (0,)), ((), ())), precision=lax.Precision.HIGHEST,
                        preferred_element_type=F32)
    b_last = b[C - 1:C]
    b_ref = b[C // 2:C // 2 + 1]
    v3 = _to_heads(ih)
    st_out = st * _to_heads(jnp.exp(b_last)) + _hdot(v3, _to_heads(k * jnp.exp(b_last - b)), 1, 1)
    inter = _hdot(_to_heads(q * jnp.exp(b)), st, 2, 2)
    a = _hdot(_to_heads(q * jnp.exp(b - b_ref)), _to_heads(k * jnp.exp(b_ref - b)), 2, 2)
    a = jnp.where(causal[None], a, 0.0)
    o = inter + _hdot(a, v3, 2, 1)
    o = o * lax.rsqrt(jnp.mean(o * o, axis=-1, keepdims=True) + RMS_EPS) * nw
    return _from_heads(o) * (gh * jax.nn.sigmoid(gh)), st_out


def _head_cols(h):
    return slice(h * LANE, (h + 1) * LANE)


def _hg_in_specs(chunk_of):
    half = 4 * LANE
    specs = []
    for base in (QH0, FH0, IH0, GH0):
        for part in range(2):
            specs.append(pl.BlockSpec((HG_CHUNK, half), lambda n, col=base // 4 + part: (chunk_of(n), col)))
    return specs


def _hg_sections(sec):
    return [jnp.concatenate([sec[2 * s][...], sec[2 * s + 1][...]], axis=1).astype(F32) for s in range(4)]


def _hg_fwd(proj, lb, nw):
    S = proj.shape[0]
    C = HG_CHUNK
    NC = S // C

    def body(*refs):
        sec = refs[:8]
        lb_ref, nw_ref, o_ref, st_ref, st_scr = refs[8:]

        @pl.when(pl.program_id(0) == 0)
        def _():
            st_scr[...] = jnp.zeros_like(st_scr)

        st = st_scr[...]
        st_ref[:, 0] = st
        o, st_out = _hg_chunk_all(*_hg_sections(sec), lb_ref[...], nw_ref[...], st)
        o_ref[...] = o.astype(o_ref.dtype)
        st_scr[...] = st_out

    return _pcall(
        body, grid=(NC,),
        in_specs=_hg_in_specs(lambda n: n) + [pl.BlockSpec((2, HG_WIDTH), lambda n: (0, 0)), pl.BlockSpec((1, LANE), lambda n: (0, 0))],
        out_specs=[pl.BlockSpec((C, HG_WIDTH), lambda n: (n, 0)), pl.BlockSpec((HG_HEADS, 1, LANE, LANE), lambda n: (0, n, 0, 0))],
        out_shape=[SDS((S, HG_WIDTH), BF), SDS((HG_HEADS, NC, LANE, LANE), F32)],
        scratch_shapes=[pltpu.VMEM((HG_HEADS, LANE, LANE), F32)], name="hg_fwd")(*([proj] * 8), lb, nw)


def _hg_bwd(proj, lb, nw, states, do_hg, dproj):
    S = proj.shape[0]
    C = HG_CHUNK
    NC = S // C

    def body(*refs):
        sec = refs[:8]
        lb_ref, nw_ref, st_ref, do_ref, dp_in, dp_ref, dlb_ref, dnw_ref, dst_scr, stage, sem = refs[8:]
        n = pl.program_id(0)
        slot = n % 2

        def chunk_copies(s):
            rows = pl.ds(pl.multiple_of((NC - 1 - n) * C, C), C)
            return [pltpu.make_async_copy(stage.at[s, t], dp_ref.at[rows, pl.ds((QH0 + 8 * t) * LANE, HG_WIDTH)], sem.at[s, t])
                    for t in range(4)]

        @pl.when(n == 0)
        def _():
            dst_scr[...] = jnp.zeros_like(dst_scr)
            dlb_ref[...] = jnp.zeros_like(dlb_ref)
            dnw_ref[...] = jnp.zeros_like(dnw_ref)

        _, vjp = jax.vjp(_hg_chunk_all, *_hg_sections(sec), lb_ref[...], nw_ref[...], st_ref[:, 0])
        dq, df, di, dg, dlb, dnw, dst = vjp((do_ref[...].astype(F32), dst_scr[...]))
        dlb_ref[...] += dlb
        dnw_ref[...] += dnw
        dst_scr[...] = dst

        @pl.when(n >= 2)
        def _():
            for cp in chunk_copies(slot):
                cp.wait()

        for t, val in enumerate((dq, df, di, dg)):
            stage[slot, t] = val.astype(BF)
        for cp in chunk_copies(slot):
            cp.start()

        @pl.when(n == NC - 1)
        def _():
            for cp in chunk_copies(1 - slot) + chunk_copies(slot):
                cp.wait()

    rev = lambda n: NC - 1 - n
    ospec = pl.BlockSpec((C, HG_WIDTH), lambda n: (rev(n), 0))
    lbs = pl.BlockSpec((2, HG_WIDTH), lambda n: (0, 0))
    nws = pl.BlockSpec((1, LANE), lambda n: (0, 0))
    outs = _pcall(
        body, grid=(NC,),
        in_specs=_hg_in_specs(rev) + [lbs, nws, pl.BlockSpec((HG_HEADS, 1, LANE, LANE), lambda n: (0, rev(n), 0, 0)), ospec, _ANY],
        out_specs=[_ANY, lbs, nws],
        out_shape=[SDS((S, IN_WIDTH), BF), SDS((2, HG_WIDTH), F32), SDS((1, LANE), F32)], aliases={12: 0},
        scratch_shapes=[pltpu.VMEM((HG_HEADS, LANE, LANE), F32), pltpu.VMEM((2, 4, C, HG_WIDTH), BF), pltpu.SemaphoreType.DMA((2, 4))],
        name="hg_bwd")(*([proj] * 8), lb, nw, states, do_hg, dproj)
    return outs


def _merge(ga, gb, ba, bb):
    return jax.nn.sigmoid(ga) * ba + jax.nn.sigmoid(gb) * bb


_MT = 512


def _swiglu(a, b):
    return a * jax.nn.sigmoid(a) * b


def _cross(q, k, v):
    s = _bdot(q, k, ((1,), (1,))) * (HEAD_DIM ** -0.5)
    e = jnp.exp(s - jnp.max(s, axis=-1, keepdims=True))
    p = e / jnp.sum(e, axis=-1, keepdims=True)
    return _bdot(p, v, ((1,), (0,)))


def _cross_specs():
    q = pl.BlockSpec((_MT, LANE), lambda h, i: (i, h))
    k = pl.BlockSpec((MEM_LEN, LANE), lambda h, i: (0, h))
    v = pl.BlockSpec((MEM_LEN, LANE), lambda h, i: (0, CROSS_HEADS + h))
    return q, k, v


def _cross_fwd(qc, kvc):
    S = qc.shape[0]
    q, k, v = _cross_specs()

    def body(q_ref, k_ref, v_ref, o_ref):
        o_ref[...] = _cross(q_ref[...], k_ref[...], v_ref[...]).astype(o_ref.dtype)

    return _pcall(body, grid=(CROSS_HEADS, S // _MT), in_specs=[q, k, v], out_specs=q,
                          out_shape=SDS((S, CROSS_WIDTH), BF), name="cross_fwd")(qc, kvc, kvc)


def _cross_bwd(qc, kvc, doc):
    S = qc.shape[0]
    q, k, v = _cross_specs()

    def body(q_ref, k_ref, v_ref, do_ref, dq_ref, dk_ref, dv_ref):
        @pl.when(pl.program_id(1) == 0)
        def _():
            dk_ref[...] = jnp.zeros_like(dk_ref)
            dv_ref[...] = jnp.zeros_like(dv_ref)

        _, vjp = jax.vjp(_cross, q_ref[...], k_ref[...], v_ref[...])
        dq, dk, dv = vjp(do_ref[...])
        dq_ref[...] = dq.astype(dq_ref.dtype)
        dk_ref[...] += dk
        dv_ref[...] += dv

    return _pcall(body, grid=(CROSS_HEADS, S // _MT), in_specs=[q, k, v, q], out_specs=[q, k, k],
                          out_shape=[SDS((S, CROSS_WIDTH), BF), SDS((MEM_LEN, CROSS_WIDTH), F32), SDS((MEM_LEN, CROSS_WIDTH), F32)],
                          name="cross_bwd")(qc, kvc, kvc, doc)


def _local_step(x, mem, tgt, p, ready):
    h = _rms_fwd(x, p["ln_mix"], "rms_mix")
    proj = _mm(h, p["w_in_a"], "nt", BF, "mm_proj_a", a_part=(0, 2))
    proj = _mm(h, p["w_in_b"], "nt", BF, "mm_proj_b", a_part=(1, 2), res=proj)
    o_att, lse = _attn_fwd(proj)
    o_hg, states = _hg_fwd(proj, p["hg_lb"], p["hg_norm"])
    ba = _mm(o_att, p["w_ba"], "nn", BF, "mm_ba")
    gates = [(proj, GA0 * LANE // _MT), (proj, GB0 * LANE // _MT)]

    def f32s(*vals):
        return [v.astype(F32) for v in vals]

    bb, merged = _mm_fused(o_hg, p["w_bb"], "nn", "mm_bb_merge", gates + [ba], [BF, BF], tn=_MT,
                           epi=lambda d, ga, gb, bav: (d, _merge(*f32s(ga, gb, bav), d)))
    x1 = _mm(merged, p["w_out"], "nn", F32, "mm_out", res=x)
    hc = _rms_fwd(x1, p["ln_cross"], "rms_cross")
    qc = _mm(hc, p["wq"], "nn", F32, "mm_q")
    mn = _rms_fwd(mem, p["ln_mem"], "rms_mem")
    kvc = _mm(mn, p["wkv"], "nn", F32, "mm_kv")
    oc = _cross_fwd(qc, kvc)
    x2 = _mm(oc, p["wo"], "nn", F32, "mm_o", res=x1)
    hf = _rms_fwd(x2, p["ln_ffn"], "rms_ffn")
    a = _mm(hf, p["w1"], "nt", BF, "mm_w1")
    b, g = _mm_fused(hf, p["w3"], "nt", "mm_w3_swiglu", [a], [BF, BF], lambda d, av: (d, _swiglu(av.astype(F32), d)))
    x3 = _mm(g, p["w2"], "nn", F32, "mm_w2", res=x2)
    loss, dx3, dx3b, d_ln_final = _final_loss(x3, p["ln_final"], tgt)

    gw, gs = {}, {"ln_final": d_ln_final}
    da, db = _mm_fused(dx3b, p["w2"], "nt", "mm_dg_swiglu", [a, b], [BF, BF],
                       lambda d, av, bv: jax.vjp(_swiglu, av.astype(F32), bv.astype(F32))[1](d))
    gw["w2"] = _mm(g, dx3b, "tn", BF, "mm_dw2")
    dhf = _mm(da, p["w1"], "nn", F32, "mm_dhf1")
    dhf = _mm(db, p["w3"], "nn", BF, "mm_dhf3", res=dhf)
    gw["w1"] = _mm(da, hf, "tn", BF, "mm_dw1")
    gw["w3"] = _mm(db, hf, "tn", BF, "mm_dw3")
    ready("ffn", gw)
    dx2, dx2b, gs["ln_ffn"] = _rms_bwd(x2, p["ln_ffn"], dhf, dx3, "rms_ffn_bwd")
    doc = _mm(dx2b, p["wo"], "nt", F32, "mm_doc")
    gw["wo"] = _mm(oc, dx2b, "tn", BF, "mm_dwo")
    dqc, dk, dv = _cross_bwd(qc, kvc, doc)
    dkvc = jnp.concatenate([dk, dv], axis=1)
    dhc = _mm(dqc, p["wq"], "nt", BF, "mm_dhc")
    gw["wq"] = _mm(hc, dqc, "tn", BF, "mm_dwq")
    dmn = _mm(dkvc, p["wkv"], "nt", F32, "mm_dmn")
    gw["wkv"] = _mm(mn, dkvc, "tn", BF, "mm_dwkv")
    ready("cross", gw)
    gs["ln_mem"] = _rms_bwd(mem, p["ln_mem"], dmn, None, "rms_mem_bwd")
    dx1, dx1b, gs["ln_cross"] = _rms_bwd(x1, p["ln_cross"], dhc, dx2, "rms_cross_bwd")
    def merge_back(d, ga, gb, bav, bbv):
        dga, dgb, dba_, dbb_ = jax.vjp(_merge, *f32s(ga, gb, bav, bbv))[1](d)
        return dba_, dbb_, dga, dgb

    dba, dbb, dproj = _mm_fused(dx1b, p["w_out"], "nt", "mm_dmerged_merge", gates + [ba, bb], [BF, BF], merge_back, tn=_MT,
                                scatter=(SDS((x.shape[0], IN_WIDTH), BF), [GA0 * LANE, GB0 * LANE]))
    gw["w_out"] = _mm(merged, dx1b, "tn", BF, "mm_dwout")
    do_att = _mm(dba, p["w_ba"], "nt", F32, "mm_doatt")
    gw["w_ba"] = _mm(o_att, dba, "tn", BF, "mm_dwba")
    do_hg = _mm(dbb, p["w_bb"], "nt", F32, "mm_dohg")
    gw["w_bb"] = _mm(o_hg, dbb, "tn", BF, "mm_dwbb")
    ready("mix", gw)
    dproj = _attn_bwd(proj, o_att, lse, do_att, dproj)
    dproj, gs["hg_lb"], gs["hg_norm"] = _hg_bwd(proj, p["hg_lb"], p["hg_norm"], states, do_hg, dproj)
    for q in range(IN_PARTS):
        gw[f"w_in_q{q}"] = _mm(dproj, h, "tn", BF, f"mm_dwin_q{q}", b_part=(q, IN_PARTS))
        ready(f"in_q{q}", gw)
        if q == 0:
            ready("mixers_done", gw)
    dh_a = _mm(dproj, p["w_in_a"], "nn", BF, "mm_dh_a")
    ready("dh_half", gw)
    dh = (dh_a, _mm(dproj, p["w_in_b"], "nn", BF, "mm_dh_b"))
    grad_x, _, gs["ln_mix"] = _rms_bwd(x, p["ln_mix"], dh, dx1, "rms_mix_bwd")
    return loss, grad_x, gs


_MATS = (
    ("w_in_a", "w_in", IN_WIDTH, D_MODEL // 2, "row", IN_WIDTH // N_DEV, True),
    ("w_in_b", "w_in", IN_WIDTH, D_MODEL // 2, "row", IN_WIDTH // N_DEV, True),
    ("w_ba", "w_branch_a", ATT_OUT, D_MODEL, "col", D_MODEL // N_DEV, False),
    ("w_bb", "w_branch_b", HG_WIDTH, D_MODEL, "col", D_MODEL // N_DEV, False),
    ("w_out", "w_out", D_MODEL, D_MODEL, "row", D_MODEL // N_DEV, False),
    ("wq", "wq_cross", D_MODEL, CROSS_WIDTH, "row", D_MODEL // N_DEV, False),
    ("wkv", "wkv_cross", D_MODEL, 2 * CROSS_WIDTH, "row", D_MODEL // N_DEV, False),
    ("wo", "wo_cross", CROSS_WIDTH, D_MODEL, "col", D_MODEL // N_DEV, False),
    ("w1", "w1", D_FF, D_MODEL, "row", D_FF // N_DEV, True),
    ("w3", "w3", D_FF, D_MODEL, "row", D_FF // N_DEV, True),
    ("w2", "w2", D_FF, D_MODEL, "row", D_FF // N_DEV, False),
    ("w_in_q0", "w_in", IN_WIDTH, D_MODEL // 2, "row", IN_WIDTH // N_DEV, True),
    ("w_in_q1", "w_in", IN_WIDTH, D_MODEL // 2, "row", IN_WIDTH // N_DEV, True),
)
IN_PARTS = 2
_INDEX = {m[0]: i for i, m in enumerate(_MATS)}


def _unit_shape(i):
    _, _, K, N, kind, sh, _ = _MATS[i]
    return (sh, N) if kind == "row" else (K, sh)


def _win(ref, i, chip, par, half=None):
    _, _, K, _, kind, sh, _ = _MATS[i]
    if kind == "row":
        rows, start = (sh, 0) if half is None else (sh // 2, half * (sh // 2))
        return ref.at[pl.ds(pl.multiple_of((2 * chip + par) * sh + start, 16), rows), :]
    rows, start = (K, 0) if half is None else (K // 2, half * (K // 2))
    return ref.at[pl.ds(start, rows), pl.ds(pl.multiple_of((2 * chip + par) * sh, LANE), sh)]


def _my_place():
    x, y, c = lax.axis_index("x"), lax.axis_index("y"), lax.axis_index("c")
    chips = [(1 - x, y), (x, 1 - y), (1 - x, 1 - y)]
    return x, y, c, chips


def _cast_unit(w, name, half=None):
    K, N = w.shape
    if half is not None:
        N = N // 2
    tr = _pick(K, (256, 400, 352))

    def body(w_ref, o_ref):
        o_ref[...] = w_ref[...].astype(BF)

    src = pl.BlockSpec((tr, N), lambda i: (i, half or 0))
    dst = pl.BlockSpec((tr, N), lambda i: (i, 0))
    return _pcall(body, grid=(K // tr,), in_specs=[src], out_specs=dst, out_shape=SDS((K, N), BF), name=name)(w)


def _handshake(peers):
    barrier = pltpu.get_barrier_semaphore()
    for peer in peers:
        pl.semaphore_signal(barrier, inc=1, device_id=peer, device_id_type=MESH)
    pl.semaphore_wait(barrier, len(peers))


def _sequencer(body, out_type, sems, cid, name):
    return pl.kernel(body, out_type=out_type, mesh=plsc.ScalarSubcoreMesh(axis_name="seq", num_cores=1),
                     scratch_types=sems, compiler_params=pltpu.CompilerParams(collective_id=cid), name=name)


def _all_gather(units, idx, cid, name):
    n = len(idx)

    def body(*refs):
        us, fulls = refs[:n], refs[n:2 * n]
        ssem, rsem, lsem = refs[2 * n:]
        x, y, c, _ = _my_place()
        xn, yn, dg = (1 - x, y), (x, 1 - y), (1 - x, 1 - y)
        me, sib = (x, y, c), (x, y, 1 - c)
        _handshake([sib, (*xn, c), (*yn, c)])

        def win(i, chip_xy, par, half=None):
            return _win(fulls[i], idx[i], 2 * chip_xy[0] + chip_xy[1], par, half)

        def rcopy(i, k, src, dst, to):
            return pltpu.make_async_remote_copy(src_ref=src, dst_ref=dst, send_sem=ssem.at[i, k], recv_sem=rsem.at[i, k],
                                                device_id=to, device_id_type=MESH)

        def landed(i, k, blk):
            rcopy(i, k, blk, blk, me).wait_recv()

        sends, locs = [], []

        def send(i, k, blk, to, src=None):
            cp = rcopy(i, k, blk if src is None else src, blk, to)
            cp.start()
            sends.append(cp)

        for i in range(n):
            mine = win(i, (x, y), c)
            loc = pltpu.make_async_copy(us[i], mine, lsem.at[i])
            loc.start()
            locs.append(loc)
            send(i, 0, mine, sib, src=us[i])
            send(i, 1, mine, (*xn, c), src=us[i])
            send(i, 2, mine, (*yn, c), src=us[i])
        for i in range(n):
            landed(i, 1, win(i, xn, c))
            send(i, 3, win(i, xn, c, 0), (*yn, c))
            send(i, 5, win(i, xn, c), sib)
            landed(i, 2, win(i, yn, c))
            send(i, 4, win(i, yn, c, 1), (*xn, c))
            send(i, 6, win(i, yn, c), sib)
        for i in range(n):
            landed(i, 3, win(i, dg, c, 0))
            landed(i, 4, win(i, dg, c, 1))
            send(i, 7, win(i, dg, c), sib)
        for i in range(n):
            landed(i, 0, win(i, (x, y), 1 - c))
            landed(i, 5, win(i, xn, 1 - c))
            landed(i, 6, win(i, yn, 1 - c))
            landed(i, 7, win(i, dg, 1 - c))
        for cp in sends:
            cp.wait_send()
        for loc in locs:
            loc.wait()

    out_type = [SDS((_MATS[i][2], _MATS[i][3]), BF) for i in idx]
    sems = [pltpu.SemaphoreType.DMA((n, 8)), pltpu.SemaphoreType.DMA((n, 8)), pltpu.SemaphoreType.DMA((n,))]
    return _sequencer(body, out_type, sems, cid, name)(*units)


def _rs_sibling(grads, idx, cid, name):
    n = len(idx)

    def body(*refs):
        gs, bufs = refs[:n], refs[n:2 * n]
        ssem, rsem = refs[2 * n:]
        x, y, c, _ = _my_place()
        _handshake([(x, y, 1 - c)])
        cps = []
        for i in range(n):
            for k in range(4):
                cp = pltpu.make_async_remote_copy(
                    src_ref=_win(gs[i], idx[i], k, 1 - c), dst_ref=bufs[i].at[k], send_sem=ssem.at[i, k], recv_sem=rsem.at[i, k],
                    device_id=(x, y, 1 - c), device_id_type=MESH)
                cp.start()
                cps.append(cp)
        for cp in cps:
            cp.wait()

    out_type = [SDS((4, *_unit_shape(i)), BF) for i in idx]
    sems = [pltpu.SemaphoreType.DMA((n, 4)), pltpu.SemaphoreType.DMA((n, 4))]
    return _sequencer(body, out_type, sems, cid, name)(*grads)


def _sibling_sum_tc(grad, i, cid, name):
    unit = _unit_shape(i)

    def body(g_ref, t_ref, theirs, mine, summed, ssem, rsem, lsem, osem):
        x, y, c, _ = _my_place()
        _handshake([(x, y, 1 - c)])
        sends, loads = [], []
        for k in range(4):
            sends.append(pltpu.make_async_remote_copy(
                src_ref=_win(g_ref, i, k, 1 - c), dst_ref=theirs.at[k], send_sem=ssem.at[k], recv_sem=rsem.at[k],
                device_id=(x, y, 1 - c), device_id_type=MESH))
            loads.append(pltpu.make_async_copy(_win(g_ref, i, k, c), mine.at[k], lsem.at[k]))
            sends[-1].start()
            loads[-1].start()
        stores = []
        for k in range(4):
            loads[k].wait()
            sends[k].wait_recv()
            summed[k] = (mine[k].astype(F32) + theirs[k].astype(F32)).astype(BF)
            stores.append(pltpu.make_async_copy(summed.at[k], t_ref.at[k], osem.at[k]))
            stores[-1].start()
        for k in range(4):
            sends[k].wait_send()
            stores[k].wait()

    block = pltpu.VMEM((4, *unit), BF)
    return _pcall(body, in_specs=[_ANY], out_specs=_ANY, out_shape=SDS((4, *unit), BF),
                  scratch_shapes=[block, block, block] + [pltpu.SemaphoreType.DMA((4,))] * 4,
                  compiler_params=_cp(VMEM_BIG, collective_id=cid), name=name)(grad)


def _chip_sum(g, sib, i, place, name):
    _, _, K, N, kind, sh, _ = _MATS[i]

    def body(pref, g_ref, s_ref, o_ref):
        o_ref[...] = (g_ref[...].astype(F32) + s_ref[...].astype(F32)).astype(BF)

    if kind == "row":
        tr = _pick(sh, (800, 352, 256))
        per = sh // tr
        grid = (4, per)
        g_spec = pl.BlockSpec((tr, N), lambda k, r, pref: ((2 * k + pref[0]) * per + r, 0))
        u_spec = pl.BlockSpec((None, tr, N), lambda k, r, pref: (k, r, 0))
    else:
        grid = (4,)
        g_spec = pl.BlockSpec((K, sh), lambda k, pref: (0, 2 * k + pref[0]))
        u_spec = pl.BlockSpec((None, K, sh), lambda k, pref: (k, 0, 0))
    return _pcall(body, grid=grid, in_specs=[g_spec, u_spec], out_specs=u_spec, out_shape=SDS((4, *_unit_shape(i)), BF),
                  prefetch=True, name=name)(place, g, sib)


def _rs_chips(parts, idx, cid, name):
    n = len(idx)

    def body(*refs):
        ts, bufs, relays = refs[:n], refs[n:2 * n], refs[2 * n:3 * n]
        ssem, rsem = refs[3 * n:]
        x, y, c, _ = _my_place()
        xn, yn, dg = (1 - x, y), (x, 1 - y), (1 - x, 1 - y)
        to_x, to_y, me = (*xn, c), (*yn, c), (x, y, c)
        _handshake([to_x, to_y])

        def rcopy(i, k, src, dst, to):
            return pltpu.make_async_remote_copy(src_ref=src, dst_ref=dst, send_sem=ssem.at[i, k], recv_sem=rsem.at[i, k],
                                                device_id=to, device_id_type=MESH)

        def halves(i):
            rows = _unit_shape(idx[i])[0] // 2
            return [pl.ds(0, rows), pl.ds(rows, rows)]

        sends = []

        def send(cp):
            cp.start()
            sends.append(cp)

        for i in range(n):
            h0, h1 = halves(i)
            diag = ts[i].at[2 * dg[0] + dg[1]]
            send(rcopy(i, 0, ts[i].at[2 * xn[0] + xn[1]], bufs[i].at[0], to_x))
            send(rcopy(i, 1, ts[i].at[2 * yn[0] + yn[1]], bufs[i].at[1], to_y))
            send(rcopy(i, 2, diag.at[h0], relays[i].at[0], to_x))
            send(rcopy(i, 3, diag.at[h1], relays[i].at[1], to_y))
        for i in range(n):
            h0, h1 = halves(i)
            rcopy(i, 2, relays[i].at[0], relays[i].at[0], me).wait_recv()
            send(rcopy(i, 4, relays[i].at[0], bufs[i].at[2, h0], to_y))
            rcopy(i, 3, relays[i].at[1], relays[i].at[1], me).wait_recv()
            send(rcopy(i, 5, relays[i].at[1], bufs[i].at[2, h1], to_x))
        for i in range(n):
            h0, h1 = halves(i)
            rcopy(i, 0, bufs[i].at[0], bufs[i].at[0], me).wait_recv()
            rcopy(i, 1, bufs[i].at[1], bufs[i].at[1], me).wait_recv()
            rcopy(i, 4, bufs[i].at[2, h0], bufs[i].at[2, h0], me).wait_recv()
            rcopy(i, 5, bufs[i].at[2, h1], bufs[i].at[2, h1], me).wait_recv()
        for cp in sends:
            cp.wait_send()

    def relay_shape(i):
        rows, cols = _unit_shape(i)
        return (2, rows // 2, cols)

    out_type = [SDS((3, *_unit_shape(i)), BF) for i in idx] + [SDS(relay_shape(i), BF) for i in idx]
    sems = [pltpu.SemaphoreType.DMA((n, 6)), pltpu.SemaphoreType.DMA((n, 6))]
    return _sequencer(body, out_type, sems, cid, name)(*parts)[:n]


def _adamw(w, g, m, v):
    m = ADAM_B1 * m + (1.0 - ADAM_B1) * g
    v = ADAM_B2 * v + (1.0 - ADAM_B2) * jnp.square(g)
    m_hat = m / (1.0 - ADAM_B1 ** ADAM_STEP)
    v_hat = v / (1.0 - ADAM_B2 ** ADAM_STEP)
    delta = -ADAM_LR * (m_hat / (jnp.sqrt(v_hat) + ADAM_EPS) + ADAM_WD * w)
    return delta, m, v


def _adam_mat(parts, recvs, w, m, v, place, name):
    R, C = w.shape
    n = len(parts)
    tr = _pick(R, (256, 160, 176))

    def body(pref, *refs):
        t_refs, r_refs = refs[:n], refs[n:2 * n]
        w_ref, m_ref, v_ref, g_out, d_out, m_out, v_out = refs[2 * n:]
        cols = [t[...].astype(F32) + r[0].astype(F32) + r[1].astype(F32) + r[2].astype(F32) for t, r in zip(t_refs, r_refs)]
        g = jnp.concatenate(cols, axis=1) if n > 1 else cols[0]
        d, mn, vn = _adamw(w_ref[...], g, m_ref[...], v_ref[...])
        g_out[...] = g
        d_out[...] = d
        m_out[...] = mn
        v_out[...] = vn

    t_spec = pl.BlockSpec((None, tr, C // n), lambda i, pref: (pref[1], i, 0))
    r_spec = pl.BlockSpec((3, tr, C // n), lambda i, pref: (0, i, 0))
    s_spec = pl.BlockSpec((tr, C), lambda i, pref: (i, 0))
    return _pcall(body, grid=(R // tr,), in_specs=[t_spec] * n + [r_spec] * n + [s_spec] * 3, out_specs=[s_spec] * 4,
                  out_shape=[SDS((R, C), F32)] * 4, prefetch=True, name=name)(place, *parts, *recvs, w, m, v)


_SMALL = (("ln_mix", "ln_mix_w", D_MODEL), ("ln_cross", "ln_cross_w", D_MODEL), ("ln_mem", "ln_mem_w", D_MODEL),
          ("ln_ffn", "ln_ffn_w", D_MODEL), ("ln_final", "ln_final_w", D_MODEL), ("hg_lb", "hg_lower_bounds", 2 * HG_WIDTH),
          ("hg_norm", "hg_norm_w", HEAD_DIM))
_PACK = sum(n for _, _, n in _SMALL) + LANE


def _small_sync(grow, wrow, mrow, vrow):
    def body(g_ref, w_ref, m_ref, v_ref, o_ref, gath, ssem, rsem):
        x, y, c, _ = _my_place()
        me = 4 * x + 2 * y + c
        gath[pl.ds(me, 1), :] = g_ref[...]
        cps = []
        for dlt in range(1, N_DEV):
            peer = (me + dlt) % N_DEV
            cp = pltpu.make_async_remote_copy(
                src_ref=g_ref, dst_ref=gath.at[pl.ds(me, 1), :], send_sem=ssem.at[peer], recv_sem=rsem.at[me],
                device_id=(peer // 4, (peer // 2) % 2, peer % 2), device_id_type=MESH)
            cp.start()
            cps.append(cp)
        for dlt in range(1, N_DEV):
            src = (me + dlt) % N_DEV
            pltpu.make_async_remote_copy(
                src_ref=g_ref, dst_ref=gath.at[pl.ds(src, 1), :], send_sem=ssem.at[src], recv_sem=rsem.at[src],
                device_id=(x, y, c), device_id_type=MESH).wait_recv()
        for cp in cps:
            cp.wait_send()
        g = gath[0:1, :]
        for j in range(1, N_DEV):
            g = g + gath[j:j + 1, :]
        d, mn, vn = _adamw(w_ref[...], g, m_ref[...], v_ref[...])
        o_ref[0:1, :] = g
        o_ref[1:2, :] = d
        o_ref[2:3, :] = mn
        o_ref[3:4, :] = vn

    vm = pl.BlockSpec(memory_space=pltpu.VMEM)
    return _pcall(
        body, in_specs=[vm] * 4, out_specs=vm, out_shape=SDS((4, _PACK), F32),
        scratch_shapes=[pltpu.VMEM((N_DEV, _PACK), F32), pltpu.SemaphoreType.DMA((N_DEV,)), pltpu.SemaphoreType.DMA((N_DEV,))],
        name="small_sync")(grow, wrow, mrow, vrow)


def _pack_small(d, loss_row=None):
    parts = [d[k].reshape(1, n).astype(F32) for k, _, n in _SMALL]
    parts.append(loss_row if loss_row is not None else jnp.ones((1, LANE), F32))
    return jnp.concatenate(parts, axis=1)


def kernel(x, mem, ln_mix_w, w_in, hg_norm_w, hg_lower_bounds, w_branch_a, w_branch_b, w_out, ln_cross_w, ln_mem_w, wq_cross, wkv_cross, wo_cross, ln_ffn_w, w1, w3, w2, ln_final_w, loss_target, m_ln_mix_w, m_w_in, m_hg_norm_w, m_hg_lower_bounds, m_w_branch_a, m_w_branch_b, m_w_out, m_ln_cross_w, m_ln_mem_w, m_wq_cross, m_wkv_cross, m_wo_cross, m_ln_ffn_w, m_w1, m_w3, m_w2, m_ln_final_w, v_ln_mix_w, v_w_in, v_hg_norm_w, v_hg_lower_bounds, v_w_branch_a, v_w_branch_b, v_w_out, v_ln_cross_w, v_ln_mem_w, v_wq_cross, v_wkv_cross, v_wo_cross, v_ln_ffn_w, v_w1, v_w3, v_w2, v_ln_final_w):
    given = dict(locals())
    place = jnp.stack([lax.axis_index("c"), 2 * lax.axis_index("x") + lax.axis_index("y")]).astype(jnp.int32)

    _ORDER[0] = None
    index = _INDEX

    def local_view(a, i):
        return a[0].T if _MATS[i][6] else a[0]

    shard = {key: local_view(given[ref], i) for i, (key, ref, *_) in enumerate(_MATS)}

    p = {}
    col_half = {"w_in_a": 0, "w_in_b": 1}
    waves = (("in_a", ["w_in_a"], 1), ("in_b", ["w_in_b"], 12), ("mid", ["w_ba", "w_bb", "w_out", "wq", "wkv", "wo"], 2),
             ("w1", ["w1"], 11), ("w3", ["w3"], 15), ("w2", ["w2"], 16))
    for wave, keys, cid in waves:
        units = [_cast_unit(shard[k], "cast_" + k, col_half.get(k)) for k in keys]
        p.update(zip(keys, _all_gather(units, [index[k] for k in keys], cid, "all_gather_" + wave)))
    small_w = {"ln_mix": ln_mix_w, "ln_cross": ln_cross_w, "ln_mem": ln_mem_w, "ln_ffn": ln_ffn_w,
               "ln_final": ln_final_w.reshape(1, D_MODEL), "hg_lb": hg_lower_bounds, "hg_norm": hg_norm_w}
    p.update(small_w)

    groups = {"ffn": ["w2", "w1", "w3"], "cross": ["wo", "wq", "wkv"], "mix": ["w_out", "w_ba", "w_bb"]}
    cids = {"ffn": (3, 4), "cross": (5, 6), "mix": (7, 8)}
    for q in range(IN_PARTS):
        groups[f"in_q{q}"] = [f"w_in_q{q}"]
        cids[f"in_q{q}"] = ((9, 10), (13, 14), (17, 18), (19, 20))[q]
    sib, parts, recv, out = {}, {}, {}, {}

    grads = {}

    def chip_stage(name):
        keys = groups[name]
        parts[name] = [_chip_sum(grads[k], s, index[k], place, "chip_sum_" + k) for k, s in zip(keys, sib[name])]
        recv[name] = list(_rs_chips(parts[name], [index[k] for k in keys], cids[name][1], "rs_chips_" + name))

    def adam(ref, i, part_list, recv_list):
        res = _adam_mat(part_list, recv_list, shard[_MATS[i][0]], local_view(given["m_" + ref], i), local_view(given["v_" + ref], i),
                        place, "adam_" + ref)
        out[ref] = [(a.T if _MATS[i][6] else a)[None] for a in res]

    def adam_stage(name):
        for k, t, r in zip(groups[name], parts[name], recv[name]):
            adam(_MATS[index[k]][1], index[k], [t], [r])

    def ready(name, gw):
        if name == "mixers_done":
            adam_stage("ffn")
            chip_stage("mix")
            adam_stage("cross")
            return
        if name == "dh_half":
            return
        keys = groups[name]
        grads.update({k: gw[k] for k in keys})
        if name.startswith("in_q"):
            if name == "in_q1":
                adam_stage("mix")
            parts[name] = [_sibling_sum_tc(gw[k], index[k], cids[name][0], "sibling_sum_" + k) for k in keys]
            recv[name] = list(_rs_chips(parts[name], [index[k] for k in keys], cids[name][1], "rs_chips_" + name))
            return
        sib[name] = _rs_sibling([gw[k] for k in keys], [index[k] for k in keys], cids[name][0], "rs_sibling_" + name)
        if name == "cross":
            chip_stage("ffn")
        if name == "mix":
            chip_stage("cross")

    loss_row, grad_x, gs = _local_step(x[0], mem[0], loss_target[0], p, ready)

    sm = {k: given["m_" + ref] for k, ref, _ in _SMALL}
    sv = {k: given["v_" + ref] for k, ref, _ in _SMALL}
    synced = _small_sync(_pack_small(gs, loss_row), _pack_small(small_w), _pack_small(sm), _pack_small(sv))
    in_groups = [f"in_q{q}" for q in range(IN_PARTS)]
    adam("w_in", index["w_in_a"], sum((parts[g] for g in in_groups), []), sum((recv[g] for g in in_groups), []))
    off = 0
    for k, ref, n in _SMALL:
        shape = given[ref].shape
        out[ref] = [synced[r, off:off + n].reshape(shape) for r in range(4)]
        off += n
    loss = synced[0, off]

    order = ["ln_mix_w", "w_in", "hg_norm_w", "hg_lower_bounds", "w_branch_a", "w_branch_b", "w_out", "ln_cross_w", "ln_mem_w",
             "wq_cross", "wkv_cross", "wo_cross", "ln_ffn_w", "w1", "w3", "w2", "ln_final_w"]
    return (loss, grad_x[None], *[out[n][0] for n in order], *[out[n][1] for n in order],
            *[out[n][2] for n in order], *[out[n][3] for n in order])
```

```python
import functools
import math

import jax
import jax.numpy as jnp
from jax import lax
from jax.experimental import pallas as pl
from jax.experimental.pallas import tpu as pltpu
from jax.experimental.pallas import tpu_sc as plsc

F32 = jnp.float32
BF = jnp.bfloat16
SDS = jax.ShapeDtypeStruct

D_MODEL = 2048
SEQ = 2048
HEAD_DIM = 128
MEM_LEN = 256
ATT_GROUPS = ((128, 1), (512, 4), (2048, 16))
ATT_HEADS = 4
ATT_WIDTH = 1536
ATT_OUT = 512
HG_HEADS = 8
HG_WIDTH = 1024
HG_CHUNK = 64
IN_WIDTH = 12800
CROSS_HEADS = 4
CROSS_WIDTH = 512
D_FF = 5632
RMS_EPS = 1e-6
ADAM_LR = 0.001
ADAM_B1 = 0.9
ADAM_B2 = 0.999
ADAM_EPS = 1e-08
ADAM_WD = 0.01
ADAM_STEP = 10
N_DEV = 8

LANE = 128
QA0, KA0, VA0 = 0, 12, 24
QH0, FH0, IH0, GH0 = 36, 44, 52, 60
GA0, GB0 = 68, 84

VMEM_BIG = 56 * 1024 * 1024
MESH = pl.DeviceIdType.MESH


def _cp(vmem=None, **kw):
    if vmem is not None:
        kw["vmem_limit_bytes"] = vmem
    return pltpu.CompilerParams(**kw)


def _pick(n, cands):
    for c in cands:
        if n % c == 0:
            return c
    raise ValueError(f"no tile for {n}")


_ANY = pl.BlockSpec(memory_space=pl.ANY)
_ORDER = [None]
_ALSO_AFTER = []


def _pcall(body, *, name, in_specs, out_specs, out_shape, grid=(), scratch_shapes=(), compiler_params=None, prefetch=False,
           aliases=None):
    def run(*args):
        dep = _ORDER[0]
        specs, operands, kernel_fn = list(in_specs), list(args), body
        deps = [] if dep is None or any(dep is a for a in operands) else [dep]
        deps += _ALSO_AFTER
        del _ALSO_AFTER[:]
        if deps:
            at, nd = len(operands), len(deps)
            specs += [_ANY] * nd
            operands += deps

            def kernel_fn(*refs):
                return body(*refs[:at], *refs[at + nd:])
        if prefetch:
            call = pl.pallas_call(
                kernel_fn, out_shape=out_shape, name=name, compiler_params=compiler_params,
                grid_spec=pltpu.PrefetchScalarGridSpec(num_scalar_prefetch=1, grid=grid, in_specs=specs, out_specs=out_specs,
                                                       scratch_shapes=scratch_shapes))
        else:
            call = pl.pallas_call(kernel_fn, grid=grid, in_specs=specs, out_specs=out_specs, out_shape=out_shape,
                                  scratch_shapes=scratch_shapes, compiler_params=compiler_params, name=name,
                                  input_output_aliases=aliases or {})
        out = call(*operands)
        _ORDER[0] = out[0] if isinstance(out, (list, tuple)) else out
        return out
    return run


def _mm(a, b, form, out_dtype, name, res=None, tiles=None, a_part=(0, 1), b_part=(0, 1)):
    a_shape = (a.shape[0], a.shape[1] // a_part[1])
    b_shape = (b.shape[0], b.shape[1] // b_part[1])
    if form == "nn":
        (M, K), (K2, N) = a_shape, b_shape
    elif form == "nt":
        (M, K), (N, K2) = a_shape, b_shape
    else:
        (K, M), (K2, N) = a_shape, b_shape
    assert K == K2, (a.shape, b.shape, form)
    if tiles is None:
        tm = _pick(M, (1024, 1280, 1408, 512, 256))
        tn = _pick(N, (1280, 1024, 1408, 512, 256))
        tk = K if K <= 2048 else _pick(K, (D_FF, 2560))
        if tk == D_FF:
            tm = _pick(M, (512, 256))
    else:
        tm, tn, tk = tiles
    nk = K // tk
    ah, bh = a_part[0], b_part[0]
    if form == "nn":
        a_spec = pl.BlockSpec((tm, tk), lambda i, j, k: (i, k + ah * nk))
        b_spec = pl.BlockSpec((tk, tn), lambda i, j, k: (k, j + bh * (N // tn)))
        dims = ((1,), (0,))
    elif form == "nt":
        a_spec = pl.BlockSpec((tm, tk), lambda i, j, k: (i, k + ah * nk))
        b_spec = pl.BlockSpec((tn, tk), lambda i, j, k: (j, k + bh * nk))
        dims = ((1,), (1,))
    else:
        a_spec = pl.BlockSpec((tk, tm), lambda i, j, k: (k, i + ah * (M // tm)))
        b_spec = pl.BlockSpec((tk, tn), lambda i, j, k: (k, j + bh * (N // tn)))
        dims = ((0,), (0,))
    o_spec = pl.BlockSpec((tm, tn), lambda i, j, k: (i, j))
    in_specs = [a_spec, b_spec]
    args = [a, b]
    if res is not None:
        in_specs.append(o_spec)
        args.append(res)

    def body(*refs):
        a_ref, b_ref = refs[0], refs[1]
        r_ref = refs[2] if res is not None else None
        o_ref = refs[3] if res is not None else refs[2]
        acc = refs[-1] if nk > 1 else None
        k = pl.program_id(2)
        d = lax.dot_general(a_ref[...].astype(BF), b_ref[...].astype(BF), (dims, ((), ())), preferred_element_type=F32)

        def finish(r):
            if res is not None:
                r = r + r_ref[...].astype(F32)
            o_ref[...] = r.astype(o_ref.dtype)

        if nk == 1:
            finish(d)
        else:
            @pl.when(k == 0)
            def _():
                acc[...] = d

            @pl.when((k > 0) & (k < nk - 1))
            def _():
                acc[...] += d

            @pl.when(k == nk - 1)
            def _():
                finish(acc[...] + d)

    return _pcall(
        body, grid=(M // tm, N // tn, nk), in_specs=in_specs, out_specs=o_spec,
        out_shape=SDS((M, N), out_dtype), scratch_shapes=[pltpu.VMEM((tm, tn), F32)] if nk > 1 else [],
        compiler_params=_cp(VMEM_BIG, dimension_semantics=("parallel", "parallel", "arbitrary")),
        name=name)(*args)


_MXU_COLS = 256


def _mm_fused(a, b, form, name, extras, out_dtypes, epi, tn=None, scatter=None):
    M, K = a.shape
    N = b.shape[1] if form == "nn" else b.shape[0]
    assert K <= 2048 and K == (b.shape[0] if form == "nn" else b.shape[1])
    tm = _pick(M, (1024, 512, 256))
    tn = tn or _pick(N, (1280, 1024, 1408, 512, 256))
    ne, no = len(extras), len(out_dtypes)
    bases = scatter[1] if scatter else []
    dims = ((1,), (0,)) if form == "nn" else ((1,), (1,))

    def body(*refs):
        a_ref, b_ref, ex, outs = refs[0], refs[1], refs[2:2 + ne], refs[2 + ne:2 + ne + no]
        av = a_ref[...].astype(BF)
        if scatter:
            wide_ref, stage, sem = refs[2 + ne + no:]
        for c0 in range(0, tn, _MXU_COLS):
            cols = slice(c0, min(c0 + _MXU_COLS, tn))
            bc = b_ref[:, cols] if form == "nn" else b_ref[cols, :]
            d = lax.dot_general(av, bc.astype(BF), (dims, ((), ())), preferred_element_type=F32)
            res = epi(d, *[e[:, cols] for e in ex])
            for o_ref, r in zip(outs, res[:no]):
                o_ref[:, cols] = r.astype(o_ref.dtype)
            for s, r in enumerate(res[no:]):
                stage[s, :, cols] = r.astype(BF)
        if scatter:
            rows = pl.ds(pl.multiple_of(pl.program_id(0) * tm, tm), tm)
            cps = [pltpu.make_async_copy(stage.at[s], wide_ref.at[rows, pl.ds(pl.multiple_of(base + pl.program_id(1) * tn, LANE), tn)],
                                         sem.at[s]) for s, base in enumerate(bases)]
            for cp in cps:
                cp.start()
            for cp in cps:
                cp.wait()

    a_spec = pl.BlockSpec((tm, K), lambda i, j: (i, 0))
    b_spec = pl.BlockSpec((K, tn), lambda i, j: (0, j)) if form == "nn" else pl.BlockSpec((tn, K), lambda i, j: (j, 0))
    o_spec = pl.BlockSpec((tm, tn), lambda i, j: (i, j))
    ex_specs, ex_args = [], []
    for e in extras:
        arr, off = e if isinstance(e, tuple) else (e, 0)
        ex_specs.append(pl.BlockSpec((tm, tn), lambda i, j, off=off: (i, off + j)))
        ex_args.append(arr)
    out_specs, out_shape, scratch = [o_spec] * no, [SDS((M, N), dt) for dt in out_dtypes], []
    if scatter:
        out_specs, out_shape = out_specs + [_ANY], out_shape + [scatter[0]]
        scratch = [pltpu.VMEM((len(bases), tm, tn), BF), pltpu.SemaphoreType.DMA((len(bases),))]
    return _pcall(
        body, grid=(M // tm, N // tn), in_specs=[a_spec, b_spec] + ex_specs, out_specs=out_specs, out_shape=out_shape,
        scratch_shapes=scratch, compiler_params=_cp(VMEM_BIG, dimension_semantics=("parallel", "parallel")),
        name=name)(a, b, *ex_args)


def _rms(x, w):
    return x * lax.rsqrt(jnp.mean(x * x, axis=-1, keepdims=True) + RMS_EPS) * w


def _rms_fwd(x, w, name):
    R, D = x.shape
    tr = 256

    def body(x_ref, w_ref, o_ref):
        o_ref[...] = _rms(x_ref[...], w_ref[...]).astype(o_ref.dtype)

    return _pcall(
        body, grid=(R // tr,),
        in_specs=[pl.BlockSpec((tr, D), lambda i: (i, 0)), pl.BlockSpec((1, D), lambda i: (0, 0))],
        out_specs=pl.BlockSpec((tr, D), lambda i: (i, 0)), out_shape=SDS((R, D), BF), name=name)(x, w)


def _rms_bwd(x, w, dh, dres, name):
    R, D = x.shape
    tr = 256
    want_dx = dres is not None
    dhs = dh if isinstance(dh, (tuple, list)) else (dh,)
    nd = len(dhs)

    def body(*refs):
        x_ref, w_ref, dh_refs = refs[0], refs[1], refs[2:2 + nd]
        if want_dx:
            dr_ref, dx_ref, dxb_ref, dw_ref = refs[2 + nd:]
        else:
            dw_ref = refs[2 + nd]

        @pl.when(pl.program_id(0) == 0)
        def _():
            dw_ref[...] = jnp.zeros_like(dw_ref)

        dhv = jnp.concatenate([r[...].astype(F32) for r in dh_refs], axis=1) if nd > 1 else dh_refs[0][...].astype(F32)
        _, vjp = jax.vjp(_rms, x_ref[...], w_ref[...])
        dx, dw = vjp(dhv)
        dw_ref[...] += dw
        if want_dx:
            dx = dx + dr_ref[...]
            dx_ref[...] = dx
            dxb_ref[...] = dx.astype(BF)

    row = pl.BlockSpec((tr, D), lambda i: (i, 0))
    part = pl.BlockSpec((tr, D // nd), lambda i: (i, 0))
    vec = pl.BlockSpec((1, D), lambda i: (0, 0))
    if want_dx:
        return _pcall(body, grid=(R // tr,), in_specs=[row, vec] + [part] * nd + [row], out_specs=[row, row, vec],
                      out_shape=[SDS((R, D), F32), SDS((R, D), BF), SDS((1, D), F32)], name=name)(x, w, *dhs, dres)
    return _pcall(body, grid=(R // tr,), in_specs=[row, vec] + [part] * nd, out_specs=vec,
                  out_shape=SDS((1, D), F32), name=name)(x, w, *dhs)


def _final_loss(x3, w, tgt):
    R, D = x3.shape
    tr = 256

    def loss_fn(xv, wv, tv):
        err = _rms(xv, wv) - tv
        return 0.5 * jnp.sum(jnp.mean(err * err, axis=-1))

    def body(x_ref, w_ref, t_ref, loss_ref, dx_ref, dxb_ref, dw_ref):
        @pl.when(pl.program_id(0) == 0)
        def _():
            dw_ref[...] = jnp.zeros_like(dw_ref)
            loss_ref[...] = jnp.zeros_like(loss_ref)

        tv = t_ref[...]
        val, vjp = jax.vjp(lambda a, b: loss_fn(a, b, tv), x_ref[...], w_ref[...])
        dx, dw = vjp(jnp.ones((), F32))
        dx_ref[...] = dx
        dxb_ref[...] = dx.astype(BF)
        dw_ref[...] += dw
        loss_ref[...] += jnp.full(loss_ref.shape, val, F32)

    row = pl.BlockSpec((tr, D), lambda i: (i, 0))
    vec = pl.BlockSpec((1, D), lambda i: (0, 0))
    one = pl.BlockSpec((1, LANE), lambda i: (0, 0))
    return _pcall(body, grid=(R // tr,), in_specs=[row, vec, row], out_specs=[one, row, row, vec],
                          out_shape=[SDS((1, LANE), F32), SDS((R, D), F32), SDS((R, D), BF), SDS((1, D), F32)],
                          name="final_loss")(x3, w, tgt)


NEG = -1e30


def _att_blocks(S, d):
    L = S // d
    nb = L // LANE
    return nb, (2 if nb > 1 else 1)


def _att_rows(S, d, idx):
    nb, nkb = _att_blocks(S, d)
    r = idx // nb
    n = idx % nb
    kb = jnp.maximum(n - 1, 0)
    if d == 1:
        qrows = pl.ds(pl.multiple_of(n * LANE, LANE), LANE)
        krows = pl.ds(pl.multiple_of(kb * LANE, LANE), LANE * nkb)
    else:
        qrows = pl.ds(r + n * (LANE * d), LANE, stride=d)
        krows = pl.ds(r + kb * (LANE * d), LANE * nkb, stride=d)
    qpos = n * LANE + lax.broadcasted_iota(jnp.int32, (LANE, LANE * nkb), 0)
    kpos = kb * LANE + lax.broadcasted_iota(jnp.int32, (LANE, LANE * nkb), 1)
    return qrows, krows, qpos - kpos


def _slab(col):
    return pl.BlockSpec((SEQ, LANE), lambda h, col=col: (0, col + h))


def _widen(slabs, wide):
    for t, ref in enumerate(slabs):
        wide[t] = ref[...].astype(F32)
    return tuple(tuple(wide.at[3 * kind + g] for g in range(3)) for kind in range(3))


def _attn_fwd(proj):
    S = proj.shape[0]
    scale = HEAD_DIM ** -0.5

    def body(q0, q1, q2, k0, k1, k2, v0, v1, v2, o_ref, lse_ref, og, lg, wide):
        qs, ks, vs = _widen((q0, q1, q2, k0, k1, k2, v0, v1, v2), wide)
        for g, (window, d) in enumerate(ATT_GROUPS):
            def blk(idx, carry, g=g, window=window, d=d):
                qrows, krows, dist = _att_rows(S, d, idx)
                q = qs[g][qrows, :].astype(BF)
                k = ks[g][krows, :].astype(BF)
                v = vs[g][krows, :].astype(BF)
                s = lax.dot_general(q, k, (((1,), (1,)), ((), ())), preferred_element_type=F32) * scale
                s = jnp.where((dist >= 0) & (dist <= window // d), s, NEG)
                m = jnp.max(s, axis=-1, keepdims=True)
                p = jnp.exp(s - m)
                l = jnp.sum(p, axis=-1, keepdims=True)
                o = jnp.dot((p / l).astype(BF), v, preferred_element_type=F32)
                og[g, qrows, :] = o
                lg[g, qrows, :] = jnp.broadcast_to(m + jnp.log(l), (LANE, LANE))
                return carry
            lax.fori_loop(0, S // LANE, blk, 0, unroll=8)

        def merge(t, carry):
            rows = pl.ds(pl.multiple_of(t * 256, 256), 256)
            l0, l1, l2 = lg[0, rows, :], lg[1, rows, :], lg[2, rows, :]
            m = jnp.maximum(jnp.maximum(l0, l1), l2)
            w0, w1, w2 = jnp.exp(l0 - m), jnp.exp(l1 - m), jnp.exp(l2 - m)
            den = w0 + w1 + w2
            o_ref[rows, :] = (w0 * og[0, rows, :] + w1 * og[1, rows, :] + w2 * og[2, rows, :]) / den
            lse_ref[rows, :] = m + jnp.log(den)
            return carry
        lax.fori_loop(0, S // 256, merge, 0)

    ins = [_slab(QA0 + 4 * g) for g in range(3)] + [_slab(KA0 + 4 * g) for g in range(3)] + [_slab(VA0 + 4 * g) for g in range(3)]
    out = pl.BlockSpec((SEQ, LANE), lambda h: (0, h))
    return _pcall(
        body, grid=(ATT_HEADS,), in_specs=ins, out_specs=[out, out],
        out_shape=[SDS((S, ATT_OUT), F32), SDS((S, ATT_OUT), F32)],
        scratch_shapes=[pltpu.VMEM((3, S, LANE), F32), pltpu.VMEM((3, S, LANE), F32), pltpu.VMEM((9, S, LANE), F32)],
        compiler_params=_cp(VMEM_BIG), name="attn_fwd")(*([proj] * 9))


def _attn_bwd(proj, o_att, lse, do_att, dproj):
    S = proj.shape[0]
    scale = HEAD_DIM ** -0.5

    def body(q0, q1, q2, k0, k1, k2, v0, v1, v2, o_ref, lse_ref, do_ref, dp_in, dp_ref, dqa, dka, dva, delta, stage, sem, wide):
        qs, ks, vs = _widen((q0, q1, q2, k0, k1, k2, v0, v1, v2), wide)
        head = pl.program_id(0)

        def slab_copies(g):
            return [pltpu.make_async_copy(stage.at[t], dp_ref.at[:, pl.ds(pl.multiple_of((base + 4 * g + head) * LANE, LANE), LANE)],
                                          sem.at[t]) for t, base in enumerate((QA0, KA0, VA0))]

        delta[...] = jnp.broadcast_to(jnp.sum(do_ref[...] * o_ref[...], axis=-1, keepdims=True), delta.shape)
        for g, (window, d) in enumerate(ATT_GROUPS):
            dka[...] = jnp.zeros_like(dka)
            dva[...] = jnp.zeros_like(dva)

            def blk(idx, carry, g=g, window=window, d=d):
                qrows, krows, dist = _att_rows(S, d, idx)
                q = qs[g][qrows, :].astype(BF)
                k = ks[g][krows, :].astype(BF)
                v = vs[g][krows, :].astype(BF)
                do = do_ref[qrows, :]
                s = lax.dot_general(q, k, (((1,), (1,)), ((), ())), preferred_element_type=F32) * scale
                p = jnp.exp(s - lse_ref[qrows, :][:, 0:1])
                p = jnp.where((dist >= 0) & (dist <= window // d), p, 0.0)
                dob = do.astype(BF)
                dva[krows, :] += lax.dot_general(p.astype(BF), dob, (((0,), (0,)), ((), ())), preferred_element_type=F32)
                dp = lax.dot_general(dob, v, (((1,), (1,)), ((), ())), preferred_element_type=F32)
                ds = (p * (dp - delta[qrows, :][:, 0:1]) * scale).astype(BF)
                dqa[qrows, :] = jnp.dot(ds, k, preferred_element_type=F32)
                dka[krows, :] += lax.dot_general(ds, q, (((0,), (0,)), ((), ())), preferred_element_type=F32)
                return carry
            lax.fori_loop(0, S // LANE, blk, 0, unroll=8)
            if g > 0:
                for cp in slab_copies(g - 1):
                    cp.wait()
            stage[0] = dqa[...].astype(BF)
            stage[1] = dka[...].astype(BF)
            stage[2] = dva[...].astype(BF)
            for cp in slab_copies(g):
                cp.start()
        for cp in slab_copies(len(ATT_GROUPS) - 1):
            cp.wait()

    cols = [QA0 + 4 * g for g in range(3)] + [KA0 + 4 * g for g in range(3)] + [VA0 + 4 * g for g in range(3)]
    ins = [_slab(c) for c in cols]
    one = pl.BlockSpec((SEQ, LANE), lambda h: (0, h))
    return _pcall(
        body, grid=(ATT_HEADS,), in_specs=ins + [one, one, one, _ANY], out_specs=_ANY,
        out_shape=SDS((S, IN_WIDTH), BF), aliases={12: 0},
        scratch_shapes=[pltpu.VMEM((S, LANE), F32)] * 4 + [pltpu.VMEM((3, S, LANE), BF), pltpu.SemaphoreType.DMA((3,)),
                                                           pltpu.VMEM((9, S, LANE), F32)],
        compiler_params=_cp(VMEM_BIG), name="attn_bwd")(*([proj] * 9), o_att, lse, do_att, dproj)


def _bdot(a, b, dims):
    return lax.dot_general(a.astype(BF), b.astype(BF), (dims, ((), ())), preferred_element_type=F32)


def _hg_chunk(qh, fh, ih, gh, lbraw, nw, st):
    C = HG_CHUNK
    mx = jnp.maximum(lbraw[0:1], lbraw[1:2])
    e0 = jnp.exp(lbraw[0:1] - mx)
    e1 = jnp.exp(lbraw[1:2] - mx)
    lb = e0 / (e0 + e1)
    f = lb + (1.0 - lb) * jax.nn.sigmoid(fh)
    logf = jnp.log(f)
    k = 1.0 - f
    q = qh * jax.nn.sigmoid(qh)
    causal = lax.broadcasted_iota(jnp.int32, (C, C), 0) >= lax.broadcasted_iota(jnp.int32, (C, C), 1)
    b = lax.dot_general(causal.astype(F32), logf, (((1,), (0,)), ((), ())), precision=lax.Precision.HIGHEST,
                        preferred_element_type=F32)
    b_last = b[C - 1:C]
    st_out = st * jnp.exp(b_last) + _bdot(ih, k * jnp.exp(b_last - b), ((0,), (0,)))
    inter = _bdot(q * jnp.exp(b), st, ((1,), (1,)))
    b_ref = b[C // 2:C // 2 + 1]
    a = _bdot(q * jnp.exp(b - b_ref), k * jnp.exp(b_ref - b), ((1,), (1,)))
    a = jnp.where(causal, a, 0.0)
    o = inter + _bdot(a, ih, ((1,), (0,)))
    o = o * lax.rsqrt(jnp.mean(o * o, axis=-1, keepdims=True) + RMS_EPS) * nw
    return o * (gh * jax.nn.sigmoid(gh)), st_out


def _to_heads(x):
    return jnp.concatenate([x[None, :, h * LANE:(h + 1) * LANE] for h in range(HG_HEADS)], axis=0)


def _from_heads(x):
    return jnp.concatenate([x[h] for h in range(HG_HEADS)], axis=1)


def _hdot(a, b, ca, cb):
    return lax.dot_general(a.astype(BF), b.astype(BF), (((ca,), (cb,)), ((0,), (0,))), preferred_element_type=F32)


def _hg_chunk_all(qh, fh, ih, gh, lbraw, nw, st):
    C = HG_CHUNK
    mx = jnp.maximum(lbraw[0:1], lbraw[1:2])
    e0 = jnp.exp(lbraw[0:1] - mx)
    e1 = jnp.exp(lbraw[1:2] - mx)
    lb = e0 / (e0 + e1)
    f = lb + (1.0 - lb) * jax.nn.sigmoid(fh)
    logf = jnp.log(f)
    k = 1.0 - f
    q = qh * jax.nn.sigmoid(qh)
    causal = lax.broadcasted_iota(jnp.int32, (C, C), 0) >= lax.broadcasted_iota(jnp.int32, (C, C), 1)
    b = lax.dot_general(causal.astype(F32), logf, (((1,), (0,)), ((), ())), precision=lax.Precision.HIGHEST,
                        preferred_element_type=F32)
    b_last = b[C - 1:C]
    b_ref = b[C // 2:C // 2 + 1]
    v3 = _to_heads(ih)
    st_out = st * _to_heads(jnp.exp(b_last)) + _hdot(v3, _to_heads(k * jnp.exp(b_last - b)), 1, 1)
    inter = _hdot(_to_heads(q * jnp.exp(b)), st, 2, 2)
    a = _hdot(_to_heads(q * jnp.exp(b - b_ref)), _to_heads(k * jnp.exp(b_ref - b)), 2, 2)
    a = jnp.where(causal[None], a, 0.0)
    o = inter + _hdot(a, v3, 2, 1)
    o = o * lax.rsqrt(jnp.mean(o * o, axis=-1, keepdims=True) + RMS_EPS) * nw
    return _from_heads(o) * (gh * jax.nn.sigmoid(gh)), st_out


def _head_cols(h):
    return slice(h * LANE, (h + 1) * LANE)


def _hg_in_specs(chunk_of):
    half = 4 * LANE
    specs = []
    for base in (QH0, FH0, IH0, GH0):
        for part in range(2):
            specs.append(pl.BlockSpec((HG_CHUNK, half), lambda n, col=base // 4 + part: (chunk_of(n), col)))
    return specs


def _hg_sections(sec):
    return [jnp.concatenate([sec[2 * s][...], sec[2 * s + 1][...]], axis=1).astype(F32) for s in range(4)]


def _hg_fwd(proj, lb, nw):
    S = proj.shape[0]
    C = HG_CHUNK
    NC = S // C

    def body(*refs):
        sec = refs[:8]
        lb_ref, nw_ref, o_ref, st_ref, st_scr = refs[8:]

        @pl.when(pl.program_id(0) == 0)
        def _():
            st_scr[...] = jnp.zeros_like(st_scr)

        st = st_scr[...]
        st_ref[:, 0] = st
        o, st_out = _hg_chunk_all(*_hg_sections(sec), lb_ref[...], nw_ref[...], st)
        o_ref[...] = o.astype(o_ref.dtype)
        st_scr[...] = st_out

    return _pcall(
        body, grid=(NC,),
        in_specs=_hg_in_specs(lambda n: n) + [pl.BlockSpec((2, HG_WIDTH), lambda n: (0, 0)), pl.BlockSpec((1, LANE), lambda n: (0, 0))],
        out_specs=[pl.BlockSpec((C, HG_WIDTH), lambda n: (n, 0)), pl.BlockSpec((HG_HEADS, 1, LANE, LANE), lambda n: (0, n, 0, 0))],
        out_shape=[SDS((S, HG_WIDTH), BF), SDS((HG_HEADS, NC, LANE, LANE), F32)],
        scratch_shapes=[pltpu.VMEM((HG_HEADS, LANE, LANE), F32)], name="hg_fwd")(*([proj] * 8), lb, nw)


def _hg_bwd(proj, lb, nw, states, do_hg, dproj):
    S = proj.shape[0]
    C = HG_CHUNK
    NC = S // C

    def body(*refs):
        sec = refs[:8]
        lb_ref, nw_ref, st_ref, do_ref, dp_in, dp_ref, dlb_ref, dnw_ref, dst_scr, stage, sem = refs[8:]
        n = pl.program_id(0)
        slot = n % 2

        def chunk_copies(s):
            rows = pl.ds(pl.multiple_of((NC - 1 - n) * C, C), C)
            return [pltpu.make_async_copy(stage.at[s, t], dp_ref.at[rows, pl.ds((QH0 + 8 * t) * LANE, HG_WIDTH)], sem.at[s, t])
                    for t in range(4)]

        @pl.when(n == 0)
        def _():
            dst_scr[...] = jnp.zeros_like(dst_scr)
            dlb_ref[...] = jnp.zeros_like(dlb_ref)
            dnw_ref[...] = jnp.zeros_like(dnw_ref)

        _, vjp = jax.vjp(_hg_chunk_all, *_hg_sections(sec), lb_ref[...], nw_ref[...], st_ref[:, 0])
        dq, df, di, dg, dlb, dnw, dst = vjp((do_ref[...].astype(F32), dst_scr[...]))
        dlb_ref[...] += dlb
        dnw_ref[...] += dnw
        dst_scr[...] = dst

        @pl.when(n >= 2)
        def _():
            for cp in chunk_copies(slot):
                cp.wait()

        for t, val in enumerate((dq, df, di, dg)):
            stage[slot, t] = val.astype(BF)
        for cp in chunk_copies(slot):
            cp.start()

        @pl.when(n == NC - 1)
        def _():
            for cp in chunk_copies(1 - slot) + chunk_copies(slot):
                cp.wait()

    rev = lambda n: NC - 1 - n
    ospec = pl.BlockSpec((C, HG_WIDTH), lambda n: (rev(n), 0))
    lbs = pl.BlockSpec((2, HG_WIDTH), lambda n: (0, 0))
    nws = pl.BlockSpec((1, LANE), lambda n: (0, 0))
    outs = _pcall(
        body, grid=(NC,),
        in_specs=_hg_in_specs(rev) + [lbs, nws, pl.BlockSpec((HG_HEADS, 1, LANE, LANE), lambda n: (0, rev(n), 0, 0)), ospec, _ANY],
        out_specs=[_ANY, lbs, nws],
        out_shape=[SDS((S, IN_WIDTH), BF), SDS((2, HG_WIDTH), F32), SDS((1, LANE), F32)], aliases={12: 0},
        scratch_shapes=[pltpu.VMEM((HG_HEADS, LANE, LANE), F32), pltpu.VMEM((2, 4, C, HG_WIDTH), BF), pltpu.SemaphoreType.DMA((2, 4))],
        name="hg_bwd")(*([proj] * 8), lb, nw, states, do_hg, dproj)
    return outs


def _merge(ga, gb, ba, bb):
    return jax.nn.sigmoid(ga) * ba + jax.nn.sigmoid(gb) * bb


_MT = 512


def _swiglu(a, b):
    return a * jax.nn.sigmoid(a) * b


def _cross(q, k, v):
    s = _bdot(q, k, ((1,), (1,))) * (HEAD_DIM ** -0.5)
    e = jnp.exp(s - jnp.max(s, axis=-1, keepdims=True))
    p = e / jnp.sum(e, axis=-1, keepdims=True)
    return _bdot(p, v, ((1,), (0,)))


def _cross_specs():
    q = pl.BlockSpec((_MT, LANE), lambda h, i: (i, h))
    k = pl.BlockSpec((MEM_LEN, LANE), lambda h, i: (0, h))
    v = pl.BlockSpec((MEM_LEN, LANE), lambda h, i: (0, CROSS_HEADS + h))
    return q, k, v


def _cross_fwd(qc, kvc):
    S = qc.shape[0]
    q, k, v = _cross_specs()

    def body(q_ref, k_ref, v_ref, o_ref):
        o_ref[...] = _cross(q_ref[...], k_ref[...], v_ref[...]).astype(o_ref.dtype)

    return _pcall(body, grid=(CROSS_HEADS, S // _MT), in_specs=[q, k, v], out_specs=q,
                          out_shape=SDS((S, CROSS_WIDTH), BF), name="cross_fwd")(qc, kvc, kvc)


def _cross_bwd(qc, kvc, doc):
    S = qc.shape[0]
    q, k, v = _cross_specs()

    def body(q_ref, k_ref, v_ref, do_ref, dq_ref, dk_ref, dv_ref):
        @pl.when(pl.program_id(1) == 0)
        def _():
            dk_ref[...] = jnp.zeros_like(dk_ref)
            dv_ref[...] = jnp.zeros_like(dv_ref)

        _, vjp = jax.vjp(_cross, q_ref[...], k_ref[...], v_ref[...])
        dq, dk, dv = vjp(do_ref[...])
        dq_ref[...] = dq.astype(dq_ref.dtype)
        dk_ref[...] += dk
        dv_ref[...] += dv

    return _pcall(body, grid=(CROSS_HEADS, S // _MT), in_specs=[q, k, v, q], out_specs=[q, k, k],
                          out_shape=[SDS((S, CROSS_WIDTH), BF), SDS((MEM_LEN, CROSS_WIDTH), F32), SDS((MEM_LEN, CROSS_WIDTH), F32)],
                          name="cross_bwd")(qc, kvc, kvc, doc)


def _local_step(x, mem, tgt, p, ready):
    h = _rms_fwd(x, p["ln_mix"], "rms_mix")
    proj = _mm(h, p["w_in_a"], "nt", BF, "mm_proj_a", a_part=(0, 2))
    proj = _mm(h, p["w_in_b"], "nt", BF, "mm_proj_b", a_part=(1, 2), res=proj)
    o_att, lse = _attn_fwd(proj)
    o_hg, states = _hg_fwd(proj, p["hg_lb"], p["hg_norm"])
    ba = _mm(o_att, p["w_ba"], "nn", BF, "mm_ba")
    gates = [(proj, GA0 * LANE // _MT), (proj, GB0 * LANE // _MT)]

    def f32s(*vals):
        return [v.astype(F32) for v in vals]

    bb, merged = _mm_fused(o_hg, p["w_bb"], "nn", "mm_bb_merge", gates + [ba], [BF, BF], tn=_MT,
                           epi=lambda d, ga, gb, bav: (d, _merge(*f32s(ga, gb, bav), d)))
    x1 = _mm(merged, p["w_out"], "nn", F32, "mm_out", res=x)
    hc = _rms_fwd(x1, p["ln_cross"], "rms_cross")
    qc = _mm(hc, p["wq"], "nn", F32, "mm_q")
    mn = _rms_fwd(mem, p["ln_mem"], "rms_mem")
    kvc = _mm(mn, p["wkv"], "nn", F32, "mm_kv")
    oc = _cross_fwd(qc, kvc)
    x2 = _mm(oc, p["wo"], "nn", F32, "mm_o", res=x1)
    hf = _rms_fwd(x2, p["ln_ffn"], "rms_ffn")
    a = _mm(hf, p["w1"], "nt", BF, "mm_w1")
    b, g = _mm_fused(hf, p["w3"], "nt", "mm_w3_swiglu", [a], [BF, BF], lambda d, av: (d, _swiglu(av.astype(F32), d)))
    x3 = _mm(g, p["w2"], "nn", F32, "mm_w2", res=x2)
    loss, dx3, dx3b, d_ln_final = _final_loss(x3, p["ln_final"], tgt)

    gw, gs = {}, {"ln_final": d_ln_final}
    da, db = _mm_fused(dx3b, p["w2"], "nt", "mm_dg_swiglu", [a, b], [BF, BF],
                       lambda d, av, bv: jax.vjp(_swiglu, av.astype(F32), bv.astype(F32))[1](d))
    gw["w2"] = _mm(g, dx3b, "tn", BF, "mm_dw2")
    dhf = _mm(da, p["w1"], "nn", F32, "mm_dhf1")
    dhf = _mm(db, p["w3"], "nn", BF, "mm_dhf3", res=dhf)
    gw["w1"] = _mm(da, hf, "tn", BF, "mm_dw1")
    gw["w3"] = _mm(db, hf, "tn", BF, "mm_dw3")
    ready("ffn", gw)
    dx2, dx2b, gs["ln_ffn"] = _rms_bwd(x2, p["ln_ffn"], dhf, dx3, "rms_ffn_bwd")
    doc = _mm(dx2b, p["wo"], "nt", F32, "mm_doc")
    gw["wo"] = _mm(oc, dx2b, "tn", BF, "mm_dwo")
    dqc, dk, dv = _cross_bwd(qc, kvc, doc)
    dkvc = jnp.concatenate([dk, dv], axis=1)
    dhc = _mm(dqc, p["wq"], "nt", BF, "mm_dhc")
    gw["wq"] = _mm(hc, dqc, "tn", BF, "mm_dwq")
    dmn = _mm(dkvc, p["wkv"], "nt", F32, "mm_dmn")
    gw["wkv"] = _mm(mn, dkvc, "tn", BF, "mm_dwkv")
    ready("cross", gw)
    gs["ln_mem"] = _rms_bwd(mem, p["ln_mem"], dmn, None, "rms_mem_bwd")
    dx1, dx1b, gs["ln_cross"] = _rms_bwd(x1, p["ln_cross"], dhc, dx2, "rms_cross_bwd")
    def merge_back(d, ga, gb, bav, bbv):
        dga, dgb, dba_, dbb_ = jax.vjp(_merge, *f32s(ga, gb, bav, bbv))[1](d)
        return dba_, dbb_, dga, dgb

    dba, dbb, dproj = _mm_fused(dx1b, p["w_out"], "nt", "mm_dmerged_merge", gates + [ba, bb], [BF, BF], merge_back, tn=_MT,
                                scatter=(SDS((x.shape[0], IN_WIDTH), BF), [GA0 * LANE, GB0 * LANE]))
    gw["w_out"] = _mm(merged, dx1b, "tn", BF, "mm_dwout")
    do_att = _mm(dba, p["w_ba"], "nt", F32, "mm_doatt")
    gw["w_ba"] = _mm(o_att, dba, "tn", BF, "mm_dwba")
    do_hg = _mm(dbb, p["w_bb"], "nt", F32, "mm_dohg")
    gw["w_bb"] = _mm(o_hg, dbb, "tn", BF, "mm_dwbb")
    ready("mix", gw)
    dproj = _attn_bwd(proj, o_att, lse, do_att, dproj)
    dproj, gs["hg_lb"], gs["hg_norm"] = _hg_bwd(proj, p["hg_lb"], p["hg_norm"], states, do_hg, dproj)
    for q in range(IN_PARTS):
        gw[f"w_in_q{q}"] = _mm(dproj, h, "tn", BF, f"mm_dwin_q{q}", b_part=(q, IN_PARTS))
        ready(f"in_q{q}", gw)
        if q == 0:
            ready("mixers_done", gw)
    dh_a = _mm(dproj, p["w_in_a"], "nn", BF, "mm_dh_a")
    ready("dh_half", gw)
    dh = (dh_a, _mm(dproj, p["w_in_b"], "nn", BF, "mm_dh_b"))
    grad_x, _, gs["ln_mix"] = _rms_bwd(x, p["ln_mix"], dh, dx1, "rms_mix_bwd")
    return loss, grad_x, gs


_MATS = (
    ("w_in_a", "w_in", IN_WIDTH, D_MODEL // 2, "row", IN_WIDTH // N_DEV, True),
    ("w_in_b", "w_in", IN_WIDTH, D_MODEL // 2, "row", IN_WIDTH // N_DEV, True),
    ("w_ba", "w_branch_a", ATT_OUT, D_MODEL, "col", D_MODEL // N_DEV, False),
    ("w_bb", "w_branch_b", HG_WIDTH, D_MODEL, "col", D_MODEL // N_DEV, False),
    ("w_out", "w_out", D_MODEL, D_MODEL, "row", D_MODEL // N_DEV, False),
    ("wq", "wq_cross", D_MODEL, CROSS_WIDTH, "row", D_MODEL // N_DEV, False),
    ("wkv", "wkv_cross", D_MODEL, 2 * CROSS_WIDTH, "row", D_MODEL // N_DEV, False),
    ("wo", "wo_cross", CROSS_WIDTH, D_MODEL, "col", D_MODEL // N_DEV, False),
    ("w1", "w1", D_FF, D_MODEL, "row", D_FF // N_DEV, True),
    ("w3", "w3", D_FF, D_MODEL, "row", D_FF // N_DEV, True),
    ("w2", "w2", D_FF, D_MODEL, "row", D_FF // N_DEV, False),
    ("w_in_q0", "w_in", IN_WIDTH, D_MODEL // 2, "row", IN_WIDTH // N_DEV, True),
    ("w_in_q1", "w_in", IN_WIDTH, D_MODEL // 2, "row", IN_WIDTH // N_DEV, True),
)
IN_PARTS = 2
_INDEX = {m[0]: i for i, m in enumerate(_MATS)}


def _unit_shape(i):
    _, _, K, N, kind, sh, _ = _MATS[i]
    return (sh, N) if kind == "row" else (K, sh)


def _win(ref, i, chip, par, half=None):
    _, _, K, _, kind, sh, _ = _MATS[i]
    if kind == "row":
        rows, start = (sh, 0) if half is None else (sh // 2, half * (sh // 2))
        return ref.at[pl.ds(pl.multiple_of((2 * chip + par) * sh + start, 16), rows), :]
    rows, start = (K, 0) if half is None else (K // 2, half * (K // 2))
    return ref.at[pl.ds(start, rows), pl.ds(pl.multiple_of((2 * chip + par) * sh, LANE), sh)]


def _my_place():
    x, y, c = lax.axis_index("x"), lax.axis_index("y"), lax.axis_index("c")
    chips = [(1 - x, y), (x, 1 - y), (1 - x, 1 - y)]
    return x, y, c, chips


def _cast_unit(w, name, half=None):
    K, N = w.shape
    if half is not None:
        N = N // 2
    tr = _pick(K, (256, 400, 352))

    def body(w_ref, o_ref):
        o_ref[...] = w_ref[...].astype(BF)

    src = pl.BlockSpec((tr, N), lambda i: (i, half or 0))
    dst = pl.BlockSpec((tr, N), lambda i: (i, 0))
    return _pcall(body, grid=(K // tr,), in_specs=[src], out_specs=dst, out_shape=SDS((K, N), BF), name=name)(w)


def _handshake(peers):
    barrier = pltpu.get_barrier_semaphore()
    for peer in peers:
        pl.semaphore_signal(barrier, inc=1, device_id=peer, device_id_type=MESH)
    pl.semaphore_wait(barrier, len(peers))


def _sequencer(body, out_type, sems, cid, name):
    return pl.kernel(body, out_type=out_type, mesh=plsc.ScalarSubcoreMesh(axis_name="seq", num_cores=1),
                     scratch_types=sems, compiler_params=pltpu.CompilerParams(collective_id=cid), name=name)


def _all_gather(units, idx, cid, name):
    n = len(idx)

    def body(*refs):
        us, fulls = refs[:n], refs[n:2 * n]
        ssem, rsem, lsem = refs[2 * n:]
        x, y, c, _ = _my_place()
        xn, yn, dg = (1 - x, y), (x, 1 - y), (1 - x, 1 - y)
        me, sib = (x, y, c), (x, y, 1 - c)
        _handshake([sib, (*xn, c), (*yn, c)])

        def win(i, chip_xy, par, half=None):
            return _win(fulls[i], idx[i], 2 * chip_xy[0] + chip_xy[1], par, half)

        def rcopy(i, k, src, dst, to):
            return pltpu.make_async_remote_copy(src_ref=src, dst_ref=dst, send_sem=ssem.at[i, k], recv_sem=rsem.at[i, k],
                                                device_id=to, device_id_type=MESH)

        def landed(i, k, blk):
            rcopy(i, k, blk, blk, me).wait_recv()

        sends, locs = [], []

        def send(i, k, blk, to, src=None):
            cp = rcopy(i, k, blk if src is None else src, blk, to)
            cp.start()
            sends.append(cp)

        for i in range(n):
            mine = win(i, (x, y), c)
            loc = pltpu.make_async_copy(us[i], mine, lsem.at[i])
            loc.start()
            locs.append(loc)
            send(i, 0, mine, sib, src=us[i])
            send(i, 1, mine, (*xn, c), src=us[i])
            send(i, 2, mine, (*yn, c), src=us[i])
        for i in range(n):
            landed(i, 1, win(i, xn, c))
            send(i, 3, win(i, xn, c, 0), (*yn, c))
            send(i, 5, win(i, xn, c), sib)
            landed(i, 2, win(i, yn, c))
            send(i, 4, win(i, yn, c, 1), (*xn, c))
            send(i, 6, win(i, yn, c), sib)
        for i in range(n):
            landed(i, 3, win(i, dg, c, 0))
            landed(i, 4, win(i, dg, c, 1))
            send(i, 7, win(i, dg, c), sib)
        for i in range(n):
            landed(i, 0, win(i, (x, y), 1 - c))
            landed(i, 5, win(i, xn, 1 - c))
            landed(i, 6, win(i, yn, 1 - c))
            landed(i, 7, win(i, dg, 1 - c))
        for cp in sends:
            cp.wait_send()
        for loc in locs:
            loc.wait()

    out_type = [SDS((_MATS[i][2], _MATS[i][3]), BF) for i in idx]
    sems = [pltpu.SemaphoreType.DMA((n, 8)), pltpu.SemaphoreType.DMA((n, 8)), pltpu.SemaphoreType.DMA((n,))]
    return _sequencer(body, out_type, sems, cid, name)(*units)


def _rs_sibling(grads, idx, cid, name):
    n = len(idx)

    def body(*refs):
        gs, bufs = refs[:n], refs[n:2 * n]
        ssem, rsem = refs[2 * n:]
        x, y, c, _ = _my_place()
        _handshake([(x, y, 1 - c)])
        cps = []
        for i in range(n):
            for k in range(4):
                cp = pltpu.make_async_remote_copy(
                    src_ref=_win(gs[i], idx[i], k, 1 - c), dst_ref=bufs[i].at[k], send_sem=ssem.at[i, k], recv_sem=rsem.at[i, k],
                    device_id=(x, y, 1 - c), device_id_type=MESH)
                cp.start()
                cps.append(cp)
        for cp in cps:
            cp.wait()

    out_type = [SDS((4, *_unit_shape(i)), BF) for i in idx]
    sems = [pltpu.SemaphoreType.DMA((n, 4)), pltpu.SemaphoreType.DMA((n, 4))]
    return _sequencer(body, out_type, sems, cid, name)(*grads)


def _sibling_sum_tc(grad, i, cid, name):
    unit = _unit_shape(i)

    def body(g_ref, t_ref, theirs, mine, summed, ssem, rsem, lsem, osem):
        x, y, c, _ = _my_place()
        _handshake([(x, y, 1 - c)])
        sends, loads = [], []
        for k in range(4):
            sends.append(pltpu.make_async_remote_copy(
                src_ref=_win(g_ref, i, k, 1 - c), dst_ref=theirs.at[k], send_sem=ssem.at[k], recv_sem=rsem.at[k],
                device_id=(x, y, 1 - c), device_id_type=MESH))
            loads.append(pltpu.make_async_copy(_win(g_ref, i, k, c), mine.at[k], lsem.at[k]))
            sends[-1].start()
            loads[-1].start()
        stores = []
        for k in range(4):
            loads[k].wait()
            sends[k].wait_recv()
            summed[k] = (mine[k].astype(F32) + theirs[k].astype(F32)).astype(BF)
            stores.append(pltpu.make_async_copy(summed.at[k], t_ref.at[k], osem.at[k]))
            stores[-1].start()
        for k in range(4):
            sends[k].wait_send()
            stores[k].wait()

    block = pltpu.VMEM((4, *unit), BF)
    return _pcall(body, in_specs=[_ANY], out_specs=_ANY, out_shape=SDS((4, *unit), BF),
                  scratch_shapes=[block, block, block] + [pltpu.SemaphoreType.DMA((4,))] * 4,
                  compiler_params=_cp(VMEM_BIG, collective_id=cid), name=name)(grad)


def _chip_sum(g, sib, i, place, name):
    _, _, K, N, kind, sh, _ = _MATS[i]

    def body(pref, g_ref, s_ref, o_ref):
        o_ref[...] = (g_ref[...].astype(F32) + s_ref[...].astype(F32)).astype(BF)

    if kind == "row":
        tr = _pick(sh, (800, 352, 256))
        per = sh // tr
        grid = (4, per)
        g_spec = pl.BlockSpec((tr, N), lambda k, r, pref: ((2 * k + pref[0]) * per + r, 0))
        u_spec = pl.BlockSpec((None, tr, N), lambda k, r, pref: (k, r, 0))
    else:
        grid = (4,)
        g_spec = pl.BlockSpec((K, sh), lambda k, pref: (0, 2 * k + pref[0]))
        u_spec = pl.BlockSpec((None, K, sh), lambda k, pref: (k, 0, 0))
    return _pcall(body, grid=grid, in_specs=[g_spec, u_spec], out_specs=u_spec, out_shape=SDS((4, *_unit_shape(i)), BF),
                  prefetch=True, name=name)(place, g, sib)


def _rs_chips(parts, idx, cid, name):
    n = len(idx)

    def body(*refs):
        ts, bufs, relays = refs[:n], refs[n:2 * n], refs[2 * n:3 * n]
        ssem, rsem = refs[3 * n:]
        x, y, c, _ = _my_place()
        xn, yn, dg = (1 - x, y), (x, 1 - y), (1 - x, 1 - y)
        to_x, to_y, me = (*xn, c), (*yn, c), (x, y, c)
        _handshake([to_x, to_y])

        def rcopy(i, k, src, dst, to):
            return pltpu.make_async_remote_copy(src_ref=src, dst_ref=dst, send_sem=ssem.at[i, k], recv_sem=rsem.at[i, k],
                                                device_id=to, device_id_type=MESH)

        def halves(i):
            rows = _unit_shape(idx[i])[0] // 2
            return [pl.ds(0, rows), pl.ds(rows, rows)]

        sends = []

        def send(cp):
            cp.start()
            sends.append(cp)

        for i in range(n):
            h0, h1 = halves(i)
            diag = ts[i].at[2 * dg[0] + dg[1]]
            send(rcopy(i, 0, ts[i].at[2 * xn[0] + xn[1]], bufs[i].at[0], to_x))
            send(rcopy(i, 1, ts[i].at[2 * yn[0] + yn[1]], bufs[i].at[1], to_y))
            send(rcopy(i, 2, diag.at[h0], relays[i].at[0], to_x))
            send(rcopy(i, 3, diag.at[h1], relays[i].at[1], to_y))
        for i in range(n):
            h0, h1 = halves(i)
            rcopy(i, 2, relays[i].at[0], relays[i].at[0], me).wait_recv()
            send(rcopy(i, 4, relays[i].at[0], bufs[i].at[2, h0], to_y))
            rcopy(i, 3, relays[i].at[1], relays[i].at[1], me).wait_recv()
            send(rcopy(i, 5, relays[i].at[1], bufs[i].at[2, h1], to_x))
        for i in range(n):
            h0, h1 = halves(i)
            rcopy(i, 0, bufs[i].at[0], bufs[i].at[0], me).wait_recv()
            rcopy(i, 1, bufs[i].at[1], bufs[i].at[1], me).wait_recv()
            rcopy(i, 4, bufs[i].at[2, h0], bufs[i].at[2, h0], me).wait_recv()
            rcopy(i, 5, bufs[i].at[2, h1], bufs[i].at[2, h1], me).wait_recv()
        for cp in sends:
            cp.wait_send()

    def relay_shape(i):
        rows, cols = _unit_shape(i)
        return (2, rows // 2, cols)

    out_type = [SDS((3, *_unit_shape(i)), BF) for i in idx] + [SDS(relay_shape(i), BF) for i in idx]
    sems = [pltpu.SemaphoreType.DMA((n, 6)), pltpu.SemaphoreType.DMA((n, 6))]
    return _sequencer(body, out_type, sems, cid, name)(*parts)[:n]


def _adamw(w, g, m, v):
    m = ADAM_B1 * m + (1.0 - ADAM_B1) * g
    v = ADAM_B2 * v + (1.0 - ADAM_B2) * jnp.square(g)
    m_hat = m / (1.0 - ADAM_B1 ** ADAM_STEP)
    v_hat = v / (1.0 - ADAM_B2 ** ADAM_STEP)
    delta = -ADAM_LR * (m_hat / (jnp.sqrt(v_hat) + ADAM_EPS) + ADAM_WD * w)
    return delta, m, v


def _adam_mat(parts, recvs, w, m, v, place, name):
    R, C = w.shape
    n = len(parts)
    tr = _pick(R, (256, 160, 176))

    def body(pref, *refs):
        t_refs, r_refs = refs[:n], refs[n:2 * n]
        w_ref, m_ref, v_ref, g_out, d_out, m_out, v_out = refs[2 * n:]
        cols = [t[...].astype(F32) + r[0].astype(F32) + r[1].astype(F32) + r[2].astype(F32) for t, r in zip(t_refs, r_refs)]
        g = jnp.concatenate(cols, axis=1) if n > 1 else cols[0]
        d, mn, vn = _adamw(w_ref[...], g, m_ref[...], v_ref[...])
        g_out[...] = g
        d_out[...] = d
        m_out[...] = mn
        v_out[...] = vn

    t_spec = pl.BlockSpec((None, tr, C // n), lambda i, pref: (pref[1], i, 0))
    r_spec = pl.BlockSpec((3, tr, C // n), lambda i, pref: (0, i, 0))
    s_spec = pl.BlockSpec((tr, C), lambda i, pref: (i, 0))
    return _pcall(body, grid=(R // tr,), in_specs=[t_spec] * n + [r_spec] * n + [s_spec] * 3, out_specs=[s_spec] * 4,
                  out_shape=[SDS((R, C), F32)] * 4, prefetch=True, name=name)(place, *parts, *recvs, w, m, v)


_SMALL = (("ln_mix", "ln_mix_w", D_MODEL), ("ln_cross", "ln_cross_w", D_MODEL), ("ln_mem", "ln_mem_w", D_MODEL),
          ("ln_ffn", "ln_ffn_w", D_MODEL), ("ln_final", "ln_final_w", D_MODEL), ("hg_lb", "hg_lower_bounds", 2 * HG_WIDTH),
          ("hg_norm", "hg_norm_w", HEAD_DIM))
_PACK = sum(n for _, _, n in _SMALL) + LANE


def _small_sync(grow, wrow, mrow, vrow):
    def body(g_ref, w_ref, m_ref, v_ref, o_ref, gath, ssem, rsem):
        x, y, c, _ = _my_place()
        me = 4 * x + 2 * y + c
        gath[pl.ds(me, 1), :] = g_ref[...]
        cps = []
        for dlt in range(1, N_DEV):
            peer = (me + dlt) % N_DEV
            cp = pltpu.make_async_remote_copy(
                src_ref=g_ref, dst_ref=gath.at[pl.ds(me, 1), :], send_sem=ssem.at[peer], recv_sem=rsem.at[me],
                device_id=(peer // 4, (peer // 2) % 2, peer % 2), device_id_type=MESH)
            cp.start()
            cps.append(cp)
        for dlt in range(1, N_DEV):
            src = (me + dlt) % N_DEV
            pltpu.make_async_remote_copy(
                src_ref=g_ref, dst_ref=gath.at[pl.ds(src, 1), :], send_sem=ssem.at[src], recv_sem=rsem.at[src],
                device_id=(x, y, c), device_id_type=MESH).wait_recv()
        for cp in cps:
            cp.wait_send()
        g = gath[0:1, :]
        for j in range(1, N_DEV):
            g = g + gath[j:j + 1, :]
        d, mn, vn = _adamw(w_ref[...], g, m_ref[...], v_ref[...])
        o_ref[0:1, :] = g
        o_ref[1:2, :] = d
        o_ref[2:3, :] = mn
        o_ref[3:4, :] = vn

    vm = pl.BlockSpec(memory_space=pltpu.VMEM)
    return _pcall(
        body, in_specs=[vm] * 4, out_specs=vm, out_shape=SDS((4, _PACK), F32),
        scratch_shapes=[pltpu.VMEM((N_DEV, _PACK), F32), pltpu.SemaphoreType.DMA((N_DEV,)), pltpu.SemaphoreType.DMA((N_DEV,))],
        name="small_sync")(grow, wrow, mrow, vrow)


def _pack_small(d, loss_row=None):
    parts = [d[k].reshape(1, n).astype(F32) for k, _, n in _SMALL]
    parts.append(loss_row if loss_row is not None else jnp.ones((1, LANE), F32))
    return jnp.concatenate(parts, axis=1)


def kernel(x, mem, ln_mix_w, w_in, hg_norm_w, hg_lower_bounds, w_branch_a, w_branch_b, w_out, ln_cross_w, ln_mem_w, wq_cross, wkv_cross, wo_cross, ln_ffn_w, w1, w3, w2, ln_final_w, loss_target, m_ln_mix_w, m_w_in, m_hg_norm_w, m_hg_lower_bounds, m_w_branch_a, m_w_branch_b, m_w_out, m_ln_cross_w, m_ln_mem_w, m_wq_cross, m_wkv_cross, m_wo_cross, m_ln_ffn_w, m_w1, m_w3, m_w2, m_ln_final_w, v_ln_mix_w, v_w_in, v_hg_norm_w, v_hg_lower_bounds, v_w_branch_a, v_w_branch_b, v_w_out, v_ln_cross_w, v_ln_mem_w, v_wq_cross, v_wkv_cross, v_wo_cross, v_ln_ffn_w, v_w1, v_w3, v_w2, v_ln_final_w):
    given = dict(locals())
    place = jnp.stack([lax.axis_index("c"), 2 * lax.axis_index("x") + lax.axis_index("y")]).astype(jnp.int32)

    _ORDER[0] = None
    index = _INDEX

    def local_view(a, i):
        return a[0].T if _MATS[i][6] else a[0]

    shard = {key: local_view(given[ref], i) for i, (key, ref, *_) in enumerate(_MATS)}

    p = {}
    col_half = {"w_in_a": 0, "w_in_b": 1}
    waves = (("in_a", ["w_in_a"], 1), ("in_b", ["w_in_b"], 12), ("mid", ["w_ba", "w_bb", "w_out", "wq", "wkv", "wo"], 2),
             ("w1", ["w1"], 11), ("w3", ["w3"], 15), ("w2", ["w2"], 16))
    for wave, keys, cid in waves:
        units = [_cast_unit(shard[k], "cast_" + k, col_half.get(k)) for k in keys]
        p.update(zip(keys, _all_gather(units, [index[k] for k in keys], cid, "all_gather_" + wave)))
    small_w = {"ln_mix": ln_mix_w, "ln_cross": ln_cross_w, "ln_mem": ln_mem_w, "ln_ffn": ln_ffn_w,
               "ln_final": ln_final_w.reshape(1, D_MODEL), "hg_lb": hg_lower_bounds, "hg_norm": hg_norm_w}
    p.update(small_w)

    groups = {"ffn": ["w2", "w1", "w3"], "cross": ["wo", "wq", "wkv"], "mix": ["w_out", "w_ba", "w_bb"]}
    cids = {"ffn": (3, 4), "cross": (5, 6), "mix": (7, 8)}
    for q in range(IN_PARTS):
        groups[f"in_q{q}"] = [f"w_in_q{q}"]
        cids[f"in_q{q}"] = ((9, 10), (13, 14), (17, 18), (19, 20))[q]
    sib, parts, recv, out = {}, {}, {}, {}

    grads = {}

    def chip_stage(name):
        keys = groups[name]
        parts[name] = [_chip_sum(grads[k], s, index[k], place, "chip_sum_" + k) for k, s in zip(keys, sib[name])]
        recv[name] = list(_rs_chips(parts[name], [index[k] for k in keys], cids[name][1], "rs_chips_" + name))

    def adam(ref, i, part_list, recv_list):
        res = _adam_mat(part_list, recv_list, shard[_MATS[i][0]], local_view(given["m_" + ref], i), local_view(given["v_" + ref], i),
                        place, "adam_" + ref)
        out[ref] = [(a.T if _MATS[i][6] else a)[None] for a in res]

    def adam_stage(name, only=None, skip=None):
        for n, (k, t, r) in enumerate(zip(groups[name], parts[name], recv[name])):
            if (only is None or n == only) and n != skip:
                adam(_MATS[index[k]][1], index[k], [t], [r])

    def ready(name, gw):
        if name == "mixers_done":
            adam_stage("ffn", only=0)
            chip_stage("mix")
            adam_stage("ffn", skip=0)
            adam_stage("cross")
            return
        if name == "dh_half":
            adam_stage("mix")
            return
        keys = groups[name]
        grads.update({k: gw[k] for k in keys})
        if name.startswith("in_q"):
            parts[name] = [_sibling_sum_tc(gw[k], index[k], cids[name][0], "sibling_sum_" + k) for k in keys]
            recv[name] = list(_rs_chips(parts[name], [index[k] for k in keys], cids[name][1], "rs_chips_" + name))
            if name == "in_q1":
                _ALSO_AFTER.extend(recv["in_q0"])
            return
        sib[name] = _rs_sibling([gw[k] for k in keys], [index[k] for k in keys], cids[name][0], "rs_sibling_" + name)
        if name == "cross":
            chip_stage("ffn")
        if name == "mix":
            chip_stage("cross")

    loss_row, grad_x, gs = _local_step(x[0], mem[0], loss_target[0], p, ready)

    sm = {k: given["m_" + ref] for k, ref, _ in _SMALL}
    sv = {k: given["v_" + ref] for k, ref, _ in _SMALL}
    synced = _small_sync(_pack_small(gs, loss_row), _pack_small(small_w), _pack_small(sm), _pack_small(sv))
    in_groups = [f"in_q{q}" for q in range(IN_PARTS)]
    adam("w_in", index["w_in_a"], sum((parts[g] for g in in_groups), []), sum((recv[g] for g in in_groups), []))
    off = 0
    for k, ref, n in _SMALL:
        shape = given[ref].shape
        out[ref] = [synced[r, off:off + n].reshape(shape) for r in range(4)]
        off += n
    loss = synced[0, off]

    order = ["ln_mix_w", "w_in", "hg_norm_w", "hg_lower_bounds", "w_branch_a", "w_branch_b", "w_out", "ln_cross_w", "ln_mem_w",
             "wq_cross", "wkv_cross", "wo_cross", "ln_ffn_w", "w1", "w3", "w2", "ln_final_w"]
    return (loss, grad_x[None], *[out[n][0] for n in order], *[out[n][1] for n in order],
            *[out[n][2] for n in order], *[out[n][3] for n in order])
```

```python
import jax
import jax.numpy as jnp
from jax import lax
from jax.experimental import pallas as pl
from jax.experimental.pallas import tpu as pltpu
from jax.experimental.pallas import tpu_sc as plsc

F32 = jnp.float32
BF = jnp.bfloat16
SDS = jax.ShapeDtypeStruct

D_MODEL = 2048
SEQ = 2048
HEAD_DIM = 128
MEM_LEN = 256
ATT_GROUPS = ((128, 1), (512, 4), (2048, 16))
ATT_HEADS = 4
ATT_WIDTH = 1536
ATT_OUT = 512
HG_HEADS = 8
HG_WIDTH = 1024
HG_CHUNK = 64
IN_WIDTH = 12800
CROSS_HEADS = 4
CROSS_WIDTH = 512
D_FF = 5632
RMS_EPS = 1e-6
ADAM_LR = 0.001
ADAM_B1 = 0.9
ADAM_B2 = 0.999
ADAM_EPS = 1e-08
ADAM_WD = 0.01
ADAM_STEP = 10
N_DEV = 8

LANE = 128
QA0, KA0, VA0 = 0, 12, 24
QH0, FH0, IH0, GH0 = 36, 44, 52, 60
GA0, GB0 = 68, 84

VMEM_BIG = 56 * 1024 * 1024
MESH = pl.DeviceIdType.MESH


def _cp(vmem=None, **kw):
    if vmem is not None:
        kw["vmem_limit_bytes"] = vmem
    return pltpu.CompilerParams(**kw)


def _pick(n, cands):
    for c in cands:
        if n % c == 0:
            return c
    raise ValueError(f"no tile for {n}")


_ANY = pl.BlockSpec(memory_space=pl.ANY)
_ORDER = [None]
_ALSO_AFTER = []


def _pcall(body, *, name, in_specs, out_specs, out_shape, grid=(), scratch_shapes=(), compiler_params=None, prefetch=False,
           aliases=None):
    def run(*args):
        dep = _ORDER[0]
        specs, operands, kernel_fn = list(in_specs), list(args), body
        deps = [] if dep is None or any(dep is a for a in operands) else [dep]
        deps += _ALSO_AFTER
        del _ALSO_AFTER[:]
        if deps:
            at, nd = len(operands), len(deps)
            specs += [_ANY] * nd
            operands += deps

            def kernel_fn(*refs):
                return body(*refs[:at], *refs[at + nd:])
        if prefetch:
            call = pl.pallas_call(
                kernel_fn, out_shape=out_shape, name=name, compiler_params=compiler_params,
                grid_spec=pltpu.PrefetchScalarGridSpec(num_scalar_prefetch=1, grid=grid, in_specs=specs, out_specs=out_specs,
                                                       scratch_shapes=scratch_shapes))
        else:
            call = pl.pallas_call(kernel_fn, grid=grid, in_specs=specs, out_specs=out_specs, out_shape=out_shape,
                                  scratch_shapes=scratch_shapes, compiler_params=compiler_params, name=name,
                                  input_output_aliases=aliases or {})
        out = call(*operands)
        _ORDER[0] = out[0] if isinstance(out, (list, tuple)) else out
        return out
    return run


def _mm(a, b, form, out_dtype, name, res=None, tiles=None, a_part=(0, 1), b_part=(0, 1)):
    a_shape = (a.shape[0], a.shape[1] // a_part[1])
    b_shape = (b.shape[0], b.shape[1] // b_part[1])
    if form == "nn":
        (M, K), (K2, N) = a_shape, b_shape
    elif form == "nt":
        (M, K), (N, K2) = a_shape, b_shape
    else:
        (K, M), (K2, N) = a_shape, b_shape
    assert K == K2, (a.shape, b.shape, form)
    if tiles is None:
        tm = _pick(M, (1024, 1280, 1408, 512, 256))
        tn = _pick(N, (1280, 1024, 1408, 512, 256))
        tk = K if K <= 2048 else _pick(K, (D_FF, 2560))
        if tk == D_FF:
            tm = _pick(M, (512, 256))
    else:
        tm, tn, tk = tiles
    nk = K // tk
    ah, bh = a_part[0], b_part[0]
    if form == "nn":
        a_spec = pl.BlockSpec((tm, tk), lambda i, j, k: (i, k + ah * nk))
        b_spec = pl.BlockSpec((tk, tn), lambda i, j, k: (k, j + bh * (N // tn)))
        dims = ((1,), (0,))
    elif form == "nt":
        a_spec = pl.BlockSpec((tm, tk), lambda i, j, k: (i, k + ah * nk))
        b_spec = pl.BlockSpec((tn, tk), lambda i, j, k: (j, k + bh * nk))
        dims = ((1,), (1,))
    else:
        a_spec = pl.BlockSpec((tk, tm), lambda i, j, k: (k, i + ah * (M // tm)))
        b_spec = pl.BlockSpec((tk, tn), lambda i, j, k: (k, j + bh * (N // tn)))
        dims = ((0,), (0,))
    o_spec = pl.BlockSpec((tm, tn), lambda i, j, k: (i, j))
    in_specs = [a_spec, b_spec]
    args = [a, b]
    if res is not None:
        in_specs.append(o_spec)
        args.append(res)

    def body(*refs):
        a_ref, b_ref = refs[0], refs[1]
        r_ref = refs[2] if res is not None else None
        o_ref = refs[3] if res is not None else refs[2]
        acc = refs[-1] if nk > 1 else None
        k = pl.program_id(2)
        d = lax.dot_general(a_ref[...].astype(BF), b_ref[...].astype(BF), (dims, ((), ())), preferred_element_type=F32)

        def finish(r):
            if res is not None:
                r = r + r_ref[...].astype(F32)
            o_ref[...] = r.astype(o_ref.dtype)

        if nk == 1:
            finish(d)
        else:
            @pl.when(k == 0)
            def _():
                acc[...] = d

            @pl.when((k > 0) & (k < nk - 1))
            def _():
                acc[...] += d

            @pl.when(k == nk - 1)
            def _():
                finish(acc[...] + d)

    return _pcall(
        body, grid=(M // tm, N // tn, nk), in_specs=in_specs, out_specs=o_spec,
        out_shape=SDS((M, N), out_dtype), scratch_shapes=[pltpu.VMEM((tm, tn), F32)] if nk > 1 else [],
        compiler_params=_cp(VMEM_BIG, dimension_semantics=("parallel", "parallel", "arbitrary")),
        name=name)(*args)


_MXU_COLS = 256


def _mm_fused(a, b, form, name, extras, out_dtypes, epi, tn=None, scatter=None):
    M, K = a.shape
    N = b.shape[1] if form == "nn" else b.shape[0]
    assert K <= 2048 and K == (b.shape[0] if form == "nn" else b.shape[1])
    tm = _pick(M, (1024, 512, 256))
    tn = tn or _pick(N, (1280, 1024, 1408, 512, 256))
    ne, no = len(extras), len(out_dtypes)
    bases = scatter[1] if scatter else []
    dims = ((1,), (0,)) if form == "nn" else ((1,), (1,))

    def body(*refs):
        a_ref, b_ref, ex, outs = refs[0], refs[1], refs[2:2 + ne], refs[2 + ne:2 + ne + no]
        av = a_ref[...].astype(BF)
        if scatter:
            wide_ref, stage, sem = refs[2 + ne + no:]
        for c0 in range(0, tn, _MXU_COLS):
            cols = slice(c0, min(c0 + _MXU_COLS, tn))
            bc = b_ref[:, cols] if form == "nn" else b_ref[cols, :]
            d = lax.dot_general(av, bc.astype(BF), (dims, ((), ())), preferred_element_type=F32)
            res = epi(d, *[e[:, cols] for e in ex])
            for o_ref, r in zip(outs, res[:no]):
                o_ref[:, cols] = r.astype(o_ref.dtype)
            for s, r in enumerate(res[no:]):
                stage[s, :, cols] = r.astype(BF)
        if scatter:
            rows = pl.ds(pl.multiple_of(pl.program_id(0) * tm, tm), tm)
            cps = [pltpu.make_async_copy(stage.at[s], wide_ref.at[rows, pl.ds(pl.multiple_of(base + pl.program_id(1) * tn, LANE), tn)],
                                         sem.at[s]) for s, base in enumerate(bases)]
            for cp in cps:
                cp.start()
            for cp in cps:
                cp.wait()

    a_spec = pl.BlockSpec((tm, K), lambda i, j: (i, 0))
    b_spec = pl.BlockSpec((K, tn), lambda i, j: (0, j)) if form == "nn" else pl.BlockSpec((tn, K), lambda i, j: (j, 0))
    o_spec = pl.BlockSpec((tm, tn), lambda i, j: (i, j))
    ex_specs, ex_args = [], []
    for e in extras:
        arr, off = e if isinstance(e, tuple) else (e, 0)
        ex_specs.append(pl.BlockSpec((tm, tn), lambda i, j, off=off: (i, off + j)))
        ex_args.append(arr)
    out_specs, out_shape, scratch = [o_spec] * no, [SDS((M, N), dt) for dt in out_dtypes], []
    if scatter:
        out_specs, out_shape = out_specs + [_ANY], out_shape + [scatter[0]]
        scratch = [pltpu.VMEM((len(bases), tm, tn), BF), pltpu.SemaphoreType.DMA((len(bases),))]
    return _pcall(
        body, grid=(M // tm, N // tn), in_specs=[a_spec, b_spec] + ex_specs, out_specs=out_specs, out_shape=out_shape,
        scratch_shapes=scratch, compiler_params=_cp(VMEM_BIG, dimension_semantics=("parallel", "parallel")),
        name=name)(a, b, *ex_args)


def _rms(x, w):
    return x * lax.rsqrt(jnp.mean(x * x, axis=-1, keepdims=True) + RMS_EPS) * w


def _rms_fwd(x, w, name):
    R, D = x.shape
    tr = 256

    def body(x_ref, w_ref, o_ref):
        o_ref[...] = _rms(x_ref[...], w_ref[...]).astype(o_ref.dtype)

    return _pcall(
        body, grid=(R // tr,),
        in_specs=[pl.BlockSpec((tr, D), lambda i: (i, 0)), pl.BlockSpec((1, D), lambda i: (0, 0))],
        out_specs=pl.BlockSpec((tr, D), lambda i: (i, 0)), out_shape=SDS((R, D), BF), name=name)(x, w)


def _rms_bwd(x, w, dh, dres, name):
    R, D = x.shape
    tr = 256
    want_dx = dres is not None
    dhs = dh if isinstance(dh, (tuple, list)) else (dh,)
    nd = len(dhs)

    def body(*refs):
        x_ref, w_ref, dh_refs = refs[0], refs[1], refs[2:2 + nd]
        if want_dx:
            dr_ref, dx_ref, dxb_ref, dw_ref = refs[2 + nd:]
        else:
            dw_ref = refs[2 + nd]

        @pl.when(pl.program_id(0) == 0)
        def _():
            dw_ref[...] = jnp.zeros_like(dw_ref)

        dhv = jnp.concatenate([r[...].astype(F32) for r in dh_refs], axis=1) if nd > 1 else dh_refs[0][...].astype(F32)
        _, vjp = jax.vjp(_rms, x_ref[...], w_ref[...])
        dx, dw = vjp(dhv)
        dw_ref[...] += dw
        if want_dx:
            dx = dx + dr_ref[...]
            dx_ref[...] = dx
            dxb_ref[...] = dx.astype(BF)

    row = pl.BlockSpec((tr, D), lambda i: (i, 0))
    part = pl.BlockSpec((tr, D // nd), lambda i: (i, 0))
    vec = pl.BlockSpec((1, D), lambda i: (0, 0))
    if want_dx:
        return _pcall(body, grid=(R // tr,), in_specs=[row, vec] + [part] * nd + [row], out_specs=[row, row, vec],
                      out_shape=[SDS((R, D), F32), SDS((R, D), BF), SDS((1, D), F32)], name=name)(x, w, *dhs, dres)
    return _pcall(body, grid=(R // tr,), in_specs=[row, vec] + [part] * nd, out_specs=vec,
                  out_shape=SDS((1, D), F32), name=name)(x, w, *dhs)


def _final_loss(x3, w, tgt):
    R, D = x3.shape
    tr = 256

    def loss_fn(xv, wv, tv):
        err = _rms(xv, wv) - tv
        return 0.5 * jnp.sum(jnp.mean(err * err, axis=-1))

    def body(x_ref, w_ref, t_ref, loss_ref, dx_ref, dxb_ref, dw_ref):
        @pl.when(pl.program_id(0) == 0)
        def _():
            dw_ref[...] = jnp.zeros_like(dw_ref)
            loss_ref[...] = jnp.zeros_like(loss_ref)

        tv = t_ref[...]
        val, vjp = jax.vjp(lambda a, b: loss_fn(a, b, tv), x_ref[...], w_ref[...])
        dx, dw = vjp(jnp.ones((), F32))
        dx_ref[...] = dx
        dxb_ref[...] = dx.astype(BF)
        dw_ref[...] += dw
        loss_ref[...] += jnp.full(loss_ref.shape, val, F32)

    row = pl.BlockSpec((tr, D), lambda i: (i, 0))
    vec = pl.BlockSpec((1, D), lambda i: (0, 0))
    one = pl.BlockSpec((1, LANE), lambda i: (0, 0))
    return _pcall(body, grid=(R // tr,), in_specs=[row, vec, row], out_specs=[one, row, row, vec],
                          out_shape=[SDS((1, LANE), F32), SDS((R, D), F32), SDS((R, D), BF), SDS((1, D), F32)],
                          name="final_loss")(x3, w, tgt)


NEG = -1e30


def _att_blocks(S, d):
    L = S // d
    nb = L // LANE
    return nb, (2 if nb > 1 else 1)


def _att_rows(S, d, idx):
    nb, nkb = _att_blocks(S, d)
    r = idx // nb
    n = idx % nb
    kb = jnp.maximum(n - 1, 0)
    if d == 1:
        qrows = pl.ds(pl.multiple_of(n * LANE, LANE), LANE)
        krows = pl.ds(pl.multiple_of(kb * LANE, LANE), LANE * nkb)
    else:
        qrows = pl.ds(r + n * (LANE * d), LANE, stride=d)
        krows = pl.ds(r + kb * (LANE * d), LANE * nkb, stride=d)
    qpos = n * LANE + lax.broadcasted_iota(jnp.int32, (LANE, LANE * nkb), 0)
    kpos = kb * LANE + lax.broadcasted_iota(jnp.int32, (LANE, LANE * nkb), 1)
    return qrows, krows, qpos - kpos


def _slab(col):
    return pl.BlockSpec((SEQ, LANE), lambda h, col=col: (0, col + h))


def _widen(slabs, wide):
    for t, ref in enumerate(slabs):
        wide[t] = ref[...].astype(F32)
    return tuple(tuple(wide.at[3 * kind + g] for g in range(3)) for kind in range(3))


def _attn_fwd(proj):
    S = proj.shape[0]
    scale = HEAD_DIM ** -0.5

    def body(q0, q1, q2, k0, k1, k2, v0, v1, v2, o_ref, lse_ref, og, lg, wide):
        qs, ks, vs = _widen((q0, q1, q2, k0, k1, k2, v0, v1, v2), wide)
        for g, (window, d) in enumerate(ATT_GROUPS):
            def blk(idx, carry, g=g, window=window, d=d):
                qrows, krows, dist = _att_rows(S, d, idx)
                q = qs[g][qrows, :].astype(BF)
                k = ks[g][krows, :].astype(BF)
                v = vs[g][krows, :].astype(BF)
                s = lax.dot_general(q, k, (((1,), (1,)), ((), ())), preferred_element_type=F32) * scale
                s = jnp.where((dist >= 0) & (dist <= window // d), s, NEG)
                m = jnp.max(s, axis=-1, keepdims=True)
                p = jnp.exp(s - m)
                l = jnp.sum(p, axis=-1, keepdims=True)
                o = jnp.dot((p / l).astype(BF), v, preferred_element_type=F32)
                og[g, qrows, :] = o
                lg[g, qrows, :] = jnp.broadcast_to(m + jnp.log(l), (LANE, LANE))
                return carry
            lax.fori_loop(0, S // LANE, blk, 0, unroll=True)

        def merge(t, carry):
            rows = pl.ds(pl.multiple_of(t * 256, 256), 256)
            l0, l1, l2 = lg[0, rows, :], lg[1, rows, :], lg[2, rows, :]
            m = jnp.maximum(jnp.maximum(l0, l1), l2)
            w0, w1, w2 = jnp.exp(l0 - m), jnp.exp(l1 - m), jnp.exp(l2 - m)
            den = w0 + w1 + w2
            o_ref[rows, :] = (w0 * og[0, rows, :] + w1 * og[1, rows, :] + w2 * og[2, rows, :]) / den
            lse_ref[rows, :] = m + jnp.log(den)
            return carry
        lax.fori_loop(0, S // 256, merge, 0)

    ins = [_slab(QA0 + 4 * g) for g in range(3)] + [_slab(KA0 + 4 * g) for g in range(3)] + [_slab(VA0 + 4 * g) for g in range(3)]
    out = pl.BlockSpec((SEQ, LANE), lambda h: (0, h))
    return _pcall(
        body, grid=(ATT_HEADS,), in_specs=ins, out_specs=[out, out],
        out_shape=[SDS((S, ATT_OUT), F32), SDS((S, ATT_OUT), F32)],
        scratch_shapes=[pltpu.VMEM((3, S, LANE), F32), pltpu.VMEM((3, S, LANE), F32), pltpu.VMEM((9, S, LANE), F32)],
        compiler_params=_cp(VMEM_BIG), name="attn_fwd")(*([proj] * 9))


def _attn_bwd(proj, o_att, lse, do_att, dproj):
    S = proj.shape[0]
    scale = HEAD_DIM ** -0.5

    def body(q0, q1, q2, k0, k1, k2, v0, v1, v2, o_ref, lse_ref, do_ref, dp_in, dp_ref, dqa, dka, dva, delta, stage, sem, wide):
        qs, ks, vs = _widen((q0, q1, q2, k0, k1, k2, v0, v1, v2), wide)
        head = pl.program_id(0)

        def slab_copies(g):
            return [pltpu.make_async_copy(stage.at[t], dp_ref.at[:, pl.ds(pl.multiple_of((base + 4 * g + head) * LANE, LANE), LANE)],
                                          sem.at[t]) for t, base in enumerate((QA0, KA0, VA0))]

        delta[...] = jnp.broadcast_to(jnp.sum(do_ref[...] * o_ref[...], axis=-1, keepdims=True), delta.shape)
        for g, (window, d) in enumerate(ATT_GROUPS):
            dka[...] = jnp.zeros_like(dka)
            dva[...] = jnp.zeros_like(dva)

            def blk(idx, carry, g=g, window=window, d=d):
                qrows, krows, dist = _att_rows(S, d, idx)
                q = qs[g][qrows, :].astype(BF)
                k = ks[g][krows, :].astype(BF)
                v = vs[g][krows, :].astype(BF)
                do = do_ref[qrows, :]
                s = lax.dot_general(q, k, (((1,), (1,)), ((), ())), preferred_element_type=F32) * scale
                p = jnp.exp(s - lse_ref[qrows, :][:, 0:1])
                p = jnp.where((dist >= 0) & (dist <= window // d), p, 0.0)
                dob = do.astype(BF)
                dva[krows, :] += lax.dot_general(p.astype(BF), dob, (((0,), (0,)), ((), ())), preferred_element_type=F32)
                dp = lax.dot_general(dob, v, (((1,), (1,)), ((), ())), preferred_element_type=F32)
                ds = (p * (dp - delta[qrows, :][:, 0:1]) * scale).astype(BF)
                dqa[qrows, :] = jnp.dot(ds, k, preferred_element_type=F32)
                dka[krows, :] += lax.dot_general(ds, q, (((0,), (0,)), ((), ())), preferred_element_type=F32)
                return carry
            lax.fori_loop(0, S // LANE, blk, 0, unroll=True)
            if g > 0:
                for cp in slab_copies(g - 1):
                    cp.wait()
            stage[0] = dqa[...].astype(BF)
            stage[1] = dka[...].astype(BF)
            stage[2] = dva[...].astype(BF)
            for cp in slab_copies(g):
                cp.start()
        for cp in slab_copies(len(ATT_GROUPS) - 1):
            cp.wait()

    cols = [QA0 + 4 * g for g in range(3)] + [KA0 + 4 * g for g in range(3)] + [VA0 + 4 * g for g in range(3)]
    ins = [_slab(c) for c in cols]
    one = pl.BlockSpec((SEQ, LANE), lambda h: (0, h))
    return _pcall(
        body, grid=(ATT_HEADS,), in_specs=ins + [one, one, one, _ANY], out_specs=_ANY,
        out_shape=SDS((S, IN_WIDTH), BF), aliases={12: 0},
        scratch_shapes=[pltpu.VMEM((S, LANE), F32)] * 4 + [pltpu.VMEM((3, S, LANE), BF), pltpu.SemaphoreType.DMA((3,)),
                                                           pltpu.VMEM((9, S, LANE), F32)],
        compiler_params=_cp(VMEM_BIG), name="attn_bwd")(*([proj] * 9), o_att, lse, do_att, dproj)


def _bdot(a, b, dims):
    return lax.dot_general(a.astype(BF), b.astype(BF), (dims, ((), ())), preferred_element_type=F32)


def _hg_chunk(qh, fh, ih, gh, lbraw, nw, st):
    C = HG_CHUNK
    mx = jnp.maximum(lbraw[0:1], lbraw[1:2])
    e0 = jnp.exp(lbraw[0:1] - mx)
    e1 = jnp.exp(lbraw[1:2] - mx)
    lb = e0 / (e0 + e1)
    f = lb + (1.0 - lb) * jax.nn.sigmoid(fh)
    logf = jnp.log(f)
    k = 1.0 - f
    q = qh * jax.nn.sigmoid(qh)
    causal = lax.broadcasted_iota(jnp.int32, (C, C), 0) >= lax.broadcasted_iota(jnp.int32, (C, C), 1)
    b = lax.dot_general(causal.astype(F32), logf, (((1,), (0,)), ((), ())), precision=lax.Precision.HIGHEST,
                        preferred_element_type=F32)
    b_last = b[C - 1:C]
    st_out = st * jnp.exp(b_last) + _bdot(ih, k * jnp.exp(b_last - b), ((0,), (0,)))
    inter = _bdot(q * jnp.exp(b), st, ((1,), (1,)))
    b_ref = b[C // 2:C // 2 + 1]
    a = _bdot(q * jnp.exp(b - b_ref), k * jnp.exp(b_ref - b), ((1,), (1,)))
    a = jnp.where(causal, a, 0.0)
    o = inter + _bdot(a, ih, ((1,), (0,)))
    o = o * lax.rsqrt(jnp.mean(o * o, axis=-1, keepdims=True) + RMS_EPS) * nw
    return o * (gh * jax.nn.sigmoid(gh)), st_out


def _to_heads(x):
    return jnp.concatenate([x[None, :, h * LANE:(h + 1) * LANE] for h in range(HG_HEADS)], axis=0)


def _from_heads(x):
    return jnp.concatenate([x[h] for h in range(HG_HEADS)], axis=1)


def _hdot(a, b, ca, cb):
    return lax.dot_general(a.astype(BF), b.astype(BF), (((ca,), (cb,)), ((0,), (0,))), preferred_element_type=F32)


def _hg_chunk_all(qh, fh, ih, gh, lbraw, nw, st):
    C = HG_CHUNK
    mx = jnp.maximum(lbraw[0:1], lbraw[1:2])
    e0 = jnp.exp(lbraw[0:1] - mx)
    e1 = jnp.exp(lbraw[1:2] - mx)
    lb = e0 / (e0 + e1)
    f = lb + (1.0 - lb) * jax.nn.sigmoid(fh)
    logf = jnp.log(f)
    k = 1.0 - f
    q = qh * jax.nn.sigmoid(qh)
    causal = lax.broadcasted_iota(jnp.int32, (C, C), 0) >= lax.broadcasted_iota(jnp.int32, (C, C), 1)
    b = lax.dot_general(causal.astype(F32), logf, (((1,), (0,)), ((), ())), precision=lax.Precision.HIGHEST,
                        preferred_element_type=F32)
    b_last = b[C - 1:C]
    b_ref = b[C // 2:C // 2 + 1]
    v3 = _to_heads(ih)
    st_out = st * _to_heads(jnp.exp(b_last)) + _hdot(v3, _to_heads(k * jnp.exp(b_last - b)), 1, 1)
    inter = _hdot(_to_heads(q * jnp.exp(b)), st, 2, 2)
    a = _hdot(_to_heads(q * jnp.exp(b - b_ref)), _to_heads(k * jnp.exp(b_ref - b)), 2, 2)
    a = jnp.where(causal[None], a, 0.0)
    o = inter + _hdot(a, v3, 2, 1)
    o = o * lax.rsqrt(jnp.mean(o * o, axis=-1, keepdims=True) + RMS_EPS) * nw
    return _from_heads(o) * (gh * jax.nn.sigmoid(gh)), st_out


def _head_cols(h):
    return slice(h * LANE, (h + 1) * LANE)


def _hg_in_specs(chunk_of):
    half = 4 * LANE
    specs = []
    for base in (QH0, FH0, IH0, GH0):
        for part in range(2):
            specs.append(pl.BlockSpec((HG_CHUNK, half), lambda n, col=base // 4 + part: (chunk_of(n), col)))
    return specs


def _hg_sections(sec):
    return [jnp.concatenate([sec[2 * s][...], sec[2 * s + 1][...]], axis=1).astype(F32) for s in range(4)]


def _hg_fwd(proj, lb, nw):
    S = proj.shape[0]
    C = HG_CHUNK
    NC = S // C

    def body(*refs):
        sec = refs[:8]
        lb_ref, nw_ref, o_ref, st_ref, st_scr = refs[8:]

        @pl.when(pl.program_id(0) == 0)
        def _():
            st_scr[...] = jnp.zeros_like(st_scr)

        st = st_scr[...]
        st_ref[:, 0] = st
        o, st_out = _hg_chunk_all(*_hg_sections(sec), lb_ref[...], nw_ref[...], st)
        o_ref[...] = o.astype(o_ref.dtype)
        st_scr[...] = st_out

    return _pcall(
        body, grid=(NC,),
        in_specs=_hg_in_specs(lambda n: n) + [pl.BlockSpec((2, HG_WIDTH), lambda n: (0, 0)), pl.BlockSpec((1, LANE), lambda n: (0, 0))],
        out_specs=[pl.BlockSpec((C, HG_WIDTH), lambda n: (n, 0)), pl.BlockSpec((HG_HEADS, 1, LANE, LANE), lambda n: (0, n, 0, 0))],
        out_shape=[SDS((S, HG_WIDTH), BF), SDS((HG_HEADS, NC, LANE, LANE), F32)],
        scratch_shapes=[pltpu.VMEM((HG_HEADS, LANE, LANE), F32)], name="hg_fwd")(*([proj] * 8), lb, nw)


def _hg_bwd(proj, lb, nw, states, do_hg, dproj):
    S = proj.shape[0]
    C = HG_CHUNK
    NC = S // C

    def body(*refs):
        sec = refs[:8]
        lb_ref, nw_ref, st_ref, do_ref, dp_in, dp_ref, dlb_ref, dnw_ref, dst_scr, stage, sem = refs[8:]
        n = pl.program_id(0)
        slot = n % 2

        def chunk_copies(s):
            rows = pl.ds(pl.multiple_of((NC - 1 - n) * C, C), C)
            return [pltpu.make_async_copy(stage.at[s, t], dp_ref.at[rows, pl.ds((QH0 + 8 * t) * LANE, HG_WIDTH)], sem.at[s, t])
                    for t in range(4)]

        @pl.when(n == 0)
        def _():
            dst_scr[...] = jnp.zeros_like(dst_scr)
            dlb_ref[...] = jnp.zeros_like(dlb_ref)
            dnw_ref[...] = jnp.zeros_like(dnw_ref)

        _, vjp = jax.vjp(_hg_chunk_all, *_hg_sections(sec), lb_ref[...], nw_ref[...], st_ref[:, 0])
        dq, df, di, dg, dlb, dnw, dst = vjp((do_ref[...].astype(F32), dst_scr[...]))
        dlb_ref[...] += dlb
        dnw_ref[...] += dnw
        dst_scr[...] = dst

        @pl.when(n >= 2)
        def _():
            for cp in chunk_copies(slot):
                cp.wait()

        for t, val in enumerate((dq, df, di, dg)):
            stage[slot, t] = val.astype(BF)
        for cp in chunk_copies(slot):
            cp.start()

        @pl.when(n == NC - 1)
        def _():
            for cp in chunk_copies(1 - slot) + chunk_copies(slot):
                cp.wait()

    rev = lambda n: NC - 1 - n
    ospec = pl.BlockSpec((C, HG_WIDTH), lambda n: (rev(n), 0))
    lbs = pl.BlockSpec((2, HG_WIDTH), lambda n: (0, 0))
    nws = pl.BlockSpec((1, LANE), lambda n: (0, 0))
    outs = _pcall(
        body, grid=(NC,),
        in_specs=_hg_in_specs(rev) + [lbs, nws, pl.BlockSpec((HG_HEADS, 1, LANE, LANE), lambda n: (0, rev(n), 0, 0)), ospec, _ANY],
        out_specs=[_ANY, lbs, nws],
        out_shape=[SDS((S, IN_WIDTH), BF), SDS((2, HG_WIDTH), F32), SDS((1, LANE), F32)], aliases={12: 0},
        scratch_shapes=[pltpu.VMEM((HG_HEADS, LANE, LANE), F32), pltpu.VMEM((2, 4, C, HG_WIDTH), BF), pltpu.SemaphoreType.DMA((2, 4))],
        name="hg_bwd")(*([proj] * 8), lb, nw, states, do_hg, dproj)
    return outs


def _merge(ga, gb, ba, bb):
    return jax.nn.sigmoid(ga) * ba + jax.nn.sigmoid(gb) * bb


_MT = 512


def _swiglu(a, b):
    return a * jax.nn.sigmoid(a) * b


def _cross(q, k, v):
    s = _bdot(q, k, ((1,), (1,))) * (HEAD_DIM ** -0.5)
    e = jnp.exp(s - jnp.max(s, axis=-1, keepdims=True))
    p = e / jnp.sum(e, axis=-1, keepdims=True)
    return _bdot(p, v, ((1,), (0,)))


def _cross_specs():
    q = pl.BlockSpec((_MT, LANE), lambda h, i: (i, h))
    k = pl.BlockSpec((MEM_LEN, LANE), lambda h, i: (0, h))
    v = pl.BlockSpec((MEM_LEN, LANE), lambda h, i: (0, CROSS_HEADS + h))
    return q, k, v


def _cross_fwd(qc, kvc):
    S = qc.shape[0]
    q, k, v = _cross_specs()

    def body(q_ref, k_ref, v_ref, o_ref):
        o_ref[...] = _cross(q_ref[...], k_ref[...], v_ref[...]).astype(o_ref.dtype)

    return _pcall(body, grid=(CROSS_HEADS, S // _MT), in_specs=[q, k, v], out_specs=q,
                          out_shape=SDS((S, CROSS_WIDTH), BF), name="cross_fwd")(qc, kvc, kvc)


def _cross_bwd(qc, kvc, doc):
    S = qc.shape[0]
    q, k, v = _cross_specs()

    def body(q_ref, k_ref, v_ref, do_ref, dq_ref, dk_ref, dv_ref):
        @pl.when(pl.program_id(1) == 0)
        def _():
            dk_ref[...] = jnp.zeros_like(dk_ref)
            dv_ref[...] = jnp.zeros_like(dv_ref)

        _, vjp = jax.vjp(_cross, q_ref[...], k_ref[...], v_ref[...])
        dq, dk, dv = vjp(do_ref[...])
        dq_ref[...] = dq.astype(dq_ref.dtype)
        dk_ref[...] += dk
        dv_ref[...] += dv

    return _pcall(body, grid=(CROSS_HEADS, S // _MT), in_specs=[q, k, v, q], out_specs=[q, k, k],
                          out_shape=[SDS((S, CROSS_WIDTH), BF), SDS((MEM_LEN, CROSS_WIDTH), F32), SDS((MEM_LEN, CROSS_WIDTH), F32)],
                          name="cross_bwd")(qc, kvc, kvc, doc)


def _local_step(x, mem, tgt, p, ready):
    h = _rms_fwd(x, p["ln_mix"], "rms_mix")
    proj = _mm(h, p["w_in_a"], "nt", BF, "mm_proj_a", a_part=(0, 2))
    proj = _mm(h, p["w_in_b"], "nt", BF, "mm_proj_b", a_part=(1, 2), res=proj)
    o_att, lse = _attn_fwd(proj)
    o_hg, states = _hg_fwd(proj, p["hg_lb"], p["hg_norm"])
    ba = _mm(o_att, p["w_ba"], "nn", BF, "mm_ba")
    gates = [(proj, GA0 * LANE // _MT), (proj, GB0 * LANE // _MT)]

    def f32s(*vals):
        return [v.astype(F32) for v in vals]

    bb, merged = _mm_fused(o_hg, p["w_bb"], "nn", "mm_bb_merge", gates + [ba], [BF, BF], tn=_MT,
                           epi=lambda d, ga, gb, bav: (d, _merge(*f32s(ga, gb, bav), d)))
    x1 = _mm(merged, p["w_out"], "nn", F32, "mm_out", res=x)
    hc = _rms_fwd(x1, p["ln_cross"], "rms_cross")
    qc = _mm(hc, p["wq"], "nn", F32, "mm_q")
    mn = _rms_fwd(mem, p["ln_mem"], "rms_mem")
    kvc = _mm(mn, p["wkv"], "nn", F32, "mm_kv")
    oc = _cross_fwd(qc, kvc)
    x2 = _mm(oc, p["wo"], "nn", F32, "mm_o", res=x1)
    hf = _rms_fwd(x2, p["ln_ffn"], "rms_ffn")
    a = _mm(hf, p["w1"], "nt", BF, "mm_w1")
    b, g = _mm_fused(hf, p["w3"], "nt", "mm_w3_swiglu", [a], [BF, BF], lambda d, av: (d, _swiglu(av.astype(F32), d)))
    x3 = _mm(g, p["w2"], "nn", F32, "mm_w2", res=x2)
    loss, dx3, dx3b, d_ln_final = _final_loss(x3, p["ln_final"], tgt)

    gw, gs = {}, {"ln_final": d_ln_final}
    da, db = _mm_fused(dx3b, p["w2"], "nt", "mm_dg_swiglu", [a, b], [BF, BF],
                       lambda d, av, bv: jax.vjp(_swiglu, av.astype(F32), bv.astype(F32))[1](d))
    gw["w2"] = _mm(g, dx3b, "tn", BF, "mm_dw2")
    dhf = _mm(da, p["w1"], "nn", F32, "mm_dhf1")
    dhf = _mm(db, p["w3"], "nn", BF, "mm_dhf3", res=dhf)
    gw["w1"] = _mm(da, hf, "tn", BF, "mm_dw1")
    gw["w3"] = _mm(db, hf, "tn", BF, "mm_dw3")
    ready("ffn", gw)
    dx2, dx2b, gs["ln_ffn"] = _rms_bwd(x2, p["ln_ffn"], dhf, dx3, "rms_ffn_bwd")
    doc = _mm(dx2b, p["wo"], "nt", F32, "mm_doc")
    gw["wo"] = _mm(oc, dx2b, "tn", BF, "mm_dwo")
    dqc, dk, dv = _cross_bwd(qc, kvc, doc)
    dkvc = jnp.concatenate([dk, dv], axis=1)
    dhc = _mm(dqc, p["wq"], "nt", BF, "mm_dhc")
    gw["wq"] = _mm(hc, dqc, "tn", BF, "mm_dwq")
    dmn = _mm(dkvc, p["wkv"], "nt", F32, "mm_dmn")
    gw["wkv"] = _mm(mn, dkvc, "tn", BF, "mm_dwkv")
    ready("cross", gw)
    gs["ln_mem"] = _rms_bwd(mem, p["ln_mem"], dmn, None, "rms_mem_bwd")
    dx1, dx1b, gs["ln_cross"] = _rms_bwd(x1, p["ln_cross"], dhc, dx2, "rms_cross_bwd")
    def merge_back(d, ga, gb, bav, bbv):
        dga, dgb, dba_, dbb_ = jax.vjp(_merge, *f32s(ga, gb, bav, bbv))[1](d)
        return dba_, dbb_, dga, dgb

    dba, dbb, dproj = _mm_fused(dx1b, p["w_out"], "nt", "mm_dmerged_merge", gates + [ba, bb], [BF, BF], merge_back, tn=_MT,
                                scatter=(SDS((x.shape[0], IN_WIDTH), BF), [GA0 * LANE, GB0 * LANE]))
    gw["w_out"] = _mm(merged, dx1b, "tn", BF, "mm_dwout")
    do_att = _mm(dba, p["w_ba"], "nt", F32, "mm_doatt")
    gw["w_ba"] = _mm(o_att, dba, "tn", BF, "mm_dwba")
    do_hg = _mm(dbb, p["w_bb"], "nt", F32, "mm_dohg")
    gw["w_bb"] = _mm(o_hg, dbb, "tn", BF, "mm_dwbb")
    ready("mix", gw)
    dproj = _attn_bwd(proj, o_att, lse, do_att, dproj)
    dproj, gs["hg_lb"], gs["hg_norm"] = _hg_bwd(proj, p["hg_lb"], p["hg_norm"], states, do_hg, dproj)
    for q in range(IN_PARTS):
        gw[f"w_in_q{q}"] = _mm(dproj, h, "tn", BF, f"mm_dwin_q{q}", b_part=(q, IN_PARTS))
        ready(f"in_q{q}", gw)
        if q == 0:
            ready("mixers_done", gw)
    dh_a = _mm(dproj, p["w_in_a"], "nn", BF, "mm_dh_a")
    ready("dh_half", gw)
    dh = (dh_a, _mm(dproj, p["w_in_b"], "nn", BF, "mm_dh_b"))
    grad_x, _, gs["ln_mix"] = _rms_bwd(x, p["ln_mix"], dh, dx1, "rms_mix_bwd")
    return loss, grad_x, gs


_MATS = (
    ("w_in_a", "w_in", IN_WIDTH, D_MODEL // 2, "row", IN_WIDTH // N_DEV, True),
    ("w_in_b", "w_in", IN_WIDTH, D_MODEL // 2, "row", IN_WIDTH // N_DEV, True),
    ("w_ba", "w_branch_a", ATT_OUT, D_MODEL, "col", D_MODEL // N_DEV, False),
    ("w_bb", "w_branch_b", HG_WIDTH, D_MODEL, "col", D_MODEL // N_DEV, False),
    ("w_out", "w_out", D_MODEL, D_MODEL, "row", D_MODEL // N_DEV, False),
    ("wq", "wq_cross", D_MODEL, CROSS_WIDTH, "row", D_MODEL // N_DEV, False),
    ("wkv", "wkv_cross", D_MODEL, 2 * CROSS_WIDTH, "row", D_MODEL // N_DEV, False),
    ("wo", "wo_cross", CROSS_WIDTH, D_MODEL, "col", D_MODEL // N_DEV, False),
    ("w1", "w1", D_FF, D_MODEL, "row", D_FF // N_DEV, True),
    ("w3", "w3", D_FF, D_MODEL, "row", D_FF // N_DEV, True),
    ("w2", "w2", D_FF, D_MODEL, "row", D_FF // N_DEV, False),
    ("w_in_q0", "w_in", IN_WIDTH, D_MODEL // 2, "row", IN_WIDTH // N_DEV, True),
    ("w_in_q1", "w_in", IN_WIDTH, D_MODEL // 2, "row", IN_WIDTH // N_DEV, True),
)
IN_PARTS = 2
_INDEX = {m[0]: i for i, m in enumerate(_MATS)}


def _unit_shape(i):
    _, _, K, N, kind, sh, _ = _MATS[i]
    return (sh, N) if kind == "row" else (K, sh)


def _win(ref, i, chip, par, half=None):
    _, _, K, _, kind, sh, _ = _MATS[i]
    if kind == "row":
        rows, start = (sh, 0) if half is None else (sh // 2, half * (sh // 2))
        return ref.at[pl.ds(pl.multiple_of((2 * chip + par) * sh + start, 16), rows), :]
    rows, start = (K, 0) if half is None else (K // 2, half * (K // 2))
    return ref.at[pl.ds(start, rows), pl.ds(pl.multiple_of((2 * chip + par) * sh, LANE), sh)]


def _my_place():
    x, y, c = lax.axis_index("x"), lax.axis_index("y"), lax.axis_index("c")
    chips = [(1 - x, y), (x, 1 - y), (1 - x, 1 - y)]
    return x, y, c, chips


def _cast_unit(w, name, half=None):
    K, N = w.shape
    if half is not None:
        N = N // 2
    tr = _pick(K, (256, 400, 352))

    def body(w_ref, o_ref):
        o_ref[...] = w_ref[...].astype(BF)

    src = pl.BlockSpec((tr, N), lambda i: (i, half or 0))
    dst = pl.BlockSpec((tr, N), lambda i: (i, 0))
    return _pcall(body, grid=(K // tr,), in_specs=[src], out_specs=dst, out_shape=SDS((K, N), BF), name=name)(w)


def _handshake(peers):
    barrier = pltpu.get_barrier_semaphore()
    for peer in peers:
        pl.semaphore_signal(barrier, inc=1, device_id=peer, device_id_type=MESH)
    pl.semaphore_wait(barrier, len(peers))


def _sequencer(body, out_type, sems, cid, name):
    return pl.kernel(body, out_type=out_type, mesh=plsc.ScalarSubcoreMesh(axis_name="seq", num_cores=1),
                     scratch_types=sems, compiler_params=pltpu.CompilerParams(collective_id=cid), name=name)


def _all_gather(units, idx, cid, name):
    n = len(idx)

    def body(*refs):
        us, fulls = refs[:n], refs[n:2 * n]
        ssem, rsem, lsem = refs[2 * n:]
        x, y, c, _ = _my_place()
        xn, yn, dg = (1 - x, y), (x, 1 - y), (1 - x, 1 - y)
        me, sib = (x, y, c), (x, y, 1 - c)
        _handshake([sib, (*xn, c), (*yn, c)])

        def win(i, chip_xy, par, half=None):
            return _win(fulls[i], idx[i], 2 * chip_xy[0] + chip_xy[1], par, half)

        def rcopy(i, k, src, dst, to):
            return pltpu.make_async_remote_copy(src_ref=src, dst_ref=dst, send_sem=ssem.at[i, k], recv_sem=rsem.at[i, k],
                                                device_id=to, device_id_type=MESH)

        def landed(i, k, blk):
            rcopy(i, k, blk, blk, me).wait_recv()

        sends, locs = [], []

        def send(i, k, blk, to, src=None):
            cp = rcopy(i, k, blk if src is None else src, blk, to)
            cp.start()
            sends.append(cp)

        for i in range(n):
            mine = win(i, (x, y), c)
            loc = pltpu.make_async_copy(us[i], mine, lsem.at[i])
            loc.start()
            locs.append(loc)
            send(i, 0, mine, sib, src=us[i])
            send(i, 1, mine, (*xn, c), src=us[i])
            send(i, 2, mine, (*yn, c), src=us[i])
        for i in range(n):
            landed(i, 1, win(i, xn, c))
            send(i, 3, win(i, xn, c, 0), (*yn, c))
            send(i, 5, win(i, xn, c), sib)
            landed(i, 2, win(i, yn, c))
            send(i, 4, win(i, yn, c, 1), (*xn, c))
            send(i, 6, win(i, yn, c), sib)
        for i in range(n):
            landed(i, 3, win(i, dg, c, 0))
            landed(i, 4, win(i, dg, c, 1))
            send(i, 7, win(i, dg, c), sib)
        for i in range(n):
            landed(i, 0, win(i, (x, y), 1 - c))
            landed(i, 5, win(i, xn, 1 - c))
            landed(i, 6, win(i, yn, 1 - c))
            landed(i, 7, win(i, dg, 1 - c))
        for cp in sends:
            cp.wait_send()
        for loc in locs:
            loc.wait()

    out_type = [SDS((_MATS[i][2], _MATS[i][3]), BF) for i in idx]
    sems = [pltpu.SemaphoreType.DMA((n, 8)), pltpu.SemaphoreType.DMA((n, 8)), pltpu.SemaphoreType.DMA((n,))]
    return _sequencer(body, out_type, sems, cid, name)(*units)


def _rs_sibling(grads, idx, cid, name):
    n = len(idx)

    def body(*refs):
        gs, bufs = refs[:n], refs[n:2 * n]
        ssem, rsem = refs[2 * n:]
        x, y, c, _ = _my_place()
        _handshake([(x, y, 1 - c)])
        cps = []
        for i in range(n):
            for k in range(4):
                cp = pltpu.make_async_remote_copy(
                    src_ref=_win(gs[i], idx[i], k, 1 - c), dst_ref=bufs[i].at[k], send_sem=ssem.at[i, k], recv_sem=rsem.at[i, k],
                    device_id=(x, y, 1 - c), device_id_type=MESH)
                cp.start()
                cps.append(cp)
        for cp in cps:
            cp.wait()

    out_type = [SDS((4, *_unit_shape(i)), BF) for i in idx]
    sems = [pltpu.SemaphoreType.DMA((n, 4)), pltpu.SemaphoreType.DMA((n, 4))]
    return _sequencer(body, out_type, sems, cid, name)(*grads)


def _sibling_sum_tc(grad, i, cid, name):
    unit = _unit_shape(i)

    def body(g_ref, t_ref, theirs, mine, summed, ssem, rsem, lsem, osem):
        x, y, c, _ = _my_place()
        _handshake([(x, y, 1 - c)])
        sends, loads = [], []
        for k in range(4):
            sends.append(pltpu.make_async_remote_copy(
                src_ref=_win(g_ref, i, k, 1 - c), dst_ref=theirs.at[k], send_sem=ssem.at[k], recv_sem=rsem.at[k],
                device_id=(x, y, 1 - c), device_id_type=MESH))
            loads.append(pltpu.make_async_copy(_win(g_ref, i, k, c), mine.at[k], lsem.at[k]))
            sends[-1].start()
            loads[-1].start()
        stores = []
        for k in range(4):
            loads[k].wait()
            sends[k].wait_recv()
            summed[k] = (mine[k].astype(F32) + theirs[k].astype(F32)).astype(BF)
            stores.append(pltpu.make_async_copy(summed.at[k], t_ref.at[k], osem.at[k]))
            stores[-1].start()
        for k in range(4):
            sends[k].wait_send()
            stores[k].wait()

    block = pltpu.VMEM((4, *unit), BF)
    return _pcall(body, in_specs=[_ANY], out_specs=_ANY, out_shape=SDS((4, *unit), BF),
                  scratch_shapes=[block, block, block] + [pltpu.SemaphoreType.DMA((4,))] * 4,
                  compiler_params=_cp(VMEM_BIG, collective_id=cid), name=name)(grad)


def _chip_sum(g, sib, i, place, name):
    _, _, K, N, kind, sh, _ = _MATS[i]

    def body(pref, g_ref, s_ref, o_ref):
        o_ref[...] = (g_ref[...].astype(F32) + s_ref[...].astype(F32)).astype(BF)

    if kind == "row":
        tr = _pick(sh, (800, 352, 256))
        per = sh // tr
        grid = (4, per)
        g_spec = pl.BlockSpec((tr, N), lambda k, r, pref: ((2 * k + pref[0]) * per + r, 0))
        u_spec = pl.BlockSpec((None, tr, N), lambda k, r, pref: (k, r, 0))
    else:
        grid = (4,)
        g_spec = pl.BlockSpec((K, sh), lambda k, pref: (0, 2 * k + pref[0]))
        u_spec = pl.BlockSpec((None, K, sh), lambda k, pref: (k, 0, 0))
    return _pcall(body, grid=grid, in_specs=[g_spec, u_spec], out_specs=u_spec, out_shape=SDS((4, *_unit_shape(i)), BF),
                  prefetch=True, name=name)(place, g, sib)


def _rs_chips(parts, idx, cid, name):
    n = len(idx)

    def body(*refs):
        ts, bufs, relays = refs[:n], refs[n:2 * n], refs[2 * n:3 * n]
        ssem, rsem = refs[3 * n:]
        x, y, c, _ = _my_place()
        xn, yn, dg = (1 - x, y), (x, 1 - y), (1 - x, 1 - y)
        to_x, to_y, me = (*xn, c), (*yn, c), (x, y, c)
        _handshake([to_x, to_y])

        def rcopy(i, k, src, dst, to):
            return pltpu.make_async_remote_copy(src_ref=src, dst_ref=dst, send_sem=ssem.at[i, k], recv_sem=rsem.at[i, k],
                                                device_id=to, device_id_type=MESH)

        def halves(i):
            rows = _unit_shape(idx[i])[0] // 2
            return [pl.ds(0, rows), pl.ds(rows, rows)]

        sends = []

        def send(cp):
            cp.start()
            sends.append(cp)

        for i in range(n):
            h0, h1 = halves(i)
            diag = ts[i].at[2 * dg[0] + dg[1]]
            send(rcopy(i, 0, ts[i].at[2 * xn[0] + xn[1]], bufs[i].at[0], to_x))
            send(rcopy(i, 1, ts[i].at[2 * yn[0] + yn[1]], bufs[i].at[1], to_y))
            send(rcopy(i, 2, diag.at[h0], relays[i].at[0], to_x))
            send(rcopy(i, 3, diag.at[h1], relays[i].at[1], to_y))
        for i in range(n):
            h0, h1 = halves(i)
            rcopy(i, 2, relays[i].at[0], relays[i].at[0], me).wait_recv()
            send(rcopy(i, 4, relays[i].at[0], bufs[i].at[2, h0], to_y))
            rcopy(i, 3, relays[i].at[1], relays[i].at[1], me).wait_recv()
            send(rcopy(i, 5, relays[i].at[1], bufs[i].at[2, h1], to_x))
        for i in range(n):
            h0, h1 = halves(i)
            rcopy(i, 0, bufs[i].at[0], bufs[i].at[0], me).wait_recv()
            rcopy(i, 1, bufs[i].at[1], bufs[i].at[1], me).wait_recv()
            rcopy(i, 4, bufs[i].at[2, h0], bufs[i].at[2, h0], me).wait_recv()
            rcopy(i, 5, bufs[i].at[2, h1], bufs[i].at[2, h1], me).wait_recv()
        for cp in sends:
            cp.wait_send()

    def relay_shape(i):
        rows, cols = _unit_shape(i)
        return (2, rows // 2, cols)

    out_type = [SDS((3, *_unit_shape(i)), BF) for i in idx] + [SDS(relay_shape(i), BF) for i in idx]
    sems = [pltpu.SemaphoreType.DMA((n, 6)), pltpu.SemaphoreType.DMA((n, 6))]
    return _sequencer(body, out_type, sems, cid, name)(*parts)[:n]


def _adamw(w, g, m, v):
    m = ADAM_B1 * m + (1.0 - ADAM_B1) * g
    v = ADAM_B2 * v + (1.0 - ADAM_B2) * jnp.square(g)
    m_hat = m / (1.0 - ADAM_B1 ** ADAM_STEP)
    v_hat = v / (1.0 - ADAM_B2 ** ADAM_STEP)
    delta = -ADAM_LR * (m_hat / (jnp.sqrt(v_hat) + ADAM_EPS) + ADAM_WD * w)
    return delta, m, v


def _adam_mat(parts, recvs, w, m, v, place, name):
    R, C = w.shape
    n = len(parts)
    tr = _pick(R, (256, 160, 176))

    def body(pref, *refs):
        t_refs, r_refs = refs[:n], refs[n:2 * n]
        w_ref, m_ref, v_ref, g_out, d_out, m_out, v_out = refs[2 * n:]
        cols = [t[...].astype(F32) + r[0].astype(F32) + r[1].astype(F32) + r[2].astype(F32) for t, r in zip(t_refs, r_refs)]
        g = jnp.concatenate(cols, axis=1) if n > 1 else cols[0]
        d, mn, vn = _adamw(w_ref[...], g, m_ref[...], v_ref[...])
        g_out[...] = g
        d_out[...] = d
        m_out[...] = mn
        v_out[...] = vn

    t_spec = pl.BlockSpec((None, tr, C // n), lambda i, pref: (pref[1], i, 0))
    r_spec = pl.BlockSpec((3, tr, C // n), lambda i, pref: (0, i, 0))
    s_spec = pl.BlockSpec((tr, C), lambda i, pref: (i, 0))
    return _pcall(body, grid=(R // tr,), in_specs=[t_spec] * n + [r_spec] * n + [s_spec] * 3, out_specs=[s_spec] * 4,
                  out_shape=[SDS((R, C), F32)] * 4, prefetch=True, name=name)(place, *parts, *recvs, w, m, v)


_SMALL = (("ln_mix", "ln_mix_w", D_MODEL), ("ln_cross", "ln_cross_w", D_MODEL), ("ln_mem", "ln_mem_w", D_MODEL),
          ("ln_ffn", "ln_ffn_w", D_MODEL), ("ln_final", "ln_final_w", D_MODEL), ("hg_lb", "hg_lower_bounds", 2 * HG_WIDTH),
          ("hg_norm", "hg_norm_w", HEAD_DIM))
_PACK = sum(n for _, _, n in _SMALL) + LANE


def _small_sync(grow, wrow, mrow, vrow):
    def body(g_ref, w_ref, m_ref, v_ref, o_ref, gath, ssem, rsem):
        x, y, c, _ = _my_place()
        me = 4 * x + 2 * y + c
        gath[pl.ds(me, 1), :] = g_ref[...]
        cps = []
        for dlt in range(1, N_DEV):
            peer = (me + dlt) % N_DEV
            cp = pltpu.make_async_remote_copy(
                src_ref=g_ref, dst_ref=gath.at[pl.ds(me, 1), :], send_sem=ssem.at[peer], recv_sem=rsem.at[me],
                device_id=(peer // 4, (peer // 2) % 2, peer % 2), device_id_type=MESH)
            cp.start()
            cps.append(cp)
        for dlt in range(1, N_DEV):
            src = (me + dlt) % N_DEV
            pltpu.make_async_remote_copy(
                src_ref=g_ref, dst_ref=gath.at[pl.ds(src, 1), :], send_sem=ssem.at[src], recv_sem=rsem.at[src],
                device_id=(x, y, c), device_id_type=MESH).wait_recv()
        for cp in cps:
            cp.wait_send()
        g = gath[0:1, :]
        for j in range(1, N_DEV):
            g = g + gath[j:j + 1, :]
        d, mn, vn = _adamw(w_ref[...], g, m_ref[...], v_ref[...])
        o_ref[0:1, :] = g
        o_ref[1:2, :] = d
        o_ref[2:3, :] = mn
        o_ref[3:4, :] = vn

    vm = pl.BlockSpec(memory_space=pltpu.VMEM)
    return _pcall(
        body, in_specs=[vm] * 4, out_specs=vm, out_shape=SDS((4, _PACK), F32),
        scratch_shapes=[pltpu.VMEM((N_DEV, _PACK), F32), pltpu.SemaphoreType.DMA((N_DEV,)), pltpu.SemaphoreType.DMA((N_DEV,))],
        name="small_sync")(grow, wrow, mrow, vrow)


def _pack_small(d, loss_row=None):
    parts = [d[k].reshape(1, n).astype(F32) for k, _, n in _SMALL]
    parts.append(loss_row if loss_row is not None else jnp.ones((1, LANE), F32))
    return jnp.concatenate(parts, axis=1)


def kernel(x, mem, ln_mix_w, w_in, hg_norm_w, hg_lower_bounds, w_branch_a, w_branch_b, w_out, ln_cross_w, ln_mem_w, wq_cross, wkv_cross, wo_cross, ln_ffn_w, w1, w3, w2, ln_final_w, loss_target, m_ln_mix_w, m_w_in, m_hg_norm_w, m_hg_lower_bounds, m_w_branch_a, m_w_branch_b, m_w_out, m_ln_cross_w, m_ln_mem_w, m_wq_cross, m_wkv_cross, m_wo_cross, m_ln_ffn_w, m_w1, m_w3, m_w2, m_ln_final_w, v_ln_mix_w, v_w_in, v_hg_norm_w, v_hg_lower_bounds, v_w_branch_a, v_w_branch_b, v_w_out, v_ln_cross_w, v_ln_mem_w, v_wq_cross, v_wkv_cross, v_wo_cross, v_ln_ffn_w, v_w1, v_w3, v_w2, v_ln_final_w):
    given = dict(locals())
    place = jnp.stack([lax.axis_index("c"), 2 * lax.axis_index("x") + lax.axis_index("y")]).astype(jnp.int32)

    _ORDER[0] = None
    index = _INDEX

    def local_view(a, i):
        return a[0].T if _MATS[i][6] else a[0]

    shard = {key: local_view(given[ref], i) for i, (key, ref, *_) in enumerate(_MATS)}

    p = {}
    col_half = {"w_in_a": 0, "w_in_b": 1}
    waves = (("in_a", ["w_in_a"], 1), ("in_b", ["w_in_b"], 12), ("mid", ["w_ba", "w_bb", "w_out", "wq", "wkv", "wo"], 2),
             ("w1", ["w1"], 11), ("w3", ["w3"], 15), ("w2", ["w2"], 16))
    for wave, keys, cid in waves:
        units = [_cast_unit(shard[k], "cast_" + k, col_half.get(k)) for k in keys]
        p.update(zip(keys, _all_gather(units, [index[k] for k in keys], cid, "all_gather_" + wave)))
    small_w = {"ln_mix": ln_mix_w, "ln_cross": ln_cross_w, "ln_mem": ln_mem_w, "ln_ffn": ln_ffn_w,
               "ln_final": ln_final_w.reshape(1, D_MODEL), "hg_lb": hg_lower_bounds, "hg_norm": hg_norm_w}
    p.update(small_w)

    groups = {"ffn": ["w2", "w1", "w3"], "cross": ["wo", "wq", "wkv"], "mix": ["w_out", "w_ba", "w_bb"]}
    cids = {"ffn": (3, 4), "cross": (5, 6), "mix": (7, 8)}
    for q in range(IN_PARTS):
        groups[f"in_q{q}"] = [f"w_in_q{q}"]
        cids[f"in_q{q}"] = ((9, 10), (13, 14), (17, 18), (19, 20))[q]
    sib, parts, recv, out = {}, {}, {}, {}

    grads = {}

    def chip_stage(name):
        keys = groups[name]
        parts[name] = [_chip_sum(grads[k], s, index[k], place, "chip_sum_" + k) for k, s in zip(keys, sib[name])]
        recv[name] = list(_rs_chips(parts[name], [index[k] for k in keys], cids[name][1], "rs_chips_" + name))

    def adam(ref, i, part_list, recv_list):
        res = _adam_mat(part_list, recv_list, shard[_MATS[i][0]], local_view(given["m_" + ref], i), local_view(given["v_" + ref], i),
                        place, "adam_" + ref)
        out[ref] = [(a.T if _MATS[i][6] else a)[None] for a in res]

    def adam_stage(name, only=None, skip=None):
        for n, (k, t, r) in enumerate(zip(groups[name], parts[name], recv[name])):
            if (only is None or n == only) and n != skip:
                adam(_MATS[index[k]][1], index[k], [t], [r])

    def ready(name, gw):
        if name == "mixers_done":
            adam_stage("ffn", only=0)
            chip_stage("mix")
            adam_stage("ffn", skip=0)
            adam_stage("cross")
            return
        if name == "dh_half":
            adam_stage("mix")
            return
        keys = groups[name]
        grads.update({k: gw[k] for k in keys})
        if name.startswith("in_q"):
            parts[name] = [_sibling_sum_tc(gw[k], index[k], cids[name][0], "sibling_sum_" + k) for k in keys]
            recv[name] = list(_rs_chips(parts[name], [index[k] for k in keys], cids[name][1], "rs_chips_" + name))
            if name == "in_q1":
                _ALSO_AFTER.extend(recv["in_q0"])
            return
        sib[name] = _rs_sibling([gw[k] for k in keys], [index[k] for k in keys], cids[name][0], "rs_sibling_" + name)
        if name == "cross":
            chip_stage("ffn")
        if name == "mix":
            chip_stage("cross")

    loss_row, grad_x, gs = _local_step(x[0], mem[0], loss_target[0], p, ready)

    sm = {k: given["m_" + ref] for k, ref, _ in _SMALL}
    sv = {k: given["v_" + ref] for k, ref, _ in _SMALL}
    synced = _small_sync(_pack_small(gs, loss_row), _pack_small(small_w), _pack_small(sm), _pack_small(sv))
    in_groups = [f"in_q{q}" for q in range(IN_PARTS)]
    adam("w_in", index["w_in_a"], sum((parts[g] for g in in_groups), []), sum((recv[g] for g in in_groups), []))
    off = 0
    for k, ref, n in _SMALL:
        shape = given[ref].shape
        out[ref] = [synced[r, off:off + n].reshape(shape) for r in range(4)]
        off += n
    loss = synced[0, off]

    order = ["ln_mix_w", "w_in", "hg_norm_w", "hg_lower_bounds", "w_branch_a", "w_branch_b", "w_out", "ln_cross_w", "ln_mem_w",
             "wq_cross", "wkv_cross", "wo_cross", "ln_ffn_w", "w1", "w3", "w2", "ln_final_w"]
    return (loss, grad_x[None], *[out[n][0] for n in order], *[out[n][1] for n in order],
            *[out[n][2] for n in order], *[out[n][3] for n in order])
```

```python
import jax
import jax.numpy as jnp
from jax import lax
from jax.experimental import pallas as pl
from jax.experimental.pallas import tpu as pltpu
from jax.experimental.pallas import tpu_sc as plsc

F32 = jnp.float32
BF = jnp.bfloat16
SDS = jax.ShapeDtypeStruct

D_MODEL = 2048
SEQ = 2048
HEAD_DIM = 128
MEM_LEN = 256
ATT_GROUPS = ((128, 1), (512, 4), (2048, 16))
ATT_HEADS = 4
ATT_WIDTH = 1536
ATT_OUT = 512
HG_HEADS = 8
HG_WIDTH = 1024
HG_CHUNK = 64
IN_WIDTH = 12800
CROSS_HEADS = 4
CROSS_WIDTH = 512
D_FF = 5632
RMS_EPS = 1e-6
ADAM_LR = 0.001
ADAM_B1 = 0.9
ADAM_B2 = 0.999
ADAM_EPS = 1e-08
ADAM_WD = 0.01
ADAM_STEP = 10
N_DEV = 8

LANE = 128
QA0, KA0, VA0 = 0, 12, 24
QH0, FH0, IH0, GH0 = 36, 44, 52, 60
GA0, GB0 = 68, 84

VMEM_BIG = 56 * 1024 * 1024
MESH = pl.DeviceIdType.MESH


def _cp(vmem=None, **kw):
    if vmem is not None:
        kw["vmem_limit_bytes"] = vmem
    return pltpu.CompilerParams(**kw)


def _pick(n, cands):
    for c in cands:
        if n % c == 0:
            return c
    raise ValueError(f"no tile for {n}")


_ANY = pl.BlockSpec(memory_space=pl.ANY)
_ORDER = [None]
_ALSO_AFTER = []


def _pcall(body, *, name, in_specs, out_specs, out_shape, grid=(), scratch_shapes=(), compiler_params=None, prefetch=False,
           aliases=None):
    def run(*args):
        dep = _ORDER[0]
        specs, operands, kernel_fn = list(in_specs), list(args), body
        deps = [] if dep is None or any(dep is a for a in operands) else [dep]
        deps += _ALSO_AFTER
        del _ALSO_AFTER[:]
        if deps:
            at, nd = len(operands), len(deps)
            specs += [_ANY] * nd
            operands += deps

            def kernel_fn(*refs):
                return body(*refs[:at], *refs[at + nd:])
        if prefetch:
            call = pl.pallas_call(
                kernel_fn, out_shape=out_shape, name=name, compiler_params=compiler_params,
                grid_spec=pltpu.PrefetchScalarGridSpec(num_scalar_prefetch=1, grid=grid, in_specs=specs, out_specs=out_specs,
                                                       scratch_shapes=scratch_shapes))
        else:
            call = pl.pallas_call(kernel_fn, grid=grid, in_specs=specs, out_specs=out_specs, out_shape=out_shape,
                                  scratch_shapes=scratch_shapes, compiler_params=compiler_params, name=name,
                                  input_output_aliases=aliases or {})
        out = call(*operands)
        _ORDER[0] = out[0] if isinstance(out, (list, tuple)) else out
        return out
    return run


def _mm(a, b, form, out_dtype, name, res=None, tiles=None, a_part=(0, 1), b_part=(0, 1)):
    a_shape = (a.shape[0], a.shape[1] // a_part[1])
    b_shape = (b.shape[0], b.shape[1] // b_part[1])
    if form == "nn":
        (M, K), (K2, N) = a_shape, b_shape
    elif form == "nt":
        (M, K), (N, K2) = a_shape, b_shape
    else:
        (K, M), (K2, N) = a_shape, b_shape
    assert K == K2, (a.shape, b.shape, form)
    if tiles is None:
        tm = _pick(M, (1024, 1280, 1408, 512, 256))
        tn = _pick(N, (1280, 1024, 1408, 512, 256))
        tk = K if K <= 2048 else _pick(K, (D_FF, 2560))
        if tk == D_FF:
            tm = _pick(M, (512, 256))
    else:
        tm, tn, tk = tiles
    nk = K // tk
    ah, bh = a_part[0], b_part[0]
    if form == "nn":
        a_spec = pl.BlockSpec((tm, tk), lambda i, j, k: (i, k + ah * nk))
        b_spec = pl.BlockSpec((tk, tn), lambda i, j, k: (k, j + bh * (N // tn)))
        dims = ((1,), (0,))
    elif form == "nt":
        a_spec = pl.BlockSpec((tm, tk), lambda i, j, k: (i, k + ah * nk))
        b_spec = pl.BlockSpec((tn, tk), lambda i, j, k: (j, k + bh * nk))
        dims = ((1,), (1,))
    else:
        a_spec = pl.BlockSpec((tk, tm), lambda i, j, k: (k, i + ah * (M // tm)))
        b_spec = pl.BlockSpec((tk, tn), lambda i, j, k: (k, j + bh * (N // tn)))
        dims = ((0,), (0,))
    o_spec = pl.BlockSpec((tm, tn), lambda i, j, k: (i, j))
    in_specs = [a_spec, b_spec]
    args = [a, b]
    if res is not None:
        in_specs.append(o_spec)
        args.append(res)

    def body(*refs):
        a_ref, b_ref = refs[0], refs[1]
        r_ref = refs[2] if res is not None else None
        o_ref = refs[3] if res is not None else refs[2]
        acc = refs[-1] if nk > 1 else None
        k = pl.program_id(2)
        d = lax.dot_general(a_ref[...].astype(BF), b_ref[...].astype(BF), (dims, ((), ())), preferred_element_type=F32)

        def finish(r):
            if res is not None:
                r = r + r_ref[...].astype(F32)
            o_ref[...] = r.astype(o_ref.dtype)

        if nk == 1:
            finish(d)
        else:
            @pl.when(k == 0)
            def _():
                acc[...] = d

            @pl.when((k > 0) & (k < nk - 1))
            def _():
                acc[...] += d

            @pl.when(k == nk - 1)
            def _():
                finish(acc[...] + d)

    return _pcall(
        body, grid=(M // tm, N // tn, nk), in_specs=in_specs, out_specs=o_spec,
        out_shape=SDS((M, N), out_dtype), scratch_shapes=[pltpu.VMEM((tm, tn), F32)] if nk > 1 else [],
        compiler_params=_cp(VMEM_BIG, dimension_semantics=("parallel", "parallel", "arbitrary")),
        name=name)(*args)


_MXU_COLS = 256


def _mm_fused(a, b, form, name, extras, out_dtypes, epi, tn=None, scatter=None):
    M, K = a.shape
    N = b.shape[1] if form == "nn" else b.shape[0]
    assert K <= 2048 and K == (b.shape[0] if form == "nn" else b.shape[1])
    tm = _pick(M, (1024, 512, 256))
    tn = tn or _pick(N, (1280, 1024, 1408, 512, 256))
    ne, no = len(extras), len(out_dtypes)
    bases = scatter[1] if scatter else []
    dims = ((1,), (0,)) if form == "nn" else ((1,), (1,))

    def body(*refs):
        a_ref, b_ref, ex, outs = refs[0], refs[1], refs[2:2 + ne], refs[2 + ne:2 + ne + no]
        av = a_ref[...].astype(BF)
        if scatter:
            wide_ref, stage, sem = refs[2 + ne + no:]
        for c0 in range(0, tn, _MXU_COLS):
            cols = slice(c0, min(c0 + _MXU_COLS, tn))
            bc = b_ref[:, cols] if form == "nn" else b_ref[cols, :]
            d = lax.dot_general(av, bc.astype(BF), (dims, ((), ())), preferred_element_type=F32)
            res = epi(d, *[e[:, cols] for e in ex])
            for o_ref, r in zip(outs, res[:no]):
                o_ref[:, cols] = r.astype(o_ref.dtype)
            for s, r in enumerate(res[no:]):
                stage[s, :, cols] = r.astype(BF)
        if scatter:
            rows = pl.ds(pl.multiple_of(pl.program_id(0) * tm, tm), tm)
            cps = [pltpu.make_async_copy(stage.at[s], wide_ref.at[rows, pl.ds(pl.multiple_of(base + pl.program_id(1) * tn, LANE), tn)],
                                         sem.at[s]) for s, base in enumerate(bases)]
            for cp in cps:
                cp.start()
            for cp in cps:
                cp.wait()

    a_spec = pl.BlockSpec((tm, K), lambda i, j: (i, 0))
    b_spec = pl.BlockSpec((K, tn), lambda i, j: (0, j)) if form == "nn" else pl.BlockSpec((tn, K), lambda i, j: (j, 0))
    o_spec = pl.BlockSpec((tm, tn), lambda i, j: (i, j))
    ex_specs, ex_args = [], []
    for e in extras:
        arr, off = e if isinstance(e, tuple) else (e, 0)
        ex_specs.append(pl.BlockSpec((tm, tn), lambda i, j, off=off: (i, off + j)))
        ex_args.append(arr)
    out_specs, out_shape, scratch = [o_spec] * no, [SDS((M, N), dt) for dt in out_dtypes], []
    if scatter:
        out_specs, out_shape = out_specs + [_ANY], out_shape + [scatter[0]]
        scratch = [pltpu.VMEM((len(bases), tm, tn), BF), pltpu.SemaphoreType.DMA((len(bases),))]
    return _pcall(
        body, grid=(M // tm, N // tn), in_specs=[a_spec, b_spec] + ex_specs, out_specs=out_specs, out_shape=out_shape,
        scratch_shapes=scratch, compiler_params=_cp(VMEM_BIG, dimension_semantics=("parallel", "parallel")),
        name=name)(a, b, *ex_args)


def _rms(x, w):
    return x * lax.rsqrt(jnp.mean(x * x, axis=-1, keepdims=True) + RMS_EPS) * w


def _rms_fwd(x, w, name):
    R, D = x.shape
    tr = 256

    def body(x_ref, w_ref, o_ref):
        o_ref[...] = _rms(x_ref[...], w_ref[...]).astype(o_ref.dtype)

    return _pcall(
        body, grid=(R // tr,),
        in_specs=[pl.BlockSpec((tr, D), lambda i: (i, 0)), pl.BlockSpec((1, D), lambda i: (0, 0))],
        out_specs=pl.BlockSpec((tr, D), lambda i: (i, 0)), out_shape=SDS((R, D), BF), name=name)(x, w)


def _rms_bwd(x, w, dh, dres, name):
    R, D = x.shape
    tr = 256
    want_dx = dres is not None
    dhs = dh if isinstance(dh, (tuple, list)) else (dh,)
    nd = len(dhs)

    def body(*refs):
        x_ref, w_ref, dh_refs = refs[0], refs[1], refs[2:2 + nd]
        if want_dx:
            dr_ref, dx_ref, dxb_ref, dw_ref = refs[2 + nd:]
        else:
            dw_ref = refs[2 + nd]

        @pl.when(pl.program_id(0) == 0)
        def _():
            dw_ref[...] = jnp.zeros_like(dw_ref)

        dhv = jnp.concatenate([r[...].astype(F32) for r in dh_refs], axis=1) if nd > 1 else dh_refs[0][...].astype(F32)
        _, vjp = jax.vjp(_rms, x_ref[...], w_ref[...])
        dx, dw = vjp(dhv)
        dw_ref[...] += dw
        if want_dx:
            dx = dx + dr_ref[...]
            dx_ref[...] = dx
            dxb_ref[...] = dx.astype(BF)

    row = pl.BlockSpec((tr, D), lambda i: (i, 0))
    part = pl.BlockSpec((tr, D // nd), lambda i: (i, 0))
    vec = pl.BlockSpec((1, D), lambda i: (0, 0))
    if want_dx:
        return _pcall(body, grid=(R // tr,), in_specs=[row, vec] + [part] * nd + [row], out_specs=[row, row, vec],
                      out_shape=[SDS((R, D), F32), SDS((R, D), BF), SDS((1, D), F32)], name=name)(x, w, *dhs, dres)
    return _pcall(body, grid=(R // tr,), in_specs=[row, vec] + [part] * nd, out_specs=vec,
                  out_shape=SDS((1, D), F32), name=name)(x, w, *dhs)


def _final_loss(x3, w, tgt):
    R, D = x3.shape
    tr = 256

    def loss_fn(xv, wv, tv):
        err = _rms(xv, wv) - tv
        return 0.5 * jnp.sum(jnp.mean(err * err, axis=-1))

    def body(x_ref, w_ref, t_ref, loss_ref, dx_ref, dxb_ref, dw_ref):
        @pl.when(pl.program_id(0) == 0)
        def _():
            dw_ref[...] = jnp.zeros_like(dw_ref)
            loss_ref[...] = jnp.zeros_like(loss_ref)

        tv = t_ref[...]
        val, vjp = jax.vjp(lambda a, b: loss_fn(a, b, tv), x_ref[...], w_ref[...])
        dx, dw = vjp(jnp.ones((), F32))
        dx_ref[...] = dx
        dxb_ref[...] = dx.astype(BF)
        dw_ref[...] += dw
        loss_ref[...] += jnp.full(loss_ref.shape, val, F32)

    row = pl.BlockSpec((tr, D), lambda i: (i, 0))
    vec = pl.BlockSpec((1, D), lambda i: (0, 0))
    one = pl.BlockSpec((1, LANE), lambda i: (0, 0))
    return _pcall(body, grid=(R // tr,), in_specs=[row, vec, row], out_specs=[one, row, row, vec],
                          out_shape=[SDS((1, LANE), F32), SDS((R, D), F32), SDS((R, D), BF), SDS((1, D), F32)],
                          name="final_loss")(x3, w, tgt)


NEG = -1e30


def _att_blocks(S, d):
    L = S // d
    nb = L // LANE
    return nb, (2 if nb > 1 else 1)


def _att_rows(S, d, idx):
    nb, nkb = _att_blocks(S, d)
    r = idx // nb
    n = idx % nb
    kb = jnp.maximum(n - 1, 0)
    if d == 1:
        qrows = pl.ds(pl.multiple_of(n * LANE, LANE), LANE)
        krows = pl.ds(pl.multiple_of(kb * LANE, LANE), LANE * nkb)
    else:
        qrows = pl.ds(r + n * (LANE * d), LANE, stride=d)
        krows = pl.ds(r + kb * (LANE * d), LANE * nkb, stride=d)
    qpos = n * LANE + lax.broadcasted_iota(jnp.int32, (LANE, LANE * nkb), 0)
    kpos = kb * LANE + lax.broadcasted_iota(jnp.int32, (LANE, LANE * nkb), 1)
    return qrows, krows, qpos - kpos


def _slab(col):
    return pl.BlockSpec((SEQ, LANE), lambda h, col=col: (0, col + h))


def _widen(slabs, wide):
    for t, ref in enumerate(slabs):
        wide[t] = ref[...].astype(F32)
    return tuple(tuple(wide.at[3 * kind + g] for g in range(3)) for kind in range(3))


def _attn_fwd(proj):
    S = proj.shape[0]
    scale = HEAD_DIM ** -0.5

    def body(q0, q1, q2, k0, k1, k2, v0, v1, v2, o_ref, lse_ref, og, lg, wide):
        qs, ks, vs = _widen((q0, q1, q2, k0, k1, k2, v0, v1, v2), wide)
        for g, (window, d) in enumerate(ATT_GROUPS):
            def blk(idx, carry, g=g, window=window, d=d):
                qrows, krows, dist = _att_rows(S, d, idx)
                q = qs[g][qrows, :].astype(BF)
                k = ks[g][krows, :].astype(BF)
                v = vs[g][krows, :].astype(BF)
                s = lax.dot_general(q, k, (((1,), (1,)), ((), ())), preferred_element_type=F32) * scale
                s = jnp.where((dist >= 0) & (dist <= window // d), s, NEG)
                m = jnp.max(s, axis=-1, keepdims=True)
                p = jnp.exp(s - m)
                l = jnp.sum(p, axis=-1, keepdims=True)
                o = jnp.dot((p / l).astype(BF), v, preferred_element_type=F32)
                og[g, qrows, :] = o
                lg[g, qrows, :] = jnp.broadcast_to(m + jnp.log(l), (LANE, LANE))
                return carry
            lax.fori_loop(0, S // LANE, blk, 0, unroll=True)

        def merge(t, carry):
            rows = pl.ds(pl.multiple_of(t * 256, 256), 256)
            l0, l1, l2 = lg[0, rows, :], lg[1, rows, :], lg[2, rows, :]
            m = jnp.maximum(jnp.maximum(l0, l1), l2)
            w0, w1, w2 = jnp.exp(l0 - m), jnp.exp(l1 - m), jnp.exp(l2 - m)
            den = w0 + w1 + w2
            o_ref[rows, :] = (w0 * og[0, rows, :] + w1 * og[1, rows, :] + w2 * og[2, rows, :]) / den
            lse_ref[rows, :] = m + jnp.log(den)
            return carry
        lax.fori_loop(0, S // 256, merge, 0)

    ins = [_slab(QA0 + 4 * g) for g in range(3)] + [_slab(KA0 + 4 * g) for g in range(3)] + [_slab(VA0 + 4 * g) for g in range(3)]
    out = pl.BlockSpec((SEQ, LANE), lambda h: (0, h))
    return _pcall(
        body, grid=(ATT_HEADS,), in_specs=ins, out_specs=[out, out],
        out_shape=[SDS((S, ATT_OUT), F32), SDS((S, ATT_OUT), F32)],
        scratch_shapes=[pltpu.VMEM((3, S, LANE), F32), pltpu.VMEM((3, S, LANE), F32), pltpu.VMEM((9, S, LANE), F32)],
        compiler_params=_cp(VMEM_BIG), name="attn_fwd")(*([proj] * 9))


def _attn_bwd(proj, o_att, lse, do_att, dproj):
    S = proj.shape[0]
    scale = HEAD_DIM ** -0.5

    def body(q0, q1, q2, k0, k1, k2, v0, v1, v2, o_ref, lse_ref, do_ref, dp_in, dp_ref, dqa, dka, dva, delta, stage, sem, wide):
        qs, ks, vs = _widen((q0, q1, q2, k0, k1, k2, v0, v1, v2), wide)
        head = pl.program_id(0)

        def slab_copies(g):
            return [pltpu.make_async_copy(stage.at[t], dp_ref.at[:, pl.ds(pl.multiple_of((base + 4 * g + head) * LANE, LANE), LANE)],
                                          sem.at[t]) for t, base in enumerate((QA0, KA0, VA0))]

        delta[...] = jnp.broadcast_to(jnp.sum(do_ref[...] * o_ref[...], axis=-1, keepdims=True), delta.shape)
        for g, (window, d) in enumerate(ATT_GROUPS):
            dka[...] = jnp.zeros_like(dka)
            dva[...] = jnp.zeros_like(dva)

            def blk(idx, carry, g=g, window=window, d=d):
                qrows, krows, dist = _att_rows(S, d, idx)
                q = qs[g][qrows, :].astype(BF)
                k = ks[g][krows, :].astype(BF)
                v = vs[g][krows, :].astype(BF)
                do = do_ref[qrows, :]
                s = lax.dot_general(q, k, (((1,), (1,)), ((), ())), preferred_element_type=F32) * scale
                p = jnp.exp(s - lse_ref[qrows, :][:, 0:1])
                p = jnp.where((dist >= 0) & (dist <= window // d), p, 0.0)
                dob = do.astype(BF)
                dva[krows, :] += lax.dot_general(p.astype(BF), dob, (((0,), (0,)), ((), ())), preferred_element_type=F32)
                dp = lax.dot_general(dob, v, (((1,), (1,)), ((), ())), preferred_element_type=F32)
                ds = (p * (dp - delta[qrows, :][:, 0:1]) * scale).astype(BF)
                dqa[qrows, :] = jnp.dot(ds, k, preferred_element_type=F32)
                dka[krows, :] += lax.dot_general(ds, q, (((0,), (0,)), ((), ())), preferred_element_type=F32)
                return carry
            lax.fori_loop(0, S // LANE, blk, 0, unroll=True)
            if g > 0:
                for cp in slab_copies(g - 1):
                    cp.wait()
            stage[0] = dqa[...].astype(BF)
            stage[1] = dka[...].astype(BF)
            stage[2] = dva[...].astype(BF)
            for cp in slab_copies(g):
                cp.start()
        for cp in slab_copies(len(ATT_GROUPS) - 1):
            cp.wait()

    cols = [QA0 + 4 * g for g in range(3)] + [KA0 + 4 * g for g in range(3)] + [VA0 + 4 * g for g in range(3)]
    ins = [_slab(c) for c in cols]
    one = pl.BlockSpec((SEQ, LANE), lambda h: (0, h))
    return _pcall(
        body, grid=(ATT_HEADS,), in_specs=ins + [one, one, one, _ANY], out_specs=_ANY,
        out_shape=SDS((S, IN_WIDTH), BF), aliases={12: 0},
        scratch_shapes=[pltpu.VMEM((S, LANE), F32)] * 4 + [pltpu.VMEM((3, S, LANE), BF), pltpu.SemaphoreType.DMA((3,)),
                                                           pltpu.VMEM((9, S, LANE), F32)],
        compiler_params=_cp(VMEM_BIG), name="attn_bwd")(*([proj] * 9), o_att, lse, do_att, dproj)


def _bdot(a, b, dims):
    return lax.dot_general(a.astype(BF), b.astype(BF), (dims, ((), ())), preferred_element_type=F32)


def _hg_chunk(qh, fh, ih, gh, lbraw, nw, st):
    C = HG_CHUNK
    mx = jnp.maximum(lbraw[0:1], lbraw[1:2])
    e0 = jnp.exp(lbraw[0:1] - mx)
    e1 = jnp.exp(lbraw[1:2] - mx)
    lb = e0 / (e0 + e1)
    f = lb + (1.0 - lb) * jax.nn.sigmoid(fh)
    logf = jnp.log(f)
    k = 1.0 - f
    q = qh * jax.nn.sigmoid(qh)
    causal = lax.broadcasted_iota(jnp.int32, (C, C), 0) >= lax.broadcasted_iota(jnp.int32, (C, C), 1)
    b = lax.dot_general(causal.astype(F32), logf, (((1,), (0,)), ((), ())), precision=lax.Precision.HIGHEST,
                        preferred_element_type=F32)
    b_last = b[C - 1:C]
    st_out = st * jnp.exp(b_last) + _bdot(ih, k * jnp.exp(b_last - b), ((0,), (0,)))
    inter = _bdot(q * jnp.exp(b), st, ((1,), (1,)))
    b_ref = b[C // 2:C // 2 + 1]
    a = _bdot(q * jnp.exp(b - b_ref), k * jnp.exp(b_ref - b), ((1,), (1,)))
    a = jnp.where(causal, a, 0.0)
    o = inter + _bdot(a, ih, ((1,), (0,)))
    o = o * lax.rsqrt(jnp.mean(o * o, axis=-1, keepdims=True) + RMS_EPS) * nw
    return o * (gh * jax.nn.sigmoid(gh)), st_out


def _to_heads(x):
    return jnp.concatenate([x[None, :, h * LANE:(h + 1) * LANE] for h in range(HG_HEADS)], axis=0)


def _from_heads(x):
    return jnp.concatenate([x[h] for h in range(HG_HEADS)], axis=1)


def _hdot(a, b, ca, cb):
    return lax.dot_general(a.astype(BF), b.astype(BF), (((ca,), (cb,)), ((0,), (0,))), preferred_element_type=F32)


def _hg_chunk_all(qh, fh, ih, gh, lbraw, nw, st):
    C = HG_CHUNK
    mx = jnp.maximum(lbraw[0:1], lbraw[1:2])
    e0 = jnp.exp(lbraw[0:1] - mx)
    e1 = jnp.exp(lbraw[1:2] - mx)
    lb = e0 / (e0 + e1)
    f = lb + (1.0 - lb) * jax.nn.sigmoid(fh)
    logf = jnp.log(f)
    k = 1.0 - f
    q = qh * jax.nn.sigmoid(qh)
    causal = lax.broadcasted_iota(jnp.int32, (C, C), 0) >= lax.broadcasted_iota(jnp.int32, (C, C), 1)
    b = lax.dot_general(causal.astype(F32), logf, (((1,), (0,)), ((), ())), precision=lax.Precision.HIGHEST,
                        preferred_element_type=F32)
    b_last = b[C - 1:C]
    b_ref = b[C // 2:C // 2 + 1]
    v3 = _to_heads(ih)
    st_out = st * _to_heads(jnp.exp(b_last)) + _hdot(v3, _to_heads(k * jnp.exp(b_last - b)), 1, 1)
    inter = _hdot(_to_heads(q * jnp.exp(b)), st, 2, 2)
    a = _hdot(_to_heads(q * jnp.exp(b - b_ref)), _to_heads(k * jnp.exp(b_ref - b)), 2, 2)
    a = jnp.where(causal[None], a, 0.0)
    o = inter + _hdot(a, v3, 2, 1)
    o = o * lax.rsqrt(jnp.mean(o * o, axis=-1, keepdims=True) + RMS_EPS) * nw
    return _from_heads(o) * (gh * jax.nn.sigmoid(gh)), st_out


def _head_cols(h):
    return slice(h * LANE, (h + 1) * LANE)


HG_STEP = 4
HG_ROWS = HG_STEP * HG_CHUNK


def _hg_in_specs(step_of):
    half = 4 * LANE
    specs = []
    for base in (QH0, FH0, IH0, GH0):
        for part in range(2):
            specs.append(pl.BlockSpec((HG_ROWS, half), lambda n, col=base // 4 + part: (step_of(n), col)))
    return specs


def _hg_sections(sec, c):
    rows = slice(c * HG_CHUNK, (c + 1) * HG_CHUNK)
    return [jnp.concatenate([sec[2 * s][rows, :], sec[2 * s + 1][rows, :]], axis=1).astype(F32) for s in range(4)]


def _hg_fwd(proj, lb, nw):
    S = proj.shape[0]
    C = HG_CHUNK
    NC = S // C

    def body(*refs):
        sec = refs[:8]
        lb_ref, nw_ref, o_ref, st_ref, st_scr = refs[8:]

        @pl.when(pl.program_id(0) == 0)
        def _():
            st_scr[...] = jnp.zeros_like(st_scr)

        st = st_scr[...]
        for c in range(HG_STEP):
            st_ref[:, c] = st
            o, st = _hg_chunk_all(*_hg_sections(sec, c), lb_ref[...], nw_ref[...], st)
            o_ref[c * C:(c + 1) * C, :] = o.astype(o_ref.dtype)
        st_scr[...] = st

    return _pcall(
        body, grid=(NC // HG_STEP,),
        in_specs=_hg_in_specs(lambda n: n) + [pl.BlockSpec((2, HG_WIDTH), lambda n: (0, 0)), pl.BlockSpec((1, LANE), lambda n: (0, 0))],
        out_specs=[pl.BlockSpec((HG_ROWS, HG_WIDTH), lambda n: (n, 0)),
                   pl.BlockSpec((HG_HEADS, HG_STEP, LANE, LANE), lambda n: (0, n, 0, 0))],
        out_shape=[SDS((S, HG_WIDTH), BF), SDS((HG_HEADS, NC, LANE, LANE), F32)],
        scratch_shapes=[pltpu.VMEM((HG_HEADS, LANE, LANE), F32)], name="hg_fwd")(*([proj] * 8), lb, nw)


def _hg_bwd(proj, lb, nw, states, do_hg, dproj):
    S = proj.shape[0]
    C = HG_CHUNK
    NC = S // C

    def body(*refs):
        sec = refs[:8]
        lb_ref, nw_ref, st_ref, do_ref, dp_in, dp_ref, dlb_ref, dnw_ref, dst_scr, stage, sem = refs[8:]
        n = pl.program_id(0)
        slot = n % 2

        def step_copies(s):
            rows = pl.ds(pl.multiple_of((NS - 1 - n) * HG_ROWS, HG_ROWS), HG_ROWS)
            return [pltpu.make_async_copy(stage.at[s, t], dp_ref.at[rows, pl.ds((QH0 + 8 * t) * LANE, HG_WIDTH)], sem.at[s, t])
                    for t in range(4)]

        @pl.when(n == 0)
        def _():
            dst_scr[...] = jnp.zeros_like(dst_scr)
            dlb_ref[...] = jnp.zeros_like(dlb_ref)
            dnw_ref[...] = jnp.zeros_like(dnw_ref)

        @pl.when(n >= 2)
        def _():
            for cp in step_copies(slot):
                cp.wait()

        dst = dst_scr[...]
        for c in reversed(range(HG_STEP)):
            rows = slice(c * C, (c + 1) * C)
            _, vjp = jax.vjp(_hg_chunk_all, *_hg_sections(sec, c), lb_ref[...], nw_ref[...], st_ref[:, c])
            dq, df, di, dg, dlb, dnw, dst = vjp((do_ref[rows, :].astype(F32), dst))
            dlb_ref[...] += dlb
            dnw_ref[...] += dnw
            for t, val in enumerate((dq, df, di, dg)):
                stage[slot, t, rows, :] = val.astype(BF)
        dst_scr[...] = dst
        for cp in step_copies(slot):
            cp.start()

        @pl.when(n == NS - 1)
        def _():
            for cp in step_copies(1 - slot) + step_copies(slot):
                cp.wait()

    NS = NC // HG_STEP
    rev = lambda n: NS - 1 - n
    ospec = pl.BlockSpec((HG_ROWS, HG_WIDTH), lambda n: (rev(n), 0))
    lbs = pl.BlockSpec((2, HG_WIDTH), lambda n: (0, 0))
    nws = pl.BlockSpec((1, LANE), lambda n: (0, 0))
    outs = _pcall(
        body, grid=(NS,),
        in_specs=_hg_in_specs(rev) + [lbs, nws, pl.BlockSpec((HG_HEADS, HG_STEP, LANE, LANE), lambda n: (0, rev(n), 0, 0)), ospec, _ANY],
        out_specs=[_ANY, lbs, nws],
        out_shape=[SDS((S, IN_WIDTH), BF), SDS((2, HG_WIDTH), F32), SDS((1, LANE), F32)], aliases={12: 0},
        scratch_shapes=[pltpu.VMEM((HG_HEADS, LANE, LANE), F32), pltpu.VMEM((2, 4, HG_ROWS, HG_WIDTH), BF),
                        pltpu.SemaphoreType.DMA((2, 4))],
        name="hg_bwd")(*([proj] * 8), lb, nw, states, do_hg, dproj)
    return outs


def _merge(ga, gb, ba, bb):
    return jax.nn.sigmoid(ga) * ba + jax.nn.sigmoid(gb) * bb


_MT = 512


def _swiglu(a, b):
    return a * jax.nn.sigmoid(a) * b


def _cross(q, k, v):
    s = _bdot(q, k, ((1,), (1,))) * (HEAD_DIM ** -0.5)
    e = jnp.exp(s - jnp.max(s, axis=-1, keepdims=True))
    p = e / jnp.sum(e, axis=-1, keepdims=True)
    return _bdot(p, v, ((1,), (0,)))


def _cross_specs():
    q = pl.BlockSpec((_MT, LANE), lambda h, i: (i, h))
    k = pl.BlockSpec((MEM_LEN, LANE), lambda h, i: (0, h))
    v = pl.BlockSpec((MEM_LEN, LANE), lambda h, i: (0, CROSS_HEADS + h))
    return q, k, v


def _cross_fwd(qc, kvc):
    S = qc.shape[0]
    q, k, v = _cross_specs()

    def body(q_ref, k_ref, v_ref, o_ref):
        o_ref[...] = _cross(q_ref[...], k_ref[...], v_ref[...]).astype(o_ref.dtype)

    return _pcall(body, grid=(CROSS_HEADS, S // _MT), in_specs=[q, k, v], out_specs=q,
                          out_shape=SDS((S, CROSS_WIDTH), BF), name="cross_fwd")(qc, kvc, kvc)


def _cross_bwd(qc, kvc, doc):
    S = qc.shape[0]
    q, k, v = _cross_specs()

    def body(q_ref, k_ref, v_ref, do_ref, dq_ref, dk_ref, dv_ref):
        @pl.when(pl.program_id(1) == 0)
        def _():
            dk_ref[...] = jnp.zeros_like(dk_ref)
            dv_ref[...] = jnp.zeros_like(dv_ref)

        _, vjp = jax.vjp(_cross, q_ref[...], k_ref[...], v_ref[...])
        dq, dk, dv = vjp(do_ref[...])
        dq_ref[...] = dq.astype(dq_ref.dtype)
        dk_ref[...] += dk
        dv_ref[...] += dv

    return _pcall(body, grid=(CROSS_HEADS, S // _MT), in_specs=[q, k, v, q], out_specs=[q, k, k],
                          out_shape=[SDS((S, CROSS_WIDTH), BF), SDS((MEM_LEN, CROSS_WIDTH), F32), SDS((MEM_LEN, CROSS_WIDTH), F32)],
                          name="cross_bwd")(qc, kvc, kvc, doc)


def _local_step(x, mem, tgt, p, ready):
    h = _rms_fwd(x, p["ln_mix"], "rms_mix")
    proj = _mm(h, p["w_in_a"], "nt", BF, "mm_proj_a", a_part=(0, 2))
    proj = _mm(h, p["w_in_b"], "nt", BF, "mm_proj_b", a_part=(1, 2), res=proj)
    o_att, lse = _attn_fwd(proj)
    o_hg, states = _hg_fwd(proj, p["hg_lb"], p["hg_norm"])
    ba = _mm(o_att, p["w_ba"], "nn", BF, "mm_ba")
    gates = [(proj, GA0 * LANE // _MT), (proj, GB0 * LANE // _MT)]

    def f32s(*vals):
        return [v.astype(F32) for v in vals]

    bb, merged = _mm_fused(o_hg, p["w_bb"], "nn", "mm_bb_merge", gates + [ba], [BF, BF], tn=_MT,
                           epi=lambda d, ga, gb, bav: (d, _merge(*f32s(ga, gb, bav), d)))
    x1 = _mm(merged, p["w_out"], "nn", F32, "mm_out", res=x)
    hc = _rms_fwd(x1, p["ln_cross"], "rms_cross")
    qc = _mm(hc, p["wq"], "nn", F32, "mm_q")
    mn = _rms_fwd(mem, p["ln_mem"], "rms_mem")
    kvc = _mm(mn, p["wkv"], "nn", F32, "mm_kv")
    oc = _cross_fwd(qc, kvc)
    x2 = _mm(oc, p["wo"], "nn", F32, "mm_o", res=x1)
    hf = _rms_fwd(x2, p["ln_ffn"], "rms_ffn")
    a = _mm(hf, p["w1"], "nt", BF, "mm_w1")
    b, g = _mm_fused(hf, p["w3"], "nt", "mm_w3_swiglu", [a], [BF, BF], lambda d, av: (d, _swiglu(av.astype(F32), d)))
    x3 = _mm(g, p["w2"], "nn", F32, "mm_w2", res=x2)
    loss, dx3, dx3b, d_ln_final = _final_loss(x3, p["ln_final"], tgt)

    gw, gs = {}, {"ln_final": d_ln_final}
    da, db = _mm_fused(dx3b, p["w2"], "nt", "mm_dg_swiglu", [a, b], [BF, BF],
                       lambda d, av, bv: jax.vjp(_swiglu, av.astype(F32), bv.astype(F32))[1](d))
    gw["w2"] = _mm(g, dx3b, "tn", BF, "mm_dw2")
    dhf = _mm(da, p["w1"], "nn", F32, "mm_dhf1")
    dhf = _mm(db, p["w3"], "nn", BF, "mm_dhf3", res=dhf)
    gw["w1"] = _mm(da, hf, "tn", BF, "mm_dw1")
    gw["w3"] = _mm(db, hf, "tn", BF, "mm_dw3")
    ready("ffn", gw)
    dx2, dx2b, gs["ln_ffn"] = _rms_bwd(x2, p["ln_ffn"], dhf, dx3, "rms_ffn_bwd")
    doc = _mm(dx2b, p["wo"], "nt", F32, "mm_doc")
    gw["wo"] = _mm(oc, dx2b, "tn", BF, "mm_dwo")
    dqc, dk, dv = _cross_bwd(qc, kvc, doc)
    dkvc = jnp.concatenate([dk, dv], axis=1)
    dhc = _mm(dqc, p["wq"], "nt", BF, "mm_dhc")
    gw["wq"] = _mm(hc, dqc, "tn", BF, "mm_dwq")
    dmn = _mm(dkvc, p["wkv"], "nt", F32, "mm_dmn")
    gw["wkv"] = _mm(mn, dkvc, "tn", BF, "mm_dwkv")
    ready("cross", gw)
    gs["ln_mem"] = _rms_bwd(mem, p["ln_mem"], dmn, None, "rms_mem_bwd")
    dx1, dx1b, gs["ln_cross"] = _rms_bwd(x1, p["ln_cross"], dhc, dx2, "rms_cross_bwd")
    def merge_back(d, ga, gb, bav, bbv):
        dga, dgb, dba_, dbb_ = jax.vjp(_merge, *f32s(ga, gb, bav, bbv))[1](d)
        return dba_, dbb_, dga, dgb

    dba, dbb, dproj = _mm_fused(dx1b, p["w_out"], "nt", "mm_dmerged_merge", gates + [ba, bb], [BF, BF], merge_back, tn=_MT,
                                scatter=(SDS((x.shape[0], IN_WIDTH), BF), [GA0 * LANE, GB0 * LANE]))
    gw["w_out"] = _mm(merged, dx1b, "tn", BF, "mm_dwout")
    do_att = _mm(dba, p["w_ba"], "nt", F32, "mm_doatt")
    gw["w_ba"] = _mm(o_att, dba, "tn", BF, "mm_dwba")
    do_hg = _mm(dbb, p["w_bb"], "nt", F32, "mm_dohg")
    gw["w_bb"] = _mm(o_hg, dbb, "tn", BF, "mm_dwbb")
    ready("mix", gw)
    dproj = _attn_bwd(proj, o_att, lse, do_att, dproj)
    dproj, gs["hg_lb"], gs["hg_norm"] = _hg_bwd(proj, p["hg_lb"], p["hg_norm"], states, do_hg, dproj)
    for q in range(IN_PARTS):
        gw[f"w_in_q{q}"] = _mm(dproj, h, "tn", BF, f"mm_dwin_q{q}", b_part=(q, IN_PARTS))
        ready(f"in_q{q}", gw)
        if q == 0:
            ready("mixers_done", gw)
    dh_a = _mm(dproj, p["w_in_a"], "nn", BF, "mm_dh_a")
    ready("dh_half", gw)
    dh = (dh_a, _mm(dproj, p["w_in_b"], "nn", BF, "mm_dh_b"))
    grad_x, _, gs["ln_mix"] = _rms_bwd(x, p["ln_mix"], dh, dx1, "rms_mix_bwd")
    return loss, grad_x, gs


_MATS = (
    ("w_in_a", "w_in", IN_WIDTH, D_MODEL // 2, "row", IN_WIDTH // N_DEV, True),
    ("w_in_b", "w_in", IN_WIDTH, D_MODEL // 2, "row", IN_WIDTH // N_DEV, True),
    ("w_ba", "w_branch_a", ATT_OUT, D_MODEL, "col", D_MODEL // N_DEV, False),
    ("w_bb", "w_branch_b", HG_WIDTH, D_MODEL, "col", D_MODEL // N_DEV, False),
    ("w_out", "w_out", D_MODEL, D_MODEL, "row", D_MODEL // N_DEV, False),
    ("wq", "wq_cross", D_MODEL, CROSS_WIDTH, "row", D_MODEL // N_DEV, False),
    ("wkv", "wkv_cross", D_MODEL, 2 * CROSS_WIDTH, "row", D_MODEL // N_DEV, False),
    ("wo", "wo_cross", CROSS_WIDTH, D_MODEL, "col", D_MODEL // N_DEV, False),
    ("w1", "w1", D_FF, D_MODEL, "row", D_FF // N_DEV, True),
    ("w3", "w3", D_FF, D_MODEL, "row", D_FF // N_DEV, True),
    ("w2", "w2", D_FF, D_MODEL, "row", D_FF // N_DEV, False),
    ("w_in_q0", "w_in", IN_WIDTH, D_MODEL // 2, "row", IN_WIDTH // N_DEV, True),
    ("w_in_q1", "w_in", IN_WIDTH, D_MODEL // 2, "row", IN_WIDTH // N_DEV, True),
)
IN_PARTS = 2
_INDEX = {m[0]: i for i, m in enumerate(_MATS)}


def _unit_shape(i):
    _, _, K, N, kind, sh, _ = _MATS[i]
    return (sh, N) if kind == "row" else (K, sh)


def _win(ref, i, chip, par, half=None):
    _, _, K, _, kind, sh, _ = _MATS[i]
    if kind == "row":
        rows, start = (sh, 0) if half is None else (sh // 2, half * (sh // 2))
        return ref.at[pl.ds(pl.multiple_of((2 * chip + par) * sh + start, 16), rows), :]
    rows, start = (K, 0) if half is None else (K // 2, half * (K // 2))
    return ref.at[pl.ds(start, rows), pl.ds(pl.multiple_of((2 * chip + par) * sh, LANE), sh)]


def _my_place():
    x, y, c = lax.axis_index("x"), lax.axis_index("y"), lax.axis_index("c")
    chips = [(1 - x, y), (x, 1 - y), (1 - x, 1 - y)]
    return x, y, c, chips


def _cast_unit(w, name, half=None):
    K, N = w.shape
    if half is not None:
        N = N // 2
    tr = _pick(K, (256, 400, 352))

    def body(w_ref, o_ref):
        o_ref[...] = w_ref[...].astype(BF)

    src = pl.BlockSpec((tr, N), lambda i: (i, half or 0))
    dst = pl.BlockSpec((tr, N), lambda i: (i, 0))
    return _pcall(body, grid=(K // tr,), in_specs=[src], out_specs=dst, out_shape=SDS((K, N), BF), name=name)(w)


def _handshake(peers):
    barrier = pltpu.get_barrier_semaphore()
    for peer in peers:
        pl.semaphore_signal(barrier, inc=1, device_id=peer, device_id_type=MESH)
    pl.semaphore_wait(barrier, len(peers))


def _sequencer(body, out_type, sems, cid, name):
    return pl.kernel(body, out_type=out_type, mesh=plsc.ScalarSubcoreMesh(axis_name="seq", num_cores=1),
                     scratch_types=sems, compiler_params=pltpu.CompilerParams(collective_id=cid), name=name)


def _all_gather(units, idx, cid, name):
    n = len(idx)

    def body(*refs):
        us, fulls = refs[:n], refs[n:2 * n]
        ssem, rsem, lsem = refs[2 * n:]
        x, y, c, _ = _my_place()
        xn, yn, dg = (1 - x, y), (x, 1 - y), (1 - x, 1 - y)
        me, sib = (x, y, c), (x, y, 1 - c)
        _handshake([sib, (*xn, c), (*yn, c)])

        def win(i, chip_xy, par, half=None):
            return _win(fulls[i], idx[i], 2 * chip_xy[0] + chip_xy[1], par, half)

        def rcopy(i, k, src, dst, to):
            return pltpu.make_async_remote_copy(src_ref=src, dst_ref=dst, send_sem=ssem.at[i, k], recv_sem=rsem.at[i, k],
                                                device_id=to, device_id_type=MESH)

        def landed(i, k, blk):
            rcopy(i, k, blk, blk, me).wait_recv()

        sends, locs = [], []

        def send(i, k, blk, to, src=None):
            cp = rcopy(i, k, blk if src is None else src, blk, to)
            cp.start()
            sends.append(cp)

        for i in range(n):
            mine = win(i, (x, y), c)
            loc = pltpu.make_async_copy(us[i], mine, lsem.at[i])
            loc.start()
            locs.append(loc)
            send(i, 0, mine, sib, src=us[i])
            send(i, 1, mine, (*xn, c), src=us[i])
            send(i, 2, mine, (*yn, c), src=us[i])
        for i in range(n):
            landed(i, 1, win(i, xn, c))
            send(i, 3, win(i, xn, c, 0), (*yn, c))
            send(i, 5, win(i, xn, c), sib)
            landed(i, 2, win(i, yn, c))
            send(i, 4, win(i, yn, c, 1), (*xn, c))
            send(i, 6, win(i, yn, c), sib)
        for i in range(n):
            landed(i, 3, win(i, dg, c, 0))
            landed(i, 4, win(i, dg, c, 1))
            send(i, 7, win(i, dg, c), sib)
        for i in range(n):
            landed(i, 0, win(i, (x, y), 1 - c))
            landed(i, 5, win(i, xn, 1 - c))
            landed(i, 6, win(i, yn, 1 - c))
            landed(i, 7, win(i, dg, 1 - c))
        for cp in sends:
            cp.wait_send()
        for loc in locs:
            loc.wait()

    out_type = [SDS((_MATS[i][2], _MATS[i][3]), BF) for i in idx]
    sems = [pltpu.SemaphoreType.DMA((n, 8)), pltpu.SemaphoreType.DMA((n, 8)), pltpu.SemaphoreType.DMA((n,))]
    return _sequencer(body, out_type, sems, cid, name)(*units)


def _rs_sibling(grads, idx, cid, name):
    n = len(idx)

    def body(*refs):
        gs, bufs = refs[:n], refs[n:2 * n]
        ssem, rsem = refs[2 * n:]
        x, y, c, _ = _my_place()
        _handshake([(x, y, 1 - c)])
        cps = []
        for i in range(n):
            for k in range(4):
                cp = pltpu.make_async_remote_copy(
                    src_ref=_win(gs[i], idx[i], k, 1 - c), dst_ref=bufs[i].at[k], send_sem=ssem.at[i, k], recv_sem=rsem.at[i, k],
                    device_id=(x, y, 1 - c), device_id_type=MESH)
                cp.start()
                cps.append(cp)
        for cp in cps:
            cp.wait()

    out_type = [SDS((4, *_unit_shape(i)), BF) for i in idx]
    sems = [pltpu.SemaphoreType.DMA((n, 4)), pltpu.SemaphoreType.DMA((n, 4))]
    return _sequencer(body, out_type, sems, cid, name)(*grads)


def _sibling_sum_tc(grad, i, cid, name):
    unit = _unit_shape(i)

    def body(g_ref, t_ref, theirs, mine, summed, ssem, rsem, lsem, osem):
        x, y, c, _ = _my_place()
        _handshake([(x, y, 1 - c)])
        sends, loads = [], []
        for k in range(4):
            sends.append(pltpu.make_async_remote_copy(
                src_ref=_win(g_ref, i, k, 1 - c), dst_ref=theirs.at[k], send_sem=ssem.at[k], recv_sem=rsem.at[k],
                device_id=(x, y, 1 - c), device_id_type=MESH))
            loads.append(pltpu.make_async_copy(_win(g_ref, i, k, c), mine.at[k], lsem.at[k]))
            sends[-1].start()
            loads[-1].start()
        stores = []
        for k in range(4):
            loads[k].wait()
            sends[k].wait_recv()
            summed[k] = (mine[k].astype(F32) + theirs[k].astype(F32)).astype(BF)
            stores.append(pltpu.make_async_copy(summed.at[k], t_ref.at[k], osem.at[k]))
            stores[-1].start()
        for k in range(4):
            sends[k].wait_send()
            stores[k].wait()

    block = pltpu.VMEM((4, *unit), BF)
    return _pcall(body, in_specs=[_ANY], out_specs=_ANY, out_shape=SDS((4, *unit), BF),
                  scratch_shapes=[block, block, block] + [pltpu.SemaphoreType.DMA((4,))] * 4,
                  compiler_params=_cp(VMEM_BIG, collective_id=cid), name=name)(grad)


def _chip_sum(g, sib, i, place, name):
    _, _, K, N, kind, sh, _ = _MATS[i]

    def body(pref, g_ref, s_ref, o_ref):
        o_ref[...] = (g_ref[...].astype(F32) + s_ref[...].astype(F32)).astype(BF)

    if kind == "row":
        tr = _pick(sh, (800, 352, 256))
        per = sh // tr
        grid = (4, per)
        g_spec = pl.BlockSpec((tr, N), lambda k, r, pref: ((2 * k + pref[0]) * per + r, 0))
        u_spec = pl.BlockSpec((None, tr, N), lambda k, r, pref: (k, r, 0))
    else:
        grid = (4,)
        g_spec = pl.BlockSpec((K, sh), lambda k, pref: (0, 2 * k + pref[0]))
        u_spec = pl.BlockSpec((None, K, sh), lambda k, pref: (k, 0, 0))
    return _pcall(body, grid=grid, in_specs=[g_spec, u_spec], out_specs=u_spec, out_shape=SDS((4, *_unit_shape(i)), BF),
                  prefetch=True, name=name)(place, g, sib)


def _rs_chips(parts, idx, cid, name):
    n = len(idx)

    def body(*refs):
        ts, bufs, relays = refs[:n], refs[n:2 * n], refs[2 * n:3 * n]
        ssem, rsem = refs[3 * n:]
        x, y, c, _ = _my_place()
        xn, yn, dg = (1 - x, y), (x, 1 - y), (1 - x, 1 - y)
        to_x, to_y, me = (*xn, c), (*yn, c), (x, y, c)
        _handshake([to_x, to_y])

        def rcopy(i, k, src, dst, to):
            return pltpu.make_async_remote_copy(src_ref=src, dst_ref=dst, send_sem=ssem.at[i, k], recv_sem=rsem.at[i, k],
                                                device_id=to, device_id_type=MESH)

        def halves(i):
            rows = _unit_shape(idx[i])[0] // 2
            return [pl.ds(0, rows), pl.ds(rows, rows)]

        sends = []

        def send(cp):
            cp.start()
            sends.append(cp)

        for i in range(n):
            h0, h1 = halves(i)
            diag = ts[i].at[2 * dg[0] + dg[1]]
            send(rcopy(i, 0, ts[i].at[2 * xn[0] + xn[1]], bufs[i].at[0], to_x))
            send(rcopy(i, 1, ts[i].at[2 * yn[0] + yn[1]], bufs[i].at[1], to_y))
            send(rcopy(i, 2, diag.at[h0], relays[i].at[0], to_x))
            send(rcopy(i, 3, diag.at[h1], relays[i].at[1], to_y))
        for i in range(n):
            h0, h1 = halves(i)
            rcopy(i, 2, relays[i].at[0], relays[i].at[0], me).wait_recv()
            send(rcopy(i, 4, relays[i].at[0], bufs[i].at[2, h0], to_y))
            rcopy(i, 3, relays[i].at[1], relays[i].at[1], me).wait_recv()
            send(rcopy(i, 5, relays[i].at[1], bufs[i].at[2, h1], to_x))
        for i in range(n):
            h0, h1 = halves(i)
            rcopy(i, 0, bufs[i].at[0], bufs[i].at[0], me).wait_recv()
            rcopy(i, 1, bufs[i].at[1], bufs[i].at[1], me).wait_recv()
            rcopy(i, 4, bufs[i].at[2, h0], bufs[i].at[2, h0], me).wait_recv()
            rcopy(i, 5, bufs[i].at[2, h1], bufs[i].at[2, h1], me).wait_recv()
        for cp in sends:
            cp.wait_send()

    def relay_shape(i):
        rows, cols = _unit_shape(i)
        return (2, rows // 2, cols)

    out_type = [SDS((3, *_unit_shape(i)), BF) for i in idx] + [SDS(relay_shape(i), BF) for i in idx]
    sems = [pltpu.SemaphoreType.DMA((n, 6)), pltpu.SemaphoreType.DMA((n, 6))]
    return _sequencer(body, out_type, sems, cid, name)(*parts)[:n]


def _adamw(w, g, m, v):
    m = ADAM_B1 * m + (1.0 - ADAM_B1) * g
    v = ADAM_B2 * v + (1.0 - ADAM_B2) * jnp.square(g)
    m_hat = m / (1.0 - ADAM_B1 ** ADAM_STEP)
    v_hat = v / (1.0 - ADAM_B2 ** ADAM_STEP)
    delta = -ADAM_LR * (m_hat / (jnp.sqrt(v_hat) + ADAM_EPS) + ADAM_WD * w)
    return delta, m, v


def _adam_mat(parts, recvs, w, m, v, place, name):
    R, C = w.shape
    n = len(parts)
    tr = _pick(R, (256, 160, 176))

    def body(pref, *refs):
        t_refs, r_refs = refs[:n], refs[n:2 * n]
        w_ref, m_ref, v_ref, g_out, d_out, m_out, v_out = refs[2 * n:]
        cols = [t[...].astype(F32) + r[0].astype(F32) + r[1].astype(F32) + r[2].astype(F32) for t, r in zip(t_refs, r_refs)]
        g = jnp.concatenate(cols, axis=1) if n > 1 else cols[0]
        d, mn, vn = _adamw(w_ref[...], g, m_ref[...], v_ref[...])
        g_out[...] = g
        d_out[...] = d
        m_out[...] = mn
        v_out[...] = vn

    t_spec = pl.BlockSpec((None, tr, C // n), lambda i, pref: (pref[1], i, 0))
    r_spec = pl.BlockSpec((3, tr, C // n), lambda i, pref: (0, i, 0))
    s_spec = pl.BlockSpec((tr, C), lambda i, pref: (i, 0))
    return _pcall(body, grid=(R // tr,), in_specs=[t_spec] * n + [r_spec] * n + [s_spec] * 3, out_specs=[s_spec] * 4,
                  out_shape=[SDS((R, C), F32)] * 4, prefetch=True, name=name)(place, *parts, *recvs, w, m, v)


_SMALL = (("ln_mix", "ln_mix_w", D_MODEL), ("ln_cross", "ln_cross_w", D_MODEL), ("ln_mem", "ln_mem_w", D_MODEL),
          ("ln_ffn", "ln_ffn_w", D_MODEL), ("ln_final", "ln_final_w", D_MODEL), ("hg_lb", "hg_lower_bounds", 2 * HG_WIDTH),
          ("hg_norm", "hg_norm_w", HEAD_DIM))
_PACK = sum(n for _, _, n in _SMALL) + LANE


def _small_sync(grow, wrow, mrow, vrow):
    def body(g_ref, w_ref, m_ref, v_ref, o_ref, gath, ssem, rsem):
        x, y, c, _ = _my_place()
        me = 4 * x + 2 * y + c
        gath[pl.ds(me, 1), :] = g_ref[...]
        cps = []
        for dlt in range(1, N_DEV):
            peer = (me + dlt) % N_DEV
            cp = pltpu.make_async_remote_copy(
                src_ref=g_ref, dst_ref=gath.at[pl.ds(me, 1), :], send_sem=ssem.at[peer], recv_sem=rsem.at[me],
                device_id=(peer // 4, (peer // 2) % 2, peer % 2), device_id_type=MESH)
            cp.start()
            cps.append(cp)
        for dlt in range(1, N_DEV):
            src = (me + dlt) % N_DEV
            pltpu.make_async_remote_copy(
                src_ref=g_ref, dst_ref=gath.at[pl.ds(src, 1), :], send_sem=ssem.at[src], recv_sem=rsem.at[src],
                device_id=(x, y, c), device_id_type=MESH).wait_recv()
        for cp in cps:
            cp.wait_send()
        g = gath[0:1, :]
        for j in range(1, N_DEV):
            g = g + gath[j:j + 1, :]
        d, mn, vn = _adamw(w_ref[...], g, m_ref[...], v_ref[...])
        o_ref[0:1, :] = g
        o_ref[1:2, :] = d
        o_ref[2:3, :] = mn
        o_ref[3:4, :] = vn

    vm = pl.BlockSpec(memory_space=pltpu.VMEM)
    return _pcall(
        body, in_specs=[vm] * 4, out_specs=vm, out_shape=SDS((4, _PACK), F32),
        scratch_shapes=[pltpu.VMEM((N_DEV, _PACK), F32), pltpu.SemaphoreType.DMA((N_DEV,)), pltpu.SemaphoreType.DMA((N_DEV,))],
        name="small_sync")(grow, wrow, mrow, vrow)


def _pack_small(d, loss_row=None):
    parts = [d[k].reshape(1, n).astype(F32) for k, _, n in _SMALL]
    parts.append(loss_row if loss_row is not None else jnp.ones((1, LANE), F32))
    return jnp.concatenate(parts, axis=1)


def kernel(x, mem, ln_mix_w, w_in, hg_norm_w, hg_lower_bounds, w_branch_a, w_branch_b, w_out, ln_cross_w, ln_mem_w, wq_cross, wkv_cross, wo_cross, ln_ffn_w, w1, w3, w2, ln_final_w, loss_target, m_ln_mix_w, m_w_in, m_hg_norm_w, m_hg_lower_bounds, m_w_branch_a, m_w_branch_b, m_w_out, m_ln_cross_w, m_ln_mem_w, m_wq_cross, m_wkv_cross, m_wo_cross, m_ln_ffn_w, m_w1, m_w3, m_w2, m_ln_final_w, v_ln_mix_w, v_w_in, v_hg_norm_w, v_hg_lower_bounds, v_w_branch_a, v_w_branch_b, v_w_out, v_ln_cross_w, v_ln_mem_w, v_wq_cross, v_wkv_cross, v_wo_cross, v_ln_ffn_w, v_w1, v_w3, v_w2, v_ln_final_w):
    given = dict(locals())
    place = jnp.stack([lax.axis_index("c"), 2 * lax.axis_index("x") + lax.axis_index("y")]).astype(jnp.int32)

    _ORDER[0] = None
    index = _INDEX

    def local_view(a, i):
        return a[0].T if _MATS[i][6] else a[0]

    shard = {key: local_view(given[ref], i) for i, (key, ref, *_) in enumerate(_MATS)}

    p = {}
    col_half = {"w_in_a": 0, "w_in_b": 1}
    waves = (("in_a", ["w_in_a"], 1), ("in_b", ["w_in_b"], 12), ("mid", ["w_ba", "w_bb", "w_out", "wq", "wkv", "wo"], 2),
             ("w1", ["w1"], 11), ("w3", ["w3"], 15), ("w2", ["w2"], 16))
    for wave, keys, cid in waves:
        units = [_cast_unit(shard[k], "cast_" + k, col_half.get(k)) for k in keys]
        p.update(zip(keys, _all_gather(units, [index[k] for k in keys], cid, "all_gather_" + wave)))
    small_w = {"ln_mix": ln_mix_w, "ln_cross": ln_cross_w, "ln_mem": ln_mem_w, "ln_ffn": ln_ffn_w,
               "ln_final": ln_final_w.reshape(1, D_MODEL), "hg_lb": hg_lower_bounds, "hg_norm": hg_norm_w}
    p.update(small_w)

    groups = {"ffn": ["w2", "w1", "w3"], "cross": ["wo", "wq", "wkv"], "mix": ["w_out", "w_ba", "w_bb"]}
    cids = {"ffn": (3, 4), "cross": (5, 6), "mix": (7, 8)}
    for q in range(IN_PARTS):
        groups[f"in_q{q}"] = [f"w_in_q{q}"]
        cids[f"in_q{q}"] = ((9, 10), (13, 14), (17, 18), (19, 20))[q]
    sib, parts, recv, out = {}, {}, {}, {}

    grads = {}

    def chip_stage(name):
        keys = groups[name]
        parts[name] = [_chip_sum(grads[k], s, index[k], place, "chip_sum_" + k) for k, s in zip(keys, sib[name])]
        recv[name] = list(_rs_chips(parts[name], [index[k] for k in keys], cids[name][1], "rs_chips_" + name))

    def adam(ref, i, part_list, recv_list):
        res = _adam_mat(part_list, recv_list, shard[_MATS[i][0]], local_view(given["m_" + ref], i), local_view(given["v_" + ref], i),
                        place, "adam_" + ref)
        out[ref] = [(a.T if _MATS[i][6] else a)[None] for a in res]

    def adam_stage(name, only=None, skip=None):
        for n, (k, t, r) in enumerate(zip(groups[name], parts[name], recv[name])):
            if (only is None or n == only) and n != skip:
                adam(_MATS[index[k]][1], index[k], [t], [r])

    def ready(name, gw):
        if name == "mixers_done":
            adam_stage("ffn", only=0)
            chip_stage("mix")
            adam_stage("ffn", skip=0)
            adam_stage("cross")
            return
        if name == "dh_half":
            adam_stage("mix")
            return
        keys = groups[name]
        grads.update({k: gw[k] for k in keys})
        if name.startswith("in_q"):
            parts[name] = [_sibling_sum_tc(gw[k], index[k], cids[name][0], "sibling_sum_" + k) for k in keys]
            recv[name] = list(_rs_chips(parts[name], [index[k] for k in keys], cids[name][1], "rs_chips_" + name))
            if name == "in_q1":
                _ALSO_AFTER.extend(recv["in_q0"])
            return
        sib[name] = _rs_sibling([gw[k] for k in keys], [index[k] for k in keys], cids[name][0], "rs_sibling_" + name)
        if name == "cross":
            chip_stage("ffn")
        if name == "mix":
            chip_stage("cross")

    loss_row, grad_x, gs = _local_step(x[0], mem[0], loss_target[0], p, ready)

    sm = {k: given["m_" + ref] for k, ref, _ in _SMALL}
    sv = {k: given["v_" + ref] for k, ref, _ in _SMALL}
    synced = _small_sync(_pack_small(gs, loss_row), _pack_small(small_w), _pack_small(sm), _pack_small(sv))
    in_groups = [f"in_q{q}" for q in range(IN_PARTS)]
    adam("w_in", index["w_in_a"], sum((parts[g] for g in in_groups), []), sum((recv[g] for g in in_groups), []))
    off = 0
    for k, ref, n in _SMALL:
        shape = given[ref].shape
        out[ref] = [synced[r, off:off + n].reshape(shape) for r in range(4)]
        off += n
    loss = synced[0, off]

    order = ["ln_mix_w", "w_in", "hg_norm_w", "hg_lower_bounds", "w_branch_a", "w_branch_b", "w_out", "ln_cross_w", "ln_mem_w",
             "wq_cross", "wkv_cross", "wo_cross", "ln_ffn_w", "w1", "w3", "w2", "ln_final_w"]
    return (loss, grad_x[None], *[out[n][0] for n in order], *[out[n][1] for n in order],
            *[out[n][2] for n in order], *[out[n][3] for n in order])
```

```python
import jax
import jax.numpy as jnp
from jax import lax
from jax.experimental import pallas as pl
from jax.experimental.pallas import tpu as pltpu
from jax.experimental.pallas import tpu_sc as plsc

F32 = jnp.float32
BF = jnp.bfloat16
SDS = jax.ShapeDtypeStruct

D_MODEL = 2048
SEQ = 2048
HEAD_DIM = 128
MEM_LEN = 256
ATT_GROUPS = ((128, 1), (512, 4), (2048, 16))
ATT_HEADS = 4
ATT_WIDTH = 1536
ATT_OUT = 512
HG_HEADS = 8
HG_WIDTH = 1024
HG_CHUNK = 64
IN_WIDTH = 12800
CROSS_HEADS = 4
CROSS_WIDTH = 512
D_FF = 5632
RMS_EPS = 1e-6
ADAM_LR = 0.001
ADAM_B1 = 0.9
ADAM_B2 = 0.999
ADAM_EPS = 1e-08
ADAM_WD = 0.01
ADAM_STEP = 10
N_DEV = 8

LANE = 128
QA0, KA0, VA0 = 0, 12, 24
QH0, FH0, IH0, GH0 = 36, 44, 52, 60
GA0, GB0 = 68, 84

VMEM_BIG = 56 * 1024 * 1024
MESH = pl.DeviceIdType.MESH


def _cp(vmem=None, **kw):
    if vmem is not None:
        kw["vmem_limit_bytes"] = vmem
    return pltpu.CompilerParams(**kw)


def _pick(n, cands):
    for c in cands:
        if n % c == 0:
            return c
    raise ValueError(f"no tile for {n}")


_ANY = pl.BlockSpec(memory_space=pl.ANY)
_ORDER = [None]
_ALSO_AFTER = []


def _pcall(body, *, name, in_specs, out_specs, out_shape, grid=(), scratch_shapes=(), compiler_params=None, prefetch=False,
           aliases=None):
    def run(*args):
        dep = _ORDER[0]
        specs, operands, kernel_fn = list(in_specs), list(args), body
        deps = [] if dep is None or any(dep is a for a in operands) else [dep]
        deps += _ALSO_AFTER
        del _ALSO_AFTER[:]
        if deps:
            at, nd = len(operands), len(deps)
            specs += [_ANY] * nd
            operands += deps

            def kernel_fn(*refs):
                return body(*refs[:at], *refs[at + nd:])
        if prefetch:
            call = pl.pallas_call(
                kernel_fn, out_shape=out_shape, name=name, compiler_params=compiler_params,
                grid_spec=pltpu.PrefetchScalarGridSpec(num_scalar_prefetch=1, grid=grid, in_specs=specs, out_specs=out_specs,
                                                       scratch_shapes=scratch_shapes))
        else:
            call = pl.pallas_call(kernel_fn, grid=grid, in_specs=specs, out_specs=out_specs, out_shape=out_shape,
                                  scratch_shapes=scratch_shapes, compiler_params=compiler_params, name=name,
                                  input_output_aliases=aliases or {})
        out = call(*operands)
        _ORDER[0] = out[0] if isinstance(out, (list, tuple)) else out
        return out
    return run


def _mm(a, b, form, out_dtype, name, res=None, tiles=None, a_part=(0, 1), b_part=(0, 1)):
    a_shape = (a.shape[0], a.shape[1] // a_part[1])
    b_shape = (b.shape[0], b.shape[1] // b_part[1])
    if form == "nn":
        (M, K), (K2, N) = a_shape, b_shape
    elif form == "nt":
        (M, K), (N, K2) = a_shape, b_shape
    else:
        (K, M), (K2, N) = a_shape, b_shape
    assert K == K2, (a.shape, b.shape, form)
    if tiles is None:
        tm = _pick(M, (1024, 1280, 1408, 512, 256))
        tn = _pick(N, (1280, 1024, 1408, 768, 512, 256))
        tk = K if K <= 2048 else _pick(K, (D_FF, 2560))
        if tk == D_FF:
            tm = _pick(M, (512, 256))
    else:
        tm, tn, tk = tiles
    nk = K // tk
    ah, bh = a_part[0], b_part[0]
    if form == "nn":
        a_spec = pl.BlockSpec((tm, tk), lambda i, j, k: (i, k + ah * nk))
        b_spec = pl.BlockSpec((tk, tn), lambda i, j, k: (k, j + bh * (N // tn)))
        dims = ((1,), (0,))
    elif form == "nt":
        a_spec = pl.BlockSpec((tm, tk), lambda i, j, k: (i, k + ah * nk))
        b_spec = pl.BlockSpec((tn, tk), lambda i, j, k: (j, k + bh * nk))
        dims = ((1,), (1,))
    else:
        a_spec = pl.BlockSpec((tk, tm), lambda i, j, k: (k, i + ah * (M // tm)))
        b_spec = pl.BlockSpec((tk, tn), lambda i, j, k: (k, j + bh * (N // tn)))
        dims = ((0,), (0,))
    o_spec = pl.BlockSpec((tm, tn), lambda i, j, k: (i, j))
    in_specs = [a_spec, b_spec]
    args = [a, b]
    if res is not None:
        in_specs.append(o_spec)
        args.append(res)

    def body(*refs):
        a_ref, b_ref = refs[0], refs[1]
        r_ref = refs[2] if res is not None else None
        o_ref = refs[3] if res is not None else refs[2]
        acc = refs[-1] if nk > 1 else None
        k = pl.program_id(2)
        d = lax.dot_general(a_ref[...].astype(BF), b_ref[...].astype(BF), (dims, ((), ())), preferred_element_type=F32)

        def finish(r):
            if res is not None:
                r = r + r_ref[...].astype(F32)
            o_ref[...] = r.astype(o_ref.dtype)

        if nk == 1:
            finish(d)
        else:
            @pl.when(k == 0)
            def _():
                acc[...] = d

            @pl.when((k > 0) & (k < nk - 1))
            def _():
                acc[...] += d

            @pl.when(k == nk - 1)
            def _():
                finish(acc[...] + d)

    return _pcall(
        body, grid=(M // tm, N // tn, nk), in_specs=in_specs, out_specs=o_spec,
        out_shape=SDS((M, N), out_dtype), scratch_shapes=[pltpu.VMEM((tm, tn), F32)] if nk > 1 else [],
        compiler_params=_cp(VMEM_BIG, dimension_semantics=("parallel", "parallel", "arbitrary")),
        name=name)(*args)


_MXU_COLS = 256


def _mm_fused(a, b, form, name, extras, out_dtypes, epi, tn=None, scatter=None):
    M, K = a.shape
    N = b.shape[1] if form == "nn" else b.shape[0]
    assert K <= 2048 and K == (b.shape[0] if form == "nn" else b.shape[1])
    tm = _pick(M, (1024, 512, 256))
    tn = tn or _pick(N, (1280, 1024, 1408, 512, 256))
    ne, no = len(extras), len(out_dtypes)
    bases = scatter[1] if scatter else []
    dims = ((1,), (0,)) if form == "nn" else ((1,), (1,))

    def body(*refs):
        a_ref, b_ref, ex, outs = refs[0], refs[1], refs[2:2 + ne], refs[2 + ne:2 + ne + no]
        av = a_ref[...].astype(BF)
        if scatter:
            wide_ref, stage, sem = refs[2 + ne + no:]
        for c0 in range(0, tn, _MXU_COLS):
            cols = slice(c0, min(c0 + _MXU_COLS, tn))
            bc = b_ref[:, cols] if form == "nn" else b_ref[cols, :]
            d = lax.dot_general(av, bc.astype(BF), (dims, ((), ())), preferred_element_type=F32)
            res = epi(d, *[e[:, cols] for e in ex])
            for o_ref, r in zip(outs, res[:no]):
                o_ref[:, cols] = r.astype(o_ref.dtype)
            for s, r in enumerate(res[no:]):
                stage[s, :, cols] = r.astype(BF)
        if scatter:
            rows = pl.ds(pl.multiple_of(pl.program_id(0) * tm, tm), tm)
            cps = [pltpu.make_async_copy(stage.at[s], wide_ref.at[rows, pl.ds(pl.multiple_of(base + pl.program_id(1) * tn, LANE), tn)],
                                         sem.at[s]) for s, base in enumerate(bases)]
            for cp in cps:
                cp.start()
            for cp in cps:
                cp.wait()

    a_spec = pl.BlockSpec((tm, K), lambda i, j: (i, 0))
    b_spec = pl.BlockSpec((K, tn), lambda i, j: (0, j)) if form == "nn" else pl.BlockSpec((tn, K), lambda i, j: (j, 0))
    o_spec = pl.BlockSpec((tm, tn), lambda i, j: (i, j))
    ex_specs, ex_args = [], []
    for e in extras:
        arr, off = e if isinstance(e, tuple) else (e, 0)
        ex_specs.append(pl.BlockSpec((tm, tn), lambda i, j, off=off: (i, off + j)))
        ex_args.append(arr)
    out_specs, out_shape, scratch = [o_spec] * no, [SDS((M, N), dt) for dt in out_dtypes], []
    if scatter:
        out_specs, out_shape = out_specs + [_ANY], out_shape + [scatter[0]]
        scratch = [pltpu.VMEM((len(bases), tm, tn), BF), pltpu.SemaphoreType.DMA((len(bases),))]
    return _pcall(
        body, grid=(M // tm, N // tn), in_specs=[a_spec, b_spec] + ex_specs, out_specs=out_specs, out_shape=out_shape,
        scratch_shapes=scratch, compiler_params=_cp(VMEM_BIG, dimension_semantics=("parallel", "parallel")),
        name=name)(a, b, *ex_args)


def _rms(x, w):
    return x * lax.rsqrt(jnp.mean(x * x, axis=-1, keepdims=True) + RMS_EPS) * w


def _rms_fwd(x, w, name, split=None):
    R, D = x.shape
    tr = 256
    widths = [D] if split is None else [D, split, D - split]

    def body(x_ref, w_ref, o_ref, *part_refs):
        y = _rms(x_ref[...], w_ref[...]).astype(o_ref.dtype)
        o_ref[...] = y
        if split is not None:
            part_refs[0][...] = y[:, :split]
            part_refs[1][...] = y[:, split:]

    outs = _pcall(
        body, grid=(R // tr,),
        in_specs=[pl.BlockSpec((tr, D), lambda i: (i, 0)), pl.BlockSpec((1, D), lambda i: (0, 0))],
        out_specs=[pl.BlockSpec((tr, n), lambda i: (i, 0)) for n in widths], out_shape=[SDS((R, n), BF) for n in widths],
        name=name)(x, w)
    return outs[0] if split is None else outs


def _rms_bwd(x, w, dh, dres, name):
    R, D = x.shape
    tr = 256
    want_dx = dres is not None
    dhs = dh if isinstance(dh, (tuple, list)) else (dh,)
    nd = len(dhs)

    def body(*refs):
        x_ref, w_ref, dh_refs = refs[0], refs[1], refs[2:2 + nd]
        if want_dx:
            dr_ref, dx_ref, dxb_ref, dw_ref = refs[2 + nd:]
        else:
            dw_ref = refs[2 + nd]

        @pl.when(pl.program_id(0) == 0)
        def _():
            dw_ref[...] = jnp.zeros_like(dw_ref)

        dhv = jnp.concatenate([r[...].astype(F32) for r in dh_refs], axis=1) if nd > 1 else dh_refs[0][...].astype(F32)
        _, vjp = jax.vjp(_rms, x_ref[...], w_ref[...])
        dx, dw = vjp(dhv)
        dw_ref[...] += dw
        if want_dx:
            dx = dx + dr_ref[...]
            dx_ref[...] = dx
            dxb_ref[...] = dx.astype(BF)

    row = pl.BlockSpec((tr, D), lambda i: (i, 0))
    parts = [pl.BlockSpec((tr, d.shape[1]), lambda i: (i, 0)) for d in dhs]
    vec = pl.BlockSpec((1, D), lambda i: (0, 0))
    if want_dx:
        return _pcall(body, grid=(R // tr,), in_specs=[row, vec] + parts + [row], out_specs=[row, row, vec],
                      out_shape=[SDS((R, D), F32), SDS((R, D), BF), SDS((1, D), F32)], name=name)(x, w, *dhs, dres)
    return _pcall(body, grid=(R // tr,), in_specs=[row, vec] + parts, out_specs=vec,
                  out_shape=SDS((1, D), F32), name=name)(x, w, *dhs)


def _final_loss(x3, w, tgt):
    R, D = x3.shape
    tr = 256

    def loss_fn(xv, wv, tv):
        err = _rms(xv, wv) - tv
        return 0.5 * jnp.sum(jnp.mean(err * err, axis=-1))

    def body(x_ref, w_ref, t_ref, loss_ref, dx_ref, dxb_ref, dw_ref):
        @pl.when(pl.program_id(0) == 0)
        def _():
            dw_ref[...] = jnp.zeros_like(dw_ref)
            loss_ref[...] = jnp.zeros_like(loss_ref)

        tv = t_ref[...]
        val, vjp = jax.vjp(lambda a, b: loss_fn(a, b, tv), x_ref[...], w_ref[...])
        dx, dw = vjp(jnp.ones((), F32))
        dx_ref[...] = dx
        dxb_ref[...] = dx.astype(BF)
        dw_ref[...] += dw
        loss_ref[...] += jnp.full(loss_ref.shape, val, F32)

    row = pl.BlockSpec((tr, D), lambda i: (i, 0))
    vec = pl.BlockSpec((1, D), lambda i: (0, 0))
    one = pl.BlockSpec((1, LANE), lambda i: (0, 0))
    return _pcall(body, grid=(R // tr,), in_specs=[row, vec, row], out_specs=[one, row, row, vec],
                          out_shape=[SDS((1, LANE), F32), SDS((R, D), F32), SDS((R, D), BF), SDS((1, D), F32)],
                          name="final_loss")(x3, w, tgt)


NEG = -1e30


def _att_blocks(S, d):
    L = S // d
    nb = L // LANE
    return nb, (2 if nb > 1 else 1)


def _att_rows(S, d, idx):
    nb, nkb = _att_blocks(S, d)
    r = idx // nb
    n = idx % nb
    kb = jnp.maximum(n - 1, 0)
    if d == 1:
        qrows = pl.ds(pl.multiple_of(n * LANE, LANE), LANE)
        krows = pl.ds(pl.multiple_of(kb * LANE, LANE), LANE * nkb)
    else:
        qrows = pl.ds(r + n * (LANE * d), LANE, stride=d)
        krows = pl.ds(r + kb * (LANE * d), LANE * nkb, stride=d)
    qpos = n * LANE + lax.broadcasted_iota(jnp.int32, (LANE, LANE * nkb), 0)
    kpos = kb * LANE + lax.broadcasted_iota(jnp.int32, (LANE, LANE * nkb), 1)
    return qrows, krows, qpos - kpos


def _slab(col):
    return pl.BlockSpec((SEQ, LANE), lambda h, col=col: (0, col + h))


def _widen(slabs, wide):
    for t, ref in enumerate(slabs):
        wide[t] = ref[...].astype(F32)
    return tuple(tuple(wide.at[3 * kind + g] for g in range(3)) for kind in range(3))


def _attn_fwd(proj):
    S = proj.shape[0]
    scale = HEAD_DIM ** -0.5

    def body(q0, q1, q2, k0, k1, k2, v0, v1, v2, o_ref, lse_ref, og, lg, wide):
        qs, ks, vs = _widen((q0, q1, q2, k0, k1, k2, v0, v1, v2), wide)
        for g, (window, d) in enumerate(ATT_GROUPS):
            def blk(idx, carry, g=g, window=window, d=d):
                qrows, krows, dist = _att_rows(S, d, idx)
                q = qs[g][qrows, :].astype(BF)
                k = ks[g][krows, :].astype(BF)
                v = vs[g][krows, :].astype(BF)
                s = lax.dot_general(q, k, (((1,), (1,)), ((), ())), preferred_element_type=F32) * scale
                s = jnp.where((dist >= 0) & (dist <= window // d), s, NEG)
                m = jnp.max(s, axis=-1, keepdims=True)
                p = jnp.exp(s - m)
                l = jnp.sum(p, axis=-1, keepdims=True)
                o = jnp.dot((p / l).astype(BF), v, preferred_element_type=F32)
                og[g, qrows, :] = o
                lg[g, qrows, :] = jnp.broadcast_to(m + jnp.log(l), (LANE, LANE))
                return carry
            lax.fori_loop(0, S // LANE, blk, 0, unroll=True)

        def merge(t, carry):
            rows = pl.ds(pl.multiple_of(t * 256, 256), 256)
            l0, l1, l2 = lg[0, rows, :], lg[1, rows, :], lg[2, rows, :]
            m = jnp.maximum(jnp.maximum(l0, l1), l2)
            w0, w1, w2 = jnp.exp(l0 - m), jnp.exp(l1 - m), jnp.exp(l2 - m)
            den = w0 + w1 + w2
            o_ref[rows, :] = (w0 * og[0, rows, :] + w1 * og[1, rows, :] + w2 * og[2, rows, :]) / den
            lse_ref[rows, :] = m + jnp.log(den)
            return carry
        lax.fori_loop(0, S // 256, merge, 0)

    ins = [_slab(QA0 + 4 * g) for g in range(3)] + [_slab(KA0 + 4 * g) for g in range(3)] + [_slab(VA0 + 4 * g) for g in range(3)]
    out = pl.BlockSpec((SEQ, LANE), lambda h: (0, h))
    return _pcall(
        body, grid=(ATT_HEADS,), in_specs=ins, out_specs=[out, out],
        out_shape=[SDS((S, ATT_OUT), F32), SDS((S, ATT_OUT), F32)],
        scratch_shapes=[pltpu.VMEM((3, S, LANE), F32), pltpu.VMEM((3, S, LANE), F32), pltpu.VMEM((9, S, LANE), F32)],
        compiler_params=_cp(VMEM_BIG), name="attn_fwd")(*([proj] * 9))


def _attn_bwd(proj, o_att, lse, do_att, dproj):
    S = proj.shape[0]
    scale = HEAD_DIM ** -0.5

    def body(q0, q1, q2, k0, k1, k2, v0, v1, v2, o_ref, lse_ref, do_ref, dp_in, dp_ref, dqa, dka, dva, delta, stage, sem, wide):
        qs, ks, vs = _widen((q0, q1, q2, k0, k1, k2, v0, v1, v2), wide)
        head = pl.program_id(0)

        def slab_copies(g):
            return [pltpu.make_async_copy(stage.at[t], dp_ref.at[:, pl.ds(pl.multiple_of((base + 4 * g + head) * LANE, LANE), LANE)],
                                          sem.at[t]) for t, base in enumerate((QA0, KA0, VA0))]

        delta[...] = jnp.broadcast_to(jnp.sum(do_ref[...] * o_ref[...], axis=-1, keepdims=True), delta.shape)
        for g, (window, d) in enumerate(ATT_GROUPS):
            dka[...] = jnp.zeros_like(dka)
            dva[...] = jnp.zeros_like(dva)

            def blk(idx, carry, g=g, window=window, d=d):
                qrows, krows, dist = _att_rows(S, d, idx)
                q = qs[g][qrows, :].astype(BF)
                k = ks[g][krows, :].astype(BF)
                v = vs[g][krows, :].astype(BF)
                do = do_ref[qrows, :]
                s = lax.dot_general(q, k, (((1,), (1,)), ((), ())), preferred_element_type=F32) * scale
                p = jnp.exp(s - lse_ref[qrows, :][:, 0:1])
                p = jnp.where((dist >= 0) & (dist <= window // d), p, 0.0)
                dob = do.astype(BF)
                dva[krows, :] += lax.dot_general(p.astype(BF), dob, (((0,), (0,)), ((), ())), preferred_element_type=F32)
                dp = lax.dot_general(dob, v, (((1,), (1,)), ((), ())), preferred_element_type=F32)
                ds = (p * (dp - delta[qrows, :][:, 0:1]) * scale).astype(BF)
                dqa[qrows, :] = jnp.dot(ds, k, preferred_element_type=F32)
                dka[krows, :] += lax.dot_general(ds, q, (((0,), (0,)), ((), ())), preferred_element_type=F32)
                return carry
            lax.fori_loop(0, S // LANE, blk, 0, unroll=True)
            if g > 0:
                for cp in slab_copies(g - 1):
                    cp.wait()
            stage[0] = dqa[...].astype(BF)
            stage[1] = dka[...].astype(BF)
            stage[2] = dva[...].astype(BF)
            for cp in slab_copies(g):
                cp.start()
        for cp in slab_copies(len(ATT_GROUPS) - 1):
            cp.wait()

    cols = [QA0 + 4 * g for g in range(3)] + [KA0 + 4 * g for g in range(3)] + [VA0 + 4 * g for g in range(3)]
    ins = [_slab(c) for c in cols]
    one = pl.BlockSpec((SEQ, LANE), lambda h: (0, h))
    return _pcall(
        body, grid=(ATT_HEADS,), in_specs=ins + [one, one, one, _ANY], out_specs=_ANY,
        out_shape=SDS((S, IN_WIDTH), BF), aliases={12: 0},
        scratch_shapes=[pltpu.VMEM((S, LANE), F32)] * 4 + [pltpu.VMEM((3, S, LANE), BF), pltpu.SemaphoreType.DMA((3,)),
                                                           pltpu.VMEM((9, S, LANE), F32)],
        compiler_params=_cp(VMEM_BIG), name="attn_bwd")(*([proj] * 9), o_att, lse, do_att, dproj)


def _bdot(a, b, dims):
    return lax.dot_general(a.astype(BF), b.astype(BF), (dims, ((), ())), preferred_element_type=F32)


def _to_heads(x):
    return jnp.concatenate([x[None, :, h * LANE:(h + 1) * LANE] for h in range(HG_HEADS)], axis=0)


def _from_heads(x):
    return jnp.concatenate([x[h] for h in range(HG_HEADS)], axis=1)


def _hdot(a, b, ca, cb):
    return lax.dot_general(a.astype(BF), b.astype(BF), (((ca,), (cb,)), ((0,), (0,))), preferred_element_type=F32)


def _hg_chunk_all(qh, fh, ih, gh, lbraw, nw, st):
    C = HG_CHUNK
    mx = jnp.maximum(lbraw[0:1], lbraw[1:2])
    e0 = jnp.exp(lbraw[0:1] - mx)
    e1 = jnp.exp(lbraw[1:2] - mx)
    lb = e0 / (e0 + e1)
    f = lb + (1.0 - lb) * jax.nn.sigmoid(fh)
    logf = jnp.log(f)
    k = 1.0 - f
    q = qh * jax.nn.sigmoid(qh)
    causal = lax.broadcasted_iota(jnp.int32, (C, C), 0) >= lax.broadcasted_iota(jnp.int32, (C, C), 1)
    b = lax.dot_general(causal.astype(F32), logf, (((1,), (0,)), ((), ())), precision=lax.Precision.HIGHEST,
                        preferred_element_type=F32)
    b_last = b[C - 1:C]
    b_ref = b[C // 2:C // 2 + 1]
    v3 = _to_heads(ih)
    st_out = st * _to_heads(jnp.exp(b_last)) + _hdot(v3, _to_heads(k * jnp.exp(b_last - b)), 1, 1)
    inter = _hdot(_to_heads(q * jnp.exp(b)), st, 2, 2)
    a = _hdot(_to_heads(q * jnp.exp(b - b_ref)), _to_heads(k * jnp.exp(b_ref - b)), 2, 2)
    a = jnp.where(causal[None], a, 0.0)
    o = inter + _hdot(a, v3, 2, 1)
    o = o * lax.rsqrt(jnp.mean(o * o, axis=-1, keepdims=True) + RMS_EPS) * nw
    return _from_heads(o) * (gh * jax.nn.sigmoid(gh)), st_out


HG_STEP = 4
HG_ROWS = HG_STEP * HG_CHUNK


def _hg_in_specs(step_of):
    half = 4 * LANE
    specs = []
    for base in (QH0, FH0, IH0, GH0):
        for part in range(2):
            specs.append(pl.BlockSpec((HG_ROWS, half), lambda n, col=base // 4 + part: (step_of(n), col)))
    return specs


def _hg_sections(sec, c):
    rows = slice(c * HG_CHUNK, (c + 1) * HG_CHUNK)
    return [jnp.concatenate([sec[2 * s][rows, :], sec[2 * s + 1][rows, :]], axis=1).astype(F32) for s in range(4)]


def _hg_fwd(proj, lb, nw):
    S = proj.shape[0]
    C = HG_CHUNK
    NC = S // C

    def body(*refs):
        sec = refs[:8]
        lb_ref, nw_ref, o_ref, st_ref, st_scr = refs[8:]

        @pl.when(pl.program_id(0) == 0)
        def _():
            st_scr[...] = jnp.zeros_like(st_scr)

        st = st_scr[...]
        for c in range(HG_STEP):
            st_ref[:, c] = st
            o, st = _hg_chunk_all(*_hg_sections(sec, c), lb_ref[...], nw_ref[...], st)
            o_ref[c * C:(c + 1) * C, :] = o.astype(o_ref.dtype)
        st_scr[...] = st

    return _pcall(
        body, grid=(NC // HG_STEP,),
        in_specs=_hg_in_specs(lambda n: n) + [pl.BlockSpec((2, HG_WIDTH), lambda n: (0, 0)), pl.BlockSpec((1, LANE), lambda n: (0, 0))],
        out_specs=[pl.BlockSpec((HG_ROWS, HG_WIDTH), lambda n: (n, 0)),
                   pl.BlockSpec((HG_HEADS, HG_STEP, LANE, LANE), lambda n: (0, n, 0, 0))],
        out_shape=[SDS((S, HG_WIDTH), BF), SDS((HG_HEADS, NC, LANE, LANE), F32)],
        scratch_shapes=[pltpu.VMEM((HG_HEADS, LANE, LANE), F32)], name="hg_fwd")(*([proj] * 8), lb, nw)


def _hg_bwd(proj, lb, nw, states, do_hg, dproj):
    S = proj.shape[0]
    C = HG_CHUNK
    NC = S // C

    def body(*refs):
        sec = refs[:8]
        lb_ref, nw_ref, st_ref, do_ref, dp_in, dp_ref, dlb_ref, dnw_ref, dst_scr, stage, sem = refs[8:]
        n = pl.program_id(0)
        slot = n % 2

        def step_copies(s):
            rows = pl.ds(pl.multiple_of((NS - 1 - n) * HG_ROWS, HG_ROWS), HG_ROWS)
            return [pltpu.make_async_copy(stage.at[s, t], dp_ref.at[rows, pl.ds((QH0 + 8 * t) * LANE, HG_WIDTH)], sem.at[s, t])
                    for t in range(4)]

        @pl.when(n == 0)
        def _():
            dst_scr[...] = jnp.zeros_like(dst_scr)
            dlb_ref[...] = jnp.zeros_like(dlb_ref)
            dnw_ref[...] = jnp.zeros_like(dnw_ref)

        @pl.when(n >= 2)
        def _():
            for cp in step_copies(slot):
                cp.wait()

        dst = dst_scr[...]
        for c in reversed(range(HG_STEP)):
            rows = slice(c * C, (c + 1) * C)
            _, vjp = jax.vjp(_hg_chunk_all, *_hg_sections(sec, c), lb_ref[...], nw_ref[...], st_ref[:, c])
            dq, df, di, dg, dlb, dnw, dst = vjp((do_ref[rows, :].astype(F32), dst))
            dlb_ref[...] += dlb
            dnw_ref[...] += dnw
            for t, val in enumerate((dq, df, di, dg)):
                stage[slot, t, rows, :] = val.astype(BF)
        dst_scr[...] = dst
        for cp in step_copies(slot):
            cp.start()

        @pl.when(n == NS - 1)
        def _():
            for cp in step_copies(1 - slot) + step_copies(slot):
                cp.wait()

    NS = NC // HG_STEP
    rev = lambda n: NS - 1 - n
    ospec = pl.BlockSpec((HG_ROWS, HG_WIDTH), lambda n: (rev(n), 0))
    lbs = pl.BlockSpec((2, HG_WIDTH), lambda n: (0, 0))
    nws = pl.BlockSpec((1, LANE), lambda n: (0, 0))
    outs = _pcall(
        body, grid=(NS,),
        in_specs=_hg_in_specs(rev) + [lbs, nws, pl.BlockSpec((HG_HEADS, HG_STEP, LANE, LANE), lambda n: (0, rev(n), 0, 0)), ospec, _ANY],
        out_specs=[_ANY, lbs, nws],
        out_shape=[SDS((S, IN_WIDTH), BF), SDS((2, HG_WIDTH), F32), SDS((1, LANE), F32)], aliases={12: 0},
        scratch_shapes=[pltpu.VMEM((HG_HEADS, LANE, LANE), F32), pltpu.VMEM((2, 4, HG_ROWS, HG_WIDTH), BF),
                        pltpu.SemaphoreType.DMA((2, 4))],
        name="hg_bwd")(*([proj] * 8), lb, nw, states, do_hg, dproj)
    return outs


def _merge(ga, gb, ba, bb):
    return jax.nn.sigmoid(ga) * ba + jax.nn.sigmoid(gb) * bb


_MT = 512


def _swiglu(a, b):
    return a * jax.nn.sigmoid(a) * b


def _cross(q, k, v):
    s = _bdot(q, k, ((1,), (1,))) * (HEAD_DIM ** -0.5)
    e = jnp.exp(s - jnp.max(s, axis=-1, keepdims=True))
    p = e / jnp.sum(e, axis=-1, keepdims=True)
    return _bdot(p, v, ((1,), (0,)))


def _cross_specs():
    q = pl.BlockSpec((_MT, LANE), lambda h, i: (i, h))
    k = pl.BlockSpec((MEM_LEN, LANE), lambda h, i: (0, h))
    v = pl.BlockSpec((MEM_LEN, LANE), lambda h, i: (0, CROSS_HEADS + h))
    return q, k, v


def _cross_fwd(qc, kvc):
    S = qc.shape[0]
    q, k, v = _cross_specs()

    def body(q_ref, k_ref, v_ref, o_ref):
        o_ref[...] = _cross(q_ref[...], k_ref[...], v_ref[...]).astype(o_ref.dtype)

    return _pcall(body, grid=(CROSS_HEADS, S // _MT), in_specs=[q, k, v], out_specs=q,
                          out_shape=SDS((S, CROSS_WIDTH), BF), name="cross_fwd")(qc, kvc, kvc)


def _cross_bwd(qc, kvc, doc):
    S = qc.shape[0]
    q, k, v = _cross_specs()

    def body(q_ref, k_ref, v_ref, do_ref, dq_ref, dk_ref, dv_ref):
        @pl.when(pl.program_id(1) == 0)
        def _():
            dk_ref[...] = jnp.zeros_like(dk_ref)
            dv_ref[...] = jnp.zeros_like(dv_ref)

        _, vjp = jax.vjp(_cross, q_ref[...], k_ref[...], v_ref[...])
        dq, dk, dv = vjp(do_ref[...])
        dq_ref[...] = dq.astype(dq_ref.dtype)
        dk_ref[...] += dk
        dv_ref[...] += dv

    return _pcall(body, grid=(CROSS_HEADS, S // _MT), in_specs=[q, k, v, q], out_specs=[q, k, k],
                          out_shape=[SDS((S, CROSS_WIDTH), BF), SDS((MEM_LEN, CROSS_WIDTH), F32), SDS((MEM_LEN, CROSS_WIDTH), F32)],
                          name="cross_bwd")(qc, kvc, kvc, doc)


def _local_step(x, mem, tgt, p, ready):
    h, h_a, h_b = _rms_fwd(x, p["ln_mix"], "rms_mix", split=IN_SPLIT)
    proj = _mm(h_a, p["w_in_a"], "nt", BF, "mm_proj_a")
    proj = _mm(h_b, p["w_in_b"], "nt", BF, "mm_proj_b", res=proj)
    o_att, lse = _attn_fwd(proj)
    o_hg, states = _hg_fwd(proj, p["hg_lb"], p["hg_norm"])
    ba = _mm(o_att, p["w_ba"], "nn", BF, "mm_ba")
    gates = [(proj, GA0 * LANE // _MT), (proj, GB0 * LANE // _MT)]

    def f32s(*vals):
        return [v.astype(F32) for v in vals]

    bb, merged = _mm_fused(o_hg, p["w_bb"], "nn", "mm_bb_merge", gates + [ba], [BF, BF], tn=_MT,
                           epi=lambda d, ga, gb, bav: (d, _merge(*f32s(ga, gb, bav), d)))
    x1 = _mm(merged, p["w_out"], "nn", F32, "mm_out", res=x)
    hc = _rms_fwd(x1, p["ln_cross"], "rms_cross")
    qc = _mm(hc, p["wq"], "nn", F32, "mm_q")
    mn = _rms_fwd(mem, p["ln_mem"], "rms_mem")
    kvc = _mm(mn, p["wkv"], "nn", F32, "mm_kv")
    oc = _cross_fwd(qc, kvc)
    x2 = _mm(oc, p["wo"], "nn", F32, "mm_o", res=x1)
    hf = _rms_fwd(x2, p["ln_ffn"], "rms_ffn")
    a = _mm(hf, p["w1"], "nt", BF, "mm_w1")
    b, g = _mm_fused(hf, p["w3"], "nt", "mm_w3_swiglu", [a], [BF, BF], lambda d, av: (d, _swiglu(av.astype(F32), d)))
    x3 = _mm(g, p["w2"], "nn", F32, "mm_w2", res=x2)
    loss, dx3, dx3b, d_ln_final = _final_loss(x3, p["ln_final"], tgt)

    gw, gs = {}, {"ln_final": d_ln_final}
    da, db = _mm_fused(dx3b, p["w2"], "nt", "mm_dg_swiglu", [a, b], [BF, BF],
                       lambda d, av, bv: jax.vjp(_swiglu, av.astype(F32), bv.astype(F32))[1](d))
    gw["w2"] = _mm(g, dx3b, "tn", BF, "mm_dw2")
    dhf = _mm(da, p["w1"], "nn", F32, "mm_dhf1")
    dhf = _mm(db, p["w3"], "nn", BF, "mm_dhf3", res=dhf)
    gw["w1"] = _mm(da, hf, "tn", BF, "mm_dw1")
    gw["w3"] = _mm(db, hf, "tn", BF, "mm_dw3")
    ready("ffn", gw)
    dx2, dx2b, gs["ln_ffn"] = _rms_bwd(x2, p["ln_ffn"], dhf, dx3, "rms_ffn_bwd")
    doc = _mm(dx2b, p["wo"], "nt", F32, "mm_doc")
    gw["wo"] = _mm(oc, dx2b, "tn", BF, "mm_dwo")
    dqc, dk, dv = _cross_bwd(qc, kvc, doc)
    dkvc = jnp.concatenate([dk, dv], axis=1)
    dhc = _mm(dqc, p["wq"], "nt", BF, "mm_dhc")
    gw["wq"] = _mm(hc, dqc, "tn", BF, "mm_dwq")
    dmn = _mm(dkvc, p["wkv"], "nt", F32, "mm_dmn")
    gw["wkv"] = _mm(mn, dkvc, "tn", BF, "mm_dwkv")
    ready("cross", gw)
    gs["ln_mem"] = _rms_bwd(mem, p["ln_mem"], dmn, None, "rms_mem_bwd")
    dx1, dx1b, gs["ln_cross"] = _rms_bwd(x1, p["ln_cross"], dhc, dx2, "rms_cross_bwd")
    def merge_back(d, ga, gb, bav, bbv):
        dga, dgb, dba_, dbb_ = jax.vjp(_merge, *f32s(ga, gb, bav, bbv))[1](d)
        return dba_, dbb_, dga, dgb

    dba, dbb, dproj = _mm_fused(dx1b, p["w_out"], "nt", "mm_dmerged_merge", gates + [ba, bb], [BF, BF], merge_back, tn=_MT,
                                scatter=(SDS((x.shape[0], IN_WIDTH), BF), [GA0 * LANE, GB0 * LANE]))
    gw["w_out"] = _mm(merged, dx1b, "tn", BF, "mm_dwout")
    do_att = _mm(dba, p["w_ba"], "nt", F32, "mm_doatt")
    gw["w_ba"] = _mm(o_att, dba, "tn", BF, "mm_dwba")
    do_hg = _mm(dbb, p["w_bb"], "nt", F32, "mm_dohg")
    gw["w_bb"] = _mm(o_hg, dbb, "tn", BF, "mm_dwbb")
    ready("mix", gw)
    dproj = _attn_bwd(proj, o_att, lse, do_att, dproj)
    dproj, gs["hg_lb"], gs["hg_norm"] = _hg_bwd(proj, p["hg_lb"], p["hg_norm"], states, do_hg, dproj)
    for q in range(IN_PARTS):
        gw[f"w_in_q{q}"] = _mm(dproj, h, "tn", BF, f"mm_dwin_q{q}", b_part=(q, IN_PARTS))
        ready(f"in_q{q}", gw)
        if q == 0:
            ready("mixers_done", gw)
    dh_a = _mm(dproj, p["w_in_a"], "nn", BF, "mm_dh_a")
    ready("dh_half", gw)
    dh = (dh_a, _mm(dproj, p["w_in_b"], "nn", BF, "mm_dh_b"))
    grad_x, _, gs["ln_mix"] = _rms_bwd(x, p["ln_mix"], dh, dx1, "rms_mix_bwd")
    return loss, grad_x, gs


IN_SPLIT = 1280
_MATS = (
    ("w_in_a", "w_in", IN_WIDTH, IN_SPLIT, "row", IN_WIDTH // N_DEV, True),
    ("w_in_b", "w_in", IN_WIDTH, D_MODEL - IN_SPLIT, "row", IN_WIDTH // N_DEV, True),
    ("w_ba", "w_branch_a", ATT_OUT, D_MODEL, "col", D_MODEL // N_DEV, False),
    ("w_bb", "w_branch_b", HG_WIDTH, D_MODEL, "col", D_MODEL // N_DEV, False),
    ("w_out", "w_out", D_MODEL, D_MODEL, "row", D_MODEL // N_DEV, False),
    ("wq", "wq_cross", D_MODEL, CROSS_WIDTH, "row", D_MODEL // N_DEV, False),
    ("wkv", "wkv_cross", D_MODEL, 2 * CROSS_WIDTH, "row", D_MODEL // N_DEV, False),
    ("wo", "wo_cross", CROSS_WIDTH, D_MODEL, "col", D_MODEL // N_DEV, False),
    ("w1", "w1", D_FF, D_MODEL, "row", D_FF // N_DEV, True),
    ("w3", "w3", D_FF, D_MODEL, "row", D_FF // N_DEV, True),
    ("w2", "w2", D_FF, D_MODEL, "row", D_FF // N_DEV, False),
    ("w_in_q0", "w_in", IN_WIDTH, D_MODEL // 2, "row", IN_WIDTH // N_DEV, True),
    ("w_in_q1", "w_in", IN_WIDTH, D_MODEL // 2, "row", IN_WIDTH // N_DEV, True),
)
IN_PARTS = 2
_INDEX = {m[0]: i for i, m in enumerate(_MATS)}


def _unit_shape(i):
    _, _, K, N, kind, sh, _ = _MATS[i]
    return (sh, N) if kind == "row" else (K, sh)


def _win(ref, i, chip, par, half=None):
    _, _, K, _, kind, sh, _ = _MATS[i]
    if kind == "row":
        rows, start = (sh, 0) if half is None else (sh // 2, half * (sh // 2))
        return ref.at[pl.ds(pl.multiple_of((2 * chip + par) * sh + start, 16), rows), :]
    rows, start = (K, 0) if half is None else (K // 2, half * (K // 2))
    return ref.at[pl.ds(start, rows), pl.ds(pl.multiple_of((2 * chip + par) * sh, LANE), sh)]


def _my_place():
    x, y, c = lax.axis_index("x"), lax.axis_index("y"), lax.axis_index("c")
    chips = [(1 - x, y), (x, 1 - y), (1 - x, 1 - y)]
    return x, y, c, chips


def _cast_unit(w, name, cols=None):
    K, N = w.shape
    c0, c1 = cols or (0, N)
    tr = _pick(K, (256, 400, 352))

    def body(w_ref, o_ref):
        o_ref[...] = w_ref[:, c0:c1].astype(BF)

    src = pl.BlockSpec((tr, N), lambda i: (i, 0))
    dst = pl.BlockSpec((tr, c1 - c0), lambda i: (i, 0))
    return _pcall(body, grid=(K // tr,), in_specs=[src], out_specs=dst, out_shape=SDS((K, c1 - c0), BF), name=name)(w)


def _handshake(peers):
    barrier = pltpu.get_barrier_semaphore()
    for peer in peers:
        pl.semaphore_signal(barrier, inc=1, device_id=peer, device_id_type=MESH)
    pl.semaphore_wait(barrier, len(peers))


def _sequencer(body, out_type, sems, cid, name):
    return pl.kernel(body, out_type=out_type, mesh=plsc.ScalarSubcoreMesh(axis_name="seq", num_cores=1),
                     scratch_types=sems, compiler_params=pltpu.CompilerParams(collective_id=cid), name=name)


def _all_gather(units, idx, cid, name):
    n = len(idx)

    def body(*refs):
        us, fulls = refs[:n], refs[n:2 * n]
        ssem, rsem, lsem = refs[2 * n:]
        x, y, c, _ = _my_place()
        xn, yn, dg = (1 - x, y), (x, 1 - y), (1 - x, 1 - y)
        me, sib = (x, y, c), (x, y, 1 - c)
        _handshake([sib, (*xn, c), (*yn, c)])

        def win(i, chip_xy, par, half=None):
            return _win(fulls[i], idx[i], 2 * chip_xy[0] + chip_xy[1], par, half)

        def rcopy(i, k, src, dst, to):
            return pltpu.make_async_remote_copy(src_ref=src, dst_ref=dst, send_sem=ssem.at[i, k], recv_sem=rsem.at[i, k],
                                                device_id=to, device_id_type=MESH)

        def landed(i, k, blk):
            rcopy(i, k, blk, blk, me).wait_recv()

        sends, locs = [], []

        def send(i, k, blk, to, src=None):
            cp = rcopy(i, k, blk if src is None else src, blk, to)
            cp.start()
            sends.append(cp)

        for i in range(n):
            mine = win(i, (x, y), c)
            loc = pltpu.make_async_copy(us[i], mine, lsem.at[i])
            loc.start()
            locs.append(loc)
            send(i, 0, mine, sib, src=us[i])
            send(i, 1, mine, (*xn, c), src=us[i])
            send(i, 2, mine, (*yn, c), src=us[i])
        for i in range(n):
            landed(i, 1, win(i, xn, c))
            send(i, 3, win(i, xn, c, 0), (*yn, c))
            send(i, 5, win(i, xn, c), sib)
            landed(i, 2, win(i, yn, c))
            send(i, 4, win(i, yn, c, 1), (*xn, c))
            send(i, 6, win(i, yn, c), sib)
        for i in range(n):
            landed(i, 3, win(i, dg, c, 0))
            landed(i, 4, win(i, dg, c, 1))
            send(i, 7, win(i, dg, c), sib)
        for i in range(n):
            landed(i, 0, win(i, (x, y), 1 - c))
            landed(i, 5, win(i, xn, 1 - c))
            landed(i, 6, win(i, yn, 1 - c))
            landed(i, 7, win(i, dg, 1 - c))
        for cp in sends:
            cp.wait_send()
        for loc in locs:
            loc.wait()

    out_type = [SDS((_MATS[i][2], _MATS[i][3]), BF) for i in idx]
    sems = [pltpu.SemaphoreType.DMA((n, 8)), pltpu.SemaphoreType.DMA((n, 8)), pltpu.SemaphoreType.DMA((n,))]
    return _sequencer(body, out_type, sems, cid, name)(*units)


def _rs_sibling(grads, idx, cid, name):
    n = len(idx)

    def body(*refs):
        gs, bufs = refs[:n], refs[n:2 * n]
        ssem, rsem = refs[2 * n:]
        x, y, c, _ = _my_place()
        _handshake([(x, y, 1 - c)])
        cps = []
        for i in range(n):
            for k in range(4):
                cp = pltpu.make_async_remote_copy(
                    src_ref=_win(gs[i], idx[i], k, 1 - c), dst_ref=bufs[i].at[k], send_sem=ssem.at[i, k], recv_sem=rsem.at[i, k],
                    device_id=(x, y, 1 - c), device_id_type=MESH)
                cp.start()
                cps.append(cp)
        for cp in cps:
            cp.wait()

    out_type = [SDS((4, *_unit_shape(i)), BF) for i in idx]
    sems = [pltpu.SemaphoreType.DMA((n, 4)), pltpu.SemaphoreType.DMA((n, 4))]
    return _sequencer(body, out_type, sems, cid, name)(*grads)


def _sibling_sum_tc(grad, i, cid, name):
    unit = _unit_shape(i)

    def body(g_ref, t_ref, theirs, mine, summed, ssem, rsem, lsem, osem):
        x, y, c, _ = _my_place()
        _handshake([(x, y, 1 - c)])
        sends, loads = [], []
        for k in range(4):
            sends.append(pltpu.make_async_remote_copy(
                src_ref=_win(g_ref, i, k, 1 - c), dst_ref=theirs.at[k], send_sem=ssem.at[k], recv_sem=rsem.at[k],
                device_id=(x, y, 1 - c), device_id_type=MESH))
            loads.append(pltpu.make_async_copy(_win(g_ref, i, k, c), mine.at[k], lsem.at[k]))
            sends[-1].start()
            loads[-1].start()
        stores = []
        for k in range(4):
            loads[k].wait()
            sends[k].wait_recv()
            summed[k] = (mine[k].astype(F32) + theirs[k].astype(F32)).astype(BF)
            stores.append(pltpu.make_async_copy(summed.at[k], t_ref.at[k], osem.at[k]))
            stores[-1].start()
        for k in range(4):
            sends[k].wait_send()
            stores[k].wait()

    block = pltpu.VMEM((4, *unit), BF)
    return _pcall(body, in_specs=[_ANY], out_specs=_ANY, out_shape=SDS((4, *unit), BF),
                  scratch_shapes=[block, block, block] + [pltpu.SemaphoreType.DMA((4,))] * 4,
                  compiler_params=_cp(VMEM_BIG, collective_id=cid), name=name)(grad)


def _chip_sum(g, sib, i, place, name):
    _, _, K, N, kind, sh, _ = _MATS[i]

    def body(pref, g_ref, s_ref, o_ref):
        o_ref[...] = (g_ref[...].astype(F32) + s_ref[...].astype(F32)).astype(BF)

    if kind == "row":
        tr = _pick(sh, (800, 352, 256))
        per = sh // tr
        grid = (4, per)
        g_spec = pl.BlockSpec((tr, N), lambda k, r, pref: ((2 * k + pref[0]) * per + r, 0))
        u_spec = pl.BlockSpec((None, tr, N), lambda k, r, pref: (k, r, 0))
    else:
        grid = (4,)
        g_spec = pl.BlockSpec((K, sh), lambda k, pref: (0, 2 * k + pref[0]))
        u_spec = pl.BlockSpec((None, K, sh), lambda k, pref: (k, 0, 0))
    return _pcall(body, grid=grid, in_specs=[g_spec, u_spec], out_specs=u_spec, out_shape=SDS((4, *_unit_shape(i)), BF),
                  prefetch=True, name=name)(place, g, sib)


def _rs_chips(parts, idx, cid, name):
    n = len(idx)

    def body(*refs):
        ts, bufs, relays = refs[:n], refs[n:2 * n], refs[2 * n:3 * n]
        ssem, rsem = refs[3 * n:]
        x, y, c, _ = _my_place()
        xn, yn, dg = (1 - x, y), (x, 1 - y), (1 - x, 1 - y)
        to_x, to_y, me = (*xn, c), (*yn, c), (x, y, c)
        _handshake([to_x, to_y])

        def rcopy(i, k, src, dst, to):
            return pltpu.make_async_remote_copy(src_ref=src, dst_ref=dst, send_sem=ssem.at[i, k], recv_sem=rsem.at[i, k],
                                                device_id=to, device_id_type=MESH)

        def halves(i):
            rows = _unit_shape(idx[i])[0] // 2
            return [pl.ds(0, rows), pl.ds(rows, rows)]

        sends = []

        def send(cp):
            cp.start()
            sends.append(cp)

        for i in range(n):
            h0, h1 = halves(i)
            diag = ts[i].at[2 * dg[0] + dg[1]]
            send(rcopy(i, 0, ts[i].at[2 * xn[0] + xn[1]], bufs[i].at[0], to_x))
            send(rcopy(i, 1, ts[i].at[2 * yn[0] + yn[1]], bufs[i].at[1], to_y))
            send(rcopy(i, 2, diag.at[h0], relays[i].at[0], to_x))
            send(rcopy(i, 3, diag.at[h1], relays[i].at[1], to_y))
        for i in range(n):
            h0, h1 = halves(i)
            rcopy(i, 2, relays[i].at[0], relays[i].at[0], me).wait_recv()
            send(rcopy(i, 4, relays[i].at[0], bufs[i].at[2, h0], to_y))
            rcopy(i, 3, relays[i].at[1], relays[i].at[1], me).wait_recv()
            send(rcopy(i, 5, relays[i].at[1], bufs[i].at[2, h1], to_x))
        for i in range(n):
            h0, h1 = halves(i)
            rcopy(i, 0, bufs[i].at[0], bufs[i].at[0], me).wait_recv()
            rcopy(i, 1, bufs[i].at[1], bufs[i].at[1], me).wait_recv()
            rcopy(i, 4, bufs[i].at[2, h0], bufs[i].at[2, h0], me).wait_recv()
            rcopy(i, 5, bufs[i].at[2, h1], bufs[i].at[2, h1], me).wait_recv()
        for cp in sends:
            cp.wait_send()

    def relay_shape(i):
        rows, cols = _unit_shape(i)
        return (2, rows // 2, cols)

    out_type = [SDS((3, *_unit_shape(i)), BF) for i in idx] + [SDS(relay_shape(i), BF) for i in idx]
    sems = [pltpu.SemaphoreType.DMA((n, 6)), pltpu.SemaphoreType.DMA((n, 6))]
    return _sequencer(body, out_type, sems, cid, name)(*parts)[:n]


def _adamw(w, g, m, v):
    m = ADAM_B1 * m + (1.0 - ADAM_B1) * g
    v = ADAM_B2 * v + (1.0 - ADAM_B2) * jnp.square(g)
    m_hat = m / (1.0 - ADAM_B1 ** ADAM_STEP)
    v_hat = v / (1.0 - ADAM_B2 ** ADAM_STEP)
    delta = -ADAM_LR * (m_hat / (jnp.sqrt(v_hat) + ADAM_EPS) + ADAM_WD * w)
    return delta, m, v


def _adam_mat(parts, recvs, w, m, v, place, name):
    R, C = w.shape
    n = len(parts)
    tr = _pick(R, (256, 160, 176))

    def body(pref, *refs):
        t_refs, r_refs = refs[:n], refs[n:2 * n]
        w_ref, m_ref, v_ref, g_out, d_out, m_out, v_out = refs[2 * n:]
        cols = [t[...].astype(F32) + r[0].astype(F32) + r[1].astype(F32) + r[2].astype(F32) for t, r in zip(t_refs, r_refs)]
        g = jnp.concatenate(cols, axis=1) if n > 1 else cols[0]
        d, mn, vn = _adamw(w_ref[...], g, m_ref[...], v_ref[...])
        g_out[...] = g
        d_out[...] = d
        m_out[...] = mn
        v_out[...] = vn

    t_spec = pl.BlockSpec((None, tr, C // n), lambda i, pref: (pref[1], i, 0))
    r_spec = pl.BlockSpec((3, tr, C // n), lambda i, pref: (0, i, 0))
    s_spec = pl.BlockSpec((tr, C), lambda i, pref: (i, 0))
    return _pcall(body, grid=(R // tr,), in_specs=[t_spec] * n + [r_spec] * n + [s_spec] * 3, out_specs=[s_spec] * 4,
                  out_shape=[SDS((R, C), F32)] * 4, prefetch=True, name=name)(place, *parts, *recvs, w, m, v)


_SMALL = (("ln_mix", "ln_mix_w", D_MODEL), ("ln_cross", "ln_cross_w", D_MODEL), ("ln_mem", "ln_mem_w", D_MODEL),
          ("ln_ffn", "ln_ffn_w", D_MODEL), ("ln_final", "ln_final_w", D_MODEL), ("hg_lb", "hg_lower_bounds", 2 * HG_WIDTH),
          ("hg_norm", "hg_norm_w", HEAD_DIM))
_PACK = sum(n for _, _, n in _SMALL) + LANE


def _small_sync(grow, wrow, mrow, vrow):
    def body(g_ref, w_ref, m_ref, v_ref, o_ref, gath, ssem, rsem):
        x, y, c, _ = _my_place()
        me = 4 * x + 2 * y + c
        gath[pl.ds(me, 1), :] = g_ref[...]
        cps = []
        for dlt in range(1, N_DEV):
            peer = (me + dlt) % N_DEV
            cp = pltpu.make_async_remote_copy(
                src_ref=g_ref, dst_ref=gath.at[pl.ds(me, 1), :], send_sem=ssem.at[peer], recv_sem=rsem.at[me],
                device_id=(peer // 4, (peer // 2) % 2, peer % 2), device_id_type=MESH)
            cp.start()
            cps.append(cp)
        for dlt in range(1, N_DEV):
            src = (me + dlt) % N_DEV
            pltpu.make_async_remote_copy(
                src_ref=g_ref, dst_ref=gath.at[pl.ds(src, 1), :], send_sem=ssem.at[src], recv_sem=rsem.at[src],
                device_id=(x, y, c), device_id_type=MESH).wait_recv()
        for cp in cps:
            cp.wait_send()
        g = gath[0:1, :]
        for j in range(1, N_DEV):
            g = g + gath[j:j + 1, :]
        d, mn, vn = _adamw(w_ref[...], g, m_ref[...], v_ref[...])
        o_ref[0:1, :] = g
        o_ref[1:2, :] = d
        o_ref[2:3, :] = mn
        o_ref[3:4, :] = vn

    vm = pl.BlockSpec(memory_space=pltpu.VMEM)
    return _pcall(
        body, in_specs=[vm] * 4, out_specs=vm, out_shape=SDS((4, _PACK), F32),
        scratch_shapes=[pltpu.VMEM((N_DEV, _PACK), F32), pltpu.SemaphoreType.DMA((N_DEV,)), pltpu.SemaphoreType.DMA((N_DEV,))],
        name="small_sync")(grow, wrow, mrow, vrow)


def _pack_small(d, loss_row=None):
    parts = [d[k].reshape(1, n).astype(F32) for k, _, n in _SMALL]
    parts.append(loss_row if loss_row is not None else jnp.ones((1, LANE), F32))
    return jnp.concatenate(parts, axis=1)


def kernel(x, mem, ln_mix_w, w_in, hg_norm_w, hg_lower_bounds, w_branch_a, w_branch_b, w_out, ln_cross_w, ln_mem_w, wq_cross, wkv_cross, wo_cross, ln_ffn_w, w1, w3, w2, ln_final_w, loss_target, m_ln_mix_w, m_w_in, m_hg_norm_w, m_hg_lower_bounds, m_w_branch_a, m_w_branch_b, m_w_out, m_ln_cross_w, m_ln_mem_w, m_wq_cross, m_wkv_cross, m_wo_cross, m_ln_ffn_w, m_w1, m_w3, m_w2, m_ln_final_w, v_ln_mix_w, v_w_in, v_hg_norm_w, v_hg_lower_bounds, v_w_branch_a, v_w_branch_b, v_w_out, v_ln_cross_w, v_ln_mem_w, v_wq_cross, v_wkv_cross, v_wo_cross, v_ln_ffn_w, v_w1, v_w3, v_w2, v_ln_final_w):
    given = dict(locals())
    place = jnp.stack([lax.axis_index("c"), 2 * lax.axis_index("x") + lax.axis_index("y")]).astype(jnp.int32)

    _ORDER[0] = None
    index = _INDEX

    def local_view(a, i):
        return a[0].T if _MATS[i][6] else a[0]

    shard = {key: local_view(given[ref], i) for i, (key, ref, *_) in enumerate(_MATS)}

    p = {}
    col_part = {"w_in_a": (0, IN_SPLIT), "w_in_b": (IN_SPLIT, D_MODEL)}
    waves = (("in_a", ["w_in_a"], 1), ("in_b", ["w_in_b"], 12), ("mid", ["w_ba", "w_bb", "w_out", "wq", "wkv", "wo"], 2),
             ("w1", ["w1"], 11), ("w3", ["w3"], 15), ("w2", ["w2"], 16))
    for wave, keys, cid in waves:
        units = [_cast_unit(shard[k], "cast_" + k, col_part.get(k)) for k in keys]
        p.update(zip(keys, _all_gather(units, [index[k] for k in keys], cid, "all_gather_" + wave)))
    small_w = {"ln_mix": ln_mix_w, "ln_cross": ln_cross_w, "ln_mem": ln_mem_w, "ln_ffn": ln_ffn_w,
               "ln_final": ln_final_w.reshape(1, D_MODEL), "hg_lb": hg_lower_bounds, "hg_norm": hg_norm_w}
    p.update(small_w)

    groups = {"ffn": ["w2", "w1", "w3"], "cross": ["wo", "wq", "wkv"], "mix": ["w_out", "w_ba", "w_bb"]}
    cids = {"ffn": (3, 4), "cross": (5, 6), "mix": (7, 8)}
    for q in range(IN_PARTS):
        groups[f"in_q{q}"] = [f"w_in_q{q}"]
        cids[f"in_q{q}"] = ((9, 10), (13, 14), (17, 18), (19, 20))[q]
    sib, parts, recv, out = {}, {}, {}, {}

    grads = {}

    def chip_stage(name):
        keys = groups[name]
        parts[name] = [_chip_sum(grads[k], s, index[k], place, "chip_sum_" + k) for k, s in zip(keys, sib[name])]
        recv[name] = list(_rs_chips(parts[name], [index[k] for k in keys], cids[name][1], "rs_chips_" + name))

    def adam(ref, i, part_list, recv_list):
        res = _adam_mat(part_list, recv_list, shard[_MATS[i][0]], local_view(given["m_" + ref], i), local_view(given["v_" + ref], i),
                        place, "adam_" + ref)
        out[ref] = [(a.T if _MATS[i][6] else a)[None] for a in res]

    def adam_stage(name, only=None, skip=None):
        for n, (k, t, r) in enumerate(zip(groups[name], parts[name], recv[name])):
            if (only is None or n == only) and n != skip:
                adam(_MATS[index[k]][1], index[k], [t], [r])

    def ready(name, gw):
        if name == "mixers_done":
            adam_stage("ffn", only=0)
            chip_stage("mix")
            adam_stage("ffn", skip=0)
            adam_stage("cross")
            return
        if name == "dh_half":
            adam_stage("mix")
            return
        keys = groups[name]
        grads.update({k: gw[k] for k in keys})
        if name.startswith("in_q"):
            parts[name] = [_sibling_sum_tc(gw[k], index[k], cids[name][0], "sibling_sum_" + k) for k in keys]
            recv[name] = list(_rs_chips(parts[name], [index[k] for k in keys], cids[name][1], "rs_chips_" + name))
            if name == "in_q1":
                _ALSO_AFTER.extend(recv["in_q0"])
            return
        sib[name] = _rs_sibling([gw[k] for k in keys], [index[k] for k in keys], cids[name][0], "rs_sibling_" + name)
        if name == "cross":
            chip_stage("ffn")
        if name == "mix":
            chip_stage("cross")

    loss_row, grad_x, gs = _local_step(x[0], mem[0], loss_target[0], p, ready)

    sm = {k: given["m_" + ref] for k, ref, _ in _SMALL}
    sv = {k: given["v_" + ref] for k, ref, _ in _SMALL}
    synced = _small_sync(_pack_small(gs, loss_row), _pack_small(small_w), _pack_small(sm), _pack_small(sv))
    in_groups = [f"in_q{q}" for q in range(IN_PARTS)]
    adam("w_in", index["w_in_a"], sum((parts[g] for g in in_groups), []), sum((recv[g] for g in in_groups), []))
    off = 0
    for k, ref, n in _SMALL:
        shape = given[ref].shape
        out[ref] = [synced[r, off:off + n].reshape(shape) for r in range(4)]
        off += n
    loss = synced[0, off]

    order = ["ln_mix_w", "w_in", "hg_norm_w", "hg_lower_bounds", "w_branch_a", "w_branch_b", "w_out", "ln_cross_w", "ln_mem_w",
             "wq_cross", "wkv_cross", "wo_cross", "ln_ffn_w", "w1", "w3", "w2", "ln_final_w"]
    return (loss, grad_x[None], *[out[n][0] for n in order], *[out[n][1] for n in order],
            *[out[n][2] for n in order], *[out[n][3] for n in order])
```

```python
import jax
import jax.numpy as jnp
from jax import lax
from jax.experimental import pallas as pl
from jax.experimental.pallas import tpu as pltpu
from jax.experimental.pallas import tpu_sc as plsc

F32 = jnp.float32
BF = jnp.bfloat16
SDS = jax.ShapeDtypeStruct

D_MODEL = 2048
SEQ = 2048
HEAD_DIM = 128
MEM_LEN = 256
ATT_GROUPS = ((128, 1), (512, 4), (2048, 16))
ATT_HEADS = 4
ATT_WIDTH = 1536
ATT_OUT = 512
HG_HEADS = 8
HG_WIDTH = 1024
HG_CHUNK = 64
IN_WIDTH = 12800
CROSS_HEADS = 4
CROSS_WIDTH = 512
D_FF = 5632
RMS_EPS = 1e-6
ADAM_LR = 0.001
ADAM_B1 = 0.9
ADAM_B2 = 0.999
ADAM_EPS = 1e-08
ADAM_WD = 0.01
ADAM_STEP = 10
N_DEV = 8

LANE = 128
QA0, KA0, VA0 = 0, 12, 24
QH0, FH0, IH0, GH0 = 36, 44, 52, 60
GA0, GB0 = 68, 84

VMEM_BIG = 56 * 1024 * 1024
MESH = pl.DeviceIdType.MESH


def _cp(vmem=None, **kw):
    if vmem is not None:
        kw["vmem_limit_bytes"] = vmem
    return pltpu.CompilerParams(**kw)


def _pick(n, cands):
    for c in cands:
        if n % c == 0:
            return c
    raise ValueError(f"no tile for {n}")


_ANY = pl.BlockSpec(memory_space=pl.ANY)
_ORDER = [None]
_ALSO_AFTER = []


def _pcall(body, *, name, in_specs, out_specs, out_shape, grid=(), scratch_shapes=(), compiler_params=None, prefetch=False,
           aliases=None):
    def run(*args):
        dep = _ORDER[0]
        specs, operands, kernel_fn = list(in_specs), list(args), body
        deps = [] if dep is None or any(dep is a for a in operands) else [dep]
        deps += _ALSO_AFTER
        del _ALSO_AFTER[:]
        if deps:
            at, nd = len(operands), len(deps)
            specs += [_ANY] * nd
            operands += deps

            def kernel_fn(*refs):
                return body(*refs[:at], *refs[at + nd:])
        if prefetch:
            call = pl.pallas_call(
                kernel_fn, out_shape=out_shape, name=name, compiler_params=compiler_params,
                grid_spec=pltpu.PrefetchScalarGridSpec(num_scalar_prefetch=1, grid=grid, in_specs=specs, out_specs=out_specs,
                                                       scratch_shapes=scratch_shapes))
        else:
            call = pl.pallas_call(kernel_fn, grid=grid, in_specs=specs, out_specs=out_specs, out_shape=out_shape,
                                  scratch_shapes=scratch_shapes, compiler_params=compiler_params, name=name,
                                  input_output_aliases=aliases or {})
        out = call(*operands)
        _ORDER[0] = out[0] if isinstance(out, (list, tuple)) else out
        return out
    return run


def _mm(a, b, form, out_dtype, name, res=None, tiles=None, a_part=(0, 1), b_part=(0, 1)):
    a_shape = (a.shape[0], a.shape[1] // a_part[1])
    b_shape = (b.shape[0], b.shape[1] // b_part[1])
    if form == "nn":
        (M, K), (K2, N) = a_shape, b_shape
    elif form == "nt":
        (M, K), (N, K2) = a_shape, b_shape
    else:
        (K, M), (K2, N) = a_shape, b_shape
    assert K == K2, (a.shape, b.shape, form)
    if tiles is None:
        tm = _pick(M, (1024, 1280, 1408, 512, 256))
        tn = _pick(N, (1280, 1024, 1408, 768, 512, 256))
        tk = K if K <= 2048 else _pick(K, (D_FF, 2560))
        if tk == D_FF:
            tm = _pick(M, (512, 256))
    else:
        tm, tn, tk = tiles
    nk = K // tk
    ah, bh = a_part[0], b_part[0]
    if form == "nn":
        a_spec = pl.BlockSpec((tm, tk), lambda i, j, k: (i, k + ah * nk))
        b_spec = pl.BlockSpec((tk, tn), lambda i, j, k: (k, j + bh * (N // tn)))
        dims = ((1,), (0,))
    elif form == "nt":
        a_spec = pl.BlockSpec((tm, tk), lambda i, j, k: (i, k + ah * nk))
        b_spec = pl.BlockSpec((tn, tk), lambda i, j, k: (j, k + bh * nk))
        dims = ((1,), (1,))
    else:
        a_spec = pl.BlockSpec((tk, tm), lambda i, j, k: (k, i + ah * (M // tm)))
        b_spec = pl.BlockSpec((tk, tn), lambda i, j, k: (k, j + bh * (N // tn)))
        dims = ((0,), (0,))
    o_spec = pl.BlockSpec((tm, tn), lambda i, j, k: (i, j))
    in_specs = [a_spec, b_spec]
    args = [a, b]
    if res is not None:
        in_specs.append(o_spec)
        args.append(res)

    def body(*refs):
        a_ref, b_ref = refs[0], refs[1]
        r_ref = refs[2] if res is not None else None
        o_ref = refs[3] if res is not None else refs[2]
        acc = refs[-1] if nk > 1 else None
        k = pl.program_id(2)
        d = lax.dot_general(a_ref[...].astype(BF), b_ref[...].astype(BF), (dims, ((), ())), preferred_element_type=F32)

        def finish(r):
            if res is not None:
                r = r + r_ref[...].astype(F32)
            o_ref[...] = r.astype(o_ref.dtype)

        if nk == 1:
            finish(d)
        else:
            @pl.when(k == 0)
            def _():
                acc[...] = d

            @pl.when((k > 0) & (k < nk - 1))
            def _():
                acc[...] += d

            @pl.when(k == nk - 1)
            def _():
                finish(acc[...] + d)

    return _pcall(
        body, grid=(M // tm, N // tn, nk), in_specs=in_specs, out_specs=o_spec,
        out_shape=SDS((M, N), out_dtype), scratch_shapes=[pltpu.VMEM((tm, tn), F32)] if nk > 1 else [],
        compiler_params=_cp(VMEM_BIG, dimension_semantics=("parallel", "parallel", "arbitrary")),
        name=name)(*args)


_MXU_COLS = 256


def _mm_fused(a, b, form, name, extras, out_dtypes, epi, tn=None, scatter=None):
    M, K = a.shape
    N = b.shape[1] if form == "nn" else b.shape[0]
    assert K <= 2048 and K == (b.shape[0] if form == "nn" else b.shape[1])
    tm = _pick(M, (1024, 512, 256))
    tn = tn or _pick(N, (1280, 1024, 1408, 512, 256))
    ne, no = len(extras), len(out_dtypes)
    bases = scatter[1] if scatter else []
    dims = ((1,), (0,)) if form == "nn" else ((1,), (1,))

    def body(*refs):
        a_ref, b_ref, ex, outs = refs[0], refs[1], refs[2:2 + ne], refs[2 + ne:2 + ne + no]
        av = a_ref[...].astype(BF)
        if scatter:
            wide_ref, stage, sem = refs[2 + ne + no:]
        for c0 in range(0, tn, _MXU_COLS):
            cols = slice(c0, min(c0 + _MXU_COLS, tn))
            bc = b_ref[:, cols] if form == "nn" else b_ref[cols, :]
            d = lax.dot_general(av, bc.astype(BF), (dims, ((), ())), preferred_element_type=F32)
            res = epi(d, *[e[:, cols] for e in ex])
            for o_ref, r in zip(outs, res[:no]):
                o_ref[:, cols] = r.astype(o_ref.dtype)
            for s, r in enumerate(res[no:]):
                stage[s, :, cols] = r.astype(BF)
        if scatter:
            rows = pl.ds(pl.multiple_of(pl.program_id(0) * tm, tm), tm)
            cps = [pltpu.make_async_copy(stage.at[s], wide_ref.at[rows, pl.ds(pl.multiple_of(base + pl.program_id(1) * tn, LANE), tn)],
                                         sem.at[s]) for s, base in enumerate(bases)]
            for cp in cps:
                cp.start()
            for cp in cps:
                cp.wait()

    a_spec = pl.BlockSpec((tm, K), lambda i, j: (i, 0))
    b_spec = pl.BlockSpec((K, tn), lambda i, j: (0, j)) if form == "nn" else pl.BlockSpec((tn, K), lambda i, j: (j, 0))
    o_spec = pl.BlockSpec((tm, tn), lambda i, j: (i, j))
    ex_specs, ex_args = [], []
    for e in extras:
        arr, off = e if isinstance(e, tuple) else (e, 0)
        ex_specs.append(pl.BlockSpec((tm, tn), lambda i, j, off=off: (i, off + j)))
        ex_args.append(arr)
    out_specs, out_shape, scratch = [o_spec] * no, [SDS((M, N), dt) for dt in out_dtypes], []
    if scatter:
        out_specs, out_shape = out_specs + [_ANY], out_shape + [scatter[0]]
        scratch = [pltpu.VMEM((len(bases), tm, tn), BF), pltpu.SemaphoreType.DMA((len(bases),))]
    return _pcall(
        body, grid=(M // tm, N // tn), in_specs=[a_spec, b_spec] + ex_specs, out_specs=out_specs, out_shape=out_shape,
        scratch_shapes=scratch, compiler_params=_cp(VMEM_BIG, dimension_semantics=("parallel", "parallel")),
        name=name)(a, b, *ex_args)


def _rms(x, w):
    return x * lax.rsqrt(jnp.mean(x * x, axis=-1, keepdims=True) + RMS_EPS) * w


def _rms_fwd(x, w, name, split=None):
    R, D = x.shape
    tr = 256
    widths = [D] if split is None else [D, split, D - split]

    def body(x_ref, w_ref, o_ref, *part_refs):
        y = _rms(x_ref[...], w_ref[...]).astype(o_ref.dtype)
        o_ref[...] = y
        if split is not None:
            part_refs[0][...] = y[:, :split]
            part_refs[1][...] = y[:, split:]

    outs = _pcall(
        body, grid=(R // tr,),
        in_specs=[pl.BlockSpec((tr, D), lambda i: (i, 0)), pl.BlockSpec((1, D), lambda i: (0, 0))],
        out_specs=[pl.BlockSpec((tr, n), lambda i: (i, 0)) for n in widths], out_shape=[SDS((R, n), BF) for n in widths],
        name=name)(x, w)
    return outs[0] if split is None else outs


def _rms_bwd(x, w, dh, dres, name):
    R, D = x.shape
    tr = _pick(R, (512, 256))
    want_dx = dres is not None
    dhs = dh if isinstance(dh, (tuple, list)) else (dh,)
    nd = len(dhs)

    def body(*refs):
        x_ref, w_ref, dh_refs = refs[0], refs[1], refs[2:2 + nd]
        if want_dx:
            dr_ref, dx_ref, dxb_ref, dw_ref = refs[2 + nd:]
        else:
            dw_ref = refs[2 + nd]

        @pl.when(pl.program_id(0) == 0)
        def _():
            dw_ref[...] = jnp.zeros_like(dw_ref)

        dhv = jnp.concatenate([r[...].astype(F32) for r in dh_refs], axis=1) if nd > 1 else dh_refs[0][...].astype(F32)
        _, vjp = jax.vjp(_rms, x_ref[...], w_ref[...])
        dx, dw = vjp(dhv)
        dw_ref[...] += dw
        if want_dx:
            dx = dx + dr_ref[...]
            dx_ref[...] = dx
            dxb_ref[...] = dx.astype(BF)

    row = pl.BlockSpec((tr, D), lambda i: (i, 0))
    parts = [pl.BlockSpec((tr, d.shape[1]), lambda i: (i, 0)) for d in dhs]
    vec = pl.BlockSpec((1, D), lambda i: (0, 0))
    if want_dx:
        return _pcall(body, grid=(R // tr,), in_specs=[row, vec] + parts + [row], out_specs=[row, row, vec],
                      out_shape=[SDS((R, D), F32), SDS((R, D), BF), SDS((1, D), F32)], name=name)(x, w, *dhs, dres)
    return _pcall(body, grid=(R // tr,), in_specs=[row, vec] + parts, out_specs=vec,
                  out_shape=SDS((1, D), F32), name=name)(x, w, *dhs)


def _final_loss(x3, w, tgt):
    R, D = x3.shape
    tr = _pick(R, (512, 256))

    def loss_fn(xv, wv, tv):
        err = _rms(xv, wv) - tv
        return 0.5 * jnp.sum(jnp.mean(err * err, axis=-1))

    def body(x_ref, w_ref, t_ref, loss_ref, dx_ref, dxb_ref, dw_ref):
        @pl.when(pl.program_id(0) == 0)
        def _():
            dw_ref[...] = jnp.zeros_like(dw_ref)
            loss_ref[...] = jnp.zeros_like(loss_ref)

        tv = t_ref[...]
        val, vjp = jax.vjp(lambda a, b: loss_fn(a, b, tv), x_ref[...], w_ref[...])
        dx, dw = vjp(jnp.ones((), F32))
        dx_ref[...] = dx
        dxb_ref[...] = dx.astype(BF)
        dw_ref[...] += dw
        loss_ref[...] += jnp.full(loss_ref.shape, val, F32)

    row = pl.BlockSpec((tr, D), lambda i: (i, 0))
    vec = pl.BlockSpec((1, D), lambda i: (0, 0))
    one = pl.BlockSpec((1, LANE), lambda i: (0, 0))
    return _pcall(body, grid=(R // tr,), in_specs=[row, vec, row], out_specs=[one, row, row, vec],
                          out_shape=[SDS((1, LANE), F32), SDS((R, D), F32), SDS((R, D), BF), SDS((1, D), F32)],
                          name="final_loss")(x3, w, tgt)


NEG = -1e30


def _att_blocks(S, d):
    L = S // d
    nb = L // LANE
    return nb, (2 if nb > 1 else 1)


def _att_rows(S, d, idx):
    nb, nkb = _att_blocks(S, d)
    r = idx // nb
    n = idx % nb
    kb = jnp.maximum(n - 1, 0)
    if d == 1:
        qrows = pl.ds(pl.multiple_of(n * LANE, LANE), LANE)
        krows = pl.ds(pl.multiple_of(kb * LANE, LANE), LANE * nkb)
    else:
        qrows = pl.ds(r + n * (LANE * d), LANE, stride=d)
        krows = pl.ds(r + kb * (LANE * d), LANE * nkb, stride=d)
    qpos = n * LANE + lax.broadcasted_iota(jnp.int32, (LANE, LANE * nkb), 0)
    kpos = kb * LANE + lax.broadcasted_iota(jnp.int32, (LANE, LANE * nkb), 1)
    return qrows, krows, qpos - kpos


def _slab(col):
    return pl.BlockSpec((SEQ, LANE), lambda h, col=col: (0, col + h))


def _widen(slabs, wide):
    for t, ref in enumerate(slabs):
        wide[t] = ref[...].astype(F32)
    return tuple(tuple(wide.at[3 * kind + g] for g in range(3)) for kind in range(3))


def _attn_fwd(proj):
    S = proj.shape[0]
    scale = HEAD_DIM ** -0.5

    def body(q0, q1, q2, k0, k1, k2, v0, v1, v2, o_ref, lse_ref, og, lg, wide):
        qs, ks, vs = _widen((q0, q1, q2, k0, k1, k2, v0, v1, v2), wide)
        for g, (window, d) in enumerate(ATT_GROUPS):
            def blk(idx, carry, g=g, window=window, d=d):
                qrows, krows, dist = _att_rows(S, d, idx)
                q = qs[g][qrows, :].astype(BF)
                k = ks[g][krows, :].astype(BF)
                v = vs[g][krows, :].astype(BF)
                s = lax.dot_general(q, k, (((1,), (1,)), ((), ())), preferred_element_type=F32) * scale
                s = jnp.where((dist >= 0) & (dist <= window // d), s, NEG)
                m = jnp.max(s, axis=-1, keepdims=True)
                p = jnp.exp(s - m)
                l = jnp.sum(p, axis=-1, keepdims=True)
                o = jnp.dot((p / l).astype(BF), v, preferred_element_type=F32)
                og[g, qrows, :] = o
                lg[g, qrows, :] = jnp.broadcast_to(m + jnp.log(l), (LANE, LANE))
                return carry
            lax.fori_loop(0, S // LANE, blk, 0, unroll=True)

        def merge(t, carry):
            rows = pl.ds(pl.multiple_of(t * 256, 256), 256)
            l0, l1, l2 = lg[0, rows, :], lg[1, rows, :], lg[2, rows, :]
            m = jnp.maximum(jnp.maximum(l0, l1), l2)
            w0, w1, w2 = jnp.exp(l0 - m), jnp.exp(l1 - m), jnp.exp(l2 - m)
            den = w0 + w1 + w2
            o_ref[rows, :] = (w0 * og[0, rows, :] + w1 * og[1, rows, :] + w2 * og[2, rows, :]) / den
            lse_ref[rows, :] = m + jnp.log(den)
            return carry
        lax.fori_loop(0, S // 256, merge, 0)

    ins = [_slab(QA0 + 4 * g) for g in range(3)] + [_slab(KA0 + 4 * g) for g in range(3)] + [_slab(VA0 + 4 * g) for g in range(3)]
    out = pl.BlockSpec((SEQ, LANE), lambda h: (0, h))
    return _pcall(
        body, grid=(ATT_HEADS,), in_specs=ins, out_specs=[out, out],
        out_shape=[SDS((S, ATT_OUT), F32), SDS((S, ATT_OUT), F32)],
        scratch_shapes=[pltpu.VMEM((3, S, LANE), F32), pltpu.VMEM((3, S, LANE), F32), pltpu.VMEM((9, S, LANE), F32)],
        compiler_params=_cp(VMEM_BIG), name="attn_fwd")(*([proj] * 9))


def _attn_bwd(proj, o_att, lse, do_att, dproj):
    S = proj.shape[0]
    scale = HEAD_DIM ** -0.5

    def body(q0, q1, q2, k0, k1, k2, v0, v1, v2, o_ref, lse_ref, do_ref, dp_in, dp_ref, dqa, dka, dva, delta, stage, sem, wide):
        qs, ks, vs = _widen((q0, q1, q2, k0, k1, k2, v0, v1, v2), wide)
        head = pl.program_id(0)

        def slab_copies(g):
            return [pltpu.make_async_copy(stage.at[t], dp_ref.at[:, pl.ds(pl.multiple_of((base + 4 * g + head) * LANE, LANE), LANE)],
                                          sem.at[t]) for t, base in enumerate((QA0, KA0, VA0))]

        delta[...] = jnp.broadcast_to(jnp.sum(do_ref[...] * o_ref[...], axis=-1, keepdims=True), delta.shape)
        for g, (window, d) in enumerate(ATT_GROUPS):
            dka[...] = jnp.zeros_like(dka)
            dva[...] = jnp.zeros_like(dva)

            def blk(idx, carry, g=g, window=window, d=d):
                qrows, krows, dist = _att_rows(S, d, idx)
                q = qs[g][qrows, :].astype(BF)
                k = ks[g][krows, :].astype(BF)
                v = vs[g][krows, :].astype(BF)
                do = do_ref[qrows, :]
                s = lax.dot_general(q, k, (((1,), (1,)), ((), ())), preferred_element_type=F32) * scale
                p = jnp.exp(s - lse_ref[qrows, :][:, 0:1])
                p = jnp.where((dist >= 0) & (dist <= window // d), p, 0.0)
                dob = do.astype(BF)
                dva[krows, :] += lax.dot_general(p.astype(BF), dob, (((0,), (0,)), ((), ())), preferred_element_type=F32)
                dp = lax.dot_general(dob, v, (((1,), (1,)), ((), ())), preferred_element_type=F32)
                ds = (p * (dp - delta[qrows, :][:, 0:1]) * scale).astype(BF)
                dqa[qrows, :] = jnp.dot(ds, k, preferred_element_type=F32)
                dka[krows, :] += lax.dot_general(ds, q, (((0,), (0,)), ((), ())), preferred_element_type=F32)
                return carry
            lax.fori_loop(0, S // LANE, blk, 0, unroll=True)
            if g > 0:
                for cp in slab_copies(g - 1):
                    cp.wait()
            stage[0] = dqa[...].astype(BF)
            stage[1] = dka[...].astype(BF)
            stage[2] = dva[...].astype(BF)
            for cp in slab_copies(g):
                cp.start()
        for cp in slab_copies(len(ATT_GROUPS) - 1):
            cp.wait()

    cols = [QA0 + 4 * g for g in range(3)] + [KA0 + 4 * g for g in range(3)] + [VA0 + 4 * g for g in range(3)]
    ins = [_slab(c) for c in cols]
    one = pl.BlockSpec((SEQ, LANE), lambda h: (0, h))
    return _pcall(
        body, grid=(ATT_HEADS,), in_specs=ins + [one, one, one, _ANY], out_specs=_ANY,
        out_shape=SDS((S, IN_WIDTH), BF), aliases={12: 0},
        scratch_shapes=[pltpu.VMEM((S, LANE), F32)] * 4 + [pltpu.VMEM((3, S, LANE), BF), pltpu.SemaphoreType.DMA((3,)),
                                                           pltpu.VMEM((9, S, LANE), F32)],
        compiler_params=_cp(VMEM_BIG), name="attn_bwd")(*([proj] * 9), o_att, lse, do_att, dproj)


def _bdot(a, b, dims):
    return lax.dot_general(a.astype(BF), b.astype(BF), (dims, ((), ())), preferred_element_type=F32)


def _to_heads(x):
    return jnp.concatenate([x[None, :, h * LANE:(h + 1) * LANE] for h in range(HG_HEADS)], axis=0)


def _from_heads(x):
    return jnp.concatenate([x[h] for h in range(HG_HEADS)], axis=1)


def _hdot(a, b, ca, cb):
    return lax.dot_general(a.astype(BF), b.astype(BF), (((ca,), (cb,)), ((0,), (0,))), preferred_element_type=F32)


def _hg_chunk_all(qh, fh, ih, gh, lbraw, nw, st):
    C = HG_CHUNK
    mx = jnp.maximum(lbraw[0:1], lbraw[1:2])
    e0 = jnp.exp(lbraw[0:1] - mx)
    e1 = jnp.exp(lbraw[1:2] - mx)
    lb = e0 / (e0 + e1)
    f = lb + (1.0 - lb) * jax.nn.sigmoid(fh)
    logf = jnp.log(f)
    k = 1.0 - f
    q = qh * jax.nn.sigmoid(qh)
    causal = lax.broadcasted_iota(jnp.int32, (C, C), 0) >= lax.broadcasted_iota(jnp.int32, (C, C), 1)
    b = lax.dot_general(causal.astype(F32), logf, (((1,), (0,)), ((), ())), precision=lax.Precision.HIGHEST,
                        preferred_element_type=F32)
    b_last = b[C - 1:C]
    b_ref = b[C // 2:C // 2 + 1]
    v3 = _to_heads(ih)
    st_out = st * _to_heads(jnp.exp(b_last)) + _hdot(v3, _to_heads(k * jnp.exp(b_last - b)), 1, 1)
    inter = _hdot(_to_heads(q * jnp.exp(b)), st, 2, 2)
    a = _hdot(_to_heads(q * jnp.exp(b - b_ref)), _to_heads(k * jnp.exp(b_ref - b)), 2, 2)
    a = jnp.where(causal[None], a, 0.0)
    o = inter + _hdot(a, v3, 2, 1)
    o = o * lax.rsqrt(jnp.mean(o * o, axis=-1, keepdims=True) + RMS_EPS) * nw
    return _from_heads(o) * (gh * jax.nn.sigmoid(gh)), st_out


HG_STEP = 8
HG_ROWS = HG_STEP * HG_CHUNK


def _hg_in_specs(step_of):
    half = 4 * LANE
    specs = []
    for base in (QH0, FH0, IH0, GH0):
        for part in range(2):
            specs.append(pl.BlockSpec((HG_ROWS, half), lambda n, col=base // 4 + part: (step_of(n), col)))
    return specs


def _hg_sections(sec, c):
    rows = slice(c * HG_CHUNK, (c + 1) * HG_CHUNK)
    return [jnp.concatenate([sec[2 * s][rows, :], sec[2 * s + 1][rows, :]], axis=1).astype(F32) for s in range(4)]


def _hg_fwd(proj, lb, nw):
    S = proj.shape[0]
    C = HG_CHUNK
    NC = S // C

    def body(*refs):
        sec = refs[:8]
        lb_ref, nw_ref, o_ref, st_ref, st_scr = refs[8:]

        @pl.when(pl.program_id(0) == 0)
        def _():
            st_scr[...] = jnp.zeros_like(st_scr)

        st = st_scr[...]
        for c in range(HG_STEP):
            st_ref[:, c] = st
            o, st = _hg_chunk_all(*_hg_sections(sec, c), lb_ref[...], nw_ref[...], st)
            o_ref[c * C:(c + 1) * C, :] = o.astype(o_ref.dtype)
        st_scr[...] = st

    return _pcall(
        body, grid=(NC // HG_STEP,),
        in_specs=_hg_in_specs(lambda n: n) + [pl.BlockSpec((2, HG_WIDTH), lambda n: (0, 0)), pl.BlockSpec((1, LANE), lambda n: (0, 0))],
        out_specs=[pl.BlockSpec((HG_ROWS, HG_WIDTH), lambda n: (n, 0)),
                   pl.BlockSpec((HG_HEADS, HG_STEP, LANE, LANE), lambda n: (0, n, 0, 0))],
        out_shape=[SDS((S, HG_WIDTH), BF), SDS((HG_HEADS, NC, LANE, LANE), F32)],
        scratch_shapes=[pltpu.VMEM((HG_HEADS, LANE, LANE), F32)], name="hg_fwd")(*([proj] * 8), lb, nw)


def _hg_bwd(proj, lb, nw, states, do_hg, dproj):
    S = proj.shape[0]
    C = HG_CHUNK
    NC = S // C

    def body(*refs):
        sec = refs[:8]
        lb_ref, nw_ref, st_ref, do_ref, dp_in, dp_ref, dlb_ref, dnw_ref, dst_scr, stage, sem = refs[8:]
        n = pl.program_id(0)
        slot = n % 2

        def step_copies(s):
            rows = pl.ds(pl.multiple_of((NS - 1 - n) * HG_ROWS, HG_ROWS), HG_ROWS)
            return [pltpu.make_async_copy(stage.at[s, t], dp_ref.at[rows, pl.ds((QH0 + 8 * t) * LANE, HG_WIDTH)], sem.at[s, t])
                    for t in range(4)]

        @pl.when(n == 0)
        def _():
            dst_scr[...] = jnp.zeros_like(dst_scr)
            dlb_ref[...] = jnp.zeros_like(dlb_ref)
            dnw_ref[...] = jnp.zeros_like(dnw_ref)

        @pl.when(n >= 2)
        def _():
            for cp in step_copies(slot):
                cp.wait()

        dst = dst_scr[...]
        for c in reversed(range(HG_STEP)):
            rows = slice(c * C, (c + 1) * C)
            _, vjp = jax.vjp(_hg_chunk_all, *_hg_sections(sec, c), lb_ref[...], nw_ref[...], st_ref[:, c])
            dq, df, di, dg, dlb, dnw, dst = vjp((do_ref[rows, :].astype(F32), dst))
            dlb_ref[...] += dlb
            dnw_ref[...] += dnw
            for t, val in enumerate((dq, df, di, dg)):
                stage[slot, t, rows, :] = val.astype(BF)
        dst_scr[...] = dst
        for cp in step_copies(slot):
            cp.start()

        @pl.when(n == NS - 1)
        def _():
            for cp in step_copies(1 - slot) + step_copies(slot):
                cp.wait()

    NS = NC // HG_STEP
    rev = lambda n: NS - 1 - n
    ospec = pl.BlockSpec((HG_ROWS, HG_WIDTH), lambda n: (rev(n), 0))
    lbs = pl.BlockSpec((2, HG_WIDTH), lambda n: (0, 0))
    nws = pl.BlockSpec((1, LANE), lambda n: (0, 0))
    outs = _pcall(
        body, grid=(NS,),
        in_specs=_hg_in_specs(rev) + [lbs, nws, pl.BlockSpec((HG_HEADS, HG_STEP, LANE, LANE), lambda n: (0, rev(n), 0, 0)), ospec, _ANY],
        out_specs=[_ANY, lbs, nws],
        out_shape=[SDS((S, IN_WIDTH), BF), SDS((2, HG_WIDTH), F32), SDS((1, LANE), F32)], aliases={12: 0},
        scratch_shapes=[pltpu.VMEM((HG_HEADS, LANE, LANE), F32), pltpu.VMEM((2, 4, HG_ROWS, HG_WIDTH), BF),
                        pltpu.SemaphoreType.DMA((2, 4))],
        name="hg_bwd")(*([proj] * 8), lb, nw, states, do_hg, dproj)
    return outs


def _merge(ga, gb, ba, bb):
    return jax.nn.sigmoid(ga) * ba + jax.nn.sigmoid(gb) * bb


_MT = 512


def _swiglu(a, b):
    return a * jax.nn.sigmoid(a) * b


def _cross(q, k, v):
    s = _bdot(q, k, ((1,), (1,))) * (HEAD_DIM ** -0.5)
    e = jnp.exp(s - jnp.max(s, axis=-1, keepdims=True))
    p = e / jnp.sum(e, axis=-1, keepdims=True)
    return _bdot(p, v, ((1,), (0,)))


def _cross_specs():
    q = pl.BlockSpec((_MT, LANE), lambda h, i: (i, h))
    k = pl.BlockSpec((MEM_LEN, LANE), lambda h, i: (0, h))
    v = pl.BlockSpec((MEM_LEN, LANE), lambda h, i: (0, CROSS_HEADS + h))
    return q, k, v


def _cross_fwd(qc, kvc):
    S = qc.shape[0]
    q, k, v = _cross_specs()

    def body(q_ref, k_ref, v_ref, o_ref):
        o_ref[...] = _cross(q_ref[...], k_ref[...], v_ref[...]).astype(o_ref.dtype)

    return _pcall(body, grid=(CROSS_HEADS, S // _MT), in_specs=[q, k, v], out_specs=q,
                          out_shape=SDS((S, CROSS_WIDTH), BF), name="cross_fwd")(qc, kvc, kvc)


def _cross_bwd(qc, kvc, doc):
    S = qc.shape[0]
    q, k, v = _cross_specs()

    def body(q_ref, k_ref, v_ref, do_ref, dq_ref, dk_ref, dv_ref):
        @pl.when(pl.program_id(1) == 0)
        def _():
            dk_ref[...] = jnp.zeros_like(dk_ref)
            dv_ref[...] = jnp.zeros_like(dv_ref)

        _, vjp = jax.vjp(_cross, q_ref[...], k_ref[...], v_ref[...])
        dq, dk, dv = vjp(do_ref[...])
        dq_ref[...] = dq.astype(dq_ref.dtype)
        dk_ref[...] += dk
        dv_ref[...] += dv

    return _pcall(body, grid=(CROSS_HEADS, S // _MT), in_specs=[q, k, v, q], out_specs=[q, k, k],
                          out_shape=[SDS((S, CROSS_WIDTH), BF), SDS((MEM_LEN, CROSS_WIDTH), F32), SDS((MEM_LEN, CROSS_WIDTH), F32)],
                          name="cross_bwd")(qc, kvc, kvc, doc)


def _local_step(x, mem, tgt, p, ready):
    h, h_a, h_b = _rms_fwd(x, p["ln_mix"], "rms_mix", split=IN_SPLIT)
    proj = _mm(h_a, p["w_in_a"], "nt", BF, "mm_proj_a")
    proj = _mm(h_b, p["w_in_b"], "nt", BF, "mm_proj_b", res=proj)
    o_att, lse = _attn_fwd(proj)
    o_hg, states = _hg_fwd(proj, p["hg_lb"], p["hg_norm"])
    ba = _mm(o_att, p["w_ba"], "nn", BF, "mm_ba")
    gates = [(proj, GA0 * LANE // _MT), (proj, GB0 * LANE // _MT)]

    def f32s(*vals):
        return [v.astype(F32) for v in vals]

    bb, merged = _mm_fused(o_hg, p["w_bb"], "nn", "mm_bb_merge", gates + [ba], [BF, BF], tn=_MT,
                           epi=lambda d, ga, gb, bav: (d, _merge(*f32s(ga, gb, bav), d)))
    x1 = _mm(merged, p["w_out"], "nn", F32, "mm_out", res=x)
    hc = _rms_fwd(x1, p["ln_cross"], "rms_cross")
    qc = _mm(hc, p["wq"], "nn", F32, "mm_q")
    mn = _rms_fwd(mem, p["ln_mem"], "rms_mem")
    kvc = _mm(mn, p["wkv"], "nn", F32, "mm_kv")
    oc = _cross_fwd(qc, kvc)
    x2 = _mm(oc, p["wo"], "nn", F32, "mm_o", res=x1)
    hf = _rms_fwd(x2, p["ln_ffn"], "rms_ffn")
    a = _mm(hf, p["w1"], "nt", BF, "mm_w1")
    b, g = _mm_fused(hf, p["w3"], "nt", "mm_w3_swiglu", [a], [BF, BF], lambda d, av: (d, _swiglu(av.astype(F32), d)))
    x3 = _mm(g, p["w2"], "nn", F32, "mm_w2", res=x2)
    loss, dx3, dx3b, d_ln_final = _final_loss(x3, p["ln_final"], tgt)

    gw, gs = {}, {"ln_final": d_ln_final}
    da, db = _mm_fused(dx3b, p["w2"], "nt", "mm_dg_swiglu", [a, b], [BF, BF],
                       lambda d, av, bv: jax.vjp(_swiglu, av.astype(F32), bv.astype(F32))[1](d))
    gw["w2"] = _mm(g, dx3b, "tn", BF, "mm_dw2")
    dhf = _mm(da, p["w1"], "nn", F32, "mm_dhf1")
    dhf = _mm(db, p["w3"], "nn", BF, "mm_dhf3", res=dhf)
    gw["w1"] = _mm(da, hf, "tn", BF, "mm_dw1")
    gw["w3"] = _mm(db, hf, "tn", BF, "mm_dw3")
    ready("ffn", gw)
    dx2, dx2b, gs["ln_ffn"] = _rms_bwd(x2, p["ln_ffn"], dhf, dx3, "rms_ffn_bwd")
    doc = _mm(dx2b, p["wo"], "nt", F32, "mm_doc")
    gw["wo"] = _mm(oc, dx2b, "tn", BF, "mm_dwo")
    dqc, dk, dv = _cross_bwd(qc, kvc, doc)
    dkvc = jnp.concatenate([dk, dv], axis=1)
    dhc = _mm(dqc, p["wq"], "nt", BF, "mm_dhc")
    gw["wq"] = _mm(hc, dqc, "tn", BF, "mm_dwq")
    dmn = _mm(dkvc, p["wkv"], "nt", F32, "mm_dmn")
    gw["wkv"] = _mm(mn, dkvc, "tn", BF, "mm_dwkv")
    ready("cross", gw)
    gs["ln_mem"] = _rms_bwd(mem, p["ln_mem"], dmn, None, "rms_mem_bwd")
    dx1, dx1b, gs["ln_cross"] = _rms_bwd(x1, p["ln_cross"], dhc, dx2, "rms_cross_bwd")
    def merge_back(d, ga, gb, bav, bbv):
        dga, dgb, dba_, dbb_ = jax.vjp(_merge, *f32s(ga, gb, bav, bbv))[1](d)
        return dba_, dbb_, dga, dgb

    dba, dbb, dproj = _mm_fused(dx1b, p["w_out"], "nt", "mm_dmerged_merge", gates + [ba, bb], [BF, BF], merge_back, tn=_MT,
                                scatter=(SDS((x.shape[0], IN_WIDTH), BF), [GA0 * LANE, GB0 * LANE]))
    gw["w_out"] = _mm(merged, dx1b, "tn", BF, "mm_dwout")
    do_att = _mm(dba, p["w_ba"], "nt", F32, "mm_doatt")
    gw["w_ba"] = _mm(o_att, dba, "tn", BF, "mm_dwba")
    do_hg = _mm(dbb, p["w_bb"], "nt", F32, "mm_dohg")
    gw["w_bb"] = _mm(o_hg, dbb, "tn", BF, "mm_dwbb")
    ready("mix", gw)
    dproj = _attn_bwd(proj, o_att, lse, do_att, dproj)
    dproj, gs["hg_lb"], gs["hg_norm"] = _hg_bwd(proj, p["hg_lb"], p["hg_norm"], states, do_hg, dproj)
    for q in range(IN_PARTS):
        gw[f"w_in_q{q}"] = _mm(dproj, h, "tn", BF, f"mm_dwin_q{q}", b_part=(q, IN_PARTS))
        ready(f"in_q{q}", gw)
        if q == 0:
            ready("mixers_done", gw)
    dh_a = _mm(dproj, p["w_in_a"], "nn", BF, "mm_dh_a")
    ready("dh_half", gw)
    dh = (dh_a, _mm(dproj, p["w_in_b"], "nn", BF, "mm_dh_b"))
    grad_x, _, gs["ln_mix"] = _rms_bwd(x, p["ln_mix"], dh, dx1, "rms_mix_bwd")
    return loss, grad_x, gs


IN_SPLIT = 1280
_MATS = (
    ("w_in_a", "w_in", IN_WIDTH, IN_SPLIT, "row", IN_WIDTH // N_DEV, True),
    ("w_in_b", "w_in", IN_WIDTH, D_MODEL - IN_SPLIT, "row", IN_WIDTH // N_DEV, True),
    ("w_ba", "w_branch_a", ATT_OUT, D_MODEL, "col", D_MODEL // N_DEV, False),
    ("w_bb", "w_branch_b", HG_WIDTH, D_MODEL, "col", D_MODEL // N_DEV, False),
    ("w_out", "w_out", D_MODEL, D_MODEL, "row", D_MODEL // N_DEV, False),
    ("wq", "wq_cross", D_MODEL, CROSS_WIDTH, "row", D_MODEL // N_DEV, False),
    ("wkv", "wkv_cross", D_MODEL, 2 * CROSS_WIDTH, "row", D_MODEL // N_DEV, False),
    ("wo", "wo_cross", CROSS_WIDTH, D_MODEL, "col", D_MODEL // N_DEV, False),
    ("w1", "w1", D_FF, D_MODEL, "row", D_FF // N_DEV, True),
    ("w3", "w3", D_FF, D_MODEL, "row", D_FF // N_DEV, True),
    ("w2", "w2", D_FF, D_MODEL, "row", D_FF // N_DEV, False),
    ("w_in_q0", "w_in", IN_WIDTH, D_MODEL // 2, "row", IN_WIDTH // N_DEV, True),
    ("w_in_q1", "w_in", IN_WIDTH, D_MODEL // 2, "row", IN_WIDTH // N_DEV, True),
)
IN_PARTS = 2
_INDEX = {m[0]: i for i, m in enumerate(_MATS)}


def _unit_shape(i):
    _, _, K, N, kind, sh, _ = _MATS[i]
    return (sh, N) if kind == "row" else (K, sh)


def _win(ref, i, chip, par, half=None):
    _, _, K, _, kind, sh, _ = _MATS[i]
    if kind == "row":
        rows, start = (sh, 0) if half is None else (sh // 2, half * (sh // 2))
        return ref.at[pl.ds(pl.multiple_of((2 * chip + par) * sh + start, 16), rows), :]
    rows, start = (K, 0) if half is None else (K // 2, half * (K // 2))
    return ref.at[pl.ds(start, rows), pl.ds(pl.multiple_of((2 * chip + par) * sh, LANE), sh)]


def _my_place():
    x, y, c = lax.axis_index("x"), lax.axis_index("y"), lax.axis_index("c")
    chips = [(1 - x, y), (x, 1 - y), (1 - x, 1 - y)]
    return x, y, c, chips


def _cast_unit(w, name, cols=None):
    K, N = w.shape
    c0, c1 = cols or (0, N)
    tr = _pick(K, (256, 400, 352))

    def body(w_ref, o_ref):
        o_ref[...] = w_ref[:, c0:c1].astype(BF)

    src = pl.BlockSpec((tr, N), lambda i: (i, 0))
    dst = pl.BlockSpec((tr, c1 - c0), lambda i: (i, 0))
    return _pcall(body, grid=(K // tr,), in_specs=[src], out_specs=dst, out_shape=SDS((K, c1 - c0), BF), name=name)(w)


def _handshake(peers):
    barrier = pltpu.get_barrier_semaphore()
    for peer in peers:
        pl.semaphore_signal(barrier, inc=1, device_id=peer, device_id_type=MESH)
    pl.semaphore_wait(barrier, len(peers))


def _sequencer(body, out_type, sems, cid, name):
    return pl.kernel(body, out_type=out_type, mesh=plsc.ScalarSubcoreMesh(axis_name="seq", num_cores=1),
                     scratch_types=sems, compiler_params=pltpu.CompilerParams(collective_id=cid), name=name)


def _all_gather(units, idx, cid, name):
    n = len(idx)

    def body(*refs):
        us, fulls = refs[:n], refs[n:2 * n]
        ssem, rsem, lsem = refs[2 * n:]
        x, y, c, _ = _my_place()
        xn, yn, dg = (1 - x, y), (x, 1 - y), (1 - x, 1 - y)
        me, sib = (x, y, c), (x, y, 1 - c)
        _handshake([sib, (*xn, c), (*yn, c)])

        def win(i, chip_xy, par, half=None):
            return _win(fulls[i], idx[i], 2 * chip_xy[0] + chip_xy[1], par, half)

        def rcopy(i, k, src, dst, to):
            return pltpu.make_async_remote_copy(src_ref=src, dst_ref=dst, send_sem=ssem.at[i, k], recv_sem=rsem.at[i, k],
                                                device_id=to, device_id_type=MESH)

        def landed(i, k, blk):
            rcopy(i, k, blk, blk, me).wait_recv()

        sends, locs = [], []

        def send(i, k, blk, to, src=None):
            cp = rcopy(i, k, blk if src is None else src, blk, to)
            cp.start()
            sends.append(cp)

        for i in range(n):
            mine = win(i, (x, y), c)
            loc = pltpu.make_async_copy(us[i], mine, lsem.at[i])
            loc.start()
            locs.append(loc)
            send(i, 0, mine, sib, src=us[i])
            send(i, 1, mine, (*xn, c), src=us[i])
            send(i, 2, mine, (*yn, c), src=us[i])
        for i in range(n):
            landed(i, 1, win(i, xn, c))
            send(i, 3, win(i, xn, c, 0), (*yn, c))
            send(i, 5, win(i, xn, c), sib)
            landed(i, 2, win(i, yn, c))
            send(i, 4, win(i, yn, c, 1), (*xn, c))
            send(i, 6, win(i, yn, c), sib)
        for i in range(n):
            landed(i, 3, win(i, dg, c, 0))
            landed(i, 4, win(i, dg, c, 1))
            send(i, 7, win(i, dg, c), sib)
        for i in range(n):
            landed(i, 0, win(i, (x, y), 1 - c))
            landed(i, 5, win(i, xn, 1 - c))
            landed(i, 6, win(i, yn, 1 - c))
            landed(i, 7, win(i, dg, 1 - c))
        for cp in sends:
            cp.wait_send()
        for loc in locs:
            loc.wait()

    out_type = [SDS((_MATS[i][2], _MATS[i][3]), BF) for i in idx]
    sems = [pltpu.SemaphoreType.DMA((n, 8)), pltpu.SemaphoreType.DMA((n, 8)), pltpu.SemaphoreType.DMA((n,))]
    return _sequencer(body, out_type, sems, cid, name)(*units)


def _rs_sibling(grads, idx, cid, name):
    n = len(idx)

    def body(*refs):
        gs, bufs = refs[:n], refs[n:2 * n]
        ssem, rsem = refs[2 * n:]
        x, y, c, _ = _my_place()
        _handshake([(x, y, 1 - c)])
        cps = []
        for i in range(n):
            for k in range(4):
                cp = pltpu.make_async_remote_copy(
                    src_ref=_win(gs[i], idx[i], k, 1 - c), dst_ref=bufs[i].at[k], send_sem=ssem.at[i, k], recv_sem=rsem.at[i, k],
                    device_id=(x, y, 1 - c), device_id_type=MESH)
                cp.start()
                cps.append(cp)
        for cp in cps:
            cp.wait()

    out_type = [SDS((4, *_unit_shape(i)), BF) for i in idx]
    sems = [pltpu.SemaphoreType.DMA((n, 4)), pltpu.SemaphoreType.DMA((n, 4))]
    return _sequencer(body, out_type, sems, cid, name)(*grads)


def _sibling_sum_tc(grad, i, cid, name):
    unit = _unit_shape(i)

    def body(g_ref, t_ref, theirs, mine, summed, ssem, rsem, lsem, osem):
        x, y, c, _ = _my_place()
        _handshake([(x, y, 1 - c)])
        sends, loads = [], []
        for k in range(4):
            sends.append(pltpu.make_async_remote_copy(
                src_ref=_win(g_ref, i, k, 1 - c), dst_ref=theirs.at[k], send_sem=ssem.at[k], recv_sem=rsem.at[k],
                device_id=(x, y, 1 - c), device_id_type=MESH))
            loads.append(pltpu.make_async_copy(_win(g_ref, i, k, c), mine.at[k], lsem.at[k]))
            sends[-1].start()
            loads[-1].start()
        stores = []
        for k in range(4):
            loads[k].wait()
            sends[k].wait_recv()
            summed[k] = (mine[k].astype(F32) + theirs[k].astype(F32)).astype(BF)
            stores.append(pltpu.make_async_copy(summed.at[k], t_ref.at[k], osem.at[k]))
            stores[-1].start()
        for k in range(4):
            sends[k].wait_send()
            stores[k].wait()

    block = pltpu.VMEM((4, *unit), BF)
    return _pcall(body, in_specs=[_ANY], out_specs=_ANY, out_shape=SDS((4, *unit), BF),
                  scratch_shapes=[block, block, block] + [pltpu.SemaphoreType.DMA((4,))] * 4,
                  compiler_params=_cp(VMEM_BIG, collective_id=cid), name=name)(grad)


def _chip_sum(g, sib, i, place, name):
    _, _, K, N, kind, sh, _ = _MATS[i]

    def body(pref, g_ref, s_ref, o_ref):
        o_ref[...] = (g_ref[...].astype(F32) + s_ref[...].astype(F32)).astype(BF)

    if kind == "row":
        tr = _pick(sh, (800, 352, 256))
        per = sh // tr
        grid = (4, per)
        g_spec = pl.BlockSpec((tr, N), lambda k, r, pref: ((2 * k + pref[0]) * per + r, 0))
        u_spec = pl.BlockSpec((None, tr, N), lambda k, r, pref: (k, r, 0))
    else:
        grid = (4,)
        g_spec = pl.BlockSpec((K, sh), lambda k, pref: (0, 2 * k + pref[0]))
        u_spec = pl.BlockSpec((None, K, sh), lambda k, pref: (k, 0, 0))
    return _pcall(body, grid=grid, in_specs=[g_spec, u_spec], out_specs=u_spec, out_shape=SDS((4, *_unit_shape(i)), BF),
                  prefetch=True, name=name)(place, g, sib)


def _rs_chips(parts, idx, cid, name):
    n = len(idx)

    def body(*refs):
        ts, bufs, relays = refs[:n], refs[n:2 * n], refs[2 * n:3 * n]
        ssem, rsem = refs[3 * n:]
        x, y, c, _ = _my_place()
        xn, yn, dg = (1 - x, y), (x, 1 - y), (1 - x, 1 - y)
        to_x, to_y, me = (*xn, c), (*yn, c), (x, y, c)
        _handshake([to_x, to_y])

        def rcopy(i, k, src, dst, to):
            return pltpu.make_async_remote_copy(src_ref=src, dst_ref=dst, send_sem=ssem.at[i, k], recv_sem=rsem.at[i, k],
                                                device_id=to, device_id_type=MESH)

        def halves(i):
            rows = _unit_shape(idx[i])[0] // 2
            return [pl.ds(0, rows), pl.ds(rows, rows)]

        sends = []

        def send(cp):
            cp.start()
            sends.append(cp)

        for i in range(n):
            h0, h1 = halves(i)
            diag = ts[i].at[2 * dg[0] + dg[1]]
            send(rcopy(i, 0, ts[i].at[2 * xn[0] + xn[1]], bufs[i].at[0], to_x))
            send(rcopy(i, 1, ts[i].at[2 * yn[0] + yn[1]], bufs[i].at[1], to_y))
            send(rcopy(i, 2, diag.at[h0], relays[i].at[0], to_x))
            send(rcopy(i, 3, diag.at[h1], relays[i].at[1], to_y))
        for i in range(n):
            h0, h1 = halves(i)
            rcopy(i, 2, relays[i].at[0], relays[i].at[0], me).wait_recv()
            send(rcopy(i, 4, relays[i].at[0], bufs[i].at[2, h0], to_y))
            rcopy(i, 3, relays[i].at[1], relays[i].at[1], me).wait_recv()
            send(rcopy(i, 5, relays[i].at[1], bufs[i].at[2, h1], to_x))
        for i in range(n):
            h0, h1 = halves(i)
            rcopy(i, 0, bufs[i].at[0], bufs[i].at[0], me).wait_recv()
            rcopy(i, 1, bufs[i].at[1], bufs[i].at[1], me).wait_recv()
            rcopy(i, 4, bufs[i].at[2, h0], bufs[i].at[2, h0], me).wait_recv()
            rcopy(i, 5, bufs[i].at[2, h1], bufs[i].at[2, h1], me).wait_recv()
        for cp in sends:
            cp.wait_send()

    def relay_shape(i):
        rows, cols = _unit_shape(i)
        return (2, rows // 2, cols)

    out_type = [SDS((3, *_unit_shape(i)), BF) for i in idx] + [SDS(relay_shape(i), BF) for i in idx]
    sems = [pltpu.SemaphoreType.DMA((n, 6)), pltpu.SemaphoreType.DMA((n, 6))]
    return _sequencer(body, out_type, sems, cid, name)(*parts)[:n]


def _adamw(w, g, m, v):
    m = ADAM_B1 * m + (1.0 - ADAM_B1) * g
    v = ADAM_B2 * v + (1.0 - ADAM_B2) * jnp.square(g)
    m_hat = m / (1.0 - ADAM_B1 ** ADAM_STEP)
    v_hat = v / (1.0 - ADAM_B2 ** ADAM_STEP)
    delta = -ADAM_LR * (m_hat / (jnp.sqrt(v_hat) + ADAM_EPS) + ADAM_WD * w)
    return delta, m, v


def _adam_mat(parts, recvs, w, m, v, place, name):
    R, C = w.shape
    n = len(parts)
    tr = _pick(R, (256, 160, 176))

    def body(pref, *refs):
        t_refs, r_refs = refs[:n], refs[n:2 * n]
        w_ref, m_ref, v_ref, g_out, d_out, m_out, v_out = refs[2 * n:]
        cols = [t[...].astype(F32) + r[0].astype(F32) + r[1].astype(F32) + r[2].astype(F32) for t, r in zip(t_refs, r_refs)]
        g = jnp.concatenate(cols, axis=1) if n > 1 else cols[0]
        d, mn, vn = _adamw(w_ref[...], g, m_ref[...], v_ref[...])
        g_out[...] = g
        d_out[...] = d
        m_out[...] = mn
        v_out[...] = vn

    t_spec = pl.BlockSpec((None, tr, C // n), lambda i, pref: (pref[1], i, 0))
    r_spec = pl.BlockSpec((3, tr, C // n), lambda i, pref: (0, i, 0))
    s_spec = pl.BlockSpec((tr, C), lambda i, pref: (i, 0))
    return _pcall(body, grid=(R // tr,), in_specs=[t_spec] * n + [r_spec] * n + [s_spec] * 3, out_specs=[s_spec] * 4,
                  out_shape=[SDS((R, C), F32)] * 4, prefetch=True, name=name)(place, *parts, *recvs, w, m, v)


_SMALL = (("ln_mix", "ln_mix_w", D_MODEL), ("ln_cross", "ln_cross_w", D_MODEL), ("ln_mem", "ln_mem_w", D_MODEL),
          ("ln_ffn", "ln_ffn_w", D_MODEL), ("ln_final", "ln_final_w", D_MODEL), ("hg_lb", "hg_lower_bounds", 2 * HG_WIDTH),
          ("hg_norm", "hg_norm_w", HEAD_DIM))
_PACK = sum(n for _, _, n in _SMALL) + LANE


def _small_sync(grow, wrow, mrow, vrow):
    def body(g_ref, w_ref, m_ref, v_ref, o_ref, gath, ssem, rsem):
        x, y, c, _ = _my_place()
        me = 4 * x + 2 * y + c
        gath[pl.ds(me, 1), :] = g_ref[...]
        cps = []
        for dlt in range(1, N_DEV):
            peer = (me + dlt) % N_DEV
            cp = pltpu.make_async_remote_copy(
                src_ref=g_ref, dst_ref=gath.at[pl.ds(me, 1), :], send_sem=ssem.at[peer], recv_sem=rsem.at[me],
                device_id=(peer // 4, (peer // 2) % 2, peer % 2), device_id_type=MESH)
            cp.start()
            cps.append(cp)
        for dlt in range(1, N_DEV):
            src = (me + dlt) % N_DEV
            pltpu.make_async_remote_copy(
                src_ref=g_ref, dst_ref=gath.at[pl.ds(src, 1), :], send_sem=ssem.at[src], recv_sem=rsem.at[src],
                device_id=(x, y, c), device_id_type=MESH).wait_recv()
        for cp in cps:
            cp.wait_send()
        g = gath[0:1, :]
        for j in range(1, N_DEV):
            g = g + gath[j:j + 1, :]
        d, mn, vn = _adamw(w_ref[...], g, m_ref[...], v_ref[...])
        o_ref[0:1, :] = g
        o_ref[1:2, :] = d
        o_ref[2:3, :] = mn
        o_ref[3:4, :] = vn

    vm = pl.BlockSpec(memory_space=pltpu.VMEM)
    return _pcall(
        body, in_specs=[vm] * 4, out_specs=vm, out_shape=SDS((4, _PACK), F32),
        scratch_shapes=[pltpu.VMEM((N_DEV, _PACK), F32), pltpu.SemaphoreType.DMA((N_DEV,)), pltpu.SemaphoreType.DMA((N_DEV,))],
        name="small_sync")(grow, wrow, mrow, vrow)


def _pack_small(d, loss_row=None):
    parts = [d[k].reshape(1, n).astype(F32) for k, _, n in _SMALL]
    parts.append(loss_row if loss_row is not None else jnp.ones((1, LANE), F32))
    return jnp.concatenate(parts, axis=1)


def kernel(x, mem, ln_mix_w, w_in, hg_norm_w, hg_lower_bounds, w_branch_a, w_branch_b, w_out, ln_cross_w, ln_mem_w, wq_cross, wkv_cross, wo_cross, ln_ffn_w, w1, w3, w2, ln_final_w, loss_target, m_ln_mix_w, m_w_in, m_hg_norm_w, m_hg_lower_bounds, m_w_branch_a, m_w_branch_b, m_w_out, m_ln_cross_w, m_ln_mem_w, m_wq_cross, m_wkv_cross, m_wo_cross, m_ln_ffn_w, m_w1, m_w3, m_w2, m_ln_final_w, v_ln_mix_w, v_w_in, v_hg_norm_w, v_hg_lower_bounds, v_w_branch_a, v_w_branch_b, v_w_out, v_ln_cross_w, v_ln_mem_w, v_wq_cross, v_wkv_cross, v_wo_cross, v_ln_ffn_w, v_w1, v_w3, v_w2, v_ln_final_w):
    given = dict(locals())
    place = jnp.stack([lax.axis_index("c"), 2 * lax.axis_index("x") + lax.axis_index("y")]).astype(jnp.int32)

    _ORDER[0] = None
    index = _INDEX

    def local_view(a, i):
        return a[0].T if _MATS[i][6] else a[0]

    shard = {key: local_view(given[ref], i) for i, (key, ref, *_) in enumerate(_MATS)}

    p = {}
    col_part = {"w_in_a": (0, IN_SPLIT), "w_in_b": (IN_SPLIT, D_MODEL)}
    waves = (("in_a", ["w_in_a"], 1), ("in_b", ["w_in_b"], 12), ("mid", ["w_ba", "w_bb", "w_out", "wq", "wkv", "wo"], 2),
             ("w1", ["w1"], 11), ("w3", ["w3"], 15), ("w2", ["w2"], 16))
    for wave, keys, cid in waves:
        units = [_cast_unit(shard[k], "cast_" + k, col_part.get(k)) for k in keys]
        p.update(zip(keys, _all_gather(units, [index[k] for k in keys], cid, "all_gather_" + wave)))
    small_w = {"ln_mix": ln_mix_w, "ln_cross": ln_cross_w, "ln_mem": ln_mem_w, "ln_ffn": ln_ffn_w,
               "ln_final": ln_final_w.reshape(1, D_MODEL), "hg_lb": hg_lower_bounds, "hg_norm": hg_norm_w}
    p.update(small_w)

    groups = {"ffn": ["w2", "w1", "w3"], "cross": ["wo", "wq", "wkv"], "mix": ["w_out", "w_ba", "w_bb"]}
    cids = {"ffn": (3, 4), "cross": (5, 6), "mix": (7, 8)}
    for q in range(IN_PARTS):
        groups[f"in_q{q}"] = [f"w_in_q{q}"]
        cids[f"in_q{q}"] = ((9, 10), (13, 14), (17, 18), (19, 20))[q]
    sib, parts, recv, out = {}, {}, {}, {}

    grads = {}

    def chip_stage(name):
        keys = groups[name]
        parts[name] = [_chip_sum(grads[k], s, index[k], place, "chip_sum_" + k) for k, s in zip(keys, sib[name])]
        recv[name] = list(_rs_chips(parts[name], [index[k] for k in keys], cids[name][1], "rs_chips_" + name))

    def adam(ref, i, part_list, recv_list):
        res = _adam_mat(part_list, recv_list, shard[_MATS[i][0]], local_view(given["m_" + ref], i), local_view(given["v_" + ref], i),
                        place, "adam_" + ref)
        out[ref] = [(a.T if _MATS[i][6] else a)[None] for a in res]

    def adam_stage(name, only=None, skip=None):
        for n, (k, t, r) in enumerate(zip(groups[name], parts[name], recv[name])):
            if (only is None or n == only) and n != skip:
                adam(_MATS[index[k]][1], index[k], [t], [r])

    def ready(name, gw):
        if name == "mixers_done":
            adam_stage("ffn", only=0)
            chip_stage("mix")
            adam_stage("ffn", skip=0)
            adam_stage("cross")
            return
        if name == "dh_half":
            adam_stage("mix")
            return
        keys = groups[name]
        grads.update({k: gw[k] for k in keys})
        if name.startswith("in_q"):
            parts[name] = [_sibling_sum_tc(gw[k], index[k], cids[name][0], "sibling_sum_" + k) for k in keys]
            recv[name] = list(_rs_chips(parts[name], [index[k] for k in keys], cids[name][1], "rs_chips_" + name))
            if name == "in_q1":
                _ALSO_AFTER.extend(recv["in_q0"])
            return
        sib[name] = _rs_sibling([gw[k] for k in keys], [index[k] for k in keys], cids[name][0], "rs_sibling_" + name)
        if name == "cross":
            chip_stage("ffn")
        if name == "mix":
            chip_stage("cross")

    loss_row, grad_x, gs = _local_step(x[0], mem[0], loss_target[0], p, ready)

    sm = {k: given["m_" + ref] for k, ref, _ in _SMALL}
    sv = {k: given["v_" + ref] for k, ref, _ in _SMALL}
    synced = _small_sync(_pack_small(gs, loss_row), _pack_small(small_w), _pack_small(sm), _pack_small(sv))
    in_groups = [f"in_q{q}" for q in range(IN_PARTS)]
    adam("w_in", index["w_in_a"], sum((parts[g] for g in in_groups), []), sum((recv[g] for g in in_groups), []))
    off = 0
    for k, ref, n in _SMALL:
        shape = given[ref].shape
        out[ref] = [synced[r, off:off + n].reshape(shape) for r in range(4)]
        off += n
    loss = synced[0, off]

    order = ["ln_mix_w", "w_in", "hg_norm_w", "hg_lower_bounds", "w_branch_a", "w_branch_b", "w_out", "ln_cross_w", "ln_mem_w",
             "wq_cross", "wkv_cross", "wo_cross", "ln_ffn_w", "w1", "w3", "w2", "ln_final_w"]
    return (loss, grad_x[None], *[out[n][0] for n in order], *[out[n][1] for n in order],
            *[out[n][2] for n in order], *[out[n][3] for n in order])
```

```python
import jax
import jax.numpy as jnp
from jax import lax
from jax.experimental import pallas as pl
from jax.experimental.pallas import tpu as pltpu
from jax.experimental.pallas import tpu_sc as plsc

F32 = jnp.float32
BF = jnp.bfloat16
SDS = jax.ShapeDtypeStruct

D_MODEL = 2048
SEQ = 2048
HEAD_DIM = 128
MEM_LEN = 256
ATT_GROUPS = ((128, 1), (512, 4), (2048, 16))
ATT_HEADS = 4
ATT_WIDTH = 1536
ATT_OUT = 512
HG_HEADS = 8
HG_WIDTH = 1024
HG_CHUNK = 64
IN_WIDTH = 12800
CROSS_HEADS = 4
CROSS_WIDTH = 512
D_FF = 5632
RMS_EPS = 1e-6
ADAM_LR = 0.001
ADAM_B1 = 0.9
ADAM_B2 = 0.999
ADAM_EPS = 1e-08
ADAM_WD = 0.01
ADAM_STEP = 10
N_DEV = 8

LANE = 128
QA0, KA0, VA0 = 0, 12, 24
QH0, FH0, IH0, GH0 = 36, 44, 52, 60
GA0, GB0 = 68, 84

VMEM_BIG = 56 * 1024 * 1024
MESH = pl.DeviceIdType.MESH


def _cp(vmem=None, **kw):
    if vmem is not None:
        kw["vmem_limit_bytes"] = vmem
    return pltpu.CompilerParams(**kw)


def _pick(n, cands):
    for c in cands:
        if n % c == 0:
            return c
    raise ValueError(f"no tile for {n}")


_ANY = pl.BlockSpec(memory_space=pl.ANY)
_ORDER = [None]
_ALSO_AFTER = []


def _pcall(body, *, name, in_specs, out_specs, out_shape, grid=(), scratch_shapes=(), compiler_params=None, prefetch=False,
           aliases=None):
    def run(*args):
        dep = _ORDER[0]
        specs, operands, kernel_fn = list(in_specs), list(args), body
        deps = [] if dep is None or any(dep is a for a in operands) else [dep]
        deps += _ALSO_AFTER
        del _ALSO_AFTER[:]
        if deps:
            at, nd = len(operands), len(deps)
            specs += [_ANY] * nd
            operands += deps

            def kernel_fn(*refs):
                return body(*refs[:at], *refs[at + nd:])
        if prefetch:
            call = pl.pallas_call(
                kernel_fn, out_shape=out_shape, name=name, compiler_params=compiler_params,
                grid_spec=pltpu.PrefetchScalarGridSpec(num_scalar_prefetch=1, grid=grid, in_specs=specs, out_specs=out_specs,
                                                       scratch_shapes=scratch_shapes))
        else:
            call = pl.pallas_call(kernel_fn, grid=grid, in_specs=specs, out_specs=out_specs, out_shape=out_shape,
                                  scratch_shapes=scratch_shapes, compiler_params=compiler_params, name=name,
                                  input_output_aliases=aliases or {})
        out = call(*operands)
        _ORDER[0] = out[0] if isinstance(out, (list, tuple)) else out
        return out
    return run


def _mm(a, b, form, out_dtype, name, res=None, tiles=None, a_part=(0, 1), b_part=(0, 1)):
    a_shape = (a.shape[0], a.shape[1] // a_part[1])
    b_shape = (b.shape[0], b.shape[1] // b_part[1])
    if form == "nn":
        (M, K), (K2, N) = a_shape, b_shape
    elif form == "nt":
        (M, K), (N, K2) = a_shape, b_shape
    else:
        (K, M), (K2, N) = a_shape, b_shape
    assert K == K2, (a.shape, b.shape, form)
    if tiles is None:
        tm = _pick(M, (1024, 1280, 1408, 512, 256))
        tn = _pick(N, (1280, 1024, 1408, 768, 512, 256))
        tk = K if K <= 2048 else _pick(K, (D_FF, 2560))
        if tk == D_FF:
            tm = _pick(M, (512, 256))
    else:
        tm, tn, tk = tiles
    nk = K // tk
    ah, bh = a_part[0], b_part[0]
    if form == "nn":
        a_spec = pl.BlockSpec((tm, tk), lambda i, j, k: (i, k + ah * nk))
        b_spec = pl.BlockSpec((tk, tn), lambda i, j, k: (k, j + bh * (N // tn)))
        dims = ((1,), (0,))
    elif form == "nt":
        a_spec = pl.BlockSpec((tm, tk), lambda i, j, k: (i, k + ah * nk))
        b_spec = pl.BlockSpec((tn, tk), lambda i, j, k: (j, k + bh * nk))
        dims = ((1,), (1,))
    else:
        a_spec = pl.BlockSpec((tk, tm), lambda i, j, k: (k, i + ah * (M // tm)))
        b_spec = pl.BlockSpec((tk, tn), lambda i, j, k: (k, j + bh * (N // tn)))
        dims = ((0,), (0,))
    o_spec = pl.BlockSpec((tm, tn), lambda i, j, k: (i, j))
    in_specs = [a_spec, b_spec]
    args = [a, b]
    if res is not None:
        in_specs.append(o_spec)
        args.append(res)

    def body(*refs):
        a_ref, b_ref = refs[0], refs[1]
        r_ref = refs[2] if res is not None else None
        o_ref = refs[3] if res is not None else refs[2]
        acc = refs[-1] if nk > 1 else None
        k = pl.program_id(2)
        d = lax.dot_general(a_ref[...].astype(BF), b_ref[...].astype(BF), (dims, ((), ())), preferred_element_type=F32)

        def finish(r):
            if res is not None:
                r = r + r_ref[...].astype(F32)
            o_ref[...] = r.astype(o_ref.dtype)

        if nk == 1:
            finish(d)
        else:
            @pl.when(k == 0)
            def _():
                acc[...] = d

            @pl.when((k > 0) & (k < nk - 1))
            def _():
                acc[...] += d

            @pl.when(k == nk - 1)
            def _():
                finish(acc[...] + d)

    return _pcall(
        body, grid=(M // tm, N // tn, nk), in_specs=in_specs, out_specs=o_spec,
        out_shape=SDS((M, N), out_dtype), scratch_shapes=[pltpu.VMEM((tm, tn), F32)] if nk > 1 else [],
        compiler_params=_cp(VMEM_BIG, dimension_semantics=("parallel", "parallel", "arbitrary")),
        name=name)(*args)


_MXU_COLS = 256


def _mm_fused(a, b, form, name, extras, out_dtypes, epi, tn=None, scatter=None):
    M, K = a.shape
    N = b.shape[1] if form == "nn" else b.shape[0]
    assert K <= 2048 and K == (b.shape[0] if form == "nn" else b.shape[1])
    tm = _pick(M, (1024, 512, 256))
    tn = tn or _pick(N, (1280, 1024, 1408, 512, 256))
    ne, no = len(extras), len(out_dtypes)
    bases = scatter[1] if scatter else []
    dims = ((1,), (0,)) if form == "nn" else ((1,), (1,))

    def body(*refs):
        a_ref, b_ref, ex, outs = refs[0], refs[1], refs[2:2 + ne], refs[2 + ne:2 + ne + no]
        av = a_ref[...].astype(BF)
        if scatter:
            wide_ref, stage, sem = refs[2 + ne + no:]
        for c0 in range(0, tn, _MXU_COLS):
            cols = slice(c0, min(c0 + _MXU_COLS, tn))
            bc = b_ref[:, cols] if form == "nn" else b_ref[cols, :]
            d = lax.dot_general(av, bc.astype(BF), (dims, ((), ())), preferred_element_type=F32)
            res = epi(d, *[e[:, cols] for e in ex])
            for o_ref, r in zip(outs, res[:no]):
                o_ref[:, cols] = r.astype(o_ref.dtype)
            for s, r in enumerate(res[no:]):
                stage[s, :, cols] = r.astype(BF)
        if scatter:
            rows = pl.ds(pl.multiple_of(pl.program_id(0) * tm, tm), tm)
            cps = [pltpu.make_async_copy(stage.at[s], wide_ref.at[rows, pl.ds(pl.multiple_of(base + pl.program_id(1) * tn, LANE), tn)],
                                         sem.at[s]) for s, base in enumerate(bases)]
            for cp in cps:
                cp.start()
            for cp in cps:
                cp.wait()

    a_spec = pl.BlockSpec((tm, K), lambda i, j: (i, 0))
    b_spec = pl.BlockSpec((K, tn), lambda i, j: (0, j)) if form == "nn" else pl.BlockSpec((tn, K), lambda i, j: (j, 0))
    o_spec = pl.BlockSpec((tm, tn), lambda i, j: (i, j))
    ex_specs, ex_args = [], []
    for e in extras:
        arr, off = e if isinstance(e, tuple) else (e, 0)
        ex_specs.append(pl.BlockSpec((tm, tn), lambda i, j, off=off: (i, off + j)))
        ex_args.append(arr)
    out_specs, out_shape, scratch = [o_spec] * no, [SDS((M, N), dt) for dt in out_dtypes], []
    if scatter:
        out_specs, out_shape = out_specs + [_ANY], out_shape + [scatter[0]]
        scratch = [pltpu.VMEM((len(bases), tm, tn), BF), pltpu.SemaphoreType.DMA((len(bases),))]
    return _pcall(
        body, grid=(M // tm, N // tn), in_specs=[a_spec, b_spec] + ex_specs, out_specs=out_specs, out_shape=out_shape,
        scratch_shapes=scratch, compiler_params=_cp(VMEM_BIG, dimension_semantics=("parallel", "parallel")),
        name=name)(a, b, *ex_args)


def _rms(x, w):
    return x * lax.rsqrt(jnp.mean(x * x, axis=-1, keepdims=True) + RMS_EPS) * w


def _rms_fwd(x, w, name, split=None):
    R, D = x.shape
    tr = 256
    widths = [D] if split is None else [D, split, D - split]

    def body(x_ref, w_ref, o_ref, *part_refs):
        y = _rms(x_ref[...], w_ref[...]).astype(o_ref.dtype)
        o_ref[...] = y
        if split is not None:
            part_refs[0][...] = y[:, :split]
            part_refs[1][...] = y[:, split:]

    outs = _pcall(
        body, grid=(R // tr,),
        in_specs=[pl.BlockSpec((tr, D), lambda i: (i, 0)), pl.BlockSpec((1, D), lambda i: (0, 0))],
        out_specs=[pl.BlockSpec((tr, n), lambda i: (i, 0)) for n in widths], out_shape=[SDS((R, n), BF) for n in widths],
        name=name)(x, w)
    return outs[0] if split is None else outs


def _rms_bwd(x, w, dh, dres, name, dx_dtype=BF):
    R, D = x.shape
    tr = 256
    want_dx = dres is not None
    dhs = dh if isinstance(dh, (tuple, list)) else (dh,)
    nd = len(dhs)

    def body(*refs):
        x_ref, w_ref, dh_refs = refs[0], refs[1], refs[2:2 + nd]
        if want_dx:
            dr_ref, dx_ref, dw_ref = refs[2 + nd:]
        else:
            dw_ref = refs[2 + nd]

        @pl.when(pl.program_id(0) == 0)
        def _():
            dw_ref[...] = jnp.zeros_like(dw_ref)

        dhv = jnp.concatenate([r[...].astype(F32) for r in dh_refs], axis=1) if nd > 1 else dh_refs[0][...].astype(F32)
        _, vjp = jax.vjp(_rms, x_ref[...], w_ref[...])
        dx, dw = vjp(dhv)
        dw_ref[...] += dw
        if want_dx:
            dx_ref[...] = (dx + dr_ref[...].astype(F32)).astype(dx_ref.dtype)

    row = pl.BlockSpec((tr, D), lambda i: (i, 0))
    parts = [pl.BlockSpec((tr, d.shape[1]), lambda i: (i, 0)) for d in dhs]
    vec = pl.BlockSpec((1, D), lambda i: (0, 0))
    if want_dx:
        return _pcall(body, grid=(R // tr,), in_specs=[row, vec] + parts + [row], out_specs=[row, vec],
                      out_shape=[SDS((R, D), dx_dtype), SDS((1, D), F32)], name=name)(x, w, *dhs, dres)
    return _pcall(body, grid=(R // tr,), in_specs=[row, vec] + parts, out_specs=vec,
                  out_shape=SDS((1, D), F32), name=name)(x, w, *dhs)


def _final_loss(x3, w, tgt):
    R, D = x3.shape
    tr = 256

    def loss_fn(xv, wv, tv):
        err = _rms(xv, wv) - tv
        return 0.5 * jnp.sum(jnp.mean(err * err, axis=-1))

    def body(x_ref, w_ref, t_ref, loss_ref, dx_ref, dw_ref):
        @pl.when(pl.program_id(0) == 0)
        def _():
            dw_ref[...] = jnp.zeros_like(dw_ref)
            loss_ref[...] = jnp.zeros_like(loss_ref)

        tv = t_ref[...]
        val, vjp = jax.vjp(lambda a, b: loss_fn(a, b, tv), x_ref[...], w_ref[...])
        dx, dw = vjp(jnp.ones((), F32))
        dx_ref[...] = dx.astype(BF)
        dw_ref[...] += dw
        loss_ref[...] += jnp.full(loss_ref.shape, val, F32)

    row = pl.BlockSpec((tr, D), lambda i: (i, 0))
    vec = pl.BlockSpec((1, D), lambda i: (0, 0))
    one = pl.BlockSpec((1, LANE), lambda i: (0, 0))
    return _pcall(body, grid=(R // tr,), in_specs=[row, vec, row], out_specs=[one, row, vec],
                  out_shape=[SDS((1, LANE), F32), SDS((R, D), BF), SDS((1, D), F32)], name="final_loss")(x3, w, tgt)


NEG = -1e30


def _att_blocks(S, d):
    L = S // d
    nb = L // LANE
    return nb, (2 if nb > 1 else 1)


def _att_rows(S, d, idx):
    nb, nkb = _att_blocks(S, d)
    r = idx // nb
    n = idx % nb
    kb = jnp.maximum(n - 1, 0)
    if d == 1:
        qrows = pl.ds(pl.multiple_of(n * LANE, LANE), LANE)
        krows = pl.ds(pl.multiple_of(kb * LANE, LANE), LANE * nkb)
    else:
        qrows = pl.ds(r + n * (LANE * d), LANE, stride=d)
        krows = pl.ds(r + kb * (LANE * d), LANE * nkb, stride=d)
    qpos = n * LANE + lax.broadcasted_iota(jnp.int32, (LANE, LANE * nkb), 0)
    kpos = kb * LANE + lax.broadcasted_iota(jnp.int32, (LANE, LANE * nkb), 1)
    return qrows, krows, qpos - kpos


def _slab(col):
    return pl.BlockSpec((SEQ, LANE), lambda h, col=col: (0, col + h))


def _widen(slabs, wide):
    for t, ref in enumerate(slabs):
        wide[t] = ref[...].astype(F32)
    return tuple(tuple(wide.at[3 * kind + g] for g in range(3)) for kind in range(3))


def _attn_fwd(proj):
    S = proj.shape[0]
    scale = HEAD_DIM ** -0.5

    def body(q0, q1, q2, k0, k1, k2, v0, v1, v2, o_ref, lse_ref, og, lg, wide):
        qs, ks, vs = _widen((q0, q1, q2, k0, k1, k2, v0, v1, v2), wide)
        for g, (window, d) in enumerate(ATT_GROUPS):
            def blk(idx, carry, g=g, window=window, d=d):
                qrows, krows, dist = _att_rows(S, d, idx)
                q = qs[g][qrows, :].astype(BF)
                k = ks[g][krows, :].astype(BF)
                v = vs[g][krows, :].astype(BF)
                s = lax.dot_general(q, k, (((1,), (1,)), ((), ())), preferred_element_type=F32) * scale
                s = jnp.where((dist >= 0) & (dist <= window // d), s, NEG)
                m = jnp.max(s, axis=-1, keepdims=True)
                p = jnp.exp(s - m)
                l = jnp.sum(p, axis=-1, keepdims=True)
                o = jnp.dot((p / l).astype(BF), v, preferred_element_type=F32)
                og[g, qrows, :] = o
                lg[g, qrows, :] = jnp.broadcast_to(m + jnp.log(l), (LANE, LANE))
                return carry
            lax.fori_loop(0, S // LANE, blk, 0, unroll=True)

        def merge(t, carry):
            rows = pl.ds(pl.multiple_of(t * 256, 256), 256)
            l0, l1, l2 = lg[0, rows, :], lg[1, rows, :], lg[2, rows, :]
            m = jnp.maximum(jnp.maximum(l0, l1), l2)
            w0, w1, w2 = jnp.exp(l0 - m), jnp.exp(l1 - m), jnp.exp(l2 - m)
            den = w0 + w1 + w2
            o_ref[rows, :] = (w0 * og[0, rows, :] + w1 * og[1, rows, :] + w2 * og[2, rows, :]) / den
            lse_ref[rows, :] = m + jnp.log(den)
            return carry
        lax.fori_loop(0, S // 256, merge, 0)

    ins = [_slab(QA0 + 4 * g) for g in range(3)] + [_slab(KA0 + 4 * g) for g in range(3)] + [_slab(VA0 + 4 * g) for g in range(3)]
    out = pl.BlockSpec((SEQ, LANE), lambda h: (0, h))
    return _pcall(
        body, grid=(ATT_HEADS,), in_specs=ins, out_specs=[out, out],
        out_shape=[SDS((S, ATT_OUT), F32), SDS((S, ATT_OUT), F32)],
        scratch_shapes=[pltpu.VMEM((3, S, LANE), F32), pltpu.VMEM((3, S, LANE), F32), pltpu.VMEM((9, S, LANE), F32)],
        compiler_params=_cp(VMEM_BIG), name="attn_fwd")(*([proj] * 9))


def _attn_bwd(proj, o_att, lse, do_att, dproj):
    S = proj.shape[0]
    scale = HEAD_DIM ** -0.5

    def body(q0, q1, q2, k0, k1, k2, v0, v1, v2, o_ref, lse_ref, do_ref, dp_in, dp_ref, dqa, dka, dva, delta, stage, sem, wide):
        qs, ks, vs = _widen((q0, q1, q2, k0, k1, k2, v0, v1, v2), wide)
        head = pl.program_id(0)

        def slab_copies(g):
            return [pltpu.make_async_copy(stage.at[t], dp_ref.at[:, pl.ds(pl.multiple_of((base + 4 * g + head) * LANE, LANE), LANE)],
                                          sem.at[t]) for t, base in enumerate((QA0, KA0, VA0))]

        delta[...] = jnp.broadcast_to(jnp.sum(do_ref[...] * o_ref[...], axis=-1, keepdims=True), delta.shape)
        for g, (window, d) in enumerate(ATT_GROUPS):
            dka[...] = jnp.zeros_like(dka)
            dva[...] = jnp.zeros_like(dva)

            def blk(idx, carry, g=g, window=window, d=d):
                qrows, krows, dist = _att_rows(S, d, idx)
                q = qs[g][qrows, :].astype(BF)
                k = ks[g][krows, :].astype(BF)
                v = vs[g][krows, :].astype(BF)
                do = do_ref[qrows, :]
                s = lax.dot_general(q, k, (((1,), (1,)), ((), ())), preferred_element_type=F32) * scale
                p = jnp.exp(s - lse_ref[qrows, :][:, 0:1])
                p = jnp.where((dist >= 0) & (dist <= window // d), p, 0.0)
                dob = do.astype(BF)
                dva[krows, :] += lax.dot_general(p.astype(BF), dob, (((0,), (0,)), ((), ())), preferred_element_type=F32)
                dp = lax.dot_general(dob, v, (((1,), (1,)), ((), ())), preferred_element_type=F32)
                ds = (p * (dp - delta[qrows, :][:, 0:1]) * scale).astype(BF)
                dqa[qrows, :] = jnp.dot(ds, k, preferred_element_type=F32)
                dka[krows, :] += lax.dot_general(ds, q, (((0,), (0,)), ((), ())), preferred_element_type=F32)
                return carry
            lax.fori_loop(0, S // LANE, blk, 0, unroll=True)
            if g > 0:
                for cp in slab_copies(g - 1):
                    cp.wait()
            stage[0] = dqa[...].astype(BF)
            stage[1] = dka[...].astype(BF)
            stage[2] = dva[...].astype(BF)
            for cp in slab_copies(g):
                cp.start()
        for cp in slab_copies(len(ATT_GROUPS) - 1):
            cp.wait()

    cols = [QA0 + 4 * g for g in range(3)] + [KA0 + 4 * g for g in range(3)] + [VA0 + 4 * g for g in range(3)]
    ins = [_slab(c) for c in cols]
    one = pl.BlockSpec((SEQ, LANE), lambda h: (0, h))
    return _pcall(
        body, grid=(ATT_HEADS,), in_specs=ins + [one, one, one, _ANY], out_specs=_ANY,
        out_shape=SDS((S, IN_WIDTH), BF), aliases={12: 0},
        scratch_shapes=[pltpu.VMEM((S, LANE), F32)] * 4 + [pltpu.VMEM((3, S, LANE), BF), pltpu.SemaphoreType.DMA((3,)),
                                                           pltpu.VMEM((9, S, LANE), F32)],
        compiler_params=_cp(VMEM_BIG), name="attn_bwd")(*([proj] * 9), o_att, lse, do_att, dproj)


def _bdot(a, b, dims):
    return lax.dot_general(a.astype(BF), b.astype(BF), (dims, ((), ())), preferred_element_type=F32)


def _to_heads(x):
    return jnp.concatenate([x[None, :, h * LANE:(h + 1) * LANE] for h in range(HG_HEADS)], axis=0)


def _from_heads(x):
    return jnp.concatenate([x[h] for h in range(HG_HEADS)], axis=1)


def _hdot(a, b, ca, cb):
    return lax.dot_general(a.astype(BF), b.astype(BF), (((ca,), (cb,)), ((0,), (0,))), preferred_element_type=F32)


def _hg_chunk_all(qh, fh, ih, gh, lbraw, nw, st):
    C = HG_CHUNK
    mx = jnp.maximum(lbraw[0:1], lbraw[1:2])
    e0 = jnp.exp(lbraw[0:1] - mx)
    e1 = jnp.exp(lbraw[1:2] - mx)
    lb = e0 / (e0 + e1)
    f = lb + (1.0 - lb) * jax.nn.sigmoid(fh)
    logf = jnp.log(f)
    k = 1.0 - f
    q = qh * jax.nn.sigmoid(qh)
    causal = lax.broadcasted_iota(jnp.int32, (C, C), 0) >= lax.broadcasted_iota(jnp.int32, (C, C), 1)
    b = lax.dot_general(causal.astype(F32), logf, (((1,), (0,)), ((), ())), precision=lax.Precision.HIGHEST,
                        preferred_element_type=F32)
    b_last = b[C - 1:C]
    b_ref = b[C // 2:C // 2 + 1]
    v3 = _to_heads(ih)
    st_out = st * _to_heads(jnp.exp(b_last)) + _hdot(v3, _to_heads(k * jnp.exp(b_last - b)), 1, 1)
    inter = _hdot(_to_heads(q * jnp.exp(b)), st, 2, 2)
    a = _hdot(_to_heads(q * jnp.exp(b - b_ref)), _to_heads(k * jnp.exp(b_ref - b)), 2, 2)
    a = jnp.where(causal[None], a, 0.0)
    o = inter + _hdot(a, v3, 2, 1)
    o = o * lax.rsqrt(jnp.mean(o * o, axis=-1, keepdims=True) + RMS_EPS) * nw
    return _from_heads(o) * (gh * jax.nn.sigmoid(gh)), st_out


HG_STEP = 4
HG_ROWS = HG_STEP * HG_CHUNK


def _hg_in_specs(step_of):
    half = 4 * LANE
    specs = []
    for base in (QH0, FH0, IH0, GH0):
        for part in range(2):
            specs.append(pl.BlockSpec((HG_ROWS, half), lambda n, col=base // 4 + part: (step_of(n), col)))
    return specs


def _hg_sections(sec, c):
    rows = slice(c * HG_CHUNK, (c + 1) * HG_CHUNK)
    return [jnp.concatenate([sec[2 * s][rows, :], sec[2 * s + 1][rows, :]], axis=1).astype(F32) for s in range(4)]


def _hg_fwd(proj, lb, nw):
    S = proj.shape[0]
    C = HG_CHUNK
    NC = S // C

    def body(*refs):
        sec = refs[:8]
        lb_ref, nw_ref, o_ref, st_ref, st_scr = refs[8:]

        @pl.when(pl.program_id(0) == 0)
        def _():
            st_scr[...] = jnp.zeros_like(st_scr)

        st = st_scr[...]
        for c in range(HG_STEP):
            st_ref[:, c] = st
            o, st = _hg_chunk_all(*_hg_sections(sec, c), lb_ref[...], nw_ref[...], st)
            o_ref[c * C:(c + 1) * C, :] = o.astype(o_ref.dtype)
        st_scr[...] = st

    return _pcall(
        body, grid=(NC // HG_STEP,),
        in_specs=_hg_in_specs(lambda n: n) + [pl.BlockSpec((2, HG_WIDTH), lambda n: (0, 0)), pl.BlockSpec((1, LANE), lambda n: (0, 0))],
        out_specs=[pl.BlockSpec((HG_ROWS, HG_WIDTH), lambda n: (n, 0)),
                   pl.BlockSpec((HG_HEADS, HG_STEP, LANE, LANE), lambda n: (0, n, 0, 0))],
        out_shape=[SDS((S, HG_WIDTH), BF), SDS((HG_HEADS, NC, LANE, LANE), F32)],
        scratch_shapes=[pltpu.VMEM((HG_HEADS, LANE, LANE), F32)], name="hg_fwd")(*([proj] * 8), lb, nw)


def _hg_bwd(proj, lb, nw, states, do_hg, dproj):
    S = proj.shape[0]
    C = HG_CHUNK
    NC = S // C

    def body(*refs):
        sec = refs[:8]
        lb_ref, nw_ref, st_ref, do_ref, dp_in, dp_ref, dlb_ref, dnw_ref, dst_scr, stage, sem = refs[8:]
        n = pl.program_id(0)
        slot = n % 2

        def step_copies(s):
            rows = pl.ds(pl.multiple_of((NS - 1 - n) * HG_ROWS, HG_ROWS), HG_ROWS)
            return [pltpu.make_async_copy(stage.at[s, t], dp_ref.at[rows, pl.ds((QH0 + 8 * t) * LANE, HG_WIDTH)], sem.at[s, t])
                    for t in range(4)]

        @pl.when(n == 0)
        def _():
            dst_scr[...] = jnp.zeros_like(dst_scr)
            dlb_ref[...] = jnp.zeros_like(dlb_ref)
            dnw_ref[...] = jnp.zeros_like(dnw_ref)

        @pl.when(n >= 2)
        def _():
            for cp in step_copies(slot):
                cp.wait()

        dst = dst_scr[...]
        for c in reversed(range(HG_STEP)):
            rows = slice(c * C, (c + 1) * C)
            _, vjp = jax.vjp(_hg_chunk_all, *_hg_sections(sec, c), lb_ref[...], nw_ref[...], st_ref[:, c])
            dq, df, di, dg, dlb, dnw, dst = vjp((do_ref[rows, :].astype(F32), dst))
            dlb_ref[...] += dlb
            dnw_ref[...] += dnw
            for t, val in enumerate((dq, df, di, dg)):
                stage[slot, t, rows, :] = val.astype(BF)
        dst_scr[...] = dst
        for cp in step_copies(slot):
            cp.start()

        @pl.when(n == NS - 1)
        def _():
            for cp in step_copies(1 - slot) + step_copies(slot):
                cp.wait()

    NS = NC // HG_STEP
    rev = lambda n: NS - 1 - n
    ospec = pl.BlockSpec((HG_ROWS, HG_WIDTH), lambda n: (rev(n), 0))
    lbs = pl.BlockSpec((2, HG_WIDTH), lambda n: (0, 0))
    nws = pl.BlockSpec((1, LANE), lambda n: (0, 0))
    outs = _pcall(
        body, grid=(NS,),
        in_specs=_hg_in_specs(rev) + [lbs, nws, pl.BlockSpec((HG_HEADS, HG_STEP, LANE, LANE), lambda n: (0, rev(n), 0, 0)), ospec, _ANY],
        out_specs=[_ANY, lbs, nws],
        out_shape=[SDS((S, IN_WIDTH), BF), SDS((2, HG_WIDTH), F32), SDS((1, LANE), F32)], aliases={12: 0},
        scratch_shapes=[pltpu.VMEM((HG_HEADS, LANE, LANE), F32), pltpu.VMEM((2, 4, HG_ROWS, HG_WIDTH), BF),
                        pltpu.SemaphoreType.DMA((2, 4))],
        name="hg_bwd")(*([proj] * 8), lb, nw, states, do_hg, dproj)
    return outs


def _merge(ga, gb, ba, bb):
    return jax.nn.sigmoid(ga) * ba + jax.nn.sigmoid(gb) * bb


_MT = 512


def _swiglu(a, b):
    return a * jax.nn.sigmoid(a) * b


def _cross(q, k, v):
    s = _bdot(q, k, ((1,), (1,))) * (HEAD_DIM ** -0.5)
    e = jnp.exp(s - jnp.max(s, axis=-1, keepdims=True))
    p = e / jnp.sum(e, axis=-1, keepdims=True)
    return _bdot(p, v, ((1,), (0,)))


def _cross_specs():
    q = pl.BlockSpec((_MT, LANE), lambda h, i: (i, h))
    k = pl.BlockSpec((MEM_LEN, LANE), lambda h, i: (0, h))
    v = pl.BlockSpec((MEM_LEN, LANE), lambda h, i: (0, CROSS_HEADS + h))
    return q, k, v


def _cross_fwd(qc, kvc):
    S = qc.shape[0]
    q, k, v = _cross_specs()

    def body(q_ref, k_ref, v_ref, o_ref):
        o_ref[...] = _cross(q_ref[...], k_ref[...], v_ref[...]).astype(o_ref.dtype)

    return _pcall(body, grid=(CROSS_HEADS, S // _MT), in_specs=[q, k, v], out_specs=q,
                          out_shape=SDS((S, CROSS_WIDTH), BF), name="cross_fwd")(qc, kvc, kvc)


def _cross_bwd(qc, kvc, doc):
    S = qc.shape[0]
    q, k, v = _cross_specs()

    def body(q_ref, k_ref, v_ref, do_ref, dq_ref, dk_ref, dv_ref):
        @pl.when(pl.program_id(1) == 0)
        def _():
            dk_ref[...] = jnp.zeros_like(dk_ref)
            dv_ref[...] = jnp.zeros_like(dv_ref)

        _, vjp = jax.vjp(_cross, q_ref[...], k_ref[...], v_ref[...])
        dq, dk, dv = vjp(do_ref[...])
        dq_ref[...] = dq.astype(dq_ref.dtype)
        dk_ref[...] += dk
        dv_ref[...] += dv

    return _pcall(body, grid=(CROSS_HEADS, S // _MT), in_specs=[q, k, v, q], out_specs=[q, k, k],
                          out_shape=[SDS((S, CROSS_WIDTH), BF), SDS((MEM_LEN, CROSS_WIDTH), F32), SDS((MEM_LEN, CROSS_WIDTH), F32)],
                          name="cross_bwd")(qc, kvc, kvc, doc)


def _local_step(x, mem, tgt, p, ready):
    h, h_a, h_b = _rms_fwd(x, p["ln_mix"], "rms_mix", split=IN_SPLIT)
    proj = _mm(h_a, p["w_in_a"], "nt", BF, "mm_proj_a")
    proj = _mm(h_b, p["w_in_b"], "nt", BF, "mm_proj_b", res=proj)
    o_att, lse = _attn_fwd(proj)
    o_hg, states = _hg_fwd(proj, p["hg_lb"], p["hg_norm"])
    ba = _mm(o_att, p["w_ba"], "nn", BF, "mm_ba")
    gates = [(proj, GA0 * LANE // _MT), (proj, GB0 * LANE // _MT)]

    def f32s(*vals):
        return [v.astype(F32) for v in vals]

    bb, merged = _mm_fused(o_hg, p["w_bb"], "nn", "mm_bb_merge", gates + [ba], [BF, BF], tn=_MT,
                           epi=lambda d, ga, gb, bav: (d, _merge(*f32s(ga, gb, bav), d)))
    x1 = _mm(merged, p["w_out"], "nn", F32, "mm_out", res=x)
    hc = _rms_fwd(x1, p["ln_cross"], "rms_cross")
    qc = _mm(hc, p["wq"], "nn", F32, "mm_q")
    mn = _rms_fwd(mem, p["ln_mem"], "rms_mem")
    kvc = _mm(mn, p["wkv"], "nn", F32, "mm_kv")
    oc = _cross_fwd(qc, kvc)
    x2 = _mm(oc, p["wo"], "nn", F32, "mm_o", res=x1)
    hf = _rms_fwd(x2, p["ln_ffn"], "rms_ffn")
    a = _mm(hf, p["w1"], "nt", BF, "mm_w1")
    b, g = _mm_fused(hf, p["w3"], "nt", "mm_w3_swiglu", [a], [BF, BF], lambda d, av: (d, _swiglu(av.astype(F32), d)))
    x3 = _mm(g, p["w2"], "nn", F32, "mm_w2", res=x2)
    loss, dx3b, d_ln_final = _final_loss(x3, p["ln_final"], tgt)

    gw, gs = {}, {"ln_final": d_ln_final}
    da, db = _mm_fused(dx3b, p["w2"], "nt", "mm_dg_swiglu", [a, b], [BF, BF],
                       lambda d, av, bv: jax.vjp(_swiglu, av.astype(F32), bv.astype(F32))[1](d))
    gw["w2"] = _mm(g, dx3b, "tn", BF, "mm_dw2")
    dhf = _mm(da, p["w1"], "nn", F32, "mm_dhf1")
    dhf = _mm(db, p["w3"], "nn", BF, "mm_dhf3", res=dhf)
    gw["w1"] = _mm(da, hf, "tn", BF, "mm_dw1")
    gw["w3"] = _mm(db, hf, "tn", BF, "mm_dw3")
    ready("ffn", gw)
    dx2b, gs["ln_ffn"] = _rms_bwd(x2, p["ln_ffn"], dhf, dx3b, "rms_ffn_bwd")
    doc = _mm(dx2b, p["wo"], "nt", F32, "mm_doc")
    gw["wo"] = _mm(oc, dx2b, "tn", BF, "mm_dwo")
    dqc, dk, dv = _cross_bwd(qc, kvc, doc)
    dkvc = jnp.concatenate([dk, dv], axis=1)
    dhc = _mm(dqc, p["wq"], "nt", BF, "mm_dhc")
    gw["wq"] = _mm(hc, dqc, "tn", BF, "mm_dwq")
    dmn = _mm(dkvc, p["wkv"], "nt", F32, "mm_dmn")
    gw["wkv"] = _mm(mn, dkvc, "tn", BF, "mm_dwkv")
    ready("cross", gw)
    gs["ln_mem"] = _rms_bwd(mem, p["ln_mem"], dmn, None, "rms_mem_bwd")
    dx1b, gs["ln_cross"] = _rms_bwd(x1, p["ln_cross"], dhc, dx2b, "rms_cross_bwd")
    def merge_back(d, ga, gb, bav, bbv):
        dga, dgb, dba_, dbb_ = jax.vjp(_merge, *f32s(ga, gb, bav, bbv))[1](d)
        return dba_, dbb_, dga, dgb

    dba, dbb, dproj = _mm_fused(dx1b, p["w_out"], "nt", "mm_dmerged_merge", gates + [ba, bb], [BF, BF], merge_back, tn=_MT,
                                scatter=(SDS((x.shape[0], IN_WIDTH), BF), [GA0 * LANE, GB0 * LANE]))
    gw["w_out"] = _mm(merged, dx1b, "tn", BF, "mm_dwout")
    do_att = _mm(dba, p["w_ba"], "nt", F32, "mm_doatt")
    gw["w_ba"] = _mm(o_att, dba, "tn", BF, "mm_dwba")
    do_hg = _mm(dbb, p["w_bb"], "nt", F32, "mm_dohg")
    gw["w_bb"] = _mm(o_hg, dbb, "tn", BF, "mm_dwbb")
    ready("mix", gw)
    dproj = _attn_bwd(proj, o_att, lse, do_att, dproj)
    dproj, gs["hg_lb"], gs["hg_norm"] = _hg_bwd(proj, p["hg_lb"], p["hg_norm"], states, do_hg, dproj)
    for q in range(IN_PARTS):
        gw[f"w_in_q{q}"] = _mm(dproj, h, "tn", BF, f"mm_dwin_q{q}", b_part=(q, IN_PARTS))
        ready(f"in_q{q}", gw)
        if q == 0:
            ready("mixers_done", gw)
    dh_a = _mm(dproj, p["w_in_a"], "nn", BF, "mm_dh_a")
    ready("dh_half", gw)
    dh = (dh_a, _mm(dproj, p["w_in_b"], "nn", BF, "mm_dh_b"))
    grad_x, gs["ln_mix"] = _rms_bwd(x, p["ln_mix"], dh, dx1b, "rms_mix_bwd", dx_dtype=F32)
    return loss, grad_x, gs


IN_SPLIT = 1280
_MATS = (
    ("w_in_a", "w_in", IN_WIDTH, IN_SPLIT, "row", IN_WIDTH // N_DEV, True),
    ("w_in_b", "w_in", IN_WIDTH, D_MODEL - IN_SPLIT, "row", IN_WIDTH // N_DEV, True),
    ("w_ba", "w_branch_a", ATT_OUT, D_MODEL, "col", D_MODEL // N_DEV, False),
    ("w_bb", "w_branch_b", HG_WIDTH, D_MODEL, "col", D_MODEL // N_DEV, False),
    ("w_out", "w_out", D_MODEL, D_MODEL, "row", D_MODEL // N_DEV, False),
    ("wq", "wq_cross", D_MODEL, CROSS_WIDTH, "row", D_MODEL // N_DEV, False),
    ("wkv", "wkv_cross", D_MODEL, 2 * CROSS_WIDTH, "row", D_MODEL // N_DEV, False),
    ("wo", "wo_cross", CROSS_WIDTH, D_MODEL, "col", D_MODEL // N_DEV, False),
    ("w1", "w1", D_FF, D_MODEL, "row", D_FF // N_DEV, True),
    ("w3", "w3", D_FF, D_MODEL, "row", D_FF // N_DEV, True),
    ("w2", "w2", D_FF, D_MODEL, "row", D_FF // N_DEV, False),
    ("w_in_q0", "w_in", IN_WIDTH, D_MODEL // 2, "row", IN_WIDTH // N_DEV, True),
    ("w_in_q1", "w_in", IN_WIDTH, D_MODEL // 2, "row", IN_WIDTH // N_DEV, True),
)
IN_PARTS = 2
_INDEX = {m[0]: i for i, m in enumerate(_MATS)}


def _unit_shape(i):
    _, _, K, N, kind, sh, _ = _MATS[i]
    return (sh, N) if kind == "row" else (K, sh)


def _win(ref, i, chip, par, half=None):
    _, _, K, _, kind, sh, _ = _MATS[i]
    if kind == "row":
        rows, start = (sh, 0) if half is None else (sh // 2, half * (sh // 2))
        return ref.at[pl.ds(pl.multiple_of((2 * chip + par) * sh + start, 16), rows), :]
    rows, start = (K, 0) if half is None else (K // 2, half * (K // 2))
    return ref.at[pl.ds(start, rows), pl.ds(pl.multiple_of((2 * chip + par) * sh, LANE), sh)]


def _my_place():
    x, y, c = lax.axis_index("x"), lax.axis_index("y"), lax.axis_index("c")
    chips = [(1 - x, y), (x, 1 - y), (1 - x, 1 - y)]
    return x, y, c, chips


def _cast_unit(w, name, cols=None):
    K, N = w.shape
    c0, c1 = cols or (0, N)
    tr = _pick(K, (256, 400, 352))

    def body(w_ref, o_ref):
        o_ref[...] = w_ref[:, c0:c1].astype(BF)

    src = pl.BlockSpec((tr, N), lambda i: (i, 0))
    dst = pl.BlockSpec((tr, c1 - c0), lambda i: (i, 0))
    return _pcall(body, grid=(K // tr,), in_specs=[src], out_specs=dst, out_shape=SDS((K, c1 - c0), BF), name=name)(w)


def _handshake(peers):
    barrier = pltpu.get_barrier_semaphore()
    for peer in peers:
        pl.semaphore_signal(barrier, inc=1, device_id=peer, device_id_type=MESH)
    pl.semaphore_wait(barrier, len(peers))


def _sequencer(body, out_type, sems, cid, name):
    return pl.kernel(body, out_type=out_type, mesh=plsc.ScalarSubcoreMesh(axis_name="seq", num_cores=1),
                     scratch_types=sems, compiler_params=pltpu.CompilerParams(collective_id=cid), name=name)


def _all_gather(units, idx, cid, name):
    n = len(idx)

    def body(*refs):
        us, fulls = refs[:n], refs[n:2 * n]
        ssem, rsem, lsem = refs[2 * n:]
        x, y, c, _ = _my_place()
        xn, yn, dg = (1 - x, y), (x, 1 - y), (1 - x, 1 - y)
        me, sib = (x, y, c), (x, y, 1 - c)
        _handshake([sib, (*xn, c), (*yn, c)])

        def win(i, chip_xy, par, half=None):
            return _win(fulls[i], idx[i], 2 * chip_xy[0] + chip_xy[1], par, half)

        def rcopy(i, k, src, dst, to):
            return pltpu.make_async_remote_copy(src_ref=src, dst_ref=dst, send_sem=ssem.at[i, k], recv_sem=rsem.at[i, k],
                                                device_id=to, device_id_type=MESH)

        def landed(i, k, blk):
            rcopy(i, k, blk, blk, me).wait_recv()

        sends, locs = [], []

        def send(i, k, blk, to, src=None):
            cp = rcopy(i, k, blk if src is None else src, blk, to)
            cp.start()
            sends.append(cp)

        for i in range(n):
            mine = win(i, (x, y), c)
            loc = pltpu.make_async_copy(us[i], mine, lsem.at[i])
            loc.start()
            locs.append(loc)
            send(i, 0, mine, sib, src=us[i])
            send(i, 1, mine, (*xn, c), src=us[i])
            send(i, 2, mine, (*yn, c), src=us[i])
        for i in range(n):
            landed(i, 1, win(i, xn, c))
            send(i, 3, win(i, xn, c, 0), (*yn, c))
            send(i, 5, win(i, xn, c), sib)
            landed(i, 2, win(i, yn, c))
            send(i, 4, win(i, yn, c, 1), (*xn, c))
            send(i, 6, win(i, yn, c), sib)
        for i in range(n):
            landed(i, 3, win(i, dg, c, 0))
            landed(i, 4, win(i, dg, c, 1))
            send(i, 7, win(i, dg, c), sib)
        for i in range(n):
            landed(i, 0, win(i, (x, y), 1 - c))
            landed(i, 5, win(i, xn, 1 - c))
            landed(i, 6, win(i, yn, 1 - c))
            landed(i, 7, win(i, dg, 1 - c))
        for cp in sends:
            cp.wait_send()
        for loc in locs:
            loc.wait()

    out_type = [SDS((_MATS[i][2], _MATS[i][3]), BF) for i in idx]
    sems = [pltpu.SemaphoreType.DMA((n, 8)), pltpu.SemaphoreType.DMA((n, 8)), pltpu.SemaphoreType.DMA((n,))]
    return _sequencer(body, out_type, sems, cid, name)(*units)


def _rs_sibling(grads, idx, cid, name):
    n = len(idx)

    def body(*refs):
        gs, bufs = refs[:n], refs[n:2 * n]
        ssem, rsem = refs[2 * n:]
        x, y, c, _ = _my_place()
        _handshake([(x, y, 1 - c)])
        cps = []
        for i in range(n):
            for k in range(4):
                cp = pltpu.make_async_remote_copy(
                    src_ref=_win(gs[i], idx[i], k, 1 - c), dst_ref=bufs[i].at[k], send_sem=ssem.at[i, k], recv_sem=rsem.at[i, k],
                    device_id=(x, y, 1 - c), device_id_type=MESH)
                cp.start()
                cps.append(cp)
        for cp in cps:
            cp.wait()

    out_type = [SDS((4, *_unit_shape(i)), BF) for i in idx]
    sems = [pltpu.SemaphoreType.DMA((n, 4)), pltpu.SemaphoreType.DMA((n, 4))]
    return _sequencer(body, out_type, sems, cid, name)(*grads)


def _sibling_sum_tc(grad, i, cid, name):
    unit = _unit_shape(i)

    def body(g_ref, t_ref, theirs, mine, summed, ssem, rsem, lsem, osem):
        x, y, c, _ = _my_place()
        _handshake([(x, y, 1 - c)])
        sends, loads = [], []
        for k in range(4):
            sends.append(pltpu.make_async_remote_copy(
                src_ref=_win(g_ref, i, k, 1 - c), dst_ref=theirs.at[k], send_sem=ssem.at[k], recv_sem=rsem.at[k],
                device_id=(x, y, 1 - c), device_id_type=MESH))
            loads.append(pltpu.make_async_copy(_win(g_ref, i, k, c), mine.at[k], lsem.at[k]))
            sends[-1].start()
            loads[-1].start()
        stores = []
        for k in range(4):
            loads[k].wait()
            sends[k].wait_recv()
            summed[k] = (mine[k].astype(F32) + theirs[k].astype(F32)).astype(BF)
            stores.append(pltpu.make_async_copy(summed.at[k], t_ref.at[k], osem.at[k]))
            stores[-1].start()
        for k in range(4):
            sends[k].wait_send()
            stores[k].wait()

    block = pltpu.VMEM((4, *unit), BF)
    return _pcall(body, in_specs=[_ANY], out_specs=_ANY, out_shape=SDS((4, *unit), BF),
                  scratch_shapes=[block, block, block] + [pltpu.SemaphoreType.DMA((4,))] * 4,
                  compiler_params=_cp(VMEM_BIG, collective_id=cid), name=name)(grad)


def _chip_sum(g, sib, i, place, name):
    _, _, K, N, kind, sh, _ = _MATS[i]

    def body(pref, g_ref, s_ref, o_ref):
        o_ref[...] = (g_ref[...].astype(F32) + s_ref[...].astype(F32)).astype(BF)

    if kind == "row":
        tr = _pick(sh, (800, 352, 256))
        per = sh // tr
        grid = (4, per)
        g_spec = pl.BlockSpec((tr, N), lambda k, r, pref: ((2 * k + pref[0]) * per + r, 0))
        u_spec = pl.BlockSpec((None, tr, N), lambda k, r, pref: (k, r, 0))
    else:
        grid = (4,)
        g_spec = pl.BlockSpec((K, sh), lambda k, pref: (0, 2 * k + pref[0]))
        u_spec = pl.BlockSpec((None, K, sh), lambda k, pref: (k, 0, 0))
    return _pcall(body, grid=grid, in_specs=[g_spec, u_spec], out_specs=u_spec, out_shape=SDS((4, *_unit_shape(i)), BF),
                  prefetch=True, name=name)(place, g, sib)


def _rs_chips(parts, idx, cid, name):
    n = len(idx)

    def body(*refs):
        ts, bufs, relays = refs[:n], refs[n:2 * n], refs[2 * n:3 * n]
        ssem, rsem = refs[3 * n:]
        x, y, c, _ = _my_place()
        xn, yn, dg = (1 - x, y), (x, 1 - y), (1 - x, 1 - y)
        to_x, to_y, me = (*xn, c), (*yn, c), (x, y, c)
        _handshake([to_x, to_y])

        def rcopy(i, k, src, dst, to):
            return pltpu.make_async_remote_copy(src_ref=src, dst_ref=dst, send_sem=ssem.at[i, k], recv_sem=rsem.at[i, k],
                                                device_id=to, device_id_type=MESH)

        def halves(i):
            rows = _unit_shape(idx[i])[0] // 2
            return [pl.ds(0, rows), pl.ds(rows, rows)]

        sends = []

        def send(cp):
            cp.start()
            sends.append(cp)

        for i in range(n):
            h0, h1 = halves(i)
            diag = ts[i].at[2 * dg[0] + dg[1]]
            send(rcopy(i, 0, ts[i].at[2 * xn[0] + xn[1]], bufs[i].at[0], to_x))
            send(rcopy(i, 1, ts[i].at[2 * yn[0] + yn[1]], bufs[i].at[1], to_y))
            send(rcopy(i, 2, diag.at[h0], relays[i].at[0], to_x))
            send(rcopy(i, 3, diag.at[h1], relays[i].at[1], to_y))
        for i in range(n):
            h0, h1 = halves(i)
            rcopy(i, 2, relays[i].at[0], relays[i].at[0], me).wait_recv()
            send(rcopy(i, 4, relays[i].at[0], bufs[i].at[2, h0], to_y))
            rcopy(i, 3, relays[i].at[1], relays[i].at[1], me).wait_recv()
            send(rcopy(i, 5, relays[i].at[1], bufs[i].at[2, h1], to_x))
        for i in range(n):
            h0, h1 = halves(i)
            rcopy(i, 0, bufs[i].at[0], bufs[i].at[0], me).wait_recv()
            rcopy(i, 1, bufs[i].at[1], bufs[i].at[1], me).wait_recv()
            rcopy(i, 4, bufs[i].at[2, h0], bufs[i].at[2, h0], me).wait_recv()
            rcopy(i, 5, bufs[i].at[2, h1], bufs[i].at[2, h1], me).wait_recv()
        for cp in sends:
            cp.wait_send()

    def relay_shape(i):
        rows, cols = _unit_shape(i)
        return (2, rows // 2, cols)

    out_type = [SDS((3, *_unit_shape(i)), BF) for i in idx] + [SDS(relay_shape(i), BF) for i in idx]
    sems = [pltpu.SemaphoreType.DMA((n, 6)), pltpu.SemaphoreType.DMA((n, 6))]
    return _sequencer(body, out_type, sems, cid, name)(*parts)[:n]


def _adamw(w, g, m, v):
    m = ADAM_B1 * m + (1.0 - ADAM_B1) * g
    v = ADAM_B2 * v + (1.0 - ADAM_B2) * jnp.square(g)
    m_hat = m / (1.0 - ADAM_B1 ** ADAM_STEP)
    v_hat = v / (1.0 - ADAM_B2 ** ADAM_STEP)
    delta = -ADAM_LR * (m_hat / (jnp.sqrt(v_hat) + ADAM_EPS) + ADAM_WD * w)
    return delta, m, v


def _adam_mat(parts, recvs, w, m, v, place, name):
    R, C = w.shape
    n = len(parts)
    tr = _pick(R, (256, 160, 176))

    def body(pref, *refs):
        t_refs, r_refs = refs[:n], refs[n:2 * n]
        w_ref, m_ref, v_ref, g_out, d_out, m_out, v_out = refs[2 * n:]
        cols = [t[...].astype(F32) + r[0].astype(F32) + r[1].astype(F32) + r[2].astype(F32) for t, r in zip(t_refs, r_refs)]
        g = jnp.concatenate(cols, axis=1) if n > 1 else cols[0]
        d, mn, vn = _adamw(w_ref[...], g, m_ref[...], v_ref[...])
        g_out[...] = g
        d_out[...] = d
        m_out[...] = mn
        v_out[...] = vn

    t_spec = pl.BlockSpec((None, tr, C // n), lambda i, pref: (pref[1], i, 0))
    r_spec = pl.BlockSpec((3, tr, C // n), lambda i, pref: (0, i, 0))
    s_spec = pl.BlockSpec((tr, C), lambda i, pref: (i, 0))
    return _pcall(body, grid=(R // tr,), in_specs=[t_spec] * n + [r_spec] * n + [s_spec] * 3, out_specs=[s_spec] * 4,
                  out_shape=[SDS((R, C), F32)] * 4, prefetch=True, name=name)(place, *parts, *recvs, w, m, v)


_SMALL = (("ln_mix", "ln_mix_w", D_MODEL), ("ln_cross", "ln_cross_w", D_MODEL), ("ln_mem", "ln_mem_w", D_MODEL),
          ("ln_ffn", "ln_ffn_w", D_MODEL), ("ln_final", "ln_final_w", D_MODEL), ("hg_lb", "hg_lower_bounds", 2 * HG_WIDTH),
          ("hg_norm", "hg_norm_w", HEAD_DIM))
_PACK = sum(n for _, _, n in _SMALL) + LANE


def _small_sync(grow, wrow, mrow, vrow):
    def body(g_ref, w_ref, m_ref, v_ref, o_ref, gath, ssem, rsem):
        x, y, c, _ = _my_place()
        me = 4 * x + 2 * y + c
        gath[pl.ds(me, 1), :] = g_ref[...]
        cps = []
        for dlt in range(1, N_DEV):
            peer = (me + dlt) % N_DEV
            cp = pltpu.make_async_remote_copy(
                src_ref=g_ref, dst_ref=gath.at[pl.ds(me, 1), :], send_sem=ssem.at[peer], recv_sem=rsem.at[me],
                device_id=(peer // 4, (peer // 2) % 2, peer % 2), device_id_type=MESH)
            cp.start()
            cps.append(cp)
        for dlt in range(1, N_DEV):
            src = (me + dlt) % N_DEV
            pltpu.make_async_remote_copy(
                src_ref=g_ref, dst_ref=gath.at[pl.ds(src, 1), :], send_sem=ssem.at[src], recv_sem=rsem.at[src],
                device_id=(x, y, c), device_id_type=MESH).wait_recv()
        for cp in cps:
            cp.wait_send()
        g = gath[0:1, :]
        for j in range(1, N_DEV):
            g = g + gath[j:j + 1, :]
        d, mn, vn = _adamw(w_ref[...], g, m_ref[...], v_ref[...])
        o_ref[0:1, :] = g
        o_ref[1:2, :] = d
        o_ref[2:3, :] = mn
        o_ref[3:4, :] = vn

    vm = pl.BlockSpec(memory_space=pltpu.VMEM)
    return _pcall(
        body, in_specs=[vm] * 4, out_specs=vm, out_shape=SDS((4, _PACK), F32),
        scratch_shapes=[pltpu.VMEM((N_DEV, _PACK), F32), pltpu.SemaphoreType.DMA((N_DEV,)), pltpu.SemaphoreType.DMA((N_DEV,))],
        name="small_sync")(grow, wrow, mrow, vrow)


def _pack_small(d, loss_row=None):
    parts = [d[k].reshape(1, n).astype(F32) for k, _, n in _SMALL]
    parts.append(loss_row if loss_row is not None else jnp.ones((1, LANE), F32))
    return jnp.concatenate(parts, axis=1)


def kernel(x, mem, ln_mix_w, w_in, hg_norm_w, hg_lower_bounds, w_branch_a, w_branch_b, w_out, ln_cross_w, ln_mem_w, wq_cross, wkv_cross, wo_cross, ln_ffn_w, w1, w3, w2, ln_final_w, loss_target, m_ln_mix_w, m_w_in, m_hg_norm_w, m_hg_lower_bounds, m_w_branch_a, m_w_branch_b, m_w_out, m_ln_cross_w, m_ln_mem_w, m_wq_cross, m_wkv_cross, m_wo_cross, m_ln_ffn_w, m_w1, m_w3, m_w2, m_ln_final_w, v_ln_mix_w, v_w_in, v_hg_norm_w, v_hg_lower_bounds, v_w_branch_a, v_w_branch_b, v_w_out, v_ln_cross_w, v_ln_mem_w, v_wq_cross, v_wkv_cross, v_wo_cross, v_ln_ffn_w, v_w1, v_w3, v_w2, v_ln_final_w):
    given = dict(locals())
    place = jnp.stack([lax.axis_index("c"), 2 * lax.axis_index("x") + lax.axis_index("y")]).astype(jnp.int32)

    _ORDER[0] = None
    index = _INDEX

    def local_view(a, i):
        return a[0].T if _MATS[i][6] else a[0]

    shard = {key: local_view(given[ref], i) for i, (key, ref, *_) in enumerate(_MATS)}

    p = {}
    col_part = {"w_in_a": (0, IN_SPLIT), "w_in_b": (IN_SPLIT, D_MODEL)}
    waves = (("in_a", ["w_in_a"], 1), ("in_b", ["w_in_b"], 12), ("mid", ["w_ba", "w_bb", "w_out", "wq", "wkv", "wo"], 2),
             ("w1", ["w1"], 11), ("w3", ["w3"], 15), ("w2", ["w2"], 16))
    for wave, keys, cid in waves:
        units = [_cast_unit(shard[k], "cast_" + k, col_part.get(k)) for k in keys]
        p.update(zip(keys, _all_gather(units, [index[k] for k in keys], cid, "all_gather_" + wave)))
    small_w = {"ln_mix": ln_mix_w, "ln_cross": ln_cross_w, "ln_mem": ln_mem_w, "ln_ffn": ln_ffn_w,
               "ln_final": ln_final_w.reshape(1, D_MODEL), "hg_lb": hg_lower_bounds, "hg_norm": hg_norm_w}
    p.update(small_w)

    groups = {"ffn": ["w2", "w1", "w3"], "cross": ["wo", "wq", "wkv"], "mix": ["w_out", "w_ba", "w_bb"]}
    cids = {"ffn": (3, 4), "cross": (5, 6), "mix": (7, 8)}
    for q in range(IN_PARTS):
        groups[f"in_q{q}"] = [f"w_in_q{q}"]
        cids[f"in_q{q}"] = ((9, 10), (13, 14), (17, 18), (19, 20))[q]
    sib, parts, recv, out = {}, {}, {}, {}

    grads = {}

    def chip_stage(name):
        keys = groups[name]
        parts[name] = [_chip_sum(grads[k], s, index[k], place, "chip_sum_" + k) for k, s in zip(keys, sib[name])]
        recv[name] = list(_rs_chips(parts[name], [index[k] for k in keys], cids[name][1], "rs_chips_" + name))

    def adam(ref, i, part_list, recv_list):
        res = _adam_mat(part_list, recv_list, shard[_MATS[i][0]], local_view(given["m_" + ref], i), local_view(given["v_" + ref], i),
                        place, "adam_" + ref)
        out[ref] = [(a.T if _MATS[i][6] else a)[None] for a in res]

    def adam_stage(name, only=None, skip=None):
        for n, (k, t, r) in enumerate(zip(groups[name], parts[name], recv[name])):
            if (only is None or n == only) and n != skip:
                adam(_MATS[index[k]][1], index[k], [t], [r])

    def ready(name, gw):
        if name == "mixers_done":
            adam_stage("ffn", only=0)
            chip_stage("mix")
            adam_stage("ffn", skip=0)
            adam_stage("cross")
            return
        if name == "dh_half":
            adam_stage("mix")
            return
        keys = groups[name]
        grads.update({k: gw[k] for k in keys})
        if name.startswith("in_q"):
            parts[name] = [_sibling_sum_tc(gw[k], index[k], cids[name][0], "sibling_sum_" + k) for k in keys]
            recv[name] = list(_rs_chips(parts[name], [index[k] for k in keys], cids[name][1], "rs_chips_" + name))
            if name == "in_q1":
                _ALSO_AFTER.extend(recv["in_q0"])
            return
        sib[name] = _rs_sibling([gw[k] for k in keys], [index[k] for k in keys], cids[name][0], "rs_sibling_" + name)
        if name == "cross":
            chip_stage("ffn")
        if name == "mix":
            chip_stage("cross")

    loss_row, grad_x, gs = _local_step(x[0], mem[0], loss_target[0], p, ready)

    sm = {k: given["m_" + ref] for k, ref, _ in _SMALL}
    sv = {k: given["v_" + ref] for k, ref, _ in _SMALL}
    synced = _small_sync(_pack_small(gs, loss_row), _pack_small(small_w), _pack_small(sm), _pack_small(sv))
    in_groups = [f"in_q{q}" for q in range(IN_PARTS)]
    adam("w_in", index["w_in_a"], sum((parts[g] for g in in_groups), []), sum((recv[g] for g in in_groups), []))
    off = 0
    for k, ref, n in _SMALL:
        shape = given[ref].shape
        out[ref] = [synced[r, off:off + n].reshape(shape) for r in range(4)]
        off += n
    loss = synced[0, off]

    order = ["ln_mix_w", "w_in", "hg_norm_w", "hg_lower_bounds", "w_branch_a", "w_branch_b", "w_out", "ln_cross_w", "ln_mem_w",
             "wq_cross", "wkv_cross", "wo_cross", "ln_ffn_w", "w1", "w3", "w2", "ln_final_w"]
    return (loss, grad_x[None], *[out[n][0] for n in order], *[out[n][1] for n in order],
            *[out[n][2] for n in order], *[out[n][3] for n in order])
```

```python
import jax
import jax.numpy as jnp
from jax import lax
from jax.experimental import pallas as pl
from jax.experimental.pallas import tpu as pltpu
from jax.experimental.pallas import tpu_sc as plsc

F32 = jnp.float32
BF = jnp.bfloat16
SDS = jax.ShapeDtypeStruct

D_MODEL = 2048
SEQ = 2048
HEAD_DIM = 128
MEM_LEN = 256
ATT_GROUPS = ((128, 1), (512, 4), (2048, 16))
ATT_HEADS = 4
ATT_WIDTH = 1536
ATT_OUT = 512
HG_HEADS = 8
HG_WIDTH = 1024
HG_CHUNK = 64
IN_WIDTH = 12800
CROSS_HEADS = 4
CROSS_WIDTH = 512
D_FF = 5632
RMS_EPS = 1e-6
ADAM_LR = 0.001
ADAM_B1 = 0.9
ADAM_B2 = 0.999
ADAM_EPS = 1e-08
ADAM_WD = 0.01
ADAM_STEP = 10
N_DEV = 8

LANE = 128
QA0, KA0, VA0 = 0, 12, 24
QH0, FH0, IH0, GH0 = 36, 44, 52, 60
GA0, GB0 = 68, 84

VMEM_BIG = 56 * 1024 * 1024
MESH = pl.DeviceIdType.MESH


def _cp(vmem=None, **kw):
    if vmem is not None:
        kw["vmem_limit_bytes"] = vmem
    return pltpu.CompilerParams(**kw)


def _pick(n, cands):
    for c in cands:
        if n % c == 0:
            return c
    raise ValueError(f"no tile for {n}")


_ANY = pl.BlockSpec(memory_space=pl.ANY)
_ORDER = [None]
_ALSO_AFTER = []


def _pcall(body, *, name, in_specs, out_specs, out_shape, grid=(), scratch_shapes=(), compiler_params=None, prefetch=False,
           aliases=None):
    def run(*args):
        dep = _ORDER[0]
        specs, operands, kernel_fn = list(in_specs), list(args), body
        deps = [] if dep is None or any(dep is a for a in operands) else [dep]
        deps += _ALSO_AFTER
        del _ALSO_AFTER[:]
        if deps:
            at, nd = len(operands), len(deps)
            specs += [_ANY] * nd
            operands += deps

            def kernel_fn(*refs):
                return body(*refs[:at], *refs[at + nd:])
        if prefetch:
            call = pl.pallas_call(
                kernel_fn, out_shape=out_shape, name=name, compiler_params=compiler_params,
                grid_spec=pltpu.PrefetchScalarGridSpec(num_scalar_prefetch=1, grid=grid, in_specs=specs, out_specs=out_specs,
                                                       scratch_shapes=scratch_shapes))
        else:
            call = pl.pallas_call(kernel_fn, grid=grid, in_specs=specs, out_specs=out_specs, out_shape=out_shape,
                                  scratch_shapes=scratch_shapes, compiler_params=compiler_params, name=name,
                                  input_output_aliases=aliases or {})
        out = call(*operands)
        _ORDER[0] = out[0] if isinstance(out, (list, tuple)) else out
        return out
    return run


def _mm(a, b, form, out_dtype, name, res=None, tiles=None, a_part=(0, 1), b_part=(0, 1)):
    a_shape = (a.shape[0], a.shape[1] // a_part[1])
    b_shape = (b.shape[0], b.shape[1] // b_part[1])
    if form == "nn":
        (M, K), (K2, N) = a_shape, b_shape
    elif form == "nt":
        (M, K), (N, K2) = a_shape, b_shape
    else:
        (K, M), (K2, N) = a_shape, b_shape
    assert K == K2, (a.shape, b.shape, form)
    if tiles is None:
        tm = _pick(M, (1024, 1280, 1408, 512, 256))
        tn = _pick(N, (1280, 1024, 1408, 768, 512, 256))
        tk = K if K <= 2048 else _pick(K, (D_FF, 2560))
        if tk == D_FF:
            tm = _pick(M, (512, 256))
    else:
        tm, tn, tk = tiles
    nk = K // tk
    ah, bh = a_part[0], b_part[0]
    if form == "nn":
        a_spec = pl.BlockSpec((tm, tk), lambda i, j, k: (i, k + ah * nk))
        b_spec = pl.BlockSpec((tk, tn), lambda i, j, k: (k, j + bh * (N // tn)))
        dims = ((1,), (0,))
    elif form == "nt":
        a_spec = pl.BlockSpec((tm, tk), lambda i, j, k: (i, k + ah * nk))
        b_spec = pl.BlockSpec((tn, tk), lambda i, j, k: (j, k + bh * nk))
        dims = ((1,), (1,))
    else:
        a_spec = pl.BlockSpec((tk, tm), lambda i, j, k: (k, i + ah * (M // tm)))
        b_spec = pl.BlockSpec((tk, tn), lambda i, j, k: (k, j + bh * (N // tn)))
        dims = ((0,), (0,))
    o_spec = pl.BlockSpec((tm, tn), lambda i, j, k: (i, j))
    in_specs = [a_spec, b_spec]
    args = [a, b]
    if res is not None:
        in_specs.append(o_spec)
        args.append(res)

    def body(*refs):
        a_ref, b_ref = refs[0], refs[1]
        r_ref = refs[2] if res is not None else None
        o_ref = refs[3] if res is not None else refs[2]
        acc = refs[-1] if nk > 1 else None
        k = pl.program_id(2)
        d = lax.dot_general(a_ref[...].astype(BF), b_ref[...].astype(BF), (dims, ((), ())), preferred_element_type=F32)

        def finish(r):
            if res is not None:
                r = r + r_ref[...].astype(F32)
            o_ref[...] = r.astype(o_ref.dtype)

        if nk == 1:
            finish(d)
        else:
            @pl.when(k == 0)
            def _():
                acc[...] = d

            @pl.when((k > 0) & (k < nk - 1))
            def _():
                acc[...] += d

            @pl.when(k == nk - 1)
            def _():
                finish(acc[...] + d)

    return _pcall(
        body, grid=(M // tm, N // tn, nk), in_specs=in_specs, out_specs=o_spec,
        out_shape=SDS((M, N), out_dtype), scratch_shapes=[pltpu.VMEM((tm, tn), F32)] if nk > 1 else [],
        compiler_params=_cp(VMEM_BIG, dimension_semantics=("parallel", "parallel", "arbitrary")),
        name=name)(*args)


_MXU_COLS = 256


def _mm_fused(a, b, form, name, extras, out_dtypes, epi, tn=None, scatter=None):
    M, K = a.shape
    N = b.shape[1] if form == "nn" else b.shape[0]
    assert K <= 2048 and K == (b.shape[0] if form == "nn" else b.shape[1])
    tm = _pick(M, (1024, 512, 256))
    tn = tn or _pick(N, (1280, 1024, 1408, 512, 256))
    ne, no = len(extras), len(out_dtypes)
    bases = scatter[1] if scatter else []
    dims = ((1,), (0,)) if form == "nn" else ((1,), (1,))

    def body(*refs):
        a_ref, b_ref, ex, outs = refs[0], refs[1], refs[2:2 + ne], refs[2 + ne:2 + ne + no]
        av = a_ref[...].astype(BF)
        if scatter:
            wide_ref, stage, sem = refs[2 + ne + no:]
        for c0 in range(0, tn, _MXU_COLS):
            cols = slice(c0, min(c0 + _MXU_COLS, tn))
            bc = b_ref[:, cols] if form == "nn" else b_ref[cols, :]
            d = lax.dot_general(av, bc.astype(BF), (dims, ((), ())), preferred_element_type=F32)
            res = epi(d, *[e[:, cols] for e in ex])
            for o_ref, r in zip(outs, res[:no]):
                o_ref[:, cols] = r.astype(o_ref.dtype)
            for s, r in enumerate(res[no:]):
                stage[s, :, cols] = r.astype(BF)
        if scatter:
            rows = pl.ds(pl.multiple_of(pl.program_id(0) * tm, tm), tm)
            cps = [pltpu.make_async_copy(stage.at[s], wide_ref.at[rows, pl.ds(pl.multiple_of(base + pl.program_id(1) * tn, LANE), tn)],
                                         sem.at[s]) for s, base in enumerate(bases)]
            for cp in cps:
                cp.start()
            for cp in cps:
                cp.wait()

    a_spec = pl.BlockSpec((tm, K), lambda i, j: (i, 0))
    b_spec = pl.BlockSpec((K, tn), lambda i, j: (0, j)) if form == "nn" else pl.BlockSpec((tn, K), lambda i, j: (j, 0))
    o_spec = pl.BlockSpec((tm, tn), lambda i, j: (i, j))
    ex_specs, ex_args = [], []
    for e in extras:
        arr, off = e if isinstance(e, tuple) else (e, 0)
        ex_specs.append(pl.BlockSpec((tm, tn), lambda i, j, off=off: (i, off + j)))
        ex_args.append(arr)
    out_specs, out_shape, scratch = [o_spec] * no, [SDS((M, N), dt) for dt in out_dtypes], []
    if scatter:
        out_specs, out_shape = out_specs + [_ANY], out_shape + [scatter[0]]
        scratch = [pltpu.VMEM((len(bases), tm, tn), BF), pltpu.SemaphoreType.DMA((len(bases),))]
    return _pcall(
        body, grid=(M // tm, N // tn), in_specs=[a_spec, b_spec] + ex_specs, out_specs=out_specs, out_shape=out_shape,
        scratch_shapes=scratch, compiler_params=_cp(VMEM_BIG, dimension_semantics=("parallel", "parallel")),
        name=name)(a, b, *ex_args)


def _rms(x, w):
    return x * lax.rsqrt(jnp.mean(x * x, axis=-1, keepdims=True) + RMS_EPS) * w


def _rms_fwd(x, w, name, split=None):
    R, D = x.shape
    tr = 256
    widths = [D] if split is None else [D, split, D - split]

    def body(x_ref, w_ref, o_ref, *part_refs):
        y = _rms(x_ref[...], w_ref[...]).astype(o_ref.dtype)
        o_ref[...] = y
        if split is not None:
            part_refs[0][...] = y[:, :split]
            part_refs[1][...] = y[:, split:]

    outs = _pcall(
        body, grid=(R // tr,),
        in_specs=[pl.BlockSpec((tr, D), lambda i: (i, 0)), pl.BlockSpec((1, D), lambda i: (0, 0))],
        out_specs=[pl.BlockSpec((tr, n), lambda i: (i, 0)) for n in widths], out_shape=[SDS((R, n), BF) for n in widths],
        name=name)(x, w)
    return outs[0] if split is None else outs


def _rms_bwd(x, w, dh, dres, name, dx_dtype=BF):
    R, D = x.shape
    tr = 256
    want_dx = dres is not None
    dhs = dh if isinstance(dh, (tuple, list)) else (dh,)
    nd = len(dhs)

    def body(*refs):
        x_ref, w_ref, dh_refs = refs[0], refs[1], refs[2:2 + nd]
        if want_dx:
            dr_ref, dx_ref, dw_ref = refs[2 + nd:]
        else:
            dw_ref = refs[2 + nd]

        @pl.when(pl.program_id(0) == 0)
        def _():
            dw_ref[...] = jnp.zeros_like(dw_ref)

        dhv = jnp.concatenate([r[...].astype(F32) for r in dh_refs], axis=1) if nd > 1 else dh_refs[0][...].astype(F32)
        _, vjp = jax.vjp(_rms, x_ref[...], w_ref[...])
        dx, dw = vjp(dhv)
        dw_ref[...] += dw
        if want_dx:
            dx_ref[...] = (dx + dr_ref[...].astype(F32)).astype(dx_ref.dtype)

    row = pl.BlockSpec((tr, D), lambda i: (i, 0))
    parts = [pl.BlockSpec((tr, d.shape[1]), lambda i: (i, 0)) for d in dhs]
    vec = pl.BlockSpec((1, D), lambda i: (0, 0))
    if want_dx:
        return _pcall(body, grid=(R // tr,), in_specs=[row, vec] + parts + [row], out_specs=[row, vec],
                      out_shape=[SDS((R, D), dx_dtype), SDS((1, D), F32)], name=name)(x, w, *dhs, dres)
    return _pcall(body, grid=(R // tr,), in_specs=[row, vec] + parts, out_specs=vec,
                  out_shape=SDS((1, D), F32), name=name)(x, w, *dhs)


def _final_loss(x3, w, tgt):
    R, D = x3.shape
    tr = 256

    def loss_fn(xv, wv, tv):
        err = _rms(xv, wv) - tv
        return 0.5 * jnp.sum(jnp.mean(err * err, axis=-1))

    def body(x_ref, w_ref, t_ref, loss_ref, dx_ref, dw_ref):
        @pl.when(pl.program_id(0) == 0)
        def _():
            dw_ref[...] = jnp.zeros_like(dw_ref)
            loss_ref[...] = jnp.zeros_like(loss_ref)

        tv = t_ref[...]
        val, vjp = jax.vjp(lambda a, b: loss_fn(a, b, tv), x_ref[...], w_ref[...])
        dx, dw = vjp(jnp.ones((), F32))
        dx_ref[...] = dx.astype(BF)
        dw_ref[...] += dw
        loss_ref[...] += jnp.full(loss_ref.shape, val, F32)

    row = pl.BlockSpec((tr, D), lambda i: (i, 0))
    vec = pl.BlockSpec((1, D), lambda i: (0, 0))
    one = pl.BlockSpec((1, LANE), lambda i: (0, 0))
    return _pcall(body, grid=(R // tr,), in_specs=[row, vec, row], out_specs=[one, row, vec],
                  out_shape=[SDS((1, LANE), F32), SDS((R, D), BF), SDS((1, D), F32)], name="final_loss")(x3, w, tgt)


NEG = -1e30


def _att_blocks(S, d):
    L = S // d
    nb = L // LANE
    return nb, (2 if nb > 1 else 1)


def _att_rows(S, d, idx):
    nb, nkb = _att_blocks(S, d)
    r = idx // nb
    n = idx % nb
    kb = jnp.maximum(n - 1, 0)
    if d == 1:
        qrows = pl.ds(pl.multiple_of(n * LANE, LANE), LANE)
        krows = pl.ds(pl.multiple_of(kb * LANE, LANE), LANE * nkb)
    else:
        qrows = pl.ds(r + n * (LANE * d), LANE, stride=d)
        krows = pl.ds(r + kb * (LANE * d), LANE * nkb, stride=d)
    qpos = n * LANE + lax.broadcasted_iota(jnp.int32, (LANE, LANE * nkb), 0)
    kpos = kb * LANE + lax.broadcasted_iota(jnp.int32, (LANE, LANE * nkb), 1)
    return qrows, krows, qpos - kpos


def _slab(col):
    return pl.BlockSpec((SEQ, LANE), lambda h, col=col: (0, col + h))


def _widen(slabs, wide):
    for t, ref in enumerate(slabs):
        wide[t] = ref[...].astype(F32)
    return tuple(tuple(wide.at[3 * kind + g] for g in range(3)) for kind in range(3))


def _attn_fwd(proj):
    S = proj.shape[0]
    scale = HEAD_DIM ** -0.5

    def body(q0, q1, q2, k0, k1, k2, v0, v1, v2, o_ref, lse_ref, og, lg, wide):
        qs, ks, vs = _widen((q0, q1, q2, k0, k1, k2, v0, v1, v2), wide)
        for g, (window, d) in enumerate(ATT_GROUPS):
            def blk(idx, carry, g=g, window=window, d=d):
                qrows, krows, dist = _att_rows(S, d, idx)
                q = qs[g][qrows, :].astype(BF)
                k = ks[g][krows, :].astype(BF)
                v = vs[g][krows, :].astype(BF)
                s = lax.dot_general(q, k, (((1,), (1,)), ((), ())), preferred_element_type=F32) * scale
                s = jnp.where((dist >= 0) & (dist <= window // d), s, NEG)
                m = jnp.max(s, axis=-1, keepdims=True)
                p = jnp.exp(s - m)
                l = jnp.sum(p, axis=-1, keepdims=True)
                o = jnp.dot((p / l).astype(BF), v, preferred_element_type=F32)
                og[g, qrows, :] = o
                lg[g, qrows, :] = jnp.broadcast_to(m + jnp.log(l), (LANE, LANE))
                return carry
            lax.fori_loop(0, S // LANE, blk, 0, unroll=True)

        def merge(t, carry):
            rows = pl.ds(pl.multiple_of(t * 256, 256), 256)
            l0, l1, l2 = lg[0, rows, :], lg[1, rows, :], lg[2, rows, :]
            m = jnp.maximum(jnp.maximum(l0, l1), l2)
            w0, w1, w2 = jnp.exp(l0 - m), jnp.exp(l1 - m), jnp.exp(l2 - m)
            den = w0 + w1 + w2
            o_ref[rows, :] = (w0 * og[0, rows, :] + w1 * og[1, rows, :] + w2 * og[2, rows, :]) / den
            lse_ref[rows, :] = m + jnp.log(den)
            return carry
        lax.fori_loop(0, S // 256, merge, 0)

    ins = [_slab(QA0 + 4 * g) for g in range(3)] + [_slab(KA0 + 4 * g) for g in range(3)] + [_slab(VA0 + 4 * g) for g in range(3)]
    out = pl.BlockSpec((SEQ, LANE), lambda h: (0, h))
    return _pcall(
        body, grid=(ATT_HEADS,), in_specs=ins, out_specs=[out, out],
        out_shape=[SDS((S, ATT_OUT), F32), SDS((S, ATT_OUT), F32)],
        scratch_shapes=[pltpu.VMEM((3, S, LANE), F32), pltpu.VMEM((3, S, LANE), F32), pltpu.VMEM((9, S, LANE), F32)],
        compiler_params=_cp(VMEM_BIG), name="attn_fwd")(*([proj] * 9))


def _attn_bwd(proj, o_att, lse, do_att, dproj):
    S = proj.shape[0]
    scale = HEAD_DIM ** -0.5

    def body(q0, q1, q2, k0, k1, k2, v0, v1, v2, o_ref, lse_ref, do_ref, dp_in, dp_ref, dqa, dka, dva, delta, stage, sem, wide):
        qs, ks, vs = _widen((q0, q1, q2, k0, k1, k2, v0, v1, v2), wide)
        head = pl.program_id(0)

        def slab_copies(g):
            return [pltpu.make_async_copy(stage.at[t], dp_ref.at[:, pl.ds(pl.multiple_of((base + 4 * g + head) * LANE, LANE), LANE)],
                                          sem.at[t]) for t, base in enumerate((QA0, KA0, VA0))]

        delta[...] = jnp.broadcast_to(jnp.sum(do_ref[...] * o_ref[...], axis=-1, keepdims=True), delta.shape)
        for g, (window, d) in enumerate(ATT_GROUPS):
            dka[...] = jnp.zeros_like(dka)
            dva[...] = jnp.zeros_like(dva)

            def blk(idx, carry, g=g, window=window, d=d):
                qrows, krows, dist = _att_rows(S, d, idx)
                q = qs[g][qrows, :].astype(BF)
                k = ks[g][krows, :].astype(BF)
                v = vs[g][krows, :].astype(BF)
                do = do_ref[qrows, :]
                s = lax.dot_general(q, k, (((1,), (1,)), ((), ())), preferred_element_type=F32) * scale
                p = jnp.exp(s - lse_ref[qrows, :][:, 0:1])
                p = jnp.where((dist >= 0) & (dist <= window // d), p, 0.0)
                dob = do.astype(BF)
                dva[krows, :] += lax.dot_general(p.astype(BF), dob, (((0,), (0,)), ((), ())), preferred_element_type=F32)
                dp = lax.dot_general(dob, v, (((1,), (1,)), ((), ())), preferred_element_type=F32)
                ds = (p * (dp - delta[qrows, :][:, 0:1]) * scale).astype(BF)
                dqa[qrows, :] = jnp.dot(ds, k, preferred_element_type=F32)
                dka[krows, :] += lax.dot_general(ds, q, (((0,), (0,)), ((), ())), preferred_element_type=F32)
                return carry
            lax.fori_loop(0, S // LANE, blk, 0, unroll=True)
            if g > 0:
                for cp in slab_copies(g - 1):
                    cp.wait()
            stage[0] = dqa[...].astype(BF)
            stage[1] = dka[...].astype(BF)
            stage[2] = dva[...].astype(BF)
            for cp in slab_copies(g):
                cp.start()
        for cp in slab_copies(len(ATT_GROUPS) - 1):
            cp.wait()

    cols = [QA0 + 4 * g for g in range(3)] + [KA0 + 4 * g for g in range(3)] + [VA0 + 4 * g for g in range(3)]
    ins = [_slab(c) for c in cols]
    one = pl.BlockSpec((SEQ, LANE), lambda h: (0, h))
    return _pcall(
        body, grid=(ATT_HEADS,), in_specs=ins + [one, one, one, _ANY], out_specs=_ANY,
        out_shape=SDS((S, IN_WIDTH), BF), aliases={12: 0},
        scratch_shapes=[pltpu.VMEM((S, LANE), F32)] * 4 + [pltpu.VMEM((3, S, LANE), BF), pltpu.SemaphoreType.DMA((3,)),
                                                           pltpu.VMEM((9, S, LANE), F32)],
        compiler_params=_cp(VMEM_BIG), name="attn_bwd")(*([proj] * 9), o_att, lse, do_att, dproj)


def _bdot(a, b, dims):
    return lax.dot_general(a.astype(BF), b.astype(BF), (dims, ((), ())), preferred_element_type=F32)


def _to_heads(x):
    return jnp.concatenate([x[None, :, h * LANE:(h + 1) * LANE] for h in range(HG_HEADS)], axis=0)


def _from_heads(x):
    return jnp.concatenate([x[h] for h in range(HG_HEADS)], axis=1)


def _hdot(a, b, ca, cb):
    return lax.dot_general(a.astype(BF), b.astype(BF), (((ca,), (cb,)), ((0,), (0,))), preferred_element_type=F32)


def _hg_chunk_all(qh, fh, ih, gh, lbraw, nw, st):
    C = HG_CHUNK
    mx = jnp.maximum(lbraw[0:1], lbraw[1:2])
    e0 = jnp.exp(lbraw[0:1] - mx)
    e1 = jnp.exp(lbraw[1:2] - mx)
    lb = e0 / (e0 + e1)
    f = lb + (1.0 - lb) * jax.nn.sigmoid(fh)
    logf = jnp.log(f)
    k = 1.0 - f
    q = qh * jax.nn.sigmoid(qh)
    causal = lax.broadcasted_iota(jnp.int32, (C, C), 0) >= lax.broadcasted_iota(jnp.int32, (C, C), 1)
    b = lax.dot_general(causal.astype(F32), logf, (((1,), (0,)), ((), ())), precision=lax.Precision.HIGHEST,
                        preferred_element_type=F32)
    b_last = b[C - 1:C]
    b_ref = b[C // 2:C // 2 + 1]
    v3 = _to_heads(ih)
    st_out = st * _to_heads(jnp.exp(b_last)) + _hdot(v3, _to_heads(k * jnp.exp(b_last - b)), 1, 1)
    inter = _hdot(_to_heads(q * jnp.exp(b)), st, 2, 2)
    a = _hdot(_to_heads(q * jnp.exp(b - b_ref)), _to_heads(k * jnp.exp(b_ref - b)), 2, 2)
    a = jnp.where(causal[None], a, 0.0)
    o = inter + _hdot(a, v3, 2, 1)
    o = o * lax.rsqrt(jnp.mean(o * o, axis=-1, keepdims=True) + RMS_EPS) * nw
    return _from_heads(o) * (gh * jax.nn.sigmoid(gh)), st_out


HG_STEP = 4
HG_ROWS = HG_STEP * HG_CHUNK


def _hg_in_specs(step_of):
    half = 4 * LANE
    specs = []
    for base in (QH0, FH0, IH0, GH0):
        for part in range(2):
            specs.append(pl.BlockSpec((HG_ROWS, half), lambda n, col=base // 4 + part: (step_of(n), col)))
    return specs


def _hg_sections(sec, c):
    rows = slice(c * HG_CHUNK, (c + 1) * HG_CHUNK)
    return [jnp.concatenate([sec[2 * s][rows, :], sec[2 * s + 1][rows, :]], axis=1).astype(F32) for s in range(4)]


def _hg_fwd(proj, lb, nw):
    S = proj.shape[0]
    C = HG_CHUNK
    NC = S // C

    def body(*refs):
        sec = refs[:8]
        lb_ref, nw_ref, o_ref, st_ref, st_scr = refs[8:]

        @pl.when(pl.program_id(0) == 0)
        def _():
            st_scr[...] = jnp.zeros_like(st_scr)

        st = st_scr[...]
        for c in range(HG_STEP):
            st_ref[:, c] = st
            o, st = _hg_chunk_all(*_hg_sections(sec, c), lb_ref[...], nw_ref[...], st)
            o_ref[c * C:(c + 1) * C, :] = o.astype(o_ref.dtype)
        st_scr[...] = st

    return _pcall(
        body, grid=(NC // HG_STEP,),
        in_specs=_hg_in_specs(lambda n: n) + [pl.BlockSpec((2, HG_WIDTH), lambda n: (0, 0)), pl.BlockSpec((1, LANE), lambda n: (0, 0))],
        out_specs=[pl.BlockSpec((HG_ROWS, HG_WIDTH), lambda n: (n, 0)),
                   pl.BlockSpec((HG_HEADS, HG_STEP, LANE, LANE), lambda n: (0, n, 0, 0))],
        out_shape=[SDS((S, HG_WIDTH), BF), SDS((HG_HEADS, NC, LANE, LANE), F32)],
        scratch_shapes=[pltpu.VMEM((HG_HEADS, LANE, LANE), F32)], name="hg_fwd")(*([proj] * 8), lb, nw)


def _hg_bwd(proj, lb, nw, states, do_hg, dproj):
    S = proj.shape[0]
    C = HG_CHUNK
    NC = S // C

    def body(*refs):
        sec = refs[:8]
        lb_ref, nw_ref, st_ref, do_ref, dp_in, dp_ref, dlb_ref, dnw_ref, dst_scr, stage, sem = refs[8:]
        n = pl.program_id(0)
        slot = n % 2

        def step_copies(s):
            rows = pl.ds(pl.multiple_of((NS - 1 - n) * HG_ROWS, HG_ROWS), HG_ROWS)
            return [pltpu.make_async_copy(stage.at[s, t], dp_ref.at[rows, pl.ds((QH0 + 8 * t) * LANE, HG_WIDTH)], sem.at[s, t])
                    for t in range(4)]

        @pl.when(n == 0)
        def _():
            dst_scr[...] = jnp.zeros_like(dst_scr)
            dlb_ref[...] = jnp.zeros_like(dlb_ref)
            dnw_ref[...] = jnp.zeros_like(dnw_ref)

        @pl.when(n >= 2)
        def _():
            for cp in step_copies(slot):
                cp.wait()

        dst = dst_scr[...]
        for c in reversed(range(HG_STEP)):
            rows = slice(c * C, (c + 1) * C)
            _, vjp = jax.vjp(_hg_chunk_all, *_hg_sections(sec, c), lb_ref[...], nw_ref[...], st_ref[:, c])
            dq, df, di, dg, dlb, dnw, dst = vjp((do_ref[rows, :].astype(F32), dst))
            dlb_ref[...] += dlb
            dnw_ref[...] += dnw
            for t, val in enumerate((dq, df, di, dg)):
                stage[slot, t, rows, :] = val.astype(BF)
        dst_scr[...] = dst
        for cp in step_copies(slot):
            cp.start()

        @pl.when(n == NS - 1)
        def _():
            for cp in step_copies(1 - slot) + step_copies(slot):
                cp.wait()

    NS = NC // HG_STEP
    rev = lambda n: NS - 1 - n
    ospec = pl.BlockSpec((HG_ROWS, HG_WIDTH), lambda n: (rev(n), 0))
    lbs = pl.BlockSpec((2, HG_WIDTH), lambda n: (0, 0))
    nws = pl.BlockSpec((1, LANE), lambda n: (0, 0))
    outs = _pcall(
        body, grid=(NS,),
        in_specs=_hg_in_specs(rev) + [lbs, nws, pl.BlockSpec((HG_HEADS, HG_STEP, LANE, LANE), lambda n: (0, rev(n), 0, 0)), ospec, _ANY],
        out_specs=[_ANY, lbs, nws],
        out_shape=[SDS((S, IN_WIDTH), BF), SDS((2, HG_WIDTH), F32), SDS((1, LANE), F32)], aliases={12: 0},
        scratch_shapes=[pltpu.VMEM((HG_HEADS, LANE, LANE), F32), pltpu.VMEM((2, 4, HG_ROWS, HG_WIDTH), BF),
                        pltpu.SemaphoreType.DMA((2, 4))],
        name="hg_bwd")(*([proj] * 8), lb, nw, states, do_hg, dproj)
    return outs


def _merge(ga, gb, ba, bb):
    return jax.nn.sigmoid(ga) * ba + jax.nn.sigmoid(gb) * bb


_MT = 512


def _swiglu(a, b):
    return a * jax.nn.sigmoid(a) * b


def _cross(q, k, v):
    s = _bdot(q, k, ((1,), (1,))) * (HEAD_DIM ** -0.5)
    e = jnp.exp(s - jnp.max(s, axis=-1, keepdims=True))
    p = e / jnp.sum(e, axis=-1, keepdims=True)
    return _bdot(p, v, ((1,), (0,)))


def _cross_specs():
    q = pl.BlockSpec((_MT, LANE), lambda h, i: (i, h))
    k = pl.BlockSpec((MEM_LEN, LANE), lambda h, i: (0, h))
    v = pl.BlockSpec((MEM_LEN, LANE), lambda h, i: (0, CROSS_HEADS + h))
    return q, k, v


def _cross_fwd(qc, kvc):
    S = qc.shape[0]
    q, k, v = _cross_specs()

    def body(q_ref, k_ref, v_ref, o_ref):
        o_ref[...] = _cross(q_ref[...], k_ref[...], v_ref[...]).astype(o_ref.dtype)

    return _pcall(body, grid=(CROSS_HEADS, S // _MT), in_specs=[q, k, v], out_specs=q,
                          out_shape=SDS((S, CROSS_WIDTH), BF), name="cross_fwd")(qc, kvc, kvc)


def _cross_bwd(qc, kvc, doc):
    S = qc.shape[0]
    q, k, v = _cross_specs()

    def body(q_ref, k_ref, v_ref, do_ref, dq_ref, dk_ref, dv_ref):
        @pl.when(pl.program_id(1) == 0)
        def _():
            dk_ref[...] = jnp.zeros_like(dk_ref)
            dv_ref[...] = jnp.zeros_like(dv_ref)

        _, vjp = jax.vjp(_cross, q_ref[...], k_ref[...], v_ref[...])
        dq, dk, dv = vjp(do_ref[...])
        dq_ref[...] = dq.astype(dq_ref.dtype)
        dk_ref[...] += dk
        dv_ref[...] += dv

    return _pcall(body, grid=(CROSS_HEADS, S // _MT), in_specs=[q, k, v, q], out_specs=[q, k, k],
                          out_shape=[SDS((S, CROSS_WIDTH), BF), SDS((MEM_LEN, CROSS_WIDTH), F32), SDS((MEM_LEN, CROSS_WIDTH), F32)],
                          name="cross_bwd")(qc, kvc, kvc, doc)


def _local_step(x, mem, tgt, p, ready):
    h, h_a, h_b = _rms_fwd(x, p["ln_mix"], "rms_mix", split=IN_SPLIT)
    proj = _mm(h_a, p["w_in_a"], "nt", BF, "mm_proj_a")
    proj = _mm(h_b, p["w_in_b"], "nt", BF, "mm_proj_b", res=proj)
    o_att, lse = _attn_fwd(proj)
    o_hg, states = _hg_fwd(proj, p["hg_lb"], p["hg_norm"])
    ba = _mm(o_att, p["w_ba"], "nn", BF, "mm_ba")
    gates = [(proj, GA0 * LANE // _MT), (proj, GB0 * LANE // _MT)]

    def f32s(*vals):
        return [v.astype(F32) for v in vals]

    bb, merged = _mm_fused(o_hg, p["w_bb"], "nn", "mm_bb_merge", gates + [ba], [BF, BF], tn=_MT,
                           epi=lambda d, ga, gb, bav: (d, _merge(*f32s(ga, gb, bav), d)))
    x1 = _mm(merged, p["w_out"], "nn", F32, "mm_out", res=x)
    hc = _rms_fwd(x1, p["ln_cross"], "rms_cross")
    qc = _mm(hc, p["wq"], "nn", F32, "mm_q")
    mn = _rms_fwd(mem, p["ln_mem"], "rms_mem")
    kvc = _mm(mn, p["wkv"], "nn", F32, "mm_kv")
    oc = _cross_fwd(qc, kvc)
    x2 = _mm(oc, p["wo"], "nn", F32, "mm_o", res=x1)
    hf = _rms_fwd(x2, p["ln_ffn"], "rms_ffn")
    a = _mm(hf, p["w1"], "nt", BF, "mm_w1")
    b, g = _mm_fused(hf, p["w3"], "nt", "mm_w3_swiglu", [a], [BF, BF], lambda d, av: (d, _swiglu(av.astype(F32), d)))
    x3 = _mm(g, p["w2"], "nn", F32, "mm_w2", res=x2)
    loss, dx3b, d_ln_final = _final_loss(x3, p["ln_final"], tgt)

    gw, gs = {}, {"ln_final": d_ln_final}
    da, db = _mm_fused(dx3b, p["w2"], "nt", "mm_dg_swiglu", [a, b], [BF, BF],
                       lambda d, av, bv: jax.vjp(_swiglu, av.astype(F32), bv.astype(F32))[1](d))
    gw["w2"] = _mm(g, dx3b, "tn", BF, "mm_dw2")
    dhf = _mm(da, p["w1"], "nn", BF, "mm_dhf1")
    dhf = _mm(db, p["w3"], "nn", BF, "mm_dhf3", res=dhf)
    gw["w1"] = _mm(da, hf, "tn", BF, "mm_dw1")
    gw["w3"] = _mm(db, hf, "tn", BF, "mm_dw3")
    ready("ffn", gw)
    dx2b, gs["ln_ffn"] = _rms_bwd(x2, p["ln_ffn"], dhf, dx3b, "rms_ffn_bwd")
    doc = _mm(dx2b, p["wo"], "nt", F32, "mm_doc")
    gw["wo"] = _mm(oc, dx2b, "tn", BF, "mm_dwo")
    dqc, dk, dv = _cross_bwd(qc, kvc, doc)
    dkvc = jnp.concatenate([dk, dv], axis=1)
    dhc = _mm(dqc, p["wq"], "nt", BF, "mm_dhc")
    gw["wq"] = _mm(hc, dqc, "tn", BF, "mm_dwq")
    dmn = _mm(dkvc, p["wkv"], "nt", F32, "mm_dmn")
    gw["wkv"] = _mm(mn, dkvc, "tn", BF, "mm_dwkv")
    ready("cross", gw)
    gs["ln_mem"] = _rms_bwd(mem, p["ln_mem"], dmn, None, "rms_mem_bwd")
    dx1b, gs["ln_cross"] = _rms_bwd(x1, p["ln_cross"], dhc, dx2b, "rms_cross_bwd")
    def merge_back(d, ga, gb, bav, bbv):
        dga, dgb, dba_, dbb_ = jax.vjp(_merge, *f32s(ga, gb, bav, bbv))[1](d)
        return dba_, dbb_, dga, dgb

    dba, dbb, dproj = _mm_fused(dx1b, p["w_out"], "nt", "mm_dmerged_merge", gates + [ba, bb], [BF, BF], merge_back, tn=_MT,
                                scatter=(SDS((x.shape[0], IN_WIDTH), BF), [GA0 * LANE, GB0 * LANE]))
    gw["w_out"] = _mm(merged, dx1b, "tn", BF, "mm_dwout")
    do_att = _mm(dba, p["w_ba"], "nt", F32, "mm_doatt")
    gw["w_ba"] = _mm(o_att, dba, "tn", BF, "mm_dwba")
    do_hg = _mm(dbb, p["w_bb"], "nt", BF, "mm_dohg")
    gw["w_bb"] = _mm(o_hg, dbb, "tn", BF, "mm_dwbb")
    ready("mix", gw)
    dproj = _attn_bwd(proj, o_att, lse, do_att, dproj)
    dproj, gs["hg_lb"], gs["hg_norm"] = _hg_bwd(proj, p["hg_lb"], p["hg_norm"], states, do_hg, dproj)
    for q in range(IN_PARTS):
        gw[f"w_in_q{q}"] = _mm(dproj, h, "tn", BF, f"mm_dwin_q{q}", b_part=(q, IN_PARTS))
        ready(f"in_q{q}", gw)
        if q == 0:
            ready("mixers_done", gw)
    dh_a = _mm(dproj, p["w_in_a"], "nn", BF, "mm_dh_a")
    ready("dh_half", gw)
    dh = (dh_a, _mm(dproj, p["w_in_b"], "nn", BF, "mm_dh_b"))
    grad_x, gs["ln_mix"] = _rms_bwd(x, p["ln_mix"], dh, dx1b, "rms_mix_bwd", dx_dtype=F32)
    return loss, grad_x, gs


IN_SPLIT = 1280
_MATS = (
    ("w_in_a", "w_in", IN_WIDTH, IN_SPLIT, "row", IN_WIDTH // N_DEV, True),
    ("w_in_b", "w_in", IN_WIDTH, D_MODEL - IN_SPLIT, "row", IN_WIDTH // N_DEV, True),
    ("w_ba", "w_branch_a", ATT_OUT, D_MODEL, "col", D_MODEL // N_DEV, False),
    ("w_bb", "w_branch_b", HG_WIDTH, D_MODEL, "col", D_MODEL // N_DEV, False),
    ("w_out", "w_out", D_MODEL, D_MODEL, "row", D_MODEL // N_DEV, False),
    ("wq", "wq_cross", D_MODEL, CROSS_WIDTH, "row", D_MODEL // N_DEV, False),
    ("wkv", "wkv_cross", D_MODEL, 2 * CROSS_WIDTH, "row", D_MODEL // N_DEV, False),
    ("wo", "wo_cross", CROSS_WIDTH, D_MODEL, "col", D_MODEL // N_DEV, False),
    ("w1", "w1", D_FF, D_MODEL, "row", D_FF // N_DEV, True),
    ("w3", "w3", D_FF, D_MODEL, "row", D_FF // N_DEV, True),
    ("w2", "w2", D_FF, D_MODEL, "row", D_FF // N_DEV, False),
    ("w_in_q0", "w_in", IN_WIDTH, D_MODEL // 2, "row", IN_WIDTH // N_DEV, True),
    ("w_in_q1", "w_in", IN_WIDTH, D_MODEL // 2, "row", IN_WIDTH // N_DEV, True),
)
IN_PARTS = 2
_INDEX = {m[0]: i for i, m in enumerate(_MATS)}


def _unit_shape(i):
    _, _, K, N, kind, sh, _ = _MATS[i]
    return (sh, N) if kind == "row" else (K, sh)


def _win(ref, i, chip, par, half=None):
    _, _, K, _, kind, sh, _ = _MATS[i]
    if kind == "row":
        rows, start = (sh, 0) if half is None else (sh // 2, half * (sh // 2))
        return ref.at[pl.ds(pl.multiple_of((2 * chip + par) * sh + start, 16), rows), :]
    rows, start = (K, 0) if half is None else (K // 2, half * (K // 2))
    return ref.at[pl.ds(start, rows), pl.ds(pl.multiple_of((2 * chip + par) * sh, LANE), sh)]


def _my_place():
    x, y, c = lax.axis_index("x"), lax.axis_index("y"), lax.axis_index("c")
    chips = [(1 - x, y), (x, 1 - y), (1 - x, 1 - y)]
    return x, y, c, chips


def _cast_unit(w, name, cols=None):
    K, N = w.shape
    c0, c1 = cols or (0, N)
    tr = _pick(K, (256, 400, 352))

    def body(w_ref, o_ref):
        o_ref[...] = w_ref[:, c0:c1].astype(BF)

    src = pl.BlockSpec((tr, N), lambda i: (i, 0))
    dst = pl.BlockSpec((tr, c1 - c0), lambda i: (i, 0))
    return _pcall(body, grid=(K // tr,), in_specs=[src], out_specs=dst, out_shape=SDS((K, c1 - c0), BF), name=name)(w)


def _handshake(peers):
    barrier = pltpu.get_barrier_semaphore()
    for peer in peers:
        pl.semaphore_signal(barrier, inc=1, device_id=peer, device_id_type=MESH)
    pl.semaphore_wait(barrier, len(peers))


def _sequencer(body, out_type, sems, cid, name):
    return pl.kernel(body, out_type=out_type, mesh=plsc.ScalarSubcoreMesh(axis_name="seq", num_cores=1),
                     scratch_types=sems, compiler_params=pltpu.CompilerParams(collective_id=cid), name=name)


def _all_gather(units, idx, cid, name):
    n = len(idx)

    def body(*refs):
        us, fulls = refs[:n], refs[n:2 * n]
        ssem, rsem, lsem = refs[2 * n:]
        x, y, c, _ = _my_place()
        xn, yn, dg = (1 - x, y), (x, 1 - y), (1 - x, 1 - y)
        me, sib = (x, y, c), (x, y, 1 - c)
        _handshake([sib, (*xn, c), (*yn, c)])

        def win(i, chip_xy, par, half=None):
            return _win(fulls[i], idx[i], 2 * chip_xy[0] + chip_xy[1], par, half)

        def rcopy(i, k, src, dst, to):
            return pltpu.make_async_remote_copy(src_ref=src, dst_ref=dst, send_sem=ssem.at[i, k], recv_sem=rsem.at[i, k],
                                                device_id=to, device_id_type=MESH)

        def landed(i, k, blk):
            rcopy(i, k, blk, blk, me).wait_recv()

        sends, locs = [], []

        def send(i, k, blk, to, src=None):
            cp = rcopy(i, k, blk if src is None else src, blk, to)
            cp.start()
            sends.append(cp)

        for i in range(n):
            mine = win(i, (x, y), c)
            loc = pltpu.make_async_copy(us[i], mine, lsem.at[i])
            loc.start()
            locs.append(loc)
            send(i, 0, mine, sib, src=us[i])
            send(i, 1, mine, (*xn, c), src=us[i])
            send(i, 2, mine, (*yn, c), src=us[i])
        for i in range(n):
            landed(i, 1, win(i, xn, c))
            send(i, 3, win(i, xn, c, 0), (*yn, c))
            send(i, 5, win(i, xn, c), sib)
            landed(i, 2, win(i, yn, c))
            send(i, 4, win(i, yn, c, 1), (*xn, c))
            send(i, 6, win(i, yn, c), sib)
        for i in range(n):
            landed(i, 3, win(i, dg, c, 0))
            landed(i, 4, win(i, dg, c, 1))
            send(i, 7, win(i, dg, c), sib)
        for i in range(n):
            landed(i, 0, win(i, (x, y), 1 - c))
            landed(i, 5, win(i, xn, 1 - c))
            landed(i, 6, win(i, yn, 1 - c))
            landed(i, 7, win(i, dg, 1 - c))
        for cp in sends:
            cp.wait_send()
        for loc in locs:
            loc.wait()

    out_type = [SDS((_MATS[i][2], _MATS[i][3]), BF) for i in idx]
    sems = [pltpu.SemaphoreType.DMA((n, 8)), pltpu.SemaphoreType.DMA((n, 8)), pltpu.SemaphoreType.DMA((n,))]
    return _sequencer(body, out_type, sems, cid, name)(*units)


def _rs_sibling(grads, idx, cid, name):
    n = len(idx)

    def body(*refs):
        gs, bufs = refs[:n], refs[n:2 * n]
        ssem, rsem = refs[2 * n:]
        x, y, c, _ = _my_place()
        _handshake([(x, y, 1 - c)])
        cps = []
        for i in range(n):
            for k in range(4):
                cp = pltpu.make_async_remote_copy(
                    src_ref=_win(gs[i], idx[i], k, 1 - c), dst_ref=bufs[i].at[k], send_sem=ssem.at[i, k], recv_sem=rsem.at[i, k],
                    device_id=(x, y, 1 - c), device_id_type=MESH)
                cp.start()
                cps.append(cp)
        for cp in cps:
            cp.wait()

    out_type = [SDS((4, *_unit_shape(i)), BF) for i in idx]
    sems = [pltpu.SemaphoreType.DMA((n, 4)), pltpu.SemaphoreType.DMA((n, 4))]
    return _sequencer(body, out_type, sems, cid, name)(*grads)


def _sibling_sum_tc(grad, i, cid, name):
    unit = _unit_shape(i)

    def body(g_ref, t_ref, theirs, mine, summed, ssem, rsem, lsem, osem):
        x, y, c, _ = _my_place()
        _handshake([(x, y, 1 - c)])
        sends, loads = [], []
        for k in range(4):
            sends.append(pltpu.make_async_remote_copy(
                src_ref=_win(g_ref, i, k, 1 - c), dst_ref=theirs.at[k], send_sem=ssem.at[k], recv_sem=rsem.at[k],
                device_id=(x, y, 1 - c), device_id_type=MESH))
            loads.append(pltpu.make_async_copy(_win(g_ref, i, k, c), mine.at[k], lsem.at[k]))
            sends[-1].start()
            loads[-1].start()
        stores = []
        for k in range(4):
            loads[k].wait()
            sends[k].wait_recv()
            summed[k] = (mine[k].astype(F32) + theirs[k].astype(F32)).astype(BF)
            stores.append(pltpu.make_async_copy(summed.at[k], t_ref.at[k], osem.at[k]))
            stores[-1].start()
        for k in range(4):
            sends[k].wait_send()
            stores[k].wait()

    block = pltpu.VMEM((4, *unit), BF)
    return _pcall(body, in_specs=[_ANY], out_specs=_ANY, out_shape=SDS((4, *unit), BF),
                  scratch_shapes=[block, block, block] + [pltpu.SemaphoreType.DMA((4,))] * 4,
                  compiler_params=_cp(VMEM_BIG, collective_id=cid), name=name)(grad)


def _chip_sum(g, sib, i, place, name):
    _, _, K, N, kind, sh, _ = _MATS[i]

    def body(pref, g_ref, s_ref, o_ref):
        o_ref[...] = (g_ref[...].astype(F32) + s_ref[...].astype(F32)).astype(BF)

    if kind == "row":
        tr = _pick(sh, (800, 352, 256))
        per = sh // tr
        grid = (4, per)
        g_spec = pl.BlockSpec((tr, N), lambda k, r, pref: ((2 * k + pref[0]) * per + r, 0))
        u_spec = pl.BlockSpec((None, tr, N), lambda k, r, pref: (k, r, 0))
    else:
        grid = (4,)
        g_spec = pl.BlockSpec((K, sh), lambda k, pref: (0, 2 * k + pref[0]))
        u_spec = pl.BlockSpec((None, K, sh), lambda k, pref: (k, 0, 0))
    return _pcall(body, grid=grid, in_specs=[g_spec, u_spec], out_specs=u_spec, out_shape=SDS((4, *_unit_shape(i)), BF),
                  prefetch=True, name=name)(place, g, sib)


def _rs_chips(parts, idx, cid, name):
    n = len(idx)

    def body(*refs):
        ts, bufs, relays = refs[:n], refs[n:2 * n], refs[2 * n:3 * n]
        ssem, rsem = refs[3 * n:]
        x, y, c, _ = _my_place()
        xn, yn, dg = (1 - x, y), (x, 1 - y), (1 - x, 1 - y)
        to_x, to_y, me = (*xn, c), (*yn, c), (x, y, c)
        _handshake([to_x, to_y])

        def rcopy(i, k, src, dst, to):
            return pltpu.make_async_remote_copy(src_ref=src, dst_ref=dst, send_sem=ssem.at[i, k], recv_sem=rsem.at[i, k],
                                                device_id=to, device_id_type=MESH)

        def halves(i):
            rows = _unit_shape(idx[i])[0] // 2
            return [pl.ds(0, rows), pl.ds(rows, rows)]

        sends = []

        def send(cp):
            cp.start()
            sends.append(cp)

        for i in range(n):
            h0, h1 = halves(i)
            diag = ts[i].at[2 * dg[0] + dg[1]]
            send(rcopy(i, 0, ts[i].at[2 * xn[0] + xn[1]], bufs[i].at[0], to_x))
            send(rcopy(i, 1, ts[i].at[2 * yn[0] + yn[1]], bufs[i].at[1], to_y))
            send(rcopy(i, 2, diag.at[h0], relays[i].at[0], to_x))
            send(rcopy(i, 3, diag.at[h1], relays[i].at[1], to_y))
        for i in range(n):
            h0, h1 = halves(i)
            rcopy(i, 2, relays[i].at[0], relays[i].at[0], me).wait_recv()
            send(rcopy(i, 4, relays[i].at[0], bufs[i].at[2, h0], to_y))
            rcopy(i, 3, relays[i].at[1], relays[i].at[1], me).wait_recv()
            send(rcopy(i, 5, relays[i].at[1], bufs[i].at[2, h1], to_x))
        for i in range(n):
            h0, h1 = halves(i)
            rcopy(i, 0, bufs[i].at[0], bufs[i].at[0], me).wait_recv()
            rcopy(i, 1, bufs[i].at[1], bufs[i].at[1], me).wait_recv()
            rcopy(i, 4, bufs[i].at[2, h0], bufs[i].at[2, h0], me).wait_recv()
            rcopy(i, 5, bufs[i].at[2, h1], bufs[i].at[2, h1], me).wait_recv()
        for cp in sends:
            cp.wait_send()

    def relay_shape(i):
        rows, cols = _unit_shape(i)
        return (2, rows // 2, cols)

    out_type = [SDS((3, *_unit_shape(i)), BF) for i in idx] + [SDS(relay_shape(i), BF) for i in idx]
    sems = [pltpu.SemaphoreType.DMA((n, 6)), pltpu.SemaphoreType.DMA((n, 6))]
    return _sequencer(body, out_type, sems, cid, name)(*parts)[:n]


def _adamw(w, g, m, v):
    m = ADAM_B1 * m + (1.0 - ADAM_B1) * g
    v = ADAM_B2 * v + (1.0 - ADAM_B2) * jnp.square(g)
    m_hat = m / (1.0 - ADAM_B1 ** ADAM_STEP)
    v_hat = v / (1.0 - ADAM_B2 ** ADAM_STEP)
    delta = -ADAM_LR * (m_hat / (jnp.sqrt(v_hat) + ADAM_EPS) + ADAM_WD * w)
    return delta, m, v


def _adam_mat(parts, recvs, w, m, v, place, name):
    R, C = w.shape
    n = len(parts)
    tr = _pick(R, (256, 160, 176))

    def body(pref, *refs):
        t_refs, r_refs = refs[:n], refs[n:2 * n]
        w_ref, m_ref, v_ref, g_out, d_out, m_out, v_out = refs[2 * n:]
        cols = [t[...].astype(F32) + r[0].astype(F32) + r[1].astype(F32) + r[2].astype(F32) for t, r in zip(t_refs, r_refs)]
        g = jnp.concatenate(cols, axis=1) if n > 1 else cols[0]
        d, mn, vn = _adamw(w_ref[...], g, m_ref[...], v_ref[...])
        g_out[...] = g
        d_out[...] = d
        m_out[...] = mn
        v_out[...] = vn

    t_spec = pl.BlockSpec((None, tr, C // n), lambda i, pref: (pref[1], i, 0))
    r_spec = pl.BlockSpec((3, tr, C // n), lambda i, pref: (0, i, 0))
    s_spec = pl.BlockSpec((tr, C), lambda i, pref: (i, 0))
    return _pcall(body, grid=(R // tr,), in_specs=[t_spec] * n + [r_spec] * n + [s_spec] * 3, out_specs=[s_spec] * 4,
                  out_shape=[SDS((R, C), F32)] * 4, prefetch=True, name=name)(place, *parts, *recvs, w, m, v)


_SMALL = (("ln_mix", "ln_mix_w", D_MODEL), ("ln_cross", "ln_cross_w", D_MODEL), ("ln_mem", "ln_mem_w", D_MODEL),
          ("ln_ffn", "ln_ffn_w", D_MODEL), ("ln_final", "ln_final_w", D_MODEL), ("hg_lb", "hg_lower_bounds", 2 * HG_WIDTH),
          ("hg_norm", "hg_norm_w", HEAD_DIM))
_PACK = sum(n for _, _, n in _SMALL) + LANE


def _small_sync(grow, wrow, mrow, vrow):
    def body(g_ref, w_ref, m_ref, v_ref, o_ref, gath, ssem, rsem):
        x, y, c, _ = _my_place()
        me = 4 * x + 2 * y + c
        gath[pl.ds(me, 1), :] = g_ref[...]
        cps = []
        for dlt in range(1, N_DEV):
            peer = (me + dlt) % N_DEV
            cp = pltpu.make_async_remote_copy(
                src_ref=g_ref, dst_ref=gath.at[pl.ds(me, 1), :], send_sem=ssem.at[peer], recv_sem=rsem.at[me],
                device_id=(peer // 4, (peer // 2) % 2, peer % 2), device_id_type=MESH)
            cp.start()
            cps.append(cp)
        for dlt in range(1, N_DEV):
            src = (me + dlt) % N_DEV
            pltpu.make_async_remote_copy(
                src_ref=g_ref, dst_ref=gath.at[pl.ds(src, 1), :], send_sem=ssem.at[src], recv_sem=rsem.at[src],
                device_id=(x, y, c), device_id_type=MESH).wait_recv()
        for cp in cps:
            cp.wait_send()
        g = gath[0:1, :]
        for j in range(1, N_DEV):
            g = g + gath[j:j + 1, :]
        d, mn, vn = _adamw(w_ref[...], g, m_ref[...], v_ref[...])
        o_ref[0:1, :] = g
        o_ref[1:2, :] = d
        o_ref[2:3, :] = mn
        o_ref[3:4, :] = vn

    vm = pl.BlockSpec(memory_space=pltpu.VMEM)
    return _pcall(
        body, in_specs=[vm] * 4, out_specs=vm, out_shape=SDS((4, _PACK), F32),
        scratch_shapes=[pltpu.VMEM((N_DEV, _PACK), F32), pltpu.SemaphoreType.DMA((N_DEV,)), pltpu.SemaphoreType.DMA((N_DEV,))],
        name="small_sync")(grow, wrow, mrow, vrow)


def _pack_small(d, loss_row=None):
    parts = [d[k].reshape(1, n).astype(F32) for k, _, n in _SMALL]
    parts.append(loss_row if loss_row is not None else jnp.ones((1, LANE), F32))
    return jnp.concatenate(parts, axis=1)


def kernel(x, mem, ln_mix_w, w_in, hg_norm_w, hg_lower_bounds, w_branch_a, w_branch_b, w_out, ln_cross_w, ln_mem_w, wq_cross, wkv_cross, wo_cross, ln_ffn_w, w1, w3, w2, ln_final_w, loss_target, m_ln_mix_w, m_w_in, m_hg_norm_w, m_hg_lower_bounds, m_w_branch_a, m_w_branch_b, m_w_out, m_ln_cross_w, m_ln_mem_w, m_wq_cross, m_wkv_cross, m_wo_cross, m_ln_ffn_w, m_w1, m_w3, m_w2, m_ln_final_w, v_ln_mix_w, v_w_in, v_hg_norm_w, v_hg_lower_bounds, v_w_branch_a, v_w_branch_b, v_w_out, v_ln_cross_w, v_ln_mem_w, v_wq_cross, v_wkv_cross, v_wo_cross, v_ln_ffn_w, v_w1, v_w3, v_w2, v_ln_final_w):
    given = dict(locals())
    place = jnp.stack([lax.axis_index("c"), 2 * lax.axis_index("x") + lax.axis_index("y")]).astype(jnp.int32)

    _ORDER[0] = None
    index = _INDEX

    def local_view(a, i):
        return a[0].T if _MATS[i][6] else a[0]

    shard = {key: local_view(given[ref], i) for i, (key, ref, *_) in enumerate(_MATS)}

    p = {}
    col_part = {"w_in_a": (0, IN_SPLIT), "w_in_b": (IN_SPLIT, D_MODEL)}
    waves = (("in_a", ["w_in_a"], 1), ("in_b", ["w_in_b"], 12), ("mid", ["w_ba", "w_bb", "w_out", "wq", "wkv", "wo"], 2),
             ("w1", ["w1"], 11), ("w3", ["w3"], 15), ("w2", ["w2"], 16))
    for wave, keys, cid in waves:
        units = [_cast_unit(shard[k], "cast_" + k, col_part.get(k)) for k in keys]
        p.update(zip(keys, _all_gather(units, [index[k] for k in keys], cid, "all_gather_" + wave)))
    small_w = {"ln_mix": ln_mix_w, "ln_cross": ln_cross_w, "ln_mem": ln_mem_w, "ln_ffn": ln_ffn_w,
               "ln_final": ln_final_w.reshape(1, D_MODEL), "hg_lb": hg_lower_bounds, "hg_norm": hg_norm_w}
    p.update(small_w)

    groups = {"ffn": ["w2", "w1", "w3"], "cross": ["wo", "wq", "wkv"], "mix": ["w_out", "w_ba", "w_bb"]}
    cids = {"ffn": (3, 4), "cross": (5, 6), "mix": (7, 8)}
    for q in range(IN_PARTS):
        groups[f"in_q{q}"] = [f"w_in_q{q}"]
        cids[f"in_q{q}"] = ((9, 10), (13, 14), (17, 18), (19, 20))[q]
    sib, parts, recv, out = {}, {}, {}, {}

    grads = {}

    def chip_stage(name):
        keys = groups[name]
        parts[name] = [_chip_sum(grads[k], s, index[k], place, "chip_sum_" + k) for k, s in zip(keys, sib[name])]
        recv[name] = list(_rs_chips(parts[name], [index[k] for k in keys], cids[name][1], "rs_chips_" + name))

    def adam(ref, i, part_list, recv_list):
        res = _adam_mat(part_list, recv_list, shard[_MATS[i][0]], local_view(given["m_" + ref], i), local_view(given["v_" + ref], i),
                        place, "adam_" + ref)
        out[ref] = [(a.T if _MATS[i][6] else a)[None] for a in res]

    def adam_stage(name, only=None, skip=None):
        for n, (k, t, r) in enumerate(zip(groups[name], parts[name], recv[name])):
            if (only is None or n == only) and n != skip:
                adam(_MATS[index[k]][1], index[k], [t], [r])

    def ready(name, gw):
        if name == "mixers_done":
            adam_stage("ffn", only=0)
            chip_stage("mix")
            adam_stage("ffn", skip=0)
            adam_stage("cross")
            return
        if name == "dh_half":
            adam_stage("mix")
            return
        keys = groups[name]
        grads.update({k: gw[k] for k in keys})
        if name.startswith("in_q"):
            parts[name] = [_sibling_sum_tc(gw[k], index[k], cids[name][0], "sibling_sum_" + k) for k in keys]
            recv[name] = list(_rs_chips(parts[name], [index[k] for k in keys], cids[name][1], "rs_chips_" + name))
            if name == "in_q1":
                _ALSO_AFTER.extend(recv["in_q0"])
            return
        sib[name] = _rs_sibling([gw[k] for k in keys], [index[k] for k in keys], cids[name][0], "rs_sibling_" + name)
        if name == "cross":
            chip_stage("ffn")
        if name == "mix":
            chip_stage("cross")

    loss_row, grad_x, gs = _local_step(x[0], mem[0], loss_target[0], p, ready)

    sm = {k: given["m_" + ref] for k, ref, _ in _SMALL}
    sv = {k: given["v_" + ref] for k, ref, _ in _SMALL}
    synced = _small_sync(_pack_small(gs, loss_row), _pack_small(small_w), _pack_small(sm), _pack_small(sv))
    in_groups = [f"in_q{q}" for q in range(IN_PARTS)]
    adam("w_in", index["w_in_a"], sum((parts[g] for g in in_groups), []), sum((recv[g] for g in in_groups), []))
    off = 0
    for k, ref, n in _SMALL:
        shape = given[ref].shape
        out[ref] = [synced[r, off:off + n].reshape(shape) for r in range(4)]
        off += n
    loss = synced[0, off]

    order = ["ln_mix_w", "w_in", "hg_norm_w", "hg_lower_bounds", "w_branch_a", "w_branch_b", "w_out", "ln_cross_w", "ln_mem_w",
             "wq_cross", "wkv_cross", "wo_cross", "ln_ffn_w", "w1", "w3", "w2", "ln_final_w"]
    return (loss, grad_x[None], *[out[n][0] for n in order], *[out[n][1] for n in order],
            *[out[n][2] for n in order], *[out[n][3] for n in order])
```

```python
import jax
import jax.numpy as jnp
from jax import lax
from jax.experimental import pallas as pl
from jax.experimental.pallas import tpu as pltpu
from jax.experimental.pallas import tpu_sc as plsc

F32 = jnp.float32
BF = jnp.bfloat16
SDS = jax.ShapeDtypeStruct

D_MODEL = 2048
SEQ = 2048
HEAD_DIM = 128
MEM_LEN = 256
ATT_GROUPS = ((128, 1), (512, 4), (2048, 16))
ATT_HEADS = 4
ATT_WIDTH = 1536
ATT_OUT = 512
HG_HEADS = 8
HG_WIDTH = 1024
HG_CHUNK = 64
IN_WIDTH = 12800
CROSS_HEADS = 4
CROSS_WIDTH = 512
D_FF = 5632
RMS_EPS = 1e-6
ADAM_LR = 0.001
ADAM_B1 = 0.9
ADAM_B2 = 0.999
ADAM_EPS = 1e-08
ADAM_WD = 0.01
ADAM_STEP = 10
N_DEV = 8

LANE = 128
QA0, KA0, VA0 = 0, 12, 24
QH0, FH0, IH0, GH0 = 36, 44, 52, 60
GA0, GB0 = 68, 84

VMEM_BIG = 56 * 1024 * 1024
MESH = pl.DeviceIdType.MESH


def _cp(vmem=None, **kw):
    if vmem is not None:
        kw["vmem_limit_bytes"] = vmem
    return pltpu.CompilerParams(**kw)


def _pick(n, cands):
    for c in cands:
        if n % c == 0:
            return c
    raise ValueError(f"no tile for {n}")


_ANY = pl.BlockSpec(memory_space=pl.ANY)
_ORDER = [None]
_ALSO_AFTER = []


def _pcall(body, *, name, in_specs, out_specs, out_shape, grid=(), scratch_shapes=(), compiler_params=None, prefetch=False,
           aliases=None):
    def run(*args):
        dep = _ORDER[0]
        specs, operands, kernel_fn = list(in_specs), list(args), body
        deps = [] if dep is None or any(dep is a for a in operands) else [dep]
        deps += _ALSO_AFTER
        del _ALSO_AFTER[:]
        if deps:
            at, nd = len(operands), len(deps)
            specs += [_ANY] * nd
            operands += deps

            def kernel_fn(*refs):
                return body(*refs[:at], *refs[at + nd:])
        if prefetch:
            call = pl.pallas_call(
                kernel_fn, out_shape=out_shape, name=name, compiler_params=compiler_params,
                grid_spec=pltpu.PrefetchScalarGridSpec(num_scalar_prefetch=1, grid=grid, in_specs=specs, out_specs=out_specs,
                                                       scratch_shapes=scratch_shapes))
        else:
            call = pl.pallas_call(kernel_fn, grid=grid, in_specs=specs, out_specs=out_specs, out_shape=out_shape,
                                  scratch_shapes=scratch_shapes, compiler_params=compiler_params, name=name,
                                  input_output_aliases=aliases or {})
        out = call(*operands)
        _ORDER[0] = out[0] if isinstance(out, (list, tuple)) else out
        return out
    return run


def _mm(a, b, form, out_dtype, name, res=None, tiles=None, a_part=(0, 1), b_part=(0, 1)):
    a_shape = (a.shape[0], a.shape[1] // a_part[1])
    b_shape = (b.shape[0], b.shape[1] // b_part[1])
    if form == "nn":
        (M, K), (K2, N) = a_shape, b_shape
    elif form == "nt":
        (M, K), (N, K2) = a_shape, b_shape
    else:
        (K, M), (K2, N) = a_shape, b_shape
    assert K == K2, (a.shape, b.shape, form)
    if tiles is None:
        tm = _pick(M, (1024, 1280, 1408, 512, 256))
        tn = _pick(N, (1280, 1024, 1408, 768, 512, 256))
        tk = K if K <= 2048 else _pick(K, (D_FF, 2560))
        if tk == D_FF:
            tm = _pick(M, (512, 256))
    else:
        tm, tn, tk = tiles
    nk = K // tk
    ah, bh = a_part[0], b_part[0]
    if form == "nn":
        a_spec = pl.BlockSpec((tm, tk), lambda i, j, k: (i, k + ah * nk))
        b_spec = pl.BlockSpec((tk, tn), lambda i, j, k: (k, j + bh * (N // tn)))
        dims = ((1,), (0,))
    elif form == "nt":
        a_spec = pl.BlockSpec((tm, tk), lambda i, j, k: (i, k + ah * nk))
        b_spec = pl.BlockSpec((tn, tk), lambda i, j, k: (j, k + bh * nk))
        dims = ((1,), (1,))
    else:
        a_spec = pl.BlockSpec((tk, tm), lambda i, j, k: (k, i + ah * (M // tm)))
        b_spec = pl.BlockSpec((tk, tn), lambda i, j, k: (k, j + bh * (N // tn)))
        dims = ((0,), (0,))
    o_spec = pl.BlockSpec((tm, tn), lambda i, j, k: (i, j))
    in_specs = [a_spec, b_spec]
    args = [a, b]
    if res is not None:
        in_specs.append(o_spec)
        args.append(res)

    def body(*refs):
        a_ref, b_ref = refs[0], refs[1]
        r_ref = refs[2] if res is not None else None
        o_ref = refs[3] if res is not None else refs[2]
        acc = refs[-1] if nk > 1 else None
        k = pl.program_id(2)
        d = lax.dot_general(a_ref[...].astype(BF), b_ref[...].astype(BF), (dims, ((), ())), preferred_element_type=F32)

        def finish(r):
            if res is not None:
                r = r + r_ref[...].astype(F32)
            o_ref[...] = r.astype(o_ref.dtype)

        if nk == 1:
            finish(d)
        else:
            @pl.when(k == 0)
            def _():
                acc[...] = d

            @pl.when((k > 0) & (k < nk - 1))
            def _():
                acc[...] += d

            @pl.when(k == nk - 1)
            def _():
                finish(acc[...] + d)

    return _pcall(
        body, grid=(M // tm, N // tn, nk), in_specs=in_specs, out_specs=o_spec,
        out_shape=SDS((M, N), out_dtype), scratch_shapes=[pltpu.VMEM((tm, tn), F32)] if nk > 1 else [],
        compiler_params=_cp(VMEM_BIG, dimension_semantics=("parallel", "parallel", "arbitrary")),
        name=name)(*args)


_MXU_COLS = 256


def _mm_fused(a, b, form, name, extras, out_dtypes, epi, tn=None, scatter=None):
    M, K = a.shape
    N = b.shape[1] if form == "nn" else b.shape[0]
    assert K <= 2048 and K == (b.shape[0] if form == "nn" else b.shape[1])
    tm = _pick(M, (1024, 512, 256))
    tn = tn or _pick(N, (1280, 1024, 1408, 512, 256))
    ne, no = len(extras), len(out_dtypes)
    bases = scatter[1] if scatter else []
    dims = ((1,), (0,)) if form == "nn" else ((1,), (1,))

    def body(*refs):
        a_ref, b_ref, ex, outs = refs[0], refs[1], refs[2:2 + ne], refs[2 + ne:2 + ne + no]
        av = a_ref[...].astype(BF)
        if scatter:
            wide_ref, stage, sem = refs[2 + ne + no:]
        for c0 in range(0, tn, _MXU_COLS):
            cols = slice(c0, min(c0 + _MXU_COLS, tn))
            bc = b_ref[:, cols] if form == "nn" else b_ref[cols, :]
            d = lax.dot_general(av, bc.astype(BF), (dims, ((), ())), preferred_element_type=F32)
            res = epi(d, *[e[:, cols] for e in ex])
            for o_ref, r in zip(outs, res[:no]):
                o_ref[:, cols] = r.astype(o_ref.dtype)
            for s, r in enumerate(res[no:]):
                stage[s, :, cols] = r.astype(BF)
        if scatter:
            rows = pl.ds(pl.multiple_of(pl.program_id(0) * tm, tm), tm)
            cps = [pltpu.make_async_copy(stage.at[s], wide_ref.at[rows, pl.ds(pl.multiple_of(base + pl.program_id(1) * tn, LANE), tn)],
                                         sem.at[s]) for s, base in enumerate(bases)]
            for cp in cps:
                cp.start()
            for cp in cps:
                cp.wait()

    a_spec = pl.BlockSpec((tm, K), lambda i, j: (i, 0))
    b_spec = pl.BlockSpec((K, tn), lambda i, j: (0, j)) if form == "nn" else pl.BlockSpec((tn, K), lambda i, j: (j, 0))
    o_spec = pl.BlockSpec((tm, tn), lambda i, j: (i, j))
    ex_specs, ex_args = [], []
    for e in extras:
        arr, off = e if isinstance(e, tuple) else (e, 0)
        ex_specs.append(pl.BlockSpec((tm, tn), lambda i, j, off=off: (i, off + j)))
        ex_args.append(arr)
    out_specs, out_shape, scratch = [o_spec] * no, [SDS((M, N), dt) for dt in out_dtypes], []
    if scatter:
        out_specs, out_shape = out_specs + [_ANY], out_shape + [scatter[0]]
        scratch = [pltpu.VMEM((len(bases), tm, tn), BF), pltpu.SemaphoreType.DMA((len(bases),))]
    return _pcall(
        body, grid=(M // tm, N // tn), in_specs=[a_spec, b_spec] + ex_specs, out_specs=out_specs, out_shape=out_shape,
        scratch_shapes=scratch, compiler_params=_cp(VMEM_BIG, dimension_semantics=("parallel", "parallel")),
        name=name)(a, b, *ex_args)


def _rms(x, w):
    return x * lax.rsqrt(jnp.mean(x * x, axis=-1, keepdims=True) + RMS_EPS) * w


def _rms_fwd(x, w, name, split=None):
    R, D = x.shape
    tr = 256
    widths = [D] if split is None else [D, split, D - split]

    def body(x_ref, w_ref, o_ref, *part_refs):
        y = _rms(x_ref[...], w_ref[...]).astype(o_ref.dtype)
        o_ref[...] = y
        if split is not None:
            part_refs[0][...] = y[:, :split]
            part_refs[1][...] = y[:, split:]

    outs = _pcall(
        body, grid=(R // tr,),
        in_specs=[pl.BlockSpec((tr, D), lambda i: (i, 0)), pl.BlockSpec((1, D), lambda i: (0, 0))],
        out_specs=[pl.BlockSpec((tr, n), lambda i: (i, 0)) for n in widths], out_shape=[SDS((R, n), BF) for n in widths],
        name=name)(x, w)
    return outs[0] if split is None else outs


def _rms_bwd(x, w, dh, dres, name, dx_dtype=BF):
    R, D = x.shape
    tr = 256
    want_dx = dres is not None
    dhs = dh if isinstance(dh, (tuple, list)) else (dh,)
    nd = len(dhs)

    def body(*refs):
        x_ref, w_ref, dh_refs = refs[0], refs[1], refs[2:2 + nd]
        if want_dx:
            dr_ref, dx_ref, dw_ref = refs[2 + nd:]
        else:
            dw_ref = refs[2 + nd]

        @pl.when(pl.program_id(0) == 0)
        def _():
            dw_ref[...] = jnp.zeros_like(dw_ref)

        dhv = jnp.concatenate([r[...].astype(F32) for r in dh_refs], axis=1) if nd > 1 else dh_refs[0][...].astype(F32)
        _, vjp = jax.vjp(_rms, x_ref[...], w_ref[...])
        dx, dw = vjp(dhv)
        dw_ref[...] += dw
        if want_dx:
            dx_ref[...] = (dx + dr_ref[...].astype(F32)).astype(dx_ref.dtype)

    row = pl.BlockSpec((tr, D), lambda i: (i, 0))
    parts = [pl.BlockSpec((tr, d.shape[1]), lambda i: (i, 0)) for d in dhs]
    vec = pl.BlockSpec((1, D), lambda i: (0, 0))
    if want_dx:
        return _pcall(body, grid=(R // tr,), in_specs=[row, vec] + parts + [row], out_specs=[row, vec],
                      out_shape=[SDS((R, D), dx_dtype), SDS((1, D), F32)], name=name)(x, w, *dhs, dres)
    return _pcall(body, grid=(R // tr,), in_specs=[row, vec] + parts, out_specs=vec,
                  out_shape=SDS((1, D), F32), name=name)(x, w, *dhs)


def _final_loss(x3, w, tgt):
    R, D = x3.shape
    tr = 256

    def loss_fn(xv, wv, tv):
        err = _rms(xv, wv) - tv
        return 0.5 * jnp.sum(jnp.mean(err * err, axis=-1))

    def body(x_ref, w_ref, t_ref, loss_ref, dx_ref, dw_ref):
        @pl.when(pl.program_id(0) == 0)
        def _():
            dw_ref[...] = jnp.zeros_like(dw_ref)
            loss_ref[...] = jnp.zeros_like(loss_ref)

        tv = t_ref[...]
        val, vjp = jax.vjp(lambda a, b: loss_fn(a, b, tv), x_ref[...], w_ref[...])
        dx, dw = vjp(jnp.ones((), F32))
        dx_ref[...] = dx.astype(BF)
        dw_ref[...] += dw
        loss_ref[...] += jnp.full(loss_ref.shape, val, F32)

    row = pl.BlockSpec((tr, D), lambda i: (i, 0))
    vec = pl.BlockSpec((1, D), lambda i: (0, 0))
    one = pl.BlockSpec((1, LANE), lambda i: (0, 0))
    return _pcall(body, grid=(R // tr,), in_specs=[row, vec, row], out_specs=[one, row, vec],
                  out_shape=[SDS((1, LANE), F32), SDS((R, D), BF), SDS((1, D), F32)], name="final_loss")(x3, w, tgt)


NEG = -1e30


def _att_blocks(S, d):
    L = S // d
    nb = L // LANE
    return nb, (2 if nb > 1 else 1)


def _att_rows(S, d, idx):
    nb, nkb = _att_blocks(S, d)
    r = idx // nb
    n = idx % nb
    kb = jnp.maximum(n - 1, 0)
    if d == 1:
        qrows = pl.ds(pl.multiple_of(n * LANE, LANE), LANE)
        krows = pl.ds(pl.multiple_of(kb * LANE, LANE), LANE * nkb)
    else:
        qrows = pl.ds(r + n * (LANE * d), LANE, stride=d)
        krows = pl.ds(r + kb * (LANE * d), LANE * nkb, stride=d)
    qpos = n * LANE + lax.broadcasted_iota(jnp.int32, (LANE, LANE * nkb), 0)
    kpos = kb * LANE + lax.broadcasted_iota(jnp.int32, (LANE, LANE * nkb), 1)
    return qrows, krows, qpos - kpos


def _slab(col):
    return pl.BlockSpec((SEQ, LANE), lambda h, col=col: (0, col + h))


def _widen(slabs, wide):
    for t, ref in enumerate(slabs):
        wide[t] = ref[...].astype(F32)
    return tuple(tuple(wide.at[3 * kind + g] for g in range(3)) for kind in range(3))


def _attn_fwd(proj):
    S = proj.shape[0]
    scale = HEAD_DIM ** -0.5

    def body(q0, q1, q2, k0, k1, k2, v0, v1, v2, o_ref, lse_ref, og, lg, wide):
        qs, ks, vs = _widen((q0, q1, q2, k0, k1, k2, v0, v1, v2), wide)
        for g, (window, d) in enumerate(ATT_GROUPS):
            def blk(idx, carry, g=g, window=window, d=d):
                qrows, krows, dist = _att_rows(S, d, idx)
                q = qs[g][qrows, :].astype(BF)
                k = ks[g][krows, :].astype(BF)
                v = vs[g][krows, :].astype(BF)
                s = lax.dot_general(q, k, (((1,), (1,)), ((), ())), preferred_element_type=F32) * scale
                s = jnp.where((dist >= 0) & (dist <= window // d), s, NEG)
                m = jnp.max(s, axis=-1, keepdims=True)
                p = jnp.exp(s - m)
                l = jnp.sum(p, axis=-1, keepdims=True)
                o = jnp.dot((p / l).astype(BF), v, preferred_element_type=F32)
                og[g, qrows, :] = o
                lg[g, qrows, :] = jnp.broadcast_to(m + jnp.log(l), (LANE, LANE))
                return carry
            lax.fori_loop(0, S // LANE, blk, 0, unroll=True)

        def merge(t, carry):
            rows = pl.ds(pl.multiple_of(t * 256, 256), 256)
            l0, l1, l2 = lg[0, rows, :], lg[1, rows, :], lg[2, rows, :]
            m = jnp.maximum(jnp.maximum(l0, l1), l2)
            w0, w1, w2 = jnp.exp(l0 - m), jnp.exp(l1 - m), jnp.exp(l2 - m)
            den = w0 + w1 + w2
            o_ref[rows, :] = (w0 * og[0, rows, :] + w1 * og[1, rows, :] + w2 * og[2, rows, :]) / den
            lse_ref[rows, :] = m + jnp.log(den)
            return carry
        lax.fori_loop(0, S // 256, merge, 0)

    ins = [_slab(QA0 + 4 * g) for g in range(3)] + [_slab(KA0 + 4 * g) for g in range(3)] + [_slab(VA0 + 4 * g) for g in range(3)]
    out = pl.BlockSpec((SEQ, LANE), lambda h: (0, h))
    return _pcall(
        body, grid=(ATT_HEADS,), in_specs=ins, out_specs=[out, out],
        out_shape=[SDS((S, ATT_OUT), F32), SDS((S, ATT_OUT), F32)],
        scratch_shapes=[pltpu.VMEM((3, S, LANE), F32), pltpu.VMEM((3, S, LANE), F32), pltpu.VMEM((9, S, LANE), F32)],
        compiler_params=_cp(VMEM_BIG), name="attn_fwd")(*([proj] * 9))


def _attn_bwd(proj, o_att, lse, do_att, dproj):
    S = proj.shape[0]
    scale = HEAD_DIM ** -0.5

    def body(q0, q1, q2, k0, k1, k2, v0, v1, v2, o_ref, lse_ref, do_ref, dp_in, dp_ref, dqa, dka, dva, delta, stage, sem, wide):
        qs, ks, vs = _widen((q0, q1, q2, k0, k1, k2, v0, v1, v2), wide)
        head = pl.program_id(0)

        def slab_copies(g):
            return [pltpu.make_async_copy(stage.at[t], dp_ref.at[:, pl.ds(pl.multiple_of((base + 4 * g + head) * LANE, LANE), LANE)],
                                          sem.at[t]) for t, base in enumerate((QA0, KA0, VA0))]

        delta[...] = jnp.broadcast_to(jnp.sum(do_ref[...] * o_ref[...], axis=-1, keepdims=True), delta.shape)
        for g, (window, d) in enumerate(ATT_GROUPS):
            dka[...] = jnp.zeros_like(dka)
            dva[...] = jnp.zeros_like(dva)

            def blk(idx, carry, g=g, window=window, d=d):
                qrows, krows, dist = _att_rows(S, d, idx)
                q = qs[g][qrows, :].astype(BF)
                k = ks[g][krows, :].astype(BF)
                v = vs[g][krows, :].astype(BF)
                do = do_ref[qrows, :]
                s = lax.dot_general(q, k, (((1,), (1,)), ((), ())), preferred_element_type=F32) * scale
                p = jnp.exp(s - lse_ref[qrows, :][:, 0:1])
                p = jnp.where((dist >= 0) & (dist <= window // d), p, 0.0)
                dob = do.astype(BF)
                dva[krows, :] += lax.dot_general(p.astype(BF), dob, (((0,), (0,)), ((), ())), preferred_element_type=F32)
                dp = lax.dot_general(dob, v, (((1,), (1,)), ((), ())), preferred_element_type=F32)
                ds = (p * (dp - delta[qrows, :][:, 0:1]) * scale).astype(BF)
                dqa[qrows, :] = jnp.dot(ds, k, preferred_element_type=F32)
                dka[krows, :] += lax.dot_general(ds, q, (((0,), (0,)), ((), ())), preferred_element_type=F32)
                return carry
            lax.fori_loop(0, S // LANE, blk, 0, unroll=True)
            if g > 0:
                for cp in slab_copies(g - 1):
                    cp.wait()
            stage[0] = dqa[...].astype(BF)
            stage[1] = dka[...].astype(BF)
            stage[2] = dva[...].astype(BF)
            for cp in slab_copies(g):
                cp.start()
        for cp in slab_copies(len(ATT_GROUPS) - 1):
            cp.wait()

    cols = [QA0 + 4 * g for g in range(3)] + [KA0 + 4 * g for g in range(3)] + [VA0 + 4 * g for g in range(3)]
    ins = [_slab(c) for c in cols]
    one = pl.BlockSpec((SEQ, LANE), lambda h: (0, h))
    return _pcall(
        body, grid=(ATT_HEADS,), in_specs=ins + [one, one, one, _ANY], out_specs=_ANY,
        out_shape=SDS((S, IN_WIDTH), BF), aliases={12: 0},
        scratch_shapes=[pltpu.VMEM((S, LANE), F32)] * 4 + [pltpu.VMEM((3, S, LANE), BF), pltpu.SemaphoreType.DMA((3,)),
                                                           pltpu.VMEM((9, S, LANE), F32)],
        compiler_params=_cp(VMEM_BIG), name="attn_bwd")(*([proj] * 9), o_att, lse, do_att, dproj)


def _bdot(a, b, dims):
    return lax.dot_general(a.astype(BF), b.astype(BF), (dims, ((), ())), preferred_element_type=F32)


def _to_heads(x):
    return jnp.concatenate([x[None, :, h * LANE:(h + 1) * LANE] for h in range(HG_HEADS)], axis=0)


def _from_heads(x):
    return jnp.concatenate([x[h] for h in range(HG_HEADS)], axis=1)


def _hdot(a, b, ca, cb):
    return lax.dot_general(a.astype(BF), b.astype(BF), (((ca,), (cb,)), ((0,), (0,))), preferred_element_type=F32)


def _hg_chunk_all(qh, fh, ih, gh, lbraw, nw, st):
    C = HG_CHUNK
    mx = jnp.maximum(lbraw[0:1], lbraw[1:2])
    e0 = jnp.exp(lbraw[0:1] - mx)
    e1 = jnp.exp(lbraw[1:2] - mx)
    lb = e0 / (e0 + e1)
    f = lb + (1.0 - lb) * jax.nn.sigmoid(fh)
    logf = jnp.log(f)
    k = 1.0 - f
    q = qh * jax.nn.sigmoid(qh)
    causal = lax.broadcasted_iota(jnp.int32, (C, C), 0) >= lax.broadcasted_iota(jnp.int32, (C, C), 1)
    b = lax.dot_general(causal.astype(F32), logf, (((1,), (0,)), ((), ())), precision=lax.Precision.HIGHEST,
                        preferred_element_type=F32)
    b_last = b[C - 1:C]
    b_ref = b[C // 2:C // 2 + 1]
    v3 = _to_heads(ih)
    st_out = st * _to_heads(jnp.exp(b_last)) + _hdot(v3, _to_heads(k * jnp.exp(b_last - b)), 1, 1)
    inter = _hdot(_to_heads(q * jnp.exp(b)), st, 2, 2)
    a = _hdot(_to_heads(q * jnp.exp(b - b_ref)), _to_heads(k * jnp.exp(b_ref - b)), 2, 2)
    a = jnp.where(causal[None], a, 0.0)
    o = inter + _hdot(a, v3, 2, 1)
    o = o * lax.rsqrt(jnp.mean(o * o, axis=-1, keepdims=True) + RMS_EPS) * nw
    return _from_heads(o) * (gh * jax.nn.sigmoid(gh)), st_out


HG_STEP = 8
HG_ROWS = HG_STEP * HG_CHUNK


def _hg_in_specs(step_of):
    half = 4 * LANE
    specs = []
    for base in (QH0, FH0, IH0, GH0):
        for part in range(2):
            specs.append(pl.BlockSpec((HG_ROWS, half), lambda n, col=base // 4 + part: (step_of(n), col)))
    return specs


def _hg_sections(sec, c):
    rows = slice(c * HG_CHUNK, (c + 1) * HG_CHUNK)
    return [jnp.concatenate([sec[2 * s][rows, :], sec[2 * s + 1][rows, :]], axis=1).astype(F32) for s in range(4)]


def _hg_fwd(proj, lb, nw):
    S = proj.shape[0]
    C = HG_CHUNK
    NC = S // C

    def body(*refs):
        sec = refs[:8]
        lb_ref, nw_ref, o_ref, st_ref, st_scr = refs[8:]

        @pl.when(pl.program_id(0) == 0)
        def _():
            st_scr[...] = jnp.zeros_like(st_scr)

        st = st_scr[...]
        for c in range(HG_STEP):
            st_ref[:, c] = st
            o, st = _hg_chunk_all(*_hg_sections(sec, c), lb_ref[...], nw_ref[...], st)
            o_ref[c * C:(c + 1) * C, :] = o.astype(o_ref.dtype)
        st_scr[...] = st

    return _pcall(
        body, grid=(NC // HG_STEP,),
        in_specs=_hg_in_specs(lambda n: n) + [pl.BlockSpec((2, HG_WIDTH), lambda n: (0, 0)), pl.BlockSpec((1, LANE), lambda n: (0, 0))],
        out_specs=[pl.BlockSpec((HG_ROWS, HG_WIDTH), lambda n: (n, 0)),
                   pl.BlockSpec((HG_HEADS, HG_STEP, LANE, LANE), lambda n: (0, n, 0, 0))],
        out_shape=[SDS((S, HG_WIDTH), BF), SDS((HG_HEADS, NC, LANE, LANE), F32)],
        scratch_shapes=[pltpu.VMEM((HG_HEADS, LANE, LANE), F32)], name="hg_fwd")(*([proj] * 8), lb, nw)


def _hg_bwd(proj, lb, nw, states, do_hg, dproj):
    S = proj.shape[0]
    C = HG_CHUNK
    NC = S // C

    def body(*refs):
        sec = refs[:8]
        lb_ref, nw_ref, st_ref, do_ref, dp_in, dp_ref, dlb_ref, dnw_ref, dst_scr, stage, sem = refs[8:]
        n = pl.program_id(0)
        slot = n % 2

        def step_copies(s):
            rows = pl.ds(pl.multiple_of((NS - 1 - n) * HG_ROWS, HG_ROWS), HG_ROWS)
            return [pltpu.make_async_copy(stage.at[s, t], dp_ref.at[rows, pl.ds((QH0 + 8 * t) * LANE, HG_WIDTH)], sem.at[s, t])
                    for t in range(4)]

        @pl.when(n == 0)
        def _():
            dst_scr[...] = jnp.zeros_like(dst_scr)
            dlb_ref[...] = jnp.zeros_like(dlb_ref)
            dnw_ref[...] = jnp.zeros_like(dnw_ref)

        @pl.when(n >= 2)
        def _():
            for cp in step_copies(slot):
                cp.wait()

        dst = dst_scr[...]
        for c in reversed(range(HG_STEP)):
            rows = slice(c * C, (c + 1) * C)
            _, vjp = jax.vjp(_hg_chunk_all, *_hg_sections(sec, c), lb_ref[...], nw_ref[...], st_ref[:, c])
            dq, df, di, dg, dlb, dnw, dst = vjp((do_ref[rows, :].astype(F32), dst))
            dlb_ref[...] += dlb
            dnw_ref[...] += dnw
            for t, val in enumerate((dq, df, di, dg)):
                stage[slot, t, rows, :] = val.astype(BF)
        dst_scr[...] = dst
        for cp in step_copies(slot):
            cp.start()

        @pl.when(n == NS - 1)
        def _():
            for cp in step_copies(1 - slot) + step_copies(slot):
                cp.wait()

    NS = NC // HG_STEP
    rev = lambda n: NS - 1 - n
    ospec = pl.BlockSpec((HG_ROWS, HG_WIDTH), lambda n: (rev(n), 0))
    lbs = pl.BlockSpec((2, HG_WIDTH), lambda n: (0, 0))
    nws = pl.BlockSpec((1, LANE), lambda n: (0, 0))
    outs = _pcall(
        body, grid=(NS,),
        in_specs=_hg_in_specs(rev) + [lbs, nws, pl.BlockSpec((HG_HEADS, HG_STEP, LANE, LANE), lambda n: (0, rev(n), 0, 0)), ospec, _ANY],
        out_specs=[_ANY, lbs, nws],
        out_shape=[SDS((S, IN_WIDTH), BF), SDS((2, HG_WIDTH), F32), SDS((1, LANE), F32)], aliases={12: 0},
        scratch_shapes=[pltpu.VMEM((HG_HEADS, LANE, LANE), F32), pltpu.VMEM((2, 4, HG_ROWS, HG_WIDTH), BF),
                        pltpu.SemaphoreType.DMA((2, 4))],
        name="hg_bwd")(*([proj] * 8), lb, nw, states, do_hg, dproj)
    return outs


def _merge(ga, gb, ba, bb):
    return jax.nn.sigmoid(ga) * ba + jax.nn.sigmoid(gb) * bb


_MT = 512


def _swiglu(a, b):
    return a * jax.nn.sigmoid(a) * b


def _cross(q, k, v):
    s = _bdot(q, k, ((1,), (1,))) * (HEAD_DIM ** -0.5)
    e = jnp.exp(s - jnp.max(s, axis=-1, keepdims=True))
    p = e / jnp.sum(e, axis=-1, keepdims=True)
    return _bdot(p, v, ((1,), (0,)))


def _cross_specs():
    q = pl.BlockSpec((_MT, LANE), lambda h, i: (i, h))
    k = pl.BlockSpec((MEM_LEN, LANE), lambda h, i: (0, h))
    v = pl.BlockSpec((MEM_LEN, LANE), lambda h, i: (0, CROSS_HEADS + h))
    return q, k, v


def _cross_fwd(qc, kvc):
    S = qc.shape[0]
    q, k, v = _cross_specs()

    def body(q_ref, k_ref, v_ref, o_ref):
        o_ref[...] = _cross(q_ref[...], k_ref[...], v_ref[...]).astype(o_ref.dtype)

    return _pcall(body, grid=(CROSS_HEADS, S // _MT), in_specs=[q, k, v], out_specs=q,
                          out_shape=SDS((S, CROSS_WIDTH), BF), name="cross_fwd")(qc, kvc, kvc)


def _cross_bwd(qc, kvc, doc):
    S = qc.shape[0]
    q, k, v = _cross_specs()

    def body(q_ref, k_ref, v_ref, do_ref, dq_ref, dk_ref, dv_ref):
        @pl.when(pl.program_id(1) == 0)
        def _():
            dk_ref[...] = jnp.zeros_like(dk_ref)
            dv_ref[...] = jnp.zeros_like(dv_ref)

        _, vjp = jax.vjp(_cross, q_ref[...], k_ref[...], v_ref[...])
        dq, dk, dv = vjp(do_ref[...])
        dq_ref[...] = dq.astype(dq_ref.dtype)
        dk_ref[...] += dk
        dv_ref[...] += dv

    return _pcall(body, grid=(CROSS_HEADS, S // _MT), in_specs=[q, k, v, q], out_specs=[q, k, k],
                          out_shape=[SDS((S, CROSS_WIDTH), BF), SDS((MEM_LEN, CROSS_WIDTH), F32), SDS((MEM_LEN, CROSS_WIDTH), F32)],
                          name="cross_bwd")(qc, kvc, kvc, doc)


def _local_step(x, mem, tgt, p, ready):
    h, h_a, h_b = _rms_fwd(x, p["ln_mix"], "rms_mix", split=IN_SPLIT)
    proj = _mm(h_a, p["w_in_a"], "nt", BF, "mm_proj_a")
    proj = _mm(h_b, p["w_in_b"], "nt", BF, "mm_proj_b", res=proj)
    o_att, lse = _attn_fwd(proj)
    o_hg, states = _hg_fwd(proj, p["hg_lb"], p["hg_norm"])
    ba = _mm(o_att, p["w_ba"], "nn", BF, "mm_ba")
    gates = [(proj, GA0 * LANE // _MT), (proj, GB0 * LANE // _MT)]

    def f32s(*vals):
        return [v.astype(F32) for v in vals]

    bb, merged = _mm_fused(o_hg, p["w_bb"], "nn", "mm_bb_merge", gates + [ba], [BF, BF], tn=_MT,
                           epi=lambda d, ga, gb, bav: (d, _merge(*f32s(ga, gb, bav), d)))
    x1 = _mm(merged, p["w_out"], "nn", F32, "mm_out", res=x)
    hc = _rms_fwd(x1, p["ln_cross"], "rms_cross")
    qc = _mm(hc, p["wq"], "nn", F32, "mm_q")
    mn = _rms_fwd(mem, p["ln_mem"], "rms_mem")
    kvc = _mm(mn, p["wkv"], "nn", F32, "mm_kv")
    oc = _cross_fwd(qc, kvc)
    x2 = _mm(oc, p["wo"], "nn", F32, "mm_o", res=x1)
    hf = _rms_fwd(x2, p["ln_ffn"], "rms_ffn")
    a = _mm(hf, p["w1"], "nt", BF, "mm_w1")
    b, g = _mm_fused(hf, p["w3"], "nt", "mm_w3_swiglu", [a], [BF, BF], lambda d, av: (d, _swiglu(av.astype(F32), d)))
    x3 = _mm(g, p["w2"], "nn", F32, "mm_w2", res=x2)
    loss, dx3b, d_ln_final = _final_loss(x3, p["ln_final"], tgt)

    gw, gs = {}, {"ln_final": d_ln_final}
    da, db = _mm_fused(dx3b, p["w2"], "nt", "mm_dg_swiglu", [a, b], [BF, BF],
                       lambda d, av, bv: jax.vjp(_swiglu, av.astype(F32), bv.astype(F32))[1](d))
    gw["w2"] = _mm(g, dx3b, "tn", BF, "mm_dw2")
    dhf = _mm(da, p["w1"], "nn", F32, "mm_dhf1")
    dhf = _mm(db, p["w3"], "nn", BF, "mm_dhf3", res=dhf)
    gw["w1"] = _mm(da, hf, "tn", BF, "mm_dw1")
    gw["w3"] = _mm(db, hf, "tn", BF, "mm_dw3")
    ready("ffn", gw)
    dx2b, gs["ln_ffn"] = _rms_bwd(x2, p["ln_ffn"], dhf, dx3b, "rms_ffn_bwd")
    doc = _mm(dx2b, p["wo"], "nt", F32, "mm_doc")
    gw["wo"] = _mm(oc, dx2b, "tn", BF, "mm_dwo")
    dqc, dk, dv = _cross_bwd(qc, kvc, doc)
    dkvc = jnp.concatenate([dk, dv], axis=1)
    dhc = _mm(dqc, p["wq"], "nt", BF, "mm_dhc")
    gw["wq"] = _mm(hc, dqc, "tn", BF, "mm_dwq")
    dmn = _mm(dkvc, p["wkv"], "nt", F32, "mm_dmn")
    gw["wkv"] = _mm(mn, dkvc, "tn", BF, "mm_dwkv")
    ready("cross", gw)
    gs["ln_mem"] = _rms_bwd(mem, p["ln_mem"], dmn, None, "rms_mem_bwd")
    dx1b, gs["ln_cross"] = _rms_bwd(x1, p["ln_cross"], dhc, dx2b, "rms_cross_bwd")
    def merge_back(d, ga, gb, bav, bbv):
        dga, dgb, dba_, dbb_ = jax.vjp(_merge, *f32s(ga, gb, bav, bbv))[1](d)
        return dba_, dbb_, dga, dgb

    dba, dbb, dproj = _mm_fused(dx1b, p["w_out"], "nt", "mm_dmerged_merge", gates + [ba, bb], [BF, BF], merge_back, tn=_MT,
                                scatter=(SDS((x.shape[0], IN_WIDTH), BF), [GA0 * LANE, GB0 * LANE]))
    gw["w_out"] = _mm(merged, dx1b, "tn", BF, "mm_dwout")
    do_att = _mm(dba, p["w_ba"], "nt", F32, "mm_doatt")
    gw["w_ba"] = _mm(o_att, dba, "tn", BF, "mm_dwba")
    do_hg = _mm(dbb, p["w_bb"], "nt", F32, "mm_dohg")
    gw["w_bb"] = _mm(o_hg, dbb, "tn", BF, "mm_dwbb")
    ready("mix", gw)
    dproj = _attn_bwd(proj, o_att, lse, do_att, dproj)
    dproj, gs["hg_lb"], gs["hg_norm"] = _hg_bwd(proj, p["hg_lb"], p["hg_norm"], states, do_hg, dproj)
    for q in range(IN_PARTS):
        gw[f"w_in_q{q}"] = _mm(dproj, h, "tn", BF, f"mm_dwin_q{q}", b_part=(q, IN_PARTS))
        ready(f"in_q{q}", gw)
        if q == 0:
            ready("mixers_done", gw)
    dh_a = _mm(dproj, p["w_in_a"], "nn", BF, "mm_dh_a")
    ready("dh_half", gw)
    dh = (dh_a, _mm(dproj, p["w_in_b"], "nn", BF, "mm_dh_b"))
    grad_x, gs["ln_mix"] = _rms_bwd(x, p["ln_mix"], dh, dx1b, "rms_mix_bwd", dx_dtype=F32)
    return loss, grad_x, gs


IN_SPLIT = 1280
_MATS = (
    ("w_in_a", "w_in", IN_WIDTH, IN_SPLIT, "row", IN_WIDTH // N_DEV, True),
    ("w_in_b", "w_in", IN_WIDTH, D_MODEL - IN_SPLIT, "row", IN_WIDTH // N_DEV, True),
    ("w_ba", "w_branch_a", ATT_OUT, D_MODEL, "col", D_MODEL // N_DEV, False),
    ("w_bb", "w_branch_b", HG_WIDTH, D_MODEL, "col", D_MODEL // N_DEV, False),
    ("w_out", "w_out", D_MODEL, D_MODEL, "row", D_MODEL // N_DEV, False),
    ("wq", "wq_cross", D_MODEL, CROSS_WIDTH, "row", D_MODEL // N_DEV, False),
    ("wkv", "wkv_cross", D_MODEL, 2 * CROSS_WIDTH, "row", D_MODEL // N_DEV, False),
    ("wo", "wo_cross", CROSS_WIDTH, D_MODEL, "col", D_MODEL // N_DEV, False),
    ("w1", "w1", D_FF, D_MODEL, "row", D_FF // N_DEV, True),
    ("w3", "w3", D_FF, D_MODEL, "row", D_FF // N_DEV, True),
    ("w2", "w2", D_FF, D_MODEL, "row", D_FF // N_DEV, False),
    ("w_in_q0", "w_in", IN_WIDTH, D_MODEL // 2, "row", IN_WIDTH // N_DEV, True),
    ("w_in_q1", "w_in", IN_WIDTH, D_MODEL // 2, "row", IN_WIDTH // N_DEV, True),
)
IN_PARTS = 2
_INDEX = {m[0]: i for i, m in enumerate(_MATS)}


def _unit_shape(i):
    _, _, K, N, kind, sh, _ = _MATS[i]
    return (sh, N) if kind == "row" else (K, sh)


def _win(ref, i, chip, par, half=None):
    _, _, K, _, kind, sh, _ = _MATS[i]
    if kind == "row":
        rows, start = (sh, 0) if half is None else (sh // 2, half * (sh // 2))
        return ref.at[pl.ds(pl.multiple_of((2 * chip + par) * sh + start, 16), rows), :]
    rows, start = (K, 0) if half is None else (K // 2, half * (K // 2))
    return ref.at[pl.ds(start, rows), pl.ds(pl.multiple_of((2 * chip + par) * sh, LANE), sh)]


def _my_place():
    x, y, c = lax.axis_index("x"), lax.axis_index("y"), lax.axis_index("c")
    chips = [(1 - x, y), (x, 1 - y), (1 - x, 1 - y)]
    return x, y, c, chips


def _cast_unit(w, name, cols=None):
    K, N = w.shape
    c0, c1 = cols or (0, N)
    tr = _pick(K, (256, 400, 352))

    def body(w_ref, o_ref):
        o_ref[...] = w_ref[:, c0:c1].astype(BF)

    src = pl.BlockSpec((tr, N), lambda i: (i, 0))
    dst = pl.BlockSpec((tr, c1 - c0), lambda i: (i, 0))
    return _pcall(body, grid=(K // tr,), in_specs=[src], out_specs=dst, out_shape=SDS((K, c1 - c0), BF), name=name)(w)


def _handshake(peers):
    barrier = pltpu.get_barrier_semaphore()
    for peer in peers:
        pl.semaphore_signal(barrier, inc=1, device_id=peer, device_id_type=MESH)
    pl.semaphore_wait(barrier, len(peers))


def _sequencer(body, out_type, sems, cid, name):
    return pl.kernel(body, out_type=out_type, mesh=plsc.ScalarSubcoreMesh(axis_name="seq", num_cores=1),
                     scratch_types=sems, compiler_params=pltpu.CompilerParams(collective_id=cid), name=name)


def _all_gather(units, idx, cid, name):
    n = len(idx)

    def body(*refs):
        us, fulls = refs[:n], refs[n:2 * n]
        ssem, rsem, lsem = refs[2 * n:]
        x, y, c, _ = _my_place()
        xn, yn, dg = (1 - x, y), (x, 1 - y), (1 - x, 1 - y)
        me, sib = (x, y, c), (x, y, 1 - c)
        _handshake([sib, (*xn, c), (*yn, c)])

        def win(i, chip_xy, par, half=None):
            return _win(fulls[i], idx[i], 2 * chip_xy[0] + chip_xy[1], par, half)

        def rcopy(i, k, src, dst, to):
            return pltpu.make_async_remote_copy(src_ref=src, dst_ref=dst, send_sem=ssem.at[i, k], recv_sem=rsem.at[i, k],
                                                device_id=to, device_id_type=MESH)

        def landed(i, k, blk):
            rcopy(i, k, blk, blk, me).wait_recv()

        sends, locs = [], []

        def send(i, k, blk, to, src=None):
            cp = rcopy(i, k, blk if src is None else src, blk, to)
            cp.start()
            sends.append(cp)

        for i in range(n):
            mine = win(i, (x, y), c)
            loc = pltpu.make_async_copy(us[i], mine, lsem.at[i])
            loc.start()
            locs.append(loc)
            send(i, 0, mine, sib, src=us[i])
            send(i, 1, mine, (*xn, c), src=us[i])
            send(i, 2, mine, (*yn, c), src=us[i])
        for i in range(n):
            landed(i, 1, win(i, xn, c))
            send(i, 3, win(i, xn, c, 0), (*yn, c))
            send(i, 5, win(i, xn, c), sib)
            landed(i, 2, win(i, yn, c))
            send(i, 4, win(i, yn, c, 1), (*xn, c))
            send(i, 6, win(i, yn, c), sib)
        for i in range(n):
            landed(i, 3, win(i, dg, c, 0))
            landed(i, 4, win(i, dg, c, 1))
            send(i, 7, win(i, dg, c), sib)
        for i in range(n):
            landed(i, 0, win(i, (x, y), 1 - c))
            landed(i, 5, win(i, xn, 1 - c))
            landed(i, 6, win(i, yn, 1 - c))
            landed(i, 7, win(i, dg, 1 - c))
        for cp in sends:
            cp.wait_send()
        for loc in locs:
            loc.wait()

    out_type = [SDS((_MATS[i][2], _MATS[i][3]), BF) for i in idx]
    sems = [pltpu.SemaphoreType.DMA((n, 8)), pltpu.SemaphoreType.DMA((n, 8)), pltpu.SemaphoreType.DMA((n,))]
    return _sequencer(body, out_type, sems, cid, name)(*units)


def _rs_sibling(grads, idx, cid, name):
    n = len(idx)

    def body(*refs):
        gs, bufs = refs[:n], refs[n:2 * n]
        ssem, rsem = refs[2 * n:]
        x, y, c, _ = _my_place()
        _handshake([(x, y, 1 - c)])
        cps = []
        for i in range(n):
            for k in range(4):
                cp = pltpu.make_async_remote_copy(
                    src_ref=_win(gs[i], idx[i], k, 1 - c), dst_ref=bufs[i].at[k], send_sem=ssem.at[i, k], recv_sem=rsem.at[i, k],
                    device_id=(x, y, 1 - c), device_id_type=MESH)
                cp.start()
                cps.append(cp)
        for cp in cps:
            cp.wait()

    out_type = [SDS((4, *_unit_shape(i)), BF) for i in idx]
    sems = [pltpu.SemaphoreType.DMA((n, 4)), pltpu.SemaphoreType.DMA((n, 4))]
    return _sequencer(body, out_type, sems, cid, name)(*grads)


def _sibling_sum_tc(grad, i, cid, name):
    unit = _unit_shape(i)

    def body(g_ref, t_ref, theirs, mine, summed, ssem, rsem, lsem, osem):
        x, y, c, _ = _my_place()
        _handshake([(x, y, 1 - c)])
        sends, loads = [], []
        for k in range(4):
            sends.append(pltpu.make_async_remote_copy(
                src_ref=_win(g_ref, i, k, 1 - c), dst_ref=theirs.at[k], send_sem=ssem.at[k], recv_sem=rsem.at[k],
                device_id=(x, y, 1 - c), device_id_type=MESH))
            loads.append(pltpu.make_async_copy(_win(g_ref, i, k, c), mine.at[k], lsem.at[k]))
            sends[-1].start()
            loads[-1].start()
        stores = []
        for k in range(4):
            loads[k].wait()
            sends[k].wait_recv()
            summed[k] = (mine[k].astype(F32) + theirs[k].astype(F32)).astype(BF)
            stores.append(pltpu.make_async_copy(summed.at[k], t_ref.at[k], osem.at[k]))
            stores[-1].start()
        for k in range(4):
            sends[k].wait_send()
            stores[k].wait()

    block = pltpu.VMEM((4, *unit), BF)
    return _pcall(body, in_specs=[_ANY], out_specs=_ANY, out_shape=SDS((4, *unit), BF),
                  scratch_shapes=[block, block, block] + [pltpu.SemaphoreType.DMA((4,))] * 4,
                  compiler_params=_cp(VMEM_BIG, collective_id=cid), name=name)(grad)


def _chip_sum(g, sib, i, place, name):
    _, _, K, N, kind, sh, _ = _MATS[i]

    def body(pref, g_ref, s_ref, o_ref):
        o_ref[...] = (g_ref[...].astype(F32) + s_ref[...].astype(F32)).astype(BF)

    if kind == "row":
        tr = _pick(sh, (800, 352, 256))
        per = sh // tr
        grid = (4, per)
        g_spec = pl.BlockSpec((tr, N), lambda k, r, pref: ((2 * k + pref[0]) * per + r, 0))
        u_spec = pl.BlockSpec((None, tr, N), lambda k, r, pref: (k, r, 0))
    else:
        grid = (4,)
        g_spec = pl.BlockSpec((K, sh), lambda k, pref: (0, 2 * k + pref[0]))
        u_spec = pl.BlockSpec((None, K, sh), lambda k, pref: (k, 0, 0))
    return _pcall(body, grid=grid, in_specs=[g_spec, u_spec], out_specs=u_spec, out_shape=SDS((4, *_unit_shape(i)), BF),
                  prefetch=True, name=name)(place, g, sib)


def _rs_chips(parts, idx, cid, name):
    n = len(idx)

    def body(*refs):
        ts, bufs, relays = refs[:n], refs[n:2 * n], refs[2 * n:3 * n]
        ssem, rsem = refs[3 * n:]
        x, y, c, _ = _my_place()
        xn, yn, dg = (1 - x, y), (x, 1 - y), (1 - x, 1 - y)
        to_x, to_y, me = (*xn, c), (*yn, c), (x, y, c)
        _handshake([to_x, to_y])

        def rcopy(i, k, src, dst, to):
            return pltpu.make_async_remote_copy(src_ref=src, dst_ref=dst, send_sem=ssem.at[i, k], recv_sem=rsem.at[i, k],
                                                device_id=to, device_id_type=MESH)

        def halves(i):
            rows = _unit_shape(idx[i])[0] // 2
            return [pl.ds(0, rows), pl.ds(rows, rows)]

        sends = []

        def send(cp):
            cp.start()
            sends.append(cp)

        for i in range(n):
            h0, h1 = halves(i)
            diag = ts[i].at[2 * dg[0] + dg[1]]
            send(rcopy(i, 0, ts[i].at[2 * xn[0] + xn[1]], bufs[i].at[0], to_x))
            send(rcopy(i, 1, ts[i].at[2 * yn[0] + yn[1]], bufs[i].at[1], to_y))
            send(rcopy(i, 2, diag.at[h0], relays[i].at[0], to_x))
            send(rcopy(i, 3, diag.at[h1], relays[i].at[1], to_y))
        for i in range(n):
            h0, h1 = halves(i)
            rcopy(i, 2, relays[i].at[0], relays[i].at[0], me).wait_recv()
            send(rcopy(i, 4, relays[i].at[0], bufs[i].at[2, h0], to_y))
            rcopy(i, 3, relays[i].at[1], relays[i].at[1], me).wait_recv()
            send(rcopy(i, 5, relays[i].at[1], bufs[i].at[2, h1], to_x))
        for i in range(n):
            h0, h1 = halves(i)
            rcopy(i, 0, bufs[i].at[0], bufs[i].at[0], me).wait_recv()
            rcopy(i, 1, bufs[i].at[1], bufs[i].at[1], me).wait_recv()
            rcopy(i, 4, bufs[i].at[2, h0], bufs[i].at[2, h0], me).wait_recv()
            rcopy(i, 5, bufs[i].at[2, h1], bufs[i].at[2, h1], me).wait_recv()
        for cp in sends:
            cp.wait_send()

    def relay_shape(i):
        rows, cols = _unit_shape(i)
        return (2, rows // 2, cols)

    out_type = [SDS((3, *_unit_shape(i)), BF) for i in idx] + [SDS(relay_shape(i), BF) for i in idx]
    sems = [pltpu.SemaphoreType.DMA((n, 6)), pltpu.SemaphoreType.DMA((n, 6))]
    return _sequencer(body, out_type, sems, cid, name)(*parts)[:n]


def _adamw(w, g, m, v):
    m = ADAM_B1 * m + (1.0 - ADAM_B1) * g
    v = ADAM_B2 * v + (1.0 - ADAM_B2) * jnp.square(g)
    m_hat = m / (1.0 - ADAM_B1 ** ADAM_STEP)
    v_hat = v / (1.0 - ADAM_B2 ** ADAM_STEP)
    delta = -ADAM_LR * (m_hat / (jnp.sqrt(v_hat) + ADAM_EPS) + ADAM_WD * w)
    return delta, m, v


def _adam_mat(parts, recvs, w, m, v, place, name):
    R, C = w.shape
    n = len(parts)
    tr = _pick(R, (256, 160, 176))

    def body(pref, *refs):
        t_refs, r_refs = refs[:n], refs[n:2 * n]
        w_ref, m_ref, v_ref, g_out, d_out, m_out, v_out = refs[2 * n:]
        cols = [t[...].astype(F32) + r[0].astype(F32) + r[1].astype(F32) + r[2].astype(F32) for t, r in zip(t_refs, r_refs)]
        g = jnp.concatenate(cols, axis=1) if n > 1 else cols[0]
        d, mn, vn = _adamw(w_ref[...], g, m_ref[...], v_ref[...])
        g_out[...] = g
        d_out[...] = d
        m_out[...] = mn
        v_out[...] = vn

    t_spec = pl.BlockSpec((None, tr, C // n), lambda i, pref: (pref[1], i, 0))
    r_spec = pl.BlockSpec((3, tr, C // n), lambda i, pref: (0, i, 0))
    s_spec = pl.BlockSpec((tr, C), lambda i, pref: (i, 0))
    return _pcall(body, grid=(R // tr,), in_specs=[t_spec] * n + [r_spec] * n + [s_spec] * 3, out_specs=[s_spec] * 4,
                  out_shape=[SDS((R, C), F32)] * 4, prefetch=True, name=name)(place, *parts, *recvs, w, m, v)


_SMALL = (("ln_mix", "ln_mix_w", D_MODEL), ("ln_cross", "ln_cross_w", D_MODEL), ("ln_mem", "ln_mem_w", D_MODEL),
          ("ln_ffn", "ln_ffn_w", D_MODEL), ("ln_final", "ln_final_w", D_MODEL), ("hg_lb", "hg_lower_bounds", 2 * HG_WIDTH),
          ("hg_norm", "hg_norm_w", HEAD_DIM))
_PACK = sum(n for _, _, n in _SMALL) + LANE


def _small_sync(grow, wrow, mrow, vrow):
    def body(g_ref, w_ref, m_ref, v_ref, o_ref, gath, ssem, rsem):
        x, y, c, _ = _my_place()
        me = 4 * x + 2 * y + c
        gath[pl.ds(me, 1), :] = g_ref[...]
        cps = []
        for dlt in range(1, N_DEV):
            peer = (me + dlt) % N_DEV
            cp = pltpu.make_async_remote_copy(
                src_ref=g_ref, dst_ref=gath.at[pl.ds(me, 1), :], send_sem=ssem.at[peer], recv_sem=rsem.at[me],
                device_id=(peer // 4, (peer // 2) % 2, peer % 2), device_id_type=MESH)
            cp.start()
            cps.append(cp)
        for dlt in range(1, N_DEV):
            src = (me + dlt) % N_DEV
            pltpu.make_async_remote_copy(
                src_ref=g_ref, dst_ref=gath.at[pl.ds(src, 1), :], send_sem=ssem.at[src], recv_sem=rsem.at[src],
                device_id=(x, y, c), device_id_type=MESH).wait_recv()
        for cp in cps:
            cp.wait_send()
        g = gath[0:1, :]
        for j in range(1, N_DEV):
            g = g + gath[j:j + 1, :]
        d, mn, vn = _adamw(w_ref[...], g, m_ref[...], v_ref[...])
        o_ref[0:1, :] = g
        o_ref[1:2, :] = d
        o_ref[2:3, :] = mn
        o_ref[3:4, :] = vn

    vm = pl.BlockSpec(memory_space=pltpu.VMEM)
    return _pcall(
        body, in_specs=[vm] * 4, out_specs=vm, out_shape=SDS((4, _PACK), F32),
        scratch_shapes=[pltpu.VMEM((N_DEV, _PACK), F32), pltpu.SemaphoreType.DMA((N_DEV,)), pltpu.SemaphoreType.DMA((N_DEV,))],
        name="small_sync")(grow, wrow, mrow, vrow)


def _pack_small(d, loss_row=None):
    parts = [d[k].reshape(1, n).astype(F32) for k, _, n in _SMALL]
    parts.append(loss_row if loss_row is not None else jnp.ones((1, LANE), F32))
    return jnp.concatenate(parts, axis=1)


def kernel(x, mem, ln_mix_w, w_in, hg_norm_w, hg_lower_bounds, w_branch_a, w_branch_b, w_out, ln_cross_w, ln_mem_w, wq_cross, wkv_cross, wo_cross, ln_ffn_w, w1, w3, w2, ln_final_w, loss_target, m_ln_mix_w, m_w_in, m_hg_norm_w, m_hg_lower_bounds, m_w_branch_a, m_w_branch_b, m_w_out, m_ln_cross_w, m_ln_mem_w, m_wq_cross, m_wkv_cross, m_wo_cross, m_ln_ffn_w, m_w1, m_w3, m_w2, m_ln_final_w, v_ln_mix_w, v_w_in, v_hg_norm_w, v_hg_lower_bounds, v_w_branch_a, v_w_branch_b, v_w_out, v_ln_cross_w, v_ln_mem_w, v_wq_cross, v_wkv_cross, v_wo_cross, v_ln_ffn_w, v_w1, v_w3, v_w2, v_ln_final_w):
    given = dict(locals())
    place = jnp.stack([lax.axis_index("c"), 2 * lax.axis_index("x") + lax.axis_index("y")]).astype(jnp.int32)

    _ORDER[0] = None
    index = _INDEX

    def local_view(a, i):
        return a[0].T if _MATS[i][6] else a[0]

    shard = {key: local_view(given[ref], i) for i, (key, ref, *_) in enumerate(_MATS)}

    p = {}
    col_part = {"w_in_a": (0, IN_SPLIT), "w_in_b": (IN_SPLIT, D_MODEL)}
    waves = (("in_a", ["w_in_a"], 1), ("in_b", ["w_in_b"], 12), ("mid", ["w_ba", "w_bb", "w_out", "wq", "wkv", "wo"], 2),
             ("w1", ["w1"], 11), ("w3", ["w3"], 15), ("w2", ["w2"], 16))
    for wave, keys, cid in waves:
        units = [_cast_unit(shard[k], "cast_" + k, col_part.get(k)) for k in keys]
        p.update(zip(keys, _all_gather(units, [index[k] for k in keys], cid, "all_gather_" + wave)))
    small_w = {"ln_mix": ln_mix_w, "ln_cross": ln_cross_w, "ln_mem": ln_mem_w, "ln_ffn": ln_ffn_w,
               "ln_final": ln_final_w.reshape(1, D_MODEL), "hg_lb": hg_lower_bounds, "hg_norm": hg_norm_w}
    p.update(small_w)

    groups = {"ffn": ["w2", "w1", "w3"], "cross": ["wo", "wq", "wkv"], "mix": ["w_out", "w_ba", "w_bb"]}
    cids = {"ffn": (3, 4), "cross": (5, 6), "mix": (7, 8)}
    for q in range(IN_PARTS):
        groups[f"in_q{q}"] = [f"w_in_q{q}"]
        cids[f"in_q{q}"] = ((9, 10), (13, 14), (17, 18), (19, 20))[q]
    sib, parts, recv, out = {}, {}, {}, {}

    grads = {}

    def chip_stage(name):
        keys = groups[name]
        parts[name] = [_chip_sum(grads[k], s, index[k], place, "chip_sum_" + k) for k, s in zip(keys, sib[name])]
        recv[name] = list(_rs_chips(parts[name], [index[k] for k in keys], cids[name][1], "rs_chips_" + name))

    def adam(ref, i, part_list, recv_list):
        res = _adam_mat(part_list, recv_list, shard[_MATS[i][0]], local_view(given["m_" + ref], i), local_view(given["v_" + ref], i),
                        place, "adam_" + ref)
        out[ref] = [(a.T if _MATS[i][6] else a)[None] for a in res]

    def adam_stage(name, only=None, skip=None):
        for n, (k, t, r) in enumerate(zip(groups[name], parts[name], recv[name])):
            if (only is None or n == only) and n != skip:
                adam(_MATS[index[k]][1], index[k], [t], [r])

    def ready(name, gw):
        if name == "mixers_done":
            adam_stage("ffn", only=0)
            chip_stage("mix")
            adam_stage("ffn", skip=0)
            adam_stage("cross")
            return
        if name == "dh_half":
            adam_stage("mix")
            return
        keys = groups[name]
        grads.update({k: gw[k] for k in keys})
        if name.startswith("in_q"):
            parts[name] = [_sibling_sum_tc(gw[k], index[k], cids[name][0], "sibling_sum_" + k) for k in keys]
            recv[name] = list(_rs_chips(parts[name], [index[k] for k in keys], cids[name][1], "rs_chips_" + name))
            if name == "in_q1":
                _ALSO_AFTER.extend(recv["in_q0"])
            return
        sib[name] = _rs_sibling([gw[k] for k in keys], [index[k] for k in keys], cids[name][0], "rs_sibling_" + name)
        if name == "cross":
            chip_stage("ffn")
        if name == "mix":
            chip_stage("cross")

    loss_row, grad_x, gs = _local_step(x[0], mem[0], loss_target[0], p, ready)

    sm = {k: given["m_" + ref] for k, ref, _ in _SMALL}
    sv = {k: given["v_" + ref] for k, ref, _ in _SMALL}
    synced = _small_sync(_pack_small(gs, loss_row), _pack_small(small_w), _pack_small(sm), _pack_small(sv))
    in_groups = [f"in_q{q}" for q in range(IN_PARTS)]
    adam("w_in", index["w_in_a"], sum((parts[g] for g in in_groups), []), sum((recv[g] for g in in_groups), []))
    off = 0
    for k, ref, n in _SMALL:
        shape = given[ref].shape
        out[ref] = [synced[r, off:off + n].reshape(shape) for r in range(4)]
        off += n
    loss = synced[0, off]

    order = ["ln_mix_w", "w_in", "hg_norm_w", "hg_lower_bounds", "w_branch_a", "w_branch_b", "w_out", "ln_cross_w", "ln_mem_w",
             "wq_cross", "wkv_cross", "wo_cross", "ln_ffn_w", "w1", "w3", "w2", "ln_final_w"]
    return (loss, grad_x[None], *[out[n][0] for n in order], *[out[n][1] for n in order],
            *[out[n][2] for n in order], *[out[n][3] for n in order])
```
